```python
import jax, jax.numpy as jnp
from jax import lax
import numpy as np

D_MODEL = 2048
BATCH = 8
SEQ = 2048
DEPTH = 1

HEAD_DIM = 128
N_HEADS_TOTAL = D_MODEL // HEAD_DIM
N_FOX_HEADS = N_HEADS_TOTAL // 2
N_SWA_HEADS = N_HEADS_TOTAL - N_FOX_HEADS
N_SWA_KV_HEADS = max(1, N_SWA_HEADS // 4)
SWA_WINDOW = 128
Q_BLOCK = 128
D_FF = 4 * D_MODEL
ROPE_THETA = 10000.0
NORM_EPS = 1e-6
FOX_W = N_FOX_HEADS * HEAD_DIM
SWA_Q_W = N_SWA_HEADS * HEAD_DIM
SWA_KV_W = N_SWA_KV_HEADS * HEAD_DIM
MIX_W = FOX_W + SWA_Q_W
IN_SPLITS = [FOX_W, FOX_W, FOX_W, N_FOX_HEADS, SWA_Q_W, SWA_KV_W, SWA_KV_W]
IN_PROJ_W = sum(IN_SPLITS)
N_MOD = 6

kernel_name = "hymba_fox_swa_sink_hybrid"


def rmsnorm(x, g):
    xf = x.astype(jnp.float32)
    y = xf * lax.rsqrt(jnp.mean(xf * xf, axis=-1, keepdims=True) + NORM_EPS)
    return (y * g.astype(jnp.float32)).astype(x.dtype)


def rope(x, pos):
    d = x.shape[-1]
    half = d // 2
    inv_freq = 1.0 / (ROPE_THETA ** (jnp.arange(half, dtype=jnp.float32) * (2.0 / d)))
    ang = pos.astype(jnp.float32)[:, None] * inv_freq[None, :]
    cos = jnp.cos(ang)[None, :, None, :]
    sin = jnp.sin(ang)[None, :, None, :]
    xf = x.astype(jnp.float32)
    x1, x2 = xf[..., :half], xf[..., half:]
    out = jnp.concatenate([x1 * cos - x2 * sin, x2 * cos + x1 * sin], axis=-1)
    return out.astype(x.dtype)


def forgetting_attention(q, k, v, log_f):
    B, S, H, d = q.shape
    cum = jnp.cumsum(log_f, axis=1).transpose(0, 2, 1)
    scale = d ** -0.5
    tri = jnp.tril(jnp.ones((Q_BLOCK, Q_BLOCK), dtype=bool))
    outs = []
    for i in range(S // Q_BLOCK):
        q0 = i * Q_BLOCK
        end = q0 + Q_BLOCK
        s = jnp.einsum('bqhd,bkhd->bhqk', q[:, q0:end], k[:, :end],
                       preferred_element_type=jnp.float32) * scale
        s = s + cum[:, :, q0:end, None] - cum[:, :, None, :end]
        mask = jnp.concatenate([jnp.ones((Q_BLOCK, q0), dtype=bool), tri], axis=1)
        s = jnp.where(mask[None, None], s, -jnp.inf)
        p = jax.nn.softmax(s, axis=-1)
        outs.append(jnp.einsum('bhqk,bkhd->bqhd', p.astype(v.dtype), v[:, :end]))
    return jnp.concatenate(outs, axis=1)


def sliding_window_sink_attention(q, k, v, sinks):
    B, S, H, d = q.shape
    KVH = k.shape[2]
    G = H // KVH
    nb = S // Q_BLOCK
    scale = d ** -0.5
    pad = ((0, 0), (Q_BLOCK, 0), (0, 0), (0, 0))
    kp = jnp.pad(k, pad).reshape(B, nb + 1, Q_BLOCK, KVH, d)
    vp = jnp.pad(v, pad).reshape(B, nb + 1, Q_BLOCK, KVH, d)
    kb = jnp.concatenate([kp[:, :-1], kp[:, 1:]], axis=2)
    vb = jnp.concatenate([vp[:, :-1], vp[:, 1:]], axis=2)
    qb = q.reshape(B, nb, Q_BLOCK, KVH, G, d)
    s = jnp.einsum('bnqkgd,bnjkd->bnkgqj', qb, kb,
                   preferred_element_type=jnp.float32) * scale
    qi = jnp.arange(Q_BLOCK)[:, None]
    kj = jnp.arange(2 * Q_BLOCK)[None, :]
    diff = qi + Q_BLOCK - kj
    band = (diff >= 0) & (diff < SWA_WINDOW)
    key_idx = jnp.arange(nb)[:, None] * Q_BLOCK + jnp.arange(2 * Q_BLOCK)[None, :] - Q_BLOCK
    valid = key_idx >= 0
    mask = band[None, :, :] & valid[:, None, :]
    s = jnp.where(mask[None, :, None, None], s, -jnp.inf)
    sink = jnp.broadcast_to(sinks.astype(jnp.float32).reshape(KVH, G)[None, None, :, :, None, None],
                            s.shape[:-1] + (1,))
    p = jax.nn.softmax(jnp.concatenate([s, sink], axis=-1), axis=-1)[..., :-1]
    o = jnp.einsum('bnkgqj,bnjkd->bnqkgd', p.astype(v.dtype), vb)
    return o.reshape(B, S, H, d)


def _fwd_setup_inputs(seed: int = 0) -> dict:
    key = jax.random.key(seed)
    ks = jax.random.split(key, 16)
    f32 = jnp.float32
    D = D_MODEL
    def nrm(k, shape, s):
        return jax.random.normal(k, shape, f32) * s
    return {
        "x": nrm(ks[0], (BATCH, SEQ, D), 1.0),
        "c": nrm(ks[1], (BATCH, D), 1.0),
        "w_mod": nrm(ks[2], (DEPTH, D, N_MOD * D), D ** -0.5),
        "b_mod": nrm(ks[3], (DEPTH, N_MOD * D), 0.02),
        "g_pre_mix": 1.0 + nrm(ks[4], (DEPTH, D), 0.02),
        "g_post_mix": 1.0 + nrm(ks[5], (DEPTH, D), 0.02),
        "w_in": nrm(ks[6], (DEPTH, D, IN_PROJ_W), D ** -0.5),
        "b_forget": jax.random.uniform(ks[7], (DEPTH, N_FOX_HEADS), f32, 1.0, 5.0),
        "swa_sinks": nrm(ks[8], (DEPTH, N_SWA_HEADS), 0.5),
        "w_out": nrm(ks[9], (DEPTH, MIX_W, D), MIX_W ** -0.5),
        "g_pre_mlp": 1.0 + nrm(ks[10], (DEPTH, D), 0.02),
        "g_post_mlp": 1.0 + nrm(ks[11], (DEPTH, D), 0.02),
        "w_up": nrm(ks[12], (DEPTH, D, D_FF), D ** -0.5),
        "w_down": nrm(ks[13], (DEPTH, D_FF, D), D_FF ** -0.5),
    }


def _fwd_reference(x, c, w_mod, b_mod, g_pre_mix, g_post_mix, w_in, b_forget, swa_sinks,
              w_out, g_pre_mlp, g_post_mlp, w_up, w_down):
    B, S, D = x.shape
    pos = jnp.arange(S)
    split_idx = np.cumsum(IN_SPLITS)[:-1].tolist()
    cond = jax.nn.silu(c)
    for l in range(DEPTH):
        mod = cond @ w_mod[l] + b_mod[l]
        sh_a, sc_a, gt_a, sh_m, sc_m, gt_m = [m[:, None, :] for m in jnp.split(mod, N_MOD, axis=-1)]

        h = rmsnorm(x, g_pre_mix[l]) * (1.0 + sc_a) + sh_a
        proj = h @ w_in[l]
        fq, fk, fv, fg, sq, sk, sv = jnp.split(proj, split_idx, axis=-1)

        log_f = jax.nn.log_sigmoid(fg.astype(jnp.float32) + b_forget[l].astype(jnp.float32))
        fox = forgetting_attention(fq.reshape(B, S, N_FOX_HEADS, HEAD_DIM),
                                   fk.reshape(B, S, N_FOX_HEADS, HEAD_DIM),
                                   fv.reshape(B, S, N_FOX_HEADS, HEAD_DIM), log_f)

        sq = rope(sq.reshape(B, S, N_SWA_HEADS, HEAD_DIM), pos)
        sk = rope(sk.reshape(B, S, N_SWA_KV_HEADS, HEAD_DIM), pos)
        sv = sv.reshape(B, S, N_SWA_KV_HEADS, HEAD_DIM)
        swa = sliding_window_sink_attention(sq, sk, sv, swa_sinks[l])

        mix = jnp.concatenate([fox.reshape(B, S, FOX_W), swa.reshape(B, S, SWA_Q_W)], axis=-1) @ w_out[l]
        x = x + gt_a * rmsnorm(mix, g_post_mix[l])

        h = rmsnorm(x, g_pre_mlp[l]) * (1.0 + sc_m) + sh_m
        y = jnp.square(jax.nn.relu(h @ w_up[l])) @ w_down[l]
        x = x + gt_m * rmsnorm(y, g_post_mlp[l])
    return x


import jax as _jax
import jax.numpy as _jnp

TWIN_FORMAT = 'train_step'
FWD_PARAMS = ['x', 'c', 'w_mod', 'b_mod', 'g_pre_mix', 'g_post_mix', 'w_in', 'b_forget', 'swa_sinks', 'w_out', 'g_pre_mlp', 'g_post_mlp', 'w_up', 'w_down']
TWIN_WEIGHTS = ['w_mod', 'b_mod', 'g_pre_mix', 'g_post_mix', 'w_in', 'b_forget', 'swa_sinks', 'w_out', 'g_pre_mlp', 'g_post_mlp', 'w_up', 'w_down']
TWIN_DIFF_INPUT = 'x'
TWIN_INPUTS = ['x', 'c', 'w_mod', 'b_mod', 'g_pre_mix', 'g_post_mix', 'w_in', 'b_forget', 'swa_sinks', 'w_out', 'g_pre_mlp', 'g_post_mlp', 'w_up', 'w_down', 'loss_target', 'm_w_mod', 'm_b_mod', 'm_g_pre_mix', 'm_g_post_mix', 'm_w_in', 'm_b_forget', 'm_swa_sinks', 'm_w_out', 'm_g_pre_mlp', 'm_g_post_mlp', 'm_w_up', 'm_w_down', 'v_w_mod', 'v_b_mod', 'v_g_pre_mix', 'v_g_post_mix', 'v_w_in', 'v_b_forget', 'v_swa_sinks', 'v_w_out', 'v_g_pre_mlp', 'v_g_post_mlp', 'v_w_up', 'v_w_down']
TWIN_OUTPUTS = ['loss', 'grad_x', 'grad_w_mod', 'grad_b_mod', 'grad_g_pre_mix', 'grad_g_post_mix', 'grad_w_in', 'grad_b_forget', 'grad_swa_sinks', 'grad_w_out', 'grad_g_pre_mlp', 'grad_g_post_mlp', 'grad_w_up', 'grad_w_down', 'delta_w_mod', 'delta_b_mod', 'delta_g_pre_mix', 'delta_g_post_mix', 'delta_w_in', 'delta_b_forget', 'delta_swa_sinks', 'delta_w_out', 'delta_g_pre_mlp', 'delta_g_post_mlp', 'delta_w_up', 'delta_w_down', 'new_m_w_mod', 'new_m_b_mod', 'new_m_g_pre_mix', 'new_m_g_post_mix', 'new_m_w_in', 'new_m_b_forget', 'new_m_swa_sinks', 'new_m_w_out', 'new_m_g_pre_mlp', 'new_m_g_post_mlp', 'new_m_w_up', 'new_m_w_down', 'new_v_w_mod', 'new_v_b_mod', 'new_v_g_pre_mix', 'new_v_g_post_mix', 'new_v_w_in', 'new_v_b_forget', 'new_v_swa_sinks', 'new_v_w_out', 'new_v_g_pre_mlp', 'new_v_g_post_mlp', 'new_v_w_up', 'new_v_w_down']
TWIN_LEAF_KINDS = {'loss': 'loss', 'grad_x': 'grad_x', 'grad_w_mod': 'grad_w', 'grad_b_mod': 'grad_w', 'grad_g_pre_mix': 'grad_w', 'grad_g_post_mix': 'grad_w', 'grad_w_in': 'grad_w', 'grad_b_forget': 'grad_w', 'grad_swa_sinks': 'grad_w', 'grad_w_out': 'grad_w', 'grad_g_pre_mlp': 'grad_w', 'grad_g_post_mlp': 'grad_w', 'grad_w_up': 'grad_w', 'grad_w_down': 'grad_w', 'delta_w_mod': 'delta_w', 'delta_b_mod': 'delta_w', 'delta_g_pre_mix': 'delta_w', 'delta_g_post_mix': 'delta_w', 'delta_w_in': 'delta_w', 'delta_b_forget': 'delta_w', 'delta_swa_sinks': 'delta_w', 'delta_w_out': 'delta_w', 'delta_g_pre_mlp': 'delta_w', 'delta_g_post_mlp': 'delta_w', 'delta_w_up': 'delta_w', 'delta_w_down': 'delta_w', 'new_m_w_mod': 'new_m', 'new_m_b_mod': 'new_m', 'new_m_g_pre_mix': 'new_m', 'new_m_g_post_mix': 'new_m', 'new_m_w_in': 'new_m', 'new_m_b_forget': 'new_m', 'new_m_swa_sinks': 'new_m', 'new_m_w_out': 'new_m', 'new_m_g_pre_mlp': 'new_m', 'new_m_g_post_mlp': 'new_m', 'new_m_w_up': 'new_m', 'new_m_w_down': 'new_m', 'new_v_w_mod': 'new_v', 'new_v_b_mod': 'new_v', 'new_v_g_pre_mix': 'new_v', 'new_v_g_post_mix': 'new_v', 'new_v_w_in': 'new_v', 'new_v_b_forget': 'new_v', 'new_v_swa_sinks': 'new_v', 'new_v_w_out': 'new_v', 'new_v_g_pre_mlp': 'new_v', 'new_v_g_post_mlp': 'new_v', 'new_v_w_up': 'new_v', 'new_v_w_down': 'new_v'}


def _forward(args):
    return _fwd_reference(*[args[k] for k in FWD_PARAMS])


def _output_shape():
    out = _jax.eval_shape(lambda: _forward(_fwd_setup_inputs(0)))
    return out.shape, out.dtype

N_MICROBATCH = 1
ADAM_LR = 0.001
ADAM_B1 = 0.9
ADAM_B2 = 0.999
ADAM_EPS = 1e-08
ADAM_WD = 0.01
ADAM_STEP = 10
PER_EXAMPLE_BATCH_AXIS = {'x': 0, 'c': 0, 'loss_target': 0}
SHARED_INPUTS = []
_WEIGHT_DTYPES = {'w_mod': _jnp.float32, 'b_mod': _jnp.float32, 'g_pre_mix': _jnp.float32, 'g_post_mix': _jnp.float32, 'w_in': _jnp.float32, 'b_forget': _jnp.float32, 'swa_sinks': _jnp.float32, 'w_out': _jnp.float32, 'g_pre_mlp': _jnp.float32, 'g_post_mlp': _jnp.float32, 'w_up': _jnp.float32, 'w_down': _jnp.float32}
MOMENT_SCALE = {'w_mod': 1.352760e+00, 'b_mod': 2.389395e+00, 'g_pre_mix': 1.219358e-01, 'g_post_mix': 4.373227e+00, 'w_in': 1.040129e+00, 'b_forget': 3.083751e-01, 'swa_sinks': 2.625782e-02, 'w_out': 1.616581e+00, 'g_pre_mlp': 4.786316e-01, 'g_post_mlp': 3.958417e+00, 'w_up': 4.090679e-01, 'w_down': 1.004460e+00}


def _to_microbatches(a, axis):
    t = _jnp.moveaxis(a, axis, 0)
    t = t.reshape((N_MICROBATCH, t.shape[0] // N_MICROBATCH) + t.shape[1:])
    return _jnp.moveaxis(t, 1, axis + 1)


def setup_inputs(seed: int = 0) -> dict:
    inp = _fwd_setup_inputs(seed)
    key = _jax.random.fold_in(_jax.random.key(seed), 7919)
    shape, _ = _output_shape()
    out = dict(inp)
    out["loss_target"] = _jax.random.normal(_jax.random.fold_in(key, 0), shape, _jnp.float32)
    for i, name in enumerate(TWIN_WEIGHTS):
        w = inp[name].astype(_jnp.float32)
        if MOMENT_SCALE is None:
            s = _jnp.sqrt(_jnp.mean(_jnp.square(w)) + 1e-30)
        else:
            s = MOMENT_SCALE[name]
        km, kv = _jax.random.split(_jax.random.fold_in(key, i + 1))
        out[name] = w
        out["m_" + name] = s * _jax.random.normal(km, w.shape, _jnp.float32)
        out["v_" + name] = (s * s) * _jax.random.uniform(kv, w.shape, _jnp.float32, 0.5, 1.5)
    if N_MICROBATCH > 1:
        for name, axis in PER_EXAMPLE_BATCH_AXIS.items():
            out[name] = _to_microbatches(out[name], axis)
    return {'x': out['x'], 'c': out['c'], 'w_mod': out['w_mod'], 'b_mod': out['b_mod'], 'g_pre_mix': out['g_pre_mix'], 'g_post_mix': out['g_post_mix'], 'w_in': out['w_in'], 'b_forget': out['b_forget'], 'swa_sinks': out['swa_sinks'], 'w_out': out['w_out'], 'g_pre_mlp': out['g_pre_mlp'], 'g_post_mlp': out['g_post_mlp'], 'w_up': out['w_up'], 'w_down': out['w_down'], 'loss_target': out['loss_target'], 'm_w_mod': out['m_w_mod'], 'm_b_mod': out['m_b_mod'], 'm_g_pre_mix': out['m_g_pre_mix'], 'm_g_post_mix': out['m_g_post_mix'], 'm_w_in': out['m_w_in'], 'm_b_forget': out['m_b_forget'], 'm_swa_sinks': out['m_swa_sinks'], 'm_w_out': out['m_w_out'], 'm_g_pre_mlp': out['m_g_pre_mlp'], 'm_g_post_mlp': out['m_g_post_mlp'], 'm_w_up': out['m_w_up'], 'm_w_down': out['m_w_down'], 'v_w_mod': out['v_w_mod'], 'v_b_mod': out['v_b_mod'], 'v_g_pre_mix': out['v_g_pre_mix'], 'v_g_post_mix': out['v_g_post_mix'], 'v_w_in': out['v_w_in'], 'v_b_forget': out['v_b_forget'], 'v_swa_sinks': out['v_swa_sinks'], 'v_w_out': out['v_w_out'], 'v_g_pre_mlp': out['v_g_pre_mlp'], 'v_g_post_mlp': out['v_g_post_mlp'], 'v_w_up': out['v_w_up'], 'v_w_down': out['v_w_down']}


def _loss(weights, diff, rest, loss_target):
    with _jax.named_scope("forward"):
        args = {**rest, TWIN_DIFF_INPUT: diff, **{k: w.astype(_WEIGHT_DTYPES[k]) for k, w in weights.items()}}
        y = _forward(args)
    with _jax.named_scope("loss_head"):
        err = _jnp.square(y.astype(_jnp.float32) - loss_target)
        return 0.5 * _jnp.sum(_jnp.mean(err, axis=-1)) if err.ndim else 0.5 * err


def _adamw(w, g, m, v):
    m = ADAM_B1 * m + (1.0 - ADAM_B1) * g
    v = ADAM_B2 * v + (1.0 - ADAM_B2) * _jnp.square(g)
    m_hat = m / (1.0 - ADAM_B1 ** ADAM_STEP)
    v_hat = v / (1.0 - ADAM_B2 ** ADAM_STEP)
    delta = -ADAM_LR * (m_hat / (_jnp.sqrt(v_hat) + ADAM_EPS) + ADAM_WD * w)
    return delta, m, v


def reference(x, c, w_mod, b_mod, g_pre_mix, g_post_mix, w_in, b_forget, swa_sinks, w_out, g_pre_mlp, g_post_mlp, w_up, w_down, loss_target, m_w_mod, m_b_mod, m_g_pre_mix, m_g_post_mix, m_w_in, m_b_forget, m_swa_sinks, m_w_out, m_g_pre_mlp, m_g_post_mlp, m_w_up, m_w_down, v_w_mod, v_b_mod, v_g_pre_mix, v_g_post_mix, v_w_in, v_b_forget, v_swa_sinks, v_w_out, v_g_pre_mlp, v_g_post_mlp, v_w_up, v_w_down):
    given = dict(x=x, c=c, w_mod=w_mod, b_mod=b_mod, g_pre_mix=g_pre_mix, g_post_mix=g_post_mix, w_in=w_in, b_forget=b_forget, swa_sinks=swa_sinks, w_out=w_out, g_pre_mlp=g_pre_mlp, g_post_mlp=g_post_mlp, w_up=w_up, w_down=w_down, loss_target=loss_target, m_w_mod=m_w_mod, m_b_mod=m_b_mod, m_g_pre_mix=m_g_pre_mix, m_g_post_mix=m_g_post_mix, m_w_in=m_w_in, m_b_forget=m_b_forget, m_swa_sinks=m_swa_sinks, m_w_out=m_w_out, m_g_pre_mlp=m_g_pre_mlp, m_g_post_mlp=m_g_post_mlp, m_w_up=m_w_up, m_w_down=m_w_down, v_w_mod=v_w_mod, v_b_mod=v_b_mod, v_g_pre_mix=v_g_pre_mix, v_g_post_mix=v_g_post_mix, v_w_in=v_w_in, v_b_forget=v_b_forget, v_swa_sinks=v_swa_sinks, v_w_out=v_w_out, v_g_pre_mlp=v_g_pre_mlp, v_g_post_mlp=v_g_post_mlp, v_w_up=v_w_up, v_w_down=v_w_down)
    weights = {n: given[n] for n in TWIN_WEIGHTS}
    shared = {n: given[n] for n in SHARED_INPUTS}
    per_example = {n: given[n] for n in ['x', 'c']}
    grad_fn = _jax.value_and_grad(_loss, argnums=(0, 1))

    def one_microbatch(ex, loss_target):
        ex = dict(ex)
        diff = ex.pop(TWIN_DIFF_INPUT)
        return grad_fn(weights, diff, {**shared, **ex}, loss_target)

    if N_MICROBATCH == 1:
        loss, (grad_w, grad_x) = one_microbatch(per_example, given["loss_target"])
    else:
        def body(carry, xs):
            loss_sum, grad_sum = carry
            l_k, (gw_k, gx_k) = one_microbatch(xs[0], xs[1])
            with _jax.named_scope("update"):
                return (loss_sum + l_k, _jax.tree.map(_jnp.add, grad_sum, gw_k)), gx_k

        init = (_jnp.zeros((), _jnp.float32), _jax.tree.map(_jnp.zeros_like, weights))
        (loss, grad_w), grad_x = _jax.lax.scan(body, init, (per_example, given["loss_target"]))
    with _jax.named_scope("update"):
        delta_w, new_m, new_v = {}, {}, {}
        for n in TWIN_WEIGHTS:
            delta_w[n], new_m[n], new_v[n] = _adamw(weights[n], grad_w[n], given["m_" + n], given["v_" + n])
    return (loss, grad_x, *[grad_w[n] for n in TWIN_WEIGHTS], *[delta_w[n] for n in TWIN_WEIGHTS],
            *[new_m[n] for n in TWIN_WEIGHTS], *[new_v[n] for n in TWIN_WEIGHTS])
```

```python
import functools

import jax
import jax.numpy as jnp
from jax import lax
from jax.experimental import pallas as pl
from jax.experimental.pallas import tpu as pltpu

F32 = jnp.float32
BF16 = jnp.bfloat16
MESH = pl.DeviceIdType.MESH

N_DEV = 8
N_CHIP = 4
LANES = 128
HEAD_DIM = 128
N_FOX = 8
N_SWA = 8
N_KV = 2
GQA = N_SWA // N_KV
WINDOW = 128
FOX_W = N_FOX * HEAD_DIM
SWA_W = N_SWA * HEAD_DIM
KV_W = N_KV * HEAD_DIM
ROPE_THETA = 10000.0
NORM_EPS = 1e-6
ATT_SCALE = HEAD_DIM ** -0.5
FG_PAD = 512

ADAM_LR = 0.001
ADAM_B1 = 0.9
ADAM_B2 = 0.999
ADAM_EPS = 1e-08
ADAM_WD = 0.01
ADAM_STEP = 10

VMEM_LIMIT = 56 * 1024 * 1024

NT_DIMS = (((1,), (1,)), ((), ()))
TN_DIMS = (((0,), (0,)), ((), ()))
NN_DIMS = (((1,), (0,)), ((), ()))


def _pcall(body, *, name, out_shape, grid=(), in_specs=None, out_specs=None, scratch_shapes=(), grid_spec=None):
    params = pltpu.CompilerParams(vmem_limit_bytes=VMEM_LIMIT)
    if grid_spec is not None:
        return pl.pallas_call(body, name=name, out_shape=out_shape, grid_spec=grid_spec, compiler_params=params)
    return pl.pallas_call(body, name=name, out_shape=out_shape, grid=grid, in_specs=in_specs, out_specs=out_specs,
                          scratch_shapes=scratch_shapes, compiler_params=params)


def _blk(n, pref):
    if n <= pref:
        return n
    b = (pref // LANES) * LANES
    while n % b:
        b -= LANES
    return b


def _position():
    return lax.axis_index("x"), lax.axis_index("y"), lax.axis_index("c")


ANY = pl.BlockSpec(memory_space=pl.ANY)


def _all_gather(arrs, name):
    n = len(arrs)

    def body(*refs):
        ins, outs = refs[:n], refs[n:2 * n]
        send_sems, recv_sems, local_sems = refs[2 * n:]
        x, y, c = _position()
        me, sibling = (x, y, c), (x, y, 1 - c)
        chips = [(1 - x, y), (x, 1 - y), (1 - x, 1 - y)]

        def slot(p):
            return 4 * p[0] + 2 * p[1] + p[2]

        def copy(a, k, block, to, src=None):
            dst = outs[a].at[slot(block)]
            return pltpu.make_async_remote_copy(
                src_ref=dst if src is None else src, dst_ref=dst,
                send_sem=send_sems.at[7 * a + k], recv_sem=recv_sems.at[7 * a + k],
                device_id=to, device_id_type=MESH)

        mine = [pltpu.make_async_copy(ins[a], outs[a].at[slot(me)], local_sems.at[a]) for a in range(n)]
        for cp in mine:
            cp.start()
        first = []
        for a in range(n):
            first.append(copy(a, 0, me, sibling, src=ins[a]))
            first += [copy(a, 1 + j, me, (*chip, c), src=ins[a]) for j, chip in enumerate(chips)]
        for cp in first:
            cp.start()
        passed = []
        for a in range(n):
            for j, chip in enumerate(chips):
                copy(a, 1 + j, (*chip, c), me).wait_recv()
                cp = copy(a, 4 + j, (*chip, c), sibling)
                cp.start()
                passed.append(cp)
        for a in range(n):
            copy(a, 0, sibling, me).wait_recv()
            for j, chip in enumerate(chips):
                copy(a, 4 + j, (*chip, 1 - c), me).wait_recv()
        for cp in first + passed:
            cp.wait_send()
        for cp in mine:
            cp.wait()

    return _pcall(
        body, name=name,
        out_shape=[jax.ShapeDtypeStruct((N_DEV,) + a.shape, a.dtype) for a in arrs],
        in_specs=[ANY] * n, out_specs=[ANY] * n,
        scratch_shapes=[pltpu.SemaphoreType.DMA((7 * n,)), pltpu.SemaphoreType.DMA((7 * n,)),
                        pltpu.SemaphoreType.DMA((n,))],
    )(*arrs)


def _rs_sibling(arrs, name):
    n = len(arrs)

    def body(*refs):
        ins, outs = refs[:n], refs[n:2 * n]
        send_sems, recv_sems = refs[2 * n:]
        x, y, c = _position()
        copies = []
        for a in range(n):
            for k in range(N_CHIP):
                cp = pltpu.make_async_remote_copy(
                    src_ref=ins[a].at[2 * k + (1 - c)], dst_ref=outs[a].at[k],
                    send_sem=send_sems.at[N_CHIP * a + k], recv_sem=recv_sems.at[N_CHIP * a + k],
                    device_id=(x, y, 1 - c), device_id_type=MESH)
                cp.start()
                copies.append(cp)
        for cp in copies:
            cp.wait()

    return _pcall(
        body, name=name,
        out_shape=[jax.ShapeDtypeStruct((N_CHIP,) + a.shape[1:], a.dtype) for a in arrs],
        in_specs=[ANY] * n, out_specs=[ANY] * n,
        scratch_shapes=[pltpu.SemaphoreType.DMA((N_CHIP * n,)), pltpu.SemaphoreType.DMA((N_CHIP * n,))],
    )(*arrs)


def _rs_ici(arrs, name):
    n = len(arrs)

    def body(*refs):
        ins, outs = refs[:n], refs[n:2 * n]
        send_sems, recv_sems, local_sems = refs[2 * n:]
        x, y, c = _position()
        my_chip = 2 * x + y
        chips = [(1 - x, y), (x, 1 - y), (1 - x, 1 - y)]
        mine = [pltpu.make_async_copy(ins[a].at[my_chip], outs[a].at[my_chip], local_sems.at[a]) for a in range(n)]
        for cp in mine:
            cp.start()
        copies = []
        for a in range(n):
            for j, (px, py) in enumerate(chips):
                cp = pltpu.make_async_remote_copy(
                    src_ref=ins[a].at[2 * px + py], dst_ref=outs[a].at[my_chip],
                    send_sem=send_sems.at[3 * a + j], recv_sem=recv_sems.at[3 * a + j],
                    device_id=(px, py, c), device_id_type=MESH)
                cp.start()
                copies.append(cp)
        for cp in copies:
            cp.wait()
        for cp in mine:
            cp.wait()

    return _pcall(
        body, name=name,
        out_shape=[jax.ShapeDtypeStruct(a.shape, a.dtype) for a in arrs],
        in_specs=[ANY] * n, out_specs=[ANY] * n,
        scratch_shapes=[pltpu.SemaphoreType.DMA((3 * n,)), pltpu.SemaphoreType.DMA((3 * n,)),
                        pltpu.SemaphoreType.DMA((n,))],
    )(*arrs)


def _chip_sum(full, recv, core, name):
    _, rows, cols = full.shape
    tr = 256 if rows % 256 == 0 else rows

    def body(core_ref, a_ref, b_ref, o_ref):
        o_ref[...] = (a_ref[...].astype(F32) + b_ref[...].astype(F32)).astype(o_ref.dtype)

    grid_spec = pltpu.PrefetchScalarGridSpec(
        num_scalar_prefetch=1, grid=(N_CHIP, rows // tr),
        in_specs=[pl.BlockSpec((None, tr, cols), lambda k, i, core_ref: (2 * k + core_ref[0], i, 0)),
                  pl.BlockSpec((None, tr, cols), lambda k, i, core_ref: (k, i, 0))],
        out_specs=pl.BlockSpec((None, tr, cols), lambda k, i, core_ref: (k, i, 0)))
    return _pcall(body, name=name, out_shape=jax.ShapeDtypeStruct((N_CHIP, rows, cols), full.dtype),
                  grid_spec=grid_spec)(core, full, recv)


def _matmul(a, b, *, name, ta=False, tb=False, tm=1024, tn=1024, tk=512, out_dtypes=(BF16,), epilogue=None,
            row_extras=(), tile_extras=(), out_shape=None, out_map=None):
    m, k = (a.shape[1], a.shape[0]) if ta else a.shape
    n = b.shape[0] if tb else b.shape[1]
    tm, tn, tk = _blk(m, tm), _blk(n, tn), _blk(k, tk)
    nk = k // tk
    dims = (((0 if ta else 1,), (1 if tb else 0,)), ((), ()))
    n_row, n_tile, n_out = len(row_extras), len(tile_extras), len(out_dtypes)

    def body(*refs):
        a_ref, b_ref = refs[:2]
        extras = refs[2:2 + n_row + n_tile]
        outs = refs[2 + n_row + n_tile:2 + n_row + n_tile + n_out]
        acc_ref = refs[-1]
        kk = pl.program_id(2)
        part = lax.dot_general(a_ref[...].astype(BF16), b_ref[...].astype(BF16), dims, preferred_element_type=F32)

        def finish(acc):
            res = (acc,) if epilogue is None else epilogue(acc, *[e[...] for e in extras])
            for o_ref, r in zip(outs, res):
                o_ref[...] = r.astype(o_ref.dtype)

        if nk == 1:
            finish(part)
        else:
            @pl.when(kk == 0)
            def _():
                acc_ref[...] = part

            @pl.when(kk > 0)
            def _():
                acc_ref[...] += part

            @pl.when(kk == nk - 1)
            def _():
                finish(acc_ref[...])

    a_spec = pl.BlockSpec((tk, tm), lambda i, j, kk: (kk, i)) if ta else pl.BlockSpec((tm, tk), lambda i, j, kk: (i, kk))
    b_spec = pl.BlockSpec((tn, tk), lambda i, j, kk: (j, kk)) if tb else pl.BlockSpec((tk, tn), lambda i, j, kk: (kk, j))
    in_specs = [a_spec, b_spec]
    in_specs += [pl.BlockSpec((tm, LANES), lambda i, j, kk: (i, 0)) for _ in row_extras]
    in_specs += [pl.BlockSpec((tm, tn), lambda i, j, kk: (i, j)) for _ in tile_extras]
    if out_map is None:
        out_specs = [pl.BlockSpec((tm, tn), lambda i, j, kk: (i, j)) for _ in out_dtypes]
        shapes = [jax.ShapeDtypeStruct((m, n), dt) for dt in out_dtypes]
    else:
        out_specs = [out_map(tm, tn)]
        shapes = [jax.ShapeDtypeStruct(out_shape, out_dtypes[0])]
    acc_shape = (tm, tn) if nk > 1 else (8, LANES)
    res = _pcall(body, name=name, out_shape=shapes, grid=(m // tm, n // tn, nk), in_specs=in_specs,
                 out_specs=out_specs, scratch_shapes=[pltpu.VMEM(acc_shape, F32)])(a, b, *row_extras, *tile_extras)
    return res[0] if n_out == 1 else res


def _rope_tile(acc, cos, sin):
    parts = []
    for g in range(acc.shape[1] // HEAD_DIM):
        xg = acc[:, g * HEAD_DIM:(g + 1) * HEAD_DIM]
        parts.append(xg * cos + pltpu.roll(xg, HEAD_DIM // 2, 1) * sin)
    return jnp.concatenate(parts, axis=1) if len(parts) > 1 else parts[0]


def _silu(v):
    return v / (1.0 + jnp.exp(-v))


def _mod_part(c_all, w_mod, b_part):
    d, w = w_mod.shape
    tk = _blk(d, 512)

    def body(c_ref, w_ref, b_ref, o_ref):
        kk = pl.program_id(0)
        cond = _silu(c_ref[...]).astype(BF16)
        part = jnp.dot(cond, w_ref[...].astype(BF16), preferred_element_type=F32)

        @pl.when(kk == 0)
        def _():
            o_ref[...] = part + b_ref[...]

        @pl.when(kk > 0)
        def _():
            o_ref[...] += part

    return _pcall(body, name="mod_part", out_shape=jax.ShapeDtypeStruct((N_DEV, w), F32), grid=(d // tk,),
                  in_specs=[pl.BlockSpec((N_DEV, tk), lambda kk: (0, kk)), pl.BlockSpec((tk, w), lambda kk: (kk, 0)),
                            pl.BlockSpec((1, w), lambda kk: (0, 0))],
                  out_specs=pl.BlockSpec((N_DEV, w), lambda kk: (0, 0)))(c_all, w_mod, b_part)


def _row_call(body, name, t, d, tiled_in, vec_in, tiled_out_dtypes, n_vec_out, tr=256):
    tr = _blk(t, tr)
    tile = pl.BlockSpec((tr, d), lambda i: (i, 0))
    vec = pl.BlockSpec((1, d), lambda i: (0, 0))
    out_shape = [jax.ShapeDtypeStruct((t, d), dt) for dt in tiled_out_dtypes]
    out_shape += [jax.ShapeDtypeStruct((1, d), F32)] * n_vec_out
    return _pcall(body, name=name, out_shape=out_shape, grid=(t // tr,),
                  in_specs=[tile] * len(tiled_in) + [vec] * len(vec_in),
                  out_specs=[tile] * len(tiled_out_dtypes) + [vec] * n_vec_out)(*tiled_in, *vec_in)


def _accumulate(ref, val):
    @pl.when(pl.program_id(0) == 0)
    def _():
        ref[...] = val

    @pl.when(pl.program_id(0) > 0)
    def _():
        ref[...] += val


def _rsum(v):
    return jnp.sum(v, axis=0, keepdims=True)


def _rms(v):
    return lax.rsqrt(jnp.mean(v * v, axis=-1, keepdims=True) + NORM_EPS)


def _rms_bwd(vhat, r, dvhat):
    return r * (dvhat - vhat * jnp.mean(dvhat * vhat, axis=-1, keepdims=True))


def _pre_attn(x, g0, sc_a, sh_a):
    def body(x_ref, g_ref, sc_ref, sh_ref, h_ref):
        xv = x_ref[...]
        h_ref[...] = (xv * _rms(xv) * g_ref[...] * (1.0 + sc_ref[...]) + sh_ref[...]).astype(BF16)

    t, d = x.shape
    return _row_call(body, "pre_attn", t, d, [x], [g0, sc_a, sh_a], [BF16], 0)[0]


def _post_mix(x, mix, gt_a, g1, g2, sc_m, sh_m):
    def body(x_ref, mix_ref, gt_ref, g1_ref, g2_ref, sc_ref, sh_ref, x2_ref, h2_ref):
        mv = mix_ref[...]
        x2 = x_ref[...] + gt_ref[...] * (mv * _rms(mv) * g1_ref[...])
        x2_ref[...] = x2
        h2_ref[...] = (x2 * _rms(x2) * g2_ref[...] * (1.0 + sc_ref[...]) + sh_ref[...]).astype(BF16)

    t, d = x.shape
    return _row_call(body, "post_mix", t, d, [x, mix], [gt_a, g1, g2, sc_m, sh_m], [F32, BF16], 0)


def _final(y, x2, target, gt_m, g3):
    t, d = y.shape

    def body(y_ref, x2_ref, tg_ref, gt_ref, g3_ref, dy_ref, dout_ref, dgt_ref, dg3_ref, loss_ref):
        yv = y_ref[...]
        r = _rms(yv)
        yhat = yv * r
        n3 = yhat * g3_ref[...]
        err = x2_ref[...] + gt_ref[...] * n3 - tg_ref[...]
        _accumulate(loss_ref, jnp.zeros((1, d), F32) + 0.5 * jnp.sum(err * err) / d)
        dout = err * (1.0 / d)
        dout_ref[...] = dout
        _accumulate(dgt_ref, _rsum(dout * n3))
        dn3 = dout * gt_ref[...]
        _accumulate(dg3_ref, _rsum(dn3 * yhat))
        dy_ref[...] = _rms_bwd(yhat, r, dn3 * g3_ref[...]).astype(BF16)

    return _row_call(body, "final", t, d, [y, x2, target], [gt_m, g3], [BF16, F32], 3)


def _mid_bwd(dh2, dout, x2, mix, g2, sc_m, gt_a, g1):
    t, d = x2.shape

    def body(dh2_ref, dout_ref, x2_ref, mix_ref, g2_ref, sc_ref, gt_ref, g1_ref,
             dmix_ref, dx2_ref, dsh_ref, dsc_ref, dg2_ref, dgt_ref, dg1_ref):
        dh2v = dh2_ref[...]
        x2v = x2_ref[...]
        r2 = _rms(x2v)
        x2hat = x2v * r2
        _accumulate(dsh_ref, _rsum(dh2v))
        _accumulate(dsc_ref, _rsum(dh2v * (x2hat * g2_ref[...])))
        dn2 = dh2v * (1.0 + sc_ref[...])
        _accumulate(dg2_ref, _rsum(dn2 * x2hat))
        dx2 = dout_ref[...] + _rms_bwd(x2hat, r2, dn2 * g2_ref[...])
        dx2_ref[...] = dx2
        mv = mix_ref[...]
        r1 = _rms(mv)
        mhat = mv * r1
        _accumulate(dgt_ref, _rsum(dx2 * (mhat * g1_ref[...])))
        dn1 = dx2 * gt_ref[...]
        _accumulate(dg1_ref, _rsum(dn1 * mhat))
        dmix_ref[...] = _rms_bwd(mhat, r1, dn1 * g1_ref[...]).astype(BF16)

    return _row_call(body, "mid_bwd", t, d, [dh2, dout, x2, mix], [g2, sc_m, gt_a, g1], [BF16, F32], 5)


def _x_bwd(dh1, dx2, x, g0, sc_a):
    t, d = x.shape

    def body(dh1_ref, dx2_ref, x_ref, g0_ref, sc_ref, dx_ref, dsh_ref, dsc_ref, dg0_ref):
        dh1v = dh1_ref[...]
        xv = x_ref[...]
        r0 = _rms(xv)
        xhat = xv * r0
        _accumulate(dsh_ref, _rsum(dh1v))
        _accumulate(dsc_ref, _rsum(dh1v * (xhat * g0_ref[...])))
        dn0 = dh1v * (1.0 + sc_ref[...])
        _accumulate(dg0_ref, _rsum(dn0 * xhat))
        dx_ref[...] = dx2_ref[...] + _rms_bwd(xhat, r0, dn0 * g0_ref[...])

    return _row_call(body, "x_bwd", t, d, [dh1, dx2, x], [g0, sc_a], [F32], 3)


def _pick_lane(block, h):
    lane = lax.broadcasted_iota(jnp.int32, block.shape, 1)
    return jnp.sum(jnp.where(lane == h, block, 0.0), axis=1, keepdims=True)


def _put_lane(ref, rows, h, col):
    old = ref[rows, :]
    lane = lax.broadcasted_iota(jnp.int32, old.shape, 1)
    ref[rows, :] = jnp.where(lane == h, col, old)


def _tri(n, lower):
    r = lax.broadcasted_iota(jnp.int32, (n, n), 0)
    c = lax.broadcasted_iota(jnp.int32, (n, n), 1)
    return jnp.where((c <= r) if lower else (c >= r), 1.0, 0.0).astype(F32)


def _cum_fwd(fg, b128):
    t = fg.shape[0]
    nb = t // LANES

    def body(fg_ref, b_ref, cum_ref, cumt_ref):
        tri = _tri(LANES, True)
        carry = jnp.zeros((1, LANES), F32)
        for i in range(nb):
            z = fg_ref[i * LANES:(i + 1) * LANES, :] + b_ref[...]
            lf = jnp.minimum(z, 0.0) - jnp.log(1.0 + jnp.exp(-jnp.abs(z)))
            blk = jnp.dot(tri, lf, precision=lax.Precision.HIGHEST, preferred_element_type=F32) + carry
            cum_ref[i * LANES:(i + 1) * LANES, :] = blk
            carry = blk[LANES - 1:LANES, :]
        cumt_ref[...] = cum_ref[...].T[0:N_FOX, :]

    return _pcall(body, name="cum_fwd",
                  out_shape=[jax.ShapeDtypeStruct((t, LANES), F32), jax.ShapeDtypeStruct((N_FOX, t), F32)],
                  grid=(1,),
                  in_specs=[pl.BlockSpec((t, LANES), lambda i: (0, 0)), pl.BlockSpec((1, LANES), lambda i: (0, 0))],
                  out_specs=[pl.BlockSpec((t, LANES), lambda i: (0, 0)), pl.BlockSpec((N_FOX, t), lambda i: (0, 0))],
                  )(fg, b128)


def _fg_bwd(dcs_rows, fg, b128):
    t = fg.shape[0]
    nb = t // LANES

    def body(dcs_ref, fg_ref, b_ref, dfg_ref, db_ref, dcum_ref):
        dcum_ref[...] = -jnp.concatenate([dcs_ref[...], jnp.zeros((LANES - N_FOX, t), F32)], axis=0).T
        tri = _tri(LANES, False)
        carry = jnp.zeros((1, LANES), F32)
        db = jnp.zeros((1, LANES), F32)
        for i in reversed(range(nb)):
            rows = slice(i * LANES, (i + 1) * LANES)
            dlf = jnp.dot(tri, dcum_ref[rows, :], precision=lax.Precision.HIGHEST, preferred_element_type=F32) + carry
            carry = dlf[0:1, :]
            z = fg_ref[rows, :] + b_ref[...]
            dfg = dlf / (1.0 + jnp.exp(z))
            dfg_ref[rows, :] = dfg.astype(BF16)
            db = db + _rsum(dfg)
        db_ref[...] = db

    full = pl.BlockSpec((t, LANES), lambda i: (0, 0))
    vec = pl.BlockSpec((1, LANES), lambda i: (0, 0))
    return _pcall(body, name="fg_bwd",
                  out_shape=[jax.ShapeDtypeStruct((t, LANES), BF16), jax.ShapeDtypeStruct((1, LANES), F32)],
                  grid=(1,), in_specs=[pl.BlockSpec((N_FOX, t), lambda i: (0, 0)), full, vec], out_specs=[full, vec],
                  scratch_shapes=[pltpu.VMEM((t, LANES), F32)])(dcs_rows, fg, b128)


def _head_spec(t, col0, div=1):
    return pl.BlockSpec((t, HEAD_DIM), lambda h: (0, col0 + h // div))


def _fox_scores(q, k, cq, ck, i, tq, end):
    s = lax.dot_general(q, k, NT_DIMS, preferred_element_type=F32) * ATT_SCALE + cq - ck
    row = lax.broadcasted_iota(jnp.int32, (tq, end), 0) + i * tq
    col = lax.broadcasted_iota(jnp.int32, (tq, end), 1)
    return jnp.where(row >= col, s, -jnp.inf)


def _fox_fwd(proj_a, cum, cumt):
    t = proj_a.shape[0]
    tq = _blk(t, 512)
    nq = t // tq

    def body(q_ref, k_ref, v_ref, cum_ref, cumt_ref, o_ref, lse_ref):
        h = pl.program_id(0)
        cq_all = _pick_lane(cum_ref[...], h)
        ck_all = cumt_ref[pl.ds(h, 1), :]

        @pl.when(h == 0)
        def _():
            lse_ref[...] = jnp.zeros_like(lse_ref)

        for i in range(nq):
            rows, end = slice(i * tq, (i + 1) * tq), (i + 1) * tq
            s = _fox_scores(q_ref[rows, :], k_ref[0:end, :], cq_all[rows, :], ck_all[:, 0:end], i, tq, end)
            m = jnp.max(s, axis=1, keepdims=True)
            p = jnp.exp(s - m)
            l = jnp.sum(p, axis=1, keepdims=True)
            o = jnp.dot(p.astype(BF16), v_ref[0:end, :], preferred_element_type=F32) / l
            o_ref[rows, :] = o.astype(BF16)
            _put_lane(lse_ref, rows, h, m + jnp.log(l))

    nh = FOX_W // HEAD_DIM
    stat = pl.BlockSpec((t, LANES), lambda h: (0, 0))
    return _pcall(body, name="fox_fwd",
                  out_shape=[jax.ShapeDtypeStruct((t, FOX_W), BF16), jax.ShapeDtypeStruct((t, LANES), F32)],
                  grid=(N_FOX,),
                  in_specs=[_head_spec(t, 0), _head_spec(t, nh), _head_spec(t, 2 * nh), stat,
                            pl.BlockSpec((N_FOX, t), lambda h: (0, 0))],
                  out_specs=[_head_spec(t, 0), stat])(proj_a, proj_a, proj_a, cum, cumt)


def _fox_bwd(proj_a, d_attn, cum, cumt, lse):
    t = proj_a.shape[0]
    tq = _blk(t, 512)
    nq = t // tq

    def body(q_ref, k_ref, v_ref, do_ref, cum_ref, cumt_ref, lse_ref,
             dq_ref, dk_ref, dv_ref, dcs_ref, dk_acc, dv_acc, dcs_acc):
        h = pl.program_id(0)
        cq_all = _pick_lane(cum_ref[...], h)
        ck_all = cumt_ref[pl.ds(h, 1), :]
        lse_all = _pick_lane(lse_ref[...], h)
        dk_acc[...] = jnp.zeros_like(dk_acc)
        dv_acc[...] = jnp.zeros_like(dv_acc)
        dcs_acc[...] = jnp.zeros_like(dcs_acc)
        for i in range(nq):
            rows, end = slice(i * tq, (i + 1) * tq), (i + 1) * tq
            q, k, v, do = q_ref[rows, :], k_ref[0:end, :], v_ref[0:end, :], do_ref[rows, :]
            s = _fox_scores(q, k, cq_all[rows, :], ck_all[:, 0:end], i, tq, end)
            p = jnp.exp(s - lse_all[rows, :])
            dp = lax.dot_general(do, v, NT_DIMS, preferred_element_type=F32)
            ds = p * (dp - jnp.sum(p * dp, axis=1, keepdims=True))
            dcs_acc[:, 0:end] += jnp.sum(ds, axis=0, keepdims=True)
            ds = ds.astype(BF16)
            dq_ref[rows, :] = (jnp.dot(ds, k, preferred_element_type=F32) * ATT_SCALE).astype(BF16)
            dk_acc[0:end, :] += lax.dot_general(ds, q, TN_DIMS, preferred_element_type=F32)
            dv_acc[0:end, :] += lax.dot_general(p.astype(BF16), do, TN_DIMS, preferred_element_type=F32)
        dk_ref[...] = (dk_acc[...] * ATT_SCALE).astype(BF16)
        dv_ref[...] = dv_acc[...].astype(BF16)
        dcs_ref[pl.ds(h, 1), :] = dcs_acc[...]

    nh = FOX_W // HEAD_DIM
    stat = pl.BlockSpec((t, LANES), lambda h: (0, 0))
    rows8 = pl.BlockSpec((N_FOX, t), lambda h: (0, 0))
    head = _head_spec(t, 0)
    wide = jax.ShapeDtypeStruct((t, FOX_W), BF16)
    return _pcall(body, name="fox_bwd",
                  out_shape=[wide, wide, wide, jax.ShapeDtypeStruct((N_FOX, t), F32)],
                  grid=(N_FOX,),
                  in_specs=[_head_spec(t, 0), _head_spec(t, nh), _head_spec(t, 2 * nh), head, stat, rows8, stat],
                  out_specs=[head, head, head, rows8],
                  scratch_shapes=[pltpu.VMEM((t, HEAD_DIM), F32), pltpu.VMEM((t, HEAD_DIM), F32),
                                  pltpu.VMEM((1, t), F32)],
                  )(proj_a, proj_a, proj_a, d_attn, cum, cumt, lse)


def _swa_scores(q, k, i, tq, start, end):
    s = lax.dot_general(q, k, NT_DIMS, preferred_element_type=F32) * ATT_SCALE
    row = lax.broadcasted_iota(jnp.int32, (tq, end - start), 0) + i * tq
    col = lax.broadcasted_iota(jnp.int32, (tq, end - start), 1) + start
    diff = row - col
    return jnp.where((diff >= 0) & (diff < WINDOW), s, -jnp.inf)


def _swa_blocks(t):
    tq = _blk(t, 256)
    return tq, [(i, max(0, i * tq - WINDOW), (i + 1) * tq) for i in range(t // tq)]


def _swa_fwd(proj_b, proj_a, sinks128):
    t = proj_b.shape[0]
    tq, blocks = _swa_blocks(t)

    def body(q_ref, k_ref, v_ref, sink_ref, o_ref, lse_ref):
        h = pl.program_id(0)
        sink = _pick_lane(sink_ref[...], h)

        @pl.when(h == 0)
        def _():
            lse_ref[...] = jnp.zeros_like(lse_ref)

        for i, start, end in blocks:
            rows = slice(i * tq, end)
            s = _swa_scores(q_ref[rows, :], k_ref[start:end, :], i, tq, start, end)
            m = jnp.maximum(jnp.max(s, axis=1, keepdims=True), sink)
            p = jnp.exp(s - m)
            l = jnp.sum(p, axis=1, keepdims=True) + jnp.exp(sink - m)
            o = jnp.dot(p.astype(BF16), v_ref[start:end, :], preferred_element_type=F32) / l
            o_ref[rows, :] = o.astype(BF16)
            _put_lane(lse_ref, rows, h, m + jnp.log(l))

    stat = pl.BlockSpec((t, LANES), lambda h: (0, 0))
    return _pcall(body, name="swa_fwd",
                  out_shape=[jax.ShapeDtypeStruct((t, SWA_W), BF16), jax.ShapeDtypeStruct((t, LANES), F32)],
                  grid=(N_SWA,),
                  in_specs=[_head_spec(t, 0), _head_spec(t, N_SWA, GQA), _head_spec(t, 3 * N_FOX, GQA),
                            pl.BlockSpec((1, LANES), lambda h: (0, 0))],
                  out_specs=[_head_spec(t, 0), stat])(proj_b, proj_b, proj_a, sinks128)


def _rope_bwd(d, cos, sin):
    return d * cos + pltpu.roll(d * sin, HEAD_DIM // 2, 1)


def _swa_bwd(proj_b, proj_a, d_attn, lse, sinks128, cos, sin):
    t = proj_b.shape[0]
    tq, blocks = _swa_blocks(t)

    def body(q_ref, k_ref, v_ref, do_ref, lse_ref, sink_ref, cos_ref, sin_ref,
             dq_ref, dk_ref, dv_ref, dsink_ref, dk_acc, dv_acc):
        h = pl.program_id(0)
        sink = _pick_lane(sink_ref[...], h)
        lse_all = _pick_lane(lse_ref[...], h)

        @pl.when(h == 0)
        def _():
            dsink_ref[...] = jnp.zeros_like(dsink_ref)

        @pl.when(h % GQA == 0)
        def _():
            dk_acc[...] = jnp.zeros_like(dk_acc)
            dv_acc[...] = jnp.zeros_like(dv_acc)

        dsink = jnp.zeros((1, 1), F32)
        for i, start, end in blocks:
            rows = slice(i * tq, end)
            q, k, v, do = q_ref[rows, :], k_ref[start:end, :], v_ref[start:end, :], do_ref[rows, :]
            s = _swa_scores(q, k, i, tq, start, end)
            p = jnp.exp(s - lse_all[rows, :])
            dp = lax.dot_general(do, v, NT_DIMS, preferred_element_type=F32)
            delta = jnp.sum(p * dp, axis=1, keepdims=True)
            ds = (p * (dp - delta)).astype(BF16)
            dq = jnp.dot(ds, k, preferred_element_type=F32) * ATT_SCALE
            dq_ref[rows, :] = _rope_bwd(dq, cos_ref[rows, :], sin_ref[rows, :]).astype(BF16)
            dk_acc[start:end, :] += lax.dot_general(ds, q, TN_DIMS, preferred_element_type=F32)
            dv_acc[start:end, :] += lax.dot_general(p.astype(BF16), do, TN_DIMS, preferred_element_type=F32)
            dsink = dsink - jnp.sum(jnp.exp(sink - lse_all[rows, :]) * delta, axis=0, keepdims=True)
        old = dsink_ref[...]
        lane = lax.broadcasted_iota(jnp.int32, old.shape, 1)
        dsink_ref[...] = jnp.where(lane == h, dsink, old)

        @pl.when(h % GQA == GQA - 1)
        def _():
            dk_ref[...] = _rope_bwd(dk_acc[...] * ATT_SCALE, cos_ref[...], sin_ref[...]).astype(BF16)
            dv_ref[...] = dv_acc[...].astype(BF16)

    stat = pl.BlockSpec((t, LANES), lambda h: (0, 0))
    vec = pl.BlockSpec((1, LANES), lambda h: (0, 0))
    head = _head_spec(t, 0)
    kv_out = _head_spec(t, 0, GQA)
    return _pcall(body, name="swa_bwd",
                  out_shape=[jax.ShapeDtypeStruct((t, SWA_W), BF16), jax.ShapeDtypeStruct((t, KV_W), BF16),
                             jax.ShapeDtypeStruct((t, KV_W), BF16), jax.ShapeDtypeStruct((1, LANES), F32)],
                  grid=(N_SWA,),
                  in_specs=[head, _head_spec(t, N_SWA, GQA), _head_spec(t, 3 * N_FOX, GQA),
                            _head_spec(t, N_FOX), stat, vec, stat, stat],
                  out_specs=[head, kv_out, kv_out, vec],
                  scratch_shapes=[pltpu.VMEM((t, HEAD_DIM), F32), pltpu.VMEM((t, HEAD_DIM), F32)],
                  )(proj_b, proj_b, proj_a, d_attn, lse, sinks128, cos, sin)


def _adamw(w, g, m, v):
    m = ADAM_B1 * m + (1.0 - ADAM_B1) * g
    v = ADAM_B2 * v + (1.0 - ADAM_B2) * (g * g)
    m_hat = m / (1.0 - ADAM_B1 ** ADAM_STEP)
    v_hat = v / (1.0 - ADAM_B2 ** ADAM_STEP)
    delta = -ADAM_LR * (m_hat / (jnp.sqrt(v_hat) + ADAM_EPS) + ADAM_WD * w)
    return delta, m, v


def _adam_pieces(w, m, v, pieces, name):
    rows, cols = w.shape
    n_p = pieces.shape[0]
    tr = 256 if rows % 256 == 0 else rows

    def body(p_ref, w_ref, m_ref, v_ref, g_ref, d_ref, mo_ref, vo_ref):
        g = p_ref[0].astype(F32)
        for k in range(1, n_p):
            g = g + p_ref[k].astype(F32)
        g_ref[...] = g
        d_ref[...], mo_ref[...], vo_ref[...] = _adamw(w_ref[...], g, m_ref[...], v_ref[...])

    tile = pl.BlockSpec((tr, cols), lambda i: (i, 0))
    out = jax.ShapeDtypeStruct((rows, cols), F32)
    return _pcall(body, name=name, out_shape=[out] * 4, grid=(rows // tr,),
                  in_specs=[pl.BlockSpec((n_p, tr, cols), lambda i: (0, i, 0)), tile, tile, tile],
                  out_specs=[tile] * 4)(pieces, w, m, v)


def _adam_mod(c_all, dmod_cols, w, m, v):
    rows, cols = w.shape
    tr = _blk(rows, 256)

    def body(c_ref, dm_ref, w_ref, m_ref, v_ref, g_ref, d_ref, mo_ref, vo_ref):
        cond = _silu(c_ref[...]).astype(BF16)
        g = lax.dot_general(cond, dm_ref[...].astype(BF16), TN_DIMS, preferred_element_type=F32)
        g_ref[...] = g
        d_ref[...], mo_ref[...], vo_ref[...] = _adamw(w_ref[...], g, m_ref[...], v_ref[...])

    tile = pl.BlockSpec((tr, cols), lambda i: (i, 0))
    out = jax.ShapeDtypeStruct((rows, cols), F32)
    return _pcall(body, name="adam_mod", out_shape=[out] * 4, grid=(rows // tr,),
                  in_specs=[pl.BlockSpec((N_DEV, tr), lambda i: (0, i)), pl.BlockSpec((N_DEV, cols), lambda i: (0, 0)),
                            tile, tile, tile],
                  out_specs=[tile] * 4)(c_all, dmod_cols, w, m, v)


def _adam_small(parts, w, m, v):
    nv = w.shape[1]

    def body(p_ref, w_ref, m_ref, v_ref, g_ref, d_ref, mo_ref, vo_ref):
        g = p_ref[0:1, :]
        for k in range(1, N_DEV):
            g = g + p_ref[k:k + 1, :]
        g_ref[...] = g
        d_ref[...], mo_ref[...], vo_ref[...] = _adamw(w_ref[...], g, m_ref[...], v_ref[...])

    vec = pl.BlockSpec((1, nv), lambda i: (0, 0))
    out = jax.ShapeDtypeStruct((1, nv), F32)
    return _pcall(body, name="adam_small", out_shape=[out] * 4, grid=(1,),
                  in_specs=[pl.BlockSpec((N_DEV, nv), lambda i: (0, 0)), vec, vec, vec],
                  out_specs=[vec] * 4)(parts, w, m, v)


def _pad_lanes(v, width=LANES):
    return jnp.pad(v, ((0, 0), (0, width - v.shape[1])))


def kernel(x, c, w_mod, b_mod, g_pre_mix, g_post_mix, w_in, b_forget, swa_sinks, w_out, g_pre_mlp, g_post_mlp, w_up, w_down, loss_target, m_w_mod, m_b_mod, m_g_pre_mix, m_g_post_mix, m_w_in, m_b_forget, m_swa_sinks, m_w_out, m_g_pre_mlp, m_g_post_mlp, m_w_up, m_w_down, v_w_mod, v_b_mod, v_g_pre_mix, v_g_post_mix, v_w_in, v_b_forget, v_swa_sinks, v_w_out, v_g_pre_mlp, v_g_post_mlp, v_w_up, v_w_down):
    ax, ay, ac = _position()
    me = 4 * ax + 2 * ay + ac
    x, target = x[0], loss_target[0]
    t, d = x.shape
    w_mod, w_in, w_out, w_up, w_down = w_mod[0], w_in[0], w_out[0], w_up[0], w_down[0]
    mod_w = w_mod.shape[1]
    in_w = w_in.shape[1]
    in_total = N_DEV * in_w
    assert in_total == 3 * FOX_W + N_FOX + SWA_W + 2 * KV_W and d == FOX_W + SWA_W

    c_all, w_in_g, w_out_g, w_up_g, w_down_g = _all_gather(
        [c, w_in.astype(BF16), w_out.astype(BF16), w_up.astype(BF16), w_down.astype(BF16)], "gather_weights")
    c_all = c_all.reshape(N_DEV, d)
    w_in_full = w_in_g.transpose(1, 0, 2).reshape(d, in_total)
    o_fk, o_fv, o_fg, o_sq, o_sk, o_sv = FOX_W, 2 * FOX_W, 3 * FOX_W, 3 * FOX_W + N_FOX, 3 * FOX_W + N_FOX + SWA_W, 3 * FOX_W + N_FOX + SWA_W + KV_W
    w_a = jnp.concatenate([w_in_full[:, 0:o_fg], w_in_full[:, o_sv:in_total]], axis=1)
    w_b = w_in_full[:, o_sq:o_sv]
    w_fg = _pad_lanes(w_in_full[:, o_fg:o_sq], FG_PAD)
    w_in_r = jnp.concatenate([w_a, w_b, w_fg], axis=1)
    w_out_full = w_out_g.reshape(d, d)
    d_ff = N_DEV * w_up.shape[1]
    w_up_full = w_up_g.transpose(1, 0, 2).reshape(d, d_ff)
    w_down_full = w_down_g.reshape(d_ff, d)

    b_part = lax.dynamic_slice(b_mod, (0, me * mod_w), (1, mod_w))
    mod_parts = _all_gather([_mod_part(c_all, w_mod, b_part)], "gather_mod")[0]
    mod = lax.dynamic_index_in_dim(mod_parts, me, axis=1, keepdims=False).reshape(1, N_DEV * mod_w)
    sh_a, sc_a, gt_a, sh_m, sc_m, gt_m = [mod[:, i * d:(i + 1) * d] for i in range(6)]

    half = HEAD_DIM // 2
    inv_freq = 1.0 / (ROPE_THETA ** (jnp.arange(half, dtype=F32) * (2.0 / HEAD_DIM)))
    ang = jnp.arange(t).astype(F32)[:, None] * inv_freq[None, :]
    cos = jnp.concatenate([jnp.cos(ang), jnp.cos(ang)], axis=1)
    sin = jnp.concatenate([-jnp.sin(ang), jnp.sin(ang)], axis=1)

    b128 = _pad_lanes(b_forget)
    sinks128 = _pad_lanes(swa_sinks)

    h1 = _pre_attn(x, g_pre_mix, sc_a, sh_a)
    proj_a = _matmul(h1, w_a, name="proj_a", tn=256 if w_a.shape[1] % 512 else 512)
    proj_b = _matmul(h1, w_b, name="proj_b", tn=256, epilogue=lambda acc, cs, sn: (_rope_tile(acc, cs, sn),),
                     row_extras=(cos, sin))
    fg = _matmul(h1, w_fg, name="proj_fg", out_dtypes=(F32,))[:, 0:LANES]
    cum, cumt = _cum_fwd(fg, b128)
    fox_o, fox_lse = _fox_fwd(proj_a, cum, cumt)
    swa_o, swa_lse = _swa_fwd(proj_b, proj_a, sinks128)
    attn = jnp.concatenate([fox_o, swa_o], axis=1)
    mix = _matmul(attn, w_out_full, name="out_proj", out_dtypes=(F32,))
    x2, h2 = _post_mix(x, mix, gt_a, g_post_mix, g_pre_mlp, sc_m, sh_m)
    u, act = _matmul(h2, w_up_full, name="mlp_up", out_dtypes=(BF16, BF16),
                     epilogue=lambda acc: (acc, jnp.square(jnp.maximum(acc, 0.0))))
    y = _matmul(act, w_down_full, name="mlp_down", out_dtypes=(F32,))

    dy, dout, dgt_m, dg3, loss_vec = _final(y, x2, target, gt_m, g_post_mlp)
    du = _matmul(dy, w_down_full, name="d_act", tb=True, tile_extras=(u,),
                 epilogue=lambda acc, uu: (acc * (2.0 * jnp.maximum(uu.astype(F32), 0.0)),))
    dw_down = _matmul(act, dy, name="dw_down", ta=True)
    dh2 = _matmul(du, w_up_full, name="d_h2", tb=True, out_dtypes=(F32,))
    shard_ff = d_ff // N_DEV
    dw_up = _matmul(h2, du, name="dw_up", ta=True, tn=_blk(shard_ff, 1024),
                    out_shape=(N_DEV, d, shard_ff),
                    out_map=lambda tm, tn: pl.BlockSpec((None, tm, tn), lambda i, j, kk: (j // (shard_ff // tn), i, j % (shard_ff // tn))))
    dmix, dx2, dsh_m, dsc_m, dg2, dgt_a, dg1 = _mid_bwd(dh2, dout, x2, mix, g_pre_mlp, sc_m, gt_a, g_post_mix)
    d_attn = _matmul(dmix, w_out_full, name="d_attn", tb=True)
    dw_out = _matmul(attn, dmix, name="dw_out", ta=True)
    dqf, dkf, dvf, dcs = _fox_bwd(proj_a, d_attn, cum, cumt, fox_lse)
    dsq, dsk, dsv, dsinks = _swa_bwd(proj_b, proj_a, d_attn, swa_lse, sinks128, cos, sin)
    dfg, db_forget = _fg_bwd(dcs, fg, b128)
    dproj = jnp.concatenate([dqf, dkf, dvf, dsv, dsq, dsk, _pad_lanes(dfg, FG_PAD)], axis=1)
    dh1 = _matmul(dproj, w_in_r, name="d_h1", tb=True, out_dtypes=(F32,))
    dw_in_r = _matmul(h1, dproj, name="dw_in", ta=True)
    grad_x, dsh_a, dsc_a, dg0 = _x_bwd(dh1, dx2, x, g_pre_mix, sc_a)

    n_a = 3 * FOX_W
    dw_in_full = jnp.concatenate([dw_in_r[:, 0:n_a], dw_in_r[:, n_a + KV_W + SWA_W + KV_W:n_a + KV_W + SWA_W + KV_W + N_FOX],
                                  dw_in_r[:, n_a + KV_W:n_a + KV_W + SWA_W + KV_W], dw_in_r[:, n_a:n_a + KV_W]], axis=1)
    dw_in_s = dw_in_full.reshape(d, N_DEV, in_w).transpose(1, 0, 2)
    dw_out_s = dw_out.reshape(N_DEV, d // N_DEV, d)
    dw_down_s = dw_down.reshape(N_DEV, shard_ff, d)

    fulls = [dw_in_s, dw_out_s, dw_up, dw_down_s]
    from_sibling = _rs_sibling(fulls, "rs_sibling")
    core = jnp.reshape(ac, (1,)).astype(jnp.int32)
    chip_sums = [_chip_sum(f, r, core, "chip_sum_%d" % i) for i, (f, r) in enumerate(zip(fulls, from_sibling))]
    pieces = _rs_ici(chip_sums, "rs_ici")

    g_w_in, d_w_in, nm_w_in, nv_w_in = _adam_pieces(w_in, m_w_in[0], v_w_in[0], pieces[0], "adam_w_in")
    g_w_out, d_w_out, nm_w_out, nv_w_out = _adam_pieces(w_out, m_w_out[0], v_w_out[0], pieces[1], "adam_w_out")
    g_w_up, d_w_up, nm_w_up, nv_w_up = _adam_pieces(w_up, m_w_up[0], v_w_up[0], pieces[2], "adam_w_up")
    g_w_down, d_w_down, nm_w_down, nv_w_down = _adam_pieces(w_down, m_w_down[0], v_w_down[0], pieces[3], "adam_w_down")

    small = jnp.concatenate([dsh_a, dsc_a, dgt_a, dsh_m, dsc_m, dgt_m, dg0, dg1, dg2, dg3, db_forget, dsinks,
                             loss_vec[:, 0:LANES]], axis=1)
    small_all = _all_gather([small], "gather_small")[0].reshape(N_DEV, small.shape[1])
    pack = lambda bm, g0_, g1_, g2_, g3_, bf_, sk_: jnp.concatenate(
        [bm, g0_, g1_, g2_, g3_, _pad_lanes(bf_), _pad_lanes(sk_), jnp.zeros((1, LANES), F32)], axis=1)
    p_small = pack(b_mod, g_pre_mix, g_post_mix, g_pre_mlp, g_post_mlp, b_forget, swa_sinks)
    m_small = pack(m_b_mod, m_g_pre_mix, m_g_post_mix, m_g_pre_mlp, m_g_post_mlp, m_b_forget, m_swa_sinks)
    v_small = pack(v_b_mod, v_g_pre_mix, v_g_post_mix, v_g_pre_mlp, v_g_post_mlp, v_b_forget, v_swa_sinks)
    small_out = _adam_small(small_all, p_small, m_small, v_small)

    n_mod = 6 * d

    def unpack(vec):
        o = n_mod
        return (vec[:, 0:n_mod], vec[:, o:o + d], vec[:, o + d:o + 2 * d], vec[:, o + 2 * d:o + 3 * d],
                vec[:, o + 3 * d:o + 4 * d], vec[:, o + 4 * d:o + 4 * d + N_FOX],
                vec[:, o + 4 * d + LANES:o + 4 * d + LANES + N_SWA])

    loss = small_out[0][0, n_mod + 4 * d + 2 * LANES]
    g_small, d_small, nm_small, nv_small = [unpack(vec) for vec in small_out]

    dmod_cols = lax.dynamic_slice(small_all, (0, me * mod_w), (N_DEV, mod_w))
    g_w_mod, d_w_mod, nm_w_mod, nv_w_mod = _adam_mod(c_all, dmod_cols, w_mod, m_w_mod[0], v_w_mod[0])

    def assemble(w_mod_, small_, w_in_, w_out_, w_up_, w_down_):
        b_mod_, g0_, g1_, g2_, g3_, bf_, sk_ = small_
        return [w_mod_[None], b_mod_, g0_, g1_, w_in_[None], bf_, sk_, w_out_[None], g2_, g3_, w_up_[None], w_down_[None]]

    outs = [loss, grad_x[None]]
    outs += assemble(g_w_mod, g_small, g_w_in, g_w_out, g_w_up, g_w_down)
    outs += assemble(d_w_mod, d_small, d_w_in, d_w_out, d_w_up, d_w_down)
    outs += assemble(nm_w_mod, nm_small, nm_w_in, nm_w_out, nm_w_up, nm_w_down)
    outs += assemble(nv_w_mod, nv_small, nv_w_in, nv_w_out, nv_w_up, nv_w_down)
    return tuple(outs)
```

```python
import functools

import jax
import jax.numpy as jnp
from jax import lax
from jax.experimental import pallas as pl
from jax.experimental.pallas import tpu as pltpu

F32 = jnp.float32
BF16 = jnp.bfloat16
MESH = pl.DeviceIdType.MESH

N_DEV = 8
N_CHIP = 4
LANES = 128
HEAD_DIM = 128
N_FOX = 8
N_SWA = 8
N_KV = 2
GQA = N_SWA // N_KV
WINDOW = 128
FOX_W = N_FOX * HEAD_DIM
SWA_W = N_SWA * HEAD_DIM
KV_W = N_KV * HEAD_DIM
ROPE_THETA = 10000.0
NORM_EPS = 1e-6
ATT_SCALE = HEAD_DIM ** -0.5
FG_PAD = 512

ADAM_LR = 0.001
ADAM_B1 = 0.9
ADAM_B2 = 0.999
ADAM_EPS = 1e-08
ADAM_WD = 0.01
ADAM_STEP = 10

VMEM_LIMIT = 56 * 1024 * 1024

NT_DIMS = (((1,), (1,)), ((), ()))
TN_DIMS = (((0,), (0,)), ((), ()))
NN_DIMS = (((1,), (0,)), ((), ()))


def _pcall(body, *, name, out_shape, grid=(), in_specs=None, out_specs=None, scratch_shapes=(), grid_spec=None):
    params = pltpu.CompilerParams(vmem_limit_bytes=VMEM_LIMIT)
    if grid_spec is not None:
        return pl.pallas_call(body, name=name, out_shape=out_shape, grid_spec=grid_spec, compiler_params=params)
    return pl.pallas_call(body, name=name, out_shape=out_shape, grid=grid, in_specs=in_specs, out_specs=out_specs,
                          scratch_shapes=scratch_shapes, compiler_params=params)


def _blk(n, pref):
    if n <= pref:
        return n
    b = (pref // LANES) * LANES
    while n % b:
        b -= LANES
    return b


def _position():
    return lax.axis_index("x"), lax.axis_index("y"), lax.axis_index("c")


ANY = pl.BlockSpec(memory_space=pl.ANY)


def _all_gather(arrs, name):
    n = len(arrs)

    def body(*refs):
        ins, outs = refs[:n], refs[n:2 * n]
        send_sems, recv_sems, local_sems = refs[2 * n:]
        x, y, c = _position()
        me, sibling = (x, y, c), (x, y, 1 - c)
        chips = [(1 - x, y), (x, 1 - y), (1 - x, 1 - y)]

        def slot(p):
            return 4 * p[0] + 2 * p[1] + p[2]

        def copy(a, k, block, to, src=None):
            dst = outs[a].at[slot(block)]
            return pltpu.make_async_remote_copy(
                src_ref=dst if src is None else src, dst_ref=dst,
                send_sem=send_sems.at[7 * a + k], recv_sem=recv_sems.at[7 * a + k],
                device_id=to, device_id_type=MESH)

        mine = [pltpu.make_async_copy(ins[a], outs[a].at[slot(me)], local_sems.at[a]) for a in range(n)]
        for cp in mine:
            cp.start()
        first = []
        for a in range(n):
            first.append(copy(a, 0, me, sibling, src=ins[a]))
            first += [copy(a, 1 + j, me, (*chip, c), src=ins[a]) for j, chip in enumerate(chips)]
        for cp in first:
            cp.start()
        passed = []
        for a in range(n):
            for j, chip in enumerate(chips):
                copy(a, 1 + j, (*chip, c), me).wait_recv()
                cp = copy(a, 4 + j, (*chip, c), sibling)
                cp.start()
                passed.append(cp)
        for a in range(n):
            copy(a, 0, sibling, me).wait_recv()
            for j, chip in enumerate(chips):
                copy(a, 4 + j, (*chip, 1 - c), me).wait_recv()
        for cp in first + passed:
            cp.wait_send()
        for cp in mine:
            cp.wait()

    return _pcall(
        body, name=name,
        out_shape=[jax.ShapeDtypeStruct((N_DEV,) + a.shape, a.dtype) for a in arrs],
        in_specs=[ANY] * n, out_specs=[ANY] * n,
        scratch_shapes=[pltpu.SemaphoreType.DMA((7 * n,)), pltpu.SemaphoreType.DMA((7 * n,)),
                        pltpu.SemaphoreType.DMA((n,))],
    )(*arrs)


HBM = pl.BlockSpec(memory_space=pltpu.HBM)
SEM = pl.BlockSpec(memory_space=pltpu.SEMAPHORE)
EFFECT = pltpu.SideEffectType.DATAFLOW_SIDE_EFFECTING


def _hbm(a):
    return pltpu.with_memory_space_constraint(a, pltpu.HBM)


def _gather_peers():
    x, y, c = _position()
    return [(x, y, 1 - c), (1 - x, y, c), (x, 1 - y, c), (1 - x, 1 - y, c)]


def _ag_start(shards):
    n = len(shards)
    lands = [_hbm(lax.empty((N_DEV,) + s.shape, s.dtype)) for s in shards]

    def body(*refs):
        srcs, land, send, recv = refs[:n], refs[n:2 * n], refs[2 * n:3 * n], refs[3 * n:4 * n]
        token = refs[6 * n]
        x, y, c = _position()
        for a in range(n):
            for k, to in enumerate(_gather_peers()):
                pltpu.make_async_remote_copy(
                    src_ref=srcs[a], dst_ref=land[a].at[4 * x + 2 * y + c], send_sem=send[a].at[k],
                    recv_sem=recv[a].at[k], device_id=to, device_id_type=MESH).start()
        token[...] = jnp.zeros_like(token)

    sems = [pltpu.SemaphoreType.DMA((4,))] * (2 * n)
    out = pl.pallas_call(
        body, name="ag_start",
        out_shape=sems + [pltpu.HBM(s.shape, s.dtype) for s in shards] + [pltpu.HBM(l.shape, l.dtype) for l in lands]
        + [jax.ShapeDtypeStruct((8, LANES), F32)],
        in_specs=[HBM] * (2 * n), out_specs=[SEM] * (2 * n) + [HBM] * (2 * n) + [pl.BlockSpec(memory_space=pltpu.VMEM)],
        input_output_aliases={**{a: 2 * n + a for a in range(n)}, **{n + a: 3 * n + a for a in range(n)}},
        compiler_params=pltpu.CompilerParams(has_side_effects=EFFECT),
    )(*[_hbm(s) for s in shards], *lands)
    return out[:n], out[n:2 * n], out[2 * n:3 * n], out[3 * n:4 * n], out[4 * n]


def _ag_wait(send, recv, shard_thru, land_thru, after, name):
    def body(v_ref, land_ref, send_sem, recv_sem, after_ref, v_dead, got_ref):
        for k, to in enumerate(_gather_peers()):
            cp = pltpu.make_async_remote_copy(
                src_ref=v_ref, dst_ref=land_ref.at[0], send_sem=send_sem.at[k], recv_sem=recv_sem.at[k],
                device_id=to, device_id_type=MESH)
            cp.wait_send()
            cp.wait_recv()

    return pl.pallas_call(
        body, name=name,
        out_shape=(pltpu.HBM(shard_thru.shape, shard_thru.dtype), pltpu.HBM(land_thru.shape, land_thru.dtype)),
        in_specs=(HBM, HBM, SEM, SEM, ANY), out_specs=(HBM, HBM), input_output_aliases={0: 0, 1: 1},
        compiler_params=pltpu.CompilerParams(has_side_effects=EFFECT),
    )(shard_thru, land_thru, send, recv, after)


def _ag_forward(land, shard, name):
    def body(land_in, shard_ref, land_ref, send_sems, recv_sems, local_sem):
        x, y, c = _position()
        mine = pltpu.make_async_copy(shard_ref, land_ref.at[4 * x + 2 * y + c], local_sem)
        mine.start()
        copies = []
        for j, (px, py) in enumerate([(1 - x, y), (x, 1 - y), (1 - x, 1 - y)]):
            block = land_ref.at[4 * px + 2 * py + c]
            cp = pltpu.make_async_remote_copy(src_ref=block, dst_ref=block, send_sem=send_sems.at[j],
                                              recv_sem=recv_sems.at[j], device_id=(x, y, 1 - c), device_id_type=MESH)
            cp.start()
            copies.append(cp)
        for cp in copies:
            cp.wait()
        mine.wait()

    return pl.pallas_call(
        body, name=name, out_shape=jax.ShapeDtypeStruct(land.shape, land.dtype),
        in_specs=[ANY, ANY], out_specs=ANY, input_output_aliases={0: 0},
        scratch_shapes=[pltpu.SemaphoreType.DMA((3,)), pltpu.SemaphoreType.DMA((3,)), pltpu.SemaphoreType.DMA],
    )(land, shard)


def _rs_peers():
    x, y, c = _position()
    return [(1 - x, y, c), (x, 1 - y, c), (1 - x, 1 - y, c)]


def _rs_start(chip_sums, name):
    land = _hbm(lax.empty(chip_sums.shape, chip_sums.dtype))

    def body(src, land_ref, send, recv, src_thru, land_thru, token):
        x, y, c = _position()
        for j, (px, py, pc) in enumerate(_rs_peers()):
            pltpu.make_async_remote_copy(
                src_ref=src.at[2 * px + py], dst_ref=land_ref.at[2 * x + y], send_sem=send.at[j], recv_sem=recv.at[j],
                device_id=(px, py, pc), device_id_type=MESH).start()
        token[...] = jnp.zeros_like(token)

    return pl.pallas_call(
        body, name=name,
        out_shape=[pltpu.SemaphoreType.DMA((3,)), pltpu.SemaphoreType.DMA((3,)),
                   pltpu.HBM(chip_sums.shape, chip_sums.dtype), pltpu.HBM(land.shape, land.dtype),
                   jax.ShapeDtypeStruct((8, LANES), F32)],
        in_specs=[HBM, HBM], out_specs=[SEM, SEM, HBM, HBM, pl.BlockSpec(memory_space=pltpu.VMEM)],
        input_output_aliases={0: 2, 1: 3},
        compiler_params=pltpu.CompilerParams(has_side_effects=EFFECT),
    )(_hbm(chip_sums), land)


def _rs_wait(send, recv, src_thru, land_thru, after, name):
    def body(src, land_ref, send_sem, recv_sem, after_ref, src_out, land_out):
        for j, to in enumerate(_rs_peers()):
            cp = pltpu.make_async_remote_copy(
                src_ref=src.at[0], dst_ref=land_ref.at[0], send_sem=send_sem.at[j], recv_sem=recv_sem.at[j],
                device_id=to, device_id_type=MESH)
            cp.wait_send()
            cp.wait_recv()

    return pl.pallas_call(
        body, name=name,
        out_shape=(pltpu.HBM(src_thru.shape, src_thru.dtype), pltpu.HBM(land_thru.shape, land_thru.dtype)),
        in_specs=(HBM, HBM, SEM, SEM, ANY), out_specs=(HBM, HBM), input_output_aliases={0: 0, 1: 1},
        compiler_params=pltpu.CompilerParams(has_side_effects=EFFECT),
    )(src_thru, land_thru, send, recv, after)


def _rs_sibling(arrs, name):
    n = len(arrs)

    def body(*refs):
        ins, outs = refs[:n], refs[n:2 * n]
        send_sems, recv_sems = refs[2 * n:]
        x, y, c = _position()
        copies = []
        for a in range(n):
            for k in range(N_CHIP):
                cp = pltpu.make_async_remote_copy(
                    src_ref=ins[a].at[2 * k + (1 - c)], dst_ref=outs[a].at[k],
                    send_sem=send_sems.at[N_CHIP * a + k], recv_sem=recv_sems.at[N_CHIP * a + k],
                    device_id=(x, y, 1 - c), device_id_type=MESH)
                cp.start()
                copies.append(cp)
        for cp in copies:
            cp.wait()

    return _pcall(
        body, name=name,
        out_shape=[jax.ShapeDtypeStruct((N_CHIP,) + a.shape[1:], a.dtype) for a in arrs],
        in_specs=[ANY] * n, out_specs=[ANY] * n,
        scratch_shapes=[pltpu.SemaphoreType.DMA((N_CHIP * n,)), pltpu.SemaphoreType.DMA((N_CHIP * n,))],
    )(*arrs)


def _chip_sum(full, recv, core, name):
    _, rows, cols = full.shape
    tr = 256 if rows % 256 == 0 else rows

    def body(core_ref, a_ref, b_ref, o_ref):
        o_ref[...] = (a_ref[...].astype(F32) + b_ref[...].astype(F32)).astype(o_ref.dtype)

    grid_spec = pltpu.PrefetchScalarGridSpec(
        num_scalar_prefetch=1, grid=(N_CHIP, rows // tr),
        in_specs=[pl.BlockSpec((None, tr, cols), lambda k, i, core_ref: (2 * k + core_ref[0], i, 0)),
                  pl.BlockSpec((None, tr, cols), lambda k, i, core_ref: (k, i, 0))],
        out_specs=pl.BlockSpec((None, tr, cols), lambda k, i, core_ref: (k, i, 0)))
    return _pcall(body, name=name, out_shape=jax.ShapeDtypeStruct((N_CHIP, rows, cols), full.dtype),
                  grid_spec=grid_spec)(core, full, recv)


def _matmul(a, b, *, name, ta=False, tb=False, tm=1024, tn=1024, tk=2048, out_dtypes=(BF16,), epilogue=None,
            row_extras=(), tile_extras=(), out_shape=None, out_map=None, b_sharded=False, n_cols=None, n_off=0):
    m, k = (a.shape[1], a.shape[0]) if ta else a.shape
    if b_sharded:
        shard_c = b.shape[2]
        n, kb = (b.shape[1], N_DEV * shard_c) if tb else (N_DEV * shard_c, b.shape[1])
        tn, tk = (tn, min(tk, shard_c)) if tb else (min(tn, shard_c), tk)
    else:
        n, kb = b.shape if tb else (b.shape[1], b.shape[0])
    assert kb == k, (name, kb, k)
    if n_cols is not None:
        n = n_cols
    tm, tn, tk = _blk(m, tm), _blk(n, tn), _blk(k, tk)
    assert n_off % tn == 0
    nk = k // tk
    dims = (((0 if ta else 1,), (1 if tb else 0,)), ((), ()))
    n_row, n_tile, n_out = len(row_extras), len(tile_extras), len(out_dtypes)

    def body(*refs):
        a_ref, b_ref = refs[:2]
        extras = refs[2:2 + n_row + n_tile]
        outs = refs[2 + n_row + n_tile:2 + n_row + n_tile + n_out]
        acc_ref = refs[-1]
        jj, kk = pl.program_id(1), pl.program_id(2)
        part = lax.dot_general(a_ref[...].astype(BF16), b_ref[...].astype(BF16), dims, preferred_element_type=F32)

        def finish(acc):
            res = (acc,) if epilogue is None else epilogue(acc, jj, *[e[...] for e in extras])
            for o_ref, r in zip(outs, res):
                o_ref[...] = r.astype(o_ref.dtype)

        if nk == 1:
            finish(part)
        else:
            @pl.when(kk == 0)
            def _():
                acc_ref[...] = part

            @pl.when(kk > 0)
            def _():
                acc_ref[...] += part

            @pl.when(kk == nk - 1)
            def _():
                finish(acc_ref[...])

    a_spec = pl.BlockSpec((tk, tm), lambda i, j, kk: (kk, i)) if ta else pl.BlockSpec((tm, tk), lambda i, j, kk: (i, kk))
    if b_sharded and tb:
        per = shard_c // tk
        b_spec = pl.BlockSpec((None, tn, tk), lambda i, j, kk: (kk // per, j, kk % per))
    elif b_sharded:
        per = shard_c // tn
        b_spec = pl.BlockSpec((None, tk, tn), lambda i, j, kk: (j // per, kk, j % per))
    elif tb:
        b_spec = pl.BlockSpec((tn, tk), lambda i, j, kk: (j, kk))
    else:
        b_spec = pl.BlockSpec((tk, tn), lambda i, j, kk: (kk, j + n_off // tn))
    in_specs = [a_spec, b_spec]
    in_specs += [pl.BlockSpec((tm, LANES), lambda i, j, kk: (i, 0)) for _ in row_extras]
    in_specs += [pl.BlockSpec((tm, tn), lambda i, j, kk: (i, j)) for _ in tile_extras]
    if out_map is None:
        out_specs = [pl.BlockSpec((tm, tn), lambda i, j, kk: (i, j)) for _ in out_dtypes]
        shapes = [jax.ShapeDtypeStruct((m, n), dt) for dt in out_dtypes]
    else:
        out_specs = [out_map(tm, tn)]
        shapes = [jax.ShapeDtypeStruct(out_shape, out_dtypes[0])]
    acc_shape = (tm, tn) if nk > 1 else (8, LANES)
    res = _pcall(body, name=name, out_shape=shapes, grid=(m // tm, n // tn, nk), in_specs=in_specs,
                 out_specs=out_specs, scratch_shapes=[pltpu.VMEM(acc_shape, F32)])(a, b, *row_extras, *tile_extras)
    return res[0] if n_out == 1 else res


def _rope_cols(acc, j, cos, sin, n_rope):
    width = acc.shape[1]
    parts = []
    for g in range(width // HEAD_DIM):
        xg = acc[:, g * HEAD_DIM:(g + 1) * HEAD_DIM]
        roped = xg * cos + pltpu.roll(xg, HEAD_DIM // 2, 1) * sin
        parts.append(jnp.where(j * width + g * HEAD_DIM < n_rope, roped, xg))
    return jnp.concatenate(parts, axis=1) if len(parts) > 1 else parts[0]


def _silu(v):
    return v / (1.0 + jnp.exp(-v))


def _mod_part(c_all, w_mod, b_part):
    d, w = w_mod.shape
    tk = _blk(d, 512)

    def body(c_ref, w_ref, b_ref, o_ref):
        kk = pl.program_id(0)
        cond = _silu(c_ref[...]).astype(BF16)
        part = jnp.dot(cond, w_ref[...].astype(BF16), preferred_element_type=F32)

        @pl.when(kk == 0)
        def _():
            o_ref[...] = part + b_ref[...]

        @pl.when(kk > 0)
        def _():
            o_ref[...] += part

    return _pcall(body, name="mod_part", out_shape=jax.ShapeDtypeStruct((N_DEV, w), F32), grid=(d // tk,),
                  in_specs=[pl.BlockSpec((N_DEV, tk), lambda kk: (0, kk)), pl.BlockSpec((tk, w), lambda kk: (kk, 0)),
                            pl.BlockSpec((1, w), lambda kk: (0, 0))],
                  out_specs=pl.BlockSpec((N_DEV, w), lambda kk: (0, 0)))(c_all, w_mod, b_part)


def _row_call(body, name, t, d, tiled_in, vec_in, tiled_out_dtypes, n_vec_out, tr=256):
    tr = _blk(t, tr)
    tile = pl.BlockSpec((tr, d), lambda i: (i, 0))
    vec = pl.BlockSpec((1, d), lambda i: (0, 0))
    out_shape = [jax.ShapeDtypeStruct((t, d), dt) for dt in tiled_out_dtypes]
    out_shape += [jax.ShapeDtypeStruct((1, d), F32)] * n_vec_out
    return _pcall(body, name=name, out_shape=out_shape, grid=(t // tr,),
                  in_specs=[tile] * len(tiled_in) + [vec] * len(vec_in),
                  out_specs=[tile] * len(tiled_out_dtypes) + [vec] * n_vec_out)(*tiled_in, *vec_in)


def _accumulate(ref, val):
    @pl.when(pl.program_id(0) == 0)
    def _():
        ref[...] = val

    @pl.when(pl.program_id(0) > 0)
    def _():
        ref[...] += val


def _rsum(v):
    return jnp.sum(v, axis=0, keepdims=True)


def _rms(v):
    return lax.rsqrt(jnp.mean(v * v, axis=-1, keepdims=True) + NORM_EPS)


def _rms_bwd(vhat, r, dvhat):
    return r * (dvhat - vhat * jnp.mean(dvhat * vhat, axis=-1, keepdims=True))


def _pre_attn(x, g0, sc_a, sh_a):
    def body(x_ref, g_ref, sc_ref, sh_ref, h_ref):
        xv = x_ref[...]
        h_ref[...] = (xv * _rms(xv) * g_ref[...] * (1.0 + sc_ref[...]) + sh_ref[...]).astype(BF16)

    t, d = x.shape
    return _row_call(body, "pre_attn", t, d, [x], [g0, sc_a, sh_a], [BF16], 0)[0]


def _post_mix(x, mix, gt_a, g1, g2, sc_m, sh_m):
    def body(x_ref, mix_ref, gt_ref, g1_ref, g2_ref, sc_ref, sh_ref, x2_ref, h2_ref):
        mv = mix_ref[...]
        x2 = x_ref[...] + gt_ref[...] * (mv * _rms(mv) * g1_ref[...])
        x2_ref[...] = x2
        h2_ref[...] = (x2 * _rms(x2) * g2_ref[...] * (1.0 + sc_ref[...]) + sh_ref[...]).astype(BF16)

    t, d = x.shape
    return _row_call(body, "post_mix", t, d, [x, mix], [gt_a, g1, g2, sc_m, sh_m], [F32, BF16], 0)


def _final(y, x2, target, gt_m, g3):
    t, d = y.shape

    def body(y_ref, x2_ref, tg_ref, gt_ref, g3_ref, dy_ref, dout_ref, dgt_ref, dg3_ref, loss_ref):
        yv = y_ref[...]
        r = _rms(yv)
        yhat = yv * r
        n3 = yhat * g3_ref[...]
        err = x2_ref[...] + gt_ref[...] * n3 - tg_ref[...]
        _accumulate(loss_ref, jnp.zeros((1, d), F32) + 0.5 * jnp.sum(err * err) / d)
        dout = err * (1.0 / d)
        dout_ref[...] = dout
        _accumulate(dgt_ref, _rsum(dout * n3))
        dn3 = dout * gt_ref[...]
        _accumulate(dg3_ref, _rsum(dn3 * yhat))
        dy_ref[...] = _rms_bwd(yhat, r, dn3 * g3_ref[...]).astype(BF16)

    return _row_call(body, "final", t, d, [y, x2, target], [gt_m, g3], [BF16, F32], 3)


def _mid_bwd(dh2, dout, x2, mix, g2, sc_m, gt_a, g1):
    t, d = x2.shape

    def body(dh2_ref, dout_ref, x2_ref, mix_ref, g2_ref, sc_ref, gt_ref, g1_ref,
             dmix_ref, dx2_ref, dsh_ref, dsc_ref, dg2_ref, dgt_ref, dg1_ref):
        dh2v = dh2_ref[...]
        x2v = x2_ref[...]
        r2 = _rms(x2v)
        x2hat = x2v * r2
        _accumulate(dsh_ref, _rsum(dh2v))
        _accumulate(dsc_ref, _rsum(dh2v * (x2hat * g2_ref[...])))
        dn2 = dh2v * (1.0 + sc_ref[...])
        _accumulate(dg2_ref, _rsum(dn2 * x2hat))
        dx2 = dout_ref[...] + _rms_bwd(x2hat, r2, dn2 * g2_ref[...])
        dx2_ref[...] = dx2
        mv = mix_ref[...]
        r1 = _rms(mv)
        mhat = mv * r1
        _accumulate(dgt_ref, _rsum(dx2 * (mhat * g1_ref[...])))
        dn1 = dx2 * gt_ref[...]
        _accumulate(dg1_ref, _rsum(dn1 * mhat))
        dmix_ref[...] = _rms_bwd(mhat, r1, dn1 * g1_ref[...]).astype(BF16)

    return _row_call(body, "mid_bwd", t, d, [dh2, dout, x2, mix], [g2, sc_m, gt_a, g1], [BF16, F32], 5)


def _x_bwd(dh1, dx2, x, g0, sc_a):
    t, d = x.shape

    def body(dh1_ref, dx2_ref, x_ref, g0_ref, sc_ref, dx_ref, dsh_ref, dsc_ref, dg0_ref):
        dh1v = dh1_ref[...]
        xv = x_ref[...]
        r0 = _rms(xv)
        xhat = xv * r0
        _accumulate(dsh_ref, _rsum(dh1v))
        _accumulate(dsc_ref, _rsum(dh1v * (xhat * g0_ref[...])))
        dn0 = dh1v * (1.0 + sc_ref[...])
        _accumulate(dg0_ref, _rsum(dn0 * xhat))
        dx_ref[...] = dx2_ref[...] + _rms_bwd(xhat, r0, dn0 * g0_ref[...])

    return _row_call(body, "x_bwd", t, d, [dh1, dx2, x], [g0, sc_a], [F32], 3)


def _pick_lane(block, h):
    lane = lax.broadcasted_iota(jnp.int32, block.shape, 1)
    return jnp.sum(jnp.where(lane == h, block, 0.0), axis=1, keepdims=True)


def _put_lane(ref, rows, h, col):
    old = ref[rows, :]
    lane = lax.broadcasted_iota(jnp.int32, old.shape, 1)
    ref[rows, :] = jnp.where(lane == h, col, old)


def _tri(n, lower):
    r = lax.broadcasted_iota(jnp.int32, (n, n), 0)
    c = lax.broadcasted_iota(jnp.int32, (n, n), 1)
    return jnp.where((c <= r) if lower else (c >= r), 1.0, 0.0).astype(F32)


def _cum_fwd(fg, b128):
    t = fg.shape[0]
    nb = t // LANES

    def body(fg_ref, b_ref, cum_ref, cumt_ref):
        tri = _tri(LANES, True)
        carry = jnp.zeros((1, LANES), F32)
        for i in range(nb):
            z = fg_ref[i * LANES:(i + 1) * LANES, :] + b_ref[...]
            lf = jnp.minimum(z, 0.0) - jnp.log(1.0 + jnp.exp(-jnp.abs(z)))
            blk = jnp.dot(tri, lf, precision=lax.Precision.HIGHEST, preferred_element_type=F32) + carry
            cum_ref[i * LANES:(i + 1) * LANES, :] = blk
            carry = blk[LANES - 1:LANES, :]
        cumt_ref[...] = cum_ref[...].T[0:N_FOX, :]

    return _pcall(body, name="cum_fwd",
                  out_shape=[jax.ShapeDtypeStruct((t, LANES), F32), jax.ShapeDtypeStruct((N_FOX, t), F32)],
                  grid=(1,),
                  in_specs=[pl.BlockSpec((t, LANES), lambda i: (0, 0)), pl.BlockSpec((1, LANES), lambda i: (0, 0))],
                  out_specs=[pl.BlockSpec((t, LANES), lambda i: (0, 0)), pl.BlockSpec((N_FOX, t), lambda i: (0, 0))],
                  )(fg, b128)


def _fg_bwd(dcs_rows, fg, b128):
    t = fg.shape[0]
    nb = t // LANES

    def body(dcs_ref, fg_ref, b_ref, dfg_ref, db_ref, dcum_ref):
        dcum_ref[...] = -jnp.concatenate([dcs_ref[...], jnp.zeros((LANES - N_FOX, t), F32)], axis=0).T
        tri = _tri(LANES, False)
        carry = jnp.zeros((1, LANES), F32)
        db = jnp.zeros((1, LANES), F32)
        for i in reversed(range(nb)):
            rows = slice(i * LANES, (i + 1) * LANES)
            dlf = jnp.dot(tri, dcum_ref[rows, :], precision=lax.Precision.HIGHEST, preferred_element_type=F32) + carry
            carry = dlf[0:1, :]
            z = fg_ref[rows, :] + b_ref[...]
            dfg = dlf / (1.0 + jnp.exp(z))
            dfg_ref[rows, :] = dfg.astype(BF16)
            db = db + _rsum(dfg)
        db_ref[...] = db

    full = pl.BlockSpec((t, LANES), lambda i: (0, 0))
    vec = pl.BlockSpec((1, LANES), lambda i: (0, 0))
    return _pcall(body, name="fg_bwd",
                  out_shape=[jax.ShapeDtypeStruct((t, LANES), BF16), jax.ShapeDtypeStruct((1, LANES), F32)],
                  grid=(1,), in_specs=[pl.BlockSpec((N_FOX, t), lambda i: (0, 0)), full, vec], out_specs=[full, vec],
                  scratch_shapes=[pltpu.VMEM((t, LANES), F32)])(dcs_rows, fg, b128)


def _head_spec(t, col0, div=1):
    return pl.BlockSpec((t, HEAD_DIM), lambda h: (0, col0 + h // div))


def _fox_scores(q, k, cq, ck, i, tq, end):
    s = lax.dot_general(q, k, NT_DIMS, preferred_element_type=F32) * ATT_SCALE + cq - ck
    row = lax.broadcasted_iota(jnp.int32, (tq, end), 0) + i * tq
    col = lax.broadcasted_iota(jnp.int32, (tq, end), 1)
    return jnp.where(row >= col, s, -jnp.inf)


def _fox_fwd(proj_a, cum, cumt):
    t = proj_a.shape[0]
    tq = _blk(t, 512)
    nq = t // tq

    def body(q_ref, k_ref, v_ref, cum_ref, cumt_ref, o_ref, lse_ref):
        h = pl.program_id(0)
        cq_all = _pick_lane(cum_ref[...], h)
        ck_all = cumt_ref[pl.ds(h, 1), :]

        @pl.when(h == 0)
        def _():
            lse_ref[...] = jnp.zeros_like(lse_ref)

        for i in range(nq):
            rows, end = slice(i * tq, (i + 1) * tq), (i + 1) * tq
            s = _fox_scores(q_ref[rows, :], k_ref[0:end, :], cq_all[rows, :], ck_all[:, 0:end], i, tq, end)
            m = jnp.max(s, axis=1, keepdims=True)
            p = jnp.exp(s - m)
            l = jnp.sum(p, axis=1, keepdims=True)
            o = jnp.dot(p.astype(BF16), v_ref[0:end, :], preferred_element_type=F32) / l
            o_ref[rows, :] = o.astype(BF16)
            _put_lane(lse_ref, rows, h, m + jnp.log(l))

    nh = FOX_W // HEAD_DIM
    stat = pl.BlockSpec((t, LANES), lambda h: (0, 0))
    return _pcall(body, name="fox_fwd",
                  out_shape=[jax.ShapeDtypeStruct((t, FOX_W), BF16), jax.ShapeDtypeStruct((t, LANES), F32)],
                  grid=(N_FOX,),
                  in_specs=[_head_spec(t, 0), _head_spec(t, nh), _head_spec(t, 2 * nh), stat,
                            pl.BlockSpec((N_FOX, t), lambda h: (0, 0))],
                  out_specs=[_head_spec(t, 0), stat])(proj_a, proj_a, proj_a, cum, cumt)


def _fox_bwd(proj_a, d_attn, cum, cumt, lse):
    t = proj_a.shape[0]
    tq = _blk(t, 512)
    nq = t // tq

    def body(q_ref, k_ref, v_ref, do_ref, cum_ref, cumt_ref, lse_ref,
             dq_ref, dk_ref, dv_ref, dcs_ref, dk_acc, dv_acc, dcs_acc):
        h = pl.program_id(0)
        cq_all = _pick_lane(cum_ref[...], h)
        ck_all = cumt_ref[pl.ds(h, 1), :]
        lse_all = _pick_lane(lse_ref[...], h)
        dk_acc[...] = jnp.zeros_like(dk_acc)
        dv_acc[...] = jnp.zeros_like(dv_acc)
        dcs_acc[...] = jnp.zeros_like(dcs_acc)
        for i in range(nq):
            rows, end = slice(i * tq, (i + 1) * tq), (i + 1) * tq
            q, k, v, do = q_ref[rows, :], k_ref[0:end, :], v_ref[0:end, :], do_ref[rows, :]
            s = _fox_scores(q, k, cq_all[rows, :], ck_all[:, 0:end], i, tq, end)
            p = jnp.exp(s - lse_all[rows, :])
            dp = lax.dot_general(do, v, NT_DIMS, preferred_element_type=F32)
            ds = p * (dp - jnp.sum(p * dp, axis=1, keepdims=True))
            dcs_acc[:, 0:end] += jnp.sum(ds, axis=0, keepdims=True)
            ds = ds.astype(BF16)
            dq_ref[rows, :] = (jnp.dot(ds, k, preferred_element_type=F32) * ATT_SCALE).astype(BF16)
            dk_acc[0:end, :] += lax.dot_general(ds, q, TN_DIMS, preferred_element_type=F32)
            dv_acc[0:end, :] += lax.dot_general(p.astype(BF16), do, TN_DIMS, preferred_element_type=F32)
        dk_ref[...] = (dk_acc[...] * ATT_SCALE).astype(BF16)
        dv_ref[...] = dv_acc[...].astype(BF16)
        dcs_ref[pl.ds(h, 1), :] = dcs_acc[...]

    nh = FOX_W // HEAD_DIM
    stat = pl.BlockSpec((t, LANES), lambda h: (0, 0))
    rows8 = pl.BlockSpec((N_FOX, t), lambda h: (0, 0))
    head = _head_spec(t, 0)
    wide = jax.ShapeDtypeStruct((t, FOX_W), BF16)
    return _pcall(body, name="fox_bwd",
                  out_shape=[wide, wide, wide, jax.ShapeDtypeStruct((N_FOX, t), F32)],
                  grid=(N_FOX,),
                  in_specs=[_head_spec(t, 0), _head_spec(t, nh), _head_spec(t, 2 * nh), head, stat, rows8, stat],
                  out_specs=[head, head, head, rows8],
                  scratch_shapes=[pltpu.VMEM((t, HEAD_DIM), F32), pltpu.VMEM((t, HEAD_DIM), F32),
                                  pltpu.VMEM((1, t), F32)],
                  )(proj_a, proj_a, proj_a, d_attn, cum, cumt, lse)


def _swa_scores(q, k, i, tq, start, end):
    s = lax.dot_general(q, k, NT_DIMS, preferred_element_type=F32) * ATT_SCALE
    row = lax.broadcasted_iota(jnp.int32, (tq, end - start), 0) + i * tq
    col = lax.broadcasted_iota(jnp.int32, (tq, end - start), 1) + start
    diff = row - col
    return jnp.where((diff >= 0) & (diff < WINDOW), s, -jnp.inf)


def _swa_blocks(t):
    tq = _blk(t, 256)
    return tq, [(i, max(0, i * tq - WINDOW), (i + 1) * tq) for i in range(t // tq)]


def _swa_fwd(proj_b, sinks128):
    t = proj_b.shape[0]
    tq, blocks = _swa_blocks(t)

    def body(q_ref, k_ref, v_ref, sink_ref, o_ref, lse_ref):
        h = pl.program_id(0)
        sink = _pick_lane(sink_ref[...], h)

        @pl.when(h == 0)
        def _():
            lse_ref[...] = jnp.zeros_like(lse_ref)

        for i, start, end in blocks:
            rows = slice(i * tq, end)
            s = _swa_scores(q_ref[rows, :], k_ref[start:end, :], i, tq, start, end)
            m = jnp.maximum(jnp.max(s, axis=1, keepdims=True), sink)
            p = jnp.exp(s - m)
            l = jnp.sum(p, axis=1, keepdims=True) + jnp.exp(sink - m)
            o = jnp.dot(p.astype(BF16), v_ref[start:end, :], preferred_element_type=F32) / l
            o_ref[rows, :] = o.astype(BF16)
            _put_lane(lse_ref, rows, h, m + jnp.log(l))

    stat = pl.BlockSpec((t, LANES), lambda h: (0, 0))
    return _pcall(body, name="swa_fwd",
                  out_shape=[jax.ShapeDtypeStruct((t, SWA_W), BF16), jax.ShapeDtypeStruct((t, LANES), F32)],
                  grid=(N_SWA,),
                  in_specs=[_head_spec(t, 0), _head_spec(t, N_SWA, GQA), _head_spec(t, N_SWA + N_KV, GQA),
                            pl.BlockSpec((1, LANES), lambda h: (0, 0))],
                  out_specs=[_head_spec(t, 0), stat])(proj_b, proj_b, proj_b, sinks128)


def _rope_bwd(d, cos, sin):
    return d * cos + pltpu.roll(d * sin, HEAD_DIM // 2, 1)


def _swa_bwd(proj_b, d_attn, lse, sinks128, cos, sin):
    t = proj_b.shape[0]
    tq, blocks = _swa_blocks(t)

    def body(q_ref, k_ref, v_ref, do_ref, lse_ref, sink_ref, cos_ref, sin_ref,
             dq_ref, dk_ref, dv_ref, dsink_ref, dk_acc, dv_acc):
        h = pl.program_id(0)
        sink = _pick_lane(sink_ref[...], h)
        lse_all = _pick_lane(lse_ref[...], h)

        @pl.when(h == 0)
        def _():
            dsink_ref[...] = jnp.zeros_like(dsink_ref)

        @pl.when(h % GQA == 0)
        def _():
            dk_acc[...] = jnp.zeros_like(dk_acc)
            dv_acc[...] = jnp.zeros_like(dv_acc)

        dsink = jnp.zeros((1, 1), F32)
        for i, start, end in blocks:
            rows = slice(i * tq, end)
            q, k, v, do = q_ref[rows, :], k_ref[start:end, :], v_ref[start:end, :], do_ref[rows, :]
            s = _swa_scores(q, k, i, tq, start, end)
            p = jnp.exp(s - lse_all[rows, :])
            dp = lax.dot_general(do, v, NT_DIMS, preferred_element_type=F32)
            delta = jnp.sum(p * dp, axis=1, keepdims=True)
            ds = (p * (dp - delta)).astype(BF16)
            dq = jnp.dot(ds, k, preferred_element_type=F32) * ATT_SCALE
            dq_ref[rows, :] = _rope_bwd(dq, cos_ref[rows, :], sin_ref[rows, :]).astype(BF16)
            dk_acc[start:end, :] += lax.dot_general(ds, q, TN_DIMS, preferred_element_type=F32)
            dv_acc[start:end, :] += lax.dot_general(p.astype(BF16), do, TN_DIMS, preferred_element_type=F32)
            dsink = dsink - jnp.sum(jnp.exp(sink - lse_all[rows, :]) * delta, axis=0, keepdims=True)
        old = dsink_ref[...]
        lane = lax.broadcasted_iota(jnp.int32, old.shape, 1)
        dsink_ref[...] = jnp.where(lane == h, dsink, old)

        @pl.when(h % GQA == GQA - 1)
        def _():
            dk_ref[...] = _rope_bwd(dk_acc[...] * ATT_SCALE, cos_ref[...], sin_ref[...]).astype(BF16)
            dv_ref[...] = dv_acc[...].astype(BF16)

    stat = pl.BlockSpec((t, LANES), lambda h: (0, 0))
    vec = pl.BlockSpec((1, LANES), lambda h: (0, 0))
    head = _head_spec(t, 0)
    kv_out = _head_spec(t, 0, GQA)
    return _pcall(body, name="swa_bwd",
                  out_shape=[jax.ShapeDtypeStruct((t, SWA_W), BF16), jax.ShapeDtypeStruct((t, KV_W), BF16),
                             jax.ShapeDtypeStruct((t, KV_W), BF16), jax.ShapeDtypeStruct((1, LANES), F32)],
                  grid=(N_SWA,),
                  in_specs=[head, _head_spec(t, N_SWA, GQA), _head_spec(t, N_SWA + N_KV, GQA),
                            _head_spec(t, N_FOX), stat, vec, stat, stat],
                  out_specs=[head, kv_out, kv_out, vec],
                  scratch_shapes=[pltpu.VMEM((t, HEAD_DIM), F32), pltpu.VMEM((t, HEAD_DIM), F32)],
                  )(proj_b, proj_b, proj_b, d_attn, lse, sinks128, cos, sin)


def _adamw(w, g, m, v):
    m = ADAM_B1 * m + (1.0 - ADAM_B1) * g
    v = ADAM_B2 * v + (1.0 - ADAM_B2) * (g * g)
    m_hat = m / (1.0 - ADAM_B1 ** ADAM_STEP)
    v_hat = v / (1.0 - ADAM_B2 ** ADAM_STEP)
    delta = -ADAM_LR * (m_hat / (jnp.sqrt(v_hat) + ADAM_EPS) + ADAM_WD * w)
    return delta, m, v


def _adam_pieces(w, m, v, own, land, idx, name):
    rows, cols = w.shape
    tr = 256 if rows % 256 == 0 else rows

    def body(idx_ref, own_ref, l1_ref, l2_ref, l3_ref, w_ref, m_ref, v_ref, g_ref, d_ref, mo_ref, vo_ref):
        g = own_ref[...].astype(F32) + l1_ref[...].astype(F32) + l2_ref[...].astype(F32) + l3_ref[...].astype(F32)
        g_ref[...] = g
        d_ref[...], mo_ref[...], vo_ref[...] = _adamw(w_ref[...], g, m_ref[...], v_ref[...])

    def piece(p):
        return pl.BlockSpec((None, tr, cols), lambda i, idx_ref: (idx_ref[p], i, 0))

    tile = pl.BlockSpec((tr, cols), lambda i, idx_ref: (i, 0))
    out = jax.ShapeDtypeStruct((rows, cols), F32)
    grid_spec = pltpu.PrefetchScalarGridSpec(
        num_scalar_prefetch=1, grid=(rows // tr,),
        in_specs=[piece(0), piece(1), piece(2), piece(3), tile, tile, tile], out_specs=[tile] * 4)
    return _pcall(body, name=name, out_shape=[out] * 4, grid_spec=grid_spec)(idx, own, land, land, land, w, m, v)


def _adam_mod(c_all, dmod_cols, w, m, v):
    rows, cols = w.shape
    tr = _blk(rows, 256)

    def body(c_ref, dm_ref, w_ref, m_ref, v_ref, g_ref, d_ref, mo_ref, vo_ref):
        cond = _silu(c_ref[...]).astype(BF16)
        g = lax.dot_general(cond, dm_ref[...].astype(BF16), TN_DIMS, preferred_element_type=F32)
        g_ref[...] = g
        d_ref[...], mo_ref[...], vo_ref[...] = _adamw(w_ref[...], g, m_ref[...], v_ref[...])

    tile = pl.BlockSpec((tr, cols), lambda i: (i, 0))
    out = jax.ShapeDtypeStruct((rows, cols), F32)
    return _pcall(body, name="adam_mod", out_shape=[out] * 4, grid=(rows // tr,),
                  in_specs=[pl.BlockSpec((N_DEV, tr), lambda i: (0, i)), pl.BlockSpec((N_DEV, cols), lambda i: (0, 0)),
                            tile, tile, tile],
                  out_specs=[tile] * 4)(c_all, dmod_cols, w, m, v)


def _adam_small(parts, w, m, v):
    nv = w.shape[1]

    def body(p_ref, w_ref, m_ref, v_ref, g_ref, d_ref, mo_ref, vo_ref):
        g = p_ref[0:1, :]
        for k in range(1, N_DEV):
            g = g + p_ref[k:k + 1, :]
        g_ref[...] = g
        d_ref[...], mo_ref[...], vo_ref[...] = _adamw(w_ref[...], g, m_ref[...], v_ref[...])

    vec = pl.BlockSpec((1, nv), lambda i: (0, 0))
    out = jax.ShapeDtypeStruct((1, nv), F32)
    return _pcall(body, name="adam_small", out_shape=[out] * 4, grid=(1,),
                  in_specs=[pl.BlockSpec((N_DEV, nv), lambda i: (0, 0)), vec, vec, vec],
                  out_specs=[vec] * 4)(parts, w, m, v)


def _pad_lanes(v, width=LANES):
    return jnp.pad(v, ((0, 0), (0, width - v.shape[1])))


def kernel(x, c, w_mod, b_mod, g_pre_mix, g_post_mix, w_in, b_forget, swa_sinks, w_out, g_pre_mlp, g_post_mlp, w_up, w_down, loss_target, m_w_mod, m_b_mod, m_g_pre_mix, m_g_post_mix, m_w_in, m_b_forget, m_swa_sinks, m_w_out, m_g_pre_mlp, m_g_post_mlp, m_w_up, m_w_down, v_w_mod, v_b_mod, v_g_pre_mix, v_g_post_mix, v_w_in, v_b_forget, v_swa_sinks, v_w_out, v_g_pre_mlp, v_g_post_mlp, v_w_up, v_w_down):
    ax, ay, ac = _position()
    me = 4 * ax + 2 * ay + ac
    x, target = x[0], loss_target[0]
    t, d = x.shape
    w_mod, w_in, w_out, w_up, w_down = w_mod[0], w_in[0], w_out[0], w_up[0], w_down[0]
    mod_w = w_mod.shape[1]
    in_w = w_in.shape[1]
    in_total = N_DEV * in_w
    shard_ff = w_up.shape[1]
    n_fox3 = 3 * FOX_W
    n_swa3 = SWA_W + 2 * KV_W
    assert in_total == n_fox3 + N_FOX + n_swa3 and d == FOX_W + SWA_W

    c_all = _all_gather([c], "gather_c")[0].reshape(N_DEV, d)
    b_part = lax.dynamic_slice(b_mod, (0, me * mod_w), (1, mod_w))
    mod_parts = _all_gather([_mod_part(c_all, w_mod, b_part)], "gather_mod")[0]
    mod = lax.dynamic_index_in_dim(mod_parts, me, axis=1, keepdims=False).reshape(1, N_DEV * mod_w)

    shards = [w_in.astype(BF16), w_out.astype(BF16), w_up.astype(BF16), w_down.astype(BF16)]
    shards, mod = lax.optimization_barrier((shards, mod))
    ag_send, ag_recv, ag_shard, ag_land, ag_token = _ag_start(shards)

    def gathered(i, after, name):
        shard, land = _ag_wait(ag_send[i], ag_recv[i], ag_shard[i], ag_land[i], after, "ag_wait_" + name)
        return _ag_forward(land, shard, "ag_fwd_" + name)

    sh_a, sc_a, gt_a, sh_m, sc_m, gt_m = [mod[:, i * d:(i + 1) * d] for i in range(6)]

    half = HEAD_DIM // 2
    inv_freq = 1.0 / (ROPE_THETA ** (jnp.arange(half, dtype=F32) * (2.0 / HEAD_DIM)))
    ang = jnp.arange(t).astype(F32)[:, None] * inv_freq[None, :]
    cos = jnp.concatenate([jnp.cos(ang), jnp.cos(ang)], axis=1)
    sin = jnp.concatenate([-jnp.sin(ang), jnp.sin(ang)], axis=1)

    b128 = _pad_lanes(b_forget)
    sinks128 = _pad_lanes(swa_sinks)

    h1 = _pre_attn(x, g_pre_mix + ag_token[0:1, 0:1], sc_a, sh_a)
    w_in_g = gathered(0, h1, "in")
    o_fg, o_sq, o_sv = n_fox3, n_fox3 + N_FOX, n_fox3 + N_FOX + SWA_W + KV_W

    def cols(lo, hi):
        parts = []
        for j in range(lo // in_w, (hi - 1) // in_w + 1):
            parts.append(w_in_g[j, :, max(lo - j * in_w, 0):min(hi - j * in_w, in_w)])
        return parts

    w_in_r = jnp.concatenate(cols(0, o_fg) + cols(o_sq, in_total) + cols(o_fg, o_sq)
                             + [jnp.zeros((d, FG_PAD - N_FOX), BF16)], axis=1)
    proj_a = _matmul(h1, w_in_r, name="proj_a", n_cols=n_fox3, n_off=0)
    proj_b = _matmul(h1, w_in_r, name="proj_b", n_cols=n_swa3, n_off=n_fox3, tn=512, row_extras=(cos, sin),
                     epilogue=lambda acc, j, cs, sn: (_rope_cols(acc, j, cs, sn, SWA_W + KV_W),))
    fg = _matmul(h1, w_in_r, name="proj_fg", n_cols=FG_PAD, n_off=n_fox3 + n_swa3, tn=FG_PAD,
                 out_dtypes=(F32,))[:, 0:LANES]
    cum, cumt = _cum_fwd(fg, b128)
    fox_o, fox_lse = _fox_fwd(proj_a, cum, cumt)
    swa_o, swa_lse = _swa_fwd(proj_b, sinks128)
    w_out_full = gathered(1, swa_o, "out").reshape(d, d)
    attn = jnp.concatenate([fox_o, swa_o], axis=1)
    mix = _matmul(attn, w_out_full, name="out_proj", out_dtypes=(F32,))
    x2, h2 = _post_mix(x, mix, gt_a, g_post_mix, g_pre_mlp, sc_m, sh_m)
    w_up_g = gathered(2, h2, "up")
    u, act = _matmul(h2, w_up_g, name="mlp_up", b_sharded=True, out_dtypes=(BF16, BF16),
                     epilogue=lambda acc, j: (acc, jnp.square(jnp.maximum(acc, 0.0))))
    w_down_full = gathered(3, act, "down").reshape(N_DEV * shard_ff, d)
    y = _matmul(act, w_down_full, name="mlp_down", out_dtypes=(F32,))

    core = jnp.reshape(ac, (1,)).astype(jnp.int32)

    def reduce_start(full, name):
        from_sibling = _rs_sibling([full], "rs_sibling_" + name)[0]
        return _rs_start(_chip_sum(full, from_sibling, core, "chip_sum_" + name), "rs_start_" + name)

    idx = jnp.stack([2 * ax + ay, 2 * (1 - ax) + ay, 2 * ax + (1 - ay), 2 * (1 - ax) + (1 - ay)]).astype(jnp.int32)

    def reduce_finish(started, after, w, m, v, name):
        send, recv, src, land, _ = started
        own, landed = _rs_wait(send, recv, src, land, after, "rs_wait_" + name)
        return _adam_pieces(w, m[0], v[0], own, landed, idx, "adam_" + name)

    dy, dout, dgt_m, dg3, loss_vec = _final(y, x2, target, gt_m, g_post_mlp)
    du = _matmul(dy, w_down_full, name="d_act", tb=True, tile_extras=(u,),
                 epilogue=lambda acc, j, uu: (acc * (2.0 * jnp.maximum(uu.astype(F32), 0.0)),))
    dw_down = _matmul(act, dy, name="dw_down", ta=True)
    rs_down = reduce_start(dw_down.reshape(N_DEV, shard_ff, d), "down")
    dh2 = _matmul(du, w_up_g, name="d_h2", tb=True, b_sharded=True, out_dtypes=(F32,))
    per = shard_ff // _blk(shard_ff, 1024)
    dw_up = _matmul(h2, du, name="dw_up", ta=True, tn=_blk(shard_ff, 1024), out_shape=(N_DEV, d, shard_ff),
                    out_map=lambda tm, tn: pl.BlockSpec((None, tm, tn), lambda i, j, kk: (j // per, i, j % per)))
    rs_up = reduce_start(dw_up, "up")
    dmix, dx2, dsh_m, dsc_m, dg2, dgt_a, dg1 = _mid_bwd(
        dh2, dout, x2, mix, g_pre_mlp + (rs_down[4][0:1, 0:1] + rs_up[4][0:1, 0:1]), sc_m, gt_a, g_post_mix)
    d_attn = _matmul(dmix, w_out_full, name="d_attn", tb=True)
    dw_out = _matmul(attn, dmix, name="dw_out", ta=True)
    rs_out = reduce_start(dw_out.reshape(N_DEV, d // N_DEV, d), "out")
    dqf, dkf, dvf, dcs = _fox_bwd(proj_a, d_attn, cum, cumt, fox_lse)
    dsq, dsk, dsv, dsinks = _swa_bwd(proj_b, d_attn, swa_lse, sinks128, cos, sin)
    dfg, db_forget = _fg_bwd(dcs, fg, b128 + rs_out[4][0:1, 0:1])
    dproj = jnp.concatenate([dqf, dkf, dvf, dsq, dsk, dsv, _pad_lanes(dfg, FG_PAD)], axis=1)
    dh1 = _matmul(dproj, w_in_r, name="d_h1", tb=True, out_dtypes=(F32,))
    dw_in_r = _matmul(h1, dproj, name="dw_in", ta=True)
    dw_in_full = jnp.concatenate([dw_in_r[:, 0:n_fox3], dw_in_r[:, n_fox3 + n_swa3:n_fox3 + n_swa3 + N_FOX],
                                  dw_in_r[:, n_fox3:n_fox3 + n_swa3]], axis=1)
    rs_in = reduce_start(dw_in_full.reshape(d, N_DEV, in_w).transpose(1, 0, 2), "in")
    grad_x, dsh_a, dsc_a, dg0 = _x_bwd(dh1, dx2, x, g_pre_mix + rs_in[4][0:1, 0:1], sc_a)

    g_w_down, d_w_down, nm_w_down, nv_w_down = reduce_finish(rs_down, grad_x, w_down, m_w_down, v_w_down, "w_down")
    g_w_up, d_w_up, nm_w_up, nv_w_up = reduce_finish(rs_up, d_w_down, w_up, m_w_up, v_w_up, "w_up")
    g_w_out, d_w_out, nm_w_out, nv_w_out = reduce_finish(rs_out, d_w_up, w_out, m_w_out, v_w_out, "w_out")

    small = jnp.concatenate([dsh_a, dsc_a, dgt_a, dsh_m, dsc_m, dgt_m, dg0, dg1, dg2, dg3, db_forget, dsinks,
                             loss_vec[:, 0:LANES]], axis=1)
    small, d_w_out = lax.optimization_barrier((small, d_w_out))
    small_all = _all_gather([small], "gather_small")[0].reshape(N_DEV, small.shape[1])
    pack = lambda bm, g0_, g1_, g2_, g3_, bf_, sk_: jnp.concatenate(
        [bm, g0_, g1_, g2_, g3_, _pad_lanes(bf_), _pad_lanes(sk_), jnp.zeros((1, LANES), F32)], axis=1)
    p_small = pack(b_mod, g_pre_mix, g_post_mix, g_pre_mlp, g_post_mlp, b_forget, swa_sinks)
    m_small = pack(m_b_mod, m_g_pre_mix, m_g_post_mix, m_g_pre_mlp, m_g_post_mlp, m_b_forget, m_swa_sinks)
    v_small = pack(v_b_mod, v_g_pre_mix, v_g_post_mix, v_g_pre_mlp, v_g_post_mlp, v_b_forget, v_swa_sinks)
    small_out = _adam_small(small_all, p_small, m_small, v_small)

    n_mod = 6 * d

    def unpack(vec):
        o = n_mod
        return (vec[:, 0:n_mod], vec[:, o:o + d], vec[:, o + d:o + 2 * d], vec[:, o + 2 * d:o + 3 * d],
                vec[:, o + 3 * d:o + 4 * d], vec[:, o + 4 * d:o + 4 * d + N_FOX],
                vec[:, o + 4 * d + LANES:o + 4 * d + LANES + N_SWA])

    loss = small_out[0][0, n_mod + 4 * d + 2 * LANES]
    g_small, d_small, nm_small, nv_small = [unpack(vec) for vec in small_out]

    dmod_cols = lax.dynamic_slice(small_all, (0, me * mod_w), (N_DEV, mod_w))
    g_w_mod, d_w_mod, nm_w_mod, nv_w_mod = _adam_mod(c_all, dmod_cols, w_mod, m_w_mod[0], v_w_mod[0])
    g_w_in, d_w_in, nm_w_in, nv_w_in = reduce_finish(rs_in, d_w_mod, w_in, m_w_in, v_w_in, "w_in")

    def assemble(w_mod_, small_, w_in_, w_out_, w_up_, w_down_):
        b_mod_, g0_, g1_, g2_, g3_, bf_, sk_ = small_
        return [w_mod_[None], b_mod_, g0_, g1_, w_in_[None], bf_, sk_, w_out_[None], g2_, g3_, w_up_[None], w_down_[None]]

    outs = [loss, grad_x[None]]
    outs += assemble(g_w_mod, g_small, g_w_in, g_w_out, g_w_up, g_w_down)
    outs += assemble(d_w_mod, d_small, d_w_in, d_w_out, d_w_up, d_w_down)
    outs += assemble(nm_w_mod, nm_small, nm_w_in, nm_w_out, nm_w_up, nm_w_down)
    outs += assemble(nv_w_mod, nv_small, nv_w_in, nv_w_out, nv_w_up, nv_w_down)
    return tuple(outs)
```

```python
import functools

import jax
import jax.numpy as jnp
from jax import lax
from jax.experimental import pallas as pl
from jax.experimental.pallas import tpu as pltpu

F32 = jnp.float32
BF16 = jnp.bfloat16
MESH = pl.DeviceIdType.MESH

N_DEV = 8
N_CHIP = 4
LANES = 128
HEAD_DIM = 128
N_FOX = 8
N_SWA = 8
N_KV = 2
GQA = N_SWA // N_KV
WINDOW = 128
FOX_W = N_FOX * HEAD_DIM
SWA_W = N_SWA * HEAD_DIM
KV_W = N_KV * HEAD_DIM
ROPE_THETA = 10000.0
NORM_EPS = 1e-6
ATT_SCALE = HEAD_DIM ** -0.5
FG_PAD = 512

ADAM_LR = 0.001
ADAM_B1 = 0.9
ADAM_B2 = 0.999
ADAM_EPS = 1e-08
ADAM_WD = 0.01
ADAM_STEP = 10

VMEM_LIMIT = 56 * 1024 * 1024

NT_DIMS = (((1,), (1,)), ((), ()))
TN_DIMS = (((0,), (0,)), ((), ()))
NN_DIMS = (((1,), (0,)), ((), ()))


def _pcall(body, *, name, out_shape, grid=(), in_specs=None, out_specs=None, scratch_shapes=(), grid_spec=None):
    params = pltpu.CompilerParams(vmem_limit_bytes=VMEM_LIMIT)
    if grid_spec is not None:
        return pl.pallas_call(body, name=name, out_shape=out_shape, grid_spec=grid_spec, compiler_params=params)
    return pl.pallas_call(body, name=name, out_shape=out_shape, grid=grid, in_specs=in_specs, out_specs=out_specs,
                          scratch_shapes=scratch_shapes, compiler_params=params)


def _blk(n, pref):
    if n <= pref:
        return n
    b = (pref // LANES) * LANES
    while n % b:
        b -= LANES
    return b


def _position():
    return lax.axis_index("x"), lax.axis_index("y"), lax.axis_index("c")


ANY = pl.BlockSpec(memory_space=pl.ANY)


def _all_gather(arrs, name):
    n = len(arrs)

    def body(*refs):
        ins, outs = refs[:n], refs[n:2 * n]
        send_sems, recv_sems, local_sems = refs[2 * n:]
        x, y, c = _position()
        me, sibling = (x, y, c), (x, y, 1 - c)
        chips = [(1 - x, y), (x, 1 - y), (1 - x, 1 - y)]

        def slot(p):
            return 4 * p[0] + 2 * p[1] + p[2]

        def copy(a, k, block, to, src=None):
            dst = outs[a].at[slot(block)]
            return pltpu.make_async_remote_copy(
                src_ref=dst if src is None else src, dst_ref=dst,
                send_sem=send_sems.at[7 * a + k], recv_sem=recv_sems.at[7 * a + k],
                device_id=to, device_id_type=MESH)

        mine = [pltpu.make_async_copy(ins[a], outs[a].at[slot(me)], local_sems.at[a]) for a in range(n)]
        for cp in mine:
            cp.start()
        first = []
        for a in range(n):
            first.append(copy(a, 0, me, sibling, src=ins[a]))
            first += [copy(a, 1 + j, me, (*chip, c), src=ins[a]) for j, chip in enumerate(chips)]
        for cp in first:
            cp.start()
        passed = []
        for a in range(n):
            for j, chip in enumerate(chips):
                copy(a, 1 + j, (*chip, c), me).wait_recv()
                cp = copy(a, 4 + j, (*chip, c), sibling)
                cp.start()
                passed.append(cp)
        for a in range(n):
            copy(a, 0, sibling, me).wait_recv()
            for j, chip in enumerate(chips):
                copy(a, 4 + j, (*chip, 1 - c), me).wait_recv()
        for cp in first + passed:
            cp.wait_send()
        for cp in mine:
            cp.wait()

    return _pcall(
        body, name=name,
        out_shape=[jax.ShapeDtypeStruct((N_DEV,) + a.shape, a.dtype) for a in arrs],
        in_specs=[ANY] * n, out_specs=[ANY] * n,
        scratch_shapes=[pltpu.SemaphoreType.DMA((7 * n,)), pltpu.SemaphoreType.DMA((7 * n,)),
                        pltpu.SemaphoreType.DMA((n,))],
    )(*arrs)


HBM = pl.BlockSpec(memory_space=pltpu.HBM)
SEM = pl.BlockSpec(memory_space=pltpu.SEMAPHORE)
EFFECT = pltpu.SideEffectType.DATAFLOW_SIDE_EFFECTING


def _hbm(a):
    return pltpu.with_memory_space_constraint(a, pltpu.HBM)


def _gather_peers():
    x, y, c = _position()
    return [(x, y, 1 - c), (1 - x, y, c), (x, 1 - y, c), (1 - x, 1 - y, c)]


def _ag_start(shards):
    n = len(shards)
    lands = [_hbm(lax.empty((4,) + s.shape, s.dtype)) for s in shards]

    def body(*refs):
        srcs, land, send, recv = refs[:n], refs[n:2 * n], refs[2 * n:3 * n], refs[3 * n:4 * n]
        token = refs[6 * n]
        for a in range(n):
            for k, to in enumerate(_gather_peers()):
                pltpu.make_async_remote_copy(
                    src_ref=srcs[a], dst_ref=land[a].at[k], send_sem=send[a].at[k],
                    recv_sem=recv[a].at[k], device_id=to, device_id_type=MESH).start()
        token[...] = jnp.zeros_like(token)

    sems = [pltpu.SemaphoreType.DMA((4,))] * (2 * n)
    out = pl.pallas_call(
        body, name="ag_start",
        out_shape=sems + [pltpu.HBM(s.shape, s.dtype) for s in shards] + [pltpu.HBM(l.shape, l.dtype) for l in lands]
        + [jax.ShapeDtypeStruct((8, LANES), F32)],
        in_specs=[HBM] * (2 * n), out_specs=[SEM] * (2 * n) + [HBM] * (2 * n) + [pl.BlockSpec(memory_space=pltpu.VMEM)],
        input_output_aliases={**{a: 2 * n + a for a in range(n)}, **{n + a: 3 * n + a for a in range(n)}},
        compiler_params=pltpu.CompilerParams(has_side_effects=EFFECT),
    )(*[_hbm(s) for s in shards], *lands)
    return out[:n], out[n:2 * n], out[2 * n:3 * n], out[3 * n:4 * n], out[4 * n]


def _ag_wait(send, recv, shard_thru, land_thru, after, name):
    def body(v_ref, land_ref, send_sem, recv_sem, after_ref, v_dead, got_ref):
        for k, to in enumerate(_gather_peers()):
            cp = pltpu.make_async_remote_copy(
                src_ref=v_ref, dst_ref=land_ref.at[0], send_sem=send_sem.at[k], recv_sem=recv_sem.at[k],
                device_id=to, device_id_type=MESH)
            cp.wait_send()
            cp.wait_recv()

    return pl.pallas_call(
        body, name=name,
        out_shape=(pltpu.HBM(shard_thru.shape, shard_thru.dtype), pltpu.HBM(land_thru.shape, land_thru.dtype)),
        in_specs=(HBM, HBM, SEM, SEM, ANY), out_specs=(HBM, HBM), input_output_aliases={0: 0, 1: 1},
        compiler_params=pltpu.CompilerParams(has_side_effects=EFFECT),
    )(shard_thru, land_thru, send, recv, after)


def _ag_forward(landed, shard, name):
    def body(landed_ref, shard_ref, out_ref, send_sems, recv_sems, local_sems):
        x, y, c = _position()
        slots = [4 * x + 2 * y + (1 - c)] + [4 * px + 2 * py + c for px, py in [(1 - x, y), (x, 1 - y), (1 - x, 1 - y)]]
        copies = [pltpu.make_async_copy(shard_ref, out_ref.at[4 * x + 2 * y + c], local_sems.at[4])]
        for k in range(4):
            copies.append(pltpu.make_async_copy(landed_ref.at[k], out_ref.at[slots[k]], local_sems.at[k]))
        for k in range(1, 4):
            copies.append(pltpu.make_async_remote_copy(
                src_ref=landed_ref.at[k], dst_ref=out_ref.at[slots[k]], send_sem=send_sems.at[k - 1],
                recv_sem=recv_sems.at[k - 1], device_id=(x, y, 1 - c), device_id_type=MESH))
        for cp in copies:
            cp.start()
        for cp in copies:
            cp.wait()

    return _pcall(
        body, name=name, out_shape=jax.ShapeDtypeStruct((N_DEV,) + shard.shape, shard.dtype),
        in_specs=[ANY, ANY], out_specs=ANY,
        scratch_shapes=[pltpu.SemaphoreType.DMA((3,)), pltpu.SemaphoreType.DMA((3,)), pltpu.SemaphoreType.DMA((5,))],
    )(landed, shard)


def _rs_peers():
    x, y, c = _position()
    return [(1 - x, y, c), (x, 1 - y, c), (1 - x, 1 - y, c)]


def _rs_start(chip_sums, name):
    land = _hbm(lax.empty(chip_sums.shape, chip_sums.dtype))

    def body(src, land_ref, send, recv, src_thru, land_thru, token):
        x, y, c = _position()
        for j, (px, py, pc) in enumerate(_rs_peers()):
            pltpu.make_async_remote_copy(
                src_ref=src.at[2 * px + py], dst_ref=land_ref.at[2 * x + y], send_sem=send.at[j], recv_sem=recv.at[j],
                device_id=(px, py, pc), device_id_type=MESH).start()
        token[...] = jnp.zeros_like(token)

    return pl.pallas_call(
        body, name=name,
        out_shape=[pltpu.SemaphoreType.DMA((3,)), pltpu.SemaphoreType.DMA((3,)),
                   pltpu.HBM(chip_sums.shape, chip_sums.dtype), pltpu.HBM(land.shape, land.dtype),
                   jax.ShapeDtypeStruct((8, LANES), F32)],
        in_specs=[HBM, HBM], out_specs=[SEM, SEM, HBM, HBM, pl.BlockSpec(memory_space=pltpu.VMEM)],
        input_output_aliases={0: 2, 1: 3},
        compiler_params=pltpu.CompilerParams(has_side_effects=EFFECT),
    )(_hbm(chip_sums), land)


def _rs_wait(send, recv, src_thru, land_thru, after, name):
    def body(src, land_ref, send_sem, recv_sem, after_ref, src_out, land_out):
        for j, to in enumerate(_rs_peers()):
            cp = pltpu.make_async_remote_copy(
                src_ref=src.at[0], dst_ref=land_ref.at[0], send_sem=send_sem.at[j], recv_sem=recv_sem.at[j],
                device_id=to, device_id_type=MESH)
            cp.wait_send()
            cp.wait_recv()

    return pl.pallas_call(
        body, name=name,
        out_shape=(pltpu.HBM(src_thru.shape, src_thru.dtype), pltpu.HBM(land_thru.shape, land_thru.dtype)),
        in_specs=(HBM, HBM, SEM, SEM, ANY), out_specs=(HBM, HBM), input_output_aliases={0: 0, 1: 1},
        compiler_params=pltpu.CompilerParams(has_side_effects=EFFECT),
    )(src_thru, land_thru, send, recv, after)


def _rs_sibling(arrs, name):
    n = len(arrs)

    def body(*refs):
        ins, outs = refs[:n], refs[n:2 * n]
        send_sems, recv_sems = refs[2 * n:]
        x, y, c = _position()
        copies = []
        for a in range(n):
            for k in range(N_CHIP):
                cp = pltpu.make_async_remote_copy(
                    src_ref=ins[a].at[2 * k + (1 - c)], dst_ref=outs[a].at[k],
                    send_sem=send_sems.at[N_CHIP * a + k], recv_sem=recv_sems.at[N_CHIP * a + k],
                    device_id=(x, y, 1 - c), device_id_type=MESH)
                cp.start()
                copies.append(cp)
        for cp in copies:
            cp.wait()

    return _pcall(
        body, name=name,
        out_shape=[jax.ShapeDtypeStruct((N_CHIP,) + a.shape[1:], a.dtype) for a in arrs],
        in_specs=[ANY] * n, out_specs=[ANY] * n,
        scratch_shapes=[pltpu.SemaphoreType.DMA((N_CHIP * n,)), pltpu.SemaphoreType.DMA((N_CHIP * n,))],
    )(*arrs)


def _chip_sum(full, recv, core, name):
    _, rows, cols = full.shape
    tr = 256 if rows % 256 == 0 else rows

    def body(core_ref, a_ref, b_ref, o_ref):
        o_ref[...] = (a_ref[...].astype(F32) + b_ref[...].astype(F32)).astype(o_ref.dtype)

    grid_spec = pltpu.PrefetchScalarGridSpec(
        num_scalar_prefetch=1, grid=(N_CHIP, rows // tr),
        in_specs=[pl.BlockSpec((None, tr, cols), lambda k, i, core_ref: (2 * k + core_ref[0], i, 0)),
                  pl.BlockSpec((None, tr, cols), lambda k, i, core_ref: (k, i, 0))],
        out_specs=pl.BlockSpec((None, tr, cols), lambda k, i, core_ref: (k, i, 0)))
    return _pcall(body, name=name, out_shape=jax.ShapeDtypeStruct((N_CHIP, rows, cols), full.dtype),
                  grid_spec=grid_spec)(core, full, recv)


def _matmul(a, b, *, name, ta=False, tb=False, tm=1024, tn=1024, tk=2048, out_dtypes=(BF16,), epilogue=None,
            row_extras=(), tile_extras=(), out_shape=None, out_map=None, b_sharded=False, n_cols=None, n_off=0):
    m, k = (a.shape[1], a.shape[0]) if ta else a.shape
    if b_sharded:
        shard_c = b.shape[2]
        n, kb = (b.shape[1], N_DEV * shard_c) if tb else (N_DEV * shard_c, b.shape[1])
        tn, tk = (tn, min(tk, shard_c)) if tb else (min(tn, shard_c), tk)
    else:
        n, kb = b.shape if tb else (b.shape[1], b.shape[0])
    assert kb == k, (name, kb, k)
    if n_cols is not None:
        n = n_cols
    tm, tn, tk = _blk(m, tm), _blk(n, tn), _blk(k, tk)
    assert n_off % tn == 0
    nk = k // tk
    dims = (((0 if ta else 1,), (1 if tb else 0,)), ((), ()))
    n_row, n_tile, n_out = len(row_extras), len(tile_extras), len(out_dtypes)

    def body(*refs):
        a_ref, b_ref = refs[:2]
        extras = refs[2:2 + n_row + n_tile]
        outs = refs[2 + n_row + n_tile:2 + n_row + n_tile + n_out]
        acc_ref = refs[-1]
        jj, kk = pl.program_id(1), pl.program_id(2)
        part = lax.dot_general(a_ref[...].astype(BF16), b_ref[...].astype(BF16), dims, preferred_element_type=F32)

        def finish(acc):
            res = (acc,) if epilogue is None else epilogue(acc, jj, *[e[...] for e in extras])
            for o_ref, r in zip(outs, res):
                o_ref[...] = r.astype(o_ref.dtype)

        if nk == 1:
            finish(part)
        else:
            @pl.when(kk == 0)
            def _():
                acc_ref[...] = part

            @pl.when(kk > 0)
            def _():
                acc_ref[...] += part

            @pl.when(kk == nk - 1)
            def _():
                finish(acc_ref[...])

    a_spec = pl.BlockSpec((tk, tm), lambda i, j, kk: (kk, i)) if ta else pl.BlockSpec((tm, tk), lambda i, j, kk: (i, kk))
    if b_sharded and tb:
        per = shard_c // tk
        b_spec = pl.BlockSpec((None, tn, tk), lambda i, j, kk: (kk // per, j, kk % per))
    elif b_sharded:
        per = shard_c // tn
        b_spec = pl.BlockSpec((None, tk, tn), lambda i, j, kk: (j // per, kk, j % per))
    elif tb:
        b_spec = pl.BlockSpec((tn, tk), lambda i, j, kk: (j, kk))
    else:
        b_spec = pl.BlockSpec((tk, tn), lambda i, j, kk: (kk, j + n_off // tn))
    in_specs = [a_spec, b_spec]
    in_specs += [pl.BlockSpec((tm, LANES), lambda i, j, kk: (i, 0)) for _ in row_extras]
    in_specs += [pl.BlockSpec((tm, tn), lambda i, j, kk: (i, j)) for _ in tile_extras]
    if out_map is None:
        out_specs = [pl.BlockSpec((tm, tn), lambda i, j, kk: (i, j)) for _ in out_dtypes]
        shapes = [jax.ShapeDtypeStruct((m, n), dt) for dt in out_dtypes]
    else:
        out_specs = [out_map(tm, tn)]
        shapes = [jax.ShapeDtypeStruct(out_shape, out_dtypes[0])]
    acc_shape = (tm, tn) if nk > 1 else (8, LANES)
    res = _pcall(body, name=name, out_shape=shapes, grid=(m // tm, n // tn, nk), in_specs=in_specs,
                 out_specs=out_specs, scratch_shapes=[pltpu.VMEM(acc_shape, F32)])(a, b, *row_extras, *tile_extras)
    return res[0] if n_out == 1 else res


def _rope_cols(acc, j, cos, sin, n_rope):
    width = acc.shape[1]
    parts = []
    for g in range(width // HEAD_DIM):
        xg = acc[:, g * HEAD_DIM:(g + 1) * HEAD_DIM]
        roped = xg * cos + pltpu.roll(xg, HEAD_DIM // 2, 1) * sin
        parts.append(jnp.where(j * width + g * HEAD_DIM < n_rope, roped, xg))
    return jnp.concatenate(parts, axis=1) if len(parts) > 1 else parts[0]


def _silu(v):
    return v / (1.0 + jnp.exp(-v))


def _mod_part(c_all, w_mod, b_part):
    d, w = w_mod.shape
    tk = _blk(d, 512)

    def body(c_ref, w_ref, b_ref, o_ref):
        kk = pl.program_id(0)
        cond = _silu(c_ref[...]).astype(BF16)
        part = jnp.dot(cond, w_ref[...].astype(BF16), preferred_element_type=F32)

        @pl.when(kk == 0)
        def _():
            o_ref[...] = part + b_ref[...]

        @pl.when(kk > 0)
        def _():
            o_ref[...] += part

    return _pcall(body, name="mod_part", out_shape=jax.ShapeDtypeStruct((N_DEV, w), F32), grid=(d // tk,),
                  in_specs=[pl.BlockSpec((N_DEV, tk), lambda kk: (0, kk)), pl.BlockSpec((tk, w), lambda kk: (kk, 0)),
                            pl.BlockSpec((1, w), lambda kk: (0, 0))],
                  out_specs=pl.BlockSpec((N_DEV, w), lambda kk: (0, 0)))(c_all, w_mod, b_part)


def _row_call(body, name, t, d, tiled_in, vec_in, tiled_out_dtypes, n_vec_out, tr=256):
    tr = _blk(t, tr)
    tile = pl.BlockSpec((tr, d), lambda i: (i, 0))
    vec = pl.BlockSpec((1, d), lambda i: (0, 0))
    out_shape = [jax.ShapeDtypeStruct((t, d), dt) for dt in tiled_out_dtypes]
    out_shape += [jax.ShapeDtypeStruct((1, d), F32)] * n_vec_out
    return _pcall(body, name=name, out_shape=out_shape, grid=(t // tr,),
                  in_specs=[tile] * len(tiled_in) + [vec] * len(vec_in),
                  out_specs=[tile] * len(tiled_out_dtypes) + [vec] * n_vec_out)(*tiled_in, *vec_in)


def _accumulate(ref, val):
    @pl.when(pl.program_id(0) == 0)
    def _():
        ref[...] = val

    @pl.when(pl.program_id(0) > 0)
    def _():
        ref[...] += val


def _rsum(v):
    return jnp.sum(v, axis=0, keepdims=True)


def _rms(v):
    return lax.rsqrt(jnp.mean(v * v, axis=-1, keepdims=True) + NORM_EPS)


def _rms_bwd(vhat, r, dvhat):
    return r * (dvhat - vhat * jnp.mean(dvhat * vhat, axis=-1, keepdims=True))


def _pre_attn(x, g0, sc_a, sh_a):
    def body(x_ref, g_ref, sc_ref, sh_ref, h_ref):
        xv = x_ref[...]
        h_ref[...] = (xv * _rms(xv) * g_ref[...] * (1.0 + sc_ref[...]) + sh_ref[...]).astype(BF16)

    t, d = x.shape
    return _row_call(body, "pre_attn", t, d, [x], [g0, sc_a, sh_a], [BF16], 0)[0]


def _post_mix(x, mix, gt_a, g1, g2, sc_m, sh_m):
    def body(x_ref, mix_ref, gt_ref, g1_ref, g2_ref, sc_ref, sh_ref, x2_ref, h2_ref):
        mv = mix_ref[...]
        x2 = x_ref[...] + gt_ref[...] * (mv * _rms(mv) * g1_ref[...])
        x2_ref[...] = x2
        h2_ref[...] = (x2 * _rms(x2) * g2_ref[...] * (1.0 + sc_ref[...]) + sh_ref[...]).astype(BF16)

    t, d = x.shape
    return _row_call(body, "post_mix", t, d, [x, mix], [gt_a, g1, g2, sc_m, sh_m], [F32, BF16], 0)


def _final(y, x2, target, gt_m, g3):
    t, d = y.shape

    def body(y_ref, x2_ref, tg_ref, gt_ref, g3_ref, dy_ref, dout_ref, dgt_ref, dg3_ref, loss_ref):
        yv = y_ref[...]
        r = _rms(yv)
        yhat = yv * r
        n3 = yhat * g3_ref[...]
        err = x2_ref[...] + gt_ref[...] * n3 - tg_ref[...]
        _accumulate(loss_ref, jnp.zeros((1, d), F32) + 0.5 * jnp.sum(err * err) / d)
        dout = err * (1.0 / d)
        dout_ref[...] = dout
        _accumulate(dgt_ref, _rsum(dout * n3))
        dn3 = dout * gt_ref[...]
        _accumulate(dg3_ref, _rsum(dn3 * yhat))
        dy_ref[...] = _rms_bwd(yhat, r, dn3 * g3_ref[...]).astype(BF16)

    return _row_call(body, "final", t, d, [y, x2, target], [gt_m, g3], [BF16, F32], 3)


def _mid_bwd(dh2, dout, x2, mix, g2, sc_m, gt_a, g1):
    t, d = x2.shape

    def body(dh2_ref, dout_ref, x2_ref, mix_ref, g2_ref, sc_ref, gt_ref, g1_ref,
             dmix_ref, dx2_ref, dsh_ref, dsc_ref, dg2_ref, dgt_ref, dg1_ref):
        dh2v = dh2_ref[...]
        x2v = x2_ref[...]
        r2 = _rms(x2v)
        x2hat = x2v * r2
        _accumulate(dsh_ref, _rsum(dh2v))
        _accumulate(dsc_ref, _rsum(dh2v * (x2hat * g2_ref[...])))
        dn2 = dh2v * (1.0 + sc_ref[...])
        _accumulate(dg2_ref, _rsum(dn2 * x2hat))
        dx2 = dout_ref[...] + _rms_bwd(x2hat, r2, dn2 * g2_ref[...])
        dx2_ref[...] = dx2
        mv = mix_ref[...]
        r1 = _rms(mv)
        mhat = mv * r1
        _accumulate(dgt_ref, _rsum(dx2 * (mhat * g1_ref[...])))
        dn1 = dx2 * gt_ref[...]
        _accumulate(dg1_ref, _rsum(dn1 * mhat))
        dmix_ref[...] = _rms_bwd(mhat, r1, dn1 * g1_ref[...]).astype(BF16)

    return _row_call(body, "mid_bwd", t, d, [dh2, dout, x2, mix], [g2, sc_m, gt_a, g1], [BF16, F32], 5)


def _x_bwd(dh1, dx2, x, g0, sc_a):
    t, d = x.shape

    def body(dh1_ref, dx2_ref, x_ref, g0_ref, sc_ref, dx_ref, dsh_ref, dsc_ref, dg0_ref):
        dh1v = dh1_ref[...]
        xv = x_ref[...]
        r0 = _rms(xv)
        xhat = xv * r0
        _accumulate(dsh_ref, _rsum(dh1v))
        _accumulate(dsc_ref, _rsum(dh1v * (xhat * g0_ref[...])))
        dn0 = dh1v * (1.0 + sc_ref[...])
        _accumulate(dg0_ref, _rsum(dn0 * xhat))
        dx_ref[...] = dx2_ref[...] + _rms_bwd(xhat, r0, dn0 * g0_ref[...])

    return _row_call(body, "x_bwd", t, d, [dh1, dx2, x], [g0, sc_a], [F32], 3)


def _pick_lane(block, h):
    lane = lax.broadcasted_iota(jnp.int32, block.shape, 1)
    return jnp.sum(jnp.where(lane == h, block, 0.0), axis=1, keepdims=True)


def _put_lane(ref, rows, h, col):
    old = ref[rows, :]
    lane = lax.broadcasted_iota(jnp.int32, old.shape, 1)
    ref[rows, :] = jnp.where(lane == h, col, old)


def _tri(n, lower):
    r = lax.broadcasted_iota(jnp.int32, (n, n), 0)
    c = lax.broadcasted_iota(jnp.int32, (n, n), 1)
    return jnp.where((c <= r) if lower else (c >= r), 1.0, 0.0).astype(F32)


def _cum_fwd(fg, b128):
    t = fg.shape[0]
    nb = t // LANES

    def body(fg_ref, b_ref, cum_ref, cumt_ref):
        tri = _tri(LANES, True)
        carry = jnp.zeros((1, LANES), F32)
        for i in range(nb):
            z = fg_ref[i * LANES:(i + 1) * LANES, :] + b_ref[...]
            lf = jnp.minimum(z, 0.0) - jnp.log(1.0 + jnp.exp(-jnp.abs(z)))
            blk = jnp.dot(tri, lf, precision=lax.Precision.HIGHEST, preferred_element_type=F32) + carry
            cum_ref[i * LANES:(i + 1) * LANES, :] = blk
            carry = blk[LANES - 1:LANES, :]
        cumt_ref[...] = cum_ref[...].T[0:N_FOX, :]

    return _pcall(body, name="cum_fwd",
                  out_shape=[jax.ShapeDtypeStruct((t, LANES), F32), jax.ShapeDtypeStruct((N_FOX, t), F32)],
                  grid=(1,),
                  in_specs=[pl.BlockSpec((t, LANES), lambda i: (0, 0)), pl.BlockSpec((1, LANES), lambda i: (0, 0))],
                  out_specs=[pl.BlockSpec((t, LANES), lambda i: (0, 0)), pl.BlockSpec((N_FOX, t), lambda i: (0, 0))],
                  )(fg, b128)


def _fg_bwd(dcs_rows, fg, b128):
    t = fg.shape[0]
    nb = t // LANES

    def body(dcs_ref, fg_ref, b_ref, dfg_ref, db_ref, dcum_ref):
        dcum_ref[...] = -jnp.concatenate([dcs_ref[...], jnp.zeros((LANES - N_FOX, t), F32)], axis=0).T
        tri = _tri(LANES, False)
        carry = jnp.zeros((1, LANES), F32)
        db = jnp.zeros((1, LANES), F32)
        for i in reversed(range(nb)):
            rows = slice(i * LANES, (i + 1) * LANES)
            dlf = jnp.dot(tri, dcum_ref[rows, :], precision=lax.Precision.HIGHEST, preferred_element_type=F32) + carry
            carry = dlf[0:1, :]
            z = fg_ref[rows, :] + b_ref[...]
            dfg = dlf / (1.0 + jnp.exp(z))
            dfg_ref[rows, :] = dfg.astype(BF16)
            db = db + _rsum(dfg)
        db_ref[...] = db

    full = pl.BlockSpec((t, LANES), lambda i: (0, 0))
    vec = pl.BlockSpec((1, LANES), lambda i: (0, 0))
    return _pcall(body, name="fg_bwd",
                  out_shape=[jax.ShapeDtypeStruct((t, LANES), BF16), jax.ShapeDtypeStruct((1, LANES), F32)],
                  grid=(1,), in_specs=[pl.BlockSpec((N_FOX, t), lambda i: (0, 0)), full, vec], out_specs=[full, vec],
                  scratch_shapes=[pltpu.VMEM((t, LANES), F32)])(dcs_rows, fg, b128)


def _head_spec(t, col0, div=1):
    return pl.BlockSpec((t, HEAD_DIM), lambda h: (0, col0 + h // div))


def _fox_scores(q, k, cq, ck, i, tq, end):
    s = lax.dot_general(q, k, NT_DIMS, preferred_element_type=F32) * ATT_SCALE + cq - ck
    row = lax.broadcasted_iota(jnp.int32, (tq, end), 0) + i * tq
    col = lax.broadcasted_iota(jnp.int32, (tq, end), 1)
    return jnp.where(row >= col, s, -jnp.inf)


def _fox_fwd(proj_a, cum, cumt):
    t = proj_a.shape[0]
    tq = _blk(t, 512)
    nq = t // tq

    def body(q_ref, k_ref, v_ref, cum_ref, cumt_ref, o_ref, lse_ref):
        h = pl.program_id(0)
        cq_all = _pick_lane(cum_ref[...], h)
        ck_all = cumt_ref[pl.ds(h, 1), :]

        @pl.when(h == 0)
        def _():
            lse_ref[...] = jnp.zeros_like(lse_ref)

        for i in range(nq):
            rows, end = slice(i * tq, (i + 1) * tq), (i + 1) * tq
            s = _fox_scores(q_ref[rows, :], k_ref[0:end, :], cq_all[rows, :], ck_all[:, 0:end], i, tq, end)
            m = jnp.max(s, axis=1, keepdims=True)
            p = jnp.exp(s - m)
            l = jnp.sum(p, axis=1, keepdims=True)
            o = jnp.dot(p.astype(BF16), v_ref[0:end, :], preferred_element_type=F32) / l
            o_ref[rows, :] = o.astype(BF16)
            _put_lane(lse_ref, rows, h, m + jnp.log(l))

    nh = FOX_W // HEAD_DIM
    stat = pl.BlockSpec((t, LANES), lambda h: (0, 0))
    return _pcall(body, name="fox_fwd",
                  out_shape=[jax.ShapeDtypeStruct((t, FOX_W), BF16), jax.ShapeDtypeStruct((t, LANES), F32)],
                  grid=(N_FOX,),
                  in_specs=[_head_spec(t, 0), _head_spec(t, nh), _head_spec(t, 2 * nh), stat,
                            pl.BlockSpec((N_FOX, t), lambda h: (0, 0))],
                  out_specs=[_head_spec(t, 0), stat])(proj_a, proj_a, proj_a, cum, cumt)


def _fox_bwd(proj_a, d_attn, cum, cumt, lse):
    t = proj_a.shape[0]
    tq = _blk(t, 512)
    nq = t // tq

    def body(q_ref, k_ref, v_ref, do_ref, cum_ref, cumt_ref, lse_ref,
             dq_ref, dk_ref, dv_ref, dcs_ref, dk_acc, dv_acc, dcs_acc):
        h = pl.program_id(0)
        cq_all = _pick_lane(cum_ref[...], h)
        ck_all = cumt_ref[pl.ds(h, 1), :]
        lse_all = _pick_lane(lse_ref[...], h)
        dk_acc[...] = jnp.zeros_like(dk_acc)
        dv_acc[...] = jnp.zeros_like(dv_acc)
        dcs_acc[...] = jnp.zeros_like(dcs_acc)
        for i in range(nq):
            rows, end = slice(i * tq, (i + 1) * tq), (i + 1) * tq
            q, k, v, do = q_ref[rows, :], k_ref[0:end, :], v_ref[0:end, :], do_ref[rows, :]
            s = _fox_scores(q, k, cq_all[rows, :], ck_all[:, 0:end], i, tq, end)
            p = jnp.exp(s - lse_all[rows, :])
            dp = lax.dot_general(do, v, NT_DIMS, preferred_element_type=F32)
            ds = p * (dp - jnp.sum(p * dp, axis=1, keepdims=True))
            dcs_acc[:, 0:end] += jnp.sum(ds, axis=0, keepdims=True)
            ds = ds.astype(BF16)
            dq_ref[rows, :] = (jnp.dot(ds, k, preferred_element_type=F32) * ATT_SCALE).astype(BF16)
            dk_acc[0:end, :] += lax.dot_general(ds, q, TN_DIMS, preferred_element_type=F32)
            dv_acc[0:end, :] += lax.dot_general(p.astype(BF16), do, TN_DIMS, preferred_element_type=F32)
        dk_ref[...] = (dk_acc[...] * ATT_SCALE).astype(BF16)
        dv_ref[...] = dv_acc[...].astype(BF16)
        dcs_ref[pl.ds(h, 1), :] = dcs_acc[...]

    nh = FOX_W // HEAD_DIM
    stat = pl.BlockSpec((t, LANES), lambda h: (0, 0))
    rows8 = pl.BlockSpec((N_FOX, t), lambda h: (0, 0))
    head = _head_spec(t, 0)
    wide = jax.ShapeDtypeStruct((t, FOX_W), BF16)
    return _pcall(body, name="fox_bwd",
                  out_shape=[wide, wide, wide, jax.ShapeDtypeStruct((N_FOX, t), F32)],
                  grid=(N_FOX,),
                  in_specs=[_head_spec(t, 0), _head_spec(t, nh), _head_spec(t, 2 * nh), head, stat, rows8, stat],
                  out_specs=[head, head, head, rows8],
                  scratch_shapes=[pltpu.VMEM((t, HEAD_DIM), F32), pltpu.VMEM((t, HEAD_DIM), F32),
                                  pltpu.VMEM((1, t), F32)],
                  )(proj_a, proj_a, proj_a, d_attn, cum, cumt, lse)


def _swa_scores(q, k, i, tq, start, end):
    s = lax.dot_general(q, k, NT_DIMS, preferred_element_type=F32) * ATT_SCALE
    row = lax.broadcasted_iota(jnp.int32, (tq, end - start), 0) + i * tq
    col = lax.broadcasted_iota(jnp.int32, (tq, end - start), 1) + start
    diff = row - col
    return jnp.where((diff >= 0) & (diff < WINDOW), s, -jnp.inf)


def _swa_blocks(t):
    tq = _blk(t, 256)
    return tq, [(i, max(0, i * tq - WINDOW), (i + 1) * tq) for i in range(t // tq)]


def _swa_fwd(proj_b, sinks128):
    t = proj_b.shape[0]
    tq, blocks = _swa_blocks(t)

    def body(q_ref, k_ref, v_ref, sink_ref, o_ref, lse_ref):
        h = pl.program_id(0)
        sink = _pick_lane(sink_ref[...], h)

        @pl.when(h == 0)
        def _():
            lse_ref[...] = jnp.zeros_like(lse_ref)

        for i, start, end in blocks:
            rows = slice(i * tq, end)
            s = _swa_scores(q_ref[rows, :], k_ref[start:end, :], i, tq, start, end)
            m = jnp.maximum(jnp.max(s, axis=1, keepdims=True), sink)
            p = jnp.exp(s - m)
            l = jnp.sum(p, axis=1, keepdims=True) + jnp.exp(sink - m)
            o = jnp.dot(p.astype(BF16), v_ref[start:end, :], preferred_element_type=F32) / l
            o_ref[rows, :] = o.astype(BF16)
            _put_lane(lse_ref, rows, h, m + jnp.log(l))

    stat = pl.BlockSpec((t, LANES), lambda h: (0, 0))
    return _pcall(body, name="swa_fwd",
                  out_shape=[jax.ShapeDtypeStruct((t, SWA_W), BF16), jax.ShapeDtypeStruct((t, LANES), F32)],
                  grid=(N_SWA,),
                  in_specs=[_head_spec(t, 0), _head_spec(t, N_SWA, GQA), _head_spec(t, N_SWA + N_KV, GQA),
                            pl.BlockSpec((1, LANES), lambda h: (0, 0))],
                  out_specs=[_head_spec(t, 0), stat])(proj_b, proj_b, proj_b, sinks128)


def _rope_bwd(d, cos, sin):
    return d * cos + pltpu.roll(d * sin, HEAD_DIM // 2, 1)


def _swa_bwd(proj_b, d_attn, lse, sinks128, cos, sin):
    t = proj_b.shape[0]
    tq, blocks = _swa_blocks(t)

    def body(q_ref, k_ref, v_ref, do_ref, lse_ref, sink_ref, cos_ref, sin_ref,
             dq_ref, dk_ref, dv_ref, dsink_ref, dk_acc, dv_acc):
        h = pl.program_id(0)
        sink = _pick_lane(sink_ref[...], h)
        lse_all = _pick_lane(lse_ref[...], h)

        @pl.when(h == 0)
        def _():
            dsink_ref[...] = jnp.zeros_like(dsink_ref)

        @pl.when(h % GQA == 0)
        def _():
            dk_acc[...] = jnp.zeros_like(dk_acc)
            dv_acc[...] = jnp.zeros_like(dv_acc)

        dsink = jnp.zeros((1, 1), F32)
        for i, start, end in blocks:
            rows = slice(i * tq, end)
            q, k, v, do = q_ref[rows, :], k_ref[start:end, :], v_ref[start:end, :], do_ref[rows, :]
            s = _swa_scores(q, k, i, tq, start, end)
            p = jnp.exp(s - lse_all[rows, :])
            dp = lax.dot_general(do, v, NT_DIMS, preferred_element_type=F32)
            delta = jnp.sum(p * dp, axis=1, keepdims=True)
            ds = (p * (dp - delta)).astype(BF16)
            dq = jnp.dot(ds, k, preferred_element_type=F32) * ATT_SCALE
            dq_ref[rows, :] = _rope_bwd(dq, cos_ref[rows, :], sin_ref[rows, :]).astype(BF16)
            dk_acc[start:end, :] += lax.dot_general(ds, q, TN_DIMS, preferred_element_type=F32)
            dv_acc[start:end, :] += lax.dot_general(p.astype(BF16), do, TN_DIMS, preferred_element_type=F32)
            dsink = dsink - jnp.sum(jnp.exp(sink - lse_all[rows, :]) * delta, axis=0, keepdims=True)
        old = dsink_ref[...]
        lane = lax.broadcasted_iota(jnp.int32, old.shape, 1)
        dsink_ref[...] = jnp.where(lane == h, dsink, old)

        @pl.when(h % GQA == GQA - 1)
        def _():
            dk_ref[...] = _rope_bwd(dk_acc[...] * ATT_SCALE, cos_ref[...], sin_ref[...]).astype(BF16)
            dv_ref[...] = dv_acc[...].astype(BF16)

    stat = pl.BlockSpec((t, LANES), lambda h: (0, 0))
    vec = pl.BlockSpec((1, LANES), lambda h: (0, 0))
    head = _head_spec(t, 0)
    kv_out = _head_spec(t, 0, GQA)
    return _pcall(body, name="swa_bwd",
                  out_shape=[jax.ShapeDtypeStruct((t, SWA_W), BF16), jax.ShapeDtypeStruct((t, KV_W), BF16),
                             jax.ShapeDtypeStruct((t, KV_W), BF16), jax.ShapeDtypeStruct((1, LANES), F32)],
                  grid=(N_SWA,),
                  in_specs=[head, _head_spec(t, N_SWA, GQA), _head_spec(t, N_SWA + N_KV, GQA),
                            _head_spec(t, N_FOX), stat, vec, stat, stat],
                  out_specs=[head, kv_out, kv_out, vec],
                  scratch_shapes=[pltpu.VMEM((t, HEAD_DIM), F32), pltpu.VMEM((t, HEAD_DIM), F32)],
                  )(proj_b, proj_b, proj_b, d_attn, lse, sinks128, cos, sin)


def _adamw(w, g, m, v):
    m = ADAM_B1 * m + (1.0 - ADAM_B1) * g
    v = ADAM_B2 * v + (1.0 - ADAM_B2) * (g * g)
    m_hat = m / (1.0 - ADAM_B1 ** ADAM_STEP)
    v_hat = v / (1.0 - ADAM_B2 ** ADAM_STEP)
    delta = -ADAM_LR * (m_hat / (jnp.sqrt(v_hat) + ADAM_EPS) + ADAM_WD * w)
    return delta, m, v


def _adam_pieces(w, m, v, own, land, idx, name):
    rows, cols = w.shape
    tr = 256 if rows % 256 == 0 else rows

    def body(idx_ref, own_ref, l1_ref, l2_ref, l3_ref, w_ref, m_ref, v_ref, g_ref, d_ref, mo_ref, vo_ref):
        g = own_ref[...].astype(F32) + l1_ref[...].astype(F32) + l2_ref[...].astype(F32) + l3_ref[...].astype(F32)
        g_ref[...] = g
        d_ref[...], mo_ref[...], vo_ref[...] = _adamw(w_ref[...], g, m_ref[...], v_ref[...])

    def piece(p):
        return pl.BlockSpec((None, tr, cols), lambda i, idx_ref: (idx_ref[p], i, 0))

    tile = pl.BlockSpec((tr, cols), lambda i, idx_ref: (i, 0))
    out = jax.ShapeDtypeStruct((rows, cols), F32)
    grid_spec = pltpu.PrefetchScalarGridSpec(
        num_scalar_prefetch=1, grid=(rows // tr,),
        in_specs=[piece(0), piece(1), piece(2), piece(3), tile, tile, tile], out_specs=[tile] * 4)
    return _pcall(body, name=name, out_shape=[out] * 4, grid_spec=grid_spec)(idx, own, land, land, land, w, m, v)


def _adam_mod(c_all, dmod_cols, w, m, v):
    rows, cols = w.shape
    tr = _blk(rows, 256)

    def body(c_ref, dm_ref, w_ref, m_ref, v_ref, g_ref, d_ref, mo_ref, vo_ref):
        cond = _silu(c_ref[...]).astype(BF16)
        g = lax.dot_general(cond, dm_ref[...].astype(BF16), TN_DIMS, preferred_element_type=F32)
        g_ref[...] = g
        d_ref[...], mo_ref[...], vo_ref[...] = _adamw(w_ref[...], g, m_ref[...], v_ref[...])

    tile = pl.BlockSpec((tr, cols), lambda i: (i, 0))
    out = jax.ShapeDtypeStruct((rows, cols), F32)
    return _pcall(body, name="adam_mod", out_shape=[out] * 4, grid=(rows // tr,),
                  in_specs=[pl.BlockSpec((N_DEV, tr), lambda i: (0, i)), pl.BlockSpec((N_DEV, cols), lambda i: (0, 0)),
                            tile, tile, tile],
                  out_specs=[tile] * 4)(c_all, dmod_cols, w, m, v)


def _adam_small(parts, w, m, v):
    nv = w.shape[1]

    def body(p_ref, w_ref, m_ref, v_ref, g_ref, d_ref, mo_ref, vo_ref):
        g = p_ref[0:1, :]
        for k in range(1, N_DEV):
            g = g + p_ref[k:k + 1, :]
        g_ref[...] = g
        d_ref[...], mo_ref[...], vo_ref[...] = _adamw(w_ref[...], g, m_ref[...], v_ref[...])

    vec = pl.BlockSpec((1, nv), lambda i: (0, 0))
    out = jax.ShapeDtypeStruct((1, nv), F32)
    return _pcall(body, name="adam_small", out_shape=[out] * 4, grid=(1,),
                  in_specs=[pl.BlockSpec((N_DEV, nv), lambda i: (0, 0)), vec, vec, vec],
                  out_specs=[vec] * 4)(parts, w, m, v)


def _pad_lanes(v, width=LANES):
    return jnp.pad(v, ((0, 0), (0, width - v.shape[1])))


def kernel(x, c, w_mod, b_mod, g_pre_mix, g_post_mix, w_in, b_forget, swa_sinks, w_out, g_pre_mlp, g_post_mlp, w_up, w_down, loss_target, m_w_mod, m_b_mod, m_g_pre_mix, m_g_post_mix, m_w_in, m_b_forget, m_swa_sinks, m_w_out, m_g_pre_mlp, m_g_post_mlp, m_w_up, m_w_down, v_w_mod, v_b_mod, v_g_pre_mix, v_g_post_mix, v_w_in, v_b_forget, v_swa_sinks, v_w_out, v_g_pre_mlp, v_g_post_mlp, v_w_up, v_w_down):
    ax, ay, ac = _position()
    me = 4 * ax + 2 * ay + ac
    x, target = x[0], loss_target[0]
    t, d = x.shape
    w_mod, w_in, w_out, w_up, w_down = w_mod[0], w_in[0], w_out[0], w_up[0], w_down[0]
    mod_w = w_mod.shape[1]
    in_w = w_in.shape[1]
    in_total = N_DEV * in_w
    shard_ff = w_up.shape[1]
    n_fox3 = 3 * FOX_W
    n_swa3 = SWA_W + 2 * KV_W
    assert in_total == n_fox3 + N_FOX + n_swa3 and d == FOX_W + SWA_W

    c_all = _all_gather([c], "gather_c")[0].reshape(N_DEV, d)
    b_part = lax.dynamic_slice(b_mod, (0, me * mod_w), (1, mod_w))
    mod_parts = _all_gather([_mod_part(c_all, w_mod, b_part)], "gather_mod")[0]
    mod = lax.dynamic_index_in_dim(mod_parts, me, axis=1, keepdims=False).reshape(1, N_DEV * mod_w)

    shards = [w_in.astype(BF16), w_out.astype(BF16), w_up.astype(BF16), w_down.astype(BF16)]
    shards, mod = lax.optimization_barrier((shards, mod))
    ag_send, ag_recv, ag_shard, ag_land, ag_token = _ag_start(shards)

    def gathered(i, after, name):
        shard, land = _ag_wait(ag_send[i], ag_recv[i], ag_shard[i], ag_land[i], after, "ag_wait_" + name)
        return _ag_forward(land, shard, "ag_fwd_" + name)

    sh_a, sc_a, gt_a, sh_m, sc_m, gt_m = [mod[:, i * d:(i + 1) * d] for i in range(6)]

    half = HEAD_DIM // 2
    inv_freq = 1.0 / (ROPE_THETA ** (jnp.arange(half, dtype=F32) * (2.0 / HEAD_DIM)))
    ang = jnp.arange(t).astype(F32)[:, None] * inv_freq[None, :]
    cos = jnp.concatenate([jnp.cos(ang), jnp.cos(ang)], axis=1)
    sin = jnp.concatenate([-jnp.sin(ang), jnp.sin(ang)], axis=1)

    b128 = _pad_lanes(b_forget)
    sinks128 = _pad_lanes(swa_sinks)

    h1 = _pre_attn(x, g_pre_mix + ag_token[0:1, 0:1], sc_a, sh_a)
    w_in_g = gathered(0, h1, "in")
    o_fg, o_sq, o_sv = n_fox3, n_fox3 + N_FOX, n_fox3 + N_FOX + SWA_W + KV_W

    def cols(lo, hi):
        parts = []
        for j in range(lo // in_w, (hi - 1) // in_w + 1):
            parts.append(w_in_g[j, :, max(lo - j * in_w, 0):min(hi - j * in_w, in_w)])
        return parts

    w_in_r = jnp.concatenate(cols(0, o_fg) + cols(o_sq, in_total) + cols(o_fg, o_sq)
                             + [jnp.zeros((d, FG_PAD - N_FOX), BF16)], axis=1)
    proj_a = _matmul(h1, w_in_r, name="proj_a", n_cols=n_fox3, n_off=0)
    proj_b = _matmul(h1, w_in_r, name="proj_b", n_cols=n_swa3, n_off=n_fox3, tn=512, row_extras=(cos, sin),
                     epilogue=lambda acc, j, cs, sn: (_rope_cols(acc, j, cs, sn, SWA_W + KV_W),))
    fg = _matmul(h1, w_in_r, name="proj_fg", n_cols=FG_PAD, n_off=n_fox3 + n_swa3, tn=FG_PAD,
                 out_dtypes=(F32,))[:, 0:LANES]
    cum, cumt = _cum_fwd(fg, b128)
    fox_o, fox_lse = _fox_fwd(proj_a, cum, cumt)
    swa_o, swa_lse = _swa_fwd(proj_b, sinks128)
    w_out_full = gathered(1, swa_o, "out").reshape(d, d)
    attn = jnp.concatenate([fox_o, swa_o], axis=1)
    mix = _matmul(attn, w_out_full, name="out_proj", out_dtypes=(F32,))
    x2, h2 = _post_mix(x, mix, gt_a, g_post_mix, g_pre_mlp, sc_m, sh_m)
    w_up_g = gathered(2, h2, "up")
    u, act = _matmul(h2, w_up_g, name="mlp_up", b_sharded=True, out_dtypes=(BF16, BF16),
                     epilogue=lambda acc, j: (acc, jnp.square(jnp.maximum(acc, 0.0))))
    w_down_full = gathered(3, act, "down").reshape(N_DEV * shard_ff, d)
    y = _matmul(act, w_down_full, name="mlp_down", out_dtypes=(F32,))

    core = jnp.reshape(ac, (1,)).astype(jnp.int32)

    def reduce_start(full, name):
        from_sibling = _rs_sibling([full], "rs_sibling_" + name)[0]
        return _rs_start(_chip_sum(full, from_sibling, core, "chip_sum_" + name), "rs_start_" + name)

    idx = jnp.stack([2 * ax + ay, 2 * (1 - ax) + ay, 2 * ax + (1 - ay), 2 * (1 - ax) + (1 - ay)]).astype(jnp.int32)

    def reduce_finish(started, after, w, m, v, name):
        send, recv, src, land, _ = started
        own, landed = _rs_wait(send, recv, src, land, after, "rs_wait_" + name)
        return _adam_pieces(w, m[0], v[0], own, landed, idx, "adam_" + name)

    dy, dout, dgt_m, dg3, loss_vec = _final(y, x2, target, gt_m, g_post_mlp)
    du = _matmul(dy, w_down_full, name="d_act", tb=True, tile_extras=(u,),
                 epilogue=lambda acc, j, uu: (acc * (2.0 * jnp.maximum(uu.astype(F32), 0.0)),))
    dw_down = _matmul(act, dy, name="dw_down", ta=True)
    rs_down = reduce_start(dw_down.reshape(N_DEV, shard_ff, d), "down")
    dh2 = _matmul(du, w_up_g, name="d_h2", tb=True, b_sharded=True, out_dtypes=(F32,))
    per = shard_ff // _blk(shard_ff, 1024)
    dw_up = _matmul(h2, du, name="dw_up", ta=True, tn=_blk(shard_ff, 1024), out_shape=(N_DEV, d, shard_ff),
                    out_map=lambda tm, tn: pl.BlockSpec((None, tm, tn), lambda i, j, kk: (j // per, i, j % per)))
    rs_up = reduce_start(dw_up, "up")
    dmix, dx2, dsh_m, dsc_m, dg2, dgt_a, dg1 = _mid_bwd(
        dh2, dout, x2, mix, g_pre_mlp + (rs_down[4][0:1, 0:1] + rs_up[4][0:1, 0:1]), sc_m, gt_a, g_post_mix)
    d_attn = _matmul(dmix, w_out_full, name="d_attn", tb=True)
    dw_out = _matmul(attn, dmix, name="dw_out", ta=True)
    rs_out = reduce_start(dw_out.reshape(N_DEV, d // N_DEV, d), "out")
    dqf, dkf, dvf, dcs = _fox_bwd(proj_a, d_attn, cum, cumt, fox_lse)
    dsq, dsk, dsv, dsinks = _swa_bwd(proj_b, d_attn, swa_lse, sinks128, cos, sin)
    dfg, db_forget = _fg_bwd(dcs, fg, b128 + rs_out[4][0:1, 0:1])
    dproj = jnp.concatenate([dqf, dkf, dvf, dsq, dsk, dsv, _pad_lanes(dfg, FG_PAD)], axis=1)
    dh1 = _matmul(dproj, w_in_r, name="d_h1", tb=True, out_dtypes=(F32,))
    dw_in_r = _matmul(h1, dproj, name="dw_in", ta=True)

    def shard_cols(j):
        lo, hi = j * in_w, (j + 1) * in_w
        parts = []
        for seg_lo, seg_hi, shift in ((0, o_fg, 0), (o_fg, o_sq, n_swa3), (o_sq, in_total, -N_FOX)):
            a, b = max(lo, seg_lo), min(hi, seg_hi)
            if a < b:
                parts.append(dw_in_r[:, a + shift:b + shift])
        return parts[0] if len(parts) == 1 else jnp.concatenate(parts, axis=1)

    rs_in = reduce_start(jnp.stack([shard_cols(j) for j in range(N_DEV)]), "in")
    grad_x, dsh_a, dsc_a, dg0 = _x_bwd(dh1, dx2, x, g_pre_mix + rs_in[4][0:1, 0:1], sc_a)

    g_w_down, d_w_down, nm_w_down, nv_w_down = reduce_finish(rs_down, grad_x, w_down, m_w_down, v_w_down, "w_down")
    g_w_up, d_w_up, nm_w_up, nv_w_up = reduce_finish(rs_up, d_w_down, w_up, m_w_up, v_w_up, "w_up")
    g_w_out, d_w_out, nm_w_out, nv_w_out = reduce_finish(rs_out, d_w_up, w_out, m_w_out, v_w_out, "w_out")

    small = jnp.concatenate([dsh_a, dsc_a, dgt_a, dsh_m, dsc_m, dgt_m, dg0, dg1, dg2, dg3, db_forget, dsinks,
                             loss_vec[:, 0:LANES]], axis=1)
    small, d_w_out = lax.optimization_barrier((small, d_w_out))
    small_all = _all_gather([small], "gather_small")[0].reshape(N_DEV, small.shape[1])
    pack = lambda bm, g0_, g1_, g2_, g3_, bf_, sk_: jnp.concatenate(
        [bm, g0_, g1_, g2_, g3_, _pad_lanes(bf_), _pad_lanes(sk_), jnp.zeros((1, LANES), F32)], axis=1)
    p_small = pack(b_mod, g_pre_mix, g_post_mix, g_pre_mlp, g_post_mlp, b_forget, swa_sinks)
    m_small = pack(m_b_mod, m_g_pre_mix, m_g_post_mix, m_g_pre_mlp, m_g_post_mlp, m_b_forget, m_swa_sinks)
    v_small = pack(v_b_mod, v_g_pre_mix, v_g_post_mix, v_g_pre_mlp, v_g_post_mlp, v_b_forget, v_swa_sinks)
    small_out = _adam_small(small_all, p_small, m_small, v_small)

    n_mod = 6 * d

    def unpack(vec):
        o = n_mod
        return (vec[:, 0:n_mod], vec[:, o:o + d], vec[:, o + d:o + 2 * d], vec[:, o + 2 * d:o + 3 * d],
                vec[:, o + 3 * d:o + 4 * d], vec[:, o + 4 * d:o + 4 * d + N_FOX],
                vec[:, o + 4 * d + LANES:o + 4 * d + LANES + N_SWA])

    loss = small_out[0][0, n_mod + 4 * d + 2 * LANES]
    g_small, d_small, nm_small, nv_small = [unpack(vec) for vec in small_out]

    dmod_cols = lax.dynamic_slice(small_all, (0, me * mod_w), (N_DEV, mod_w))
    g_w_mod, d_w_mod, nm_w_mod, nv_w_mod = _adam_mod(c_all, dmod_cols, w_mod, m_w_mod[0], v_w_mod[0])
    g_w_in, d_w_in, nm_w_in, nv_w_in = reduce_finish(rs_in, d_w_mod, w_in, m_w_in, v_w_in, "w_in")

    def assemble(w_mod_, small_, w_in_, w_out_, w_up_, w_down_):
        b_mod_, g0_, g1_, g2_, g3_, bf_, sk_ = small_
        return [w_mod_[None], b_mod_, g0_, g1_, w_in_[None], bf_, sk_, w_out_[None], g2_, g3_, w_up_[None], w_down_[None]]

    outs = [loss, grad_x[None]]
    outs += assemble(g_w_mod, g_small, g_w_in, g_w_out, g_w_up, g_w_down)
    outs += assemble(d_w_mod, d_small, d_w_in, d_w_out, d_w_up, d_w_down)
    outs += assemble(nm_w_mod, nm_small, nm_w_in, nm_w_out, nm_w_up, nm_w_down)
    outs += assemble(nv_w_mod, nv_small, nv_w_in, nv_w_out, nv_w_up, nv_w_down)
    return tuple(outs)
```

```python
import functools

import jax
import jax.numpy as jnp
from jax import lax
from jax.experimental import pallas as pl
from jax.experimental.pallas import tpu as pltpu

F32 = jnp.float32
BF16 = jnp.bfloat16
MESH = pl.DeviceIdType.MESH

N_DEV = 8
N_CHIP = 4
LANES = 128
HEAD_DIM = 128
N_FOX = 8
N_SWA = 8
N_KV = 2
GQA = N_SWA // N_KV
WINDOW = 128
FOX_W = N_FOX * HEAD_DIM
SWA_W = N_SWA * HEAD_DIM
KV_W = N_KV * HEAD_DIM
ROPE_THETA = 10000.0
NORM_EPS = 1e-6
ATT_SCALE = HEAD_DIM ** -0.5
FG_PAD = 512

ADAM_LR = 0.001
ADAM_B1 = 0.9
ADAM_B2 = 0.999
ADAM_EPS = 1e-08
ADAM_WD = 0.01
ADAM_STEP = 10

VMEM_LIMIT = 56 * 1024 * 1024

NT_DIMS = (((1,), (1,)), ((), ()))
TN_DIMS = (((0,), (0,)), ((), ()))
NN_DIMS = (((1,), (0,)), ((), ()))


def _pcall(body, *, name, out_shape, grid=(), in_specs=None, out_specs=None, scratch_shapes=(), grid_spec=None):
    params = pltpu.CompilerParams(vmem_limit_bytes=VMEM_LIMIT)
    if grid_spec is not None:
        return pl.pallas_call(body, name=name, out_shape=out_shape, grid_spec=grid_spec, compiler_params=params)
    return pl.pallas_call(body, name=name, out_shape=out_shape, grid=grid, in_specs=in_specs, out_specs=out_specs,
                          scratch_shapes=scratch_shapes, compiler_params=params)


def _blk(n, pref):
    if n <= pref:
        return n
    b = (pref // LANES) * LANES
    while n % b:
        b -= LANES
    return b


def _position():
    return lax.axis_index("x"), lax.axis_index("y"), lax.axis_index("c")


ANY = pl.BlockSpec(memory_space=pl.ANY)


def _all_gather(arrs, name):
    n = len(arrs)

    def body(*refs):
        ins, outs = refs[:n], refs[n:2 * n]
        send_sems, recv_sems, local_sems = refs[2 * n:]
        x, y, c = _position()
        me, sibling = (x, y, c), (x, y, 1 - c)
        chips = [(1 - x, y), (x, 1 - y), (1 - x, 1 - y)]

        def slot(p):
            return 4 * p[0] + 2 * p[1] + p[2]

        def copy(a, k, block, to, src=None):
            dst = outs[a].at[slot(block)]
            return pltpu.make_async_remote_copy(
                src_ref=dst if src is None else src, dst_ref=dst,
                send_sem=send_sems.at[7 * a + k], recv_sem=recv_sems.at[7 * a + k],
                device_id=to, device_id_type=MESH)

        mine = [pltpu.make_async_copy(ins[a], outs[a].at[slot(me)], local_sems.at[a]) for a in range(n)]
        for cp in mine:
            cp.start()
        first = []
        for a in range(n):
            first.append(copy(a, 0, me, sibling, src=ins[a]))
            first += [copy(a, 1 + j, me, (*chip, c), src=ins[a]) for j, chip in enumerate(chips)]
        for cp in first:
            cp.start()
        passed = []
        for a in range(n):
            for j, chip in enumerate(chips):
                copy(a, 1 + j, (*chip, c), me).wait_recv()
                cp = copy(a, 4 + j, (*chip, c), sibling)
                cp.start()
                passed.append(cp)
        for a in range(n):
            copy(a, 0, sibling, me).wait_recv()
            for j, chip in enumerate(chips):
                copy(a, 4 + j, (*chip, 1 - c), me).wait_recv()
        for cp in first + passed:
            cp.wait_send()
        for cp in mine:
            cp.wait()

    return _pcall(
        body, name=name,
        out_shape=[jax.ShapeDtypeStruct((N_DEV,) + a.shape, a.dtype) for a in arrs],
        in_specs=[ANY] * n, out_specs=[ANY] * n,
        scratch_shapes=[pltpu.SemaphoreType.DMA((7 * n,)), pltpu.SemaphoreType.DMA((7 * n,)),
                        pltpu.SemaphoreType.DMA((n,))],
    )(*arrs)


HBM = pl.BlockSpec(memory_space=pltpu.HBM)
SEM = pl.BlockSpec(memory_space=pltpu.SEMAPHORE)
EFFECT = pltpu.SideEffectType.DATAFLOW_SIDE_EFFECTING


def _hbm(a):
    return pltpu.with_memory_space_constraint(a, pltpu.HBM)


def _gather_peers():
    x, y, c = _position()
    return [(x, y, 1 - c), (1 - x, y, c), (x, 1 - y, c), (1 - x, 1 - y, c)]


def _ag_start(shards):
    n = len(shards)
    lands = [_hbm(lax.empty((N_DEV,) + s.shape, s.dtype)) for s in shards]

    def body(*refs):
        srcs, land, send, recv = refs[:n], refs[n:2 * n], refs[2 * n:3 * n], refs[3 * n:4 * n]
        token = refs[6 * n]
        x, y, c = _position()
        for a in range(n):
            for k, to in enumerate(_gather_peers()):
                pltpu.make_async_remote_copy(
                    src_ref=srcs[a], dst_ref=land[a].at[4 * x + 2 * y + c], send_sem=send[a].at[k],
                    recv_sem=recv[a].at[k], device_id=to, device_id_type=MESH).start()
        token[...] = jnp.zeros_like(token)

    sems = [pltpu.SemaphoreType.DMA((4,))] * (2 * n)
    out = pl.pallas_call(
        body, name="ag_start",
        out_shape=sems + [pltpu.HBM(s.shape, s.dtype) for s in shards] + [pltpu.HBM(l.shape, l.dtype) for l in lands]
        + [jax.ShapeDtypeStruct((8, LANES), F32)],
        in_specs=[HBM] * (2 * n), out_specs=[SEM] * (2 * n) + [HBM] * (2 * n) + [pl.BlockSpec(memory_space=pltpu.VMEM)],
        input_output_aliases={**{a: 2 * n + a for a in range(n)}, **{n + a: 3 * n + a for a in range(n)}},
        compiler_params=pltpu.CompilerParams(has_side_effects=EFFECT),
    )(*[_hbm(s) for s in shards], *lands)
    return out[:n], out[n:2 * n], out[2 * n:3 * n], out[3 * n:4 * n], out[4 * n]


def _ag_wait(send, recv, shard_thru, land_thru, after, name):
    def body(v_ref, land_ref, send_sem, recv_sem, after_ref, v_dead, got_ref):
        for k, to in enumerate(_gather_peers()):
            cp = pltpu.make_async_remote_copy(
                src_ref=v_ref, dst_ref=land_ref.at[0], send_sem=send_sem.at[k], recv_sem=recv_sem.at[k],
                device_id=to, device_id_type=MESH)
            cp.wait_send()
            cp.wait_recv()

    return pl.pallas_call(
        body, name=name,
        out_shape=(pltpu.HBM(shard_thru.shape, shard_thru.dtype), pltpu.HBM(land_thru.shape, land_thru.dtype)),
        in_specs=(HBM, HBM, SEM, SEM, ANY), out_specs=(HBM, HBM), input_output_aliases={0: 0, 1: 1},
        compiler_params=pltpu.CompilerParams(has_side_effects=EFFECT),
    )(shard_thru, land_thru, send, recv, after)


def _ag_forward(land, name):
    def body(land_in, land_ref, send_sems, recv_sems):
        x, y, c = _position()
        copies = []
        for j, (px, py) in enumerate([(1 - x, y), (x, 1 - y), (1 - x, 1 - y)]):
            block = land_ref.at[4 * px + 2 * py + c]
            cp = pltpu.make_async_remote_copy(src_ref=block, dst_ref=block, send_sem=send_sems.at[j],
                                              recv_sem=recv_sems.at[j], device_id=(x, y, 1 - c), device_id_type=MESH)
            cp.start()
            copies.append(cp)
        for cp in copies:
            cp.wait()

    return pl.pallas_call(
        body, name=name, out_shape=jax.ShapeDtypeStruct(land.shape, land.dtype),
        in_specs=[ANY], out_specs=ANY, input_output_aliases={0: 0},
        scratch_shapes=[pltpu.SemaphoreType.DMA((3,)), pltpu.SemaphoreType.DMA((3,))],
    )(land)


def _rs_peers():
    x, y, c = _position()
    return [(1 - x, y, c), (x, 1 - y, c), (1 - x, 1 - y, c)]


def _rs_start(chip_sums, name):
    land = _hbm(lax.empty(chip_sums.shape, chip_sums.dtype))

    def body(src, land_ref, send, recv, src_thru, land_thru, token):
        x, y, c = _position()
        for j, (px, py, pc) in enumerate(_rs_peers()):
            pltpu.make_async_remote_copy(
                src_ref=src.at[2 * px + py], dst_ref=land_ref.at[2 * x + y], send_sem=send.at[j], recv_sem=recv.at[j],
                device_id=(px, py, pc), device_id_type=MESH).start()
        token[...] = jnp.zeros_like(token)

    return pl.pallas_call(
        body, name=name,
        out_shape=[pltpu.SemaphoreType.DMA((3,)), pltpu.SemaphoreType.DMA((3,)),
                   pltpu.HBM(chip_sums.shape, chip_sums.dtype), pltpu.HBM(land.shape, land.dtype),
                   jax.ShapeDtypeStruct((8, LANES), F32)],
        in_specs=[HBM, HBM], out_specs=[SEM, SEM, HBM, HBM, pl.BlockSpec(memory_space=pltpu.VMEM)],
        input_output_aliases={0: 2, 1: 3},
        compiler_params=pltpu.CompilerParams(has_side_effects=EFFECT),
    )(_hbm(chip_sums), land)


def _rs_wait(send, recv, src_thru, land_thru, after, name):
    def body(src, land_ref, send_sem, recv_sem, after_ref, src_out, land_out):
        for j, to in enumerate(_rs_peers()):
            cp = pltpu.make_async_remote_copy(
                src_ref=src.at[0], dst_ref=land_ref.at[0], send_sem=send_sem.at[j], recv_sem=recv_sem.at[j],
                device_id=to, device_id_type=MESH)
            cp.wait_send()
            cp.wait_recv()

    return pl.pallas_call(
        body, name=name,
        out_shape=(pltpu.HBM(src_thru.shape, src_thru.dtype), pltpu.HBM(land_thru.shape, land_thru.dtype)),
        in_specs=(HBM, HBM, SEM, SEM, ANY), out_specs=(HBM, HBM), input_output_aliases={0: 0, 1: 1},
        compiler_params=pltpu.CompilerParams(has_side_effects=EFFECT),
    )(src_thru, land_thru, send, recv, after)


def _rs_sibling(arrs, name):
    n = len(arrs)

    def body(*refs):
        ins, outs = refs[:n], refs[n:2 * n]
        send_sems, recv_sems = refs[2 * n:]
        x, y, c = _position()
        copies = []
        for a in range(n):
            for k in range(N_CHIP):
                cp = pltpu.make_async_remote_copy(
                    src_ref=ins[a].at[2 * k + (1 - c)], dst_ref=outs[a].at[k],
                    send_sem=send_sems.at[N_CHIP * a + k], recv_sem=recv_sems.at[N_CHIP * a + k],
                    device_id=(x, y, 1 - c), device_id_type=MESH)
                cp.start()
                copies.append(cp)
        for cp in copies:
            cp.wait()

    return _pcall(
        body, name=name,
        out_shape=[jax.ShapeDtypeStruct((N_CHIP,) + a.shape[1:], a.dtype) for a in arrs],
        in_specs=[ANY] * n, out_specs=[ANY] * n,
        scratch_shapes=[pltpu.SemaphoreType.DMA((N_CHIP * n,)), pltpu.SemaphoreType.DMA((N_CHIP * n,))],
    )(*arrs)


def _chip_sum(full, recv, core, name):
    _, rows, cols = full.shape
    tr = 256 if rows % 256 == 0 else rows

    def body(core_ref, a_ref, b_ref, o_ref):
        o_ref[...] = (a_ref[...].astype(F32) + b_ref[...].astype(F32)).astype(o_ref.dtype)

    grid_spec = pltpu.PrefetchScalarGridSpec(
        num_scalar_prefetch=1, grid=(N_CHIP, rows // tr),
        in_specs=[pl.BlockSpec((None, tr, cols), lambda k, i, core_ref: (2 * k + core_ref[0], i, 0)),
                  pl.BlockSpec((None, tr, cols), lambda k, i, core_ref: (k, i, 0))],
        out_specs=pl.BlockSpec((None, tr, cols), lambda k, i, core_ref: (k, i, 0)))
    return _pcall(body, name=name, out_shape=jax.ShapeDtypeStruct((N_CHIP, rows, cols), full.dtype),
                  grid_spec=grid_spec)(core, full, recv)


def _matmul(a, b, *, name, ta=False, tb=False, tm=1024, tn=1024, tk=2048, out_dtypes=(BF16,), epilogue=None,
            row_extras=(), tile_extras=(), out_shape=None, out_map=None, b_sharded=False, n_cols=None, n_off=0):
    m, k = (a.shape[1], a.shape[0]) if ta else a.shape
    if b_sharded:
        shard_c = b.shape[2]
        n, kb = (b.shape[1], N_DEV * shard_c) if tb else (N_DEV * shard_c, b.shape[1])
        tn, tk = (tn, min(tk, shard_c)) if tb else (min(tn, shard_c), tk)
    else:
        n, kb = b.shape if tb else (b.shape[1], b.shape[0])
    assert kb == k, (name, kb, k)
    if n_cols is not None:
        n = n_cols
    tm, tn, tk = _blk(m, tm), _blk(n, tn), _blk(k, tk)
    assert n_off % tn == 0
    nk = k // tk
    dims = (((0 if ta else 1,), (1 if tb else 0,)), ((), ()))
    n_row, n_tile, n_out = len(row_extras), len(tile_extras), len(out_dtypes)

    def body(*refs):
        a_ref, b_ref = refs[:2]
        extras = refs[2:2 + n_row + n_tile]
        outs = refs[2 + n_row + n_tile:2 + n_row + n_tile + n_out]
        acc_ref = refs[-1]
        jj, kk = pl.program_id(1), pl.program_id(2)
        part = lax.dot_general(a_ref[...].astype(BF16), b_ref[...].astype(BF16), dims, preferred_element_type=F32)

        def finish(acc):
            res = (acc,) if epilogue is None else epilogue(acc, jj, *[e[...] for e in extras])
            for o_ref, r in zip(outs, res):
                o_ref[...] = r.astype(o_ref.dtype)

        if nk == 1:
            finish(part)
        else:
            @pl.when(kk == 0)
            def _():
                acc_ref[...] = part

            @pl.when(kk > 0)
            def _():
                acc_ref[...] += part

            @pl.when(kk == nk - 1)
            def _():
                finish(acc_ref[...])

    a_spec = pl.BlockSpec((tk, tm), lambda i, j, kk: (kk, i)) if ta else pl.BlockSpec((tm, tk), lambda i, j, kk: (i, kk))
    if b_sharded and tb:
        per = shard_c // tk
        b_spec = pl.BlockSpec((None, tn, tk), lambda i, j, kk: (kk // per, j, kk % per))
    elif b_sharded:
        per = shard_c // tn
        b_spec = pl.BlockSpec((None, tk, tn), lambda i, j, kk: (j // per, kk, j % per))
    elif tb:
        b_spec = pl.BlockSpec((tn, tk), lambda i, j, kk: (j, kk))
    else:
        b_spec = pl.BlockSpec((tk, tn), lambda i, j, kk: (kk, j + n_off // tn))
    in_specs = [a_spec, b_spec]
    in_specs += [pl.BlockSpec((tm, LANES), lambda i, j, kk: (i, 0)) for _ in row_extras]
    in_specs += [pl.BlockSpec((tm, tn), lambda i, j, kk: (i, j)) for _ in tile_extras]
    if out_map is None:
        out_specs = [pl.BlockSpec((tm, tn), lambda i, j, kk: (i, j)) for _ in out_dtypes]
        shapes = [jax.ShapeDtypeStruct((m, n), dt) for dt in out_dtypes]
    else:
        out_specs = [out_map(tm, tn)]
        shapes = [jax.ShapeDtypeStruct(out_shape, out_dtypes[0])]
    acc_shape = (tm, tn) if nk > 1 else (8, LANES)
    res = _pcall(body, name=name, out_shape=shapes, grid=(m // tm, n // tn, nk), in_specs=in_specs,
                 out_specs=out_specs, scratch_shapes=[pltpu.VMEM(acc_shape, F32)])(a, b, *row_extras, *tile_extras)
    return res[0] if n_out == 1 else res


def _rope_cols(acc, j, cos, sin, n_rope):
    width = acc.shape[1]
    parts = []
    for g in range(width // HEAD_DIM):
        xg = acc[:, g * HEAD_DIM:(g + 1) * HEAD_DIM]
        roped = xg * cos + pltpu.roll(xg, HEAD_DIM // 2, 1) * sin
        parts.append(jnp.where(j * width + g * HEAD_DIM < n_rope, roped, xg))
    return jnp.concatenate(parts, axis=1) if len(parts) > 1 else parts[0]


def _silu(v):
    return v / (1.0 + jnp.exp(-v))


def _mod_part(c_all, w_mod, b_part):
    d, w = w_mod.shape
    tk = _blk(d, 512)

    def body(c_ref, w_ref, b_ref, o_ref):
        kk = pl.program_id(0)
        cond = _silu(c_ref[...]).astype(BF16)
        part = jnp.dot(cond, w_ref[...].astype(BF16), preferred_element_type=F32)

        @pl.when(kk == 0)
        def _():
            o_ref[...] = part + b_ref[...]

        @pl.when(kk > 0)
        def _():
            o_ref[...] += part

    return _pcall(body, name="mod_part", out_shape=jax.ShapeDtypeStruct((N_DEV, w), F32), grid=(d // tk,),
                  in_specs=[pl.BlockSpec((N_DEV, tk), lambda kk: (0, kk)), pl.BlockSpec((tk, w), lambda kk: (kk, 0)),
                            pl.BlockSpec((1, w), lambda kk: (0, 0))],
                  out_specs=pl.BlockSpec((N_DEV, w), lambda kk: (0, 0)))(c_all, w_mod, b_part)


def _row_call(body, name, t, d, tiled_in, vec_in, tiled_out_dtypes, n_vec_out, tr=256):
    tr = _blk(t, tr)
    tile = pl.BlockSpec((tr, d), lambda i: (i, 0))
    vec = pl.BlockSpec((1, d), lambda i: (0, 0))
    out_shape = [jax.ShapeDtypeStruct((t, d), dt) for dt in tiled_out_dtypes]
    out_shape += [jax.ShapeDtypeStruct((1, d), F32)] * n_vec_out
    return _pcall(body, name=name, out_shape=out_shape, grid=(t // tr,),
                  in_specs=[tile] * len(tiled_in) + [vec] * len(vec_in),
                  out_specs=[tile] * len(tiled_out_dtypes) + [vec] * n_vec_out)(*tiled_in, *vec_in)


def _accumulate(ref, val):
    @pl.when(pl.program_id(0) == 0)
    def _():
        ref[...] = val

    @pl.when(pl.program_id(0) > 0)
    def _():
        ref[...] += val


def _rsum(v):
    return jnp.sum(v, axis=0, keepdims=True)


def _rms(v):
    return lax.rsqrt(jnp.mean(v * v, axis=-1, keepdims=True) + NORM_EPS)


def _rms_bwd(vhat, r, dvhat):
    return r * (dvhat - vhat * jnp.mean(dvhat * vhat, axis=-1, keepdims=True))


def _pre_attn(x, g0, sc_a, sh_a):
    def body(x_ref, g_ref, sc_ref, sh_ref, h_ref):
        xv = x_ref[...]
        h_ref[...] = (xv * _rms(xv) * g_ref[...] * (1.0 + sc_ref[...]) + sh_ref[...]).astype(BF16)

    t, d = x.shape
    return _row_call(body, "pre_attn", t, d, [x], [g0, sc_a, sh_a], [BF16], 0)[0]


def _post_mix(x, mix, gt_a, g1, g2, sc_m, sh_m):
    def body(x_ref, mix_ref, gt_ref, g1_ref, g2_ref, sc_ref, sh_ref, x2_ref, h2_ref):
        mv = mix_ref[...]
        x2 = x_ref[...] + gt_ref[...] * (mv * _rms(mv) * g1_ref[...])
        x2_ref[...] = x2
        h2_ref[...] = (x2 * _rms(x2) * g2_ref[...] * (1.0 + sc_ref[...]) + sh_ref[...]).astype(BF16)

    t, d = x.shape
    return _row_call(body, "post_mix", t, d, [x, mix], [gt_a, g1, g2, sc_m, sh_m], [F32, BF16], 0)


def _final(y, x2, target, gt_m, g3):
    t, d = y.shape

    def body(y_ref, x2_ref, tg_ref, gt_ref, g3_ref, dy_ref, dout_ref, dgt_ref, dg3_ref, loss_ref):
        yv = y_ref[...]
        r = _rms(yv)
        yhat = yv * r
        n3 = yhat * g3_ref[...]
        err = x2_ref[...] + gt_ref[...] * n3 - tg_ref[...]
        _accumulate(loss_ref, jnp.zeros((1, d), F32) + 0.5 * jnp.sum(err * err) / d)
        dout = err * (1.0 / d)
        dout_ref[...] = dout
        _accumulate(dgt_ref, _rsum(dout * n3))
        dn3 = dout * gt_ref[...]
        _accumulate(dg3_ref, _rsum(dn3 * yhat))
        dy_ref[...] = _rms_bwd(yhat, r, dn3 * g3_ref[...]).astype(BF16)

    return _row_call(body, "final", t, d, [y, x2, target], [gt_m, g3], [BF16, F32], 3)


def _mid_bwd(dh2, dout, x2, mix, g2, sc_m, gt_a, g1):
    t, d = x2.shape

    def body(dh2_ref, dout_ref, x2_ref, mix_ref, g2_ref, sc_ref, gt_ref, g1_ref,
             dmix_ref, dx2_ref, dsh_ref, dsc_ref, dg2_ref, dgt_ref, dg1_ref):
        dh2v = dh2_ref[...]
        x2v = x2_ref[...]
        r2 = _rms(x2v)
        x2hat = x2v * r2
        _accumulate(dsh_ref, _rsum(dh2v))
        _accumulate(dsc_ref, _rsum(dh2v * (x2hat * g2_ref[...])))
        dn2 = dh2v * (1.0 + sc_ref[...])
        _accumulate(dg2_ref, _rsum(dn2 * x2hat))
        dx2 = dout_ref[...] + _rms_bwd(x2hat, r2, dn2 * g2_ref[...])
        dx2_ref[...] = dx2
        mv = mix_ref[...]
        r1 = _rms(mv)
        mhat = mv * r1
        _accumulate(dgt_ref, _rsum(dx2 * (mhat * g1_ref[...])))
        dn1 = dx2 * gt_ref[...]
        _accumulate(dg1_ref, _rsum(dn1 * mhat))
        dmix_ref[...] = _rms_bwd(mhat, r1, dn1 * g1_ref[...]).astype(BF16)

    return _row_call(body, "mid_bwd", t, d, [dh2, dout, x2, mix], [g2, sc_m, gt_a, g1], [BF16, F32], 5)


def _x_bwd(dh1, dx2, x, g0, sc_a):
    t, d = x.shape

    def body(dh1_ref, dx2_ref, x_ref, g0_ref, sc_ref, dx_ref, dsh_ref, dsc_ref, dg0_ref):
        dh1v = dh1_ref[...]
        xv = x_ref[...]
        r0 = _rms(xv)
        xhat = xv * r0
        _accumulate(dsh_ref, _rsum(dh1v))
        _accumulate(dsc_ref, _rsum(dh1v * (xhat * g0_ref[...])))
        dn0 = dh1v * (1.0 + sc_ref[...])
        _accumulate(dg0_ref, _rsum(dn0 * xhat))
        dx_ref[...] = dx2_ref[...] + _rms_bwd(xhat, r0, dn0 * g0_ref[...])

    return _row_call(body, "x_bwd", t, d, [dh1, dx2, x], [g0, sc_a], [F32], 3)


def _pick_lane(block, h):
    lane = lax.broadcasted_iota(jnp.int32, block.shape, 1)
    return jnp.sum(jnp.where(lane == h, block, 0.0), axis=1, keepdims=True)


def _put_lane(ref, rows, h, col):
    old = ref[rows, :]
    lane = lax.broadcasted_iota(jnp.int32, old.shape, 1)
    ref[rows, :] = jnp.where(lane == h, col, old)


def _tri(n, lower):
    r = lax.broadcasted_iota(jnp.int32, (n, n), 0)
    c = lax.broadcasted_iota(jnp.int32, (n, n), 1)
    return jnp.where((c <= r) if lower else (c >= r), 1.0, 0.0).astype(F32)


def _cum_fwd(fg, b128):
    t = fg.shape[0]
    nb = t // LANES

    def body(fg_ref, b_ref, cum_ref, cumt_ref):
        tri = _tri(LANES, True)
        carry = jnp.zeros((1, LANES), F32)
        for i in range(nb):
            z = fg_ref[i * LANES:(i + 1) * LANES, :] + b_ref[...]
            lf = jnp.minimum(z, 0.0) - jnp.log(1.0 + jnp.exp(-jnp.abs(z)))
            blk = jnp.dot(tri, lf, precision=lax.Precision.HIGHEST, preferred_element_type=F32) + carry
            cum_ref[i * LANES:(i + 1) * LANES, :] = blk
            carry = blk[LANES - 1:LANES, :]
        cumt_ref[...] = cum_ref[...].T[0:N_FOX, :]

    return _pcall(body, name="cum_fwd",
                  out_shape=[jax.ShapeDtypeStruct((t, LANES), F32), jax.ShapeDtypeStruct((N_FOX, t), F32)],
                  grid=(1,),
                  in_specs=[pl.BlockSpec((t, LANES), lambda i: (0, 0)), pl.BlockSpec((1, LANES), lambda i: (0, 0))],
                  out_specs=[pl.BlockSpec((t, LANES), lambda i: (0, 0)), pl.BlockSpec((N_FOX, t), lambda i: (0, 0))],
                  )(fg, b128)


def _fg_bwd(dcs_rows, fg, b128):
    t = fg.shape[0]
    nb = t // LANES

    def body(dcs_ref, fg_ref, b_ref, dfg_ref, db_ref, dcum_ref):
        dcum_ref[...] = -jnp.concatenate([dcs_ref[...], jnp.zeros((LANES - N_FOX, t), F32)], axis=0).T
        tri = _tri(LANES, False)
        carry = jnp.zeros((1, LANES), F32)
        db = jnp.zeros((1, LANES), F32)
        for i in reversed(range(nb)):
            rows = slice(i * LANES, (i + 1) * LANES)
            dlf = jnp.dot(tri, dcum_ref[rows, :], precision=lax.Precision.HIGHEST, preferred_element_type=F32) + carry
            carry = dlf[0:1, :]
            z = fg_ref[rows, :] + b_ref[...]
            dfg = dlf / (1.0 + jnp.exp(z))
            dfg_ref[rows, :] = dfg.astype(BF16)
            db = db + _rsum(dfg)
        db_ref[...] = db

    full = pl.BlockSpec((t, LANES), lambda i: (0, 0))
    vec = pl.BlockSpec((1, LANES), lambda i: (0, 0))
    return _pcall(body, name="fg_bwd",
                  out_shape=[jax.ShapeDtypeStruct((t, LANES), BF16), jax.ShapeDtypeStruct((1, LANES), F32)],
                  grid=(1,), in_specs=[pl.BlockSpec((N_FOX, t), lambda i: (0, 0)), full, vec], out_specs=[full, vec],
                  scratch_shapes=[pltpu.VMEM((t, LANES), F32)])(dcs_rows, fg, b128)


def _head_spec(t, col0, div=1):
    return pl.BlockSpec((t, HEAD_DIM), lambda h: (0, col0 + h // div))


def _fox_scores(q, k, cq, ck, i, tq, end):
    s = lax.dot_general(q, k, NT_DIMS, preferred_element_type=F32) * ATT_SCALE + cq - ck
    row = lax.broadcasted_iota(jnp.int32, (tq, end), 0) + i * tq
    col = lax.broadcasted_iota(jnp.int32, (tq, end), 1)
    return jnp.where(row >= col, s, -jnp.inf)


def _fox_fwd(proj_a, cum, cumt):
    t = proj_a.shape[0]
    tq = _blk(t, 512)
    nq = t // tq

    def body(q_ref, k_ref, v_ref, cum_ref, cumt_ref, o_ref, lse_ref):
        h = pl.program_id(0)
        cq_all = _pick_lane(cum_ref[...], h)
        ck_all = cumt_ref[pl.ds(h, 1), :]

        @pl.when(h == 0)
        def _():
            lse_ref[...] = jnp.zeros_like(lse_ref)

        for i in range(nq):
            rows, end = slice(i * tq, (i + 1) * tq), (i + 1) * tq
            s = _fox_scores(q_ref[rows, :], k_ref[0:end, :], cq_all[rows, :], ck_all[:, 0:end], i, tq, end)
            m = jnp.max(s, axis=1, keepdims=True)
            p = jnp.exp(s - m)
            l = jnp.sum(p, axis=1, keepdims=True)
            o = jnp.dot(p.astype(BF16), v_ref[0:end, :], preferred_element_type=F32) / l
            o_ref[rows, :] = o.astype(BF16)
            _put_lane(lse_ref, rows, h, m + jnp.log(l))

    nh = FOX_W // HEAD_DIM
    stat = pl.BlockSpec((t, LANES), lambda h: (0, 0))
    return _pcall(body, name="fox_fwd",
                  out_shape=[jax.ShapeDtypeStruct((t, FOX_W), BF16), jax.ShapeDtypeStruct((t, LANES), F32)],
                  grid=(N_FOX,),
                  in_specs=[_head_spec(t, 0), _head_spec(t, nh), _head_spec(t, 2 * nh), stat,
                            pl.BlockSpec((N_FOX, t), lambda h: (0, 0))],
                  out_specs=[_head_spec(t, 0), stat])(proj_a, proj_a, proj_a, cum, cumt)


def _fox_bwd(proj_a, d_attn, cum, cumt, lse):
    t = proj_a.shape[0]
    tq = _blk(t, 512)
    nq = t // tq

    def body(q_ref, k_ref, v_ref, do_ref, cum_ref, cumt_ref, lse_ref,
             dq_ref, dk_ref, dv_ref, dcs_ref, dk_acc, dv_acc, dcs_acc):
        h = pl.program_id(0)
        cq_all = _pick_lane(cum_ref[...], h)
        ck_all = cumt_ref[pl.ds(h, 1), :]
        lse_all = _pick_lane(lse_ref[...], h)
        dk_acc[...] = jnp.zeros_like(dk_acc)
        dv_acc[...] = jnp.zeros_like(dv_acc)
        dcs_acc[...] = jnp.zeros_like(dcs_acc)
        for i in range(nq):
            rows, end = slice(i * tq, (i + 1) * tq), (i + 1) * tq
            q, k, v, do = q_ref[rows, :], k_ref[0:end, :], v_ref[0:end, :], do_ref[rows, :]
            s = _fox_scores(q, k, cq_all[rows, :], ck_all[:, 0:end], i, tq, end)
            p = jnp.exp(s - lse_all[rows, :])
            dp = lax.dot_general(do, v, NT_DIMS, preferred_element_type=F32)
            ds = p * (dp - jnp.sum(p * dp, axis=1, keepdims=True))
            dcs_acc[:, 0:end] += jnp.sum(ds, axis=0, keepdims=True)
            ds = ds.astype(BF16)
            dq_ref[rows, :] = (jnp.dot(ds, k, preferred_element_type=F32) * ATT_SCALE).astype(BF16)
            dk_acc[0:end, :] += lax.dot_general(ds, q, TN_DIMS, preferred_element_type=F32)
            dv_acc[0:end, :] += lax.dot_general(p.astype(BF16), do, TN_DIMS, preferred_element_type=F32)
        dk_ref[...] = (dk_acc[...] * ATT_SCALE).astype(BF16)
        dv_ref[...] = dv_acc[...].astype(BF16)
        dcs_ref[pl.ds(h, 1), :] = dcs_acc[...]

    nh = FOX_W // HEAD_DIM
    stat = pl.BlockSpec((t, LANES), lambda h: (0, 0))
    rows8 = pl.BlockSpec((N_FOX, t), lambda h: (0, 0))
    head = _head_spec(t, 0)
    wide = jax.ShapeDtypeStruct((t, FOX_W), BF16)
    return _pcall(body, name="fox_bwd",
                  out_shape=[wide, wide, wide, jax.ShapeDtypeStruct((N_FOX, t), F32)],
                  grid=(N_FOX,),
                  in_specs=[_head_spec(t, 0), _head_spec(t, nh), _head_spec(t, 2 * nh), head, stat, rows8, stat],
                  out_specs=[head, head, head, rows8],
                  scratch_shapes=[pltpu.VMEM((t, HEAD_DIM), F32), pltpu.VMEM((t, HEAD_DIM), F32),
                                  pltpu.VMEM((1, t), F32)],
                  )(proj_a, proj_a, proj_a, d_attn, cum, cumt, lse)


def _swa_scores(q, k, i, tq, start, end):
    s = lax.dot_general(q, k, NT_DIMS, preferred_element_type=F32) * ATT_SCALE
    row = lax.broadcasted_iota(jnp.int32, (tq, end - start), 0) + i * tq
    col = lax.broadcasted_iota(jnp.int32, (tq, end - start), 1) + start
    diff = row - col
    return jnp.where((diff >= 0) & (diff < WINDOW), s, -jnp.inf)


def _swa_blocks(t):
    tq = _blk(t, 256)
    return tq, [(i, max(0, i * tq - WINDOW), (i + 1) * tq) for i in range(t // tq)]


def _swa_fwd(proj_b, sinks128):
    t = proj_b.shape[0]
    tq, blocks = _swa_blocks(t)

    def body(q_ref, k_ref, v_ref, sink_ref, o_ref, lse_ref):
        h = pl.program_id(0)
        sink = _pick_lane(sink_ref[...], h)

        @pl.when(h == 0)
        def _():
            lse_ref[...] = jnp.zeros_like(lse_ref)

        for i, start, end in blocks:
            rows = slice(i * tq, end)
            s = _swa_scores(q_ref[rows, :], k_ref[start:end, :], i, tq, start, end)
            m = jnp.maximum(jnp.max(s, axis=1, keepdims=True), sink)
            p = jnp.exp(s - m)
            l = jnp.sum(p, axis=1, keepdims=True) + jnp.exp(sink - m)
            o = jnp.dot(p.astype(BF16), v_ref[start:end, :], preferred_element_type=F32) / l
            o_ref[rows, :] = o.astype(BF16)
            _put_lane(lse_ref, rows, h, m + jnp.log(l))

    stat = pl.BlockSpec((t, LANES), lambda h: (0, 0))
    return _pcall(body, name="swa_fwd",
                  out_shape=[jax.ShapeDtypeStruct((t, SWA_W), BF16), jax.ShapeDtypeStruct((t, LANES), F32)],
                  grid=(N_SWA,),
                  in_specs=[_head_spec(t, 0), _head_spec(t, N_SWA, GQA), _head_spec(t, N_SWA + N_KV, GQA),
                            pl.BlockSpec((1, LANES), lambda h: (0, 0))],
                  out_specs=[_head_spec(t, 0), stat])(proj_b, proj_b, proj_b, sinks128)


def _rope_bwd(d, cos, sin):
    return d * cos + pltpu.roll(d * sin, HEAD_DIM // 2, 1)


def _swa_bwd(proj_b, d_attn, lse, sinks128, cos, sin):
    t = proj_b.shape[0]
    tq, blocks = _swa_blocks(t)

    def body(q_ref, k_ref, v_ref, do_ref, lse_ref, sink_ref, cos_ref, sin_ref,
             dq_ref, dk_ref, dv_ref, dsink_ref, dk_acc, dv_acc):
        h = pl.program_id(0)
        sink = _pick_lane(sink_ref[...], h)
        lse_all = _pick_lane(lse_ref[...], h)

        @pl.when(h == 0)
        def _():
            dsink_ref[...] = jnp.zeros_like(dsink_ref)

        @pl.when(h % GQA == 0)
        def _():
            dk_acc[...] = jnp.zeros_like(dk_acc)
            dv_acc[...] = jnp.zeros_like(dv_acc)

        dsink = jnp.zeros((1, 1), F32)
        for i, start, end in blocks:
            rows = slice(i * tq, end)
            q, k, v, do = q_ref[rows, :], k_ref[start:end, :], v_ref[start:end, :], do_ref[rows, :]
            s = _swa_scores(q, k, i, tq, start, end)
            p = jnp.exp(s - lse_all[rows, :])
            dp = lax.dot_general(do, v, NT_DIMS, preferred_element_type=F32)
            delta = jnp.sum(p * dp, axis=1, keepdims=True)
            ds = (p * (dp - delta)).astype(BF16)
            dq = jnp.dot(ds, k, preferred_element_type=F32) * ATT_SCALE
            dq_ref[rows, :] = _rope_bwd(dq, cos_ref[rows, :], sin_ref[rows, :]).astype(BF16)
            dk_acc[start:end, :] += lax.dot_general(ds, q, TN_DIMS, preferred_element_type=F32)
            dv_acc[start:end, :] += lax.dot_general(p.astype(BF16), do, TN_DIMS, preferred_element_type=F32)
            dsink = dsink - jnp.sum(jnp.exp(sink - lse_all[rows, :]) * delta, axis=0, keepdims=True)
        old = dsink_ref[...]
        lane = lax.broadcasted_iota(jnp.int32, old.shape, 1)
        dsink_ref[...] = jnp.where(lane == h, dsink, old)

        @pl.when(h % GQA == GQA - 1)
        def _():
            dk_ref[...] = _rope_bwd(dk_acc[...] * ATT_SCALE, cos_ref[...], sin_ref[...]).astype(BF16)
            dv_ref[...] = dv_acc[...].astype(BF16)

    stat = pl.BlockSpec((t, LANES), lambda h: (0, 0))
    vec = pl.BlockSpec((1, LANES), lambda h: (0, 0))
    head = _head_spec(t, 0)
    kv_out = _head_spec(t, 0, GQA)
    return _pcall(body, name="swa_bwd",
                  out_shape=[jax.ShapeDtypeStruct((t, SWA_W), BF16), jax.ShapeDtypeStruct((t, KV_W), BF16),
                             jax.ShapeDtypeStruct((t, KV_W), BF16), jax.ShapeDtypeStruct((1, LANES), F32)],
                  grid=(N_SWA,),
                  in_specs=[head, _head_spec(t, N_SWA, GQA), _head_spec(t, N_SWA + N_KV, GQA),
                            _head_spec(t, N_FOX), stat, vec, stat, stat],
                  out_specs=[head, kv_out, kv_out, vec],
                  scratch_shapes=[pltpu.VMEM((t, HEAD_DIM), F32), pltpu.VMEM((t, HEAD_DIM), F32)],
                  )(proj_b, proj_b, proj_b, d_attn, lse, sinks128, cos, sin)


def _adamw(w, g, m, v):
    m = ADAM_B1 * m + (1.0 - ADAM_B1) * g
    v = ADAM_B2 * v + (1.0 - ADAM_B2) * (g * g)
    m_hat = m / (1.0 - ADAM_B1 ** ADAM_STEP)
    v_hat = v / (1.0 - ADAM_B2 ** ADAM_STEP)
    delta = -ADAM_LR * (m_hat / (jnp.sqrt(v_hat) + ADAM_EPS) + ADAM_WD * w)
    return delta, m, v


def _adam_pieces(w, m, v, own, land, idx, name):
    rows, cols = w.shape
    tr = 256 if rows % 256 == 0 else rows

    def body(idx_ref, own_ref, l1_ref, l2_ref, l3_ref, w_ref, m_ref, v_ref, g_ref, d_ref, mo_ref, vo_ref):
        g = own_ref[...].astype(F32) + l1_ref[...].astype(F32) + l2_ref[...].astype(F32) + l3_ref[...].astype(F32)
        g_ref[...] = g
        d_ref[...], mo_ref[...], vo_ref[...] = _adamw(w_ref[...], g, m_ref[...], v_ref[...])

    def piece(p):
        return pl.BlockSpec((None, tr, cols), lambda i, idx_ref: (idx_ref[p], i, 0))

    tile = pl.BlockSpec((tr, cols), lambda i, idx_ref: (i, 0))
    out = jax.ShapeDtypeStruct((rows, cols), F32)
    grid_spec = pltpu.PrefetchScalarGridSpec(
        num_scalar_prefetch=1, grid=(rows // tr,),
        in_specs=[piece(0), piece(1), piece(2), piece(3), tile, tile, tile], out_specs=[tile] * 4)
    return _pcall(body, name=name, out_shape=[out] * 4, grid_spec=grid_spec)(idx, own, land, land, land, w, m, v)


def _adam_mod(c_all, dmod_cols, w, m, v):
    rows, cols = w.shape
    tr = _blk(rows, 256)

    def body(c_ref, dm_ref, w_ref, m_ref, v_ref, g_ref, d_ref, mo_ref, vo_ref):
        cond = _silu(c_ref[...]).astype(BF16)
        g = lax.dot_general(cond, dm_ref[...].astype(BF16), TN_DIMS, preferred_element_type=F32)
        g_ref[...] = g
        d_ref[...], mo_ref[...], vo_ref[...] = _adamw(w_ref[...], g, m_ref[...], v_ref[...])

    tile = pl.BlockSpec((tr, cols), lambda i: (i, 0))
    out = jax.ShapeDtypeStruct((rows, cols), F32)
    return _pcall(body, name="adam_mod", out_shape=[out] * 4, grid=(rows // tr,),
                  in_specs=[pl.BlockSpec((N_DEV, tr), lambda i: (0, i)), pl.BlockSpec((N_DEV, cols), lambda i: (0, 0)),
                            tile, tile, tile],
                  out_specs=[tile] * 4)(c_all, dmod_cols, w, m, v)


def _adam_small(parts, w, m, v):
    nv = w.shape[1]

    def body(p_ref, w_ref, m_ref, v_ref, g_ref, d_ref, mo_ref, vo_ref):
        g = p_ref[0:1, :]
        for k in range(1, N_DEV):
            g = g + p_ref[k:k + 1, :]
        g_ref[...] = g
        d_ref[...], mo_ref[...], vo_ref[...] = _adamw(w_ref[...], g, m_ref[...], v_ref[...])

    vec = pl.BlockSpec((1, nv), lambda i: (0, 0))
    out = jax.ShapeDtypeStruct((1, nv), F32)
    return _pcall(body, name="adam_small", out_shape=[out] * 4, grid=(1,),
                  in_specs=[pl.BlockSpec((N_DEV, nv), lambda i: (0, 0)), vec, vec, vec],
                  out_specs=[vec] * 4)(parts, w, m, v)


def _pad_lanes(v, width=LANES):
    return jnp.pad(v, ((0, 0), (0, width - v.shape[1])))


def kernel(x, c, w_mod, b_mod, g_pre_mix, g_post_mix, w_in, b_forget, swa_sinks, w_out, g_pre_mlp, g_post_mlp, w_up, w_down, loss_target, m_w_mod, m_b_mod, m_g_pre_mix, m_g_post_mix, m_w_in, m_b_forget, m_swa_sinks, m_w_out, m_g_pre_mlp, m_g_post_mlp, m_w_up, m_w_down, v_w_mod, v_b_mod, v_g_pre_mix, v_g_post_mix, v_w_in, v_b_forget, v_swa_sinks, v_w_out, v_g_pre_mlp, v_g_post_mlp, v_w_up, v_w_down):
    ax, ay, ac = _position()
    me = 4 * ax + 2 * ay + ac
    x, target = x[0], loss_target[0]
    t, d = x.shape
    w_mod, w_in, w_out, w_up, w_down = w_mod[0], w_in[0], w_out[0], w_up[0], w_down[0]
    mod_w = w_mod.shape[1]
    in_w = w_in.shape[1]
    in_total = N_DEV * in_w
    shard_ff = w_up.shape[1]
    n_fox3 = 3 * FOX_W
    n_swa3 = SWA_W + 2 * KV_W
    assert in_total == n_fox3 + N_FOX + n_swa3 and d == FOX_W + SWA_W

    c_all = _all_gather([c], "gather_c")[0].reshape(N_DEV, d)
    b_part = lax.dynamic_slice(b_mod, (0, me * mod_w), (1, mod_w))
    mod_parts = _all_gather([_mod_part(c_all, w_mod, b_part)], "gather_mod")[0]
    mod = lax.dynamic_index_in_dim(mod_parts, me, axis=1, keepdims=False).reshape(1, N_DEV * mod_w)

    shards = [w_in.astype(BF16), w_out.astype(BF16), w_up.astype(BF16), w_down.astype(BF16)]
    shards, mod = lax.optimization_barrier((shards, mod))
    ag_send, ag_recv, ag_shard, ag_land, ag_token = _ag_start(shards)

    def gathered(i, after, name):
        shard, land = _ag_wait(ag_send[i], ag_recv[i], ag_shard[i], ag_land[i], after, "ag_wait_" + name)
        return lax.dynamic_update_slice(_ag_forward(land, "ag_fwd_" + name), shard[None], (me, 0, 0))

    sh_a, sc_a, gt_a, sh_m, sc_m, gt_m = [mod[:, i * d:(i + 1) * d] for i in range(6)]

    half = HEAD_DIM // 2
    inv_freq = 1.0 / (ROPE_THETA ** (jnp.arange(half, dtype=F32) * (2.0 / HEAD_DIM)))
    ang = jnp.arange(t).astype(F32)[:, None] * inv_freq[None, :]
    cos = jnp.concatenate([jnp.cos(ang), jnp.cos(ang)], axis=1)
    sin = jnp.concatenate([-jnp.sin(ang), jnp.sin(ang)], axis=1)

    b128 = _pad_lanes(b_forget)
    sinks128 = _pad_lanes(swa_sinks)

    h1 = _pre_attn(x, g_pre_mix + ag_token[0:1, 0:1], sc_a, sh_a)
    w_in_g = gathered(0, h1, "in")
    o_fg, o_sq, o_sv = n_fox3, n_fox3 + N_FOX, n_fox3 + N_FOX + SWA_W + KV_W

    def cols(lo, hi):
        parts = []
        for j in range(lo // in_w, (hi - 1) // in_w + 1):
            parts.append(w_in_g[j, :, max(lo - j * in_w, 0):min(hi - j * in_w, in_w)])
        return parts

    w_in_r = jnp.concatenate(cols(0, o_fg) + cols(o_sq, in_total) + cols(o_fg, o_sq)
                             + [jnp.zeros((d, FG_PAD - N_FOX), BF16)], axis=1)
    proj_a = _matmul(h1, w_in_r, name="proj_a", n_cols=n_fox3, n_off=0)
    proj_b = _matmul(h1, w_in_r, name="proj_b", n_cols=n_swa3, n_off=n_fox3, tn=512, row_extras=(cos, sin),
                     epilogue=lambda acc, j, cs, sn: (_rope_cols(acc, j, cs, sn, SWA_W + KV_W),))
    fg = _matmul(h1, w_in_r, name="proj_fg", n_cols=FG_PAD, n_off=n_fox3 + n_swa3, tn=FG_PAD,
                 out_dtypes=(F32,))[:, 0:LANES]
    cum, cumt = _cum_fwd(fg, b128)
    fox_o, fox_lse = _fox_fwd(proj_a, cum, cumt)
    swa_o, swa_lse = _swa_fwd(proj_b, sinks128)
    w_out_full = gathered(1, swa_o, "out").reshape(d, d)
    attn = jnp.concatenate([fox_o, swa_o], axis=1)
    mix = _matmul(attn, w_out_full, name="out_proj", out_dtypes=(F32,))
    x2, h2 = _post_mix(x, mix, gt_a, g_post_mix, g_pre_mlp, sc_m, sh_m)
    w_up_g = gathered(2, h2, "up")
    u, act = _matmul(h2, w_up_g, name="mlp_up", b_sharded=True, out_dtypes=(BF16, BF16),
                     epilogue=lambda acc, j: (acc, jnp.square(jnp.maximum(acc, 0.0))))
    w_down_full = gathered(3, act, "down").reshape(N_DEV * shard_ff, d)
    y = _matmul(act, w_down_full, name="mlp_down", out_dtypes=(F32,))

    core = jnp.reshape(ac, (1,)).astype(jnp.int32)

    def reduce_start(full, name):
        from_sibling = _rs_sibling([full], "rs_sibling_" + name)[0]
        return _rs_start(_chip_sum(full, from_sibling, core, "chip_sum_" + name), "rs_start_" + name)

    idx = jnp.stack([2 * ax + ay, 2 * (1 - ax) + ay, 2 * ax + (1 - ay), 2 * (1 - ax) + (1 - ay)]).astype(jnp.int32)

    def reduce_finish(started, after, w, m, v, name):
        send, recv, src, land, _ = started
        own, landed = _rs_wait(send, recv, src, land, after, "rs_wait_" + name)
        return _adam_pieces(w, m[0], v[0], own, landed, idx, "adam_" + name)

    dy, dout, dgt_m, dg3, loss_vec = _final(y, x2, target, gt_m, g_post_mlp)
    du = _matmul(dy, w_down_full, name="d_act", tb=True, tile_extras=(u,),
                 epilogue=lambda acc, j, uu: (acc * (2.0 * jnp.maximum(uu.astype(F32), 0.0)),))
    dw_down = _matmul(act, dy, name="dw_down", ta=True)
    rs_down = reduce_start(dw_down.reshape(N_DEV, shard_ff, d), "down")
    dh2 = _matmul(du, w_up_g, name="d_h2", tb=True, b_sharded=True, out_dtypes=(F32,))
    per = shard_ff // _blk(shard_ff, 1024)
    dw_up = _matmul(h2, du, name="dw_up", ta=True, tn=_blk(shard_ff, 1024), out_shape=(N_DEV, d, shard_ff),
                    out_map=lambda tm, tn: pl.BlockSpec((None, tm, tn), lambda i, j, kk: (j // per, i, j % per)))
    rs_up = reduce_start(dw_up, "up")
    dmix, dx2, dsh_m, dsc_m, dg2, dgt_a, dg1 = _mid_bwd(
        dh2, dout, x2, mix, g_pre_mlp + (rs_down[4][0:1, 0:1] + rs_up[4][0:1, 0:1]), sc_m, gt_a, g_post_mix)
    d_attn = _matmul(dmix, w_out_full, name="d_attn", tb=True)
    dw_out = _matmul(attn, dmix, name="dw_out", ta=True)
    rs_out = reduce_start(dw_out.reshape(N_DEV, d // N_DEV, d), "out")
    dqf, dkf, dvf, dcs = _fox_bwd(proj_a, d_attn, cum, cumt, fox_lse)
    dsq, dsk, dsv, dsinks = _swa_bwd(proj_b, d_attn, swa_lse, sinks128, cos, sin)
    dfg, db_forget = _fg_bwd(dcs, fg, b128 + rs_out[4][0:1, 0:1])
    dproj = jnp.concatenate([dqf, dkf, dvf, dsq, dsk, dsv, _pad_lanes(dfg, FG_PAD)], axis=1)
    dh1 = _matmul(dproj, w_in_r, name="d_h1", tb=True, out_dtypes=(F32,))
    dw_in_r = _matmul(h1, dproj, name="dw_in", ta=True)

    def shard_cols(j):
        lo, hi = j * in_w, (j + 1) * in_w
        parts = []
        for seg_lo, seg_hi, shift in ((0, o_fg, 0), (o_fg, o_sq, n_swa3), (o_sq, in_total, -N_FOX)):
            a, b = max(lo, seg_lo), min(hi, seg_hi)
            if a < b:
                parts.append(dw_in_r[:, a + shift:b + shift])
        return parts[0] if len(parts) == 1 else jnp.concatenate(parts, axis=1)

    rs_in = reduce_start(jnp.stack([shard_cols(j) for j in range(N_DEV)]), "in")
    grad_x, dsh_a, dsc_a, dg0 = _x_bwd(dh1, dx2, x, g_pre_mix + rs_in[4][0:1, 0:1], sc_a)

    g_w_down, d_w_down, nm_w_down, nv_w_down = reduce_finish(rs_down, grad_x, w_down, m_w_down, v_w_down, "w_down")
    g_w_up, d_w_up, nm_w_up, nv_w_up = reduce_finish(rs_up, d_w_down, w_up, m_w_up, v_w_up, "w_up")
    g_w_out, d_w_out, nm_w_out, nv_w_out = reduce_finish(rs_out, d_w_up, w_out, m_w_out, v_w_out, "w_out")

    small = jnp.concatenate([dsh_a, dsc_a, dgt_a, dsh_m, dsc_m, dgt_m, dg0, dg1, dg2, dg3, db_forget, dsinks,
                             loss_vec[:, 0:LANES]], axis=1)
    small, d_w_out = lax.optimization_barrier((small, d_w_out))
    small_all = _all_gather([small], "gather_small")[0].reshape(N_DEV, small.shape[1])
    pack = lambda bm, g0_, g1_, g2_, g3_, bf_, sk_: jnp.concatenate(
        [bm, g0_, g1_, g2_, g3_, _pad_lanes(bf_), _pad_lanes(sk_), jnp.zeros((1, LANES), F32)], axis=1)
    p_small = pack(b_mod, g_pre_mix, g_post_mix, g_pre_mlp, g_post_mlp, b_forget, swa_sinks)
    m_small = pack(m_b_mod, m_g_pre_mix, m_g_post_mix, m_g_pre_mlp, m_g_post_mlp, m_b_forget, m_swa_sinks)
    v_small = pack(v_b_mod, v_g_pre_mix, v_g_post_mix, v_g_pre_mlp, v_g_post_mlp, v_b_forget, v_swa_sinks)
    small_out = _adam_small(small_all, p_small, m_small, v_small)

    n_mod = 6 * d

    def unpack(vec):
        o = n_mod
        return (vec[:, 0:n_mod], vec[:, o:o + d], vec[:, o + d:o + 2 * d], vec[:, o + 2 * d:o + 3 * d],
                vec[:, o + 3 * d:o + 4 * d], vec[:, o + 4 * d:o + 4 * d + N_FOX],
                vec[:, o + 4 * d + LANES:o + 4 * d + LANES + N_SWA])

    loss = small_out[0][0, n_mod + 4 * d + 2 * LANES]
    g_small, d_small, nm_small, nv_small = [unpack(vec) for vec in small_out]

    dmod_cols = lax.dynamic_slice(small_all, (0, me * mod_w), (N_DEV, mod_w))
    g_w_mod, d_w_mod, nm_w_mod, nv_w_mod = _adam_mod(c_all, dmod_cols, w_mod, m_w_mod[0], v_w_mod[0])
    g_w_in, d_w_in, nm_w_in, nv_w_in = reduce_finish(rs_in, d_w_mod, w_in, m_w_in, v_w_in, "w_in")

    def assemble(w_mod_, small_, w_in_, w_out_, w_up_, w_down_):
        b_mod_, g0_, g1_, g2_, g3_, bf_, sk_ = small_
        return [w_mod_[None], b_mod_, g0_, g1_, w_in_[None], bf_, sk_, w_out_[None], g2_, g3_, w_up_[None], w_down_[None]]

    outs = [loss, grad_x[None]]
    outs += assemble(g_w_mod, g_small, g_w_in, g_w_out, g_w_up, g_w_down)
    outs += assemble(d_w_mod, d_small, d_w_in, d_w_out, d_w_up, d_w_down)
    outs += assemble(nm_w_mod, nm_small, nm_w_in, nm_w_out, nm_w_up, nm_w_down)
    outs += assemble(nv_w_mod, nv_small, nv_w_in, nv_w_out, nv_w_up, nv_w_down)
    return tuple(outs)
```

```python
import functools

import jax
import jax.numpy as jnp
from jax import lax
from jax.experimental import pallas as pl
from jax.experimental.pallas import tpu as pltpu

F32 = jnp.float32
BF16 = jnp.bfloat16
MESH = pl.DeviceIdType.MESH

N_DEV = 8
N_CHIP = 4
LANES = 128
HEAD_DIM = 128
N_FOX = 8
N_SWA = 8
N_KV = 2
GQA = N_SWA // N_KV
WINDOW = 128
FOX_W = N_FOX * HEAD_DIM
SWA_W = N_SWA * HEAD_DIM
KV_W = N_KV * HEAD_DIM
ROPE_THETA = 10000.0
NORM_EPS = 1e-6
ATT_SCALE = HEAD_DIM ** -0.5
FG_PAD = 512

ADAM_LR = 0.001
ADAM_B1 = 0.9
ADAM_B2 = 0.999
ADAM_EPS = 1e-08
ADAM_WD = 0.01
ADAM_STEP = 10

VMEM_LIMIT = 56 * 1024 * 1024

NT_DIMS = (((1,), (1,)), ((), ()))
TN_DIMS = (((0,), (0,)), ((), ()))
NN_DIMS = (((1,), (0,)), ((), ()))


def _pcall(body, *, name, out_shape, grid=(), in_specs=None, out_specs=None, scratch_shapes=(), grid_spec=None):
    params = pltpu.CompilerParams(vmem_limit_bytes=VMEM_LIMIT)
    if grid_spec is not None:
        return pl.pallas_call(body, name=name, out_shape=out_shape, grid_spec=grid_spec, compiler_params=params)
    return pl.pallas_call(body, name=name, out_shape=out_shape, grid=grid, in_specs=in_specs, out_specs=out_specs,
                          scratch_shapes=scratch_shapes, compiler_params=params)


def _blk(n, pref):
    if n <= pref:
        return n
    b = (pref // LANES) * LANES
    while n % b:
        b -= LANES
    return b


def _position():
    return lax.axis_index("x"), lax.axis_index("y"), lax.axis_index("c")


ANY = pl.BlockSpec(memory_space=pl.ANY)


def _all_gather(arrs, name):
    n = len(arrs)

    def body(*refs):
        ins, outs = refs[:n], refs[n:2 * n]
        send_sems, recv_sems, local_sems = refs[2 * n:]
        x, y, c = _position()
        me, sibling = (x, y, c), (x, y, 1 - c)
        chips = [(1 - x, y), (x, 1 - y), (1 - x, 1 - y)]

        def slot(p):
            return 4 * p[0] + 2 * p[1] + p[2]

        def copy(a, k, block, to, src=None):
            dst = outs[a].at[slot(block)]
            return pltpu.make_async_remote_copy(
                src_ref=dst if src is None else src, dst_ref=dst,
                send_sem=send_sems.at[7 * a + k], recv_sem=recv_sems.at[7 * a + k],
                device_id=to, device_id_type=MESH)

        mine = [pltpu.make_async_copy(ins[a], outs[a].at[slot(me)], local_sems.at[a]) for a in range(n)]
        for cp in mine:
            cp.start()
        first = []
        for a in range(n):
            first.append(copy(a, 0, me, sibling, src=ins[a]))
            first += [copy(a, 1 + j, me, (*chip, c), src=ins[a]) for j, chip in enumerate(chips)]
        for cp in first:
            cp.start()
        passed = []
        for a in range(n):
            for j, chip in enumerate(chips):
                copy(a, 1 + j, (*chip, c), me).wait_recv()
                cp = copy(a, 4 + j, (*chip, c), sibling)
                cp.start()
                passed.append(cp)
        for a in range(n):
            copy(a, 0, sibling, me).wait_recv()
            for j, chip in enumerate(chips):
                copy(a, 4 + j, (*chip, 1 - c), me).wait_recv()
        for cp in first + passed:
            cp.wait_send()
        for cp in mine:
            cp.wait()

    return _pcall(
        body, name=name,
        out_shape=[jax.ShapeDtypeStruct((N_DEV,) + a.shape, a.dtype) for a in arrs],
        in_specs=[ANY] * n, out_specs=[ANY] * n,
        scratch_shapes=[pltpu.SemaphoreType.DMA((7 * n,)), pltpu.SemaphoreType.DMA((7 * n,)),
                        pltpu.SemaphoreType.DMA((n,))],
    )(*arrs)


HBM = pl.BlockSpec(memory_space=pltpu.HBM)
SEM = pl.BlockSpec(memory_space=pltpu.SEMAPHORE)
EFFECT = pltpu.SideEffectType.DATAFLOW_SIDE_EFFECTING


def _hbm(a):
    return pltpu.with_memory_space_constraint(a, pltpu.HBM)


def _gather_peers():
    x, y, c = _position()
    return [(x, y, 1 - c), (1 - x, y, c), (x, 1 - y, c), (1 - x, 1 - y, c)]


def _ag_start(shards):
    n = len(shards)
    lands = [_hbm(lax.empty((N_DEV,) + s.shape, s.dtype)) for s in shards]

    def body(*refs):
        srcs, land, send, recv = refs[:n], refs[n:2 * n], refs[2 * n:3 * n], refs[3 * n:4 * n]
        token = refs[6 * n]
        x, y, c = _position()
        for a in range(n):
            for k, to in enumerate(_gather_peers()):
                pltpu.make_async_remote_copy(
                    src_ref=srcs[a], dst_ref=land[a].at[4 * x + 2 * y + c], send_sem=send[a].at[k],
                    recv_sem=recv[a].at[k], device_id=to, device_id_type=MESH).start()
        token[...] = jnp.zeros_like(token)

    sems = [pltpu.SemaphoreType.DMA((4,))] * (2 * n)
    out = pl.pallas_call(
        body, name="ag_start",
        out_shape=sems + [pltpu.HBM(s.shape, s.dtype) for s in shards] + [pltpu.HBM(l.shape, l.dtype) for l in lands]
        + [jax.ShapeDtypeStruct((8, LANES), F32)],
        in_specs=[HBM] * (2 * n), out_specs=[SEM] * (2 * n) + [HBM] * (2 * n) + [pl.BlockSpec(memory_space=pltpu.VMEM)],
        input_output_aliases={**{a: 2 * n + a for a in range(n)}, **{n + a: 3 * n + a for a in range(n)}},
        compiler_params=pltpu.CompilerParams(has_side_effects=EFFECT),
    )(*[_hbm(s) for s in shards], *lands)
    return out[:n], out[n:2 * n], out[2 * n:3 * n], out[3 * n:4 * n], out[4 * n]


def _ag_wait(send, recv, shard_thru, land_thru, after, name):
    def body(v_ref, land_ref, send_sem, recv_sem, after_ref, v_dead, got_ref):
        for k, to in enumerate(_gather_peers()):
            cp = pltpu.make_async_remote_copy(
                src_ref=v_ref, dst_ref=land_ref.at[0], send_sem=send_sem.at[k], recv_sem=recv_sem.at[k],
                device_id=to, device_id_type=MESH)
            cp.wait_send()
            cp.wait_recv()

    return pl.pallas_call(
        body, name=name,
        out_shape=(pltpu.HBM(shard_thru.shape, shard_thru.dtype), pltpu.HBM(land_thru.shape, land_thru.dtype)),
        in_specs=(HBM, HBM, SEM, SEM, ANY), out_specs=(HBM, HBM), input_output_aliases={0: 0, 1: 1},
        compiler_params=pltpu.CompilerParams(has_side_effects=EFFECT),
    )(shard_thru, land_thru, send, recv, after)


def _ag_forward(land, name):
    def body(land_in, land_ref, send_sems, recv_sems):
        x, y, c = _position()
        copies = []
        for j, (px, py) in enumerate([(1 - x, y), (x, 1 - y), (1 - x, 1 - y)]):
            block = land_ref.at[4 * px + 2 * py + c]
            cp = pltpu.make_async_remote_copy(src_ref=block, dst_ref=block, send_sem=send_sems.at[j],
                                              recv_sem=recv_sems.at[j], device_id=(x, y, 1 - c), device_id_type=MESH)
            cp.start()
            copies.append(cp)
        for cp in copies:
            cp.wait()

    return pl.pallas_call(
        body, name=name, out_shape=jax.ShapeDtypeStruct(land.shape, land.dtype),
        in_specs=[ANY], out_specs=ANY, input_output_aliases={0: 0},
        scratch_shapes=[pltpu.SemaphoreType.DMA((3,)), pltpu.SemaphoreType.DMA((3,))],
    )(land)


def _rs_peers():
    x, y, c = _position()
    return [(1 - x, y, c), (x, 1 - y, c), (1 - x, 1 - y, c)]


def _rs_start(chip_sums, name):
    land = _hbm(lax.empty(chip_sums.shape, chip_sums.dtype))

    def body(src, land_ref, send, recv, src_thru, land_thru, token):
        x, y, c = _position()
        for j, (px, py, pc) in enumerate(_rs_peers()):
            pltpu.make_async_remote_copy(
                src_ref=src.at[2 * px + py], dst_ref=land_ref.at[2 * x + y], send_sem=send.at[j], recv_sem=recv.at[j],
                device_id=(px, py, pc), device_id_type=MESH).start()
        token[...] = jnp.zeros_like(token)

    return pl.pallas_call(
        body, name=name,
        out_shape=[pltpu.SemaphoreType.DMA((3,)), pltpu.SemaphoreType.DMA((3,)),
                   pltpu.HBM(chip_sums.shape, chip_sums.dtype), pltpu.HBM(land.shape, land.dtype),
                   jax.ShapeDtypeStruct((8, LANES), F32)],
        in_specs=[HBM, HBM], out_specs=[SEM, SEM, HBM, HBM, pl.BlockSpec(memory_space=pltpu.VMEM)],
        input_output_aliases={0: 2, 1: 3},
        compiler_params=pltpu.CompilerParams(has_side_effects=EFFECT),
    )(_hbm(chip_sums), land)


def _rs_wait(send, recv, src_thru, land_thru, after, name):
    def body(src, land_ref, send_sem, recv_sem, after_ref, src_out, land_out):
        for j, to in enumerate(_rs_peers()):
            cp = pltpu.make_async_remote_copy(
                src_ref=src.at[0], dst_ref=land_ref.at[0], send_sem=send_sem.at[j], recv_sem=recv_sem.at[j],
                device_id=to, device_id_type=MESH)
            cp.wait_send()
            cp.wait_recv()

    return pl.pallas_call(
        body, name=name,
        out_shape=(pltpu.HBM(src_thru.shape, src_thru.dtype), pltpu.HBM(land_thru.shape, land_thru.dtype)),
        in_specs=(HBM, HBM, SEM, SEM, ANY), out_specs=(HBM, HBM), input_output_aliases={0: 0, 1: 1},
        compiler_params=pltpu.CompilerParams(has_side_effects=EFFECT),
    )(src_thru, land_thru, send, recv, after)


def _sib_start(full, name):
    land = _hbm(lax.empty((N_CHIP,) + full.shape[1:], full.dtype))

    def body(src, land_ref, send, recv, src_thru, land_thru, token):
        x, y, c = _position()
        for k in range(N_CHIP):
            pltpu.make_async_remote_copy(
                src_ref=src.at[2 * k + (1 - c)], dst_ref=land_ref.at[k], send_sem=send.at[k], recv_sem=recv.at[k],
                device_id=(x, y, 1 - c), device_id_type=MESH).start()
        token[...] = jnp.zeros_like(token)

    return pl.pallas_call(
        body, name=name,
        out_shape=[pltpu.SemaphoreType.DMA((N_CHIP,)), pltpu.SemaphoreType.DMA((N_CHIP,)),
                   pltpu.HBM(full.shape, full.dtype), pltpu.HBM(land.shape, land.dtype),
                   jax.ShapeDtypeStruct((8, LANES), F32)],
        in_specs=[HBM, HBM], out_specs=[SEM, SEM, HBM, HBM, pl.BlockSpec(memory_space=pltpu.VMEM)],
        input_output_aliases={0: 2, 1: 3},
        compiler_params=pltpu.CompilerParams(has_side_effects=EFFECT),
    )(_hbm(full), land)


def _sib_wait(send, recv, src_thru, land_thru, after, name):
    def body(src, land_ref, send_sem, recv_sem, after_ref, src_out, land_out):
        x, y, c = _position()
        for k in range(N_CHIP):
            cp = pltpu.make_async_remote_copy(
                src_ref=src.at[0], dst_ref=land_ref.at[0], send_sem=send_sem.at[k], recv_sem=recv_sem.at[k],
                device_id=(x, y, 1 - c), device_id_type=MESH)
            cp.wait_send()
            cp.wait_recv()

    return pl.pallas_call(
        body, name=name,
        out_shape=(pltpu.HBM(src_thru.shape, src_thru.dtype), pltpu.HBM(land_thru.shape, land_thru.dtype)),
        in_specs=(HBM, HBM, SEM, SEM, ANY), out_specs=(HBM, HBM), input_output_aliases={0: 0, 1: 1},
        compiler_params=pltpu.CompilerParams(has_side_effects=EFFECT),
    )(src_thru, land_thru, send, recv, after)


def _rs_sibling(arrs, name):
    n = len(arrs)

    def body(*refs):
        ins, outs = refs[:n], refs[n:2 * n]
        send_sems, recv_sems = refs[2 * n:]
        x, y, c = _position()
        copies = []
        for a in range(n):
            for k in range(N_CHIP):
                cp = pltpu.make_async_remote_copy(
                    src_ref=ins[a].at[2 * k + (1 - c)], dst_ref=outs[a].at[k],
                    send_sem=send_sems.at[N_CHIP * a + k], recv_sem=recv_sems.at[N_CHIP * a + k],
                    device_id=(x, y, 1 - c), device_id_type=MESH)
                cp.start()
                copies.append(cp)
        for cp in copies:
            cp.wait()

    return _pcall(
        body, name=name,
        out_shape=[jax.ShapeDtypeStruct((N_CHIP,) + a.shape[1:], a.dtype) for a in arrs],
        in_specs=[ANY] * n, out_specs=[ANY] * n,
        scratch_shapes=[pltpu.SemaphoreType.DMA((N_CHIP * n,)), pltpu.SemaphoreType.DMA((N_CHIP * n,))],
    )(*arrs)


def _chip_sum(full, recv, core, name):
    _, rows, cols = full.shape
    tr = _blk(rows, 1024) if rows % LANES == 0 else rows

    def body(core_ref, a_ref, b_ref, o_ref):
        o_ref[...] = (a_ref[...].astype(F32) + b_ref[...].astype(F32)).astype(o_ref.dtype)

    grid_spec = pltpu.PrefetchScalarGridSpec(
        num_scalar_prefetch=1, grid=(N_CHIP, rows // tr),
        in_specs=[pl.BlockSpec((None, tr, cols), lambda k, i, core_ref: (2 * k + core_ref[0], i, 0)),
                  pl.BlockSpec((None, tr, cols), lambda k, i, core_ref: (k, i, 0))],
        out_specs=pl.BlockSpec((None, tr, cols), lambda k, i, core_ref: (k, i, 0)))
    return _pcall(body, name=name, out_shape=jax.ShapeDtypeStruct((N_CHIP, rows, cols), full.dtype),
                  grid_spec=grid_spec)(core, full, recv)


def _matmul(a, b, *, name, ta=False, tb=False, tm=1024, tn=1024, tk=2048, out_dtypes=(BF16,), epilogue=None,
            row_extras=(), tile_extras=(), out_shape=None, out_map=None, b_sharded=False, n_cols=None, n_off=0,
            after=None):
    m, k = (a.shape[1], a.shape[0]) if ta else a.shape
    if b_sharded:
        shard_c = b.shape[2]
        n, kb = (b.shape[1], N_DEV * shard_c) if tb else (N_DEV * shard_c, b.shape[1])
        tn, tk = (tn, min(tk, shard_c)) if tb else (min(tn, shard_c), tk)
    else:
        n, kb = b.shape if tb else (b.shape[1], b.shape[0])
    assert kb == k, (name, kb, k)
    if n_cols is not None:
        n = n_cols
    tm, tn, tk = _blk(m, tm), _blk(n, tn), _blk(k, tk)
    assert n_off % tn == 0
    nk = k // tk
    dims = (((0 if ta else 1,), (1 if tb else 0,)), ((), ()))
    behind = () if after is None else (after,)
    n_row, n_tile, n_out = len(row_extras), len(tile_extras), len(out_dtypes)
    first_out = 2 + n_row + n_tile + len(behind)

    def body(*refs):
        a_ref, b_ref = refs[:2]
        extras = refs[2:2 + n_row + n_tile]
        outs = refs[first_out:first_out + n_out]
        acc_ref = refs[-1]
        jj, kk = pl.program_id(1), pl.program_id(2)
        part = lax.dot_general(a_ref[...].astype(BF16), b_ref[...].astype(BF16), dims, preferred_element_type=F32)

        def finish(acc):
            res = (acc,) if epilogue is None else epilogue(acc, jj, *[e[...] for e in extras])
            for o_ref, r in zip(outs, res):
                o_ref[...] = r.astype(o_ref.dtype)

        if nk == 1:
            finish(part)
        else:
            @pl.when(kk == 0)
            def _():
                acc_ref[...] = part

            @pl.when(kk > 0)
            def _():
                acc_ref[...] += part

            @pl.when(kk == nk - 1)
            def _():
                finish(acc_ref[...])

    a_spec = pl.BlockSpec((tk, tm), lambda i, j, kk: (kk, i)) if ta else pl.BlockSpec((tm, tk), lambda i, j, kk: (i, kk))
    if b_sharded and tb:
        per = shard_c // tk
        b_spec = pl.BlockSpec((None, tn, tk), lambda i, j, kk: (kk // per, j, kk % per))
    elif b_sharded:
        per = shard_c // tn
        b_spec = pl.BlockSpec((None, tk, tn), lambda i, j, kk: (j // per, kk, j % per))
    elif tb:
        b_spec = pl.BlockSpec((tn, tk), lambda i, j, kk: (j, kk))
    else:
        b_spec = pl.BlockSpec((tk, tn), lambda i, j, kk: (kk, j + n_off // tn))
    in_specs = [a_spec, b_spec]
    in_specs += [pl.BlockSpec((tm, LANES), lambda i, j, kk: (i, 0)) for _ in row_extras]
    in_specs += [pl.BlockSpec((tm, tn), lambda i, j, kk: (i, j)) for _ in tile_extras]
    in_specs += [ANY for _ in behind]
    if out_map is None:
        out_specs = [pl.BlockSpec((tm, tn), lambda i, j, kk: (i, j)) for _ in out_dtypes]
        shapes = [jax.ShapeDtypeStruct((m, n), dt) for dt in out_dtypes]
    else:
        out_specs = [out_map(tm, tn)]
        shapes = [jax.ShapeDtypeStruct(out_shape, out_dtypes[0])]
    acc_shape = (tm, tn) if nk > 1 else (8, LANES)
    res = _pcall(body, name=name, out_shape=shapes, grid=(m // tm, n // tn, nk), in_specs=in_specs,
                 out_specs=out_specs, scratch_shapes=[pltpu.VMEM(acc_shape, F32)])(
                     a, b, *row_extras, *tile_extras, *behind)
    return res[0] if n_out == 1 else res


def _rope_cols(acc, j, cos, sin, n_rope):
    width = acc.shape[1]
    parts = []
    for g in range(width // HEAD_DIM):
        xg = acc[:, g * HEAD_DIM:(g + 1) * HEAD_DIM]
        roped = xg * cos + pltpu.roll(xg, HEAD_DIM // 2, 1) * sin
        parts.append(jnp.where(j * width + g * HEAD_DIM < n_rope, roped, xg))
    return jnp.concatenate(parts, axis=1) if len(parts) > 1 else parts[0]


def _silu(v):
    return v / (1.0 + jnp.exp(-v))


def _mod_part(c_all, w_mod, b_part):
    d, w = w_mod.shape
    tk = _blk(d, 512)

    def body(c_ref, w_ref, b_ref, o_ref):
        kk = pl.program_id(0)
        cond = _silu(c_ref[...]).astype(BF16)
        part = jnp.dot(cond, w_ref[...].astype(BF16), preferred_element_type=F32)

        @pl.when(kk == 0)
        def _():
            o_ref[...] = part + b_ref[...]

        @pl.when(kk > 0)
        def _():
            o_ref[...] += part

    return _pcall(body, name="mod_part", out_shape=jax.ShapeDtypeStruct((N_DEV, w), F32), grid=(d // tk,),
                  in_specs=[pl.BlockSpec((N_DEV, tk), lambda kk: (0, kk)), pl.BlockSpec((tk, w), lambda kk: (kk, 0)),
                            pl.BlockSpec((1, w), lambda kk: (0, 0))],
                  out_specs=pl.BlockSpec((N_DEV, w), lambda kk: (0, 0)))(c_all, w_mod, b_part)


def _row_call(body, name, t, d, tiled_in, vec_in, tiled_out_dtypes, n_vec_out, tr=256):
    tr = _blk(t, tr)
    tile = pl.BlockSpec((tr, d), lambda i: (i, 0))
    vec = pl.BlockSpec((1, d), lambda i: (0, 0))
    out_shape = [jax.ShapeDtypeStruct((t, d), dt) for dt in tiled_out_dtypes]
    out_shape += [jax.ShapeDtypeStruct((1, d), F32)] * n_vec_out
    return _pcall(body, name=name, out_shape=out_shape, grid=(t // tr,),
                  in_specs=[tile] * len(tiled_in) + [vec] * len(vec_in),
                  out_specs=[tile] * len(tiled_out_dtypes) + [vec] * n_vec_out)(*tiled_in, *vec_in)


def _accumulate(ref, val):
    @pl.when(pl.program_id(0) == 0)
    def _():
        ref[...] = val

    @pl.when(pl.program_id(0) > 0)
    def _():
        ref[...] += val


def _rsum(v):
    return jnp.sum(v, axis=0, keepdims=True)


def _rms(v):
    return lax.rsqrt(jnp.mean(v * v, axis=-1, keepdims=True) + NORM_EPS)


def _rms_bwd(vhat, r, dvhat):
    return r * (dvhat - vhat * jnp.mean(dvhat * vhat, axis=-1, keepdims=True))


def _pre_attn(x, g0, sc_a, sh_a):
    def body(x_ref, g_ref, sc_ref, sh_ref, h_ref):
        xv = x_ref[...]
        h_ref[...] = (xv * _rms(xv) * g_ref[...] * (1.0 + sc_ref[...]) + sh_ref[...]).astype(BF16)

    t, d = x.shape
    return _row_call(body, "pre_attn", t, d, [x], [g0, sc_a, sh_a], [BF16], 0)[0]


def _post_mix(x, mix, gt_a, g1, g2, sc_m, sh_m):
    def body(x_ref, mix_ref, gt_ref, g1_ref, g2_ref, sc_ref, sh_ref, x2_ref, h2_ref):
        mv = mix_ref[...]
        x2 = x_ref[...] + gt_ref[...] * (mv * _rms(mv) * g1_ref[...])
        x2_ref[...] = x2
        h2_ref[...] = (x2 * _rms(x2) * g2_ref[...] * (1.0 + sc_ref[...]) + sh_ref[...]).astype(BF16)

    t, d = x.shape
    return _row_call(body, "post_mix", t, d, [x, mix], [gt_a, g1, g2, sc_m, sh_m], [F32, BF16], 0)


def _final(y, x2, target, gt_m, g3):
    t, d = y.shape

    def body(y_ref, x2_ref, tg_ref, gt_ref, g3_ref, dy_ref, dout_ref, dgt_ref, dg3_ref, loss_ref):
        yv = y_ref[...]
        r = _rms(yv)
        yhat = yv * r
        n3 = yhat * g3_ref[...]
        err = x2_ref[...] + gt_ref[...] * n3 - tg_ref[...]
        _accumulate(loss_ref, jnp.zeros((1, d), F32) + 0.5 * jnp.sum(err * err) / d)
        dout = err * (1.0 / d)
        dout_ref[...] = dout
        _accumulate(dgt_ref, _rsum(dout * n3))
        dn3 = dout * gt_ref[...]
        _accumulate(dg3_ref, _rsum(dn3 * yhat))
        dy_ref[...] = _rms_bwd(yhat, r, dn3 * g3_ref[...]).astype(BF16)

    return _row_call(body, "final", t, d, [y, x2, target], [gt_m, g3], [BF16, F32], 3)


def _mid_bwd(dh2, dout, x2, mix, g2, sc_m, gt_a, g1):
    t, d = x2.shape

    def body(dh2_ref, dout_ref, x2_ref, mix_ref, g2_ref, sc_ref, gt_ref, g1_ref,
             dmix_ref, dx2_ref, dsh_ref, dsc_ref, dg2_ref, dgt_ref, dg1_ref):
        dh2v = dh2_ref[...]
        x2v = x2_ref[...]
        r2 = _rms(x2v)
        x2hat = x2v * r2
        _accumulate(dsh_ref, _rsum(dh2v))
        _accumulate(dsc_ref, _rsum(dh2v * (x2hat * g2_ref[...])))
        dn2 = dh2v * (1.0 + sc_ref[...])
        _accumulate(dg2_ref, _rsum(dn2 * x2hat))
        dx2 = dout_ref[...] + _rms_bwd(x2hat, r2, dn2 * g2_ref[...])
        dx2_ref[...] = dx2
        mv = mix_ref[...]
        r1 = _rms(mv)
        mhat = mv * r1
        _accumulate(dgt_ref, _rsum(dx2 * (mhat * g1_ref[...])))
        dn1 = dx2 * gt_ref[...]
        _accumulate(dg1_ref, _rsum(dn1 * mhat))
        dmix_ref[...] = _rms_bwd(mhat, r1, dn1 * g1_ref[...]).astype(BF16)

    return _row_call(body, "mid_bwd", t, d, [dh2, dout, x2, mix], [g2, sc_m, gt_a, g1], [BF16, F32], 5)


def _x_bwd(dh1, dx2, x, g0, sc_a):
    t, d = x.shape

    def body(dh1_ref, dx2_ref, x_ref, g0_ref, sc_ref, dx_ref, dsh_ref, dsc_ref, dg0_ref):
        dh1v = dh1_ref[...]
        xv = x_ref[...]
        r0 = _rms(xv)
        xhat = xv * r0
        _accumulate(dsh_ref, _rsum(dh1v))
        _accumulate(dsc_ref, _rsum(dh1v * (xhat * g0_ref[...])))
        dn0 = dh1v * (1.0 + sc_ref[...])
        _accumulate(dg0_ref, _rsum(dn0 * xhat))
        dx_ref[...] = dx2_ref[...] + _rms_bwd(xhat, r0, dn0 * g0_ref[...])

    return _row_call(body, "x_bwd", t, d, [dh1, dx2, x], [g0, sc_a], [F32], 3)


def _pick_lane(block, h):
    lane = lax.broadcasted_iota(jnp.int32, block.shape, 1)
    return jnp.sum(jnp.where(lane == h, block, 0.0), axis=1, keepdims=True)


def _put_lane(ref, rows, h, col):
    old = ref[rows, :]
    lane = lax.broadcasted_iota(jnp.int32, old.shape, 1)
    ref[rows, :] = jnp.where(lane == h, col, old)


def _tri(n, lower):
    r = lax.broadcasted_iota(jnp.int32, (n, n), 0)
    c = lax.broadcasted_iota(jnp.int32, (n, n), 1)
    return jnp.where((c <= r) if lower else (c >= r), 1.0, 0.0).astype(F32)


def _cum_fwd(fg, b128):
    t = fg.shape[0]
    nb = t // LANES

    def body(fg_ref, b_ref, cum_ref, cumt_ref):
        tri = _tri(LANES, True)
        carry = jnp.zeros((1, LANES), F32)
        for i in range(nb):
            z = fg_ref[i * LANES:(i + 1) * LANES, :] + b_ref[...]
            lf = jnp.minimum(z, 0.0) - jnp.log(1.0 + jnp.exp(-jnp.abs(z)))
            blk = jnp.dot(tri, lf, precision=lax.Precision.HIGHEST, preferred_element_type=F32) + carry
            cum_ref[i * LANES:(i + 1) * LANES, :] = blk
            carry = blk[LANES - 1:LANES, :]
        cumt_ref[...] = cum_ref[...].T[0:N_FOX, :]

    return _pcall(body, name="cum_fwd",
                  out_shape=[jax.ShapeDtypeStruct((t, LANES), F32), jax.ShapeDtypeStruct((N_FOX, t), F32)],
                  grid=(1,),
                  in_specs=[pl.BlockSpec((t, LANES), lambda i: (0, 0)), pl.BlockSpec((1, LANES), lambda i: (0, 0))],
                  out_specs=[pl.BlockSpec((t, LANES), lambda i: (0, 0)), pl.BlockSpec((N_FOX, t), lambda i: (0, 0))],
                  )(fg, b128)


def _fg_bwd(dcs_rows, fg, b128):
    t = fg.shape[0]
    nb = t // LANES

    def body(dcs_ref, fg_ref, b_ref, dfg_ref, db_ref, dcum_ref):
        dcum_ref[...] = -jnp.concatenate([dcs_ref[...], jnp.zeros((LANES - N_FOX, t), F32)], axis=0).T
        tri = _tri(LANES, False)
        carry = jnp.zeros((1, LANES), F32)
        db = jnp.zeros((1, LANES), F32)
        for i in reversed(range(nb)):
            rows = slice(i * LANES, (i + 1) * LANES)
            dlf = jnp.dot(tri, dcum_ref[rows, :], precision=lax.Precision.HIGHEST, preferred_element_type=F32) + carry
            carry = dlf[0:1, :]
            z = fg_ref[rows, :] + b_ref[...]
            dfg = dlf / (1.0 + jnp.exp(z))
            dfg_ref[rows, :] = dfg.astype(BF16)
            db = db + _rsum(dfg)
        db_ref[...] = db

    full = pl.BlockSpec((t, LANES), lambda i: (0, 0))
    vec = pl.BlockSpec((1, LANES), lambda i: (0, 0))
    return _pcall(body, name="fg_bwd",
                  out_shape=[jax.ShapeDtypeStruct((t, LANES), BF16), jax.ShapeDtypeStruct((1, LANES), F32)],
                  grid=(1,), in_specs=[pl.BlockSpec((N_FOX, t), lambda i: (0, 0)), full, vec], out_specs=[full, vec],
                  scratch_shapes=[pltpu.VMEM((t, LANES), F32)])(dcs_rows, fg, b128)


def _head_spec(t, col0, div=1):
    return pl.BlockSpec((t, HEAD_DIM), lambda h: (0, col0 + h // div))


def _fox_scores(q, k, cq, ck, i, tq, end):
    s = lax.dot_general(q, k, NT_DIMS, preferred_element_type=F32) * ATT_SCALE + cq - ck
    row = lax.broadcasted_iota(jnp.int32, (tq, end), 0) + i * tq
    col = lax.broadcasted_iota(jnp.int32, (tq, end), 1)
    return jnp.where(row >= col, s, -jnp.inf)


def _fox_fwd(proj_a, cum, cumt):
    t = proj_a.shape[0]
    tq = _blk(t, 512)
    nq = t // tq

    def body(q_ref, k_ref, v_ref, cum_ref, cumt_ref, o_ref, lse_ref):
        h = pl.program_id(0)
        cq_all = _pick_lane(cum_ref[...], h)
        ck_all = cumt_ref[pl.ds(h, 1), :]

        @pl.when(h == 0)
        def _():
            lse_ref[...] = jnp.zeros_like(lse_ref)

        for i in range(nq):
            rows, end = slice(i * tq, (i + 1) * tq), (i + 1) * tq
            s = _fox_scores(q_ref[rows, :], k_ref[0:end, :], cq_all[rows, :], ck_all[:, 0:end], i, tq, end)
            m = jnp.max(s, axis=1, keepdims=True)
            p = jnp.exp(s - m)
            l = jnp.sum(p, axis=1, keepdims=True)
            o = jnp.dot(p.astype(BF16), v_ref[0:end, :], preferred_element_type=F32) / l
            o_ref[rows, :] = o.astype(BF16)
            _put_lane(lse_ref, rows, h, m + jnp.log(l))

    nh = FOX_W // HEAD_DIM
    stat = pl.BlockSpec((t, LANES), lambda h: (0, 0))
    return _pcall(body, name="fox_fwd",
                  out_shape=[jax.ShapeDtypeStruct((t, FOX_W), BF16), jax.ShapeDtypeStruct((t, LANES), F32)],
                  grid=(N_FOX,),
                  in_specs=[_head_spec(t, 0), _head_spec(t, nh), _head_spec(t, 2 * nh), stat,
                            pl.BlockSpec((N_FOX, t), lambda h: (0, 0))],
                  out_specs=[_head_spec(t, 0), stat])(proj_a, proj_a, proj_a, cum, cumt)


def _fox_bwd(proj_a, d_attn, cum, cumt, lse):
    t = proj_a.shape[0]
    tq = _blk(t, 512)
    nq = t // tq

    def body(q_ref, k_ref, v_ref, do_ref, cum_ref, cumt_ref, lse_ref,
             dq_ref, dk_ref, dv_ref, dcs_ref, dk_acc, dv_acc, dcs_acc):
        h = pl.program_id(0)
        cq_all = _pick_lane(cum_ref[...], h)
        ck_all = cumt_ref[pl.ds(h, 1), :]
        lse_all = _pick_lane(lse_ref[...], h)
        dk_acc[...] = jnp.zeros_like(dk_acc)
        dv_acc[...] = jnp.zeros_like(dv_acc)
        dcs_acc[...] = jnp.zeros_like(dcs_acc)
        for i in range(nq):
            rows, end = slice(i * tq, (i + 1) * tq), (i + 1) * tq
            q, k, v, do = q_ref[rows, :], k_ref[0:end, :], v_ref[0:end, :], do_ref[rows, :]
            s = _fox_scores(q, k, cq_all[rows, :], ck_all[:, 0:end], i, tq, end)
            p = jnp.exp(s - lse_all[rows, :])
            dp = lax.dot_general(do, v, NT_DIMS, preferred_element_type=F32)
            ds = p * (dp - jnp.sum(p * dp, axis=1, keepdims=True))
            dcs_acc[:, 0:end] += jnp.sum(ds, axis=0, keepdims=True)
            ds = ds.astype(BF16)
            dq_ref[rows, :] = (jnp.dot(ds, k, preferred_element_type=F32) * ATT_SCALE).astype(BF16)
            dk_acc[0:end, :] += lax.dot_general(ds, q, TN_DIMS, preferred_element_type=F32)
            dv_acc[0:end, :] += lax.dot_general(p.astype(BF16), do, TN_DIMS, preferred_element_type=F32)
        dk_ref[...] = (dk_acc[...] * ATT_SCALE).astype(BF16)
        dv_ref[...] = dv_acc[...].astype(BF16)
        dcs_ref[pl.ds(h, 1), :] = dcs_acc[...]

    nh = FOX_W // HEAD_DIM
    stat = pl.BlockSpec((t, LANES), lambda h: (0, 0))
    rows8 = pl.BlockSpec((N_FOX, t), lambda h: (0, 0))
    head = _head_spec(t, 0)
    wide = jax.ShapeDtypeStruct((t, FOX_W), BF16)
    return _pcall(body, name="fox_bwd",
                  out_shape=[wide, wide, wide, jax.ShapeDtypeStruct((N_FOX, t), F32)],
                  grid=(N_FOX,),
                  in_specs=[_head_spec(t, 0), _head_spec(t, nh), _head_spec(t, 2 * nh), head, stat, rows8, stat],
                  out_specs=[head, head, head, rows8],
                  scratch_shapes=[pltpu.VMEM((t, HEAD_DIM), F32), pltpu.VMEM((t, HEAD_DIM), F32),
                                  pltpu.VMEM((1, t), F32)],
                  )(proj_a, proj_a, proj_a, d_attn, cum, cumt, lse)


def _swa_scores(q, k, i, tq, start, end):
    s = lax.dot_general(q, k, NT_DIMS, preferred_element_type=F32) * ATT_SCALE
    row = lax.broadcasted_iota(jnp.int32, (tq, end - start), 0) + i * tq
    col = lax.broadcasted_iota(jnp.int32, (tq, end - start), 1) + start
    diff = row - col
    return jnp.where((diff >= 0) & (diff < WINDOW), s, -jnp.inf)


def _swa_blocks(t):
    tq = _blk(t, 256)
    return tq, [(i, max(0, i * tq - WINDOW), (i + 1) * tq) for i in range(t // tq)]


def _swa_fwd(proj_b, sinks128):
    t = proj_b.shape[0]
    tq, blocks = _swa_blocks(t)

    def body(q_ref, k_ref, v_ref, sink_ref, o_ref, lse_ref):
        h = pl.program_id(0)
        sink = _pick_lane(sink_ref[...], h)

        @pl.when(h == 0)
        def _():
            lse_ref[...] = jnp.zeros_like(lse_ref)

        for i, start, end in blocks:
            rows = slice(i * tq, end)
            s = _swa_scores(q_ref[rows, :], k_ref[start:end, :], i, tq, start, end)
            m = jnp.maximum(jnp.max(s, axis=1, keepdims=True), sink)
            p = jnp.exp(s - m)
            l = jnp.sum(p, axis=1, keepdims=True) + jnp.exp(sink - m)
            o = jnp.dot(p.astype(BF16), v_ref[start:end, :], preferred_element_type=F32) / l
            o_ref[rows, :] = o.astype(BF16)
            _put_lane(lse_ref, rows, h, m + jnp.log(l))

    stat = pl.BlockSpec((t, LANES), lambda h: (0, 0))
    return _pcall(body, name="swa_fwd",
                  out_shape=[jax.ShapeDtypeStruct((t, SWA_W), BF16), jax.ShapeDtypeStruct((t, LANES), F32)],
                  grid=(N_SWA,),
                  in_specs=[_head_spec(t, 0), _head_spec(t, N_SWA, GQA), _head_spec(t, N_SWA + N_KV, GQA),
                            pl.BlockSpec((1, LANES), lambda h: (0, 0))],
                  out_specs=[_head_spec(t, 0), stat])(proj_b, proj_b, proj_b, sinks128)


def _rope_bwd(d, cos, sin):
    return d * cos + pltpu.roll(d * sin, HEAD_DIM // 2, 1)


def _swa_bwd(proj_b, d_attn, lse, sinks128, cos, sin):
    t = proj_b.shape[0]
    tq, blocks = _swa_blocks(t)

    def body(q_ref, k_ref, v_ref, do_ref, lse_ref, sink_ref, cos_ref, sin_ref,
             dq_ref, dk_ref, dv_ref, dsink_ref, dk_acc, dv_acc):
        h = pl.program_id(0)
        sink = _pick_lane(sink_ref[...], h)
        lse_all = _pick_lane(lse_ref[...], h)

        @pl.when(h == 0)
        def _():
            dsink_ref[...] = jnp.zeros_like(dsink_ref)

        @pl.when(h % GQA == 0)
        def _():
            dk_acc[...] = jnp.zeros_like(dk_acc)
            dv_acc[...] = jnp.zeros_like(dv_acc)

        dsink = jnp.zeros((1, 1), F32)
        for i, start, end in blocks:
            rows = slice(i * tq, end)
            q, k, v, do = q_ref[rows, :], k_ref[start:end, :], v_ref[start:end, :], do_ref[rows, :]
            s = _swa_scores(q, k, i, tq, start, end)
            p = jnp.exp(s - lse_all[rows, :])
            dp = lax.dot_general(do, v, NT_DIMS, preferred_element_type=F32)
            delta = jnp.sum(p * dp, axis=1, keepdims=True)
            ds = (p * (dp - delta)).astype(BF16)
            dq = jnp.dot(ds, k, preferred_element_type=F32) * ATT_SCALE
            dq_ref[rows, :] = _rope_bwd(dq, cos_ref[rows, :], sin_ref[rows, :]).astype(BF16)
            dk_acc[start:end, :] += lax.dot_general(ds, q, TN_DIMS, preferred_element_type=F32)
            dv_acc[start:end, :] += lax.dot_general(p.astype(BF16), do, TN_DIMS, preferred_element_type=F32)
            dsink = dsink - jnp.sum(jnp.exp(sink - lse_all[rows, :]) * delta, axis=0, keepdims=True)
        old = dsink_ref[...]
        lane = lax.broadcasted_iota(jnp.int32, old.shape, 1)
        dsink_ref[...] = jnp.where(lane == h, dsink, old)

        @pl.when(h % GQA == GQA - 1)
        def _():
            dk_ref[...] = _rope_bwd(dk_acc[...] * ATT_SCALE, cos_ref[...], sin_ref[...]).astype(BF16)
            dv_ref[...] = dv_acc[...].astype(BF16)

    stat = pl.BlockSpec((t, LANES), lambda h: (0, 0))
    vec = pl.BlockSpec((1, LANES), lambda h: (0, 0))
    head = _head_spec(t, 0)
    kv_out = _head_spec(t, 0, GQA)
    return _pcall(body, name="swa_bwd",
                  out_shape=[jax.ShapeDtypeStruct((t, SWA_W), BF16), jax.ShapeDtypeStruct((t, KV_W), BF16),
                             jax.ShapeDtypeStruct((t, KV_W), BF16), jax.ShapeDtypeStruct((1, LANES), F32)],
                  grid=(N_SWA,),
                  in_specs=[head, _head_spec(t, N_SWA, GQA), _head_spec(t, N_SWA + N_KV, GQA),
                            _head_spec(t, N_FOX), stat, vec, stat, stat],
                  out_specs=[head, kv_out, kv_out, vec],
                  scratch_shapes=[pltpu.VMEM((t, HEAD_DIM), F32), pltpu.VMEM((t, HEAD_DIM), F32)],
                  )(proj_b, proj_b, proj_b, d_attn, lse, sinks128, cos, sin)


def _adamw(w, g, m, v):
    m = ADAM_B1 * m + (1.0 - ADAM_B1) * g
    v = ADAM_B2 * v + (1.0 - ADAM_B2) * (g * g)
    m_hat = m / (1.0 - ADAM_B1 ** ADAM_STEP)
    v_hat = v / (1.0 - ADAM_B2 ** ADAM_STEP)
    delta = -ADAM_LR * (m_hat / (jnp.sqrt(v_hat) + ADAM_EPS) + ADAM_WD * w)
    return delta, m, v


def _adam_pieces(w, m, v, own, land, idx, name):
    rows, cols = w.shape
    tr = 256 if rows % 256 == 0 else rows

    def body(idx_ref, own_ref, l1_ref, l2_ref, l3_ref, w_ref, m_ref, v_ref, g_ref, d_ref, mo_ref, vo_ref):
        g = own_ref[...].astype(F32) + l1_ref[...].astype(F32) + l2_ref[...].astype(F32) + l3_ref[...].astype(F32)
        g_ref[...] = g
        d_ref[...], mo_ref[...], vo_ref[...] = _adamw(w_ref[...], g, m_ref[...], v_ref[...])

    def piece(p):
        return pl.BlockSpec((None, tr, cols), lambda i, idx_ref: (idx_ref[p], i, 0))

    tile = pl.BlockSpec((tr, cols), lambda i, idx_ref: (i, 0))
    out = jax.ShapeDtypeStruct((rows, cols), F32)
    grid_spec = pltpu.PrefetchScalarGridSpec(
        num_scalar_prefetch=1, grid=(rows // tr,),
        in_specs=[piece(0), piece(1), piece(2), piece(3), tile, tile, tile], out_specs=[tile] * 4)
    return _pcall(body, name=name, out_shape=[out] * 4, grid_spec=grid_spec)(idx, own, land, land, land, w, m, v)


def _adam_mod(c_all, dmod_cols, w, m, v):
    rows, cols = w.shape
    tr = _blk(rows, 256)

    def body(c_ref, dm_ref, w_ref, m_ref, v_ref, g_ref, d_ref, mo_ref, vo_ref):
        cond = _silu(c_ref[...]).astype(BF16)
        g = lax.dot_general(cond, dm_ref[...].astype(BF16), TN_DIMS, preferred_element_type=F32)
        g_ref[...] = g
        d_ref[...], mo_ref[...], vo_ref[...] = _adamw(w_ref[...], g, m_ref[...], v_ref[...])

    tile = pl.BlockSpec((tr, cols), lambda i: (i, 0))
    out = jax.ShapeDtypeStruct((rows, cols), F32)
    return _pcall(body, name="adam_mod", out_shape=[out] * 4, grid=(rows // tr,),
                  in_specs=[pl.BlockSpec((N_DEV, tr), lambda i: (0, i)), pl.BlockSpec((N_DEV, cols), lambda i: (0, 0)),
                            tile, tile, tile],
                  out_specs=[tile] * 4)(c_all, dmod_cols, w, m, v)


def _adam_small(parts, w, m, v):
    nv = w.shape[1]

    def body(p_ref, w_ref, m_ref, v_ref, g_ref, d_ref, mo_ref, vo_ref):
        g = p_ref[0:1, :]
        for k in range(1, N_DEV):
            g = g + p_ref[k:k + 1, :]
        g_ref[...] = g
        d_ref[...], mo_ref[...], vo_ref[...] = _adamw(w_ref[...], g, m_ref[...], v_ref[...])

    vec = pl.BlockSpec((1, nv), lambda i: (0, 0))
    out = jax.ShapeDtypeStruct((1, nv), F32)
    return _pcall(body, name="adam_small", out_shape=[out] * 4, grid=(1,),
                  in_specs=[pl.BlockSpec((N_DEV, nv), lambda i: (0, 0)), vec, vec, vec],
                  out_specs=[vec] * 4)(parts, w, m, v)


def _pad_lanes(v, width=LANES):
    return jnp.pad(v, ((0, 0), (0, width - v.shape[1])))


def kernel(x, c, w_mod, b_mod, g_pre_mix, g_post_mix, w_in, b_forget, swa_sinks, w_out, g_pre_mlp, g_post_mlp, w_up, w_down, loss_target, m_w_mod, m_b_mod, m_g_pre_mix, m_g_post_mix, m_w_in, m_b_forget, m_swa_sinks, m_w_out, m_g_pre_mlp, m_g_post_mlp, m_w_up, m_w_down, v_w_mod, v_b_mod, v_g_pre_mix, v_g_post_mix, v_w_in, v_b_forget, v_swa_sinks, v_w_out, v_g_pre_mlp, v_g_post_mlp, v_w_up, v_w_down):
    ax, ay, ac = _position()
    me = 4 * ax + 2 * ay + ac
    x, target = x[0], loss_target[0]
    t, d = x.shape
    w_mod, w_in, w_out, w_up, w_down = w_mod[0], w_in[0], w_out[0], w_up[0], w_down[0]
    mod_w = w_mod.shape[1]
    in_w = w_in.shape[1]
    in_total = N_DEV * in_w
    shard_ff = w_up.shape[1]
    n_fox3 = 3 * FOX_W
    n_swa3 = SWA_W + 2 * KV_W
    assert in_total == n_fox3 + N_FOX + n_swa3 and d == FOX_W + SWA_W

    c_all = _all_gather([c], "gather_c")[0].reshape(N_DEV, d)
    b_part = lax.dynamic_slice(b_mod, (0, me * mod_w), (1, mod_w))
    mod_parts = _all_gather([_mod_part(c_all, w_mod, b_part)], "gather_mod")[0]
    mod = lax.dynamic_index_in_dim(mod_parts, me, axis=1, keepdims=False).reshape(1, N_DEV * mod_w)

    shards = [w_in.astype(BF16), w_out.astype(BF16), w_up.astype(BF16), w_down.astype(BF16)]
    shards, mod = lax.optimization_barrier((shards, mod))
    ag_send, ag_recv, ag_shard, ag_land, ag_token = _ag_start(shards)

    def gathered(i, after, name):
        shard, land = _ag_wait(ag_send[i], ag_recv[i], ag_shard[i], ag_land[i], after, "ag_wait_" + name)
        return lax.dynamic_update_slice(_ag_forward(land, "ag_fwd_" + name), shard[None], (me, 0, 0))

    sh_a, sc_a, gt_a, sh_m, sc_m, gt_m = [mod[:, i * d:(i + 1) * d] for i in range(6)]

    half = HEAD_DIM // 2
    inv_freq = 1.0 / (ROPE_THETA ** (jnp.arange(half, dtype=F32) * (2.0 / HEAD_DIM)))
    ang = jnp.arange(t).astype(F32)[:, None] * inv_freq[None, :]
    cos = jnp.concatenate([jnp.cos(ang), jnp.cos(ang)], axis=1)
    sin = jnp.concatenate([-jnp.sin(ang), jnp.sin(ang)], axis=1)

    b128 = _pad_lanes(b_forget)
    sinks128 = _pad_lanes(swa_sinks)

    h1 = _pre_attn(x, g_pre_mix + ag_token[0:1, 0:1], sc_a, sh_a)
    w_in_g = gathered(0, h1, "in")
    o_fg, o_sq, o_sv = n_fox3, n_fox3 + N_FOX, n_fox3 + N_FOX + SWA_W + KV_W

    def cols(lo, hi):
        parts = []
        for j in range(lo // in_w, (hi - 1) // in_w + 1):
            parts.append(w_in_g[j, :, max(lo - j * in_w, 0):min(hi - j * in_w, in_w)])
        return parts

    w_in_r = jnp.concatenate(cols(0, o_fg) + cols(o_sq, in_total) + cols(o_fg, o_sq)
                             + [jnp.zeros((d, FG_PAD - N_FOX), BF16)], axis=1)
    proj_a = _matmul(h1, w_in_r, name="proj_a", n_cols=n_fox3, n_off=0)
    proj_b = _matmul(h1, w_in_r, name="proj_b", n_cols=n_swa3, n_off=n_fox3, tn=512, row_extras=(cos, sin),
                     epilogue=lambda acc, j, cs, sn: (_rope_cols(acc, j, cs, sn, SWA_W + KV_W),))
    fg = _matmul(h1, w_in_r, name="proj_fg", n_cols=FG_PAD, n_off=n_fox3 + n_swa3, tn=FG_PAD,
                 out_dtypes=(F32,))[:, 0:LANES]
    cum, cumt = _cum_fwd(fg, b128)
    fox_o, fox_lse = _fox_fwd(proj_a, cum, cumt)
    swa_o, swa_lse = _swa_fwd(proj_b, sinks128)
    w_out_full = gathered(1, swa_o, "out").reshape(d, d)
    attn = jnp.concatenate([fox_o, swa_o], axis=1)
    mix = _matmul(attn, w_out_full, name="out_proj", out_dtypes=(F32,))
    x2, h2 = _post_mix(x, mix, gt_a, g_post_mix, g_pre_mlp, sc_m, sh_m)
    w_up_g = gathered(2, h2, "up")
    u, act = _matmul(h2, w_up_g, name="mlp_up", b_sharded=True, out_dtypes=(BF16, BF16),
                     epilogue=lambda acc, j: (acc, jnp.square(jnp.maximum(acc, 0.0))))
    w_down_full = gathered(3, act, "down").reshape(N_DEV * shard_ff, d)
    y = _matmul(act, w_down_full, name="mlp_down", out_dtypes=(F32,))

    core = jnp.reshape(ac, (1,)).astype(jnp.int32)

    def reduce_start(started, after, name):
        send, recv, src, land, _ = started
        full, from_sibling = _sib_wait(send, recv, src, land, after, "sib_wait_" + name)
        return _rs_start(_chip_sum(full, from_sibling, core, "chip_sum_" + name), "rs_start_" + name)

    def tok(started):
        return started[4][0:1, 0:1]

    idx = jnp.stack([2 * ax + ay, 2 * (1 - ax) + ay, 2 * ax + (1 - ay), 2 * (1 - ax) + (1 - ay)]).astype(jnp.int32)

    def reduce_finish(started, after, w, m, v, name):
        send, recv, src, land, _ = started
        own, landed = _rs_wait(send, recv, src, land, after, "rs_wait_" + name)
        return _adam_pieces(w, m[0], v[0], own, landed, idx, "adam_" + name)

    dy, dout, dgt_m, dg3, loss_vec = _final(y, x2, target, gt_m, g_post_mlp)
    du = _matmul(dy, w_down_full, name="d_act", tb=True, tile_extras=(u,),
                 epilogue=lambda acc, j, uu: (acc * (2.0 * jnp.maximum(uu.astype(F32), 0.0)),))
    dw_down = _matmul(act, dy, name="dw_down", ta=True)
    sb_down = _sib_start(dw_down.reshape(N_DEV, shard_ff, d), "sib_start_down")
    dh2 = _matmul(du, w_up_g, name="d_h2", tb=True, b_sharded=True, out_dtypes=(F32,), after=sb_down[4])
    rs_down = reduce_start(sb_down, dh2, "down")
    per = shard_ff // _blk(shard_ff, 1024)
    dw_up = _matmul(h2, du, name="dw_up", ta=True, tn=_blk(shard_ff, 1024), out_shape=(N_DEV, d, shard_ff),
                    out_map=lambda tm, tn: pl.BlockSpec((None, tm, tn), lambda i, j, kk: (j // per, i, j % per)),
                    after=rs_down[4])
    sb_up = _sib_start(dw_up, "sib_start_up")
    dmix, dx2, dsh_m, dsc_m, dg2, dgt_a, dg1 = _mid_bwd(
        dh2, dout, x2, mix, g_pre_mlp + tok(sb_up), sc_m, gt_a, g_post_mix)
    d_attn = _matmul(dmix, w_out_full, name="d_attn", tb=True)
    rs_up = reduce_start(sb_up, d_attn, "up")
    dw_out = _matmul(attn, dmix, name="dw_out", ta=True, after=rs_up[4])
    sb_out = _sib_start(dw_out.reshape(N_DEV, d // N_DEV, d), "sib_start_out")
    dqf, dkf, dvf, dcs = _fox_bwd(proj_a, d_attn, cum, cumt, fox_lse)
    dsq, dsk, dsv, dsinks = _swa_bwd(proj_b, d_attn, swa_lse, sinks128 + tok(sb_out), cos, sin)
    rs_out = reduce_start(sb_out, dsq, "out")
    dfg, db_forget = _fg_bwd(dcs, fg, b128 + tok(rs_out))
    dproj = jnp.concatenate([dqf, dkf, dvf, dsq, dsk, dsv, _pad_lanes(dfg, FG_PAD)], axis=1)
    dh1 = _matmul(dproj, w_in_r, name="d_h1", tb=True, out_dtypes=(F32,))
    dw_in_r = _matmul(h1, dproj, name="dw_in", ta=True)

    def shard_cols(j):
        lo, hi = j * in_w, (j + 1) * in_w
        parts = []
        for seg_lo, seg_hi, shift in ((0, o_fg, 0), (o_fg, o_sq, n_swa3), (o_sq, in_total, -N_FOX)):
            a, b = max(lo, seg_lo), min(hi, seg_hi)
            if a < b:
                parts.append(dw_in_r[:, a + shift:b + shift])
        return parts[0] if len(parts) == 1 else jnp.concatenate(parts, axis=1)

    sb_in = _sib_start(jnp.stack([shard_cols(j) for j in range(N_DEV)]), "sib_start_in")
    grad_x, dsh_a, dsc_a, dg0 = _x_bwd(dh1, dx2, x, g_pre_mix + tok(sb_in), sc_a)
    rs_in = reduce_start(sb_in, grad_x, "in")

    g_w_down, d_w_down, nm_w_down, nv_w_down = reduce_finish(rs_down, rs_in[4], w_down, m_w_down, v_w_down, "w_down")
    g_w_up, d_w_up, nm_w_up, nv_w_up = reduce_finish(rs_up, d_w_down, w_up, m_w_up, v_w_up, "w_up")
    g_w_out, d_w_out, nm_w_out, nv_w_out = reduce_finish(rs_out, d_w_up, w_out, m_w_out, v_w_out, "w_out")

    small = jnp.concatenate([dsh_a, dsc_a, dgt_a, dsh_m, dsc_m, dgt_m, dg0, dg1, dg2, dg3, db_forget, dsinks,
                             loss_vec[:, 0:LANES]], axis=1)
    small, d_w_out = lax.optimization_barrier((small, d_w_out))
    small_all = _all_gather([small], "gather_small")[0].reshape(N_DEV, small.shape[1])
    pack = lambda bm, g0_, g1_, g2_, g3_, bf_, sk_: jnp.concatenate(
        [bm, g0_, g1_, g2_, g3_, _pad_lanes(bf_), _pad_lanes(sk_), jnp.zeros((1, LANES), F32)], axis=1)
    p_small = pack(b_mod, g_pre_mix, g_post_mix, g_pre_mlp, g_post_mlp, b_forget, swa_sinks)
    m_small = pack(m_b_mod, m_g_pre_mix, m_g_post_mix, m_g_pre_mlp, m_g_post_mlp, m_b_forget, m_swa_sinks)
    v_small = pack(v_b_mod, v_g_pre_mix, v_g_post_mix, v_g_pre_mlp, v_g_post_mlp, v_b_forget, v_swa_sinks)
    small_out = _adam_small(small_all, p_small, m_small, v_small)

    n_mod = 6 * d

    def unpack(vec):
        o = n_mod
        return (vec[:, 0:n_mod], vec[:, o:o + d], vec[:, o + d:o + 2 * d], vec[:, o + 2 * d:o + 3 * d],
                vec[:, o + 3 * d:o + 4 * d], vec[:, o + 4 * d:o + 4 * d + N_FOX],
                vec[:, o + 4 * d + LANES:o + 4 * d + LANES + N_SWA])

    loss = small_out[0][0, n_mod + 4 * d + 2 * LANES]
    g_small, d_small, nm_small, nv_small = [unpack(vec) for vec in small_out]

    dmod_cols = lax.dynamic_slice(small_all, (0, me * mod_w), (N_DEV, mod_w))
    g_w_mod, d_w_mod, nm_w_mod, nv_w_mod = _adam_mod(c_all, dmod_cols, w_mod, m_w_mod[0], v_w_mod[0])
    g_w_in, d_w_in, nm_w_in, nv_w_in = reduce_finish(rs_in, d_w_mod, w_in, m_w_in, v_w_in, "w_in")

    def assemble(w_mod_, small_, w_in_, w_out_, w_up_, w_down_):
        b_mod_, g0_, g1_, g2_, g3_, bf_, sk_ = small_
        return [w_mod_[None], b_mod_, g0_, g1_, w_in_[None], bf_, sk_, w_out_[None], g2_, g3_, w_up_[None], w_down_[None]]

    outs = [loss, grad_x[None]]
    outs += assemble(g_w_mod, g_small, g_w_in, g_w_out, g_w_up, g_w_down)
    outs += assemble(d_w_mod, d_small, d_w_in, d_w_out, d_w_up, d_w_down)
    outs += assemble(nm_w_mod, nm_small, nm_w_in, nm_w_out, nm_w_up, nm_w_down)
    outs += assemble(nv_w_mod, nv_small, nv_w_in, nv_w_out, nv_w_up, nv_w_down)
    return tuple(outs)
```

```python
import functools

import jax
import jax.numpy as jnp
from jax import lax
from jax.experimental import pallas as pl
from jax.experimental.pallas import tpu as pltpu

F32 = jnp.float32
BF16 = jnp.bfloat16
MESH = pl.DeviceIdType.MESH

N_DEV = 8
N_CHIP = 4
LANES = 128
HEAD_DIM = 128
N_FOX = 8
N_SWA = 8
N_KV = 2
GQA = N_SWA // N_KV
WINDOW = 128
FOX_W = N_FOX * HEAD_DIM
SWA_W = N_SWA * HEAD_DIM
KV_W = N_KV * HEAD_DIM
ROPE_THETA = 10000.0
NORM_EPS = 1e-6
ATT_SCALE = HEAD_DIM ** -0.5
FG_PAD = 512

ADAM_LR = 0.001
ADAM_B1 = 0.9
ADAM_B2 = 0.999
ADAM_EPS = 1e-08
ADAM_WD = 0.01
ADAM_STEP = 10

VMEM_LIMIT = 56 * 1024 * 1024

NT_DIMS = (((1,), (1,)), ((), ()))
TN_DIMS = (((0,), (0,)), ((), ()))
NN_DIMS = (((1,), (0,)), ((), ()))


def _pcall(body, *, name, out_shape, grid=(), in_specs=None, out_specs=None, scratch_shapes=(), grid_spec=None):
    params = pltpu.CompilerParams(vmem_limit_bytes=VMEM_LIMIT)
    if grid_spec is not None:
        return pl.pallas_call(body, name=name, out_shape=out_shape, grid_spec=grid_spec, compiler_params=params)
    return pl.pallas_call(body, name=name, out_shape=out_shape, grid=grid, in_specs=in_specs, out_specs=out_specs,
                          scratch_shapes=scratch_shapes, compiler_params=params)


def _blk(n, pref):
    if n <= pref:
        return n
    b = (pref // LANES) * LANES
    while n % b:
        b -= LANES
    return b


def _position():
    return lax.axis_index("x"), lax.axis_index("y"), lax.axis_index("c")


ANY = pl.BlockSpec(memory_space=pl.ANY)


def _all_gather(arrs, name):
    n = len(arrs)

    def body(*refs):
        ins, outs = refs[:n], refs[n:2 * n]
        send_sems, recv_sems, local_sems = refs[2 * n:]
        x, y, c = _position()
        me, sibling = (x, y, c), (x, y, 1 - c)
        chips = [(1 - x, y), (x, 1 - y), (1 - x, 1 - y)]

        def slot(p):
            return 4 * p[0] + 2 * p[1] + p[2]

        def copy(a, k, block, to, src=None):
            dst = outs[a].at[slot(block)]
            return pltpu.make_async_remote_copy(
                src_ref=dst if src is None else src, dst_ref=dst,
                send_sem=send_sems.at[7 * a + k], recv_sem=recv_sems.at[7 * a + k],
                device_id=to, device_id_type=MESH)

        mine = [pltpu.make_async_copy(ins[a], outs[a].at[slot(me)], local_sems.at[a]) for a in range(n)]
        for cp in mine:
            cp.start()
        first = []
        for a in range(n):
            first.append(copy(a, 0, me, sibling, src=ins[a]))
            first += [copy(a, 1 + j, me, (*chip, c), src=ins[a]) for j, chip in enumerate(chips)]
        for cp in first:
            cp.start()
        passed = []
        for a in range(n):
            for j, chip in enumerate(chips):
                copy(a, 1 + j, (*chip, c), me).wait_recv()
                cp = copy(a, 4 + j, (*chip, c), sibling)
                cp.start()
                passed.append(cp)
        for a in range(n):
            copy(a, 0, sibling, me).wait_recv()
            for j, chip in enumerate(chips):
                copy(a, 4 + j, (*chip, 1 - c), me).wait_recv()
        for cp in first + passed:
            cp.wait_send()
        for cp in mine:
            cp.wait()

    return _pcall(
        body, name=name,
        out_shape=[jax.ShapeDtypeStruct((N_DEV,) + a.shape, a.dtype) for a in arrs],
        in_specs=[ANY] * n, out_specs=[ANY] * n,
        scratch_shapes=[pltpu.SemaphoreType.DMA((7 * n,)), pltpu.SemaphoreType.DMA((7 * n,)),
                        pltpu.SemaphoreType.DMA((n,))],
    )(*arrs)


HBM = pl.BlockSpec(memory_space=pltpu.HBM)
SEM = pl.BlockSpec(memory_space=pltpu.SEMAPHORE)
EFFECT = pltpu.SideEffectType.DATAFLOW_SIDE_EFFECTING


def _hbm(a):
    return pltpu.with_memory_space_constraint(a, pltpu.HBM)


def _gather_peers():
    x, y, c = _position()
    return [(x, y, 1 - c), (1 - x, y, c), (x, 1 - y, c), (1 - x, 1 - y, c)]


def _ag_start(shards):
    n = len(shards)
    lands = [_hbm(lax.empty((N_DEV,) + s.shape, s.dtype)) for s in shards]

    def body(*refs):
        srcs, land, send, recv = refs[:n], refs[n:2 * n], refs[2 * n:3 * n], refs[3 * n:4 * n]
        token = refs[6 * n]
        x, y, c = _position()
        for a in range(n):
            for k, to in enumerate(_gather_peers()):
                pltpu.make_async_remote_copy(
                    src_ref=srcs[a], dst_ref=land[a].at[4 * x + 2 * y + c], send_sem=send[a].at[k],
                    recv_sem=recv[a].at[k], device_id=to, device_id_type=MESH).start()
        token[...] = jnp.zeros_like(token)

    sems = [pltpu.SemaphoreType.DMA((4,))] * (2 * n)
    out = pl.pallas_call(
        body, name="ag_start",
        out_shape=sems + [pltpu.HBM(s.shape, s.dtype) for s in shards] + [pltpu.HBM(l.shape, l.dtype) for l in lands]
        + [jax.ShapeDtypeStruct((8, LANES), F32)],
        in_specs=[HBM] * (2 * n), out_specs=[SEM] * (2 * n) + [HBM] * (2 * n) + [pl.BlockSpec(memory_space=pltpu.VMEM)],
        input_output_aliases={**{a: 2 * n + a for a in range(n)}, **{n + a: 3 * n + a for a in range(n)}},
        compiler_params=pltpu.CompilerParams(has_side_effects=EFFECT),
    )(*[_hbm(s) for s in shards], *lands)
    return out[:n], out[n:2 * n], out[2 * n:3 * n], out[3 * n:4 * n], out[4 * n]


def _ag_wait(send, recv, shard_thru, land_thru, after, name):
    def body(v_ref, land_ref, send_sem, recv_sem, after_ref, v_dead, got_ref):
        for k, to in enumerate(_gather_peers()):
            cp = pltpu.make_async_remote_copy(
                src_ref=v_ref, dst_ref=land_ref.at[0], send_sem=send_sem.at[k], recv_sem=recv_sem.at[k],
                device_id=to, device_id_type=MESH)
            cp.wait_send()
            cp.wait_recv()

    return pl.pallas_call(
        body, name=name,
        out_shape=(pltpu.HBM(shard_thru.shape, shard_thru.dtype), pltpu.HBM(land_thru.shape, land_thru.dtype)),
        in_specs=(HBM, HBM, SEM, SEM, ANY), out_specs=(HBM, HBM), input_output_aliases={0: 0, 1: 1},
        compiler_params=pltpu.CompilerParams(has_side_effects=EFFECT),
    )(shard_thru, land_thru, send, recv, after)


def _ag_forward(land, name):
    def body(land_in, land_ref, send_sems, recv_sems):
        x, y, c = _position()
        copies = []
        for j, (px, py) in enumerate([(1 - x, y), (x, 1 - y), (1 - x, 1 - y)]):
            block = land_ref.at[4 * px + 2 * py + c]
            cp = pltpu.make_async_remote_copy(src_ref=block, dst_ref=block, send_sem=send_sems.at[j],
                                              recv_sem=recv_sems.at[j], device_id=(x, y, 1 - c), device_id_type=MESH)
            cp.start()
            copies.append(cp)
        for cp in copies:
            cp.wait()

    return pl.pallas_call(
        body, name=name, out_shape=jax.ShapeDtypeStruct(land.shape, land.dtype),
        in_specs=[ANY], out_specs=ANY, input_output_aliases={0: 0},
        scratch_shapes=[pltpu.SemaphoreType.DMA((3,)), pltpu.SemaphoreType.DMA((3,))],
    )(land)


def _rs_peers():
    x, y, c = _position()
    return [(1 - x, y, c), (x, 1 - y, c), (1 - x, 1 - y, c)]


def _rs_start(chip_sums, name):
    land = _hbm(lax.empty(chip_sums.shape, chip_sums.dtype))

    def body(src, land_ref, send, recv, src_thru, land_thru, token):
        x, y, c = _position()
        for j, (px, py, pc) in enumerate(_rs_peers()):
            pltpu.make_async_remote_copy(
                src_ref=src.at[2 * px + py], dst_ref=land_ref.at[2 * x + y], send_sem=send.at[j], recv_sem=recv.at[j],
                device_id=(px, py, pc), device_id_type=MESH).start()
        token[...] = jnp.zeros_like(token)

    return pl.pallas_call(
        body, name=name,
        out_shape=[pltpu.SemaphoreType.DMA((3,)), pltpu.SemaphoreType.DMA((3,)),
                   pltpu.HBM(chip_sums.shape, chip_sums.dtype), pltpu.HBM(land.shape, land.dtype),
                   jax.ShapeDtypeStruct((8, LANES), F32)],
        in_specs=[HBM, HBM], out_specs=[SEM, SEM, HBM, HBM, pl.BlockSpec(memory_space=pltpu.VMEM)],
        input_output_aliases={0: 2, 1: 3},
        compiler_params=pltpu.CompilerParams(has_side_effects=EFFECT),
    )(_hbm(chip_sums), land)


def _rs_wait(send, recv, src_thru, land_thru, after, name):
    def body(src, land_ref, send_sem, recv_sem, after_ref, src_out, land_out):
        for j, to in enumerate(_rs_peers()):
            cp = pltpu.make_async_remote_copy(
                src_ref=src.at[0], dst_ref=land_ref.at[0], send_sem=send_sem.at[j], recv_sem=recv_sem.at[j],
                device_id=to, device_id_type=MESH)
            cp.wait_send()
            cp.wait_recv()

    return pl.pallas_call(
        body, name=name,
        out_shape=(pltpu.HBM(src_thru.shape, src_thru.dtype), pltpu.HBM(land_thru.shape, land_thru.dtype)),
        in_specs=(HBM, HBM, SEM, SEM, ANY), out_specs=(HBM, HBM), input_output_aliases={0: 0, 1: 1},
        compiler_params=pltpu.CompilerParams(has_side_effects=EFFECT),
    )(src_thru, land_thru, send, recv, after)


def _sib_start(full, name):
    land = _hbm(lax.empty((N_CHIP,) + full.shape[1:], full.dtype))

    def body(src, land_ref, send, recv, src_thru, land_thru, token):
        x, y, c = _position()
        for k in range(N_CHIP):
            pltpu.make_async_remote_copy(
                src_ref=src.at[2 * k + (1 - c)], dst_ref=land_ref.at[k], send_sem=send.at[k], recv_sem=recv.at[k],
                device_id=(x, y, 1 - c), device_id_type=MESH).start()
        token[...] = jnp.zeros_like(token)

    return pl.pallas_call(
        body, name=name,
        out_shape=[pltpu.SemaphoreType.DMA((N_CHIP,)), pltpu.SemaphoreType.DMA((N_CHIP,)),
                   pltpu.HBM(full.shape, full.dtype), pltpu.HBM(land.shape, land.dtype),
                   jax.ShapeDtypeStruct((8, LANES), F32)],
        in_specs=[HBM, HBM], out_specs=[SEM, SEM, HBM, HBM, pl.BlockSpec(memory_space=pltpu.VMEM)],
        input_output_aliases={0: 2, 1: 3},
        compiler_params=pltpu.CompilerParams(has_side_effects=EFFECT),
    )(_hbm(full), land)


def _sib_wait(send, recv, src_thru, land_thru, after, name):
    def body(src, land_ref, send_sem, recv_sem, after_ref, src_out, land_out):
        x, y, c = _position()
        for k in range(N_CHIP):
            cp = pltpu.make_async_remote_copy(
                src_ref=src.at[0], dst_ref=land_ref.at[0], send_sem=send_sem.at[k], recv_sem=recv_sem.at[k],
                device_id=(x, y, 1 - c), device_id_type=MESH)
            cp.wait_send()
            cp.wait_recv()

    return pl.pallas_call(
        body, name=name,
        out_shape=(pltpu.HBM(src_thru.shape, src_thru.dtype), pltpu.HBM(land_thru.shape, land_thru.dtype)),
        in_specs=(HBM, HBM, SEM, SEM, ANY), out_specs=(HBM, HBM), input_output_aliases={0: 0, 1: 1},
        compiler_params=pltpu.CompilerParams(has_side_effects=EFFECT),
    )(src_thru, land_thru, send, recv, after)


def _rs_sibling(arrs, name):
    n = len(arrs)

    def body(*refs):
        ins, outs = refs[:n], refs[n:2 * n]
        send_sems, recv_sems = refs[2 * n:]
        x, y, c = _position()
        copies = []
        for a in range(n):
            for k in range(N_CHIP):
                cp = pltpu.make_async_remote_copy(
                    src_ref=ins[a].at[2 * k + (1 - c)], dst_ref=outs[a].at[k],
                    send_sem=send_sems.at[N_CHIP * a + k], recv_sem=recv_sems.at[N_CHIP * a + k],
                    device_id=(x, y, 1 - c), device_id_type=MESH)
                cp.start()
                copies.append(cp)
        for cp in copies:
            cp.wait()

    return _pcall(
        body, name=name,
        out_shape=[jax.ShapeDtypeStruct((N_CHIP,) + a.shape[1:], a.dtype) for a in arrs],
        in_specs=[ANY] * n, out_specs=[ANY] * n,
        scratch_shapes=[pltpu.SemaphoreType.DMA((N_CHIP * n,)), pltpu.SemaphoreType.DMA((N_CHIP * n,))],
    )(*arrs)


def _chip_sum(full, recv, core, name):
    _, rows, cols = full.shape
    tr = _blk(rows, 1024) if rows % LANES == 0 else rows

    def body(core_ref, a_ref, b_ref, o_ref):
        o_ref[...] = (a_ref[...].astype(F32) + b_ref[...].astype(F32)).astype(o_ref.dtype)

    grid_spec = pltpu.PrefetchScalarGridSpec(
        num_scalar_prefetch=1, grid=(N_CHIP, rows // tr),
        in_specs=[pl.BlockSpec((None, tr, cols), lambda k, i, core_ref: (2 * k + core_ref[0], i, 0)),
                  pl.BlockSpec((None, tr, cols), lambda k, i, core_ref: (k, i, 0))],
        out_specs=pl.BlockSpec((None, tr, cols), lambda k, i, core_ref: (k, i, 0)))
    return _pcall(body, name=name, out_shape=jax.ShapeDtypeStruct((N_CHIP, rows, cols), full.dtype),
                  grid_spec=grid_spec)(core, full, recv)


def _matmul(a, b, *, name, ta=False, tb=False, tm=1024, tn=1024, tk=2048, out_dtypes=(BF16,), epilogue=None,
            row_extras=(), tile_extras=(), out_shape=None, out_map=None, b_sharded=False, n_cols=None, n_off=0,
            after=None):
    m, k = (a.shape[1], a.shape[0]) if ta else a.shape
    if b_sharded:
        shard_c = b.shape[2]
        n, kb = (b.shape[1], N_DEV * shard_c) if tb else (N_DEV * shard_c, b.shape[1])
        tn, tk = (tn, min(tk, shard_c)) if tb else (min(tn, shard_c), tk)
    else:
        n, kb = b.shape if tb else (b.shape[1], b.shape[0])
    assert kb == k, (name, kb, k)
    if n_cols is not None:
        n = n_cols
    tm, tn, tk = _blk(m, tm), _blk(n, tn), _blk(k, tk)
    assert n_off % tn == 0
    nk = k // tk
    dims = (((0 if ta else 1,), (1 if tb else 0,)), ((), ()))
    behind = () if after is None else (after,)
    n_row, n_tile, n_out = len(row_extras), len(tile_extras), len(out_dtypes)
    first_out = 2 + n_row + n_tile + len(behind)

    def body(*refs):
        a_ref, b_ref = refs[:2]
        extras = refs[2:2 + n_row + n_tile]
        outs = refs[first_out:first_out + n_out]
        acc_ref = refs[-1]
        jj, kk = pl.program_id(1), pl.program_id(2)
        part = lax.dot_general(a_ref[...].astype(BF16), b_ref[...].astype(BF16), dims, preferred_element_type=F32)

        def finish(acc):
            res = (acc,) if epilogue is None else epilogue(acc, jj, *[e[...] for e in extras])
            for o_ref, r in zip(outs, res):
                o_ref[...] = r.astype(o_ref.dtype)

        if nk == 1:
            finish(part)
        else:
            @pl.when(kk == 0)
            def _():
                acc_ref[...] = part

            @pl.when(kk > 0)
            def _():
                acc_ref[...] += part

            @pl.when(kk == nk - 1)
            def _():
                finish(acc_ref[...])

    a_spec = pl.BlockSpec((tk, tm), lambda i, j, kk: (kk, i)) if ta else pl.BlockSpec((tm, tk), lambda i, j, kk: (i, kk))
    if b_sharded and tb:
        per = shard_c // tk
        b_spec = pl.BlockSpec((None, tn, tk), lambda i, j, kk: (kk // per, j, kk % per))
    elif b_sharded:
        per = shard_c // tn
        b_spec = pl.BlockSpec((None, tk, tn), lambda i, j, kk: (j // per, kk, j % per))
    elif tb:
        b_spec = pl.BlockSpec((tn, tk), lambda i, j, kk: (j, kk))
    else:
        b_spec = pl.BlockSpec((tk, tn), lambda i, j, kk: (kk, j + n_off // tn))
    in_specs = [a_spec, b_spec]
    in_specs += [pl.BlockSpec((tm, LANES), lambda i, j, kk: (i, 0)) for _ in row_extras]
    in_specs += [pl.BlockSpec((tm, tn), lambda i, j, kk: (i, j)) for _ in tile_extras]
    in_specs += [ANY for _ in behind]
    if out_map is None:
        out_specs = [pl.BlockSpec((tm, tn), lambda i, j, kk: (i, j)) for _ in out_dtypes]
        shapes = [jax.ShapeDtypeStruct((m, n), dt) for dt in out_dtypes]
    else:
        out_specs = [out_map(tm, tn)]
        shapes = [jax.ShapeDtypeStruct(out_shape, out_dtypes[0])]
    acc_shape = (tm, tn) if nk > 1 else (8, LANES)
    res = _pcall(body, name=name, out_shape=shapes, grid=(m // tm, n // tn, nk), in_specs=in_specs,
                 out_specs=out_specs, scratch_shapes=[pltpu.VMEM(acc_shape, F32)])(
                     a, b, *row_extras, *tile_extras, *behind)
    return res[0] if n_out == 1 else res


def _rope_cols(acc, j, cos, sin, n_rope):
    width = acc.shape[1]
    parts = []
    for g in range(width // HEAD_DIM):
        xg = acc[:, g * HEAD_DIM:(g + 1) * HEAD_DIM]
        roped = xg * cos + pltpu.roll(xg, HEAD_DIM // 2, 1) * sin
        parts.append(jnp.where(j * width + g * HEAD_DIM < n_rope, roped, xg))
    return jnp.concatenate(parts, axis=1) if len(parts) > 1 else parts[0]


def _silu(v):
    return v / (1.0 + jnp.exp(-v))


def _mod_part(c_all, w_mod, b_part):
    d, w = w_mod.shape
    tk = _blk(d, 512)

    def body(c_ref, w_ref, b_ref, o_ref):
        kk = pl.program_id(0)
        cond = _silu(c_ref[...]).astype(BF16)
        part = jnp.dot(cond, w_ref[...].astype(BF16), preferred_element_type=F32)

        @pl.when(kk == 0)
        def _():
            o_ref[...] = part + b_ref[...]

        @pl.when(kk > 0)
        def _():
            o_ref[...] += part

    return _pcall(body, name="mod_part", out_shape=jax.ShapeDtypeStruct((N_DEV, w), F32), grid=(d // tk,),
                  in_specs=[pl.BlockSpec((N_DEV, tk), lambda kk: (0, kk)), pl.BlockSpec((tk, w), lambda kk: (kk, 0)),
                            pl.BlockSpec((1, w), lambda kk: (0, 0))],
                  out_specs=pl.BlockSpec((N_DEV, w), lambda kk: (0, 0)))(c_all, w_mod, b_part)


def _row_call(body, name, t, d, tiled_in, vec_in, tiled_out_dtypes, n_vec_out, tr=256):
    tr = _blk(t, tr)
    tile = pl.BlockSpec((tr, d), lambda i: (i, 0))
    vec = pl.BlockSpec((1, d), lambda i: (0, 0))
    out_shape = [jax.ShapeDtypeStruct((t, d), dt) for dt in tiled_out_dtypes]
    out_shape += [jax.ShapeDtypeStruct((1, d), F32)] * n_vec_out
    return _pcall(body, name=name, out_shape=out_shape, grid=(t // tr,),
                  in_specs=[tile] * len(tiled_in) + [vec] * len(vec_in),
                  out_specs=[tile] * len(tiled_out_dtypes) + [vec] * n_vec_out)(*tiled_in, *vec_in)


def _accumulate(ref, val):
    @pl.when(pl.program_id(0) == 0)
    def _():
        ref[...] = val

    @pl.when(pl.program_id(0) > 0)
    def _():
        ref[...] += val


def _rsum(v):
    return jnp.sum(v, axis=0, keepdims=True)


def _rms(v):
    return lax.rsqrt(jnp.mean(v * v, axis=-1, keepdims=True) + NORM_EPS)


def _rms_bwd(vhat, r, dvhat):
    return r * (dvhat - vhat * jnp.mean(dvhat * vhat, axis=-1, keepdims=True))


def _pre_attn(x, g0, sc_a, sh_a):
    def body(x_ref, g_ref, sc_ref, sh_ref, h_ref):
        xv = x_ref[...]
        h_ref[...] = (xv * _rms(xv) * g_ref[...] * (1.0 + sc_ref[...]) + sh_ref[...]).astype(BF16)

    t, d = x.shape
    return _row_call(body, "pre_attn", t, d, [x], [g0, sc_a, sh_a], [BF16], 0)[0]


def _post_mix(x, mix, gt_a, g1, g2, sc_m, sh_m):
    def body(x_ref, mix_ref, gt_ref, g1_ref, g2_ref, sc_ref, sh_ref, x2_ref, h2_ref):
        mv = mix_ref[...]
        x2 = x_ref[...] + gt_ref[...] * (mv * _rms(mv) * g1_ref[...])
        x2_ref[...] = x2
        h2_ref[...] = (x2 * _rms(x2) * g2_ref[...] * (1.0 + sc_ref[...]) + sh_ref[...]).astype(BF16)

    t, d = x.shape
    return _row_call(body, "post_mix", t, d, [x, mix], [gt_a, g1, g2, sc_m, sh_m], [F32, BF16], 0)


def _final(y, x2, target, gt_m, g3):
    t, d = y.shape

    def body(y_ref, x2_ref, tg_ref, gt_ref, g3_ref, dy_ref, dout_ref, dgt_ref, dg3_ref, loss_ref):
        yv = y_ref[...]
        r = _rms(yv)
        yhat = yv * r
        n3 = yhat * g3_ref[...]
        err = x2_ref[...] + gt_ref[...] * n3 - tg_ref[...]
        _accumulate(loss_ref, jnp.zeros((1, d), F32) + 0.5 * jnp.sum(err * err) / d)
        dout = err * (1.0 / d)
        dout_ref[...] = dout
        _accumulate(dgt_ref, _rsum(dout * n3))
        dn3 = dout * gt_ref[...]
        _accumulate(dg3_ref, _rsum(dn3 * yhat))
        dy_ref[...] = _rms_bwd(yhat, r, dn3 * g3_ref[...]).astype(BF16)

    return _row_call(body, "final", t, d, [y, x2, target], [gt_m, g3], [BF16, F32], 3)


def _mid_bwd(dh2, dout, x2, mix, g2, sc_m, gt_a, g1):
    t, d = x2.shape

    def body(dh2_ref, dout_ref, x2_ref, mix_ref, g2_ref, sc_ref, gt_ref, g1_ref,
             dmix_ref, dx2_ref, dsh_ref, dsc_ref, dg2_ref, dgt_ref, dg1_ref):
        dh2v = dh2_ref[...]
        x2v = x2_ref[...]
        r2 = _rms(x2v)
        x2hat = x2v * r2
        _accumulate(dsh_ref, _rsum(dh2v))
        _accumulate(dsc_ref, _rsum(dh2v * (x2hat * g2_ref[...])))
        dn2 = dh2v * (1.0 + sc_ref[...])
        _accumulate(dg2_ref, _rsum(dn2 * x2hat))
        dx2 = dout_ref[...] + _rms_bwd(x2hat, r2, dn2 * g2_ref[...])
        dx2_ref[...] = dx2
        mv = mix_ref[...]
        r1 = _rms(mv)
        mhat = mv * r1
        _accumulate(dgt_ref, _rsum(dx2 * (mhat * g1_ref[...])))
        dn1 = dx2 * gt_ref[...]
        _accumulate(dg1_ref, _rsum(dn1 * mhat))
        dmix_ref[...] = _rms_bwd(mhat, r1, dn1 * g1_ref[...]).astype(BF16)

    return _row_call(body, "mid_bwd", t, d, [dh2, dout, x2, mix], [g2, sc_m, gt_a, g1], [BF16, F32], 5)


def _x_bwd(dh1, dx2, x, g0, sc_a):
    t, d = x.shape

    def body(dh1_ref, dx2_ref, x_ref, g0_ref, sc_ref, dx_ref, dsh_ref, dsc_ref, dg0_ref):
        dh1v = dh1_ref[...]
        xv = x_ref[...]
        r0 = _rms(xv)
        xhat = xv * r0
        _accumulate(dsh_ref, _rsum(dh1v))
        _accumulate(dsc_ref, _rsum(dh1v * (xhat * g0_ref[...])))
        dn0 = dh1v * (1.0 + sc_ref[...])
        _accumulate(dg0_ref, _rsum(dn0 * xhat))
        dx_ref[...] = dx2_ref[...] + _rms_bwd(xhat, r0, dn0 * g0_ref[...])

    return _row_call(body, "x_bwd", t, d, [dh1, dx2, x], [g0, sc_a], [F32], 3)


def _pick_lane(block, h):
    lane = lax.broadcasted_iota(jnp.int32, block.shape, 1)
    return jnp.sum(jnp.where(lane == h, block, 0.0), axis=1, keepdims=True)


def _put_lane(ref, rows, h, col):
    old = ref[rows, :]
    lane = lax.broadcasted_iota(jnp.int32, old.shape, 1)
    ref[rows, :] = jnp.where(lane == h, col, old)


def _tri(n, lower):
    r = lax.broadcasted_iota(jnp.int32, (n, n), 0)
    c = lax.broadcasted_iota(jnp.int32, (n, n), 1)
    return jnp.where((c <= r) if lower else (c >= r), 1.0, 0.0).astype(F32)


def _cum_fwd(fg, b128):
    t = fg.shape[0]
    nb = t // LANES

    def body(fg_ref, b_ref, cum_ref, cumt_ref):
        tri = _tri(LANES, True)
        carry = jnp.zeros((1, LANES), F32)
        for i in range(nb):
            z = fg_ref[i * LANES:(i + 1) * LANES, :] + b_ref[...]
            lf = jnp.minimum(z, 0.0) - jnp.log(1.0 + jnp.exp(-jnp.abs(z)))
            blk = jnp.dot(tri, lf, precision=lax.Precision.HIGHEST, preferred_element_type=F32) + carry
            cum_ref[i * LANES:(i + 1) * LANES, :] = blk
            carry = blk[LANES - 1:LANES, :]
        cumt_ref[...] = cum_ref[...].T[0:N_FOX, :]

    return _pcall(body, name="cum_fwd",
                  out_shape=[jax.ShapeDtypeStruct((t, LANES), F32), jax.ShapeDtypeStruct((N_FOX, t), F32)],
                  grid=(1,),
                  in_specs=[pl.BlockSpec((t, LANES), lambda i: (0, 0)), pl.BlockSpec((1, LANES), lambda i: (0, 0))],
                  out_specs=[pl.BlockSpec((t, LANES), lambda i: (0, 0)), pl.BlockSpec((N_FOX, t), lambda i: (0, 0))],
                  )(fg, b128)


def _fg_bwd(dcs_rows, fg, b128):
    t = fg.shape[0]
    nb = t // LANES

    def body(dcs_ref, fg_ref, b_ref, dfg_ref, db_ref, dcum_ref):
        dcum_ref[...] = -jnp.concatenate([dcs_ref[...], jnp.zeros((LANES - N_FOX, t), F32)], axis=0).T
        tri = _tri(LANES, False)
        carry = jnp.zeros((1, LANES), F32)
        db = jnp.zeros((1, LANES), F32)
        for i in reversed(range(nb)):
            rows = slice(i * LANES, (i + 1) * LANES)
            dlf = jnp.dot(tri, dcum_ref[rows, :], precision=lax.Precision.HIGHEST, preferred_element_type=F32) + carry
            carry = dlf[0:1, :]
            z = fg_ref[rows, :] + b_ref[...]
            dfg = dlf / (1.0 + jnp.exp(z))
            dfg_ref[rows, :] = dfg.astype(BF16)
            db = db + _rsum(dfg)
        db_ref[...] = db

    full = pl.BlockSpec((t, LANES), lambda i: (0, 0))
    vec = pl.BlockSpec((1, LANES), lambda i: (0, 0))
    return _pcall(body, name="fg_bwd",
                  out_shape=[jax.ShapeDtypeStruct((t, LANES), BF16), jax.ShapeDtypeStruct((1, LANES), F32)],
                  grid=(1,), in_specs=[pl.BlockSpec((N_FOX, t), lambda i: (0, 0)), full, vec], out_specs=[full, vec],
                  scratch_shapes=[pltpu.VMEM((t, LANES), F32)])(dcs_rows, fg, b128)


def _head_spec(t, col0, div=1):
    return pl.BlockSpec((t, HEAD_DIM), lambda h: (0, col0 + h // div))


def _fox_scores(q, k, cq, ck, i, tq, end):
    s = lax.dot_general(q, k, NT_DIMS, preferred_element_type=F32) * ATT_SCALE + cq - ck
    row = lax.broadcasted_iota(jnp.int32, (tq, end), 0) + i * tq
    col = lax.broadcasted_iota(jnp.int32, (tq, end), 1)
    return jnp.where(row >= col, s, -jnp.inf)


def _fox_fwd(proj_a, cum, cumt):
    t = proj_a.shape[0]
    tq = _blk(t, 512)
    nq = t // tq

    def body(q_ref, k_ref, v_ref, cum_ref, cumt_ref, o_ref, lse_ref):
        h = pl.program_id(0)
        cq_all = _pick_lane(cum_ref[...], h)
        ck_all = cumt_ref[pl.ds(h, 1), :]

        @pl.when(h == 0)
        def _():
            lse_ref[...] = jnp.zeros_like(lse_ref)

        for i in range(nq):
            rows, end = slice(i * tq, (i + 1) * tq), (i + 1) * tq
            s = _fox_scores(q_ref[rows, :], k_ref[0:end, :], cq_all[rows, :], ck_all[:, 0:end], i, tq, end)
            m = jnp.max(s, axis=1, keepdims=True)
            p = jnp.exp(s - m)
            l = jnp.sum(p, axis=1, keepdims=True)
            o = jnp.dot(p.astype(BF16), v_ref[0:end, :], preferred_element_type=F32) / l
            o_ref[rows, :] = o.astype(BF16)
            _put_lane(lse_ref, rows, h, m + jnp.log(l))

    nh = FOX_W // HEAD_DIM
    stat = pl.BlockSpec((t, LANES), lambda h: (0, 0))
    return _pcall(body, name="fox_fwd",
                  out_shape=[jax.ShapeDtypeStruct((t, FOX_W), BF16), jax.ShapeDtypeStruct((t, LANES), F32)],
                  grid=(N_FOX,),
                  in_specs=[_head_spec(t, 0), _head_spec(t, nh), _head_spec(t, 2 * nh), stat,
                            pl.BlockSpec((N_FOX, t), lambda h: (0, 0))],
                  out_specs=[_head_spec(t, 0), stat])(proj_a, proj_a, proj_a, cum, cumt)


def _fox_bwd(proj_a, d_attn, cum, cumt, lse):
    t = proj_a.shape[0]
    tq = _blk(t, 512)
    nq = t // tq

    def body(q_ref, k_ref, v_ref, do_ref, cum_ref, cumt_ref, lse_ref,
             dq_ref, dk_ref, dv_ref, dcs_ref, dk_acc, dv_acc, dcs_acc):
        h = pl.program_id(0)
        cq_all = _pick_lane(cum_ref[...], h)
        ck_all = cumt_ref[pl.ds(h, 1), :]
        lse_all = _pick_lane(lse_ref[...], h)
        dk_acc[...] = jnp.zeros_like(dk_acc)
        dv_acc[...] = jnp.zeros_like(dv_acc)
        dcs_acc[...] = jnp.zeros_like(dcs_acc)
        for i in range(nq):
            rows, end = slice(i * tq, (i + 1) * tq), (i + 1) * tq
            q, k, v, do = q_ref[rows, :], k_ref[0:end, :], v_ref[0:end, :], do_ref[rows, :]
            s = _fox_scores(q, k, cq_all[rows, :], ck_all[:, 0:end], i, tq, end)
            p = jnp.exp(s - lse_all[rows, :])
            dp = lax.dot_general(do, v, NT_DIMS, preferred_element_type=F32)
            ds = p * (dp - jnp.sum(p * dp, axis=1, keepdims=True))
            dcs_acc[:, 0:end] += jnp.sum(ds, axis=0, keepdims=True)
            ds = ds.astype(BF16)
            dq_ref[rows, :] = (jnp.dot(ds, k, preferred_element_type=F32) * ATT_SCALE).astype(BF16)
            dk_acc[0:end, :] += lax.dot_general(ds, q, TN_DIMS, preferred_element_type=F32)
            dv_acc[0:end, :] += lax.dot_general(p.astype(BF16), do, TN_DIMS, preferred_element_type=F32)
        dk_ref[...] = (dk_acc[...] * ATT_SCALE).astype(BF16)
        dv_ref[...] = dv_acc[...].astype(BF16)
        dcs_ref[pl.ds(h, 1), :] = dcs_acc[...]

    nh = FOX_W // HEAD_DIM
    stat = pl.BlockSpec((t, LANES), lambda h: (0, 0))
    rows8 = pl.BlockSpec((N_FOX, t), lambda h: (0, 0))
    head = _head_spec(t, 0)
    wide = jax.ShapeDtypeStruct((t, FOX_W), BF16)
    return _pcall(body, name="fox_bwd",
                  out_shape=[wide, wide, wide, jax.ShapeDtypeStruct((N_FOX, t), F32)],
                  grid=(N_FOX,),
                  in_specs=[_head_spec(t, 0), _head_spec(t, nh), _head_spec(t, 2 * nh), head, stat, rows8, stat],
                  out_specs=[head, head, head, rows8],
                  scratch_shapes=[pltpu.VMEM((t, HEAD_DIM), F32), pltpu.VMEM((t, HEAD_DIM), F32),
                                  pltpu.VMEM((1, t), F32)],
                  )(proj_a, proj_a, proj_a, d_attn, cum, cumt, lse)


def _swa_scores(q, k, i, tq, start, end):
    s = lax.dot_general(q, k, NT_DIMS, preferred_element_type=F32) * ATT_SCALE
    row = lax.broadcasted_iota(jnp.int32, (tq, end - start), 0) + i * tq
    col = lax.broadcasted_iota(jnp.int32, (tq, end - start), 1) + start
    diff = row - col
    return jnp.where((diff >= 0) & (diff < WINDOW), s, -jnp.inf)


def _swa_blocks(t):
    tq = _blk(t, 256)
    return tq, [(i, max(0, i * tq - WINDOW), (i + 1) * tq) for i in range(t // tq)]


def _swa_fwd(proj_b, sinks128):
    t = proj_b.shape[0]
    tq, blocks = _swa_blocks(t)

    def body(q_ref, k_ref, v_ref, sink_ref, o_ref, lse_ref):
        h = pl.program_id(0)
        sink = _pick_lane(sink_ref[...], h)

        @pl.when(h == 0)
        def _():
            lse_ref[...] = jnp.zeros_like(lse_ref)

        for i, start, end in blocks:
            rows = slice(i * tq, end)
            s = _swa_scores(q_ref[rows, :], k_ref[start:end, :], i, tq, start, end)
            m = jnp.maximum(jnp.max(s, axis=1, keepdims=True), sink)
            p = jnp.exp(s - m)
            l = jnp.sum(p, axis=1, keepdims=True) + jnp.exp(sink - m)
            o = jnp.dot(p.astype(BF16), v_ref[start:end, :], preferred_element_type=F32) / l
            o_ref[rows, :] = o.astype(BF16)
            _put_lane(lse_ref, rows, h, m + jnp.log(l))

    stat = pl.BlockSpec((t, LANES), lambda h: (0, 0))
    return _pcall(body, name="swa_fwd",
                  out_shape=[jax.ShapeDtypeStruct((t, SWA_W), BF16), jax.ShapeDtypeStruct((t, LANES), F32)],
                  grid=(N_SWA,),
                  in_specs=[_head_spec(t, 0), _head_spec(t, N_SWA, GQA), _head_spec(t, N_SWA + N_KV, GQA),
                            pl.BlockSpec((1, LANES), lambda h: (0, 0))],
                  out_specs=[_head_spec(t, 0), stat])(proj_b, proj_b, proj_b, sinks128)


def _rope_bwd(d, cos, sin):
    return d * cos + pltpu.roll(d * sin, HEAD_DIM // 2, 1)


def _swa_bwd(proj_b, d_attn, lse, sinks128, cos, sin):
    t = proj_b.shape[0]
    tq, blocks = _swa_blocks(t)

    def body(q_ref, k_ref, v_ref, do_ref, lse_ref, sink_ref, cos_ref, sin_ref,
             dq_ref, dk_ref, dv_ref, dsink_ref, dk_acc, dv_acc):
        h = pl.program_id(0)
        sink = _pick_lane(sink_ref[...], h)
        lse_all = _pick_lane(lse_ref[...], h)

        @pl.when(h == 0)
        def _():
            dsink_ref[...] = jnp.zeros_like(dsink_ref)

        @pl.when(h % GQA == 0)
        def _():
            dk_acc[...] = jnp.zeros_like(dk_acc)
            dv_acc[...] = jnp.zeros_like(dv_acc)

        dsink = jnp.zeros((1, 1), F32)
        for i, start, end in blocks:
            rows = slice(i * tq, end)
            q, k, v, do = q_ref[rows, :], k_ref[start:end, :], v_ref[start:end, :], do_ref[rows, :]
            s = _swa_scores(q, k, i, tq, start, end)
            p = jnp.exp(s - lse_all[rows, :])
            dp = lax.dot_general(do, v, NT_DIMS, preferred_element_type=F32)
            delta = jnp.sum(p * dp, axis=1, keepdims=True)
            ds = (p * (dp - delta)).astype(BF16)
            dq = jnp.dot(ds, k, preferred_element_type=F32) * ATT_SCALE
            dq_ref[rows, :] = _rope_bwd(dq, cos_ref[rows, :], sin_ref[rows, :]).astype(BF16)
            dk_acc[start:end, :] += lax.dot_general(ds, q, TN_DIMS, preferred_element_type=F32)
            dv_acc[start:end, :] += lax.dot_general(p.astype(BF16), do, TN_DIMS, preferred_element_type=F32)
            dsink = dsink - jnp.sum(jnp.exp(sink - lse_all[rows, :]) * delta, axis=0, keepdims=True)
        old = dsink_ref[...]
        lane = lax.broadcasted_iota(jnp.int32, old.shape, 1)
        dsink_ref[...] = jnp.where(lane == h, dsink, old)

        @pl.when(h % GQA == GQA - 1)
        def _():
            dk_ref[...] = _rope_bwd(dk_acc[...] * ATT_SCALE, cos_ref[...], sin_ref[...]).astype(BF16)
            dv_ref[...] = dv_acc[...].astype(BF16)

    stat = pl.BlockSpec((t, LANES), lambda h: (0, 0))
    vec = pl.BlockSpec((1, LANES), lambda h: (0, 0))
    head = _head_spec(t, 0)
    kv_out = _head_spec(t, 0, GQA)
    return _pcall(body, name="swa_bwd",
                  out_shape=[jax.ShapeDtypeStruct((t, SWA_W), BF16), jax.ShapeDtypeStruct((t, KV_W), BF16),
                             jax.ShapeDtypeStruct((t, KV_W), BF16), jax.ShapeDtypeStruct((1, LANES), F32)],
                  grid=(N_SWA,),
                  in_specs=[head, _head_spec(t, N_SWA, GQA), _head_spec(t, N_SWA + N_KV, GQA),
                            _head_spec(t, N_FOX), stat, vec, stat, stat],
                  out_specs=[head, kv_out, kv_out, vec],
                  scratch_shapes=[pltpu.VMEM((t, HEAD_DIM), F32), pltpu.VMEM((t, HEAD_DIM), F32)],
                  )(proj_b, proj_b, proj_b, d_attn, lse, sinks128, cos, sin)


def _adamw(w, g, m, v):
    m = ADAM_B1 * m + (1.0 - ADAM_B1) * g
    v = ADAM_B2 * v + (1.0 - ADAM_B2) * (g * g)
    m_hat = m / (1.0 - ADAM_B1 ** ADAM_STEP)
    v_hat = v / (1.0 - ADAM_B2 ** ADAM_STEP)
    delta = -ADAM_LR * (m_hat / (jnp.sqrt(v_hat) + ADAM_EPS) + ADAM_WD * w)
    return delta, m, v


def _adam_pieces(w, m, v, own, land, idx, name):
    rows, cols = w.shape
    tr = 256 if rows % 256 == 0 else rows

    def body(idx_ref, own_ref, l1_ref, l2_ref, l3_ref, w_ref, m_ref, v_ref, g_ref, d_ref, mo_ref, vo_ref):
        g = own_ref[...].astype(F32) + l1_ref[...].astype(F32) + l2_ref[...].astype(F32) + l3_ref[...].astype(F32)
        g_ref[...] = g
        d_ref[...], mo_ref[...], vo_ref[...] = _adamw(w_ref[...], g, m_ref[...], v_ref[...])

    def piece(p):
        return pl.BlockSpec((None, tr, cols), lambda i, idx_ref: (idx_ref[p], i, 0))

    tile = pl.BlockSpec((tr, cols), lambda i, idx_ref: (i, 0))
    out = jax.ShapeDtypeStruct((rows, cols), F32)
    grid_spec = pltpu.PrefetchScalarGridSpec(
        num_scalar_prefetch=1, grid=(rows // tr,),
        in_specs=[piece(0), piece(1), piece(2), piece(3), tile, tile, tile], out_specs=[tile] * 4)
    return _pcall(body, name=name, out_shape=[out] * 4, grid_spec=grid_spec)(idx, own, land, land, land, w, m, v)


def _adam_mod(c_all, dmod_cols, w, m, v):
    rows, cols = w.shape
    tr = _blk(rows, 256)

    def body(c_ref, dm_ref, w_ref, m_ref, v_ref, g_ref, d_ref, mo_ref, vo_ref):
        cond = _silu(c_ref[...]).astype(BF16)
        g = lax.dot_general(cond, dm_ref[...].astype(BF16), TN_DIMS, preferred_element_type=F32)
        g_ref[...] = g
        d_ref[...], mo_ref[...], vo_ref[...] = _adamw(w_ref[...], g, m_ref[...], v_ref[...])

    tile = pl.BlockSpec((tr, cols), lambda i: (i, 0))
    out = jax.ShapeDtypeStruct((rows, cols), F32)
    return _pcall(body, name="adam_mod", out_shape=[out] * 4, grid=(rows // tr,),
                  in_specs=[pl.BlockSpec((N_DEV, tr), lambda i: (0, i)), pl.BlockSpec((N_DEV, cols), lambda i: (0, 0)),
                            tile, tile, tile],
                  out_specs=[tile] * 4)(c_all, dmod_cols, w, m, v)


def _adam_small(parts, w, m, v):
    nv = w.shape[1]

    def body(p_ref, w_ref, m_ref, v_ref, g_ref, d_ref, mo_ref, vo_ref):
        g = p_ref[0:1, :]
        for k in range(1, N_DEV):
            g = g + p_ref[k:k + 1, :]
        g_ref[...] = g
        d_ref[...], mo_ref[...], vo_ref[...] = _adamw(w_ref[...], g, m_ref[...], v_ref[...])

    vec = pl.BlockSpec((1, nv), lambda i: (0, 0))
    out = jax.ShapeDtypeStruct((1, nv), F32)
    return _pcall(body, name="adam_small", out_shape=[out] * 4, grid=(1,),
                  in_specs=[pl.BlockSpec((N_DEV, nv), lambda i: (0, 0)), vec, vec, vec],
                  out_specs=[vec] * 4)(parts, w, m, v)


def _pad_lanes(v, width=LANES):
    return jnp.pad(v, ((0, 0), (0, width - v.shape[1])))


def kernel(x, c, w_mod, b_mod, g_pre_mix, g_post_mix, w_in, b_forget, swa_sinks, w_out, g_pre_mlp, g_post_mlp, w_up, w_down, loss_target, m_w_mod, m_b_mod, m_g_pre_mix, m_g_post_mix, m_w_in, m_b_forget, m_swa_sinks, m_w_out, m_g_pre_mlp, m_g_post_mlp, m_w_up, m_w_down, v_w_mod, v_b_mod, v_g_pre_mix, v_g_post_mix, v_w_in, v_b_forget, v_swa_sinks, v_w_out, v_g_pre_mlp, v_g_post_mlp, v_w_up, v_w_down):
    ax, ay, ac = _position()
    me = 4 * ax + 2 * ay + ac
    x, target = x[0], loss_target[0]
    t, d = x.shape
    w_mod, w_in, w_out, w_up, w_down = w_mod[0], w_in[0], w_out[0], w_up[0], w_down[0]
    mod_w = w_mod.shape[1]
    in_w = w_in.shape[1]
    in_total = N_DEV * in_w
    shard_ff = w_up.shape[1]
    n_fox3 = 3 * FOX_W
    n_swa3 = SWA_W + 2 * KV_W
    assert in_total == n_fox3 + N_FOX + n_swa3 and d == FOX_W + SWA_W

    c_all = _all_gather([c], "gather_c")[0].reshape(N_DEV, d)
    b_part = lax.dynamic_slice(b_mod, (0, me * mod_w), (1, mod_w))
    mod_parts = _all_gather([_mod_part(c_all, w_mod, b_part)], "gather_mod")[0]
    mod = lax.dynamic_index_in_dim(mod_parts, me, axis=1, keepdims=False).reshape(1, N_DEV * mod_w)

    shards = [w_in.astype(BF16), w_out.astype(BF16), w_up.astype(BF16), w_down.astype(BF16)]
    shards, mod = lax.optimization_barrier((shards, mod))
    ag_send, ag_recv, ag_shard, ag_land, ag_token = _ag_start(shards)

    def gathered(i, after, name):
        shard, land = _ag_wait(ag_send[i], ag_recv[i], ag_shard[i], ag_land[i], after, "ag_wait_" + name)
        return lax.dynamic_update_slice(_ag_forward(land, "ag_fwd_" + name), shard[None], (me, 0, 0))

    sh_a, sc_a, gt_a, sh_m, sc_m, gt_m = [mod[:, i * d:(i + 1) * d] for i in range(6)]

    half = HEAD_DIM // 2
    inv_freq = 1.0 / (ROPE_THETA ** (jnp.arange(half, dtype=F32) * (2.0 / HEAD_DIM)))
    ang = jnp.arange(t).astype(F32)[:, None] * inv_freq[None, :]
    cos = jnp.concatenate([jnp.cos(ang), jnp.cos(ang)], axis=1)
    sin = jnp.concatenate([-jnp.sin(ang), jnp.sin(ang)], axis=1)

    b128 = _pad_lanes(b_forget)
    sinks128 = _pad_lanes(swa_sinks)

    h1 = _pre_attn(x, g_pre_mix + ag_token[0:1, 0:1], sc_a, sh_a)
    w_in_g = gathered(0, h1, "in")
    o_fg, o_sq, o_sv = n_fox3, n_fox3 + N_FOX, n_fox3 + N_FOX + SWA_W + KV_W

    def cols(lo, hi):
        parts = []
        for j in range(lo // in_w, (hi - 1) // in_w + 1):
            parts.append(w_in_g[j, :, max(lo - j * in_w, 0):min(hi - j * in_w, in_w)])
        return parts

    w_in_r = jnp.concatenate(cols(0, o_fg) + cols(o_sq, in_total) + cols(o_fg, o_sq)
                             + [jnp.zeros((d, FG_PAD - N_FOX), BF16)], axis=1)
    proj_a = _matmul(h1, w_in_r, name="proj_a", n_cols=n_fox3, n_off=0)
    proj_b = _matmul(h1, w_in_r, name="proj_b", n_cols=n_swa3, n_off=n_fox3, tn=512, row_extras=(cos, sin),
                     epilogue=lambda acc, j, cs, sn: (_rope_cols(acc, j, cs, sn, SWA_W + KV_W),))
    fg = _matmul(h1, w_in_r, name="proj_fg", n_cols=FG_PAD, n_off=n_fox3 + n_swa3, tn=FG_PAD,
                 out_dtypes=(F32,))[:, 0:LANES]
    cum, cumt = _cum_fwd(fg, b128)
    fox_o, fox_lse = _fox_fwd(proj_a, cum, cumt)
    swa_o, swa_lse = _swa_fwd(proj_b, sinks128)
    w_out_full = gathered(1, swa_o, "out").reshape(d, d)
    attn = jnp.concatenate([fox_o, swa_o], axis=1)
    mix = _matmul(attn, w_out_full, name="out_proj", out_dtypes=(F32,))
    x2, h2 = _post_mix(x, mix, gt_a, g_post_mix, g_pre_mlp, sc_m, sh_m)
    w_up_g = gathered(2, h2, "up")
    u, act = _matmul(h2, w_up_g, name="mlp_up", b_sharded=True, out_dtypes=(BF16, BF16),
                     epilogue=lambda acc, j: (acc, jnp.square(jnp.maximum(acc, 0.0))))
    w_down_full = gathered(3, act, "down").reshape(N_DEV * shard_ff, d)
    y = _matmul(act, w_down_full, name="mlp_down", out_dtypes=(F32,))

    core = jnp.reshape(ac, (1,)).astype(jnp.int32)

    def reduce_start(started, after, name):
        send, recv, src, land, _ = started
        full, from_sibling = _sib_wait(send, recv, src, land, after, "sib_wait_" + name)
        return _rs_start(_chip_sum(full, from_sibling, core, "chip_sum_" + name), "rs_start_" + name)

    def tok(started):
        return started[4][0:1, 0:1]

    idx = jnp.stack([2 * ax + ay, 2 * (1 - ax) + ay, 2 * ax + (1 - ay), 2 * (1 - ax) + (1 - ay)]).astype(jnp.int32)

    def reduce_finish(started, after, w, m, v, name):
        send, recv, src, land, _ = started
        own, landed = _rs_wait(send, recv, src, land, after, "rs_wait_" + name)
        return _adam_pieces(w, m[0], v[0], own, landed, idx, "adam_" + name)

    dy, dout, dgt_m, dg3, loss_vec = _final(y, x2, target, gt_m, g_post_mlp)
    du = _matmul(dy, w_down_full, name="d_act", tb=True, tile_extras=(u,),
                 epilogue=lambda acc, j, uu: (acc * (2.0 * jnp.maximum(uu.astype(F32), 0.0)),))
    dw_down = _matmul(act, dy, name="dw_down", ta=True)
    sb_down = _sib_start(dw_down.reshape(N_DEV, shard_ff, d), "sib_start_down")
    dh2 = _matmul(du, w_up_g, name="d_h2", tb=True, b_sharded=True, out_dtypes=(F32,), after=sb_down[4])
    rs_down = reduce_start(sb_down, dh2, "down")
    per = shard_ff // _blk(shard_ff, 1024)
    dw_up = _matmul(h2, du, name="dw_up", ta=True, tn=_blk(shard_ff, 1024), out_shape=(N_DEV, d, shard_ff),
                    out_map=lambda tm, tn: pl.BlockSpec((None, tm, tn), lambda i, j, kk: (j // per, i, j % per)),
                    after=rs_down[4])
    sb_up = _sib_start(dw_up, "sib_start_up")
    dmix, dx2, dsh_m, dsc_m, dg2, dgt_a, dg1 = _mid_bwd(
        dh2, dout, x2, mix, g_pre_mlp + tok(sb_up), sc_m, gt_a, g_post_mix)
    d_attn = _matmul(dmix, w_out_full, name="d_attn", tb=True)
    rs_up = reduce_start(sb_up, d_attn, "up")
    dw_out = _matmul(attn, dmix, name="dw_out", ta=True, after=rs_up[4])
    sb_out = _sib_start(dw_out.reshape(N_DEV, d // N_DEV, d), "sib_start_out")
    dqf, dkf, dvf, dcs = _fox_bwd(proj_a, d_attn, cum, cumt, fox_lse)
    dsq, dsk, dsv, dsinks = _swa_bwd(proj_b, d_attn, swa_lse, sinks128 + tok(sb_out), cos, sin)
    rs_out = reduce_start(sb_out, dsq, "out")
    dfg, db_forget = _fg_bwd(dcs, fg, b128 + tok(rs_out))
    dproj = jnp.concatenate([dqf, dkf, dvf, dsq, dsk, dsv, _pad_lanes(dfg, FG_PAD)], axis=1)
    dw_in_r = _matmul(h1, dproj, name="dw_in", ta=True)

    def shard_cols(j):
        lo, hi = j * in_w, (j + 1) * in_w
        parts = []
        for seg_lo, seg_hi, shift in ((0, o_fg, 0), (o_fg, o_sq, n_swa3), (o_sq, in_total, -N_FOX)):
            a, b = max(lo, seg_lo), min(hi, seg_hi)
            if a < b:
                parts.append(dw_in_r[:, a + shift:b + shift])
        return parts[0] if len(parts) == 1 else jnp.concatenate(parts, axis=1)

    sb_in = _sib_start(jnp.stack([shard_cols(j) for j in range(N_DEV)]), "sib_start_in")
    dh1 = _matmul(dproj, w_in_r, name="d_h1", tb=True, out_dtypes=(F32,), after=sb_in[4])
    grad_x, dsh_a, dsc_a, dg0 = _x_bwd(dh1, dx2, x, g_pre_mix, sc_a)

    small = jnp.concatenate([dsh_a, dsc_a, dgt_a, dsh_m, dsc_m, dgt_m, dg0, dg1, dg2, dg3, db_forget, dsinks,
                             loss_vec[:, 0:LANES]], axis=1)
    small_all = _all_gather([small], "gather_small")[0].reshape(N_DEV, small.shape[1])
    rs_in = reduce_start(sb_in, small_all, "in")

    pack = lambda bm, g0_, g1_, g2_, g3_, bf_, sk_: jnp.concatenate(
        [bm, g0_, g1_, g2_, g3_, _pad_lanes(bf_), _pad_lanes(sk_), jnp.zeros((1, LANES), F32)], axis=1)
    p_small = pack(b_mod, g_pre_mix, g_post_mix, g_pre_mlp, g_post_mlp, b_forget, swa_sinks)
    m_small = pack(m_b_mod, m_g_pre_mix, m_g_post_mix, m_g_pre_mlp, m_g_post_mlp, m_b_forget, m_swa_sinks)
    v_small = pack(v_b_mod, v_g_pre_mix, v_g_post_mix, v_g_pre_mlp, v_g_post_mlp, v_b_forget, v_swa_sinks)
    small_out = _adam_small(small_all, p_small + tok(rs_in), m_small, v_small)

    n_mod = 6 * d

    def unpack(vec):
        o = n_mod
        return (vec[:, 0:n_mod], vec[:, o:o + d], vec[:, o + d:o + 2 * d], vec[:, o + 2 * d:o + 3 * d],
                vec[:, o + 3 * d:o + 4 * d], vec[:, o + 4 * d:o + 4 * d + N_FOX],
                vec[:, o + 4 * d + LANES:o + 4 * d + LANES + N_SWA])

    loss = small_out[0][0, n_mod + 4 * d + 2 * LANES]
    g_small, d_small, nm_small, nv_small = [unpack(vec) for vec in small_out]

    dmod_cols = lax.dynamic_slice(small_all, (0, me * mod_w), (N_DEV, mod_w))
    g_w_mod, d_w_mod, nm_w_mod, nv_w_mod = _adam_mod(c_all + tok(rs_in), dmod_cols, w_mod, m_w_mod[0], v_w_mod[0])

    g_w_down, d_w_down, nm_w_down, nv_w_down = reduce_finish(rs_down, d_w_mod, w_down, m_w_down, v_w_down, "w_down")
    g_w_up, d_w_up, nm_w_up, nv_w_up = reduce_finish(rs_up, d_w_down, w_up, m_w_up, v_w_up, "w_up")
    g_w_out, d_w_out, nm_w_out, nv_w_out = reduce_finish(rs_out, d_w_up, w_out, m_w_out, v_w_out, "w_out")
    g_w_in, d_w_in, nm_w_in, nv_w_in = reduce_finish(rs_in, d_w_out, w_in, m_w_in, v_w_in, "w_in")

    def assemble(w_mod_, small_, w_in_, w_out_, w_up_, w_down_):
        b_mod_, g0_, g1_, g2_, g3_, bf_, sk_ = small_
        return [w_mod_[None], b_mod_, g0_, g1_, w_in_[None], bf_, sk_, w_out_[None], g2_, g3_, w_up_[None], w_down_[None]]

    outs = [loss, grad_x[None]]
    outs += assemble(g_w_mod, g_small, g_w_in, g_w_out, g_w_up, g_w_down)
    outs += assemble(d_w_mod, d_small, d_w_in, d_w_out, d_w_up, d_w_down)
    outs += assemble(nm_w_mod, nm_small, nm_w_in, nm_w_out, nm_w_up, nm_w_down)
    outs += assemble(nv_w_mod, nv_small, nv_w_in, nv_w_out, nv_w_up, nv_w_down)
    return tuple(outs)
```

```python
import functools

import jax
import jax.numpy as jnp
from jax import lax
from jax.experimental import pallas as pl
from jax.experimental.pallas import tpu as pltpu

F32 = jnp.float32
BF16 = jnp.bfloat16
MESH = pl.DeviceIdType.MESH

N_DEV = 8
N_CHIP = 4
LANES = 128
HEAD_DIM = 128
N_FOX = 8
N_SWA = 8
N_KV = 2
GQA = N_SWA // N_KV
WINDOW = 128
FOX_W = N_FOX * HEAD_DIM
SWA_W = N_SWA * HEAD_DIM
KV_W = N_KV * HEAD_DIM
ROPE_THETA = 10000.0
NORM_EPS = 1e-6
ATT_SCALE = HEAD_DIM ** -0.5
FG_PAD = 512

ADAM_LR = 0.001
ADAM_B1 = 0.9
ADAM_B2 = 0.999
ADAM_EPS = 1e-08
ADAM_WD = 0.01
ADAM_STEP = 10

VMEM_LIMIT = 56 * 1024 * 1024

NT_DIMS = (((1,), (1,)), ((), ()))
TN_DIMS = (((0,), (0,)), ((), ()))
NN_DIMS = (((1,), (0,)), ((), ()))


def _pcall(body, *, name, out_shape, grid=(), in_specs=None, out_specs=None, scratch_shapes=(), grid_spec=None):
    params = pltpu.CompilerParams(vmem_limit_bytes=VMEM_LIMIT)
    if grid_spec is not None:
        return pl.pallas_call(body, name=name, out_shape=out_shape, grid_spec=grid_spec, compiler_params=params)
    return pl.pallas_call(body, name=name, out_shape=out_shape, grid=grid, in_specs=in_specs, out_specs=out_specs,
                          scratch_shapes=scratch_shapes, compiler_params=params)


def _blk(n, pref):
    if n <= pref:
        return n
    b = (pref // LANES) * LANES
    while n % b:
        b -= LANES
    return b


def _position():
    return lax.axis_index("x"), lax.axis_index("y"), lax.axis_index("c")


ANY = pl.BlockSpec(memory_space=pl.ANY)


def _all_gather(arrs, name):
    n = len(arrs)

    def body(*refs):
        ins, outs = refs[:n], refs[n:2 * n]
        send_sems, recv_sems, local_sems = refs[2 * n:]
        x, y, c = _position()
        me, sibling = (x, y, c), (x, y, 1 - c)
        chips = [(1 - x, y), (x, 1 - y), (1 - x, 1 - y)]

        def slot(p):
            return 4 * p[0] + 2 * p[1] + p[2]

        def copy(a, k, block, to, src=None):
            dst = outs[a].at[slot(block)]
            return pltpu.make_async_remote_copy(
                src_ref=dst if src is None else src, dst_ref=dst,
                send_sem=send_sems.at[7 * a + k], recv_sem=recv_sems.at[7 * a + k],
                device_id=to, device_id_type=MESH)

        mine = [pltpu.make_async_copy(ins[a], outs[a].at[slot(me)], local_sems.at[a]) for a in range(n)]
        for cp in mine:
            cp.start()
        first = []
        for a in range(n):
            first.append(copy(a, 0, me, sibling, src=ins[a]))
            first += [copy(a, 1 + j, me, (*chip, c), src=ins[a]) for j, chip in enumerate(chips)]
        for cp in first:
            cp.start()
        passed = []
        for a in range(n):
            for j, chip in enumerate(chips):
                copy(a, 1 + j, (*chip, c), me).wait_recv()
                cp = copy(a, 4 + j, (*chip, c), sibling)
                cp.start()
                passed.append(cp)
        for a in range(n):
            copy(a, 0, sibling, me).wait_recv()
            for j, chip in enumerate(chips):
                copy(a, 4 + j, (*chip, 1 - c), me).wait_recv()
        for cp in first + passed:
            cp.wait_send()
        for cp in mine:
            cp.wait()

    return _pcall(
        body, name=name,
        out_shape=[jax.ShapeDtypeStruct((N_DEV,) + a.shape, a.dtype) for a in arrs],
        in_specs=[ANY] * n, out_specs=[ANY] * n,
        scratch_shapes=[pltpu.SemaphoreType.DMA((7 * n,)), pltpu.SemaphoreType.DMA((7 * n,)),
                        pltpu.SemaphoreType.DMA((n,))],
    )(*arrs)


HBM = pl.BlockSpec(memory_space=pltpu.HBM)
SEM = pl.BlockSpec(memory_space=pltpu.SEMAPHORE)
EFFECT = pltpu.SideEffectType.DATAFLOW_SIDE_EFFECTING


def _hbm(a):
    return pltpu.with_memory_space_constraint(a, pltpu.HBM)


def _gather_peers():
    x, y, c = _position()
    return [(x, y, 1 - c), (1 - x, y, c), (x, 1 - y, c), (1 - x, 1 - y, c)]


def _ag_start(shards):
    n = len(shards)
    lands = [_hbm(lax.empty((N_DEV,) + s.shape, s.dtype)) for s in shards]

    def body(*refs):
        srcs, land, send, recv = refs[:n], refs[n:2 * n], refs[2 * n:3 * n], refs[3 * n:4 * n]
        token = refs[6 * n]
        x, y, c = _position()
        for a in range(n):
            for k, to in enumerate(_gather_peers()):
                pltpu.make_async_remote_copy(
                    src_ref=srcs[a], dst_ref=land[a].at[4 * x + 2 * y + c], send_sem=send[a].at[k],
                    recv_sem=recv[a].at[k], device_id=to, device_id_type=MESH).start()
        token[...] = jnp.zeros_like(token)

    sems = [pltpu.SemaphoreType.DMA((4,))] * (2 * n)
    out = pl.pallas_call(
        body, name="ag_start",
        out_shape=sems + [pltpu.HBM(s.shape, s.dtype) for s in shards] + [pltpu.HBM(l.shape, l.dtype) for l in lands]
        + [jax.ShapeDtypeStruct((8, LANES), F32)],
        in_specs=[HBM] * (2 * n), out_specs=[SEM] * (2 * n) + [HBM] * (2 * n) + [pl.BlockSpec(memory_space=pltpu.VMEM)],
        input_output_aliases={**{a: 2 * n + a for a in range(n)}, **{n + a: 3 * n + a for a in range(n)}},
        compiler_params=pltpu.CompilerParams(has_side_effects=EFFECT),
    )(*[_hbm(s) for s in shards], *lands)
    return out[:n], out[n:2 * n], out[2 * n:3 * n], out[3 * n:4 * n], out[4 * n]


def _ag_wait(send, recv, shard_thru, land_thru, after, name):
    def body(v_ref, land_ref, send_sem, recv_sem, after_ref, v_dead, got_ref):
        for k, to in enumerate(_gather_peers()):
            cp = pltpu.make_async_remote_copy(
                src_ref=v_ref, dst_ref=land_ref.at[0], send_sem=send_sem.at[k], recv_sem=recv_sem.at[k],
                device_id=to, device_id_type=MESH)
            cp.wait_send()
            cp.wait_recv()

    return pl.pallas_call(
        body, name=name,
        out_shape=(pltpu.HBM(shard_thru.shape, shard_thru.dtype), pltpu.HBM(land_thru.shape, land_thru.dtype)),
        in_specs=(HBM, HBM, SEM, SEM, ANY), out_specs=(HBM, HBM), input_output_aliases={0: 0, 1: 1},
        compiler_params=pltpu.CompilerParams(has_side_effects=EFFECT),
    )(shard_thru, land_thru, send, recv, after)


def _ag_forward(land, name):
    def body(land_in, land_ref, send_sems, recv_sems):
        x, y, c = _position()
        copies = []
        for j, (px, py) in enumerate([(1 - x, y), (x, 1 - y), (1 - x, 1 - y)]):
            block = land_ref.at[4 * px + 2 * py + c]
            cp = pltpu.make_async_remote_copy(src_ref=block, dst_ref=block, send_sem=send_sems.at[j],
                                              recv_sem=recv_sems.at[j], device_id=(x, y, 1 - c), device_id_type=MESH)
            cp.start()
            copies.append(cp)
        for cp in copies:
            cp.wait()

    return pl.pallas_call(
        body, name=name, out_shape=jax.ShapeDtypeStruct(land.shape, land.dtype),
        in_specs=[ANY], out_specs=ANY, input_output_aliases={0: 0},
        scratch_shapes=[pltpu.SemaphoreType.DMA((3,)), pltpu.SemaphoreType.DMA((3,))],
    )(land)


def _rs_peers():
    x, y, c = _position()
    return [(1 - x, y, c), (x, 1 - y, c), (1 - x, 1 - y, c)]


def _rs_start(chip_sums, name):
    land = _hbm(lax.empty(chip_sums.shape, chip_sums.dtype))

    def body(src, land_ref, send, recv, src_thru, land_thru, token):
        x, y, c = _position()
        for j, (px, py, pc) in enumerate(_rs_peers()):
            pltpu.make_async_remote_copy(
                src_ref=src.at[2 * px + py], dst_ref=land_ref.at[2 * x + y], send_sem=send.at[j], recv_sem=recv.at[j],
                device_id=(px, py, pc), device_id_type=MESH).start()
        token[...] = jnp.zeros_like(token)

    return pl.pallas_call(
        body, name=name,
        out_shape=[pltpu.SemaphoreType.DMA((3,)), pltpu.SemaphoreType.DMA((3,)),
                   pltpu.HBM(chip_sums.shape, chip_sums.dtype), pltpu.HBM(land.shape, land.dtype),
                   jax.ShapeDtypeStruct((8, LANES), F32)],
        in_specs=[HBM, HBM], out_specs=[SEM, SEM, HBM, HBM, pl.BlockSpec(memory_space=pltpu.VMEM)],
        input_output_aliases={0: 2, 1: 3},
        compiler_params=pltpu.CompilerParams(has_side_effects=EFFECT),
    )(_hbm(chip_sums), land)


def _rs_wait(send, recv, src_thru, land_thru, after, name):
    def body(src, land_ref, send_sem, recv_sem, after_ref, src_out, land_out):
        for j, to in enumerate(_rs_peers()):
            cp = pltpu.make_async_remote_copy(
                src_ref=src.at[0], dst_ref=land_ref.at[0], send_sem=send_sem.at[j], recv_sem=recv_sem.at[j],
                device_id=to, device_id_type=MESH)
            cp.wait_send()
            cp.wait_recv()

    return pl.pallas_call(
        body, name=name,
        out_shape=(pltpu.HBM(src_thru.shape, src_thru.dtype), pltpu.HBM(land_thru.shape, land_thru.dtype)),
        in_specs=(HBM, HBM, SEM, SEM, ANY), out_specs=(HBM, HBM), input_output_aliases={0: 0, 1: 1},
        compiler_params=pltpu.CompilerParams(has_side_effects=EFFECT),
    )(src_thru, land_thru, send, recv, after)


def _sib_start(full, name):
    land = _hbm(lax.empty((N_CHIP,) + full.shape[1:], full.dtype))

    def body(src, land_ref, send, recv, src_thru, land_thru, token):
        x, y, c = _position()
        for k in range(N_CHIP):
            pltpu.make_async_remote_copy(
                src_ref=src.at[2 * k + (1 - c)], dst_ref=land_ref.at[k], send_sem=send.at[k], recv_sem=recv.at[k],
                device_id=(x, y, 1 - c), device_id_type=MESH).start()
        token[...] = jnp.zeros_like(token)

    return pl.pallas_call(
        body, name=name,
        out_shape=[pltpu.SemaphoreType.DMA((N_CHIP,)), pltpu.SemaphoreType.DMA((N_CHIP,)),
                   pltpu.HBM(full.shape, full.dtype), pltpu.HBM(land.shape, land.dtype),
                   jax.ShapeDtypeStruct((8, LANES), F32)],
        in_specs=[HBM, HBM], out_specs=[SEM, SEM, HBM, HBM, pl.BlockSpec(memory_space=pltpu.VMEM)],
        input_output_aliases={0: 2, 1: 3},
        compiler_params=pltpu.CompilerParams(has_side_effects=EFFECT),
    )(_hbm(full), land)


def _sib_wait(send, recv, src_thru, land_thru, after, name):
    def body(src, land_ref, send_sem, recv_sem, after_ref, src_out, land_out):
        x, y, c = _position()
        for k in range(N_CHIP):
            cp = pltpu.make_async_remote_copy(
                src_ref=src.at[0], dst_ref=land_ref.at[0], send_sem=send_sem.at[k], recv_sem=recv_sem.at[k],
                device_id=(x, y, 1 - c), device_id_type=MESH)
            cp.wait_send()
            cp.wait_recv()

    return pl.pallas_call(
        body, name=name,
        out_shape=(pltpu.HBM(src_thru.shape, src_thru.dtype), pltpu.HBM(land_thru.shape, land_thru.dtype)),
        in_specs=(HBM, HBM, SEM, SEM, ANY), out_specs=(HBM, HBM), input_output_aliases={0: 0, 1: 1},
        compiler_params=pltpu.CompilerParams(has_side_effects=EFFECT),
    )(src_thru, land_thru, send, recv, after)


def _rs_sibling(arrs, name):
    n = len(arrs)

    def body(*refs):
        ins, outs = refs[:n], refs[n:2 * n]
        send_sems, recv_sems = refs[2 * n:]
        x, y, c = _position()
        copies = []
        for a in range(n):
            for k in range(N_CHIP):
                cp = pltpu.make_async_remote_copy(
                    src_ref=ins[a].at[2 * k + (1 - c)], dst_ref=outs[a].at[k],
                    send_sem=send_sems.at[N_CHIP * a + k], recv_sem=recv_sems.at[N_CHIP * a + k],
                    device_id=(x, y, 1 - c), device_id_type=MESH)
                cp.start()
                copies.append(cp)
        for cp in copies:
            cp.wait()

    return _pcall(
        body, name=name,
        out_shape=[jax.ShapeDtypeStruct((N_CHIP,) + a.shape[1:], a.dtype) for a in arrs],
        in_specs=[ANY] * n, out_specs=[ANY] * n,
        scratch_shapes=[pltpu.SemaphoreType.DMA((N_CHIP * n,)), pltpu.SemaphoreType.DMA((N_CHIP * n,))],
    )(*arrs)


def _chip_sum(full, recv, core, name):
    _, rows, cols = full.shape
    tr = _blk(rows, 1024) if rows % LANES == 0 else rows

    def body(core_ref, a_ref, b_ref, o_ref):
        o_ref[...] = (a_ref[...].astype(F32) + b_ref[...].astype(F32)).astype(o_ref.dtype)

    grid_spec = pltpu.PrefetchScalarGridSpec(
        num_scalar_prefetch=1, grid=(N_CHIP, rows // tr),
        in_specs=[pl.BlockSpec((None, tr, cols), lambda k, i, core_ref: (2 * k + core_ref[0], i, 0)),
                  pl.BlockSpec((None, tr, cols), lambda k, i, core_ref: (k, i, 0))],
        out_specs=pl.BlockSpec((None, tr, cols), lambda k, i, core_ref: (k, i, 0)))
    return _pcall(body, name=name, out_shape=jax.ShapeDtypeStruct((N_CHIP, rows, cols), full.dtype),
                  grid_spec=grid_spec)(core, full, recv)


def _matmul(a, b, *, name, ta=False, tb=False, tm=1024, tn=1024, tk=2048, out_dtypes=(BF16,), epilogue=None,
            row_extras=(), tile_extras=(), out_shape=None, out_map=None, b_sharded=False, n_cols=None, n_off=0,
            after=None):
    m, k = (a.shape[1], a.shape[0]) if ta else a.shape
    if b_sharded:
        shard_c = b.shape[2]
        n, kb = (b.shape[1], N_DEV * shard_c) if tb else (N_DEV * shard_c, b.shape[1])
        tn, tk = (tn, min(tk, shard_c)) if tb else (min(tn, shard_c), tk)
    else:
        n, kb = b.shape if tb else (b.shape[1], b.shape[0])
    assert kb == k, (name, kb, k)
    if n_cols is not None:
        n = n_cols
    tm, tn, tk = _blk(m, tm), _blk(n, tn), _blk(k, tk)
    assert n_off % tn == 0
    nk = k // tk
    dims = (((0 if ta else 1,), (1 if tb else 0,)), ((), ()))
    behind = () if after is None else (after,)
    n_row, n_tile, n_out = len(row_extras), len(tile_extras), len(out_dtypes)
    first_out = 2 + n_row + n_tile + len(behind)

    def body(*refs):
        a_ref, b_ref = refs[:2]
        extras = refs[2:2 + n_row + n_tile]
        outs = refs[first_out:first_out + n_out]
        acc_ref = refs[-1]
        jj, kk = pl.program_id(1), pl.program_id(2)
        part = lax.dot_general(a_ref[...].astype(BF16), b_ref[...].astype(BF16), dims, preferred_element_type=F32)

        def finish(acc):
            res = (acc,) if epilogue is None else epilogue(acc, jj, *[e[...] for e in extras])
            for o_ref, r in zip(outs, res):
                o_ref[...] = r.astype(o_ref.dtype)

        if nk == 1:
            finish(part)
        else:
            @pl.when(kk == 0)
            def _():
                acc_ref[...] = part

            @pl.when(kk > 0)
            def _():
                acc_ref[...] += part

            @pl.when(kk == nk - 1)
            def _():
                finish(acc_ref[...])

    a_spec = pl.BlockSpec((tk, tm), lambda i, j, kk: (kk, i)) if ta else pl.BlockSpec((tm, tk), lambda i, j, kk: (i, kk))
    if b_sharded and tb:
        per = shard_c // tk
        b_spec = pl.BlockSpec((None, tn, tk), lambda i, j, kk: (kk // per, j, kk % per))
    elif b_sharded:
        per = shard_c // tn
        b_spec = pl.BlockSpec((None, tk, tn), lambda i, j, kk: (j // per, kk, j % per))
    elif tb:
        b_spec = pl.BlockSpec((tn, tk), lambda i, j, kk: (j + n_off // tn, kk))
    else:
        b_spec = pl.BlockSpec((tk, tn), lambda i, j, kk: (kk, j + n_off // tn))
    in_specs = [a_spec, b_spec]
    in_specs += [pl.BlockSpec((tm, LANES), lambda i, j, kk: (i, 0)) for _ in row_extras]
    in_specs += [pl.BlockSpec((tm, tn), lambda i, j, kk: (i, j)) for _ in tile_extras]
    in_specs += [ANY for _ in behind]
    if out_map is None:
        out_specs = [pl.BlockSpec((tm, tn), lambda i, j, kk: (i, j)) for _ in out_dtypes]
        shapes = [jax.ShapeDtypeStruct((m, n), dt) for dt in out_dtypes]
    else:
        out_specs = [out_map(tm, tn)]
        shapes = [jax.ShapeDtypeStruct(out_shape, out_dtypes[0])]
    acc_shape = (tm, tn) if nk > 1 else (8, LANES)
    res = _pcall(body, name=name, out_shape=shapes, grid=(m // tm, n // tn, nk), in_specs=in_specs,
                 out_specs=out_specs, scratch_shapes=[pltpu.VMEM(acc_shape, F32)])(
                     a, b, *row_extras, *tile_extras, *behind)
    return res[0] if n_out == 1 else res


def _rope_cols(acc, j, cos, sin, n_rope):
    width = acc.shape[1]
    parts = []
    for g in range(width // HEAD_DIM):
        xg = acc[:, g * HEAD_DIM:(g + 1) * HEAD_DIM]
        roped = xg * cos + pltpu.roll(xg, HEAD_DIM // 2, 1) * sin
        parts.append(jnp.where(j * width + g * HEAD_DIM < n_rope, roped, xg))
    return jnp.concatenate(parts, axis=1) if len(parts) > 1 else parts[0]


def _silu(v):
    return v / (1.0 + jnp.exp(-v))


def _mod_part(c_all, w_mod, b_part):
    d, w = w_mod.shape
    tk = _blk(d, 512)

    def body(c_ref, w_ref, b_ref, o_ref):
        kk = pl.program_id(0)
        cond = _silu(c_ref[...]).astype(BF16)
        part = jnp.dot(cond, w_ref[...].astype(BF16), preferred_element_type=F32)

        @pl.when(kk == 0)
        def _():
            o_ref[...] = part + b_ref[...]

        @pl.when(kk > 0)
        def _():
            o_ref[...] += part

    return _pcall(body, name="mod_part", out_shape=jax.ShapeDtypeStruct((N_DEV, w), F32), grid=(d // tk,),
                  in_specs=[pl.BlockSpec((N_DEV, tk), lambda kk: (0, kk)), pl.BlockSpec((tk, w), lambda kk: (kk, 0)),
                            pl.BlockSpec((1, w), lambda kk: (0, 0))],
                  out_specs=pl.BlockSpec((N_DEV, w), lambda kk: (0, 0)))(c_all, w_mod, b_part)


def _row_call(body, name, t, d, tiled_in, vec_in, tiled_out_dtypes, n_vec_out, tr=256):
    tr = _blk(t, tr)
    tile = pl.BlockSpec((tr, d), lambda i: (i, 0))
    vec = pl.BlockSpec((1, d), lambda i: (0, 0))
    out_shape = [jax.ShapeDtypeStruct((t, d), dt) for dt in tiled_out_dtypes]
    out_shape += [jax.ShapeDtypeStruct((1, d), F32)] * n_vec_out
    return _pcall(body, name=name, out_shape=out_shape, grid=(t // tr,),
                  in_specs=[tile] * len(tiled_in) + [vec] * len(vec_in),
                  out_specs=[tile] * len(tiled_out_dtypes) + [vec] * n_vec_out)(*tiled_in, *vec_in)


def _accumulate(ref, val):
    @pl.when(pl.program_id(0) == 0)
    def _():
        ref[...] = val

    @pl.when(pl.program_id(0) > 0)
    def _():
        ref[...] += val


def _rsum(v):
    return jnp.sum(v, axis=0, keepdims=True)


def _rms(v):
    return lax.rsqrt(jnp.mean(v * v, axis=-1, keepdims=True) + NORM_EPS)


def _rms_bwd(vhat, r, dvhat):
    return r * (dvhat - vhat * jnp.mean(dvhat * vhat, axis=-1, keepdims=True))


def _pre_attn(x, g0, sc_a, sh_a):
    def body(x_ref, g_ref, sc_ref, sh_ref, h_ref):
        xv = x_ref[...]
        h_ref[...] = (xv * _rms(xv) * g_ref[...] * (1.0 + sc_ref[...]) + sh_ref[...]).astype(BF16)

    t, d = x.shape
    return _row_call(body, "pre_attn", t, d, [x], [g0, sc_a, sh_a], [BF16], 0)[0]


def _post_mix(x, mix, gt_a, g1, g2, sc_m, sh_m):
    def body(x_ref, mix_ref, gt_ref, g1_ref, g2_ref, sc_ref, sh_ref, x2_ref, h2_ref):
        mv = mix_ref[...]
        x2 = x_ref[...] + gt_ref[...] * (mv * _rms(mv) * g1_ref[...])
        x2_ref[...] = x2
        h2_ref[...] = (x2 * _rms(x2) * g2_ref[...] * (1.0 + sc_ref[...]) + sh_ref[...]).astype(BF16)

    t, d = x.shape
    return _row_call(body, "post_mix", t, d, [x, mix], [gt_a, g1, g2, sc_m, sh_m], [F32, BF16], 0)


def _final(y, x2, target, gt_m, g3):
    t, d = y.shape

    def body(y_ref, x2_ref, tg_ref, gt_ref, g3_ref, dy_ref, dout_ref, dgt_ref, dg3_ref, loss_ref):
        yv = y_ref[...]
        r = _rms(yv)
        yhat = yv * r
        n3 = yhat * g3_ref[...]
        err = x2_ref[...] + gt_ref[...] * n3 - tg_ref[...]
        _accumulate(loss_ref, jnp.zeros((1, d), F32) + 0.5 * jnp.sum(err * err) / d)
        dout = err * (1.0 / d)
        dout_ref[...] = dout
        _accumulate(dgt_ref, _rsum(dout * n3))
        dn3 = dout * gt_ref[...]
        _accumulate(dg3_ref, _rsum(dn3 * yhat))
        dy_ref[...] = _rms_bwd(yhat, r, dn3 * g3_ref[...]).astype(BF16)

    return _row_call(body, "final", t, d, [y, x2, target], [gt_m, g3], [BF16, F32], 3)


def _mid_bwd(dh2, dout, x2, mix, g2, sc_m, gt_a, g1):
    t, d = x2.shape

    def body(dh2_ref, dout_ref, x2_ref, mix_ref, g2_ref, sc_ref, gt_ref, g1_ref,
             dmix_ref, dx2_ref, dsh_ref, dsc_ref, dg2_ref, dgt_ref, dg1_ref):
        dh2v = dh2_ref[...]
        x2v = x2_ref[...]
        r2 = _rms(x2v)
        x2hat = x2v * r2
        _accumulate(dsh_ref, _rsum(dh2v))
        _accumulate(dsc_ref, _rsum(dh2v * (x2hat * g2_ref[...])))
        dn2 = dh2v * (1.0 + sc_ref[...])
        _accumulate(dg2_ref, _rsum(dn2 * x2hat))
        dx2 = dout_ref[...] + _rms_bwd(x2hat, r2, dn2 * g2_ref[...])
        dx2_ref[...] = dx2
        mv = mix_ref[...]
        r1 = _rms(mv)
        mhat = mv * r1
        _accumulate(dgt_ref, _rsum(dx2 * (mhat * g1_ref[...])))
        dn1 = dx2 * gt_ref[...]
        _accumulate(dg1_ref, _rsum(dn1 * mhat))
        dmix_ref[...] = _rms_bwd(mhat, r1, dn1 * g1_ref[...]).astype(BF16)

    return _row_call(body, "mid_bwd", t, d, [dh2, dout, x2, mix], [g2, sc_m, gt_a, g1], [BF16, F32], 5)


def _x_bwd(dh1, dx2, x, g0, sc_a):
    t, d = x.shape

    def body(dh1_ref, dx2_ref, x_ref, g0_ref, sc_ref, dx_ref, dsh_ref, dsc_ref, dg0_ref):
        dh1v = dh1_ref[...]
        xv = x_ref[...]
        r0 = _rms(xv)
        xhat = xv * r0
        _accumulate(dsh_ref, _rsum(dh1v))
        _accumulate(dsc_ref, _rsum(dh1v * (xhat * g0_ref[...])))
        dn0 = dh1v * (1.0 + sc_ref[...])
        _accumulate(dg0_ref, _rsum(dn0 * xhat))
        dx_ref[...] = dx2_ref[...] + _rms_bwd(xhat, r0, dn0 * g0_ref[...])

    return _row_call(body, "x_bwd", t, d, [dh1, dx2, x], [g0, sc_a], [F32], 3)


def _pick_lane(block, h):
    lane = lax.broadcasted_iota(jnp.int32, block.shape, 1)
    return jnp.sum(jnp.where(lane == h, block, 0.0), axis=1, keepdims=True)


def _put_lane(ref, rows, h, col):
    old = ref[rows, :]
    lane = lax.broadcasted_iota(jnp.int32, old.shape, 1)
    ref[rows, :] = jnp.where(lane == h, col, old)


def _tri(n, lower):
    r = lax.broadcasted_iota(jnp.int32, (n, n), 0)
    c = lax.broadcasted_iota(jnp.int32, (n, n), 1)
    return jnp.where((c <= r) if lower else (c >= r), 1.0, 0.0).astype(F32)


def _cum_fwd(fg, b128):
    t = fg.shape[0]
    nb = t // LANES

    def body(fg_ref, b_ref, cum_ref, cumt_ref):
        tri = _tri(LANES, True)
        carry = jnp.zeros((1, LANES), F32)
        for i in range(nb):
            z = fg_ref[i * LANES:(i + 1) * LANES, :] + b_ref[...]
            lf = jnp.minimum(z, 0.0) - jnp.log(1.0 + jnp.exp(-jnp.abs(z)))
            blk = jnp.dot(tri, lf, precision=lax.Precision.HIGHEST, preferred_element_type=F32) + carry
            cum_ref[i * LANES:(i + 1) * LANES, :] = blk
            carry = blk[LANES - 1:LANES, :]
        cumt_ref[...] = cum_ref[...].T[0:N_FOX, :]

    return _pcall(body, name="cum_fwd",
                  out_shape=[jax.ShapeDtypeStruct((t, LANES), F32), jax.ShapeDtypeStruct((N_FOX, t), F32)],
                  grid=(1,),
                  in_specs=[pl.BlockSpec((t, LANES), lambda i: (0, 0)), pl.BlockSpec((1, LANES), lambda i: (0, 0))],
                  out_specs=[pl.BlockSpec((t, LANES), lambda i: (0, 0)), pl.BlockSpec((N_FOX, t), lambda i: (0, 0))],
                  )(fg, b128)


def _fg_bwd(dcs_rows, fg, b128):
    t = fg.shape[0]
    nb = t // LANES

    def body(dcs_ref, fg_ref, b_ref, dfg_ref, db_ref, dcum_ref):
        dcum_ref[...] = -jnp.concatenate([dcs_ref[...], jnp.zeros((LANES - N_FOX, t), F32)], axis=0).T
        tri = _tri(LANES, False)
        carry = jnp.zeros((1, LANES), F32)
        db = jnp.zeros((1, LANES), F32)
        for i in reversed(range(nb)):
            rows = slice(i * LANES, (i + 1) * LANES)
            dlf = jnp.dot(tri, dcum_ref[rows, :], precision=lax.Precision.HIGHEST, preferred_element_type=F32) + carry
            carry = dlf[0:1, :]
            z = fg_ref[rows, :] + b_ref[...]
            dfg = dlf / (1.0 + jnp.exp(z))
            dfg_ref[rows, :] = dfg.astype(BF16)
            db = db + _rsum(dfg)
        db_ref[...] = db

    full = pl.BlockSpec((t, LANES), lambda i: (0, 0))
    vec = pl.BlockSpec((1, LANES), lambda i: (0, 0))
    return _pcall(body, name="fg_bwd",
                  out_shape=[jax.ShapeDtypeStruct((t, LANES), BF16), jax.ShapeDtypeStruct((1, LANES), F32)],
                  grid=(1,), in_specs=[pl.BlockSpec((N_FOX, t), lambda i: (0, 0)), full, vec], out_specs=[full, vec],
                  scratch_shapes=[pltpu.VMEM((t, LANES), F32)])(dcs_rows, fg, b128)


def _head_spec(t, col0, div=1):
    return pl.BlockSpec((t, HEAD_DIM), lambda h: (0, col0 + h // div))


def _fox_scores(q, k, cq, ck, i, tq, end):
    s = lax.dot_general(q, k, NT_DIMS, preferred_element_type=F32) * ATT_SCALE + cq - ck
    row = lax.broadcasted_iota(jnp.int32, (tq, end), 0) + i * tq
    col = lax.broadcasted_iota(jnp.int32, (tq, end), 1)
    return jnp.where(row >= col, s, -jnp.inf)


def _fox_fwd(proj_a, cum, cumt):
    t = proj_a.shape[0]
    tq = _blk(t, 512)
    nq = t // tq

    def body(q_ref, k_ref, v_ref, cum_ref, cumt_ref, o_ref, lse_ref):
        h = pl.program_id(0)
        cq_all = _pick_lane(cum_ref[...], h)
        ck_all = cumt_ref[pl.ds(h, 1), :]

        @pl.when(h == 0)
        def _():
            lse_ref[...] = jnp.zeros_like(lse_ref)

        for i in range(nq):
            rows, end = slice(i * tq, (i + 1) * tq), (i + 1) * tq
            s = _fox_scores(q_ref[rows, :], k_ref[0:end, :], cq_all[rows, :], ck_all[:, 0:end], i, tq, end)
            m = jnp.max(s, axis=1, keepdims=True)
            p = jnp.exp(s - m)
            l = jnp.sum(p, axis=1, keepdims=True)
            o = jnp.dot(p.astype(BF16), v_ref[0:end, :], preferred_element_type=F32) / l
            o_ref[rows, :] = o.astype(BF16)
            _put_lane(lse_ref, rows, h, m + jnp.log(l))

    nh = FOX_W // HEAD_DIM
    stat = pl.BlockSpec((t, LANES), lambda h: (0, 0))
    return _pcall(body, name="fox_fwd",
                  out_shape=[jax.ShapeDtypeStruct((t, FOX_W), BF16), jax.ShapeDtypeStruct((t, LANES), F32)],
                  grid=(N_FOX,),
                  in_specs=[_head_spec(t, 0), _head_spec(t, nh), _head_spec(t, 2 * nh), stat,
                            pl.BlockSpec((N_FOX, t), lambda h: (0, 0))],
                  out_specs=[_head_spec(t, 0), stat])(proj_a, proj_a, proj_a, cum, cumt)


def _fox_bwd(proj_a, d_attn, cum, cumt, lse):
    t = proj_a.shape[0]
    tq = _blk(t, 512)
    nq = t // tq

    def body(q_ref, k_ref, v_ref, do_ref, cum_ref, cumt_ref, lse_ref,
             dq_ref, dk_ref, dv_ref, dcs_ref, dk_acc, dv_acc, dcs_acc):
        h = pl.program_id(0)
        cq_all = _pick_lane(cum_ref[...], h)
        ck_all = cumt_ref[pl.ds(h, 1), :]
        lse_all = _pick_lane(lse_ref[...], h)
        dk_acc[...] = jnp.zeros_like(dk_acc)
        dv_acc[...] = jnp.zeros_like(dv_acc)
        dcs_acc[...] = jnp.zeros_like(dcs_acc)
        for i in range(nq):
            rows, end = slice(i * tq, (i + 1) * tq), (i + 1) * tq
            q, k, v, do = q_ref[rows, :], k_ref[0:end, :], v_ref[0:end, :], do_ref[rows, :]
            s = _fox_scores(q, k, cq_all[rows, :], ck_all[:, 0:end], i, tq, end)
            p = jnp.exp(s - lse_all[rows, :])
            dp = lax.dot_general(do, v, NT_DIMS, preferred_element_type=F32)
            ds = p * (dp - jnp.sum(p * dp, axis=1, keepdims=True))
            dcs_acc[:, 0:end] += jnp.sum(ds, axis=0, keepdims=True)
            ds = ds.astype(BF16)
            dq_ref[rows, :] = (jnp.dot(ds, k, preferred_element_type=F32) * ATT_SCALE).astype(BF16)
            dk_acc[0:end, :] += lax.dot_general(ds, q, TN_DIMS, preferred_element_type=F32)
            dv_acc[0:end, :] += lax.dot_general(p.astype(BF16), do, TN_DIMS, preferred_element_type=F32)
        dk_ref[...] = (dk_acc[...] * ATT_SCALE).astype(BF16)
        dv_ref[...] = dv_acc[...].astype(BF16)
        dcs_ref[pl.ds(h, 1), :] = dcs_acc[...]

    nh = FOX_W // HEAD_DIM
    stat = pl.BlockSpec((t, LANES), lambda h: (0, 0))
    rows8 = pl.BlockSpec((N_FOX, t), lambda h: (0, 0))
    head = _head_spec(t, 0)
    wide = jax.ShapeDtypeStruct((t, FOX_W), BF16)
    return _pcall(body, name="fox_bwd",
                  out_shape=[wide, wide, wide, jax.ShapeDtypeStruct((N_FOX, t), F32)],
                  grid=(N_FOX,),
                  in_specs=[_head_spec(t, 0), _head_spec(t, nh), _head_spec(t, 2 * nh), head, stat, rows8, stat],
                  out_specs=[head, head, head, rows8],
                  scratch_shapes=[pltpu.VMEM((t, HEAD_DIM), F32), pltpu.VMEM((t, HEAD_DIM), F32),
                                  pltpu.VMEM((1, t), F32)],
                  )(proj_a, proj_a, proj_a, d_attn, cum, cumt, lse)


def _swa_scores(q, k, i, tq, start, end):
    s = lax.dot_general(q, k, NT_DIMS, preferred_element_type=F32) * ATT_SCALE
    row = lax.broadcasted_iota(jnp.int32, (tq, end - start), 0) + i * tq
    col = lax.broadcasted_iota(jnp.int32, (tq, end - start), 1) + start
    diff = row - col
    return jnp.where((diff >= 0) & (diff < WINDOW), s, -jnp.inf)


def _swa_blocks(t):
    tq = _blk(t, 256)
    return tq, [(i, max(0, i * tq - WINDOW), (i + 1) * tq) for i in range(t // tq)]


def _swa_fwd(proj_b, sinks128):
    t = proj_b.shape[0]
    tq, blocks = _swa_blocks(t)

    def body(q_ref, k_ref, v_ref, sink_ref, o_ref, lse_ref):
        h = pl.program_id(0)
        sink = _pick_lane(sink_ref[...], h)

        @pl.when(h == 0)
        def _():
            lse_ref[...] = jnp.zeros_like(lse_ref)

        for i, start, end in blocks:
            rows = slice(i * tq, end)
            s = _swa_scores(q_ref[rows, :], k_ref[start:end, :], i, tq, start, end)
            m = jnp.maximum(jnp.max(s, axis=1, keepdims=True), sink)
            p = jnp.exp(s - m)
            l = jnp.sum(p, axis=1, keepdims=True) + jnp.exp(sink - m)
            o = jnp.dot(p.astype(BF16), v_ref[start:end, :], preferred_element_type=F32) / l
            o_ref[rows, :] = o.astype(BF16)
            _put_lane(lse_ref, rows, h, m + jnp.log(l))

    stat = pl.BlockSpec((t, LANES), lambda h: (0, 0))
    return _pcall(body, name="swa_fwd",
                  out_shape=[jax.ShapeDtypeStruct((t, SWA_W), BF16), jax.ShapeDtypeStruct((t, LANES), F32)],
                  grid=(N_SWA,),
                  in_specs=[_head_spec(t, 0), _head_spec(t, N_SWA, GQA), _head_spec(t, N_SWA + N_KV, GQA),
                            pl.BlockSpec((1, LANES), lambda h: (0, 0))],
                  out_specs=[_head_spec(t, 0), stat])(proj_b, proj_b, proj_b, sinks128)


def _rope_bwd(d, cos, sin):
    return d * cos + pltpu.roll(d * sin, HEAD_DIM // 2, 1)


def _swa_bwd(proj_b, d_attn, lse, sinks128, cos, sin):
    t = proj_b.shape[0]
    tq, blocks = _swa_blocks(t)

    def body(q_ref, k_ref, v_ref, do_ref, lse_ref, sink_ref, cos_ref, sin_ref,
             dq_ref, dk_ref, dv_ref, dsink_ref, dk_acc, dv_acc):
        h = pl.program_id(0)
        sink = _pick_lane(sink_ref[...], h)
        lse_all = _pick_lane(lse_ref[...], h)

        @pl.when(h == 0)
        def _():
            dsink_ref[...] = jnp.zeros_like(dsink_ref)

        @pl.when(h % GQA == 0)
        def _():
            dk_acc[...] = jnp.zeros_like(dk_acc)
            dv_acc[...] = jnp.zeros_like(dv_acc)

        dsink = jnp.zeros((1, 1), F32)
        for i, start, end in blocks:
            rows = slice(i * tq, end)
            q, k, v, do = q_ref[rows, :], k_ref[start:end, :], v_ref[start:end, :], do_ref[rows, :]
            s = _swa_scores(q, k, i, tq, start, end)
            p = jnp.exp(s - lse_all[rows, :])
            dp = lax.dot_general(do, v, NT_DIMS, preferred_element_type=F32)
            delta = jnp.sum(p * dp, axis=1, keepdims=True)
            ds = (p * (dp - delta)).astype(BF16)
            dq = jnp.dot(ds, k, preferred_element_type=F32) * ATT_SCALE
            dq_ref[rows, :] = _rope_bwd(dq, cos_ref[rows, :], sin_ref[rows, :]).astype(BF16)
            dk_acc[start:end, :] += lax.dot_general(ds, q, TN_DIMS, preferred_element_type=F32)
            dv_acc[start:end, :] += lax.dot_general(p.astype(BF16), do, TN_DIMS, preferred_element_type=F32)
            dsink = dsink - jnp.sum(jnp.exp(sink - lse_all[rows, :]) * delta, axis=0, keepdims=True)
        old = dsink_ref[...]
        lane = lax.broadcasted_iota(jnp.int32, old.shape, 1)
        dsink_ref[...] = jnp.where(lane == h, dsink, old)

        @pl.when(h % GQA == GQA - 1)
        def _():
            dk_ref[...] = _rope_bwd(dk_acc[...] * ATT_SCALE, cos_ref[...], sin_ref[...]).astype(BF16)
            dv_ref[...] = dv_acc[...].astype(BF16)

    stat = pl.BlockSpec((t, LANES), lambda h: (0, 0))
    vec = pl.BlockSpec((1, LANES), lambda h: (0, 0))
    head = _head_spec(t, 0)
    kv_out = _head_spec(t, 0, GQA)
    return _pcall(body, name="swa_bwd",
                  out_shape=[jax.ShapeDtypeStruct((t, SWA_W), BF16), jax.ShapeDtypeStruct((t, KV_W), BF16),
                             jax.ShapeDtypeStruct((t, KV_W), BF16), jax.ShapeDtypeStruct((1, LANES), F32)],
                  grid=(N_SWA,),
                  in_specs=[head, _head_spec(t, N_SWA, GQA), _head_spec(t, N_SWA + N_KV, GQA),
                            _head_spec(t, N_FOX), stat, vec, stat, stat],
                  out_specs=[head, kv_out, kv_out, vec],
                  scratch_shapes=[pltpu.VMEM((t, HEAD_DIM), F32), pltpu.VMEM((t, HEAD_DIM), F32)],
                  )(proj_b, proj_b, proj_b, d_attn, lse, sinks128, cos, sin)


def _adamw(w, g, m, v):
    m = ADAM_B1 * m + (1.0 - ADAM_B1) * g
    v = ADAM_B2 * v + (1.0 - ADAM_B2) * (g * g)
    m_hat = m / (1.0 - ADAM_B1 ** ADAM_STEP)
    v_hat = v / (1.0 - ADAM_B2 ** ADAM_STEP)
    delta = -ADAM_LR * (m_hat / (jnp.sqrt(v_hat) + ADAM_EPS) + ADAM_WD * w)
    return delta, m, v


def _adam_pieces(w, m, v, own, land, idx, name):
    rows, cols = w.shape
    tr, tc = (256, cols) if rows % 256 == 0 else (rows, _blk(cols, 512))

    def body(idx_ref, own_ref, l1_ref, l2_ref, l3_ref, w_ref, m_ref, v_ref, g_ref, d_ref, mo_ref, vo_ref):
        g = own_ref[...].astype(F32) + l1_ref[...].astype(F32) + l2_ref[...].astype(F32) + l3_ref[...].astype(F32)
        g_ref[...] = g
        d_ref[...], mo_ref[...], vo_ref[...] = _adamw(w_ref[...], g, m_ref[...], v_ref[...])

    def piece(p):
        return pl.BlockSpec((None, tr, tc), lambda i, j, idx_ref: (idx_ref[p], i, j))

    tile = pl.BlockSpec((tr, tc), lambda i, j, idx_ref: (i, j))
    out = jax.ShapeDtypeStruct((rows, cols), F32)
    grid_spec = pltpu.PrefetchScalarGridSpec(
        num_scalar_prefetch=1, grid=(rows // tr, cols // tc),
        in_specs=[piece(0), piece(1), piece(2), piece(3), tile, tile, tile], out_specs=[tile] * 4)
    return _pcall(body, name=name, out_shape=[out] * 4, grid_spec=grid_spec)(idx, own, land, land, land, w, m, v)


def _adam_mod(c_all, dmod_cols, w, m, v):
    rows, cols = w.shape
    tr = _blk(rows, 256)

    def body(c_ref, dm_ref, w_ref, m_ref, v_ref, g_ref, d_ref, mo_ref, vo_ref):
        cond = _silu(c_ref[...]).astype(BF16)
        g = lax.dot_general(cond, dm_ref[...].astype(BF16), TN_DIMS, preferred_element_type=F32)
        g_ref[...] = g
        d_ref[...], mo_ref[...], vo_ref[...] = _adamw(w_ref[...], g, m_ref[...], v_ref[...])

    tile = pl.BlockSpec((tr, cols), lambda i: (i, 0))
    out = jax.ShapeDtypeStruct((rows, cols), F32)
    return _pcall(body, name="adam_mod", out_shape=[out] * 4, grid=(rows // tr,),
                  in_specs=[pl.BlockSpec((N_DEV, tr), lambda i: (0, i)), pl.BlockSpec((N_DEV, cols), lambda i: (0, 0)),
                            tile, tile, tile],
                  out_specs=[tile] * 4)(c_all, dmod_cols, w, m, v)


def _adam_small(parts, w, m, v):
    nv = w.shape[1]

    def body(p_ref, w_ref, m_ref, v_ref, g_ref, d_ref, mo_ref, vo_ref):
        g = p_ref[0:1, :]
        for k in range(1, N_DEV):
            g = g + p_ref[k:k + 1, :]
        g_ref[...] = g
        d_ref[...], mo_ref[...], vo_ref[...] = _adamw(w_ref[...], g, m_ref[...], v_ref[...])

    vec = pl.BlockSpec((1, nv), lambda i: (0, 0))
    out = jax.ShapeDtypeStruct((1, nv), F32)
    return _pcall(body, name="adam_small", out_shape=[out] * 4, grid=(1,),
                  in_specs=[pl.BlockSpec((N_DEV, nv), lambda i: (0, 0)), vec, vec, vec],
                  out_specs=[vec] * 4)(parts, w, m, v)


def _pad_lanes(v, width=LANES):
    return jnp.pad(v, ((0, 0), (0, width - v.shape[1])))


def kernel(x, c, w_mod, b_mod, g_pre_mix, g_post_mix, w_in, b_forget, swa_sinks, w_out, g_pre_mlp, g_post_mlp, w_up, w_down, loss_target, m_w_mod, m_b_mod, m_g_pre_mix, m_g_post_mix, m_w_in, m_b_forget, m_swa_sinks, m_w_out, m_g_pre_mlp, m_g_post_mlp, m_w_up, m_w_down, v_w_mod, v_b_mod, v_g_pre_mix, v_g_post_mix, v_w_in, v_b_forget, v_swa_sinks, v_w_out, v_g_pre_mlp, v_g_post_mlp, v_w_up, v_w_down):
    ax, ay, ac = _position()
    me = 4 * ax + 2 * ay + ac
    x, target = x[0], loss_target[0]
    t, d = x.shape
    w_mod, w_in, w_out, w_up, w_down = w_mod[0], w_in[0], w_out[0], w_up[0], w_down[0]
    w_in = jnp.transpose(w_in)
    mod_w = w_mod.shape[1]
    in_w = w_in.shape[0]
    in_total = N_DEV * in_w
    shard_ff = w_up.shape[1]
    n_fox3 = 3 * FOX_W
    n_swa3 = SWA_W + 2 * KV_W
    assert in_total == n_fox3 + N_FOX + n_swa3 and d == FOX_W + SWA_W

    c_all = _all_gather([c], "gather_c")[0].reshape(N_DEV, d)
    b_part = lax.dynamic_slice(b_mod, (0, me * mod_w), (1, mod_w))
    mod_parts = _all_gather([_mod_part(c_all, w_mod, b_part)], "gather_mod")[0]
    mod = lax.dynamic_index_in_dim(mod_parts, me, axis=1, keepdims=False).reshape(1, N_DEV * mod_w)

    shards = [w_in.astype(BF16), w_out.astype(BF16), w_up.astype(BF16), w_down.astype(BF16)]
    shards, mod = lax.optimization_barrier((shards, mod))
    ag_send, ag_recv, ag_shard, ag_land, ag_token = _ag_start(shards)

    def gathered(i, after, name):
        shard, land = _ag_wait(ag_send[i], ag_recv[i], ag_shard[i], ag_land[i], after, "ag_wait_" + name)
        return lax.dynamic_update_slice(_ag_forward(land, "ag_fwd_" + name), shard[None], (me, 0, 0))

    sh_a, sc_a, gt_a, sh_m, sc_m, gt_m = [mod[:, i * d:(i + 1) * d] for i in range(6)]

    half = HEAD_DIM // 2
    inv_freq = 1.0 / (ROPE_THETA ** (jnp.arange(half, dtype=F32) * (2.0 / HEAD_DIM)))
    ang = jnp.arange(t).astype(F32)[:, None] * inv_freq[None, :]
    cos = jnp.concatenate([jnp.cos(ang), jnp.cos(ang)], axis=1)
    sin = jnp.concatenate([-jnp.sin(ang), jnp.sin(ang)], axis=1)

    b128 = _pad_lanes(b_forget)
    sinks128 = _pad_lanes(swa_sinks)

    h1 = _pre_attn(x, g_pre_mix + ag_token[0:1, 0:1], sc_a, sh_a)
    w_in_g = gathered(0, h1, "in")
    o_fg, o_sq = n_fox3, n_fox3 + N_FOX

    def rows(lo, hi):
        parts = []
        for j in range(lo // in_w, (hi - 1) // in_w + 1):
            parts.append(w_in_g[j, max(lo - j * in_w, 0):min(hi - j * in_w, in_w), :])
        return parts

    w_in_r = jnp.concatenate(rows(0, o_fg) + rows(o_sq, in_total) + rows(o_fg, o_sq)
                             + [jnp.zeros((FG_PAD - N_FOX, d), BF16)], axis=0)
    proj_a = _matmul(h1, w_in_r, name="proj_a", tb=True, n_cols=n_fox3, n_off=0)
    proj_b = _matmul(h1, w_in_r, name="proj_b", tb=True, n_cols=n_swa3, n_off=n_fox3, tn=512, row_extras=(cos, sin),
                     epilogue=lambda acc, j, cs, sn: (_rope_cols(acc, j, cs, sn, SWA_W + KV_W),))
    fg = _matmul(h1, w_in_r, name="proj_fg", tb=True, n_cols=FG_PAD, n_off=n_fox3 + n_swa3, tn=FG_PAD,
                 out_dtypes=(F32,))[:, 0:LANES]
    cum, cumt = _cum_fwd(fg, b128)
    fox_o, fox_lse = _fox_fwd(proj_a, cum, cumt)
    swa_o, swa_lse = _swa_fwd(proj_b, sinks128)
    w_out_full = gathered(1, swa_o, "out").reshape(d, d)
    attn = jnp.concatenate([fox_o, swa_o], axis=1)
    mix = _matmul(attn, w_out_full, name="out_proj", out_dtypes=(F32,))
    x2, h2 = _post_mix(x, mix, gt_a, g_post_mix, g_pre_mlp, sc_m, sh_m)
    w_up_g = gathered(2, h2, "up")
    u, act = _matmul(h2, w_up_g, name="mlp_up", b_sharded=True, out_dtypes=(BF16, BF16),
                     epilogue=lambda acc, j: (acc, jnp.square(jnp.maximum(acc, 0.0))))
    w_down_full = gathered(3, act, "down").reshape(N_DEV * shard_ff, d)
    y = _matmul(act, w_down_full, name="mlp_down", out_dtypes=(F32,))

    core = jnp.reshape(ac, (1,)).astype(jnp.int32)

    def reduce_start(started, after, name):
        send, recv, src, land, _ = started
        full, from_sibling = _sib_wait(send, recv, src, land, after, "sib_wait_" + name)
        return _rs_start(_chip_sum(full, from_sibling, core, "chip_sum_" + name), "rs_start_" + name)

    def tok(started):
        return started[4][0:1, 0:1]

    idx = jnp.stack([2 * ax + ay, 2 * (1 - ax) + ay, 2 * ax + (1 - ay), 2 * (1 - ax) + (1 - ay)]).astype(jnp.int32)

    def reduce_finish(started, after, w, m, v, name):
        send, recv, src, land, _ = started
        own, landed = _rs_wait(send, recv, src, land, after, "rs_wait_" + name)
        return _adam_pieces(w, m[0], v[0], own, landed, idx, "adam_" + name)

    dy, dout, dgt_m, dg3, loss_vec = _final(y, x2, target, gt_m, g_post_mlp)
    du = _matmul(dy, w_down_full, name="d_act", tb=True, tile_extras=(u,),
                 epilogue=lambda acc, j, uu: (acc * (2.0 * jnp.maximum(uu.astype(F32), 0.0)),))
    dw_down = _matmul(act, dy, name="dw_down", ta=True)
    sb_down = _sib_start(dw_down.reshape(N_DEV, shard_ff, d), "sib_start_down")
    dh2 = _matmul(du, w_up_g, name="d_h2", tb=True, b_sharded=True, out_dtypes=(F32,), after=sb_down[4])
    rs_down = reduce_start(sb_down, dh2, "down")
    per = shard_ff // _blk(shard_ff, 1024)
    dw_up = _matmul(h2, du, name="dw_up", ta=True, tn=_blk(shard_ff, 1024), out_shape=(N_DEV, d, shard_ff),
                    out_map=lambda tm, tn: pl.BlockSpec((None, tm, tn), lambda i, j, kk: (j // per, i, j % per)),
                    after=rs_down[4])
    sb_up = _sib_start(dw_up, "sib_start_up")
    dmix, dx2, dsh_m, dsc_m, dg2, dgt_a, dg1 = _mid_bwd(
        dh2, dout, x2, mix, g_pre_mlp + tok(sb_up), sc_m, gt_a, g_post_mix)
    d_attn = _matmul(dmix, w_out_full, name="d_attn", tb=True)
    rs_up = reduce_start(sb_up, d_attn, "up")
    dw_out = _matmul(attn, dmix, name="dw_out", ta=True, after=rs_up[4])
    sb_out = _sib_start(dw_out.reshape(N_DEV, d // N_DEV, d), "sib_start_out")
    dqf, dkf, dvf, dcs = _fox_bwd(proj_a, d_attn, cum, cumt, fox_lse)
    dsq, dsk, dsv, dsinks = _swa_bwd(proj_b, d_attn, swa_lse, sinks128 + tok(sb_out), cos, sin)
    rs_out = reduce_start(sb_out, dsq, "out")
    dfg, db_forget = _fg_bwd(dcs, fg, b128 + tok(rs_out))
    dproj = jnp.concatenate([dqf, dkf, dvf, dsq, dsk, dsv, _pad_lanes(dfg, FG_PAD)], axis=1)
    dw_in_r = _matmul(dproj, h1, name="dw_in", ta=True)

    def shard_rows(j):
        lo, hi = j * in_w, (j + 1) * in_w
        parts = []
        for seg_lo, seg_hi, shift in ((0, o_fg, 0), (o_fg, o_sq, n_swa3), (o_sq, in_total, -N_FOX)):
            a, b = max(lo, seg_lo), min(hi, seg_hi)
            if a < b:
                parts.append(dw_in_r[a + shift:b + shift, :])
        return parts[0] if len(parts) == 1 else jnp.concatenate(parts, axis=0)

    sb_in = _sib_start(jnp.stack([shard_rows(j) for j in range(N_DEV)]), "sib_start_in")
    dh1 = _matmul(dproj, w_in_r, name="d_h1", out_dtypes=(F32,), after=sb_in[4])
    grad_x, dsh_a, dsc_a, dg0 = _x_bwd(dh1, dx2, x, g_pre_mix, sc_a)

    small = jnp.concatenate([dsh_a, dsc_a, dgt_a, dsh_m, dsc_m, dgt_m, dg0, dg1, dg2, dg3, db_forget, dsinks,
                             loss_vec[:, 0:LANES]], axis=1)
    small_all = _all_gather([small], "gather_small")[0].reshape(N_DEV, small.shape[1])
    rs_in = reduce_start(sb_in, small_all, "in")

    pack = lambda bm, g0_, g1_, g2_, g3_, bf_, sk_: jnp.concatenate(
        [bm, g0_, g1_, g2_, g3_, _pad_lanes(bf_), _pad_lanes(sk_), jnp.zeros((1, LANES), F32)], axis=1)
    p_small = pack(b_mod, g_pre_mix, g_post_mix, g_pre_mlp, g_post_mlp, b_forget, swa_sinks)
    m_small = pack(m_b_mod, m_g_pre_mix, m_g_post_mix, m_g_pre_mlp, m_g_post_mlp, m_b_forget, m_swa_sinks)
    v_small = pack(v_b_mod, v_g_pre_mix, v_g_post_mix, v_g_pre_mlp, v_g_post_mlp, v_b_forget, v_swa_sinks)
    small_out = _adam_small(small_all, p_small + tok(rs_in), m_small, v_small)

    n_mod = 6 * d

    def unpack(vec):
        o = n_mod
        return (vec[:, 0:n_mod], vec[:, o:o + d], vec[:, o + d:o + 2 * d], vec[:, o + 2 * d:o + 3 * d],
                vec[:, o + 3 * d:o + 4 * d], vec[:, o + 4 * d:o + 4 * d + N_FOX],
                vec[:, o + 4 * d + LANES:o + 4 * d + LANES + N_SWA])

    loss = small_out[0][0, n_mod + 4 * d + 2 * LANES]
    g_small, d_small, nm_small, nv_small = [unpack(vec) for vec in small_out]

    dmod_cols = lax.dynamic_slice(small_all, (0, me * mod_w), (N_DEV, mod_w))
    g_w_mod, d_w_mod, nm_w_mod, nv_w_mod = _adam_mod(c_all + tok(rs_in), dmod_cols, w_mod, m_w_mod[0], v_w_mod[0])

    g_w_down, d_w_down, nm_w_down, nv_w_down = reduce_finish(rs_down, d_w_mod, w_down, m_w_down, v_w_down, "w_down")
    g_w_up, d_w_up, nm_w_up, nv_w_up = reduce_finish(rs_up, d_w_down, w_up, m_w_up, v_w_up, "w_up")
    g_w_out, d_w_out, nm_w_out, nv_w_out = reduce_finish(rs_out, d_w_up, w_out, m_w_out, v_w_out, "w_out")
    g_w_in, d_w_in, nm_w_in, nv_w_in = reduce_finish(rs_in, d_w_out, w_in, (jnp.transpose(m_w_in[0]),),
                                                     (jnp.transpose(v_w_in[0]),), "w_in")

    def assemble(w_mod_, small_, w_in_, w_out_, w_up_, w_down_):
        b_mod_, g0_, g1_, g2_, g3_, bf_, sk_ = small_
        return [w_mod_[None], b_mod_, g0_, g1_, jnp.transpose(w_in_)[None], bf_, sk_, w_out_[None], g2_, g3_,
                w_up_[None], w_down_[None]]

    outs = [loss, grad_x[None]]
    outs += assemble(g_w_mod, g_small, g_w_in, g_w_out, g_w_up, g_w_down)
    outs += assemble(d_w_mod, d_small, d_w_in, d_w_out, d_w_up, d_w_down)
    outs += assemble(nm_w_mod, nm_small, nm_w_in, nm_w_out, nm_w_up, nm_w_down)
    outs += assemble(nv_w_mod, nv_small, nv_w_in, nv_w_out, nv_w_up, nv_w_down)
    return tuple(outs)
```

```python
import functools

import jax
import jax.numpy as jnp
from jax import lax
from jax.experimental import pallas as pl
from jax.experimental.pallas import tpu as pltpu

F32 = jnp.float32
BF16 = jnp.bfloat16
MESH = pl.DeviceIdType.MESH

N_DEV = 8
N_CHIP = 4
LANES = 128
HEAD_DIM = 128
N_FOX = 8
N_SWA = 8
N_KV = 2
GQA = N_SWA // N_KV
WINDOW = 128
FOX_W = N_FOX * HEAD_DIM
SWA_W = N_SWA * HEAD_DIM
KV_W = N_KV * HEAD_DIM
ROPE_THETA = 10000.0
NORM_EPS = 1e-6
ATT_SCALE = HEAD_DIM ** -0.5
FG_PAD = 512

ADAM_LR = 0.001
ADAM_B1 = 0.9
ADAM_B2 = 0.999
ADAM_EPS = 1e-08
ADAM_WD = 0.01
ADAM_STEP = 10

VMEM_LIMIT = 56 * 1024 * 1024

NT_DIMS = (((1,), (1,)), ((), ()))
TN_DIMS = (((0,), (0,)), ((), ()))
NN_DIMS = (((1,), (0,)), ((), ()))


def _pcall(body, *, name, out_shape, grid=(), in_specs=None, out_specs=None, scratch_shapes=(), grid_spec=None):
    params = pltpu.CompilerParams(vmem_limit_bytes=VMEM_LIMIT)
    if grid_spec is not None:
        return pl.pallas_call(body, name=name, out_shape=out_shape, grid_spec=grid_spec, compiler_params=params)
    return pl.pallas_call(body, name=name, out_shape=out_shape, grid=grid, in_specs=in_specs, out_specs=out_specs,
                          scratch_shapes=scratch_shapes, compiler_params=params)


def _blk(n, pref):
    if n <= pref:
        return n
    b = (pref // LANES) * LANES
    while n % b:
        b -= LANES
    return b


def _position():
    return lax.axis_index("x"), lax.axis_index("y"), lax.axis_index("c")


ANY = pl.BlockSpec(memory_space=pl.ANY)


def _all_gather(arrs, name):
    n = len(arrs)

    def body(*refs):
        ins, outs = refs[:n], refs[n:2 * n]
        send_sems, recv_sems, local_sems = refs[2 * n:]
        x, y, c = _position()
        me, sibling = (x, y, c), (x, y, 1 - c)
        chips = [(1 - x, y), (x, 1 - y), (1 - x, 1 - y)]

        def slot(p):
            return 4 * p[0] + 2 * p[1] + p[2]

        def copy(a, k, block, to, src=None):
            dst = outs[a].at[slot(block)]
            return pltpu.make_async_remote_copy(
                src_ref=dst if src is None else src, dst_ref=dst,
                send_sem=send_sems.at[7 * a + k], recv_sem=recv_sems.at[7 * a + k],
                device_id=to, device_id_type=MESH)

        mine = [pltpu.make_async_copy(ins[a], outs[a].at[slot(me)], local_sems.at[a]) for a in range(n)]
        for cp in mine:
            cp.start()
        first = []
        for a in range(n):
            first.append(copy(a, 0, me, sibling, src=ins[a]))
            first += [copy(a, 1 + j, me, (*chip, c), src=ins[a]) for j, chip in enumerate(chips)]
        for cp in first:
            cp.start()
        passed = []
        for a in range(n):
            for j, chip in enumerate(chips):
                copy(a, 1 + j, (*chip, c), me).wait_recv()
                cp = copy(a, 4 + j, (*chip, c), sibling)
                cp.start()
                passed.append(cp)
        for a in range(n):
            copy(a, 0, sibling, me).wait_recv()
            for j, chip in enumerate(chips):
                copy(a, 4 + j, (*chip, 1 - c), me).wait_recv()
        for cp in first + passed:
            cp.wait_send()
        for cp in mine:
            cp.wait()

    return _pcall(
        body, name=name,
        out_shape=[jax.ShapeDtypeStruct((N_DEV,) + a.shape, a.dtype) for a in arrs],
        in_specs=[ANY] * n, out_specs=[ANY] * n,
        scratch_shapes=[pltpu.SemaphoreType.DMA((7 * n,)), pltpu.SemaphoreType.DMA((7 * n,)),
                        pltpu.SemaphoreType.DMA((n,))],
    )(*arrs)


HBM = pl.BlockSpec(memory_space=pltpu.HBM)
SEM = pl.BlockSpec(memory_space=pltpu.SEMAPHORE)
EFFECT = pltpu.SideEffectType.DATAFLOW_SIDE_EFFECTING


def _hbm(a):
    return pltpu.with_memory_space_constraint(a, pltpu.HBM)


def _gather_peers():
    x, y, c = _position()
    return [(x, y, 1 - c), (1 - x, y, c), (x, 1 - y, c), (1 - x, 1 - y, c)]


def _ag_start(shards, name):
    n = len(shards)
    lands = [_hbm(lax.empty((N_DEV,) + s.shape, s.dtype)) for s in shards]

    def body(*refs):
        srcs, land, send, recv = refs[:n], refs[n:2 * n], refs[2 * n:3 * n], refs[3 * n:4 * n]
        token = refs[6 * n]
        x, y, c = _position()
        for a in range(n):
            for k, to in enumerate(_gather_peers()):
                pltpu.make_async_remote_copy(
                    src_ref=srcs[a], dst_ref=land[a].at[4 * x + 2 * y + c], send_sem=send[a].at[k],
                    recv_sem=recv[a].at[k], device_id=to, device_id_type=MESH).start()
        token[...] = jnp.zeros_like(token)

    sems = [pltpu.SemaphoreType.DMA((4,))] * (2 * n)
    out = pl.pallas_call(
        body, name=name,
        out_shape=sems + [pltpu.HBM(s.shape, s.dtype) for s in shards] + [pltpu.HBM(l.shape, l.dtype) for l in lands]
        + [jax.ShapeDtypeStruct((8, LANES), F32)],
        in_specs=[HBM] * (2 * n), out_specs=[SEM] * (2 * n) + [HBM] * (2 * n) + [pl.BlockSpec(memory_space=pltpu.VMEM)],
        input_output_aliases={**{a: 2 * n + a for a in range(n)}, **{n + a: 3 * n + a for a in range(n)}},
        compiler_params=pltpu.CompilerParams(has_side_effects=EFFECT),
    )(*[_hbm(s) for s in shards], *lands)
    return out[:n], out[n:2 * n], out[2 * n:3 * n], out[3 * n:4 * n], out[4 * n]


def _ag_wait(send, recv, shard_thru, land_thru, after, name):
    def body(v_ref, land_ref, send_sem, recv_sem, after_ref, v_dead, got_ref):
        for k, to in enumerate(_gather_peers()):
            cp = pltpu.make_async_remote_copy(
                src_ref=v_ref, dst_ref=land_ref.at[0], send_sem=send_sem.at[k], recv_sem=recv_sem.at[k],
                device_id=to, device_id_type=MESH)
            cp.wait_send()
            cp.wait_recv()

    return pl.pallas_call(
        body, name=name,
        out_shape=(pltpu.HBM(shard_thru.shape, shard_thru.dtype), pltpu.HBM(land_thru.shape, land_thru.dtype)),
        in_specs=(HBM, HBM, SEM, SEM, ANY), out_specs=(HBM, HBM), input_output_aliases={0: 0, 1: 1},
        compiler_params=pltpu.CompilerParams(has_side_effects=EFFECT),
    )(shard_thru, land_thru, send, recv, after)


def _ag_forward(land, name):
    def body(land_in, land_ref, send_sems, recv_sems):
        x, y, c = _position()
        copies = []
        for j, (px, py) in enumerate([(1 - x, y), (x, 1 - y), (1 - x, 1 - y)]):
            block = land_ref.at[4 * px + 2 * py + c]
            cp = pltpu.make_async_remote_copy(src_ref=block, dst_ref=block, send_sem=send_sems.at[j],
                                              recv_sem=recv_sems.at[j], device_id=(x, y, 1 - c), device_id_type=MESH)
            cp.start()
            copies.append(cp)
        for cp in copies:
            cp.wait()

    return pl.pallas_call(
        body, name=name, out_shape=jax.ShapeDtypeStruct(land.shape, land.dtype),
        in_specs=[ANY], out_specs=ANY, input_output_aliases={0: 0},
        scratch_shapes=[pltpu.SemaphoreType.DMA((3,)), pltpu.SemaphoreType.DMA((3,))],
    )(land)


def _rs_peers():
    x, y, c = _position()
    return [(1 - x, y, c), (x, 1 - y, c), (1 - x, 1 - y, c)]


def _rs_start(chip_sums, name):
    land = _hbm(lax.empty(chip_sums.shape, chip_sums.dtype))

    def body(src, land_ref, send, recv, src_thru, land_thru, token):
        x, y, c = _position()
        for j, (px, py, pc) in enumerate(_rs_peers()):
            pltpu.make_async_remote_copy(
                src_ref=src.at[2 * px + py], dst_ref=land_ref.at[2 * x + y], send_sem=send.at[j], recv_sem=recv.at[j],
                device_id=(px, py, pc), device_id_type=MESH).start()
        token[...] = jnp.zeros_like(token)

    return pl.pallas_call(
        body, name=name,
        out_shape=[pltpu.SemaphoreType.DMA((3,)), pltpu.SemaphoreType.DMA((3,)),
                   pltpu.HBM(chip_sums.shape, chip_sums.dtype), pltpu.HBM(land.shape, land.dtype),
                   jax.ShapeDtypeStruct((8, LANES), F32)],
        in_specs=[HBM, HBM], out_specs=[SEM, SEM, HBM, HBM, pl.BlockSpec(memory_space=pltpu.VMEM)],
        input_output_aliases={0: 2, 1: 3},
        compiler_params=pltpu.CompilerParams(has_side_effects=EFFECT),
    )(_hbm(chip_sums), land)


def _rs_wait(send, recv, src_thru, land_thru, after, name):
    def body(src, land_ref, send_sem, recv_sem, after_ref, src_out, land_out):
        for j, to in enumerate(_rs_peers()):
            cp = pltpu.make_async_remote_copy(
                src_ref=src.at[0], dst_ref=land_ref.at[0], send_sem=send_sem.at[j], recv_sem=recv_sem.at[j],
                device_id=to, device_id_type=MESH)
            cp.wait_send()
            cp.wait_recv()

    return pl.pallas_call(
        body, name=name,
        out_shape=(pltpu.HBM(src_thru.shape, src_thru.dtype), pltpu.HBM(land_thru.shape, land_thru.dtype)),
        in_specs=(HBM, HBM, SEM, SEM, ANY), out_specs=(HBM, HBM), input_output_aliases={0: 0, 1: 1},
        compiler_params=pltpu.CompilerParams(has_side_effects=EFFECT),
    )(src_thru, land_thru, send, recv, after)


def _sib_start(full, name):
    land = _hbm(lax.empty((N_CHIP,) + full.shape[1:], full.dtype))

    def body(src, land_ref, send, recv, src_thru, land_thru, token):
        x, y, c = _position()
        for k in range(N_CHIP):
            pltpu.make_async_remote_copy(
                src_ref=src.at[2 * k + (1 - c)], dst_ref=land_ref.at[k], send_sem=send.at[k], recv_sem=recv.at[k],
                device_id=(x, y, 1 - c), device_id_type=MESH).start()
        token[...] = jnp.zeros_like(token)

    return pl.pallas_call(
        body, name=name,
        out_shape=[pltpu.SemaphoreType.DMA((N_CHIP,)), pltpu.SemaphoreType.DMA((N_CHIP,)),
                   pltpu.HBM(full.shape, full.dtype), pltpu.HBM(land.shape, land.dtype),
                   jax.ShapeDtypeStruct((8, LANES), F32)],
        in_specs=[HBM, HBM], out_specs=[SEM, SEM, HBM, HBM, pl.BlockSpec(memory_space=pltpu.VMEM)],
        input_output_aliases={0: 2, 1: 3},
        compiler_params=pltpu.CompilerParams(has_side_effects=EFFECT),
    )(_hbm(full), land)


def _sib_wait(send, recv, src_thru, land_thru, after, name):
    def body(src, land_ref, send_sem, recv_sem, after_ref, src_out, land_out):
        x, y, c = _position()
        for k in range(N_CHIP):
            cp = pltpu.make_async_remote_copy(
                src_ref=src.at[0], dst_ref=land_ref.at[0], send_sem=send_sem.at[k], recv_sem=recv_sem.at[k],
                device_id=(x, y, 1 - c), device_id_type=MESH)
            cp.wait_send()
            cp.wait_recv()

    return pl.pallas_call(
        body, name=name,
        out_shape=(pltpu.HBM(src_thru.shape, src_thru.dtype), pltpu.HBM(land_thru.shape, land_thru.dtype)),
        in_specs=(HBM, HBM, SEM, SEM, ANY), out_specs=(HBM, HBM), input_output_aliases={0: 0, 1: 1},
        compiler_params=pltpu.CompilerParams(has_side_effects=EFFECT),
    )(src_thru, land_thru, send, recv, after)


def _rs_sibling(arrs, name):
    n = len(arrs)

    def body(*refs):
        ins, outs = refs[:n], refs[n:2 * n]
        send_sems, recv_sems = refs[2 * n:]
        x, y, c = _position()
        copies = []
        for a in range(n):
            for k in range(N_CHIP):
                cp = pltpu.make_async_remote_copy(
                    src_ref=ins[a].at[2 * k + (1 - c)], dst_ref=outs[a].at[k],
                    send_sem=send_sems.at[N_CHIP * a + k], recv_sem=recv_sems.at[N_CHIP * a + k],
                    device_id=(x, y, 1 - c), device_id_type=MESH)
                cp.start()
                copies.append(cp)
        for cp in copies:
            cp.wait()

    return _pcall(
        body, name=name,
        out_shape=[jax.ShapeDtypeStruct((N_CHIP,) + a.shape[1:], a.dtype) for a in arrs],
        in_specs=[ANY] * n, out_specs=[ANY] * n,
        scratch_shapes=[pltpu.SemaphoreType.DMA((N_CHIP * n,)), pltpu.SemaphoreType.DMA((N_CHIP * n,))],
    )(*arrs)


def _chip_sum(full, recv, core, name):
    _, rows, cols = full.shape
    tr = _blk(rows, 1024) if rows % LANES == 0 else rows

    def body(core_ref, a_ref, b_ref, o_ref):
        o_ref[...] = (a_ref[...].astype(F32) + b_ref[...].astype(F32)).astype(o_ref.dtype)

    grid_spec = pltpu.PrefetchScalarGridSpec(
        num_scalar_prefetch=1, grid=(N_CHIP, rows // tr),
        in_specs=[pl.BlockSpec((None, tr, cols), lambda k, i, core_ref: (2 * k + core_ref[0], i, 0)),
                  pl.BlockSpec((None, tr, cols), lambda k, i, core_ref: (k, i, 0))],
        out_specs=pl.BlockSpec((None, tr, cols), lambda k, i, core_ref: (k, i, 0)))
    return _pcall(body, name=name, out_shape=jax.ShapeDtypeStruct((N_CHIP, rows, cols), full.dtype),
                  grid_spec=grid_spec)(core, full, recv)


def _matmul(a, b, *, name, ta=False, tb=False, tm=1024, tn=1024, tk=2048, out_dtypes=(BF16,), epilogue=None,
            row_extras=(), tile_extras=(), out_shape=None, out_map=None, b_sharded=False, n_cols=None, n_off=0,
            after=None):
    m, k = (a.shape[1], a.shape[0]) if ta else a.shape
    if b_sharded:
        shard_c = b.shape[2]
        n, kb = (b.shape[1], N_DEV * shard_c) if tb else (N_DEV * shard_c, b.shape[1])
        tn, tk = (tn, min(tk, shard_c)) if tb else (min(tn, shard_c), tk)
    else:
        n, kb = b.shape if tb else (b.shape[1], b.shape[0])
    assert kb == k, (name, kb, k)
    if n_cols is not None:
        n = n_cols
    tm, tn, tk = _blk(m, tm), _blk(n, tn), _blk(k, tk)
    assert n_off % tn == 0
    nk = k // tk
    dims = (((0 if ta else 1,), (1 if tb else 0,)), ((), ()))
    behind = () if after is None else (after,)
    n_row, n_tile, n_out = len(row_extras), len(tile_extras), len(out_dtypes)
    first_out = 2 + n_row + n_tile + len(behind)

    def body(*refs):
        a_ref, b_ref = refs[:2]
        extras = refs[2:2 + n_row + n_tile]
        outs = refs[first_out:first_out + n_out]
        acc_ref = refs[-1]
        jj, kk = pl.program_id(1), pl.program_id(2)
        part = lax.dot_general(a_ref[...].astype(BF16), b_ref[...].astype(BF16), dims, preferred_element_type=F32)

        def finish(acc):
            res = (acc,) if epilogue is None else epilogue(acc, jj, *[e[...] for e in extras])
            for o_ref, r in zip(outs, res):
                o_ref[...] = r.astype(o_ref.dtype)

        if nk == 1:
            finish(part)
        else:
            @pl.when(kk == 0)
            def _():
                acc_ref[...] = part

            @pl.when(kk > 0)
            def _():
                acc_ref[...] += part

            @pl.when(kk == nk - 1)
            def _():
                finish(acc_ref[...])

    a_spec = pl.BlockSpec((tk, tm), lambda i, j, kk: (kk, i)) if ta else pl.BlockSpec((tm, tk), lambda i, j, kk: (i, kk))
    if b_sharded and tb:
        per = shard_c // tk
        b_spec = pl.BlockSpec((None, tn, tk), lambda i, j, kk: (kk // per, j, kk % per))
    elif b_sharded:
        per = shard_c // tn
        b_spec = pl.BlockSpec((None, tk, tn), lambda i, j, kk: (j // per, kk, j % per))
    elif tb:
        b_spec = pl.BlockSpec((tn, tk), lambda i, j, kk: (j + n_off // tn, kk))
    else:
        b_spec = pl.BlockSpec((tk, tn), lambda i, j, kk: (kk, j + n_off // tn))
    in_specs = [a_spec, b_spec]
    in_specs += [pl.BlockSpec((tm, LANES), lambda i, j, kk: (i, 0)) for _ in row_extras]
    in_specs += [pl.BlockSpec((tm, tn), lambda i, j, kk: (i, j)) for _ in tile_extras]
    in_specs += [ANY for _ in behind]
    if out_map is None:
        out_specs = [pl.BlockSpec((tm, tn), lambda i, j, kk: (i, j)) for _ in out_dtypes]
        shapes = [jax.ShapeDtypeStruct((m, n), dt) for dt in out_dtypes]
    else:
        out_specs = [out_map(tm, tn)]
        shapes = [jax.ShapeDtypeStruct(out_shape, out_dtypes[0])]
    acc_shape = (tm, tn) if nk > 1 else (8, LANES)
    res = _pcall(body, name=name, out_shape=shapes, grid=(m // tm, n // tn, nk), in_specs=in_specs,
                 out_specs=out_specs, scratch_shapes=[pltpu.VMEM(acc_shape, F32)])(
                     a, b, *row_extras, *tile_extras, *behind)
    return res[0] if n_out == 1 else res


def _rope_cols(acc, j, cos, sin, n_rope):
    width = acc.shape[1]
    parts = []
    for g in range(width // HEAD_DIM):
        xg = acc[:, g * HEAD_DIM:(g + 1) * HEAD_DIM]
        roped = xg * cos + pltpu.roll(xg, HEAD_DIM // 2, 1) * sin
        parts.append(jnp.where(j * width + g * HEAD_DIM < n_rope, roped, xg))
    return jnp.concatenate(parts, axis=1) if len(parts) > 1 else parts[0]


def _silu(v):
    return v / (1.0 + jnp.exp(-v))


def _mod_part(c_all, w_mod, b_part):
    d, w = w_mod.shape
    tk = _blk(d, 512)

    def body(c_ref, w_ref, b_ref, o_ref):
        kk = pl.program_id(0)
        cond = _silu(c_ref[...]).astype(BF16)
        part = jnp.dot(cond, w_ref[...].astype(BF16), preferred_element_type=F32)

        @pl.when(kk == 0)
        def _():
            o_ref[...] = part + b_ref[...]

        @pl.when(kk > 0)
        def _():
            o_ref[...] += part

    return _pcall(body, name="mod_part", out_shape=jax.ShapeDtypeStruct((N_DEV, w), F32), grid=(d // tk,),
                  in_specs=[pl.BlockSpec((N_DEV, tk), lambda kk: (0, kk)), pl.BlockSpec((tk, w), lambda kk: (kk, 0)),
                            pl.BlockSpec((1, w), lambda kk: (0, 0))],
                  out_specs=pl.BlockSpec((N_DEV, w), lambda kk: (0, 0)))(c_all, w_mod, b_part)


def _row_call(body, name, t, d, tiled_in, vec_in, tiled_out_dtypes, n_vec_out, tr=256):
    tr = _blk(t, tr)
    tile = pl.BlockSpec((tr, d), lambda i: (i, 0))
    vec = pl.BlockSpec((1, d), lambda i: (0, 0))
    out_shape = [jax.ShapeDtypeStruct((t, d), dt) for dt in tiled_out_dtypes]
    out_shape += [jax.ShapeDtypeStruct((1, d), F32)] * n_vec_out
    return _pcall(body, name=name, out_shape=out_shape, grid=(t // tr,),
                  in_specs=[tile] * len(tiled_in) + [vec] * len(vec_in),
                  out_specs=[tile] * len(tiled_out_dtypes) + [vec] * n_vec_out)(*tiled_in, *vec_in)


def _accumulate(ref, val):
    @pl.when(pl.program_id(0) == 0)
    def _():
        ref[...] = val

    @pl.when(pl.program_id(0) > 0)
    def _():
        ref[...] += val


def _rsum(v):
    return jnp.sum(v, axis=0, keepdims=True)


def _rms(v):
    return lax.rsqrt(jnp.mean(v * v, axis=-1, keepdims=True) + NORM_EPS)


def _rms_bwd(vhat, r, dvhat):
    return r * (dvhat - vhat * jnp.mean(dvhat * vhat, axis=-1, keepdims=True))


def _pre_attn(x, g0, sc_a, sh_a):
    def body(x_ref, g_ref, sc_ref, sh_ref, h_ref):
        xv = x_ref[...]
        h_ref[...] = (xv * _rms(xv) * g_ref[...] * (1.0 + sc_ref[...]) + sh_ref[...]).astype(BF16)

    t, d = x.shape
    return _row_call(body, "pre_attn", t, d, [x], [g0, sc_a, sh_a], [BF16], 0)[0]


def _post_mix(x, mix, gt_a, g1, g2, sc_m, sh_m):
    def body(x_ref, mix_ref, gt_ref, g1_ref, g2_ref, sc_ref, sh_ref, x2_ref, h2_ref):
        mv = mix_ref[...]
        x2 = x_ref[...] + gt_ref[...] * (mv * _rms(mv) * g1_ref[...])
        x2_ref[...] = x2
        h2_ref[...] = (x2 * _rms(x2) * g2_ref[...] * (1.0 + sc_ref[...]) + sh_ref[...]).astype(BF16)

    t, d = x.shape
    return _row_call(body, "post_mix", t, d, [x, mix], [gt_a, g1, g2, sc_m, sh_m], [F32, BF16], 0)


def _final(y, x2, target, gt_m, g3):
    t, d = y.shape

    def body(y_ref, x2_ref, tg_ref, gt_ref, g3_ref, dy_ref, dout_ref, dgt_ref, dg3_ref, loss_ref):
        yv = y_ref[...]
        r = _rms(yv)
        yhat = yv * r
        n3 = yhat * g3_ref[...]
        err = x2_ref[...] + gt_ref[...] * n3 - tg_ref[...]
        _accumulate(loss_ref, jnp.zeros((1, d), F32) + 0.5 * jnp.sum(err * err) / d)
        dout = err * (1.0 / d)
        dout_ref[...] = dout
        _accumulate(dgt_ref, _rsum(dout * n3))
        dn3 = dout * gt_ref[...]
        _accumulate(dg3_ref, _rsum(dn3 * yhat))
        dy_ref[...] = _rms_bwd(yhat, r, dn3 * g3_ref[...]).astype(BF16)

    return _row_call(body, "final", t, d, [y, x2, target], [gt_m, g3], [BF16, F32], 3)


def _mid_bwd(dh2, dout, x2, mix, g2, sc_m, gt_a, g1):
    t, d = x2.shape

    def body(dh2_ref, dout_ref, x2_ref, mix_ref, g2_ref, sc_ref, gt_ref, g1_ref,
             dmix_ref, dx2_ref, dsh_ref, dsc_ref, dg2_ref, dgt_ref, dg1_ref):
        dh2v = dh2_ref[...]
        x2v = x2_ref[...]
        r2 = _rms(x2v)
        x2hat = x2v * r2
        _accumulate(dsh_ref, _rsum(dh2v))
        _accumulate(dsc_ref, _rsum(dh2v * (x2hat * g2_ref[...])))
        dn2 = dh2v * (1.0 + sc_ref[...])
        _accumulate(dg2_ref, _rsum(dn2 * x2hat))
        dx2 = dout_ref[...] + _rms_bwd(x2hat, r2, dn2 * g2_ref[...])
        dx2_ref[...] = dx2
        mv = mix_ref[...]
        r1 = _rms(mv)
        mhat = mv * r1
        _accumulate(dgt_ref, _rsum(dx2 * (mhat * g1_ref[...])))
        dn1 = dx2 * gt_ref[...]
        _accumulate(dg1_ref, _rsum(dn1 * mhat))
        dmix_ref[...] = _rms_bwd(mhat, r1, dn1 * g1_ref[...]).astype(BF16)

    return _row_call(body, "mid_bwd", t, d, [dh2, dout, x2, mix], [g2, sc_m, gt_a, g1], [BF16, F32], 5)


def _x_bwd(dh1, dx2, x, g0, sc_a):
    t, d = x.shape

    def body(dh1_ref, dx2_ref, x_ref, g0_ref, sc_ref, dx_ref, dsh_ref, dsc_ref, dg0_ref):
        dh1v = dh1_ref[...]
        xv = x_ref[...]
        r0 = _rms(xv)
        xhat = xv * r0
        _accumulate(dsh_ref, _rsum(dh1v))
        _accumulate(dsc_ref, _rsum(dh1v * (xhat * g0_ref[...])))
        dn0 = dh1v * (1.0 + sc_ref[...])
        _accumulate(dg0_ref, _rsum(dn0 * xhat))
        dx_ref[...] = dx2_ref[...] + _rms_bwd(xhat, r0, dn0 * g0_ref[...])

    return _row_call(body, "x_bwd", t, d, [dh1, dx2, x], [g0, sc_a], [F32], 3)


def _pick_lane(block, h):
    lane = lax.broadcasted_iota(jnp.int32, block.shape, 1)
    return jnp.sum(jnp.where(lane == h, block, 0.0), axis=1, keepdims=True)


def _put_lane(ref, rows, h, col):
    old = ref[rows, :]
    lane = lax.broadcasted_iota(jnp.int32, old.shape, 1)
    ref[rows, :] = jnp.where(lane == h, col, old)


def _tri(n, lower):
    r = lax.broadcasted_iota(jnp.int32, (n, n), 0)
    c = lax.broadcasted_iota(jnp.int32, (n, n), 1)
    return jnp.where((c <= r) if lower else (c >= r), 1.0, 0.0).astype(F32)


def _cum_fwd(fg, b128):
    t = fg.shape[0]
    nb = t // LANES

    def body(fg_ref, b_ref, cum_ref, cumt_ref):
        tri = _tri(LANES, True)
        carry = jnp.zeros((1, LANES), F32)
        for i in range(nb):
            z = fg_ref[i * LANES:(i + 1) * LANES, :] + b_ref[...]
            lf = jnp.minimum(z, 0.0) - jnp.log(1.0 + jnp.exp(-jnp.abs(z)))
            blk = jnp.dot(tri, lf, precision=lax.Precision.HIGHEST, preferred_element_type=F32) + carry
            cum_ref[i * LANES:(i + 1) * LANES, :] = blk
            carry = blk[LANES - 1:LANES, :]
        cumt_ref[...] = cum_ref[...].T[0:N_FOX, :]

    return _pcall(body, name="cum_fwd",
                  out_shape=[jax.ShapeDtypeStruct((t, LANES), F32), jax.ShapeDtypeStruct((N_FOX, t), F32)],
                  grid=(1,),
                  in_specs=[pl.BlockSpec((t, LANES), lambda i: (0, 0)), pl.BlockSpec((1, LANES), lambda i: (0, 0))],
                  out_specs=[pl.BlockSpec((t, LANES), lambda i: (0, 0)), pl.BlockSpec((N_FOX, t), lambda i: (0, 0))],
                  )(fg, b128)


def _fg_bwd(dcs_rows, fg, b128):
    t = fg.shape[0]
    nb = t // LANES

    def body(dcs_ref, fg_ref, b_ref, dfg_ref, db_ref, dcum_ref):
        dcum_ref[...] = -jnp.concatenate([dcs_ref[...], jnp.zeros((LANES - N_FOX, t), F32)], axis=0).T
        tri = _tri(LANES, False)
        carry = jnp.zeros((1, LANES), F32)
        db = jnp.zeros((1, LANES), F32)
        for i in reversed(range(nb)):
            rows = slice(i * LANES, (i + 1) * LANES)
            dlf = jnp.dot(tri, dcum_ref[rows, :], precision=lax.Precision.HIGHEST, preferred_element_type=F32) + carry
            carry = dlf[0:1, :]
            z = fg_ref[rows, :] + b_ref[...]
            dfg = dlf / (1.0 + jnp.exp(z))
            dfg_ref[rows, :] = dfg.astype(BF16)
            db = db + _rsum(dfg)
        db_ref[...] = db

    full = pl.BlockSpec((t, LANES), lambda i: (0, 0))
    vec = pl.BlockSpec((1, LANES), lambda i: (0, 0))
    return _pcall(body, name="fg_bwd",
                  out_shape=[jax.ShapeDtypeStruct((t, LANES), BF16), jax.ShapeDtypeStruct((1, LANES), F32)],
                  grid=(1,), in_specs=[pl.BlockSpec((N_FOX, t), lambda i: (0, 0)), full, vec], out_specs=[full, vec],
                  scratch_shapes=[pltpu.VMEM((t, LANES), F32)])(dcs_rows, fg, b128)


def _head_spec(t, col0, div=1):
    return pl.BlockSpec((t, HEAD_DIM), lambda h: (0, col0 + h // div))


def _fox_scores(q, k, cq, ck, i, tq, end):
    s = lax.dot_general(q, k, NT_DIMS, preferred_element_type=F32) * ATT_SCALE + cq - ck
    row = lax.broadcasted_iota(jnp.int32, (tq, end), 0) + i * tq
    col = lax.broadcasted_iota(jnp.int32, (tq, end), 1)
    return jnp.where(row >= col, s, -jnp.inf)


def _fox_fwd(proj_a, cum, cumt):
    t = proj_a.shape[0]
    tq = _blk(t, 512)
    nq = t // tq

    def body(q_ref, k_ref, v_ref, cum_ref, cumt_ref, o_ref, lse_ref):
        h = pl.program_id(0)
        cq_all = _pick_lane(cum_ref[...], h)
        ck_all = cumt_ref[pl.ds(h, 1), :]

        @pl.when(h == 0)
        def _():
            lse_ref[...] = jnp.zeros_like(lse_ref)

        for i in range(nq):
            rows, end = slice(i * tq, (i + 1) * tq), (i + 1) * tq
            s = _fox_scores(q_ref[rows, :], k_ref[0:end, :], cq_all[rows, :], ck_all[:, 0:end], i, tq, end)
            m = jnp.max(s, axis=1, keepdims=True)
            p = jnp.exp(s - m)
            l = jnp.sum(p, axis=1, keepdims=True)
            o = jnp.dot(p.astype(BF16), v_ref[0:end, :], preferred_element_type=F32) / l
            o_ref[rows, :] = o.astype(BF16)
            _put_lane(lse_ref, rows, h, m + jnp.log(l))

    nh = FOX_W // HEAD_DIM
    stat = pl.BlockSpec((t, LANES), lambda h: (0, 0))
    return _pcall(body, name="fox_fwd",
                  out_shape=[jax.ShapeDtypeStruct((t, FOX_W), BF16), jax.ShapeDtypeStruct((t, LANES), F32)],
                  grid=(N_FOX,),
                  in_specs=[_head_spec(t, 0), _head_spec(t, nh), _head_spec(t, 2 * nh), stat,
                            pl.BlockSpec((N_FOX, t), lambda h: (0, 0))],
                  out_specs=[_head_spec(t, 0), stat])(proj_a, proj_a, proj_a, cum, cumt)


def _fox_bwd(proj_a, d_attn, cum, cumt, lse):
    t = proj_a.shape[0]
    tq = _blk(t, 512)
    nq = t // tq

    def body(q_ref, k_ref, v_ref, do_ref, cum_ref, cumt_ref, lse_ref,
             dq_ref, dk_ref, dv_ref, dcs_ref, dk_acc, dv_acc, dcs_acc):
        h = pl.program_id(0)
        cq_all = _pick_lane(cum_ref[...], h)
        ck_all = cumt_ref[pl.ds(h, 1), :]
        lse_all = _pick_lane(lse_ref[...], h)
        dk_acc[...] = jnp.zeros_like(dk_acc)
        dv_acc[...] = jnp.zeros_like(dv_acc)
        dcs_acc[...] = jnp.zeros_like(dcs_acc)
        for i in range(nq):
            rows, end = slice(i * tq, (i + 1) * tq), (i + 1) * tq
            q, k, v, do = q_ref[rows, :], k_ref[0:end, :], v_ref[0:end, :], do_ref[rows, :]
            s = _fox_scores(q, k, cq_all[rows, :], ck_all[:, 0:end], i, tq, end)
            p = jnp.exp(s - lse_all[rows, :])
            dp = lax.dot_general(do, v, NT_DIMS, preferred_element_type=F32)
            ds = p * (dp - jnp.sum(p * dp, axis=1, keepdims=True))
            dcs_acc[:, 0:end] += jnp.sum(ds, axis=0, keepdims=True)
            ds = ds.astype(BF16)
            dq_ref[rows, :] = (jnp.dot(ds, k, preferred_element_type=F32) * ATT_SCALE).astype(BF16)
            dk_acc[0:end, :] += lax.dot_general(ds, q, TN_DIMS, preferred_element_type=F32)
            dv_acc[0:end, :] += lax.dot_general(p.astype(BF16), do, TN_DIMS, preferred_element_type=F32)
        dk_ref[...] = (dk_acc[...] * ATT_SCALE).astype(BF16)
        dv_ref[...] = dv_acc[...].astype(BF16)
        dcs_ref[pl.ds(h, 1), :] = dcs_acc[...]

    nh = FOX_W // HEAD_DIM
    stat = pl.BlockSpec((t, LANES), lambda h: (0, 0))
    rows8 = pl.BlockSpec((N_FOX, t), lambda h: (0, 0))
    head = _head_spec(t, 0)
    wide = jax.ShapeDtypeStruct((t, FOX_W), BF16)
    return _pcall(body, name="fox_bwd",
                  out_shape=[wide, wide, wide, jax.ShapeDtypeStruct((N_FOX, t), F32)],
                  grid=(N_FOX,),
                  in_specs=[_head_spec(t, 0), _head_spec(t, nh), _head_spec(t, 2 * nh), head, stat, rows8, stat],
                  out_specs=[head, head, head, rows8],
                  scratch_shapes=[pltpu.VMEM((t, HEAD_DIM), F32), pltpu.VMEM((t, HEAD_DIM), F32),
                                  pltpu.VMEM((1, t), F32)],
                  )(proj_a, proj_a, proj_a, d_attn, cum, cumt, lse)


def _swa_scores(q, k, i, tq, start, end):
    s = lax.dot_general(q, k, NT_DIMS, preferred_element_type=F32) * ATT_SCALE
    row = lax.broadcasted_iota(jnp.int32, (tq, end - start), 0) + i * tq
    col = lax.broadcasted_iota(jnp.int32, (tq, end - start), 1) + start
    diff = row - col
    return jnp.where((diff >= 0) & (diff < WINDOW), s, -jnp.inf)


def _swa_blocks(t):
    tq = _blk(t, 256)
    return tq, [(i, max(0, i * tq - WINDOW), (i + 1) * tq) for i in range(t // tq)]


def _swa_fwd(proj_b, sinks128):
    t = proj_b.shape[0]
    tq, blocks = _swa_blocks(t)

    def body(q_ref, k_ref, v_ref, sink_ref, o_ref, lse_ref):
        h = pl.program_id(0)
        sink = _pick_lane(sink_ref[...], h)

        @pl.when(h == 0)
        def _():
            lse_ref[...] = jnp.zeros_like(lse_ref)

        for i, start, end in blocks:
            rows = slice(i * tq, end)
            s = _swa_scores(q_ref[rows, :], k_ref[start:end, :], i, tq, start, end)
            m = jnp.maximum(jnp.max(s, axis=1, keepdims=True), sink)
            p = jnp.exp(s - m)
            l = jnp.sum(p, axis=1, keepdims=True) + jnp.exp(sink - m)
            o = jnp.dot(p.astype(BF16), v_ref[start:end, :], preferred_element_type=F32) / l
            o_ref[rows, :] = o.astype(BF16)
            _put_lane(lse_ref, rows, h, m + jnp.log(l))

    stat = pl.BlockSpec((t, LANES), lambda h: (0, 0))
    return _pcall(body, name="swa_fwd",
                  out_shape=[jax.ShapeDtypeStruct((t, SWA_W), BF16), jax.ShapeDtypeStruct((t, LANES), F32)],
                  grid=(N_SWA,),
                  in_specs=[_head_spec(t, 0), _head_spec(t, N_SWA, GQA), _head_spec(t, N_SWA + N_KV, GQA),
                            pl.BlockSpec((1, LANES), lambda h: (0, 0))],
                  out_specs=[_head_spec(t, 0), stat])(proj_b, proj_b, proj_b, sinks128)


def _rope_bwd(d, cos, sin):
    return d * cos + pltpu.roll(d * sin, HEAD_DIM // 2, 1)


def _swa_bwd(proj_b, d_attn, lse, sinks128, cos, sin):
    t = proj_b.shape[0]
    tq, blocks = _swa_blocks(t)

    def body(q_ref, k_ref, v_ref, do_ref, lse_ref, sink_ref, cos_ref, sin_ref,
             dq_ref, dk_ref, dv_ref, dsink_ref, dk_acc, dv_acc):
        h = pl.program_id(0)
        sink = _pick_lane(sink_ref[...], h)
        lse_all = _pick_lane(lse_ref[...], h)

        @pl.when(h == 0)
        def _():
            dsink_ref[...] = jnp.zeros_like(dsink_ref)

        @pl.when(h % GQA == 0)
        def _():
            dk_acc[...] = jnp.zeros_like(dk_acc)
            dv_acc[...] = jnp.zeros_like(dv_acc)

        dsink = jnp.zeros((1, 1), F32)
        for i, start, end in blocks:
            rows = slice(i * tq, end)
            q, k, v, do = q_ref[rows, :], k_ref[start:end, :], v_ref[start:end, :], do_ref[rows, :]
            s = _swa_scores(q, k, i, tq, start, end)
            p = jnp.exp(s - lse_all[rows, :])
            dp = lax.dot_general(do, v, NT_DIMS, preferred_element_type=F32)
            delta = jnp.sum(p * dp, axis=1, keepdims=True)
            ds = (p * (dp - delta)).astype(BF16)
            dq = jnp.dot(ds, k, preferred_element_type=F32) * ATT_SCALE
            dq_ref[rows, :] = _rope_bwd(dq, cos_ref[rows, :], sin_ref[rows, :]).astype(BF16)
            dk_acc[start:end, :] += lax.dot_general(ds, q, TN_DIMS, preferred_element_type=F32)
            dv_acc[start:end, :] += lax.dot_general(p.astype(BF16), do, TN_DIMS, preferred_element_type=F32)
            dsink = dsink - jnp.sum(jnp.exp(sink - lse_all[rows, :]) * delta, axis=0, keepdims=True)
        old = dsink_ref[...]
        lane = lax.broadcasted_iota(jnp.int32, old.shape, 1)
        dsink_ref[...] = jnp.where(lane == h, dsink, old)

        @pl.when(h % GQA == GQA - 1)
        def _():
            dk_ref[...] = _rope_bwd(dk_acc[...] * ATT_SCALE, cos_ref[...], sin_ref[...]).astype(BF16)
            dv_ref[...] = dv_acc[...].astype(BF16)

    stat = pl.BlockSpec((t, LANES), lambda h: (0, 0))
    vec = pl.BlockSpec((1, LANES), lambda h: (0, 0))
    head = _head_spec(t, 0)
    kv_out = _head_spec(t, 0, GQA)
    return _pcall(body, name="swa_bwd",
                  out_shape=[jax.ShapeDtypeStruct((t, SWA_W), BF16), jax.ShapeDtypeStruct((t, KV_W), BF16),
                             jax.ShapeDtypeStruct((t, KV_W), BF16), jax.ShapeDtypeStruct((1, LANES), F32)],
                  grid=(N_SWA,),
                  in_specs=[head, _head_spec(t, N_SWA, GQA), _head_spec(t, N_SWA + N_KV, GQA),
                            _head_spec(t, N_FOX), stat, vec, stat, stat],
                  out_specs=[head, kv_out, kv_out, vec],
                  scratch_shapes=[pltpu.VMEM((t, HEAD_DIM), F32), pltpu.VMEM((t, HEAD_DIM), F32)],
                  )(proj_b, proj_b, proj_b, d_attn, lse, sinks128, cos, sin)


def _adamw(w, g, m, v):
    m = ADAM_B1 * m + (1.0 - ADAM_B1) * g
    v = ADAM_B2 * v + (1.0 - ADAM_B2) * (g * g)
    m_hat = m / (1.0 - ADAM_B1 ** ADAM_STEP)
    v_hat = v / (1.0 - ADAM_B2 ** ADAM_STEP)
    delta = -ADAM_LR * (m_hat / (jnp.sqrt(v_hat) + ADAM_EPS) + ADAM_WD * w)
    return delta, m, v


def _adam_pieces(w, m, v, own, land, idx, name):
    rows, cols = w.shape
    tr, tc = (256, cols) if rows % 256 == 0 else (rows, _blk(cols, 512))

    def body(idx_ref, own_ref, l1_ref, l2_ref, l3_ref, w_ref, m_ref, v_ref, g_ref, d_ref, mo_ref, vo_ref):
        g = own_ref[...].astype(F32) + l1_ref[...].astype(F32) + l2_ref[...].astype(F32) + l3_ref[...].astype(F32)
        g_ref[...] = g
        d_ref[...], mo_ref[...], vo_ref[...] = _adamw(w_ref[...], g, m_ref[...], v_ref[...])

    def piece(p):
        return pl.BlockSpec((None, tr, tc), lambda i, j, idx_ref: (idx_ref[p], i, j))

    tile = pl.BlockSpec((tr, tc), lambda i, j, idx_ref: (i, j))
    out = jax.ShapeDtypeStruct((rows, cols), F32)
    grid_spec = pltpu.PrefetchScalarGridSpec(
        num_scalar_prefetch=1, grid=(rows // tr, cols // tc),
        in_specs=[piece(0), piece(1), piece(2), piece(3), tile, tile, tile], out_specs=[tile] * 4)
    return _pcall(body, name=name, out_shape=[out] * 4, grid_spec=grid_spec)(idx, own, land, land, land, w, m, v)


def _adam_mod(c_all, dmod_cols, w, m, v):
    rows, cols = w.shape
    tr = _blk(rows, 256)

    def body(c_ref, dm_ref, w_ref, m_ref, v_ref, g_ref, d_ref, mo_ref, vo_ref):
        cond = _silu(c_ref[...]).astype(BF16)
        g = lax.dot_general(cond, dm_ref[...].astype(BF16), TN_DIMS, preferred_element_type=F32)
        g_ref[...] = g
        d_ref[...], mo_ref[...], vo_ref[...] = _adamw(w_ref[...], g, m_ref[...], v_ref[...])

    tile = pl.BlockSpec((tr, cols), lambda i: (i, 0))
    out = jax.ShapeDtypeStruct((rows, cols), F32)
    return _pcall(body, name="adam_mod", out_shape=[out] * 4, grid=(rows // tr,),
                  in_specs=[pl.BlockSpec((N_DEV, tr), lambda i: (0, i)), pl.BlockSpec((N_DEV, cols), lambda i: (0, 0)),
                            tile, tile, tile],
                  out_specs=[tile] * 4)(c_all, dmod_cols, w, m, v)


def _adam_small(parts, w, m, v):
    nv = w.shape[1]

    def body(p_ref, w_ref, m_ref, v_ref, g_ref, d_ref, mo_ref, vo_ref):
        g = p_ref[0:1, :]
        for k in range(1, N_DEV):
            g = g + p_ref[k:k + 1, :]
        g_ref[...] = g
        d_ref[...], mo_ref[...], vo_ref[...] = _adamw(w_ref[...], g, m_ref[...], v_ref[...])

    vec = pl.BlockSpec((1, nv), lambda i: (0, 0))
    out = jax.ShapeDtypeStruct((1, nv), F32)
    return _pcall(body, name="adam_small", out_shape=[out] * 4, grid=(1,),
                  in_specs=[pl.BlockSpec((N_DEV, nv), lambda i: (0, 0)), vec, vec, vec],
                  out_specs=[vec] * 4)(parts, w, m, v)


def _pad_lanes(v, width=LANES):
    return jnp.pad(v, ((0, 0), (0, width - v.shape[1])))


def kernel(x, c, w_mod, b_mod, g_pre_mix, g_post_mix, w_in, b_forget, swa_sinks, w_out, g_pre_mlp, g_post_mlp, w_up, w_down, loss_target, m_w_mod, m_b_mod, m_g_pre_mix, m_g_post_mix, m_w_in, m_b_forget, m_swa_sinks, m_w_out, m_g_pre_mlp, m_g_post_mlp, m_w_up, m_w_down, v_w_mod, v_b_mod, v_g_pre_mix, v_g_post_mix, v_w_in, v_b_forget, v_swa_sinks, v_w_out, v_g_pre_mlp, v_g_post_mlp, v_w_up, v_w_down):
    ax, ay, ac = _position()
    me = 4 * ax + 2 * ay + ac
    x, target = x[0], loss_target[0]
    t, d = x.shape
    w_mod, w_in, w_out, w_up, w_down = w_mod[0], w_in[0], w_out[0], w_up[0], w_down[0]
    mod_w = w_mod.shape[1]
    in_w = w_in.shape[1]
    in_total = N_DEV * in_w
    shard_ff = w_up.shape[1]
    n_fox3 = 3 * FOX_W
    n_swa3 = SWA_W + 2 * KV_W
    assert in_total == n_fox3 + N_FOX + n_swa3 and d == FOX_W + SWA_W

    first = _ag_start([w_in.astype(BF16)], "ag_start_in")

    c_all = _all_gather([c + first[4][0:1, 0:1]], "gather_c")[0].reshape(N_DEV, d)
    b_part = lax.dynamic_slice(b_mod, (0, me * mod_w), (1, mod_w))
    mod_parts = _all_gather([_mod_part(c_all, w_mod, b_part)], "gather_mod")[0]
    mod = lax.dynamic_index_in_dim(mod_parts, me, axis=1, keepdims=False).reshape(1, N_DEV * mod_w)
    shards = [w_out.astype(BF16), w_up.astype(BF16), w_down.astype(BF16)]
    shards, mod = lax.optimization_barrier((shards, mod))
    rest = _ag_start(shards, "ag_start_rest")
    ag_send, ag_recv, ag_shard, ag_land = [a + b for a, b in zip(first[:4], rest[:4])]
    ag_token = rest[4]

    def gathered(i, after, name):
        shard, land = _ag_wait(ag_send[i], ag_recv[i], ag_shard[i], ag_land[i], after, "ag_wait_" + name)
        return lax.dynamic_update_slice(_ag_forward(land, "ag_fwd_" + name), shard[None], (me, 0, 0))

    sh_a, sc_a, gt_a, sh_m, sc_m, gt_m = [mod[:, i * d:(i + 1) * d] for i in range(6)]

    half = HEAD_DIM // 2
    inv_freq = 1.0 / (ROPE_THETA ** (jnp.arange(half, dtype=F32) * (2.0 / HEAD_DIM)))
    ang = jnp.arange(t).astype(F32)[:, None] * inv_freq[None, :]
    cos = jnp.concatenate([jnp.cos(ang), jnp.cos(ang)], axis=1)
    sin = jnp.concatenate([-jnp.sin(ang), jnp.sin(ang)], axis=1)

    b128 = _pad_lanes(b_forget)
    sinks128 = _pad_lanes(swa_sinks)

    h1 = _pre_attn(x, g_pre_mix + ag_token[0:1, 0:1], sc_a, sh_a)
    w_in_g = gathered(0, h1, "in")
    o_fg, o_sq = n_fox3, n_fox3 + N_FOX

    def cols(lo, hi):
        parts = []
        for j in range(lo // in_w, (hi - 1) // in_w + 1):
            parts.append(w_in_g[j, :, max(lo - j * in_w, 0):min(hi - j * in_w, in_w)])
        return parts

    w_in_r = jnp.concatenate(cols(0, o_fg) + cols(o_sq, in_total) + cols(o_fg, o_sq)
                             + [jnp.zeros((d, FG_PAD - N_FOX), BF16)], axis=1)
    proj_a = _matmul(h1, w_in_r, name="proj_a", n_cols=n_fox3, n_off=0)
    proj_b = _matmul(h1, w_in_r, name="proj_b", n_cols=n_swa3, n_off=n_fox3, tn=512, row_extras=(cos, sin),
                     epilogue=lambda acc, j, cs, sn: (_rope_cols(acc, j, cs, sn, SWA_W + KV_W),))
    fg = _matmul(h1, w_in_r, name="proj_fg", n_cols=FG_PAD, n_off=n_fox3 + n_swa3, tn=FG_PAD,
                 out_dtypes=(F32,))[:, 0:LANES]
    cum, cumt = _cum_fwd(fg, b128)
    fox_o, fox_lse = _fox_fwd(proj_a, cum, cumt)
    swa_o, swa_lse = _swa_fwd(proj_b, sinks128)
    w_out_full = gathered(1, swa_o, "out").reshape(d, d)
    attn = jnp.concatenate([fox_o, swa_o], axis=1)
    mix = _matmul(attn, w_out_full, name="out_proj", out_dtypes=(F32,))
    x2, h2 = _post_mix(x, mix, gt_a, g_post_mix, g_pre_mlp, sc_m, sh_m)
    w_up_g = gathered(2, h2, "up")
    u, act = _matmul(h2, w_up_g, name="mlp_up", b_sharded=True, out_dtypes=(BF16, BF16),
                     epilogue=lambda acc, j: (acc, jnp.square(jnp.maximum(acc, 0.0))))
    w_down_full = gathered(3, act, "down").reshape(N_DEV * shard_ff, d)
    y = _matmul(act, w_down_full, name="mlp_down", out_dtypes=(F32,))

    core = jnp.reshape(ac, (1,)).astype(jnp.int32)

    def reduce_start(started, after, name):
        send, recv, src, land, _ = started
        full, from_sibling = _sib_wait(send, recv, src, land, after, "sib_wait_" + name)
        return _rs_start(_chip_sum(full, from_sibling, core, "chip_sum_" + name), "rs_start_" + name)

    def tok(started):
        return started[4][0:1, 0:1]

    idx = jnp.stack([2 * ax + ay, 2 * (1 - ax) + ay, 2 * ax + (1 - ay), 2 * (1 - ax) + (1 - ay)]).astype(jnp.int32)

    def reduce_finish(started, after, w, m, v, name):
        send, recv, src, land, _ = started
        own, landed = _rs_wait(send, recv, src, land, after, "rs_wait_" + name)
        return _adam_pieces(w, m[0], v[0], own, landed, idx, "adam_" + name)

    dy, dout, dgt_m, dg3, loss_vec = _final(y, x2, target, gt_m, g_post_mlp)
    du = _matmul(dy, w_down_full, name="d_act", tb=True, tile_extras=(u,),
                 epilogue=lambda acc, j, uu: (acc * (2.0 * jnp.maximum(uu.astype(F32), 0.0)),))
    dw_down = _matmul(act, dy, name="dw_down", ta=True)
    sb_down = _sib_start(dw_down.reshape(N_DEV, shard_ff, d), "sib_start_down")
    dh2 = _matmul(du, w_up_g, name="d_h2", tb=True, b_sharded=True, out_dtypes=(F32,), after=sb_down[4])
    rs_down = reduce_start(sb_down, dh2, "down")
    per = shard_ff // _blk(shard_ff, 1024)
    dw_up = _matmul(h2, du, name="dw_up", ta=True, tn=_blk(shard_ff, 1024), out_shape=(N_DEV, d, shard_ff),
                    out_map=lambda tm, tn: pl.BlockSpec((None, tm, tn), lambda i, j, kk: (j // per, i, j % per)),
                    after=rs_down[4])
    sb_up = _sib_start(dw_up, "sib_start_up")
    dmix, dx2, dsh_m, dsc_m, dg2, dgt_a, dg1 = _mid_bwd(
        dh2, dout, x2, mix, g_pre_mlp + tok(sb_up), sc_m, gt_a, g_post_mix)
    d_attn = _matmul(dmix, w_out_full, name="d_attn", tb=True)
    rs_up = reduce_start(sb_up, d_attn, "up")
    dw_out = _matmul(attn, dmix, name="dw_out", ta=True, after=rs_up[4])
    sb_out = _sib_start(dw_out.reshape(N_DEV, d // N_DEV, d), "sib_start_out")
    dqf, dkf, dvf, dcs = _fox_bwd(proj_a, d_attn, cum, cumt, fox_lse)
    dsq, dsk, dsv, dsinks = _swa_bwd(proj_b, d_attn, swa_lse, sinks128 + tok(sb_out), cos, sin)
    rs_out = reduce_start(sb_out, dsq, "out")
    dfg, db_forget = _fg_bwd(dcs, fg, b128 + tok(rs_out))
    dproj = jnp.concatenate([dqf, dkf, dvf, dsq, dsk, dsv, _pad_lanes(dfg, FG_PAD)], axis=1)
    dw_in_r = _matmul(dproj, h1, name="dw_in", ta=True)

    def shard_rows(j):
        lo, hi = j * in_w, (j + 1) * in_w
        parts = []
        for seg_lo, seg_hi, shift in ((0, o_fg, 0), (o_fg, o_sq, n_swa3), (o_sq, in_total, -N_FOX)):
            a, b = max(lo, seg_lo), min(hi, seg_hi)
            if a < b:
                parts.append(dw_in_r[a + shift:b + shift, :])
        return parts[0] if len(parts) == 1 else jnp.concatenate(parts, axis=0)

    sb_in = _sib_start(jnp.stack([shard_rows(j) for j in range(N_DEV)]), "sib_start_in")
    dh1 = _matmul(dproj, w_in_r, name="d_h1", tb=True, out_dtypes=(F32,), after=sb_in[4])
    grad_x, dsh_a, dsc_a, dg0 = _x_bwd(dh1, dx2, x, g_pre_mix, sc_a)

    small = jnp.concatenate([dsh_a, dsc_a, dgt_a, dsh_m, dsc_m, dgt_m, dg0, dg1, dg2, dg3, db_forget, dsinks,
                             loss_vec[:, 0:LANES]], axis=1)
    small_all = _all_gather([small], "gather_small")[0].reshape(N_DEV, small.shape[1])
    rs_in = reduce_start(sb_in, small_all, "in")

    pack = lambda bm, g0_, g1_, g2_, g3_, bf_, sk_: jnp.concatenate(
        [bm, g0_, g1_, g2_, g3_, _pad_lanes(bf_), _pad_lanes(sk_), jnp.zeros((1, LANES), F32)], axis=1)
    p_small = pack(b_mod, g_pre_mix, g_post_mix, g_pre_mlp, g_post_mlp, b_forget, swa_sinks)
    m_small = pack(m_b_mod, m_g_pre_mix, m_g_post_mix, m_g_pre_mlp, m_g_post_mlp, m_b_forget, m_swa_sinks)
    v_small = pack(v_b_mod, v_g_pre_mix, v_g_post_mix, v_g_pre_mlp, v_g_post_mlp, v_b_forget, v_swa_sinks)
    small_out = _adam_small(small_all, p_small + tok(rs_in), m_small, v_small)

    n_mod = 6 * d

    def unpack(vec):
        o = n_mod
        return (vec[:, 0:n_mod], vec[:, o:o + d], vec[:, o + d:o + 2 * d], vec[:, o + 2 * d:o + 3 * d],
                vec[:, o + 3 * d:o + 4 * d], vec[:, o + 4 * d:o + 4 * d + N_FOX],
                vec[:, o + 4 * d + LANES:o + 4 * d + LANES + N_SWA])

    loss = small_out[0][0, n_mod + 4 * d + 2 * LANES]
    g_small, d_small, nm_small, nv_small = [unpack(vec) for vec in small_out]

    dmod_cols = lax.dynamic_slice(small_all, (0, me * mod_w), (N_DEV, mod_w))
    g_w_mod, d_w_mod, nm_w_mod, nv_w_mod = _adam_mod(c_all + tok(rs_in), dmod_cols, w_mod, m_w_mod[0], v_w_mod[0])

    g_w_down, d_w_down, nm_w_down, nv_w_down = reduce_finish(rs_down, d_w_mod, w_down, m_w_down, v_w_down, "w_down")
    g_w_up, d_w_up, nm_w_up, nv_w_up = reduce_finish(rs_up, d_w_down, w_up, m_w_up, v_w_up, "w_up")
    g_w_out, d_w_out, nm_w_out, nv_w_out = reduce_finish(rs_out, d_w_up, w_out, m_w_out, v_w_out, "w_out")
    g_w_in, d_w_in, nm_w_in, nv_w_in = reduce_finish(rs_in, d_w_out, jnp.transpose(w_in), (jnp.transpose(m_w_in[0]),),
                                                     (jnp.transpose(v_w_in[0]),), "w_in")

    def assemble(w_mod_, small_, w_in_, w_out_, w_up_, w_down_):
        b_mod_, g0_, g1_, g2_, g3_, bf_, sk_ = small_
        return [w_mod_[None], b_mod_, g0_, g1_, jnp.transpose(w_in_)[None], bf_, sk_, w_out_[None], g2_, g3_,
                w_up_[None], w_down_[None]]

    outs = [loss, grad_x[None]]
    outs += assemble(g_w_mod, g_small, g_w_in, g_w_out, g_w_up, g_w_down)
    outs += assemble(d_w_mod, d_small, d_w_in, d_w_out, d_w_up, d_w_down)
    outs += assemble(nm_w_mod, nm_small, nm_w_in, nm_w_out, nm_w_up, nm_w_down)
    outs += assemble(nv_w_mod, nv_small, nv_w_in, nv_w_out, nv_w_up, nv_w_down)
    return tuple(outs)
```

```python
import functools

import jax
import jax.numpy as jnp
from jax import lax
from jax.experimental import pallas as pl
from jax.experimental.pallas import tpu as pltpu

F32 = jnp.float32
BF16 = jnp.bfloat16
MESH = pl.DeviceIdType.MESH

N_DEV = 8
N_CHIP = 4
LANES = 128
HEAD_DIM = 128
N_FOX = 8
N_SWA = 8
N_KV = 2
GQA = N_SWA // N_KV
WINDOW = 128
FOX_W = N_FOX * HEAD_DIM
SWA_W = N_SWA * HEAD_DIM
KV_W = N_KV * HEAD_DIM
ROPE_THETA = 10000.0
NORM_EPS = 1e-6
ATT_SCALE = HEAD_DIM ** -0.5
FG_PAD = 512

ADAM_LR = 0.001
ADAM_B1 = 0.9
ADAM_B2 = 0.999
ADAM_EPS = 1e-08
ADAM_WD = 0.01
ADAM_STEP = 10

VMEM_LIMIT = 56 * 1024 * 1024

NT_DIMS = (((1,), (1,)), ((), ()))
TN_DIMS = (((0,), (0,)), ((), ()))
NN_DIMS = (((1,), (0,)), ((), ()))


def _pcall(body, *, name, out_shape, grid=(), in_specs=None, out_specs=None, scratch_shapes=(), grid_spec=None):
    params = pltpu.CompilerParams(vmem_limit_bytes=VMEM_LIMIT)
    if grid_spec is not None:
        return pl.pallas_call(body, name=name, out_shape=out_shape, grid_spec=grid_spec, compiler_params=params)
    return pl.pallas_call(body, name=name, out_shape=out_shape, grid=grid, in_specs=in_specs, out_specs=out_specs,
                          scratch_shapes=scratch_shapes, compiler_params=params)


def _blk(n, pref):
    if n <= pref:
        return n
    b = (pref // LANES) * LANES
    while n % b:
        b -= LANES
    return b


def _position():
    return lax.axis_index("x"), lax.axis_index("y"), lax.axis_index("c")


ANY = pl.BlockSpec(memory_space=pl.ANY)


def _all_gather(arrs, name):
    n = len(arrs)

    def body(*refs):
        ins, outs = refs[:n], refs[n:2 * n]
        send_sems, recv_sems, local_sems = refs[2 * n:]
        x, y, c = _position()
        me, sibling = (x, y, c), (x, y, 1 - c)
        chips = [(1 - x, y), (x, 1 - y), (1 - x, 1 - y)]

        def slot(p):
            return 4 * p[0] + 2 * p[1] + p[2]

        def copy(a, k, block, to, src=None):
            dst = outs[a].at[slot(block)]
            return pltpu.make_async_remote_copy(
                src_ref=dst if src is None else src, dst_ref=dst,
                send_sem=send_sems.at[7 * a + k], recv_sem=recv_sems.at[7 * a + k],
                device_id=to, device_id_type=MESH)

        mine = [pltpu.make_async_copy(ins[a], outs[a].at[slot(me)], local_sems.at[a]) for a in range(n)]
        for cp in mine:
            cp.start()
        first = []
        for a in range(n):
            first.append(copy(a, 0, me, sibling, src=ins[a]))
            first += [copy(a, 1 + j, me, (*chip, c), src=ins[a]) for j, chip in enumerate(chips)]
        for cp in first:
            cp.start()
        passed = []
        for a in range(n):
            for j, chip in enumerate(chips):
                copy(a, 1 + j, (*chip, c), me).wait_recv()
                cp = copy(a, 4 + j, (*chip, c), sibling)
                cp.start()
                passed.append(cp)
        for a in range(n):
            copy(a, 0, sibling, me).wait_recv()
            for j, chip in enumerate(chips):
                copy(a, 4 + j, (*chip, 1 - c), me).wait_recv()
        for cp in first + passed:
            cp.wait_send()
        for cp in mine:
            cp.wait()

    return _pcall(
        body, name=name,
        out_shape=[jax.ShapeDtypeStruct((N_DEV,) + a.shape, a.dtype) for a in arrs],
        in_specs=[ANY] * n, out_specs=[ANY] * n,
        scratch_shapes=[pltpu.SemaphoreType.DMA((7 * n,)), pltpu.SemaphoreType.DMA((7 * n,)),
                        pltpu.SemaphoreType.DMA((n,))],
    )(*arrs)


HBM = pl.BlockSpec(memory_space=pltpu.HBM)
SEM = pl.BlockSpec(memory_space=pltpu.SEMAPHORE)
EFFECT = pltpu.SideEffectType.DATAFLOW_SIDE_EFFECTING


def _hbm(a):
    return pltpu.with_memory_space_constraint(a, pltpu.HBM)


def _gather_peers():
    x, y, c = _position()
    return [(x, y, 1 - c), (1 - x, y, c), (x, 1 - y, c), (1 - x, 1 - y, c)]


def _ag_start(shards, name):
    n = len(shards)
    lands = [_hbm(lax.empty((N_DEV,) + s.shape, s.dtype)) for s in shards]

    def body(*refs):
        srcs, land, send, recv = refs[:n], refs[n:2 * n], refs[2 * n:3 * n], refs[3 * n:4 * n]
        token = refs[6 * n]
        x, y, c = _position()
        for a in range(n):
            for k, to in enumerate(_gather_peers()):
                pltpu.make_async_remote_copy(
                    src_ref=srcs[a], dst_ref=land[a].at[4 * x + 2 * y + c], send_sem=send[a].at[k],
                    recv_sem=recv[a].at[k], device_id=to, device_id_type=MESH).start()
        token[...] = jnp.zeros_like(token)

    sems = [pltpu.SemaphoreType.DMA((4,))] * (2 * n)
    out = pl.pallas_call(
        body, name=name,
        out_shape=sems + [pltpu.HBM(s.shape, s.dtype) for s in shards] + [pltpu.HBM(l.shape, l.dtype) for l in lands]
        + [jax.ShapeDtypeStruct((8, LANES), F32)],
        in_specs=[HBM] * (2 * n), out_specs=[SEM] * (2 * n) + [HBM] * (2 * n) + [pl.BlockSpec(memory_space=pltpu.VMEM)],
        input_output_aliases={**{a: 2 * n + a for a in range(n)}, **{n + a: 3 * n + a for a in range(n)}},
        compiler_params=pltpu.CompilerParams(has_side_effects=EFFECT),
    )(*[_hbm(s) for s in shards], *lands)
    return out[:n], out[n:2 * n], out[2 * n:3 * n], out[3 * n:4 * n], out[4 * n]


def _ag_wait(send, recv, shard_thru, land_thru, after, name):
    def body(v_ref, land_ref, send_sem, recv_sem, after_ref, v_dead, got_ref):
        for k, to in enumerate(_gather_peers()):
            cp = pltpu.make_async_remote_copy(
                src_ref=v_ref, dst_ref=land_ref.at[0], send_sem=send_sem.at[k], recv_sem=recv_sem.at[k],
                device_id=to, device_id_type=MESH)
            cp.wait_send()
            cp.wait_recv()

    return pl.pallas_call(
        body, name=name,
        out_shape=(pltpu.HBM(shard_thru.shape, shard_thru.dtype), pltpu.HBM(land_thru.shape, land_thru.dtype)),
        in_specs=(HBM, HBM, SEM, SEM, ANY), out_specs=(HBM, HBM), input_output_aliases={0: 0, 1: 1},
        compiler_params=pltpu.CompilerParams(has_side_effects=EFFECT),
    )(shard_thru, land_thru, send, recv, after)


def _ag_forward(land, name):
    def body(land_in, land_ref, send_sems, recv_sems):
        x, y, c = _position()
        copies = []
        for j, (px, py) in enumerate([(1 - x, y), (x, 1 - y), (1 - x, 1 - y)]):
            block = land_ref.at[4 * px + 2 * py + c]
            cp = pltpu.make_async_remote_copy(src_ref=block, dst_ref=block, send_sem=send_sems.at[j],
                                              recv_sem=recv_sems.at[j], device_id=(x, y, 1 - c), device_id_type=MESH)
            cp.start()
            copies.append(cp)
        for cp in copies:
            cp.wait()

    return pl.pallas_call(
        body, name=name, out_shape=jax.ShapeDtypeStruct(land.shape, land.dtype),
        in_specs=[ANY], out_specs=ANY, input_output_aliases={0: 0},
        scratch_shapes=[pltpu.SemaphoreType.DMA((3,)), pltpu.SemaphoreType.DMA((3,))],
    )(land)


def _rs_peers():
    x, y, c = _position()
    return [(1 - x, y, c), (x, 1 - y, c), (1 - x, 1 - y, c)]


def _rs_start(chip_sums, name):
    land = _hbm(lax.empty(chip_sums.shape, chip_sums.dtype))

    def body(src, land_ref, send, recv, src_thru, land_thru, token):
        x, y, c = _position()
        for j, (px, py, pc) in enumerate(_rs_peers()):
            pltpu.make_async_remote_copy(
                src_ref=src.at[2 * px + py], dst_ref=land_ref.at[2 * x + y], send_sem=send.at[j], recv_sem=recv.at[j],
                device_id=(px, py, pc), device_id_type=MESH).start()
        token[...] = jnp.zeros_like(token)

    return pl.pallas_call(
        body, name=name,
        out_shape=[pltpu.SemaphoreType.DMA((3,)), pltpu.SemaphoreType.DMA((3,)),
                   pltpu.HBM(chip_sums.shape, chip_sums.dtype), pltpu.HBM(land.shape, land.dtype),
                   jax.ShapeDtypeStruct((8, LANES), F32)],
        in_specs=[HBM, HBM], out_specs=[SEM, SEM, HBM, HBM, pl.BlockSpec(memory_space=pltpu.VMEM)],
        input_output_aliases={0: 2, 1: 3},
        compiler_params=pltpu.CompilerParams(has_side_effects=EFFECT),
    )(_hbm(chip_sums), land)


def _rs_wait(send, recv, src_thru, land_thru, after, name):
    def body(src, land_ref, send_sem, recv_sem, after_ref, src_out, land_out):
        for j, to in enumerate(_rs_peers()):
            cp = pltpu.make_async_remote_copy(
                src_ref=src.at[0], dst_ref=land_ref.at[0], send_sem=send_sem.at[j], recv_sem=recv_sem.at[j],
                device_id=to, device_id_type=MESH)
            cp.wait_send()
            cp.wait_recv()

    return pl.pallas_call(
        body, name=name,
        out_shape=(pltpu.HBM(src_thru.shape, src_thru.dtype), pltpu.HBM(land_thru.shape, land_thru.dtype)),
        in_specs=(HBM, HBM, SEM, SEM, ANY), out_specs=(HBM, HBM), input_output_aliases={0: 0, 1: 1},
        compiler_params=pltpu.CompilerParams(has_side_effects=EFFECT),
    )(src_thru, land_thru, send, recv, after)


def _sib_start(full, name):
    land = _hbm(lax.empty((N_CHIP,) + full.shape[1:], full.dtype))

    def body(src, land_ref, send, recv, src_thru, land_thru, token):
        x, y, c = _position()
        for k in range(N_CHIP):
            pltpu.make_async_remote_copy(
                src_ref=src.at[2 * k + (1 - c)], dst_ref=land_ref.at[k], send_sem=send.at[k], recv_sem=recv.at[k],
                device_id=(x, y, 1 - c), device_id_type=MESH).start()
        token[...] = jnp.zeros_like(token)

    return pl.pallas_call(
        body, name=name,
        out_shape=[pltpu.SemaphoreType.DMA((N_CHIP,)), pltpu.SemaphoreType.DMA((N_CHIP,)),
                   pltpu.HBM(full.shape, full.dtype), pltpu.HBM(land.shape, land.dtype),
                   jax.ShapeDtypeStruct((8, LANES), F32)],
        in_specs=[HBM, HBM], out_specs=[SEM, SEM, HBM, HBM, pl.BlockSpec(memory_space=pltpu.VMEM)],
        input_output_aliases={0: 2, 1: 3},
        compiler_params=pltpu.CompilerParams(has_side_effects=EFFECT),
    )(_hbm(full), land)


def _sib_wait(send, recv, src_thru, land_thru, after, name):
    def body(src, land_ref, send_sem, recv_sem, after_ref, src_out, land_out):
        x, y, c = _position()
        for k in range(N_CHIP):
            cp = pltpu.make_async_remote_copy(
                src_ref=src.at[0], dst_ref=land_ref.at[0], send_sem=send_sem.at[k], recv_sem=recv_sem.at[k],
                device_id=(x, y, 1 - c), device_id_type=MESH)
            cp.wait_send()
            cp.wait_recv()

    return pl.pallas_call(
        body, name=name,
        out_shape=(pltpu.HBM(src_thru.shape, src_thru.dtype), pltpu.HBM(land_thru.shape, land_thru.dtype)),
        in_specs=(HBM, HBM, SEM, SEM, ANY), out_specs=(HBM, HBM), input_output_aliases={0: 0, 1: 1},
        compiler_params=pltpu.CompilerParams(has_side_effects=EFFECT),
    )(src_thru, land_thru, send, recv, after)


def _rs_sibling(arrs, name):
    n = len(arrs)

    def body(*refs):
        ins, outs = refs[:n], refs[n:2 * n]
        send_sems, recv_sems = refs[2 * n:]
        x, y, c = _position()
        copies = []
        for a in range(n):
            for k in range(N_CHIP):
                cp = pltpu.make_async_remote_copy(
                    src_ref=ins[a].at[2 * k + (1 - c)], dst_ref=outs[a].at[k],
                    send_sem=send_sems.at[N_CHIP * a + k], recv_sem=recv_sems.at[N_CHIP * a + k],
                    device_id=(x, y, 1 - c), device_id_type=MESH)
                cp.start()
                copies.append(cp)
        for cp in copies:
            cp.wait()

    return _pcall(
        body, name=name,
        out_shape=[jax.ShapeDtypeStruct((N_CHIP,) + a.shape[1:], a.dtype) for a in arrs],
        in_specs=[ANY] * n, out_specs=[ANY] * n,
        scratch_shapes=[pltpu.SemaphoreType.DMA((N_CHIP * n,)), pltpu.SemaphoreType.DMA((N_CHIP * n,))],
    )(*arrs)


def _chip_sum(full, recv, core, name):
    _, rows, cols = full.shape
    tr = _blk(rows, 1024) if rows % LANES == 0 else rows

    def body(core_ref, a_ref, b_ref, o_ref):
        o_ref[...] = (a_ref[...].astype(F32) + b_ref[...].astype(F32)).astype(o_ref.dtype)

    grid_spec = pltpu.PrefetchScalarGridSpec(
        num_scalar_prefetch=1, grid=(N_CHIP, rows // tr),
        in_specs=[pl.BlockSpec((None, tr, cols), lambda k, i, core_ref: (2 * k + core_ref[0], i, 0)),
                  pl.BlockSpec((None, tr, cols), lambda k, i, core_ref: (k, i, 0))],
        out_specs=pl.BlockSpec((None, tr, cols), lambda k, i, core_ref: (k, i, 0)))
    return _pcall(body, name=name, out_shape=jax.ShapeDtypeStruct((N_CHIP, rows, cols), full.dtype),
                  grid_spec=grid_spec)(core, full, recv)


def _matmul(a, b, *, name, ta=False, tb=False, tm=1024, tn=1024, tk=2048, out_dtypes=(BF16,), epilogue=None,
            row_extras=(), tile_extras=(), out_shape=None, out_map=None, b_sharded=False, n_cols=None, n_off=0,
            after=None):
    m, k = (a.shape[1], a.shape[0]) if ta else a.shape
    if b_sharded:
        shard_c = b.shape[2]
        n, kb = (b.shape[1], N_DEV * shard_c) if tb else (N_DEV * shard_c, b.shape[1])
        tn, tk = (tn, min(tk, shard_c)) if tb else (min(tn, shard_c), tk)
    else:
        n, kb = b.shape if tb else (b.shape[1], b.shape[0])
    assert kb == k, (name, kb, k)
    if n_cols is not None:
        n = n_cols
    tm, tn, tk = _blk(m, tm), _blk(n, tn), _blk(k, tk)
    assert n_off % tn == 0
    nk = k // tk
    dims = (((0 if ta else 1,), (1 if tb else 0,)), ((), ()))
    behind = () if after is None else (after,)
    n_row, n_tile, n_out = len(row_extras), len(tile_extras), len(out_dtypes)
    first_out = 2 + n_row + n_tile + len(behind)

    def body(*refs):
        a_ref, b_ref = refs[:2]
        extras = refs[2:2 + n_row + n_tile]
        outs = refs[first_out:first_out + n_out]
        acc_ref = refs[-1]
        jj, kk = pl.program_id(1), pl.program_id(2)
        part = lax.dot_general(a_ref[...].astype(BF16), b_ref[...].astype(BF16), dims, preferred_element_type=F32)

        def finish(acc):
            res = (acc,) if epilogue is None else epilogue(acc, jj, *[e[...] for e in extras])
            for o_ref, r in zip(outs, res):
                o_ref[...] = r.astype(o_ref.dtype)

        if nk == 1:
            finish(part)
        else:
            @pl.when(kk == 0)
            def _():
                acc_ref[...] = part

            @pl.when(kk > 0)
            def _():
                acc_ref[...] += part

            @pl.when(kk == nk - 1)
            def _():
                finish(acc_ref[...])

    a_spec = pl.BlockSpec((tk, tm), lambda i, j, kk: (kk, i)) if ta else pl.BlockSpec((tm, tk), lambda i, j, kk: (i, kk))
    if b_sharded and tb:
        per = shard_c // tk
        b_spec = pl.BlockSpec((None, tn, tk), lambda i, j, kk: (kk // per, j, kk % per))
    elif b_sharded:
        per = shard_c // tn
        b_spec = pl.BlockSpec((None, tk, tn), lambda i, j, kk: (j // per, kk, j % per))
    elif tb:
        b_spec = pl.BlockSpec((tn, tk), lambda i, j, kk: (j + n_off // tn, kk))
    else:
        b_spec = pl.BlockSpec((tk, tn), lambda i, j, kk: (kk, j + n_off // tn))
    in_specs = [a_spec, b_spec]
    in_specs += [pl.BlockSpec((tm, LANES), lambda i, j, kk: (i, 0)) for _ in row_extras]
    in_specs += [pl.BlockSpec((tm, tn), lambda i, j, kk: (i, j)) for _ in tile_extras]
    in_specs += [ANY for _ in behind]
    if out_map is None:
        out_specs = [pl.BlockSpec((tm, tn), lambda i, j, kk: (i, j)) for _ in out_dtypes]
        shapes = [jax.ShapeDtypeStruct((m, n), dt) for dt in out_dtypes]
    else:
        out_specs = [out_map(tm, tn)]
        shapes = [jax.ShapeDtypeStruct(out_shape, out_dtypes[0])]
    acc_shape = (tm, tn) if nk > 1 else (8, LANES)
    res = _pcall(body, name=name, out_shape=shapes, grid=(m // tm, n // tn, nk), in_specs=in_specs,
                 out_specs=out_specs, scratch_shapes=[pltpu.VMEM(acc_shape, F32)])(
                     a, b, *row_extras, *tile_extras, *behind)
    return res[0] if n_out == 1 else res


def _rope_cols(acc, j, cos, sin, n_rope):
    width = acc.shape[1]
    parts = []
    for g in range(width // HEAD_DIM):
        xg = acc[:, g * HEAD_DIM:(g + 1) * HEAD_DIM]
        roped = xg * cos + pltpu.roll(xg, HEAD_DIM // 2, 1) * sin
        parts.append(jnp.where(j * width + g * HEAD_DIM < n_rope, roped, xg))
    return jnp.concatenate(parts, axis=1) if len(parts) > 1 else parts[0]


def _silu(v):
    return v / (1.0 + jnp.exp(-v))


def _mod_part(c_all, w_mod, b_part):
    d, w = w_mod.shape
    tk = _blk(d, 512)

    def body(c_ref, w_ref, b_ref, o_ref):
        kk = pl.program_id(0)
        cond = _silu(c_ref[...]).astype(BF16)
        part = jnp.dot(cond, w_ref[...].astype(BF16), preferred_element_type=F32)

        @pl.when(kk == 0)
        def _():
            o_ref[...] = part + b_ref[...]

        @pl.when(kk > 0)
        def _():
            o_ref[...] += part

    return _pcall(body, name="mod_part", out_shape=jax.ShapeDtypeStruct((N_DEV, w), F32), grid=(d // tk,),
                  in_specs=[pl.BlockSpec((N_DEV, tk), lambda kk: (0, kk)), pl.BlockSpec((tk, w), lambda kk: (kk, 0)),
                            pl.BlockSpec((1, w), lambda kk: (0, 0))],
                  out_specs=pl.BlockSpec((N_DEV, w), lambda kk: (0, 0)))(c_all, w_mod, b_part)


def _row_call(body, name, t, d, tiled_in, vec_in, tiled_out_dtypes, n_vec_out, tr=256):
    tr = _blk(t, tr)
    tile = pl.BlockSpec((tr, d), lambda i: (i, 0))
    vec = pl.BlockSpec((1, d), lambda i: (0, 0))
    out_shape = [jax.ShapeDtypeStruct((t, d), dt) for dt in tiled_out_dtypes]
    out_shape += [jax.ShapeDtypeStruct((1, d), F32)] * n_vec_out
    return _pcall(body, name=name, out_shape=out_shape, grid=(t // tr,),
                  in_specs=[tile] * len(tiled_in) + [vec] * len(vec_in),
                  out_specs=[tile] * len(tiled_out_dtypes) + [vec] * n_vec_out)(*tiled_in, *vec_in)


def _accumulate(ref, val):
    @pl.when(pl.program_id(0) == 0)
    def _():
        ref[...] = val

    @pl.when(pl.program_id(0) > 0)
    def _():
        ref[...] += val


def _rsum(v):
    return jnp.sum(v, axis=0, keepdims=True)


def _rms(v):
    return lax.rsqrt(jnp.mean(v * v, axis=-1, keepdims=True) + NORM_EPS)


def _rms_bwd(vhat, r, dvhat):
    return r * (dvhat - vhat * jnp.mean(dvhat * vhat, axis=-1, keepdims=True))


def _pre_attn(x, g0, sc_a, sh_a):
    def body(x_ref, g_ref, sc_ref, sh_ref, h_ref):
        xv = x_ref[...]
        h_ref[...] = (xv * _rms(xv) * g_ref[...] * (1.0 + sc_ref[...]) + sh_ref[...]).astype(BF16)

    t, d = x.shape
    return _row_call(body, "pre_attn", t, d, [x], [g0, sc_a, sh_a], [BF16], 0)[0]


def _post_mix(x, mix, gt_a, g1, g2, sc_m, sh_m):
    def body(x_ref, mix_ref, gt_ref, g1_ref, g2_ref, sc_ref, sh_ref, x2_ref, h2_ref):
        mv = mix_ref[...]
        x2 = x_ref[...] + gt_ref[...] * (mv * _rms(mv) * g1_ref[...])
        x2_ref[...] = x2
        h2_ref[...] = (x2 * _rms(x2) * g2_ref[...] * (1.0 + sc_ref[...]) + sh_ref[...]).astype(BF16)

    t, d = x.shape
    return _row_call(body, "post_mix", t, d, [x, mix], [gt_a, g1, g2, sc_m, sh_m], [F32, BF16], 0)


def _final(y, x2, target, gt_m, g3):
    t, d = y.shape

    def body(y_ref, x2_ref, tg_ref, gt_ref, g3_ref, dy_ref, dout_ref, dgt_ref, dg3_ref, loss_ref):
        yv = y_ref[...]
        r = _rms(yv)
        yhat = yv * r
        n3 = yhat * g3_ref[...]
        err = x2_ref[...] + gt_ref[...] * n3 - tg_ref[...]
        _accumulate(loss_ref, jnp.zeros((1, d), F32) + 0.5 * jnp.sum(err * err) / d)
        dout = err * (1.0 / d)
        dout_ref[...] = dout
        _accumulate(dgt_ref, _rsum(dout * n3))
        dn3 = dout * gt_ref[...]
        _accumulate(dg3_ref, _rsum(dn3 * yhat))
        dy_ref[...] = _rms_bwd(yhat, r, dn3 * g3_ref[...]).astype(BF16)

    return _row_call(body, "final", t, d, [y, x2, target], [gt_m, g3], [BF16, F32], 3)


def _mid_bwd(dh2, dout, x2, mix, g2, sc_m, gt_a, g1):
    t, d = x2.shape

    def body(dh2_ref, dout_ref, x2_ref, mix_ref, g2_ref, sc_ref, gt_ref, g1_ref,
             dmix_ref, dx2_ref, dsh_ref, dsc_ref, dg2_ref, dgt_ref, dg1_ref):
        dh2v = dh2_ref[...]
        x2v = x2_ref[...]
        r2 = _rms(x2v)
        x2hat = x2v * r2
        _accumulate(dsh_ref, _rsum(dh2v))
        _accumulate(dsc_ref, _rsum(dh2v * (x2hat * g2_ref[...])))
        dn2 = dh2v * (1.0 + sc_ref[...])
        _accumulate(dg2_ref, _rsum(dn2 * x2hat))
        dx2 = dout_ref[...] + _rms_bwd(x2hat, r2, dn2 * g2_ref[...])
        dx2_ref[...] = dx2
        mv = mix_ref[...]
        r1 = _rms(mv)
        mhat = mv * r1
        _accumulate(dgt_ref, _rsum(dx2 * (mhat * g1_ref[...])))
        dn1 = dx2 * gt_ref[...]
        _accumulate(dg1_ref, _rsum(dn1 * mhat))
        dmix_ref[...] = _rms_bwd(mhat, r1, dn1 * g1_ref[...]).astype(BF16)

    return _row_call(body, "mid_bwd", t, d, [dh2, dout, x2, mix], [g2, sc_m, gt_a, g1], [BF16, F32], 5)


def _x_bwd(dh1, dx2, x, g0, sc_a):
    t, d = x.shape

    def body(dh1_ref, dx2_ref, x_ref, g0_ref, sc_ref, dx_ref, dsh_ref, dsc_ref, dg0_ref):
        dh1v = dh1_ref[...]
        xv = x_ref[...]
        r0 = _rms(xv)
        xhat = xv * r0
        _accumulate(dsh_ref, _rsum(dh1v))
        _accumulate(dsc_ref, _rsum(dh1v * (xhat * g0_ref[...])))
        dn0 = dh1v * (1.0 + sc_ref[...])
        _accumulate(dg0_ref, _rsum(dn0 * xhat))
        dx_ref[...] = dx2_ref[...] + _rms_bwd(xhat, r0, dn0 * g0_ref[...])

    return _row_call(body, "x_bwd", t, d, [dh1, dx2, x], [g0, sc_a], [F32], 3)


def _pick_lane(block, h):
    lane = lax.broadcasted_iota(jnp.int32, block.shape, 1)
    return jnp.sum(jnp.where(lane == h, block, 0.0), axis=1, keepdims=True)


def _put_lane(ref, rows, h, col):
    old = ref[rows, :]
    lane = lax.broadcasted_iota(jnp.int32, old.shape, 1)
    ref[rows, :] = jnp.where(lane == h, col, old)


def _tri(n, lower):
    r = lax.broadcasted_iota(jnp.int32, (n, n), 0)
    c = lax.broadcasted_iota(jnp.int32, (n, n), 1)
    return jnp.where((c <= r) if lower else (c >= r), 1.0, 0.0).astype(F32)


def _cum_fwd(fg, b128):
    t = fg.shape[0]
    nb = t // LANES

    def body(fg_ref, b_ref, cum_ref, cumt_ref):
        tri = _tri(LANES, True)
        carry = jnp.zeros((1, LANES), F32)
        for i in range(nb):
            z = fg_ref[i * LANES:(i + 1) * LANES, :] + b_ref[...]
            lf = jnp.minimum(z, 0.0) - jnp.log(1.0 + jnp.exp(-jnp.abs(z)))
            blk = jnp.dot(tri, lf, precision=lax.Precision.HIGHEST, preferred_element_type=F32) + carry
            cum_ref[i * LANES:(i + 1) * LANES, :] = blk
            carry = blk[LANES - 1:LANES, :]
        cumt_ref[...] = cum_ref[...].T[0:N_FOX, :]

    return _pcall(body, name="cum_fwd",
                  out_shape=[jax.ShapeDtypeStruct((t, LANES), F32), jax.ShapeDtypeStruct((N_FOX, t), F32)],
                  grid=(1,),
                  in_specs=[pl.BlockSpec((t, LANES), lambda i: (0, 0)), pl.BlockSpec((1, LANES), lambda i: (0, 0))],
                  out_specs=[pl.BlockSpec((t, LANES), lambda i: (0, 0)), pl.BlockSpec((N_FOX, t), lambda i: (0, 0))],
                  )(fg, b128)


def _fg_bwd(dcs_rows, fg, b128):
    t = fg.shape[0]
    nb = t // LANES

    def body(dcs_ref, fg_ref, b_ref, dfg_ref, db_ref, dcum_ref):
        dcum_ref[...] = -jnp.concatenate([dcs_ref[...], jnp.zeros((LANES - N_FOX, t), F32)], axis=0).T
        tri = _tri(LANES, False)
        carry = jnp.zeros((1, LANES), F32)
        db = jnp.zeros((1, LANES), F32)
        for i in reversed(range(nb)):
            rows = slice(i * LANES, (i + 1) * LANES)
            dlf = jnp.dot(tri, dcum_ref[rows, :], precision=lax.Precision.HIGHEST, preferred_element_type=F32) + carry
            carry = dlf[0:1, :]
            z = fg_ref[rows, :] + b_ref[...]
            dfg = dlf / (1.0 + jnp.exp(z))
            dfg_ref[rows, :] = dfg.astype(BF16)
            db = db + _rsum(dfg)
        db_ref[...] = db

    full = pl.BlockSpec((t, LANES), lambda i: (0, 0))
    vec = pl.BlockSpec((1, LANES), lambda i: (0, 0))
    return _pcall(body, name="fg_bwd",
                  out_shape=[jax.ShapeDtypeStruct((t, LANES), BF16), jax.ShapeDtypeStruct((1, LANES), F32)],
                  grid=(1,), in_specs=[pl.BlockSpec((N_FOX, t), lambda i: (0, 0)), full, vec], out_specs=[full, vec],
                  scratch_shapes=[pltpu.VMEM((t, LANES), F32)])(dcs_rows, fg, b128)


def _head_spec(t, col0, div=1):
    return pl.BlockSpec((t, HEAD_DIM), lambda h: (0, col0 + h // div))


def _fox_scores(q, k, cq, ck, i, tq, end):
    s = lax.dot_general(q, k, NT_DIMS, preferred_element_type=F32) * ATT_SCALE + cq - ck
    row = lax.broadcasted_iota(jnp.int32, (tq, end), 0) + i * tq
    col = lax.broadcasted_iota(jnp.int32, (tq, end), 1)
    return jnp.where(row >= col, s, -jnp.inf)


def _fox_fwd(proj_a, cum, cumt):
    t = proj_a.shape[0]
    tq = _blk(t, 512)
    nq = t // tq

    def body(q_ref, k_ref, v_ref, cum_ref, cumt_ref, o_ref, lse_ref):
        h = pl.program_id(0)
        cq_all = _pick_lane(cum_ref[...], h)
        ck_all = cumt_ref[pl.ds(h, 1), :]

        @pl.when(h == 0)
        def _():
            lse_ref[...] = jnp.zeros_like(lse_ref)

        for i in range(nq):
            rows, end = slice(i * tq, (i + 1) * tq), (i + 1) * tq
            s = _fox_scores(q_ref[rows, :], k_ref[0:end, :], cq_all[rows, :], ck_all[:, 0:end], i, tq, end)
            m = jnp.max(s, axis=1, keepdims=True)
            p = jnp.exp(s - m)
            l = jnp.sum(p, axis=1, keepdims=True)
            o = jnp.dot(p.astype(BF16), v_ref[0:end, :], preferred_element_type=F32) / l
            o_ref[rows, :] = o.astype(BF16)
            _put_lane(lse_ref, rows, h, m + jnp.log(l))

    nh = FOX_W // HEAD_DIM
    stat = pl.BlockSpec((t, LANES), lambda h: (0, 0))
    return _pcall(body, name="fox_fwd",
                  out_shape=[jax.ShapeDtypeStruct((t, FOX_W), BF16), jax.ShapeDtypeStruct((t, LANES), F32)],
                  grid=(N_FOX,),
                  in_specs=[_head_spec(t, 0), _head_spec(t, nh), _head_spec(t, 2 * nh), stat,
                            pl.BlockSpec((N_FOX, t), lambda h: (0, 0))],
                  out_specs=[_head_spec(t, 0), stat])(proj_a, proj_a, proj_a, cum, cumt)


def _fox_bwd(proj_a, d_attn, cum, cumt, lse):
    t = proj_a.shape[0]
    tq = _blk(t, 512)
    nq = t // tq

    def body(q_ref, k_ref, v_ref, do_ref, cum_ref, cumt_ref, lse_ref,
             dq_ref, dk_ref, dv_ref, dcs_ref, dk_acc, dv_acc, dcs_acc):
        h = pl.program_id(0)
        cq_all = _pick_lane(cum_ref[...], h)
        ck_all = cumt_ref[pl.ds(h, 1), :]
        lse_all = _pick_lane(lse_ref[...], h)
        dk_acc[...] = jnp.zeros_like(dk_acc)
        dv_acc[...] = jnp.zeros_like(dv_acc)
        dcs_acc[...] = jnp.zeros_like(dcs_acc)
        for i in range(nq):
            rows, end = slice(i * tq, (i + 1) * tq), (i + 1) * tq
            q, k, v, do = q_ref[rows, :], k_ref[0:end, :], v_ref[0:end, :], do_ref[rows, :]
            s = _fox_scores(q, k, cq_all[rows, :], ck_all[:, 0:end], i, tq, end)
            p = jnp.exp(s - lse_all[rows, :])
            dp = lax.dot_general(do, v, NT_DIMS, preferred_element_type=F32)
            ds = p * (dp - jnp.sum(p * dp, axis=1, keepdims=True))
            dcs_acc[:, 0:end] += jnp.sum(ds, axis=0, keepdims=True)
            ds = ds.astype(BF16)
            dq_ref[rows, :] = (jnp.dot(ds, k, preferred_element_type=F32) * ATT_SCALE).astype(BF16)
            dk_acc[0:end, :] += lax.dot_general(ds, q, TN_DIMS, preferred_element_type=F32)
            dv_acc[0:end, :] += lax.dot_general(p.astype(BF16), do, TN_DIMS, preferred_element_type=F32)
        dk_ref[...] = (dk_acc[...] * ATT_SCALE).astype(BF16)
        dv_ref[...] = dv_acc[...].astype(BF16)
        dcs_ref[pl.ds(h, 1), :] = dcs_acc[...]

    nh = FOX_W // HEAD_DIM
    stat = pl.BlockSpec((t, LANES), lambda h: (0, 0))
    rows8 = pl.BlockSpec((N_FOX, t), lambda h: (0, 0))
    head = _head_spec(t, 0)
    wide = jax.ShapeDtypeStruct((t, FOX_W), BF16)
    return _pcall(body, name="fox_bwd",
                  out_shape=[wide, wide, wide, jax.ShapeDtypeStruct((N_FOX, t), F32)],
                  grid=(N_FOX,),
                  in_specs=[_head_spec(t, 0), _head_spec(t, nh), _head_spec(t, 2 * nh), head, stat, rows8, stat],
                  out_specs=[head, head, head, rows8],
                  scratch_shapes=[pltpu.VMEM((t, HEAD_DIM), F32), pltpu.VMEM((t, HEAD_DIM), F32),
                                  pltpu.VMEM((1, t), F32)],
                  )(proj_a, proj_a, proj_a, d_attn, cum, cumt, lse)


def _swa_scores(q, k, i, tq, start, end):
    s = lax.dot_general(q, k, NT_DIMS, preferred_element_type=F32) * ATT_SCALE
    row = lax.broadcasted_iota(jnp.int32, (tq, end - start), 0) + i * tq
    col = lax.broadcasted_iota(jnp.int32, (tq, end - start), 1) + start
    diff = row - col
    return jnp.where((diff >= 0) & (diff < WINDOW), s, -jnp.inf)


def _swa_blocks(t):
    tq = _blk(t, 256)
    return tq, [(i, max(0, i * tq - WINDOW), (i + 1) * tq) for i in range(t // tq)]


def _swa_fwd(proj_b, sinks128):
    t = proj_b.shape[0]
    tq, blocks = _swa_blocks(t)

    def body(q_ref, k_ref, v_ref, sink_ref, o_ref, lse_ref):
        h = pl.program_id(0)
        sink = _pick_lane(sink_ref[...], h)

        @pl.when(h == 0)
        def _():
            lse_ref[...] = jnp.zeros_like(lse_ref)

        for i, start, end in blocks:
            rows = slice(i * tq, end)
            s = _swa_scores(q_ref[rows, :], k_ref[start:end, :], i, tq, start, end)
            m = jnp.maximum(jnp.max(s, axis=1, keepdims=True), sink)
            p = jnp.exp(s - m)
            l = jnp.sum(p, axis=1, keepdims=True) + jnp.exp(sink - m)
            o = jnp.dot(p.astype(BF16), v_ref[start:end, :], preferred_element_type=F32) / l
            o_ref[rows, :] = o.astype(BF16)
            _put_lane(lse_ref, rows, h, m + jnp.log(l))

    stat = pl.BlockSpec((t, LANES), lambda h: (0, 0))
    return _pcall(body, name="swa_fwd",
                  out_shape=[jax.ShapeDtypeStruct((t, SWA_W), BF16), jax.ShapeDtypeStruct((t, LANES), F32)],
                  grid=(N_SWA,),
                  in_specs=[_head_spec(t, 0), _head_spec(t, N_SWA, GQA), _head_spec(t, N_SWA + N_KV, GQA),
                            pl.BlockSpec((1, LANES), lambda h: (0, 0))],
                  out_specs=[_head_spec(t, 0), stat])(proj_b, proj_b, proj_b, sinks128)


def _rope_bwd(d, cos, sin):
    return d * cos + pltpu.roll(d * sin, HEAD_DIM // 2, 1)


def _swa_bwd(proj_b, d_attn, lse, sinks128, cos, sin):
    t = proj_b.shape[0]
    tq, blocks = _swa_blocks(t)

    def body(q_ref, k_ref, v_ref, do_ref, lse_ref, sink_ref, cos_ref, sin_ref,
             dq_ref, dk_ref, dv_ref, dsink_ref, dk_acc, dv_acc):
        h = pl.program_id(0)
        sink = _pick_lane(sink_ref[...], h)
        lse_all = _pick_lane(lse_ref[...], h)

        @pl.when(h == 0)
        def _():
            dsink_ref[...] = jnp.zeros_like(dsink_ref)

        @pl.when(h % GQA == 0)
        def _():
            dk_acc[...] = jnp.zeros_like(dk_acc)
            dv_acc[...] = jnp.zeros_like(dv_acc)

        dsink = jnp.zeros((1, 1), F32)
        for i, start, end in blocks:
            rows = slice(i * tq, end)
            q, k, v, do = q_ref[rows, :], k_ref[start:end, :], v_ref[start:end, :], do_ref[rows, :]
            s = _swa_scores(q, k, i, tq, start, end)
            p = jnp.exp(s - lse_all[rows, :])
            dp = lax.dot_general(do, v, NT_DIMS, preferred_element_type=F32)
            delta = jnp.sum(p * dp, axis=1, keepdims=True)
            ds = (p * (dp - delta)).astype(BF16)
            dq = jnp.dot(ds, k, preferred_element_type=F32) * ATT_SCALE
            dq_ref[rows, :] = _rope_bwd(dq, cos_ref[rows, :], sin_ref[rows, :]).astype(BF16)
            dk_acc[start:end, :] += lax.dot_general(ds, q, TN_DIMS, preferred_element_type=F32)
            dv_acc[start:end, :] += lax.dot_general(p.astype(BF16), do, TN_DIMS, preferred_element_type=F32)
            dsink = dsink - jnp.sum(jnp.exp(sink - lse_all[rows, :]) * delta, axis=0, keepdims=True)
        old = dsink_ref[...]
        lane = lax.broadcasted_iota(jnp.int32, old.shape, 1)
        dsink_ref[...] = jnp.where(lane == h, dsink, old)

        @pl.when(h % GQA == GQA - 1)
        def _():
            dk_ref[...] = _rope_bwd(dk_acc[...] * ATT_SCALE, cos_ref[...], sin_ref[...]).astype(BF16)
            dv_ref[...] = dv_acc[...].astype(BF16)

    stat = pl.BlockSpec((t, LANES), lambda h: (0, 0))
    vec = pl.BlockSpec((1, LANES), lambda h: (0, 0))
    head = _head_spec(t, 0)
    kv_out = _head_spec(t, 0, GQA)
    return _pcall(body, name="swa_bwd",
                  out_shape=[jax.ShapeDtypeStruct((t, SWA_W), BF16), jax.ShapeDtypeStruct((t, KV_W), BF16),
                             jax.ShapeDtypeStruct((t, KV_W), BF16), jax.ShapeDtypeStruct((1, LANES), F32)],
                  grid=(N_SWA,),
                  in_specs=[head, _head_spec(t, N_SWA, GQA), _head_spec(t, N_SWA + N_KV, GQA),
                            _head_spec(t, N_FOX), stat, vec, stat, stat],
                  out_specs=[head, kv_out, kv_out, vec],
                  scratch_shapes=[pltpu.VMEM((t, HEAD_DIM), F32), pltpu.VMEM((t, HEAD_DIM), F32)],
                  )(proj_b, proj_b, proj_b, d_attn, lse, sinks128, cos, sin)


def _adamw(w, g, m, v):
    m = ADAM_B1 * m + (1.0 - ADAM_B1) * g
    v = ADAM_B2 * v + (1.0 - ADAM_B2) * (g * g)
    m_hat = m / (1.0 - ADAM_B1 ** ADAM_STEP)
    v_hat = v / (1.0 - ADAM_B2 ** ADAM_STEP)
    delta = -ADAM_LR * (m_hat / (jnp.sqrt(v_hat) + ADAM_EPS) + ADAM_WD * w)
    return delta, m, v


def _adam_pieces(w, m, v, own, land, idx, name):
    rows, cols = w.shape
    tr, tc = (256, cols) if rows % 256 == 0 else (rows, _blk(cols, 512))

    def body(idx_ref, own_ref, l1_ref, l2_ref, l3_ref, w_ref, m_ref, v_ref, g_ref, d_ref, mo_ref, vo_ref):
        g = own_ref[...].astype(F32) + l1_ref[...].astype(F32) + l2_ref[...].astype(F32) + l3_ref[...].astype(F32)
        g_ref[...] = g
        d_ref[...], mo_ref[...], vo_ref[...] = _adamw(w_ref[...], g, m_ref[...], v_ref[...])

    def piece(p):
        return pl.BlockSpec((None, tr, tc), lambda i, j, idx_ref: (idx_ref[p], i, j))

    tile = pl.BlockSpec((tr, tc), lambda i, j, idx_ref: (i, j))
    out = jax.ShapeDtypeStruct((rows, cols), F32)
    grid_spec = pltpu.PrefetchScalarGridSpec(
        num_scalar_prefetch=1, grid=(rows // tr, cols // tc),
        in_specs=[piece(0), piece(1), piece(2), piece(3), tile, tile, tile], out_specs=[tile] * 4)
    return _pcall(body, name=name, out_shape=[out] * 4, grid_spec=grid_spec)(idx, own, land, land, land, w, m, v)


def _adam_mod(c_all, dmod_cols, w, m, v):
    rows, cols = w.shape
    tr = _blk(rows, 256)

    def body(c_ref, dm_ref, w_ref, m_ref, v_ref, g_ref, d_ref, mo_ref, vo_ref):
        cond = _silu(c_ref[...]).astype(BF16)
        g = lax.dot_general(cond, dm_ref[...].astype(BF16), TN_DIMS, preferred_element_type=F32)
        g_ref[...] = g
        d_ref[...], mo_ref[...], vo_ref[...] = _adamw(w_ref[...], g, m_ref[...], v_ref[...])

    tile = pl.BlockSpec((tr, cols), lambda i: (i, 0))
    out = jax.ShapeDtypeStruct((rows, cols), F32)
    return _pcall(body, name="adam_mod", out_shape=[out] * 4, grid=(rows // tr,),
                  in_specs=[pl.BlockSpec((N_DEV, tr), lambda i: (0, i)), pl.BlockSpec((N_DEV, cols), lambda i: (0, 0)),
                            tile, tile, tile],
                  out_specs=[tile] * 4)(c_all, dmod_cols, w, m, v)


def _adam_small(parts, w, m, v):
    nv = w.shape[1]

    def body(p_ref, w_ref, m_ref, v_ref, g_ref, d_ref, mo_ref, vo_ref):
        g = p_ref[0:1, :]
        for k in range(1, N_DEV):
            g = g + p_ref[k:k + 1, :]
        g_ref[...] = g
        d_ref[...], mo_ref[...], vo_ref[...] = _adamw(w_ref[...], g, m_ref[...], v_ref[...])

    vec = pl.BlockSpec((1, nv), lambda i: (0, 0))
    out = jax.ShapeDtypeStruct((1, nv), F32)
    return _pcall(body, name="adam_small", out_shape=[out] * 4, grid=(1,),
                  in_specs=[pl.BlockSpec((N_DEV, nv), lambda i: (0, 0)), vec, vec, vec],
                  out_specs=[vec] * 4)(parts, w, m, v)


def _pad_lanes(v, width=LANES):
    return jnp.pad(v, ((0, 0), (0, width - v.shape[1])))


def kernel(x, c, w_mod, b_mod, g_pre_mix, g_post_mix, w_in, b_forget, swa_sinks, w_out, g_pre_mlp, g_post_mlp, w_up, w_down, loss_target, m_w_mod, m_b_mod, m_g_pre_mix, m_g_post_mix, m_w_in, m_b_forget, m_swa_sinks, m_w_out, m_g_pre_mlp, m_g_post_mlp, m_w_up, m_w_down, v_w_mod, v_b_mod, v_g_pre_mix, v_g_post_mix, v_w_in, v_b_forget, v_swa_sinks, v_w_out, v_g_pre_mlp, v_g_post_mlp, v_w_up, v_w_down):
    ax, ay, ac = _position()
    me = 4 * ax + 2 * ay + ac
    x, target = x[0], loss_target[0]
    t, d = x.shape
    w_mod, w_in, w_out, w_up, w_down = w_mod[0], w_in[0], w_out[0], w_up[0], w_down[0]
    mod_w = w_mod.shape[1]
    in_w = w_in.shape[1]
    in_total = N_DEV * in_w
    shard_ff = w_up.shape[1]
    n_fox3 = 3 * FOX_W
    n_swa3 = SWA_W + 2 * KV_W
    assert in_total == n_fox3 + N_FOX + n_swa3 and d == FOX_W + SWA_W

    c_all = _all_gather([c], "gather_c")[0].reshape(N_DEV, d)
    b_part = lax.dynamic_slice(b_mod, (0, me * mod_w), (1, mod_w))
    mod_parts = _all_gather([_mod_part(c_all, w_mod, b_part)], "gather_mod")[0]
    mod = lax.dynamic_index_in_dim(mod_parts, me, axis=1, keepdims=False).reshape(1, N_DEV * mod_w)

    w_in_b, mod = lax.optimization_barrier((w_in.astype(BF16), mod))
    first = _ag_start([w_in_b], "ag_start_in")
    behind_first = first[4][0, 0]
    rest = _ag_start([(w + behind_first).astype(BF16) for w in (w_out, w_up, w_down)], "ag_start_rest")
    ag_send, ag_recv, ag_shard, ag_land = [a + b for a, b in zip(first[:4], rest[:4])]
    ag_token = rest[4]

    def gathered(i, after, name):
        shard, land = _ag_wait(ag_send[i], ag_recv[i], ag_shard[i], ag_land[i], after, "ag_wait_" + name)
        return lax.dynamic_update_slice(_ag_forward(land, "ag_fwd_" + name), shard[None], (me, 0, 0))

    sh_a, sc_a, gt_a, sh_m, sc_m, gt_m = [mod[:, i * d:(i + 1) * d] for i in range(6)]

    half = HEAD_DIM // 2
    inv_freq = 1.0 / (ROPE_THETA ** (jnp.arange(half, dtype=F32) * (2.0 / HEAD_DIM)))
    ang = jnp.arange(t).astype(F32)[:, None] * inv_freq[None, :]
    cos = jnp.concatenate([jnp.cos(ang), jnp.cos(ang)], axis=1)
    sin = jnp.concatenate([-jnp.sin(ang), jnp.sin(ang)], axis=1)

    b128 = _pad_lanes(b_forget)
    sinks128 = _pad_lanes(swa_sinks)

    h1 = _pre_attn(x, g_pre_mix + ag_token[0:1, 0:1], sc_a, sh_a)
    w_in_g = gathered(0, h1, "in")
    o_fg, o_sq = n_fox3, n_fox3 + N_FOX

    def cols(lo, hi):
        parts = []
        for j in range(lo // in_w, (hi - 1) // in_w + 1):
            parts.append(w_in_g[j, :, max(lo - j * in_w, 0):min(hi - j * in_w, in_w)])
        return parts

    w_in_r = jnp.concatenate(cols(0, o_fg) + cols(o_sq, in_total) + cols(o_fg, o_sq)
                             + [jnp.zeros((d, FG_PAD - N_FOX), BF16)], axis=1)
    proj_a = _matmul(h1, w_in_r, name="proj_a", n_cols=n_fox3, n_off=0)
    proj_b = _matmul(h1, w_in_r, name="proj_b", n_cols=n_swa3, n_off=n_fox3, tn=512, row_extras=(cos, sin),
                     epilogue=lambda acc, j, cs, sn: (_rope_cols(acc, j, cs, sn, SWA_W + KV_W),))
    fg = _matmul(h1, w_in_r, name="proj_fg", n_cols=FG_PAD, n_off=n_fox3 + n_swa3, tn=FG_PAD,
                 out_dtypes=(F32,))[:, 0:LANES]
    cum, cumt = _cum_fwd(fg, b128)
    fox_o, fox_lse = _fox_fwd(proj_a, cum, cumt)
    swa_o, swa_lse = _swa_fwd(proj_b, sinks128)
    w_out_full = gathered(1, swa_o, "out").reshape(d, d)
    attn = jnp.concatenate([fox_o, swa_o], axis=1)
    mix = _matmul(attn, w_out_full, name="out_proj", out_dtypes=(F32,))
    x2, h2 = _post_mix(x, mix, gt_a, g_post_mix, g_pre_mlp, sc_m, sh_m)
    w_up_g = gathered(2, h2, "up")
    u, act = _matmul(h2, w_up_g, name="mlp_up", b_sharded=True, out_dtypes=(BF16, BF16),
                     epilogue=lambda acc, j: (acc, jnp.square(jnp.maximum(acc, 0.0))))
    w_down_full = gathered(3, act, "down").reshape(N_DEV * shard_ff, d)
    y = _matmul(act, w_down_full, name="mlp_down", out_dtypes=(F32,))

    core = jnp.reshape(ac, (1,)).astype(jnp.int32)

    def reduce_start(started, after, name):
        send, recv, src, land, _ = started
        full, from_sibling = _sib_wait(send, recv, src, land, after, "sib_wait_" + name)
        return _rs_start(_chip_sum(full, from_sibling, core, "chip_sum_" + name), "rs_start_" + name)

    def tok(started):
        return started[4][0:1, 0:1]

    idx = jnp.stack([2 * ax + ay, 2 * (1 - ax) + ay, 2 * ax + (1 - ay), 2 * (1 - ax) + (1 - ay)]).astype(jnp.int32)

    def reduce_finish(started, after, w, m, v, name):
        send, recv, src, land, _ = started
        own, landed = _rs_wait(send, recv, src, land, after, "rs_wait_" + name)
        return _adam_pieces(w, m[0], v[0], own, landed, idx, "adam_" + name)

    dy, dout, dgt_m, dg3, loss_vec = _final(y, x2, target, gt_m, g_post_mlp)
    du = _matmul(dy, w_down_full, name="d_act", tb=True, tile_extras=(u,),
                 epilogue=lambda acc, j, uu: (acc * (2.0 * jnp.maximum(uu.astype(F32), 0.0)),))
    dw_down = _matmul(act, dy, name="dw_down", ta=True)
    sb_down = _sib_start(dw_down.reshape(N_DEV, shard_ff, d), "sib_start_down")
    dh2 = _matmul(du, w_up_g, name="d_h2", tb=True, b_sharded=True, out_dtypes=(F32,), after=sb_down[4])
    rs_down = reduce_start(sb_down, dh2, "down")
    per = shard_ff // _blk(shard_ff, 1024)
    dw_up = _matmul(h2, du, name="dw_up", ta=True, tn=_blk(shard_ff, 1024), out_shape=(N_DEV, d, shard_ff),
                    out_map=lambda tm, tn: pl.BlockSpec((None, tm, tn), lambda i, j, kk: (j // per, i, j % per)),
                    after=rs_down[4])
    sb_up = _sib_start(dw_up, "sib_start_up")
    dmix, dx2, dsh_m, dsc_m, dg2, dgt_a, dg1 = _mid_bwd(
        dh2, dout, x2, mix, g_pre_mlp + tok(sb_up), sc_m, gt_a, g_post_mix)
    d_attn = _matmul(dmix, w_out_full, name="d_attn", tb=True)
    rs_up = reduce_start(sb_up, d_attn, "up")
    dw_out = _matmul(attn, dmix, name="dw_out", ta=True, after=rs_up[4])
    sb_out = _sib_start(dw_out.reshape(N_DEV, d // N_DEV, d), "sib_start_out")
    dqf, dkf, dvf, dcs = _fox_bwd(proj_a, d_attn, cum, cumt, fox_lse)
    dsq, dsk, dsv, dsinks = _swa_bwd(proj_b, d_attn, swa_lse, sinks128 + tok(sb_out), cos, sin)
    rs_out = reduce_start(sb_out, dsq, "out")
    dfg, db_forget = _fg_bwd(dcs, fg, b128 + tok(rs_out))
    dproj = jnp.concatenate([dqf, dkf, dvf, dsq, dsk, dsv, _pad_lanes(dfg, FG_PAD)], axis=1)
    dw_in_r = _matmul(dproj, h1, name="dw_in", ta=True)

    def shard_rows(j):
        lo, hi = j * in_w, (j + 1) * in_w
        parts = []
        for seg_lo, seg_hi, shift in ((0, o_fg, 0), (o_fg, o_sq, n_swa3), (o_sq, in_total, -N_FOX)):
            a, b = max(lo, seg_lo), min(hi, seg_hi)
            if a < b:
                parts.append(dw_in_r[a + shift:b + shift, :])
        return parts[0] if len(parts) == 1 else jnp.concatenate(parts, axis=0)

    sb_in = _sib_start(jnp.stack([shard_rows(j) for j in range(N_DEV)]), "sib_start_in")
    dh1 = _matmul(dproj, w_in_r, name="d_h1", tb=True, out_dtypes=(F32,), after=sb_in[4])
    grad_x, dsh_a, dsc_a, dg0 = _x_bwd(dh1, dx2, x, g_pre_mix, sc_a)

    small = jnp.concatenate([dsh_a, dsc_a, dgt_a, dsh_m, dsc_m, dgt_m, dg0, dg1, dg2, dg3, db_forget, dsinks,
                             loss_vec[:, 0:LANES]], axis=1)
    small_all = _all_gather([small], "gather_small")[0].reshape(N_DEV, small.shape[1])
    rs_in = reduce_start(sb_in, small_all, "in")

    pack = lambda bm, g0_, g1_, g2_, g3_, bf_, sk_: jnp.concatenate(
        [bm, g0_, g1_, g2_, g3_, _pad_lanes(bf_), _pad_lanes(sk_), jnp.zeros((1, LANES), F32)], axis=1)
    p_small = pack(b_mod, g_pre_mix, g_post_mix, g_pre_mlp, g_post_mlp, b_forget, swa_sinks)
    m_small = pack(m_b_mod, m_g_pre_mix, m_g_post_mix, m_g_pre_mlp, m_g_post_mlp, m_b_forget, m_swa_sinks)
    v_small = pack(v_b_mod, v_g_pre_mix, v_g_post_mix, v_g_pre_mlp, v_g_post_mlp, v_b_forget, v_swa_sinks)
    small_out = _adam_small(small_all, p_small + tok(rs_in), m_small, v_small)

    n_mod = 6 * d

    def unpack(vec):
        o = n_mod
        return (vec[:, 0:n_mod], vec[:, o:o + d], vec[:, o + d:o + 2 * d], vec[:, o + 2 * d:o + 3 * d],
                vec[:, o + 3 * d:o + 4 * d], vec[:, o + 4 * d:o + 4 * d + N_FOX],
                vec[:, o + 4 * d + LANES:o + 4 * d + LANES + N_SWA])

    loss = small_out[0][0, n_mod + 4 * d + 2 * LANES]
    g_small, d_small, nm_small, nv_small = [unpack(vec) for vec in small_out]

    dmod_cols = lax.dynamic_slice(small_all, (0, me * mod_w), (N_DEV, mod_w))
    g_w_mod, d_w_mod, nm_w_mod, nv_w_mod = _adam_mod(c_all + tok(rs_in), dmod_cols, w_mod, m_w_mod[0], v_w_mod[0])

    g_w_down, d_w_down, nm_w_down, nv_w_down = reduce_finish(rs_down, d_w_mod, w_down, m_w_down, v_w_down, "w_down")
    g_w_up, d_w_up, nm_w_up, nv_w_up = reduce_finish(rs_up, d_w_down, w_up, m_w_up, v_w_up, "w_up")
    g_w_out, d_w_out, nm_w_out, nv_w_out = reduce_finish(rs_out, d_w_up, w_out, m_w_out, v_w_out, "w_out")
    g_w_in, d_w_in, nm_w_in, nv_w_in = reduce_finish(rs_in, d_w_out, jnp.transpose(w_in), (jnp.transpose(m_w_in[0]),),
                                                     (jnp.transpose(v_w_in[0]),), "w_in")

    def assemble(w_mod_, small_, w_in_, w_out_, w_up_, w_down_):
        b_mod_, g0_, g1_, g2_, g3_, bf_, sk_ = small_
        return [w_mod_[None], b_mod_, g0_, g1_, jnp.transpose(w_in_)[None], bf_, sk_, w_out_[None], g2_, g3_,
                w_up_[None], w_down_[None]]

    outs = [loss, grad_x[None]]
    outs += assemble(g_w_mod, g_small, g_w_in, g_w_out, g_w_up, g_w_down)
    outs += assemble(d_w_mod, d_small, d_w_in, d_w_out, d_w_up, d_w_down)
    outs += assemble(nm_w_mod, nm_small, nm_w_in, nm_w_out, nm_w_up, nm_w_down)
    outs += assemble(nv_w_mod, nv_small, nv_w_in, nv_w_out, nv_w_up, nv_w_down)
    return tuple(outs)
```

```python
import functools

import jax
import jax.numpy as jnp
from jax import lax
from jax.experimental import pallas as pl
from jax.experimental.pallas import tpu as pltpu

F32 = jnp.float32
BF16 = jnp.bfloat16
MESH = pl.DeviceIdType.MESH

N_DEV = 8
N_CHIP = 4
LANES = 128
HEAD_DIM = 128
N_FOX = 8
N_SWA = 8
N_KV = 2
GQA = N_SWA // N_KV
WINDOW = 128
FOX_W = N_FOX * HEAD_DIM
SWA_W = N_SWA * HEAD_DIM
KV_W = N_KV * HEAD_DIM
ROPE_THETA = 10000.0
NORM_EPS = 1e-6
ATT_SCALE = HEAD_DIM ** -0.5
FG_PAD = 512

ADAM_LR = 0.001
ADAM_B1 = 0.9
ADAM_B2 = 0.999
ADAM_EPS = 1e-08
ADAM_WD = 0.01
ADAM_STEP = 10

VMEM_LIMIT = 56 * 1024 * 1024

NT_DIMS = (((1,), (1,)), ((), ()))
TN_DIMS = (((0,), (0,)), ((), ()))
NN_DIMS = (((1,), (0,)), ((), ()))


def _pcall(body, *, name, out_shape, grid=(), in_specs=None, out_specs=None, scratch_shapes=(), grid_spec=None):
    params = pltpu.CompilerParams(vmem_limit_bytes=VMEM_LIMIT)
    if grid_spec is not None:
        return pl.pallas_call(body, name=name, out_shape=out_shape, grid_spec=grid_spec, compiler_params=params)
    return pl.pallas_call(body, name=name, out_shape=out_shape, grid=grid, in_specs=in_specs, out_specs=out_specs,
                          scratch_shapes=scratch_shapes, compiler_params=params)


def _blk(n, pref):
    if n <= pref:
        return n
    b = (pref // LANES) * LANES
    while n % b:
        b -= LANES
    return b


def _position():
    return lax.axis_index("x"), lax.axis_index("y"), lax.axis_index("c")


ANY = pl.BlockSpec(memory_space=pl.ANY)


def _all_gather(arrs, name):
    n = len(arrs)

    def body(*refs):
        ins, outs = refs[:n], refs[n:2 * n]
        send_sems, recv_sems, local_sems = refs[2 * n:]
        x, y, c = _position()
        me, sibling = (x, y, c), (x, y, 1 - c)
        chips = [(1 - x, y), (x, 1 - y), (1 - x, 1 - y)]

        def slot(p):
            return 4 * p[0] + 2 * p[1] + p[2]

        def copy(a, k, block, to, src=None):
            dst = outs[a].at[slot(block)]
            return pltpu.make_async_remote_copy(
                src_ref=dst if src is None else src, dst_ref=dst,
                send_sem=send_sems.at[7 * a + k], recv_sem=recv_sems.at[7 * a + k],
                device_id=to, device_id_type=MESH)

        mine = [pltpu.make_async_copy(ins[a], outs[a].at[slot(me)], local_sems.at[a]) for a in range(n)]
        for cp in mine:
            cp.start()
        first = []
        for a in range(n):
            first.append(copy(a, 0, me, sibling, src=ins[a]))
            first += [copy(a, 1 + j, me, (*chip, c), src=ins[a]) for j, chip in enumerate(chips)]
        for cp in first:
            cp.start()
        passed = []
        for a in range(n):
            for j, chip in enumerate(chips):
                copy(a, 1 + j, (*chip, c), me).wait_recv()
                cp = copy(a, 4 + j, (*chip, c), sibling)
                cp.start()
                passed.append(cp)
        for a in range(n):
            copy(a, 0, sibling, me).wait_recv()
            for j, chip in enumerate(chips):
                copy(a, 4 + j, (*chip, 1 - c), me).wait_recv()
        for cp in first + passed:
            cp.wait_send()
        for cp in mine:
            cp.wait()

    return _pcall(
        body, name=name,
        out_shape=[jax.ShapeDtypeStruct((N_DEV,) + a.shape, a.dtype) for a in arrs],
        in_specs=[ANY] * n, out_specs=[ANY] * n,
        scratch_shapes=[pltpu.SemaphoreType.DMA((7 * n,)), pltpu.SemaphoreType.DMA((7 * n,)),
                        pltpu.SemaphoreType.DMA((n,))],
    )(*arrs)


HBM = pl.BlockSpec(memory_space=pltpu.HBM)
SEM = pl.BlockSpec(memory_space=pltpu.SEMAPHORE)
EFFECT = pltpu.SideEffectType.DATAFLOW_SIDE_EFFECTING


def _hbm(a):
    return pltpu.with_memory_space_constraint(a, pltpu.HBM)


def _gather_peers():
    x, y, c = _position()
    return [(x, y, 1 - c), (1 - x, y, c), (x, 1 - y, c), (1 - x, 1 - y, c)]


def _ag_start(shards, name):
    n = len(shards)
    lands = [_hbm(lax.empty((N_DEV,) + s.shape, s.dtype)) for s in shards]

    def body(*refs):
        srcs, land, send, recv = refs[:n], refs[n:2 * n], refs[2 * n:3 * n], refs[3 * n:4 * n]
        token = refs[6 * n]
        x, y, c = _position()
        for a in range(n):
            for k, to in enumerate(_gather_peers()):
                pltpu.make_async_remote_copy(
                    src_ref=srcs[a], dst_ref=land[a].at[4 * x + 2 * y + c], send_sem=send[a].at[k],
                    recv_sem=recv[a].at[k], device_id=to, device_id_type=MESH).start()
        token[...] = jnp.zeros_like(token)

    sems = [pltpu.SemaphoreType.DMA((4,))] * (2 * n)
    out = pl.pallas_call(
        body, name=name,
        out_shape=sems + [pltpu.HBM(s.shape, s.dtype) for s in shards] + [pltpu.HBM(l.shape, l.dtype) for l in lands]
        + [jax.ShapeDtypeStruct((8, LANES), F32)],
        in_specs=[HBM] * (2 * n), out_specs=[SEM] * (2 * n) + [HBM] * (2 * n) + [pl.BlockSpec(memory_space=pltpu.VMEM)],
        input_output_aliases={**{a: 2 * n + a for a in range(n)}, **{n + a: 3 * n + a for a in range(n)}},
        compiler_params=pltpu.CompilerParams(has_side_effects=EFFECT),
    )(*[_hbm(s) for s in shards], *lands)
    return out[:n], out[n:2 * n], out[2 * n:3 * n], out[3 * n:4 * n], out[4 * n]


def _ag_wait(send, recv, shard_thru, land_thru, after, name):
    def body(v_ref, land_ref, send_sem, recv_sem, after_ref, v_dead, got_ref):
        for k, to in enumerate(_gather_peers()):
            cp = pltpu.make_async_remote_copy(
                src_ref=v_ref, dst_ref=land_ref.at[0], send_sem=send_sem.at[k], recv_sem=recv_sem.at[k],
                device_id=to, device_id_type=MESH)
            cp.wait_send()
            cp.wait_recv()

    return pl.pallas_call(
        body, name=name,
        out_shape=(pltpu.HBM(shard_thru.shape, shard_thru.dtype), pltpu.HBM(land_thru.shape, land_thru.dtype)),
        in_specs=(HBM, HBM, SEM, SEM, ANY), out_specs=(HBM, HBM), input_output_aliases={0: 0, 1: 1},
        compiler_params=pltpu.CompilerParams(has_side_effects=EFFECT),
    )(shard_thru, land_thru, send, recv, after)


def _ag_forward(land, name):
    def body(land_in, land_ref, send_sems, recv_sems):
        x, y, c = _position()
        copies = []
        for j, (px, py) in enumerate([(1 - x, y), (x, 1 - y), (1 - x, 1 - y)]):
            block = land_ref.at[4 * px + 2 * py + c]
            cp = pltpu.make_async_remote_copy(src_ref=block, dst_ref=block, send_sem=send_sems.at[j],
                                              recv_sem=recv_sems.at[j], device_id=(x, y, 1 - c), device_id_type=MESH)
            cp.start()
            copies.append(cp)
        for cp in copies:
            cp.wait()

    return pl.pallas_call(
        body, name=name, out_shape=jax.ShapeDtypeStruct(land.shape, land.dtype),
        in_specs=[ANY], out_specs=ANY, input_output_aliases={0: 0},
        scratch_shapes=[pltpu.SemaphoreType.DMA((3,)), pltpu.SemaphoreType.DMA((3,))],
    )(land)


def _fwd_start(land, name):
    def body(land_ref, send, recv, land_thru, token):
        x, y, c = _position()
        for j, (px, py) in enumerate([(1 - x, y), (x, 1 - y), (1 - x, 1 - y)]):
            block = land_ref.at[4 * px + 2 * py + c]
            pltpu.make_async_remote_copy(src_ref=block, dst_ref=block, send_sem=send.at[j], recv_sem=recv.at[j],
                                         device_id=(x, y, 1 - c), device_id_type=MESH).start()
        token[...] = jnp.zeros_like(token)

    return pl.pallas_call(
        body, name=name,
        out_shape=[pltpu.SemaphoreType.DMA((3,)), pltpu.SemaphoreType.DMA((3,)), pltpu.HBM(land.shape, land.dtype),
                   jax.ShapeDtypeStruct((8, LANES), F32)],
        in_specs=[HBM], out_specs=[SEM, SEM, HBM, pl.BlockSpec(memory_space=pltpu.VMEM)],
        input_output_aliases={0: 2},
        compiler_params=pltpu.CompilerParams(has_side_effects=EFFECT),
    )(land)


def _fwd_wait(send, recv, land_thru, after, name):
    def body(land_ref, send_sem, recv_sem, after_ref, land_out):
        x, y, c = _position()
        for j in range(3):
            cp = pltpu.make_async_remote_copy(
                src_ref=land_ref.at[0], dst_ref=land_ref.at[0], send_sem=send_sem.at[j], recv_sem=recv_sem.at[j],
                device_id=(x, y, 1 - c), device_id_type=MESH)
            cp.wait_send()
            cp.wait_recv()

    return pl.pallas_call(
        body, name=name, out_shape=pltpu.HBM(land_thru.shape, land_thru.dtype),
        in_specs=(HBM, SEM, SEM, ANY), out_specs=HBM, input_output_aliases={0: 0},
        compiler_params=pltpu.CompilerParams(has_side_effects=EFFECT),
    )(land_thru, send, recv, after)


def _rs_peers():
    x, y, c = _position()
    return [(1 - x, y, c), (x, 1 - y, c), (1 - x, 1 - y, c)]


def _rs_start(chip_sums, name):
    land = _hbm(lax.empty(chip_sums.shape, chip_sums.dtype))

    def body(src, land_ref, send, recv, src_thru, land_thru, token):
        x, y, c = _position()
        for j, (px, py, pc) in enumerate(_rs_peers()):
            pltpu.make_async_remote_copy(
                src_ref=src.at[2 * px + py], dst_ref=land_ref.at[2 * x + y], send_sem=send.at[j], recv_sem=recv.at[j],
                device_id=(px, py, pc), device_id_type=MESH).start()
        token[...] = jnp.zeros_like(token)

    return pl.pallas_call(
        body, name=name,
        out_shape=[pltpu.SemaphoreType.DMA((3,)), pltpu.SemaphoreType.DMA((3,)),
                   pltpu.HBM(chip_sums.shape, chip_sums.dtype), pltpu.HBM(land.shape, land.dtype),
                   jax.ShapeDtypeStruct((8, LANES), F32)],
        in_specs=[HBM, HBM], out_specs=[SEM, SEM, HBM, HBM, pl.BlockSpec(memory_space=pltpu.VMEM)],
        input_output_aliases={0: 2, 1: 3},
        compiler_params=pltpu.CompilerParams(has_side_effects=EFFECT),
    )(_hbm(chip_sums), land)


def _rs_wait(send, recv, src_thru, land_thru, after, name):
    def body(src, land_ref, send_sem, recv_sem, after_ref, src_out, land_out):
        for j, to in enumerate(_rs_peers()):
            cp = pltpu.make_async_remote_copy(
                src_ref=src.at[0], dst_ref=land_ref.at[0], send_sem=send_sem.at[j], recv_sem=recv_sem.at[j],
                device_id=to, device_id_type=MESH)
            cp.wait_send()
            cp.wait_recv()

    return pl.pallas_call(
        body, name=name,
        out_shape=(pltpu.HBM(src_thru.shape, src_thru.dtype), pltpu.HBM(land_thru.shape, land_thru.dtype)),
        in_specs=(HBM, HBM, SEM, SEM, ANY), out_specs=(HBM, HBM), input_output_aliases={0: 0, 1: 1},
        compiler_params=pltpu.CompilerParams(has_side_effects=EFFECT),
    )(src_thru, land_thru, send, recv, after)


def _sib_start(full, name):
    land = _hbm(lax.empty((N_CHIP,) + full.shape[1:], full.dtype))

    def body(src, land_ref, send, recv, src_thru, land_thru, token):
        x, y, c = _position()
        for k in range(N_CHIP):
            pltpu.make_async_remote_copy(
                src_ref=src.at[2 * k + (1 - c)], dst_ref=land_ref.at[k], send_sem=send.at[k], recv_sem=recv.at[k],
                device_id=(x, y, 1 - c), device_id_type=MESH).start()
        token[...] = jnp.zeros_like(token)

    return pl.pallas_call(
        body, name=name,
        out_shape=[pltpu.SemaphoreType.DMA((N_CHIP,)), pltpu.SemaphoreType.DMA((N_CHIP,)),
                   pltpu.HBM(full.shape, full.dtype), pltpu.HBM(land.shape, land.dtype),
                   jax.ShapeDtypeStruct((8, LANES), F32)],
        in_specs=[HBM, HBM], out_specs=[SEM, SEM, HBM, HBM, pl.BlockSpec(memory_space=pltpu.VMEM)],
        input_output_aliases={0: 2, 1: 3},
        compiler_params=pltpu.CompilerParams(has_side_effects=EFFECT),
    )(_hbm(full), land)


def _sib_wait(send, recv, src_thru, land_thru, after, name):
    def body(src, land_ref, send_sem, recv_sem, after_ref, src_out, land_out):
        x, y, c = _position()
        for k in range(N_CHIP):
            cp = pltpu.make_async_remote_copy(
                src_ref=src.at[0], dst_ref=land_ref.at[0], send_sem=send_sem.at[k], recv_sem=recv_sem.at[k],
                device_id=(x, y, 1 - c), device_id_type=MESH)
            cp.wait_send()
            cp.wait_recv()

    return pl.pallas_call(
        body, name=name,
        out_shape=(pltpu.HBM(src_thru.shape, src_thru.dtype), pltpu.HBM(land_thru.shape, land_thru.dtype)),
        in_specs=(HBM, HBM, SEM, SEM, ANY), out_specs=(HBM, HBM), input_output_aliases={0: 0, 1: 1},
        compiler_params=pltpu.CompilerParams(has_side_effects=EFFECT),
    )(src_thru, land_thru, send, recv, after)


def _rs_sibling(arrs, name):
    n = len(arrs)

    def body(*refs):
        ins, outs = refs[:n], refs[n:2 * n]
        send_sems, recv_sems = refs[2 * n:]
        x, y, c = _position()
        copies = []
        for a in range(n):
            for k in range(N_CHIP):
                cp = pltpu.make_async_remote_copy(
                    src_ref=ins[a].at[2 * k + (1 - c)], dst_ref=outs[a].at[k],
                    send_sem=send_sems.at[N_CHIP * a + k], recv_sem=recv_sems.at[N_CHIP * a + k],
                    device_id=(x, y, 1 - c), device_id_type=MESH)
                cp.start()
                copies.append(cp)
        for cp in copies:
            cp.wait()

    return _pcall(
        body, name=name,
        out_shape=[jax.ShapeDtypeStruct((N_CHIP,) + a.shape[1:], a.dtype) for a in arrs],
        in_specs=[ANY] * n, out_specs=[ANY] * n,
        scratch_shapes=[pltpu.SemaphoreType.DMA((N_CHIP * n,)), pltpu.SemaphoreType.DMA((N_CHIP * n,))],
    )(*arrs)


def _chip_sum(full, recv, core, name):
    _, rows, cols = full.shape
    tr = _blk(rows, 1024) if rows % LANES == 0 else rows

    def body(core_ref, a_ref, b_ref, o_ref):
        o_ref[...] = (a_ref[...].astype(F32) + b_ref[...].astype(F32)).astype(o_ref.dtype)

    grid_spec = pltpu.PrefetchScalarGridSpec(
        num_scalar_prefetch=1, grid=(N_CHIP, rows // tr),
        in_specs=[pl.BlockSpec((None, tr, cols), lambda k, i, core_ref: (2 * k + core_ref[0], i, 0)),
                  pl.BlockSpec((None, tr, cols), lambda k, i, core_ref: (k, i, 0))],
        out_specs=pl.BlockSpec((None, tr, cols), lambda k, i, core_ref: (k, i, 0)))
    return _pcall(body, name=name, out_shape=jax.ShapeDtypeStruct((N_CHIP, rows, cols), full.dtype),
                  grid_spec=grid_spec)(core, full, recv)


def _matmul(a, b, *, name, ta=False, tb=False, tm=1024, tn=1024, tk=2048, out_dtypes=(BF16,), epilogue=None,
            row_extras=(), tile_extras=(), out_shape=None, out_map=None, b_sharded=False, n_cols=None, n_off=0,
            after=None, b_pair=False):
    m, k = (a.shape[1], a.shape[0]) if ta else a.shape
    if b_sharded:
        shard_c = b.shape[2]
        n, kb = (b.shape[1], N_DEV * shard_c) if tb else (N_DEV * shard_c, b.shape[1])
        tn, tk = (tn, min(tk, shard_c)) if tb else (min(tn, shard_c), tk)
        if b_pair:
            assert tb and tk == shard_c
            tk = 2 * shard_c
    else:
        n, kb = b.shape if tb else (b.shape[1], b.shape[0])
    assert kb == k, (name, kb, k)
    if n_cols is not None:
        n = n_cols
    tm, tn, tk = _blk(m, tm), _blk(n, tn), _blk(k, tk)
    assert n_off % tn == 0
    nk = k // tk
    dims = (((0 if ta else 1,), (1 if tb else 0,)), ((), ()))
    behind = () if after is None else (after,)
    n_row, n_tile, n_out = len(row_extras), len(tile_extras), len(out_dtypes)
    n_b = 2 if b_pair else 1
    first_out = 1 + n_b + n_row + n_tile + len(behind)

    def body(*refs):
        a_ref, b_ref = refs[:2]
        extras = refs[1 + n_b:1 + n_b + n_row + n_tile]
        outs = refs[first_out:first_out + n_out]
        acc_ref = refs[-1]
        jj, kk = pl.program_id(1), pl.program_id(2)
        if b_pair:
            half = tk // 2
            part = (lax.dot_general(a_ref[:, 0:half].astype(BF16), b_ref[...].astype(BF16), dims,
                                    preferred_element_type=F32)
                    + lax.dot_general(a_ref[:, half:tk].astype(BF16), refs[2][...].astype(BF16), dims,
                                      preferred_element_type=F32))
        else:
            part = lax.dot_general(a_ref[...].astype(BF16), b_ref[...].astype(BF16), dims,
                                   preferred_element_type=F32)

        def finish(acc):
            res = (acc,) if epilogue is None else epilogue(acc, jj, *[e[...] for e in extras])
            for o_ref, r in zip(outs, res):
                o_ref[...] = r.astype(o_ref.dtype)

        if nk == 1:
            finish(part)
        else:
            @pl.when(kk == 0)
            def _():
                acc_ref[...] = part

            @pl.when(kk > 0)
            def _():
                acc_ref[...] += part

            @pl.when(kk == nk - 1)
            def _():
                finish(acc_ref[...])

    a_spec = pl.BlockSpec((tk, tm), lambda i, j, kk: (kk, i)) if ta else pl.BlockSpec((tm, tk), lambda i, j, kk: (i, kk))
    if b_pair:
        b_spec = pl.BlockSpec((None, tn, shard_c), lambda i, j, kk: (2 * kk, j, 0))
    elif b_sharded and tb:
        per = shard_c // tk
        b_spec = pl.BlockSpec((None, tn, tk), lambda i, j, kk: (kk // per, j, kk % per))
    elif b_sharded:
        per = shard_c // tn
        b_spec = pl.BlockSpec((None, tk, tn), lambda i, j, kk: (j // per, kk, j % per))
    elif tb:
        b_spec = pl.BlockSpec((tn, tk), lambda i, j, kk: (j + n_off // tn, kk))
    else:
        b_spec = pl.BlockSpec((tk, tn), lambda i, j, kk: (kk, j + n_off // tn))
    in_specs = [a_spec, b_spec]
    if b_pair:
        in_specs.append(pl.BlockSpec((None, tn, shard_c), lambda i, j, kk: (2 * kk + 1, j, 0)))
    in_specs += [pl.BlockSpec((tm, LANES), lambda i, j, kk: (i, 0)) for _ in row_extras]
    in_specs += [pl.BlockSpec((tm, tn), lambda i, j, kk: (i, j)) for _ in tile_extras]
    in_specs += [ANY for _ in behind]
    if out_map is None:
        out_specs = [pl.BlockSpec((tm, tn), lambda i, j, kk: (i, j)) for _ in out_dtypes]
        shapes = [jax.ShapeDtypeStruct((m, n), dt) for dt in out_dtypes]
    else:
        out_specs = [out_map(tm, tn)]
        shapes = [jax.ShapeDtypeStruct(out_shape, out_dtypes[0])]
    acc_shape = (tm, tn) if nk > 1 else (8, LANES)
    res = _pcall(body, name=name, out_shape=shapes, grid=(m // tm, n // tn, nk), in_specs=in_specs,
                 out_specs=out_specs, scratch_shapes=[pltpu.VMEM(acc_shape, F32)])(
                     a, *([b, b] if b_pair else [b]), *row_extras, *tile_extras, *behind)
    return res[0] if n_out == 1 else res


def _rope_cols(acc, j, cos, sin, n_rope):
    width = acc.shape[1]
    parts = []
    for g in range(width // HEAD_DIM):
        xg = acc[:, g * HEAD_DIM:(g + 1) * HEAD_DIM]
        roped = xg * cos + pltpu.roll(xg, HEAD_DIM // 2, 1) * sin
        parts.append(jnp.where(j * width + g * HEAD_DIM < n_rope, roped, xg))
    return jnp.concatenate(parts, axis=1) if len(parts) > 1 else parts[0]


def _silu(v):
    return v / (1.0 + jnp.exp(-v))


def _mod_part(c_all, w_mod, b_part):
    d, w = w_mod.shape
    tk = _blk(d, 512)

    def body(c_ref, w_ref, b_ref, o_ref):
        kk = pl.program_id(0)
        cond = _silu(c_ref[...]).astype(BF16)
        part = jnp.dot(cond, w_ref[...].astype(BF16), preferred_element_type=F32)

        @pl.when(kk == 0)
        def _():
            o_ref[...] = part + b_ref[...]

        @pl.when(kk > 0)
        def _():
            o_ref[...] += part

    return _pcall(body, name="mod_part", out_shape=jax.ShapeDtypeStruct((N_DEV, w), F32), grid=(d // tk,),
                  in_specs=[pl.BlockSpec((N_DEV, tk), lambda kk: (0, kk)), pl.BlockSpec((tk, w), lambda kk: (kk, 0)),
                            pl.BlockSpec((1, w), lambda kk: (0, 0))],
                  out_specs=pl.BlockSpec((N_DEV, w), lambda kk: (0, 0)))(c_all, w_mod, b_part)


def _row_call(body, name, t, d, tiled_in, vec_in, tiled_out_dtypes, n_vec_out, tr=256):
    tr = _blk(t, tr)
    tile = pl.BlockSpec((tr, d), lambda i: (i, 0))
    vec = pl.BlockSpec((1, d), lambda i: (0, 0))
    out_shape = [jax.ShapeDtypeStruct((t, d), dt) for dt in tiled_out_dtypes]
    out_shape += [jax.ShapeDtypeStruct((1, d), F32)] * n_vec_out
    return _pcall(body, name=name, out_shape=out_shape, grid=(t // tr,),
                  in_specs=[tile] * len(tiled_in) + [vec] * len(vec_in),
                  out_specs=[tile] * len(tiled_out_dtypes) + [vec] * n_vec_out)(*tiled_in, *vec_in)


def _accumulate(ref, val):
    @pl.when(pl.program_id(0) == 0)
    def _():
        ref[...] = val

    @pl.when(pl.program_id(0) > 0)
    def _():
        ref[...] += val


def _rsum(v):
    return jnp.sum(v, axis=0, keepdims=True)


def _rms(v):
    return lax.rsqrt(jnp.mean(v * v, axis=-1, keepdims=True) + NORM_EPS)


def _rms_bwd(vhat, r, dvhat):
    return r * (dvhat - vhat * jnp.mean(dvhat * vhat, axis=-1, keepdims=True))


def _pre_attn(x, g0, sc_a, sh_a):
    def body(x_ref, g_ref, sc_ref, sh_ref, h_ref):
        xv = x_ref[...]
        h_ref[...] = (xv * _rms(xv) * g_ref[...] * (1.0 + sc_ref[...]) + sh_ref[...]).astype(BF16)

    t, d = x.shape
    return _row_call(body, "pre_attn", t, d, [x], [g0, sc_a, sh_a], [BF16], 0)[0]


def _post_mix(x, mix, gt_a, g1, g2, sc_m, sh_m):
    def body(x_ref, mix_ref, gt_ref, g1_ref, g2_ref, sc_ref, sh_ref, x2_ref, h2_ref):
        mv = mix_ref[...]
        x2 = x_ref[...] + gt_ref[...] * (mv * _rms(mv) * g1_ref[...])
        x2_ref[...] = x2
        h2_ref[...] = (x2 * _rms(x2) * g2_ref[...] * (1.0 + sc_ref[...]) + sh_ref[...]).astype(BF16)

    t, d = x.shape
    return _row_call(body, "post_mix", t, d, [x, mix], [gt_a, g1, g2, sc_m, sh_m], [F32, BF16], 0)


def _final(y, x2, target, gt_m, g3):
    t, d = y.shape

    def body(y_ref, x2_ref, tg_ref, gt_ref, g3_ref, dy_ref, dout_ref, dgt_ref, dg3_ref, loss_ref):
        yv = y_ref[...]
        r = _rms(yv)
        yhat = yv * r
        n3 = yhat * g3_ref[...]
        err = x2_ref[...] + gt_ref[...] * n3 - tg_ref[...]
        _accumulate(loss_ref, jnp.zeros((1, d), F32) + 0.5 * jnp.sum(err * err) / d)
        dout = err * (1.0 / d)
        dout_ref[...] = dout
        _accumulate(dgt_ref, _rsum(dout * n3))
        dn3 = dout * gt_ref[...]
        _accumulate(dg3_ref, _rsum(dn3 * yhat))
        dy_ref[...] = _rms_bwd(yhat, r, dn3 * g3_ref[...]).astype(BF16)

    return _row_call(body, "final", t, d, [y, x2, target], [gt_m, g3], [BF16, F32], 3)


def _mid_bwd(dh2, dout, x2, mix, g2, sc_m, gt_a, g1):
    t, d = x2.shape

    def body(dh2_ref, dout_ref, x2_ref, mix_ref, g2_ref, sc_ref, gt_ref, g1_ref,
             dmix_ref, dx2_ref, dsh_ref, dsc_ref, dg2_ref, dgt_ref, dg1_ref):
        dh2v = dh2_ref[...]
        x2v = x2_ref[...]
        r2 = _rms(x2v)
        x2hat = x2v * r2
        _accumulate(dsh_ref, _rsum(dh2v))
        _accumulate(dsc_ref, _rsum(dh2v * (x2hat * g2_ref[...])))
        dn2 = dh2v * (1.0 + sc_ref[...])
        _accumulate(dg2_ref, _rsum(dn2 * x2hat))
        dx2 = dout_ref[...] + _rms_bwd(x2hat, r2, dn2 * g2_ref[...])
        dx2_ref[...] = dx2
        mv = mix_ref[...]
        r1 = _rms(mv)
        mhat = mv * r1
        _accumulate(dgt_ref, _rsum(dx2 * (mhat * g1_ref[...])))
        dn1 = dx2 * gt_ref[...]
        _accumulate(dg1_ref, _rsum(dn1 * mhat))
        dmix_ref[...] = _rms_bwd(mhat, r1, dn1 * g1_ref[...]).astype(BF16)

    return _row_call(body, "mid_bwd", t, d, [dh2, dout, x2, mix], [g2, sc_m, gt_a, g1], [BF16, F32], 5)


def _x_bwd(dh1, dx2, x, g0, sc_a):
    t, d = x.shape

    def body(dh1_ref, dx2_ref, x_ref, g0_ref, sc_ref, dx_ref, dsh_ref, dsc_ref, dg0_ref):
        dh1v = dh1_ref[...]
        xv = x_ref[...]
        r0 = _rms(xv)
        xhat = xv * r0
        _accumulate(dsh_ref, _rsum(dh1v))
        _accumulate(dsc_ref, _rsum(dh1v * (xhat * g0_ref[...])))
        dn0 = dh1v * (1.0 + sc_ref[...])
        _accumulate(dg0_ref, _rsum(dn0 * xhat))
        dx_ref[...] = dx2_ref[...] + _rms_bwd(xhat, r0, dn0 * g0_ref[...])

    return _row_call(body, "x_bwd", t, d, [dh1, dx2, x], [g0, sc_a], [F32], 3)


def _pick_lane(block, h):
    lane = lax.broadcasted_iota(jnp.int32, block.shape, 1)
    return jnp.sum(jnp.where(lane == h, block, 0.0), axis=1, keepdims=True)


def _put_lane(ref, rows, h, col):
    old = ref[rows, :]
    lane = lax.broadcasted_iota(jnp.int32, old.shape, 1)
    ref[rows, :] = jnp.where(lane == h, col, old)


def _tri(n, lower):
    r = lax.broadcasted_iota(jnp.int32, (n, n), 0)
    c = lax.broadcasted_iota(jnp.int32, (n, n), 1)
    return jnp.where((c <= r) if lower else (c >= r), 1.0, 0.0).astype(F32)


def _cum_fwd(fg, b128):
    t = fg.shape[0]
    nb = t // LANES

    def body(fg_ref, b_ref, cum_ref, cumt_ref):
        tri = _tri(LANES, True)
        carry = jnp.zeros((1, LANES), F32)
        for i in range(nb):
            z = fg_ref[i * LANES:(i + 1) * LANES, :] + b_ref[...]
            lf = jnp.minimum(z, 0.0) - jnp.log(1.0 + jnp.exp(-jnp.abs(z)))
            blk = jnp.dot(tri, lf, precision=lax.Precision.HIGHEST, preferred_element_type=F32) + carry
            cum_ref[i * LANES:(i + 1) * LANES, :] = blk
            carry = blk[LANES - 1:LANES, :]
        cumt_ref[...] = cum_ref[...].T[0:N_FOX, :]

    return _pcall(body, name="cum_fwd",
                  out_shape=[jax.ShapeDtypeStruct((t, LANES), F32), jax.ShapeDtypeStruct((N_FOX, t), F32)],
                  grid=(1,),
                  in_specs=[pl.BlockSpec((t, LANES), lambda i: (0, 0)), pl.BlockSpec((1, LANES), lambda i: (0, 0))],
                  out_specs=[pl.BlockSpec((t, LANES), lambda i: (0, 0)), pl.BlockSpec((N_FOX, t), lambda i: (0, 0))],
                  )(fg, b128)


def _fg_bwd(dcs_rows, fg, b128):
    t = fg.shape[0]
    nb = t // LANES

    def body(dcs_ref, fg_ref, b_ref, dfg_ref, db_ref, dcum_ref):
        dcum_ref[...] = -jnp.concatenate([dcs_ref[...], jnp.zeros((LANES - N_FOX, t), F32)], axis=0).T
        tri = _tri(LANES, False)
        carry = jnp.zeros((1, LANES), F32)
        db = jnp.zeros((1, LANES), F32)
        for i in reversed(range(nb)):
            rows = slice(i * LANES, (i + 1) * LANES)
            dlf = jnp.dot(tri, dcum_ref[rows, :], precision=lax.Precision.HIGHEST, preferred_element_type=F32) + carry
            carry = dlf[0:1, :]
            z = fg_ref[rows, :] + b_ref[...]
            dfg = dlf / (1.0 + jnp.exp(z))
            dfg_ref[rows, :] = dfg.astype(BF16)
            db = db + _rsum(dfg)
        db_ref[...] = db

    full = pl.BlockSpec((t, LANES), lambda i: (0, 0))
    vec = pl.BlockSpec((1, LANES), lambda i: (0, 0))
    return _pcall(body, name="fg_bwd",
                  out_shape=[jax.ShapeDtypeStruct((t, LANES), BF16), jax.ShapeDtypeStruct((1, LANES), F32)],
                  grid=(1,), in_specs=[pl.BlockSpec((N_FOX, t), lambda i: (0, 0)), full, vec], out_specs=[full, vec],
                  scratch_shapes=[pltpu.VMEM((t, LANES), F32)])(dcs_rows, fg, b128)


def _head_spec(t, col0, div=1):
    return pl.BlockSpec((t, HEAD_DIM), lambda h: (0, col0 + h // div))


def _fox_scores(q, k, cq, ck, i, tq, end):
    s = lax.dot_general(q, k, NT_DIMS, preferred_element_type=F32) * ATT_SCALE + cq - ck
    row = lax.broadcasted_iota(jnp.int32, (tq, end), 0) + i * tq
    col = lax.broadcasted_iota(jnp.int32, (tq, end), 1)
    return jnp.where(row >= col, s, -jnp.inf)


def _fox_fwd(proj_a, cum, cumt):
    t = proj_a.shape[0]
    tq = _blk(t, 512)
    nq = t // tq

    def body(q_ref, k_ref, v_ref, cum_ref, cumt_ref, o_ref, lse_ref):
        h = pl.program_id(0)
        cq_all = _pick_lane(cum_ref[...], h)
        ck_all = cumt_ref[pl.ds(h, 1), :]

        @pl.when(h == 0)
        def _():
            lse_ref[...] = jnp.zeros_like(lse_ref)

        for i in range(nq):
            rows, end = slice(i * tq, (i + 1) * tq), (i + 1) * tq
            s = _fox_scores(q_ref[rows, :], k_ref[0:end, :], cq_all[rows, :], ck_all[:, 0:end], i, tq, end)
            m = jnp.max(s, axis=1, keepdims=True)
            p = jnp.exp(s - m)
            l = jnp.sum(p, axis=1, keepdims=True)
            o = jnp.dot(p.astype(BF16), v_ref[0:end, :], preferred_element_type=F32) / l
            o_ref[rows, :] = o.astype(BF16)
            _put_lane(lse_ref, rows, h, m + jnp.log(l))

    nh = FOX_W // HEAD_DIM
    stat = pl.BlockSpec((t, LANES), lambda h: (0, 0))
    return _pcall(body, name="fox_fwd",
                  out_shape=[jax.ShapeDtypeStruct((t, FOX_W), BF16), jax.ShapeDtypeStruct((t, LANES), F32)],
                  grid=(N_FOX,),
                  in_specs=[_head_spec(t, 0), _head_spec(t, nh), _head_spec(t, 2 * nh), stat,
                            pl.BlockSpec((N_FOX, t), lambda h: (0, 0))],
                  out_specs=[_head_spec(t, 0), stat])(proj_a, proj_a, proj_a, cum, cumt)


def _fox_bwd(proj_a, d_attn, cum, cumt, lse):
    t = proj_a.shape[0]
    tq = _blk(t, 512)
    nq = t // tq

    def body(q_ref, k_ref, v_ref, do_ref, cum_ref, cumt_ref, lse_ref,
             dq_ref, dk_ref, dv_ref, dcs_ref, dk_acc, dv_acc, dcs_acc):
        h = pl.program_id(0)
        cq_all = _pick_lane(cum_ref[...], h)
        ck_all = cumt_ref[pl.ds(h, 1), :]
        lse_all = _pick_lane(lse_ref[...], h)
        dk_acc[...] = jnp.zeros_like(dk_acc)
        dv_acc[...] = jnp.zeros_like(dv_acc)
        dcs_acc[...] = jnp.zeros_like(dcs_acc)
        for i in range(nq):
            rows, end = slice(i * tq, (i + 1) * tq), (i + 1) * tq
            q, k, v, do = q_ref[rows, :], k_ref[0:end, :], v_ref[0:end, :], do_ref[rows, :]
            s = _fox_scores(q, k, cq_all[rows, :], ck_all[:, 0:end], i, tq, end)
            p = jnp.exp(s - lse_all[rows, :])
            dp = lax.dot_general(do, v, NT_DIMS, preferred_element_type=F32)
            ds = p * (dp - jnp.sum(p * dp, axis=1, keepdims=True))
            dcs_acc[:, 0:end] += jnp.sum(ds, axis=0, keepdims=True)
            ds = ds.astype(BF16)
            dq_ref[rows, :] = (jnp.dot(ds, k, preferred_element_type=F32) * ATT_SCALE).astype(BF16)
            dk_acc[0:end, :] += lax.dot_general(ds, q, TN_DIMS, preferred_element_type=F32)
            dv_acc[0:end, :] += lax.dot_general(p.astype(BF16), do, TN_DIMS, preferred_element_type=F32)
        dk_ref[...] = (dk_acc[...] * ATT_SCALE).astype(BF16)
        dv_ref[...] = dv_acc[...].astype(BF16)
        dcs_ref[pl.ds(h, 1), :] = dcs_acc[...]

    nh = FOX_W // HEAD_DIM
    stat = pl.BlockSpec((t, LANES), lambda h: (0, 0))
    rows8 = pl.BlockSpec((N_FOX, t), lambda h: (0, 0))
    head = _head_spec(t, 0)
    wide = jax.ShapeDtypeStruct((t, FOX_W), BF16)
    return _pcall(body, name="fox_bwd",
                  out_shape=[wide, wide, wide, jax.ShapeDtypeStruct((N_FOX, t), F32)],
                  grid=(N_FOX,),
                  in_specs=[_head_spec(t, 0), _head_spec(t, nh), _head_spec(t, 2 * nh), head, stat, rows8, stat],
                  out_specs=[head, head, head, rows8],
                  scratch_shapes=[pltpu.VMEM((t, HEAD_DIM), F32), pltpu.VMEM((t, HEAD_DIM), F32),
                                  pltpu.VMEM((1, t), F32)],
                  )(proj_a, proj_a, proj_a, d_attn, cum, cumt, lse)


def _swa_scores(q, k, i, tq, start, end):
    s = lax.dot_general(q, k, NT_DIMS, preferred_element_type=F32) * ATT_SCALE
    row = lax.broadcasted_iota(jnp.int32, (tq, end - start), 0) + i * tq
    col = lax.broadcasted_iota(jnp.int32, (tq, end - start), 1) + start
    diff = row - col
    return jnp.where((diff >= 0) & (diff < WINDOW), s, -jnp.inf)


def _swa_blocks(t):
    tq = _blk(t, 256)
    return tq, [(i, max(0, i * tq - WINDOW), (i + 1) * tq) for i in range(t // tq)]


def _swa_fwd(proj_b, sinks128):
    t = proj_b.shape[0]
    tq, blocks = _swa_blocks(t)

    def body(q_ref, k_ref, v_ref, sink_ref, o_ref, lse_ref):
        h = pl.program_id(0)
        sink = _pick_lane(sink_ref[...], h)

        @pl.when(h == 0)
        def _():
            lse_ref[...] = jnp.zeros_like(lse_ref)

        for i, start, end in blocks:
            rows = slice(i * tq, end)
            s = _swa_scores(q_ref[rows, :], k_ref[start:end, :], i, tq, start, end)
            m = jnp.maximum(jnp.max(s, axis=1, keepdims=True), sink)
            p = jnp.exp(s - m)
            l = jnp.sum(p, axis=1, keepdims=True) + jnp.exp(sink - m)
            o = jnp.dot(p.astype(BF16), v_ref[start:end, :], preferred_element_type=F32) / l
            o_ref[rows, :] = o.astype(BF16)
            _put_lane(lse_ref, rows, h, m + jnp.log(l))

    stat = pl.BlockSpec((t, LANES), lambda h: (0, 0))
    return _pcall(body, name="swa_fwd",
                  out_shape=[jax.ShapeDtypeStruct((t, SWA_W), BF16), jax.ShapeDtypeStruct((t, LANES), F32)],
                  grid=(N_SWA,),
                  in_specs=[_head_spec(t, 0), _head_spec(t, N_SWA, GQA), _head_spec(t, N_SWA + N_KV, GQA),
                            pl.BlockSpec((1, LANES), lambda h: (0, 0))],
                  out_specs=[_head_spec(t, 0), stat])(proj_b, proj_b, proj_b, sinks128)


def _rope_bwd(d, cos, sin):
    return d * cos + pltpu.roll(d * sin, HEAD_DIM // 2, 1)


def _swa_bwd(proj_b, d_attn, lse, sinks128, cos, sin):
    t = proj_b.shape[0]
    tq, blocks = _swa_blocks(t)

    def body(q_ref, k_ref, v_ref, do_ref, lse_ref, sink_ref, cos_ref, sin_ref,
             dq_ref, dk_ref, dv_ref, dsink_ref, dk_acc, dv_acc):
        h = pl.program_id(0)
        sink = _pick_lane(sink_ref[...], h)
        lse_all = _pick_lane(lse_ref[...], h)

        @pl.when(h == 0)
        def _():
            dsink_ref[...] = jnp.zeros_like(dsink_ref)

        @pl.when(h % GQA == 0)
        def _():
            dk_acc[...] = jnp.zeros_like(dk_acc)
            dv_acc[...] = jnp.zeros_like(dv_acc)

        dsink = jnp.zeros((1, 1), F32)
        for i, start, end in blocks:
            rows = slice(i * tq, end)
            q, k, v, do = q_ref[rows, :], k_ref[start:end, :], v_ref[start:end, :], do_ref[rows, :]
            s = _swa_scores(q, k, i, tq, start, end)
            p = jnp.exp(s - lse_all[rows, :])
            dp = lax.dot_general(do, v, NT_DIMS, preferred_element_type=F32)
            delta = jnp.sum(p * dp, axis=1, keepdims=True)
            ds = (p * (dp - delta)).astype(BF16)
            dq = jnp.dot(ds, k, preferred_element_type=F32) * ATT_SCALE
            dq_ref[rows, :] = _rope_bwd(dq, cos_ref[rows, :], sin_ref[rows, :]).astype(BF16)
            dk_acc[start:end, :] += lax.dot_general(ds, q, TN_DIMS, preferred_element_type=F32)
            dv_acc[start:end, :] += lax.dot_general(p.astype(BF16), do, TN_DIMS, preferred_element_type=F32)
            dsink = dsink - jnp.sum(jnp.exp(sink - lse_all[rows, :]) * delta, axis=0, keepdims=True)
        old = dsink_ref[...]
        lane = lax.broadcasted_iota(jnp.int32, old.shape, 1)
        dsink_ref[...] = jnp.where(lane == h, dsink, old)

        @pl.when(h % GQA == GQA - 1)
        def _():
            dk_ref[...] = _rope_bwd(dk_acc[...] * ATT_SCALE, cos_ref[...], sin_ref[...]).astype(BF16)
            dv_ref[...] = dv_acc[...].astype(BF16)

    stat = pl.BlockSpec((t, LANES), lambda h: (0, 0))
    vec = pl.BlockSpec((1, LANES), lambda h: (0, 0))
    head = _head_spec(t, 0)
    kv_out = _head_spec(t, 0, GQA)
    return _pcall(body, name="swa_bwd",
                  out_shape=[jax.ShapeDtypeStruct((t, SWA_W), BF16), jax.ShapeDtypeStruct((t, KV_W), BF16),
                             jax.ShapeDtypeStruct((t, KV_W), BF16), jax.ShapeDtypeStruct((1, LANES), F32)],
                  grid=(N_SWA,),
                  in_specs=[head, _head_spec(t, N_SWA, GQA), _head_spec(t, N_SWA + N_KV, GQA),
                            _head_spec(t, N_FOX), stat, vec, stat, stat],
                  out_specs=[head, kv_out, kv_out, vec],
                  scratch_shapes=[pltpu.VMEM((t, HEAD_DIM), F32), pltpu.VMEM((t, HEAD_DIM), F32)],
                  )(proj_b, proj_b, proj_b, d_attn, lse, sinks128, cos, sin)


def _adamw(w, g, m, v):
    m = ADAM_B1 * m + (1.0 - ADAM_B1) * g
    v = ADAM_B2 * v + (1.0 - ADAM_B2) * (g * g)
    m_hat = m / (1.0 - ADAM_B1 ** ADAM_STEP)
    v_hat = v / (1.0 - ADAM_B2 ** ADAM_STEP)
    delta = -ADAM_LR * (m_hat / (jnp.sqrt(v_hat) + ADAM_EPS) + ADAM_WD * w)
    return delta, m, v


def _adam_pieces(w, m, v, own, land, idx, name):
    rows, cols = w.shape
    tr, tc = (256, cols) if rows % 256 == 0 else (rows, _blk(cols, 512))

    def body(idx_ref, own_ref, l1_ref, l2_ref, l3_ref, w_ref, m_ref, v_ref, g_ref, d_ref, mo_ref, vo_ref):
        g = own_ref[...].astype(F32) + l1_ref[...].astype(F32) + l2_ref[...].astype(F32) + l3_ref[...].astype(F32)
        g_ref[...] = g
        d_ref[...], mo_ref[...], vo_ref[...] = _adamw(w_ref[...], g, m_ref[...], v_ref[...])

    def piece(p):
        return pl.BlockSpec((None, tr, tc), lambda i, j, idx_ref: (idx_ref[p], i, j))

    tile = pl.BlockSpec((tr, tc), lambda i, j, idx_ref: (i, j))
    out = jax.ShapeDtypeStruct((rows, cols), F32)
    grid_spec = pltpu.PrefetchScalarGridSpec(
        num_scalar_prefetch=1, grid=(rows // tr, cols // tc),
        in_specs=[piece(0), piece(1), piece(2), piece(3), tile, tile, tile], out_specs=[tile] * 4)
    return _pcall(body, name=name, out_shape=[out] * 4, grid_spec=grid_spec)(idx, own, land, land, land, w, m, v)


def _adam_mod(c_all, dmod_cols, w, m, v):
    rows, cols = w.shape
    tr = _blk(rows, 256)

    def body(c_ref, dm_ref, w_ref, m_ref, v_ref, g_ref, d_ref, mo_ref, vo_ref):
        cond = _silu(c_ref[...]).astype(BF16)
        g = lax.dot_general(cond, dm_ref[...].astype(BF16), TN_DIMS, preferred_element_type=F32)
        g_ref[...] = g
        d_ref[...], mo_ref[...], vo_ref[...] = _adamw(w_ref[...], g, m_ref[...], v_ref[...])

    tile = pl.BlockSpec((tr, cols), lambda i: (i, 0))
    out = jax.ShapeDtypeStruct((rows, cols), F32)
    return _pcall(body, name="adam_mod", out_shape=[out] * 4, grid=(rows // tr,),
                  in_specs=[pl.BlockSpec((N_DEV, tr), lambda i: (0, i)), pl.BlockSpec((N_DEV, cols), lambda i: (0, 0)),
                            tile, tile, tile],
                  out_specs=[tile] * 4)(c_all, dmod_cols, w, m, v)


def _adam_small(parts, w, m, v):
    nv = w.shape[1]

    def body(p_ref, w_ref, m_ref, v_ref, g_ref, d_ref, mo_ref, vo_ref):
        g = p_ref[0:1, :]
        for k in range(1, N_DEV):
            g = g + p_ref[k:k + 1, :]
        g_ref[...] = g
        d_ref[...], mo_ref[...], vo_ref[...] = _adamw(w_ref[...], g, m_ref[...], v_ref[...])

    vec = pl.BlockSpec((1, nv), lambda i: (0, 0))
    out = jax.ShapeDtypeStruct((1, nv), F32)
    return _pcall(body, name="adam_small", out_shape=[out] * 4, grid=(1,),
                  in_specs=[pl.BlockSpec((N_DEV, nv), lambda i: (0, 0)), vec, vec, vec],
                  out_specs=[vec] * 4)(parts, w, m, v)


def _pad_lanes(v, width=LANES):
    return jnp.pad(v, ((0, 0), (0, width - v.shape[1])))


def kernel(x, c, w_mod, b_mod, g_pre_mix, g_post_mix, w_in, b_forget, swa_sinks, w_out, g_pre_mlp, g_post_mlp, w_up, w_down, loss_target, m_w_mod, m_b_mod, m_g_pre_mix, m_g_post_mix, m_w_in, m_b_forget, m_swa_sinks, m_w_out, m_g_pre_mlp, m_g_post_mlp, m_w_up, m_w_down, v_w_mod, v_b_mod, v_g_pre_mix, v_g_post_mix, v_w_in, v_b_forget, v_swa_sinks, v_w_out, v_g_pre_mlp, v_g_post_mlp, v_w_up, v_w_down):
    ax, ay, ac = _position()
    me = 4 * ax + 2 * ay + ac
    x, target = x[0], loss_target[0]
    t, d = x.shape
    w_mod, w_in, w_out, w_up, w_down = w_mod[0], w_in[0], w_out[0], w_up[0], w_down[0]
    mod_w = w_mod.shape[1]
    in_w = w_in.shape[1]
    in_total = N_DEV * in_w
    shard_ff = w_up.shape[1]
    n_fox3 = 3 * FOX_W
    n_swa3 = SWA_W + 2 * KV_W
    assert in_total == n_fox3 + N_FOX + n_swa3 and d == FOX_W + SWA_W

    c_all = _all_gather([c], "gather_c")[0].reshape(N_DEV, d)
    b_part = lax.dynamic_slice(b_mod, (0, me * mod_w), (1, mod_w))
    mod_parts = _all_gather([_mod_part(c_all, w_mod, b_part)], "gather_mod")[0]
    mod = lax.dynamic_index_in_dim(mod_parts, me, axis=1, keepdims=False).reshape(1, N_DEV * mod_w)

    w_in_b, mod = lax.optimization_barrier((w_in.astype(BF16), mod))
    first = _ag_start([w_in_b], "ag_start_in")
    behind_first = first[4][0, 0]
    rest = _ag_start([(w + behind_first).astype(BF16) for w in (w_out, w_up, w_down)], "ag_start_rest")
    ag_send, ag_recv, ag_shard, ag_land = [a + b for a, b in zip(first[:4], rest[:4])]
    ag_token = rest[4]

    def gathered(i, after, name):
        shard, land = _ag_wait(ag_send[i], ag_recv[i], ag_shard[i], ag_land[i], after, "ag_wait_" + name)
        return lax.dynamic_update_slice(_ag_forward(land, "ag_fwd_" + name), shard[None], (me, 0, 0))

    def gathered_start(i, after, name):
        shard, land = _ag_wait(ag_send[i], ag_recv[i], ag_shard[i], ag_land[i], after, "ag_wait_" + name)
        return shard, _fwd_start(land, "fwd_start_" + name)

    def gathered_finish(started, after, name):
        shard, (send, recv, land, _) = started
        return lax.dynamic_update_slice(_fwd_wait(send, recv, land, after, "fwd_wait_" + name), shard[None], (me, 0, 0))

    sh_a, sc_a, gt_a, sh_m, sc_m, gt_m = [mod[:, i * d:(i + 1) * d] for i in range(6)]

    half = HEAD_DIM // 2
    inv_freq = 1.0 / (ROPE_THETA ** (jnp.arange(half, dtype=F32) * (2.0 / HEAD_DIM)))
    ang = jnp.arange(t).astype(F32)[:, None] * inv_freq[None, :]
    cos = jnp.concatenate([jnp.cos(ang), jnp.cos(ang)], axis=1)
    sin = jnp.concatenate([-jnp.sin(ang), jnp.sin(ang)], axis=1)

    b128 = _pad_lanes(b_forget)
    sinks128 = _pad_lanes(swa_sinks)

    h1 = _pre_attn(x, g_pre_mix + ag_token[0:1, 0:1], sc_a, sh_a)
    w_in_g = gathered(0, h1, "in")
    o_fg, o_sq = n_fox3, n_fox3 + N_FOX

    def cols(lo, hi):
        parts = []
        for j in range(lo // in_w, (hi - 1) // in_w + 1):
            parts.append(w_in_g[j, :, max(lo - j * in_w, 0):min(hi - j * in_w, in_w)])
        return parts

    w_in_r = jnp.concatenate(cols(0, o_fg) + cols(o_sq, in_total) + cols(o_fg, o_sq)
                             + [jnp.zeros((d, FG_PAD - N_FOX), BF16)], axis=1)
    proj_a = _matmul(h1, w_in_r, name="proj_a", n_cols=n_fox3, n_off=0)
    proj_b = _matmul(h1, w_in_r, name="proj_b", n_cols=n_swa3, n_off=n_fox3, tn=512, row_extras=(cos, sin),
                     epilogue=lambda acc, j, cs, sn: (_rope_cols(acc, j, cs, sn, SWA_W + KV_W),))
    out_started = gathered_start(1, proj_a, "out")
    fg = _matmul(h1, w_in_r, name="proj_fg", n_cols=FG_PAD, n_off=n_fox3 + n_swa3, tn=FG_PAD,
                 out_dtypes=(F32,), after=out_started[1][3])[:, 0:LANES]
    cum, cumt = _cum_fwd(fg, b128)
    fox_o, fox_lse = _fox_fwd(proj_a, cum, cumt)
    up_started = gathered_start(2, fox_o, "up")
    swa_o, swa_lse = _swa_fwd(proj_b, sinks128 + up_started[1][3][0:1, 0:1])
    w_out_full = gathered_finish(out_started, swa_o, "out").reshape(d, d)
    attn = jnp.concatenate([fox_o, swa_o], axis=1)
    mix = _matmul(attn, w_out_full, name="out_proj", out_dtypes=(F32,))
    x2, h2 = _post_mix(x, mix, gt_a, g_post_mix, g_pre_mlp, sc_m, sh_m)
    w_up_g = gathered_finish(up_started, h2, "up")
    u, act = _matmul(h2, w_up_g, name="mlp_up", b_sharded=True, out_dtypes=(BF16, BF16),
                     epilogue=lambda acc, j: (acc, jnp.square(jnp.maximum(acc, 0.0))))
    w_down_full = gathered(3, act, "down").reshape(N_DEV * shard_ff, d)
    y = _matmul(act, w_down_full, name="mlp_down", out_dtypes=(F32,))

    core = jnp.reshape(ac, (1,)).astype(jnp.int32)

    def reduce_start(started, after, name):
        send, recv, src, land, _ = started
        full, from_sibling = _sib_wait(send, recv, src, land, after, "sib_wait_" + name)
        return _rs_start(_chip_sum(full, from_sibling, core, "chip_sum_" + name), "rs_start_" + name)

    def tok(started):
        return started[4][0:1, 0:1]

    idx = jnp.stack([2 * ax + ay, 2 * (1 - ax) + ay, 2 * ax + (1 - ay), 2 * (1 - ax) + (1 - ay)]).astype(jnp.int32)

    def reduce_finish(started, after, w, m, v, name):
        send, recv, src, land, _ = started
        own, landed = _rs_wait(send, recv, src, land, after, "rs_wait_" + name)
        return _adam_pieces(w, m[0], v[0], own, landed, idx, "adam_" + name)

    dy, dout, dgt_m, dg3, loss_vec = _final(y, x2, target, gt_m, g_post_mlp)
    du = _matmul(dy, w_down_full, name="d_act", tb=True, tile_extras=(u,),
                 epilogue=lambda acc, j, uu: (acc * (2.0 * jnp.maximum(uu.astype(F32), 0.0)),))
    dw_down = _matmul(act, dy, name="dw_down", ta=True)
    sb_down = _sib_start(dw_down.reshape(N_DEV, shard_ff, d), "sib_start_down")
    dh2 = _matmul(du, w_up_g, name="d_h2", tb=True, b_sharded=True, b_pair=True, out_dtypes=(F32,),
                  after=sb_down[4])
    rs_down = reduce_start(sb_down, dh2, "down")
    per = shard_ff // _blk(shard_ff, 1024)
    dw_up = _matmul(h2, du, name="dw_up", ta=True, tn=_blk(shard_ff, 1024), out_shape=(N_DEV, d, shard_ff),
                    out_map=lambda tm, tn: pl.BlockSpec((None, tm, tn), lambda i, j, kk: (j // per, i, j % per)),
                    after=rs_down[4])
    sb_up = _sib_start(dw_up, "sib_start_up")
    dmix, dx2, dsh_m, dsc_m, dg2, dgt_a, dg1 = _mid_bwd(
        dh2, dout, x2, mix, g_pre_mlp + tok(sb_up), sc_m, gt_a, g_post_mix)
    d_attn = _matmul(dmix, w_out_full, name="d_attn", tb=True)
    rs_up = reduce_start(sb_up, d_attn, "up")
    dw_out = _matmul(attn, dmix, name="dw_out", ta=True, after=rs_up[4])
    sb_out = _sib_start(dw_out.reshape(N_DEV, d // N_DEV, d), "sib_start_out")
    dqf, dkf, dvf, dcs = _fox_bwd(proj_a, d_attn, cum, cumt, fox_lse)
    dsq, dsk, dsv, dsinks = _swa_bwd(proj_b, d_attn, swa_lse, sinks128 + tok(sb_out), cos, sin)
    rs_out = reduce_start(sb_out, dsq, "out")
    dfg, db_forget = _fg_bwd(dcs, fg, b128 + tok(rs_out))
    dproj = jnp.concatenate([dqf, dkf, dvf, dsq, dsk, dsv, _pad_lanes(dfg, FG_PAD)], axis=1)
    dw_in_r = _matmul(dproj, h1, name="dw_in", ta=True)

    def shard_rows(j):
        lo, hi = j * in_w, (j + 1) * in_w
        parts = []
        for seg_lo, seg_hi, shift in ((0, o_fg, 0), (o_fg, o_sq, n_swa3), (o_sq, in_total, -N_FOX)):
            a, b = max(lo, seg_lo), min(hi, seg_hi)
            if a < b:
                parts.append(dw_in_r[a + shift:b + shift, :])
        return parts[0] if len(parts) == 1 else jnp.concatenate(parts, axis=0)

    sb_in = _sib_start(jnp.stack([shard_rows(j) for j in range(N_DEV)]), "sib_start_in")
    dh1 = _matmul(dproj, w_in_r, name="d_h1", tb=True, out_dtypes=(F32,), after=sb_in[4])
    grad_x, dsh_a, dsc_a, dg0 = _x_bwd(dh1, dx2, x, g_pre_mix, sc_a)

    small = jnp.concatenate([dsh_a, dsc_a, dgt_a, dsh_m, dsc_m, dgt_m, dg0, dg1, dg2, dg3, db_forget, dsinks,
                             loss_vec[:, 0:LANES]], axis=1)
    small_all = _all_gather([small], "gather_small")[0].reshape(N_DEV, small.shape[1])
    rs_in = reduce_start(sb_in, small_all, "in")

    pack = lambda bm, g0_, g1_, g2_, g3_, bf_, sk_: jnp.concatenate(
        [bm, g0_, g1_, g2_, g3_, _pad_lanes(bf_), _pad_lanes(sk_), jnp.zeros((1, LANES), F32)], axis=1)
    p_small = pack(b_mod, g_pre_mix, g_post_mix, g_pre_mlp, g_post_mlp, b_forget, swa_sinks)
    m_small = pack(m_b_mod, m_g_pre_mix, m_g_post_mix, m_g_pre_mlp, m_g_post_mlp, m_b_forget, m_swa_sinks)
    v_small = pack(v_b_mod, v_g_pre_mix, v_g_post_mix, v_g_pre_mlp, v_g_post_mlp, v_b_forget, v_swa_sinks)
    small_out = _adam_small(small_all, p_small + tok(rs_in), m_small, v_small)

    n_mod = 6 * d

    def unpack(vec):
        o = n_mod
        return (vec[:, 0:n_mod], vec[:, o:o + d], vec[:, o + d:o + 2 * d], vec[:, o + 2 * d:o + 3 * d],
                vec[:, o + 3 * d:o + 4 * d], vec[:, o + 4 * d:o + 4 * d + N_FOX],
                vec[:, o + 4 * d + LANES:o + 4 * d + LANES + N_SWA])

    loss = small_out[0][0, n_mod + 4 * d + 2 * LANES]
    g_small, d_small, nm_small, nv_small = [unpack(vec) for vec in small_out]

    dmod_cols = lax.dynamic_slice(small_all, (0, me * mod_w), (N_DEV, mod_w))
    g_w_mod, d_w_mod, nm_w_mod, nv_w_mod = _adam_mod(c_all + tok(rs_in), dmod_cols, w_mod, m_w_mod[0], v_w_mod[0])

    g_w_down, d_w_down, nm_w_down, nv_w_down = reduce_finish(rs_down, d_w_mod, w_down, m_w_down, v_w_down, "w_down")
    g_w_up, d_w_up, nm_w_up, nv_w_up = reduce_finish(rs_up, d_w_down, w_up, m_w_up, v_w_up, "w_up")
    g_w_out, d_w_out, nm_w_out, nv_w_out = reduce_finish(rs_out, d_w_up, w_out, m_w_out, v_w_out, "w_out")
    g_w_in, d_w_in, nm_w_in, nv_w_in = reduce_finish(rs_in, d_w_out, jnp.transpose(w_in), (jnp.transpose(m_w_in[0]),),
                                                     (jnp.transpose(v_w_in[0]),), "w_in")

    def assemble(w_mod_, small_, w_in_, w_out_, w_up_, w_down_):
        b_mod_, g0_, g1_, g2_, g3_, bf_, sk_ = small_
        return [w_mod_[None], b_mod_, g0_, g1_, jnp.transpose(w_in_)[None], bf_, sk_, w_out_[None], g2_, g3_,
                w_up_[None], w_down_[None]]

    outs = [loss, grad_x[None]]
    outs += assemble(g_w_mod, g_small, g_w_in, g_w_out, g_w_up, g_w_down)
    outs += assemble(d_w_mod, d_small, d_w_in, d_w_out, d_w_up, d_w_down)
    outs += assemble(nm_w_mod, nm_small, nm_w_in, nm_w_out, nm_w_up, nm_w_down)
    outs += assemble(nv_w_mod, nv_small, nv_w_in, nv_w_out, nv_w_up, nv_w_down)
    return tuple(outs)
```

```python
import functools

import jax
import jax.numpy as jnp
from jax import lax
from jax.experimental import pallas as pl
from jax.experimental.pallas import tpu as pltpu

F32 = jnp.float32
BF16 = jnp.bfloat16
MESH = pl.DeviceIdType.MESH

N_DEV = 8
N_CHIP = 4
LANES = 128
HEAD_DIM = 128
N_FOX = 8
N_SWA = 8
N_KV = 2
GQA = N_SWA // N_KV
WINDOW = 128
FOX_W = N_FOX * HEAD_DIM
SWA_W = N_SWA * HEAD_DIM
KV_W = N_KV * HEAD_DIM
ROPE_THETA = 10000.0
NORM_EPS = 1e-6
ATT_SCALE = HEAD_DIM ** -0.5
FG_PAD = 512

ADAM_LR = 0.001
ADAM_B1 = 0.9
ADAM_B2 = 0.999
ADAM_EPS = 1e-08
ADAM_WD = 0.01
ADAM_STEP = 10

VMEM_LIMIT = 56 * 1024 * 1024

NT_DIMS = (((1,), (1,)), ((), ()))
TN_DIMS = (((0,), (0,)), ((), ()))
NN_DIMS = (((1,), (0,)), ((), ()))


def _pcall(body, *, name, out_shape, grid=(), in_specs=None, out_specs=None, scratch_shapes=(), grid_spec=None):
    params = pltpu.CompilerParams(vmem_limit_bytes=VMEM_LIMIT)
    if grid_spec is not None:
        return pl.pallas_call(body, name=name, out_shape=out_shape, grid_spec=grid_spec, compiler_params=params)
    return pl.pallas_call(body, name=name, out_shape=out_shape, grid=grid, in_specs=in_specs, out_specs=out_specs,
                          scratch_shapes=scratch_shapes, compiler_params=params)


def _blk(n, pref):
    if n <= pref:
        return n
    b = (pref // LANES) * LANES
    while n % b:
        b -= LANES
    return b


def _position():
    return lax.axis_index("x"), lax.axis_index("y"), lax.axis_index("c")


ANY = pl.BlockSpec(memory_space=pl.ANY)


def _all_gather(arrs, name):
    n = len(arrs)

    def body(*refs):
        ins, outs = refs[:n], refs[n:2 * n]
        send_sems, recv_sems, local_sems = refs[2 * n:]
        x, y, c = _position()
        me, sibling = (x, y, c), (x, y, 1 - c)
        chips = [(1 - x, y), (x, 1 - y), (1 - x, 1 - y)]

        def slot(p):
            return 4 * p[0] + 2 * p[1] + p[2]

        def copy(a, k, block, to, src=None):
            dst = outs[a].at[slot(block)]
            return pltpu.make_async_remote_copy(
                src_ref=dst if src is None else src, dst_ref=dst,
                send_sem=send_sems.at[7 * a + k], recv_sem=recv_sems.at[7 * a + k],
                device_id=to, device_id_type=MESH)

        mine = [pltpu.make_async_copy(ins[a], outs[a].at[slot(me)], local_sems.at[a]) for a in range(n)]
        for cp in mine:
            cp.start()
        first = []
        for a in range(n):
            first.append(copy(a, 0, me, sibling, src=ins[a]))
            first += [copy(a, 1 + j, me, (*chip, c), src=ins[a]) for j, chip in enumerate(chips)]
        for cp in first:
            cp.start()
        passed = []
        for a in range(n):
            for j, chip in enumerate(chips):
                copy(a, 1 + j, (*chip, c), me).wait_recv()
                cp = copy(a, 4 + j, (*chip, c), sibling)
                cp.start()
                passed.append(cp)
        for a in range(n):
            copy(a, 0, sibling, me).wait_recv()
            for j, chip in enumerate(chips):
                copy(a, 4 + j, (*chip, 1 - c), me).wait_recv()
        for cp in first + passed:
            cp.wait_send()
        for cp in mine:
            cp.wait()

    return _pcall(
        body, name=name,
        out_shape=[jax.ShapeDtypeStruct((N_DEV,) + a.shape, a.dtype) for a in arrs],
        in_specs=[ANY] * n, out_specs=[ANY] * n,
        scratch_shapes=[pltpu.SemaphoreType.DMA((7 * n,)), pltpu.SemaphoreType.DMA((7 * n,)),
                        pltpu.SemaphoreType.DMA((n,))],
    )(*arrs)


HBM = pl.BlockSpec(memory_space=pltpu.HBM)
SEM = pl.BlockSpec(memory_space=pltpu.SEMAPHORE)
EFFECT = pltpu.SideEffectType.DATAFLOW_SIDE_EFFECTING


def _hbm(a):
    return pltpu.with_memory_space_constraint(a, pltpu.HBM)


def _gather_peers():
    x, y, c = _position()
    return [(x, y, 1 - c), (1 - x, y, c), (x, 1 - y, c), (1 - x, 1 - y, c)]


def _ag_start(shards, name):
    n = len(shards)
    lands = [_hbm(lax.empty((N_DEV,) + s.shape, s.dtype)) for s in shards]

    def body(*refs):
        srcs, land, send, recv = refs[:n], refs[n:2 * n], refs[2 * n:3 * n], refs[3 * n:4 * n]
        token = refs[6 * n]
        x, y, c = _position()
        for a in range(n):
            for k, to in enumerate(_gather_peers()):
                pltpu.make_async_remote_copy(
                    src_ref=srcs[a], dst_ref=land[a].at[4 * x + 2 * y + c], send_sem=send[a].at[k],
                    recv_sem=recv[a].at[k], device_id=to, device_id_type=MESH).start()
        token[...] = jnp.zeros_like(token)

    sems = [pltpu.SemaphoreType.DMA((4,))] * (2 * n)
    out = pl.pallas_call(
        body, name=name,
        out_shape=sems + [pltpu.HBM(s.shape, s.dtype) for s in shards] + [pltpu.HBM(l.shape, l.dtype) for l in lands]
        + [jax.ShapeDtypeStruct((8, LANES), F32)],
        in_specs=[HBM] * (2 * n), out_specs=[SEM] * (2 * n) + [HBM] * (2 * n) + [pl.BlockSpec(memory_space=pltpu.VMEM)],
        input_output_aliases={**{a: 2 * n + a for a in range(n)}, **{n + a: 3 * n + a for a in range(n)}},
        compiler_params=pltpu.CompilerParams(has_side_effects=EFFECT),
    )(*[_hbm(s) for s in shards], *lands)
    return out[:n], out[n:2 * n], out[2 * n:3 * n], out[3 * n:4 * n], out[4 * n]


def _ag_wait(send, recv, shard_thru, land_thru, after, name):
    def body(v_ref, land_ref, send_sem, recv_sem, after_ref, v_dead, got_ref):
        for k, to in enumerate(_gather_peers()):
            cp = pltpu.make_async_remote_copy(
                src_ref=v_ref, dst_ref=land_ref.at[0], send_sem=send_sem.at[k], recv_sem=recv_sem.at[k],
                device_id=to, device_id_type=MESH)
            cp.wait_send()
            cp.wait_recv()

    return pl.pallas_call(
        body, name=name,
        out_shape=(pltpu.HBM(shard_thru.shape, shard_thru.dtype), pltpu.HBM(land_thru.shape, land_thru.dtype)),
        in_specs=(HBM, HBM, SEM, SEM, ANY), out_specs=(HBM, HBM), input_output_aliases={0: 0, 1: 1},
        compiler_params=pltpu.CompilerParams(has_side_effects=EFFECT),
    )(shard_thru, land_thru, send, recv, after)


def _ag_forward(land, name):
    def body(land_in, land_ref, send_sems, recv_sems):
        x, y, c = _position()
        copies = []
        for j, (px, py) in enumerate([(1 - x, y), (x, 1 - y), (1 - x, 1 - y)]):
            block = land_ref.at[4 * px + 2 * py + c]
            cp = pltpu.make_async_remote_copy(src_ref=block, dst_ref=block, send_sem=send_sems.at[j],
                                              recv_sem=recv_sems.at[j], device_id=(x, y, 1 - c), device_id_type=MESH)
            cp.start()
            copies.append(cp)
        for cp in copies:
            cp.wait()

    return pl.pallas_call(
        body, name=name, out_shape=jax.ShapeDtypeStruct(land.shape, land.dtype),
        in_specs=[ANY], out_specs=ANY, input_output_aliases={0: 0},
        scratch_shapes=[pltpu.SemaphoreType.DMA((3,)), pltpu.SemaphoreType.DMA((3,))],
    )(land)


def _fwd_start(land, name):
    def body(land_ref, send, recv, land_thru, token):
        x, y, c = _position()
        for j, (px, py) in enumerate([(1 - x, y), (x, 1 - y), (1 - x, 1 - y)]):
            block = land_ref.at[4 * px + 2 * py + c]
            pltpu.make_async_remote_copy(src_ref=block, dst_ref=block, send_sem=send.at[j], recv_sem=recv.at[j],
                                         device_id=(x, y, 1 - c), device_id_type=MESH).start()
        token[...] = jnp.zeros_like(token)

    return pl.pallas_call(
        body, name=name,
        out_shape=[pltpu.SemaphoreType.DMA((3,)), pltpu.SemaphoreType.DMA((3,)), pltpu.HBM(land.shape, land.dtype),
                   jax.ShapeDtypeStruct((8, LANES), F32)],
        in_specs=[HBM], out_specs=[SEM, SEM, HBM, pl.BlockSpec(memory_space=pltpu.VMEM)],
        input_output_aliases={0: 2},
        compiler_params=pltpu.CompilerParams(has_side_effects=EFFECT),
    )(land)


def _fwd_wait(send, recv, land_thru, after, name):
    def body(land_ref, send_sem, recv_sem, after_ref, land_out):
        x, y, c = _position()
        for j in range(3):
            cp = pltpu.make_async_remote_copy(
                src_ref=land_ref.at[0], dst_ref=land_ref.at[0], send_sem=send_sem.at[j], recv_sem=recv_sem.at[j],
                device_id=(x, y, 1 - c), device_id_type=MESH)
            cp.wait_send()
            cp.wait_recv()

    return pl.pallas_call(
        body, name=name, out_shape=pltpu.HBM(land_thru.shape, land_thru.dtype),
        in_specs=(HBM, SEM, SEM, ANY), out_specs=HBM, input_output_aliases={0: 0},
        compiler_params=pltpu.CompilerParams(has_side_effects=EFFECT),
    )(land_thru, send, recv, after)


def _rs_peers():
    x, y, c = _position()
    return [(1 - x, y, c), (x, 1 - y, c), (1 - x, 1 - y, c)]


def _rs_start(chip_sums, name):
    land = _hbm(lax.empty(chip_sums.shape, chip_sums.dtype))

    def body(src, land_ref, send, recv, src_thru, land_thru, token):
        x, y, c = _position()
        for j, (px, py, pc) in enumerate(_rs_peers()):
            pltpu.make_async_remote_copy(
                src_ref=src.at[2 * px + py], dst_ref=land_ref.at[2 * x + y], send_sem=send.at[j], recv_sem=recv.at[j],
                device_id=(px, py, pc), device_id_type=MESH).start()
        token[...] = jnp.zeros_like(token)

    return pl.pallas_call(
        body, name=name,
        out_shape=[pltpu.SemaphoreType.DMA((3,)), pltpu.SemaphoreType.DMA((3,)),
                   pltpu.HBM(chip_sums.shape, chip_sums.dtype), pltpu.HBM(land.shape, land.dtype),
                   jax.ShapeDtypeStruct((8, LANES), F32)],
        in_specs=[HBM, HBM], out_specs=[SEM, SEM, HBM, HBM, pl.BlockSpec(memory_space=pltpu.VMEM)],
        input_output_aliases={0: 2, 1: 3},
        compiler_params=pltpu.CompilerParams(has_side_effects=EFFECT),
    )(_hbm(chip_sums), land)


def _rs_wait(send, recv, src_thru, land_thru, after, name):
    def body(src, land_ref, send_sem, recv_sem, after_ref, src_out, land_out):
        for j, to in enumerate(_rs_peers()):
            cp = pltpu.make_async_remote_copy(
                src_ref=src.at[0], dst_ref=land_ref.at[0], send_sem=send_sem.at[j], recv_sem=recv_sem.at[j],
                device_id=to, device_id_type=MESH)
            cp.wait_send()
            cp.wait_recv()

    return pl.pallas_call(
        body, name=name,
        out_shape=(pltpu.HBM(src_thru.shape, src_thru.dtype), pltpu.HBM(land_thru.shape, land_thru.dtype)),
        in_specs=(HBM, HBM, SEM, SEM, ANY), out_specs=(HBM, HBM), input_output_aliases={0: 0, 1: 1},
        compiler_params=pltpu.CompilerParams(has_side_effects=EFFECT),
    )(src_thru, land_thru, send, recv, after)


def _sib_start(full, name):
    land = _hbm(lax.empty((N_CHIP,) + full.shape[1:], full.dtype))

    def body(src, land_ref, send, recv, src_thru, land_thru, token):
        x, y, c = _position()
        for k in range(N_CHIP):
            pltpu.make_async_remote_copy(
                src_ref=src.at[2 * k + (1 - c)], dst_ref=land_ref.at[k], send_sem=send.at[k], recv_sem=recv.at[k],
                device_id=(x, y, 1 - c), device_id_type=MESH).start()
        token[...] = jnp.zeros_like(token)

    return pl.pallas_call(
        body, name=name,
        out_shape=[pltpu.SemaphoreType.DMA((N_CHIP,)), pltpu.SemaphoreType.DMA((N_CHIP,)),
                   pltpu.HBM(full.shape, full.dtype), pltpu.HBM(land.shape, land.dtype),
                   jax.ShapeDtypeStruct((8, LANES), F32)],
        in_specs=[HBM, HBM], out_specs=[SEM, SEM, HBM, HBM, pl.BlockSpec(memory_space=pltpu.VMEM)],
        input_output_aliases={0: 2, 1: 3},
        compiler_params=pltpu.CompilerParams(has_side_effects=EFFECT),
    )(_hbm(full), land)


def _sib_wait(send, recv, src_thru, land_thru, after, name):
    def body(src, land_ref, send_sem, recv_sem, after_ref, src_out, land_out):
        x, y, c = _position()
        for k in range(N_CHIP):
            cp = pltpu.make_async_remote_copy(
                src_ref=src.at[0], dst_ref=land_ref.at[0], send_sem=send_sem.at[k], recv_sem=recv_sem.at[k],
                device_id=(x, y, 1 - c), device_id_type=MESH)
            cp.wait_send()
            cp.wait_recv()

    return pl.pallas_call(
        body, name=name,
        out_shape=(pltpu.HBM(src_thru.shape, src_thru.dtype), pltpu.HBM(land_thru.shape, land_thru.dtype)),
        in_specs=(HBM, HBM, SEM, SEM, ANY), out_specs=(HBM, HBM), input_output_aliases={0: 0, 1: 1},
        compiler_params=pltpu.CompilerParams(has_side_effects=EFFECT),
    )(src_thru, land_thru, send, recv, after)


def _rs_sibling(arrs, name):
    n = len(arrs)

    def body(*refs):
        ins, outs = refs[:n], refs[n:2 * n]
        send_sems, recv_sems = refs[2 * n:]
        x, y, c = _position()
        copies = []
        for a in range(n):
            for k in range(N_CHIP):
                cp = pltpu.make_async_remote_copy(
                    src_ref=ins[a].at[2 * k + (1 - c)], dst_ref=outs[a].at[k],
                    send_sem=send_sems.at[N_CHIP * a + k], recv_sem=recv_sems.at[N_CHIP * a + k],
                    device_id=(x, y, 1 - c), device_id_type=MESH)
                cp.start()
                copies.append(cp)
        for cp in copies:
            cp.wait()

    return _pcall(
        body, name=name,
        out_shape=[jax.ShapeDtypeStruct((N_CHIP,) + a.shape[1:], a.dtype) for a in arrs],
        in_specs=[ANY] * n, out_specs=[ANY] * n,
        scratch_shapes=[pltpu.SemaphoreType.DMA((N_CHIP * n,)), pltpu.SemaphoreType.DMA((N_CHIP * n,))],
    )(*arrs)


def _chip_sum(full, recv, core, name):
    _, rows, cols = full.shape
    tr = _blk(rows, 1024) if rows % LANES == 0 else rows

    def body(core_ref, a_ref, b_ref, o_ref):
        o_ref[...] = (a_ref[...].astype(F32) + b_ref[...].astype(F32)).astype(o_ref.dtype)

    grid_spec = pltpu.PrefetchScalarGridSpec(
        num_scalar_prefetch=1, grid=(N_CHIP, rows // tr),
        in_specs=[pl.BlockSpec((None, tr, cols), lambda k, i, core_ref: (2 * k + core_ref[0], i, 0)),
                  pl.BlockSpec((None, tr, cols), lambda k, i, core_ref: (k, i, 0))],
        out_specs=pl.BlockSpec((None, tr, cols), lambda k, i, core_ref: (k, i, 0)))
    return _pcall(body, name=name, out_shape=jax.ShapeDtypeStruct((N_CHIP, rows, cols), full.dtype),
                  grid_spec=grid_spec)(core, full, recv)


def _matmul(a, b, *, name, ta=False, tb=False, tm=1024, tn=1024, tk=2048, out_dtypes=(BF16,), epilogue=None,
            row_extras=(), tile_extras=(), out_shape=None, out_map=None, b_sharded=False, n_cols=None, n_off=0,
            after=None, b_pair=False):
    m, k = (a.shape[1], a.shape[0]) if ta else a.shape
    if b_sharded:
        shard_c = b.shape[2]
        n, kb = (b.shape[1], N_DEV * shard_c) if tb else (N_DEV * shard_c, b.shape[1])
        tn, tk = (tn, min(tk, shard_c)) if tb else (min(tn, shard_c), tk)
        if b_pair:
            assert tb and tk == shard_c
            tk = 2 * shard_c
    else:
        n, kb = b.shape if tb else (b.shape[1], b.shape[0])
    assert kb == k, (name, kb, k)
    if n_cols is not None:
        n = n_cols
    tm, tn, tk = _blk(m, tm), _blk(n, tn), _blk(k, tk)
    assert n_off % tn == 0
    nk = k // tk
    dims = (((0 if ta else 1,), (1 if tb else 0,)), ((), ()))
    behind = () if after is None else (after,)
    n_row, n_tile, n_out = len(row_extras), len(tile_extras), len(out_dtypes)
    n_b = 2 if b_pair else 1
    first_out = 1 + n_b + n_row + n_tile + len(behind)

    def body(*refs):
        a_ref, b_ref = refs[:2]
        extras = refs[1 + n_b:1 + n_b + n_row + n_tile]
        outs = refs[first_out:first_out + n_out]
        acc_ref = refs[-1]
        jj, kk = pl.program_id(1), pl.program_id(2)
        if b_pair:
            half = tk // 2
            part = (lax.dot_general(a_ref[:, 0:half].astype(BF16), b_ref[...].astype(BF16), dims,
                                    preferred_element_type=F32)
                    + lax.dot_general(a_ref[:, half:tk].astype(BF16), refs[2][...].astype(BF16), dims,
                                      preferred_element_type=F32))
        else:
            part = lax.dot_general(a_ref[...].astype(BF16), b_ref[...].astype(BF16), dims,
                                   preferred_element_type=F32)

        def finish(acc):
            res = (acc,) if epilogue is None else epilogue(acc, jj, *[e[...] for e in extras])
            for o_ref, r in zip(outs, res):
                o_ref[...] = r.astype(o_ref.dtype)

        if nk == 1:
            finish(part)
        else:
            @pl.when(kk == 0)
            def _():
                acc_ref[...] = part

            @pl.when(kk > 0)
            def _():
                acc_ref[...] += part

            @pl.when(kk == nk - 1)
            def _():
                finish(acc_ref[...])

    a_spec = pl.BlockSpec((tk, tm), lambda i, j, kk: (kk, i)) if ta else pl.BlockSpec((tm, tk), lambda i, j, kk: (i, kk))
    if b_pair:
        b_spec = pl.BlockSpec((None, tn, shard_c), lambda i, j, kk: (2 * kk, j, 0))
    elif b_sharded and tb:
        per = shard_c // tk
        b_spec = pl.BlockSpec((None, tn, tk), lambda i, j, kk: (kk // per, j, kk % per))
    elif b_sharded:
        per = shard_c // tn
        b_spec = pl.BlockSpec((None, tk, tn), lambda i, j, kk: (j // per, kk, j % per))
    elif tb:
        b_spec = pl.BlockSpec((tn, tk), lambda i, j, kk: (j + n_off // tn, kk))
    else:
        b_spec = pl.BlockSpec((tk, tn), lambda i, j, kk: (kk, j + n_off // tn))
    in_specs = [a_spec, b_spec]
    if b_pair:
        in_specs.append(pl.BlockSpec((None, tn, shard_c), lambda i, j, kk: (2 * kk + 1, j, 0)))
    in_specs += [pl.BlockSpec((tm, LANES), lambda i, j, kk: (i, 0)) for _ in row_extras]
    in_specs += [pl.BlockSpec((tm, tn), lambda i, j, kk: (i, j)) for _ in tile_extras]
    in_specs += [ANY for _ in behind]
    if out_map is None:
        out_specs = [pl.BlockSpec((tm, tn), lambda i, j, kk: (i, j)) for _ in out_dtypes]
        shapes = [jax.ShapeDtypeStruct((m, n), dt) for dt in out_dtypes]
    else:
        out_specs = [out_map(tm, tn)]
        shapes = [jax.ShapeDtypeStruct(out_shape, out_dtypes[0])]
    acc_shape = (tm, tn) if nk > 1 else (8, LANES)
    res = _pcall(body, name=name, out_shape=shapes, grid=(m // tm, n // tn, nk), in_specs=in_specs,
                 out_specs=out_specs, scratch_shapes=[pltpu.VMEM(acc_shape, F32)])(
                     a, *([b, b] if b_pair else [b]), *row_extras, *tile_extras, *behind)
    return res[0] if n_out == 1 else res


def _rope_cols(acc, j, cos, sin, n_rope):
    width = acc.shape[1]
    parts = []
    for g in range(width // HEAD_DIM):
        xg = acc[:, g * HEAD_DIM:(g + 1) * HEAD_DIM]
        roped = xg * cos + pltpu.roll(xg, HEAD_DIM // 2, 1) * sin
        parts.append(jnp.where(j * width + g * HEAD_DIM < n_rope, roped, xg))
    return jnp.concatenate(parts, axis=1) if len(parts) > 1 else parts[0]


def _silu(v):
    return v / (1.0 + jnp.exp(-v))


def _mod_part(c_all, w_mod, b_part):
    d, w = w_mod.shape
    tk = _blk(d, 512)

    def body(c_ref, w_ref, b_ref, o_ref):
        kk = pl.program_id(0)
        cond = _silu(c_ref[...]).astype(BF16)
        part = jnp.dot(cond, w_ref[...].astype(BF16), preferred_element_type=F32)

        @pl.when(kk == 0)
        def _():
            o_ref[...] = part + b_ref[...]

        @pl.when(kk > 0)
        def _():
            o_ref[...] += part

    return _pcall(body, name="mod_part", out_shape=jax.ShapeDtypeStruct((N_DEV, w), F32), grid=(d // tk,),
                  in_specs=[pl.BlockSpec((N_DEV, tk), lambda kk: (0, kk)), pl.BlockSpec((tk, w), lambda kk: (kk, 0)),
                            pl.BlockSpec((1, w), lambda kk: (0, 0))],
                  out_specs=pl.BlockSpec((N_DEV, w), lambda kk: (0, 0)))(c_all, w_mod, b_part)


def _row_call(body, name, t, d, tiled_in, vec_in, tiled_out_dtypes, n_vec_out, tr=256):
    tr = _blk(t, tr)
    tile = pl.BlockSpec((tr, d), lambda i: (i, 0))
    vec = pl.BlockSpec((1, d), lambda i: (0, 0))
    out_shape = [jax.ShapeDtypeStruct((t, d), dt) for dt in tiled_out_dtypes]
    out_shape += [jax.ShapeDtypeStruct((1, d), F32)] * n_vec_out
    return _pcall(body, name=name, out_shape=out_shape, grid=(t // tr,),
                  in_specs=[tile] * len(tiled_in) + [vec] * len(vec_in),
                  out_specs=[tile] * len(tiled_out_dtypes) + [vec] * n_vec_out)(*tiled_in, *vec_in)


def _accumulate(ref, val):
    @pl.when(pl.program_id(0) == 0)
    def _():
        ref[...] = val

    @pl.when(pl.program_id(0) > 0)
    def _():
        ref[...] += val


def _rsum(v):
    return jnp.sum(v, axis=0, keepdims=True)


def _rms(v):
    return lax.rsqrt(jnp.mean(v * v, axis=-1, keepdims=True) + NORM_EPS)


def _rms_bwd(vhat, r, dvhat):
    return r * (dvhat - vhat * jnp.mean(dvhat * vhat, axis=-1, keepdims=True))


def _pre_attn(x, g0, sc_a, sh_a):
    def body(x_ref, g_ref, sc_ref, sh_ref, h_ref):
        xv = x_ref[...]
        h_ref[...] = (xv * _rms(xv) * g_ref[...] * (1.0 + sc_ref[...]) + sh_ref[...]).astype(BF16)

    t, d = x.shape
    return _row_call(body, "pre_attn", t, d, [x], [g0, sc_a, sh_a], [BF16], 0)[0]


def _post_mix(x, mix, gt_a, g1, g2, sc_m, sh_m):
    def body(x_ref, mix_ref, gt_ref, g1_ref, g2_ref, sc_ref, sh_ref, x2_ref, h2_ref):
        mv = mix_ref[...]
        x2 = x_ref[...] + gt_ref[...] * (mv * _rms(mv) * g1_ref[...])
        x2_ref[...] = x2
        h2_ref[...] = (x2 * _rms(x2) * g2_ref[...] * (1.0 + sc_ref[...]) + sh_ref[...]).astype(BF16)

    t, d = x.shape
    return _row_call(body, "post_mix", t, d, [x, mix], [gt_a, g1, g2, sc_m, sh_m], [F32, BF16], 0)


def _final(y, x2, target, gt_m, g3):
    t, d = y.shape

    def body(y_ref, x2_ref, tg_ref, gt_ref, g3_ref, dy_ref, dout_ref, dgt_ref, dg3_ref, loss_ref):
        yv = y_ref[...]
        r = _rms(yv)
        yhat = yv * r
        n3 = yhat * g3_ref[...]
        err = x2_ref[...] + gt_ref[...] * n3 - tg_ref[...]
        _accumulate(loss_ref, jnp.zeros((1, d), F32) + 0.5 * jnp.sum(err * err) / d)
        dout = err * (1.0 / d)
        dout_ref[...] = dout
        _accumulate(dgt_ref, _rsum(dout * n3))
        dn3 = dout * gt_ref[...]
        _accumulate(dg3_ref, _rsum(dn3 * yhat))
        dy_ref[...] = _rms_bwd(yhat, r, dn3 * g3_ref[...]).astype(BF16)

    return _row_call(body, "final", t, d, [y, x2, target], [gt_m, g3], [BF16, F32], 3)


def _mid_bwd(dh2, dout, x2, mix, g2, sc_m, gt_a, g1):
    t, d = x2.shape

    def body(dh2_ref, dout_ref, x2_ref, mix_ref, g2_ref, sc_ref, gt_ref, g1_ref,
             dmix_ref, dx2_ref, dsh_ref, dsc_ref, dg2_ref, dgt_ref, dg1_ref):
        dh2v = dh2_ref[...]
        x2v = x2_ref[...]
        r2 = _rms(x2v)
        x2hat = x2v * r2
        _accumulate(dsh_ref, _rsum(dh2v))
        _accumulate(dsc_ref, _rsum(dh2v * (x2hat * g2_ref[...])))
        dn2 = dh2v * (1.0 + sc_ref[...])
        _accumulate(dg2_ref, _rsum(dn2 * x2hat))
        dx2 = dout_ref[...] + _rms_bwd(x2hat, r2, dn2 * g2_ref[...])
        dx2_ref[...] = dx2
        mv = mix_ref[...]
        r1 = _rms(mv)
        mhat = mv * r1
        _accumulate(dgt_ref, _rsum(dx2 * (mhat * g1_ref[...])))
        dn1 = dx2 * gt_ref[...]
        _accumulate(dg1_ref, _rsum(dn1 * mhat))
        dmix_ref[...] = _rms_bwd(mhat, r1, dn1 * g1_ref[...]).astype(BF16)

    return _row_call(body, "mid_bwd", t, d, [dh2, dout, x2, mix], [g2, sc_m, gt_a, g1], [BF16, F32], 5)


def _x_bwd(dh1, dx2, x, g0, sc_a):
    t, d = x.shape

    def body(dh1_ref, dx2_ref, x_ref, g0_ref, sc_ref, dx_ref, dsh_ref, dsc_ref, dg0_ref):
        dh1v = dh1_ref[...]
        xv = x_ref[...]
        r0 = _rms(xv)
        xhat = xv * r0
        _accumulate(dsh_ref, _rsum(dh1v))
        _accumulate(dsc_ref, _rsum(dh1v * (xhat * g0_ref[...])))
        dn0 = dh1v * (1.0 + sc_ref[...])
        _accumulate(dg0_ref, _rsum(dn0 * xhat))
        dx_ref[...] = dx2_ref[...] + _rms_bwd(xhat, r0, dn0 * g0_ref[...])

    return _row_call(body, "x_bwd", t, d, [dh1, dx2, x], [g0, sc_a], [F32], 3)


def _pick_lane(block, h):
    lane = lax.broadcasted_iota(jnp.int32, block.shape, 1)
    return jnp.sum(jnp.where(lane == h, block, 0.0), axis=1, keepdims=True)


def _put_lane(ref, rows, h, col):
    old = ref[rows, :]
    lane = lax.broadcasted_iota(jnp.int32, old.shape, 1)
    ref[rows, :] = jnp.where(lane == h, col, old)


def _tri(n, lower):
    r = lax.broadcasted_iota(jnp.int32, (n, n), 0)
    c = lax.broadcasted_iota(jnp.int32, (n, n), 1)
    return jnp.where((c <= r) if lower else (c >= r), 1.0, 0.0).astype(F32)


def _cum_fwd(fg, b128):
    t = fg.shape[0]
    nb = t // LANES

    def body(fg_ref, b_ref, cum_ref, cumt_ref):
        tri = _tri(LANES, True)
        carry = jnp.zeros((1, LANES), F32)
        for i in range(nb):
            z = fg_ref[i * LANES:(i + 1) * LANES, :] + b_ref[...]
            lf = jnp.minimum(z, 0.0) - jnp.log(1.0 + jnp.exp(-jnp.abs(z)))
            blk = jnp.dot(tri, lf, precision=lax.Precision.HIGHEST, preferred_element_type=F32) + carry
            cum_ref[i * LANES:(i + 1) * LANES, :] = blk
            carry = blk[LANES - 1:LANES, :]
        cumt_ref[...] = cum_ref[...].T[0:N_FOX, :]

    return _pcall(body, name="cum_fwd",
                  out_shape=[jax.ShapeDtypeStruct((t, LANES), F32), jax.ShapeDtypeStruct((N_FOX, t), F32)],
                  grid=(1,),
                  in_specs=[pl.BlockSpec((t, LANES), lambda i: (0, 0)), pl.BlockSpec((1, LANES), lambda i: (0, 0))],
                  out_specs=[pl.BlockSpec((t, LANES), lambda i: (0, 0)), pl.BlockSpec((N_FOX, t), lambda i: (0, 0))],
                  )(fg, b128)


def _fg_bwd(dcs_rows, fg, b128):
    t = fg.shape[0]
    nb = t // LANES

    def body(dcs_ref, fg_ref, b_ref, dfg_ref, db_ref, dcum_ref):
        dcum_ref[...] = -jnp.concatenate([dcs_ref[...], jnp.zeros((LANES - N_FOX, t), F32)], axis=0).T
        tri = _tri(LANES, False)
        carry = jnp.zeros((1, LANES), F32)
        db = jnp.zeros((1, LANES), F32)
        for i in reversed(range(nb)):
            rows = slice(i * LANES, (i + 1) * LANES)
            dlf = jnp.dot(tri, dcum_ref[rows, :], precision=lax.Precision.HIGHEST, preferred_element_type=F32) + carry
            carry = dlf[0:1, :]
            z = fg_ref[rows, :] + b_ref[...]
            dfg = dlf / (1.0 + jnp.exp(z))
            dfg_ref[rows, :] = dfg.astype(BF16)
            db = db + _rsum(dfg)
        db_ref[...] = db

    full = pl.BlockSpec((t, LANES), lambda i: (0, 0))
    vec = pl.BlockSpec((1, LANES), lambda i: (0, 0))
    return _pcall(body, name="fg_bwd",
                  out_shape=[jax.ShapeDtypeStruct((t, LANES), BF16), jax.ShapeDtypeStruct((1, LANES), F32)],
                  grid=(1,), in_specs=[pl.BlockSpec((N_FOX, t), lambda i: (0, 0)), full, vec], out_specs=[full, vec],
                  scratch_shapes=[pltpu.VMEM((t, LANES), F32)])(dcs_rows, fg, b128)


def _head_spec(t, col0, div=1):
    return pl.BlockSpec((t, HEAD_DIM), lambda h: (0, col0 + h // div))


def _fox_scores(q, k, cq, ck, i, tq, end):
    s = lax.dot_general(q, k, NT_DIMS, preferred_element_type=F32) * ATT_SCALE + cq - ck
    row = lax.broadcasted_iota(jnp.int32, (tq, end), 0) + i * tq
    col = lax.broadcasted_iota(jnp.int32, (tq, end), 1)
    return jnp.where(row >= col, s, -jnp.inf)


def _fox_fwd(proj_a, cum, cumt):
    t = proj_a.shape[0]
    tq = _blk(t, 512)
    nq = t // tq

    def body(q_ref, k_ref, v_ref, cum_ref, cumt_ref, o_ref, lse_ref):
        h = pl.program_id(0)
        cq_all = _pick_lane(cum_ref[...], h)
        ck_all = cumt_ref[pl.ds(h, 1), :]

        @pl.when(h == 0)
        def _():
            lse_ref[...] = jnp.zeros_like(lse_ref)

        for i in range(nq):
            rows, end = slice(i * tq, (i + 1) * tq), (i + 1) * tq
            s = _fox_scores(q_ref[rows, :], k_ref[0:end, :], cq_all[rows, :], ck_all[:, 0:end], i, tq, end)
            m = jnp.max(s, axis=1, keepdims=True)
            p = jnp.exp(s - m)
            l = jnp.sum(p, axis=1, keepdims=True)
            o = jnp.dot(p.astype(BF16), v_ref[0:end, :], preferred_element_type=F32) / l
            o_ref[rows, :] = o.astype(BF16)
            _put_lane(lse_ref, rows, h, m + jnp.log(l))

    nh = FOX_W // HEAD_DIM
    stat = pl.BlockSpec((t, LANES), lambda h: (0, 0))
    return _pcall(body, name="fox_fwd",
                  out_shape=[jax.ShapeDtypeStruct((t, FOX_W), BF16), jax.ShapeDtypeStruct((t, LANES), F32)],
                  grid=(N_FOX,),
                  in_specs=[_head_spec(t, 0), _head_spec(t, nh), _head_spec(t, 2 * nh), stat,
                            pl.BlockSpec((N_FOX, t), lambda h: (0, 0))],
                  out_specs=[_head_spec(t, 0), stat])(proj_a, proj_a, proj_a, cum, cumt)


def _fox_bwd(proj_a, d_attn, cum, cumt, lse):
    t = proj_a.shape[0]
    tq = _blk(t, 512)
    nq = t // tq

    def body(q_ref, k_ref, v_ref, do_ref, cum_ref, cumt_ref, lse_ref,
             dq_ref, dk_ref, dv_ref, dcs_ref, dk_acc, dv_acc, dcs_acc):
        h = pl.program_id(0)
        cq_all = _pick_lane(cum_ref[...], h)
        ck_all = cumt_ref[pl.ds(h, 1), :]
        lse_all = _pick_lane(lse_ref[...], h)
        dk_acc[...] = jnp.zeros_like(dk_acc)
        dv_acc[...] = jnp.zeros_like(dv_acc)
        dcs_acc[...] = jnp.zeros_like(dcs_acc)
        for i in range(nq):
            rows, end = slice(i * tq, (i + 1) * tq), (i + 1) * tq
            q, k, v, do = q_ref[rows, :], k_ref[0:end, :], v_ref[0:end, :], do_ref[rows, :]
            s = _fox_scores(q, k, cq_all[rows, :], ck_all[:, 0:end], i, tq, end)
            p = jnp.exp(s - lse_all[rows, :])
            dp = lax.dot_general(do, v, NT_DIMS, preferred_element_type=F32)
            ds = p * (dp - jnp.sum(p * dp, axis=1, keepdims=True))
            dcs_acc[:, 0:end] += jnp.sum(ds, axis=0, keepdims=True)
            ds = ds.astype(BF16)
            dq_ref[rows, :] = (jnp.dot(ds, k, preferred_element_type=F32) * ATT_SCALE).astype(BF16)
            dk_acc[0:end, :] += lax.dot_general(ds, q, TN_DIMS, preferred_element_type=F32)
            dv_acc[0:end, :] += lax.dot_general(p.astype(BF16), do, TN_DIMS, preferred_element_type=F32)
        dk_ref[...] = (dk_acc[...] * ATT_SCALE).astype(BF16)
        dv_ref[...] = dv_acc[...].astype(BF16)
        dcs_ref[pl.ds(h, 1), :] = dcs_acc[...]

    nh = FOX_W // HEAD_DIM
    stat = pl.BlockSpec((t, LANES), lambda h: (0, 0))
    rows8 = pl.BlockSpec((N_FOX, t), lambda h: (0, 0))
    head = _head_spec(t, 0)
    wide = jax.ShapeDtypeStruct((t, FOX_W), BF16)
    return _pcall(body, name="fox_bwd",
                  out_shape=[wide, wide, wide, jax.ShapeDtypeStruct((N_FOX, t), F32)],
                  grid=(N_FOX,),
                  in_specs=[_head_spec(t, 0), _head_spec(t, nh), _head_spec(t, 2 * nh), head, stat, rows8, stat],
                  out_specs=[head, head, head, rows8],
                  scratch_shapes=[pltpu.VMEM((t, HEAD_DIM), F32), pltpu.VMEM((t, HEAD_DIM), F32),
                                  pltpu.VMEM((1, t), F32)],
                  )(proj_a, proj_a, proj_a, d_attn, cum, cumt, lse)


def _swa_scores(q, k, i, tq, start, end):
    s = lax.dot_general(q, k, NT_DIMS, preferred_element_type=F32) * ATT_SCALE
    row = lax.broadcasted_iota(jnp.int32, (tq, end - start), 0) + i * tq
    col = lax.broadcasted_iota(jnp.int32, (tq, end - start), 1) + start
    diff = row - col
    return jnp.where((diff >= 0) & (diff < WINDOW), s, -jnp.inf)


def _swa_blocks(t):
    tq = _blk(t, 256)
    return tq, [(i, max(0, i * tq - WINDOW), (i + 1) * tq) for i in range(t // tq)]


def _swa_fwd(proj_b, sinks128):
    t = proj_b.shape[0]
    tq, blocks = _swa_blocks(t)

    def body(q_ref, k_ref, v_ref, sink_ref, o_ref, lse_ref):
        h = pl.program_id(0)
        sink = _pick_lane(sink_ref[...], h)

        @pl.when(h == 0)
        def _():
            lse_ref[...] = jnp.zeros_like(lse_ref)

        for i, start, end in blocks:
            rows = slice(i * tq, end)
            s = _swa_scores(q_ref[rows, :], k_ref[start:end, :], i, tq, start, end)
            m = jnp.maximum(jnp.max(s, axis=1, keepdims=True), sink)
            p = jnp.exp(s - m)
            l = jnp.sum(p, axis=1, keepdims=True) + jnp.exp(sink - m)
            o = jnp.dot(p.astype(BF16), v_ref[start:end, :], preferred_element_type=F32) / l
            o_ref[rows, :] = o.astype(BF16)
            _put_lane(lse_ref, rows, h, m + jnp.log(l))

    stat = pl.BlockSpec((t, LANES), lambda h: (0, 0))
    return _pcall(body, name="swa_fwd",
                  out_shape=[jax.ShapeDtypeStruct((t, SWA_W), BF16), jax.ShapeDtypeStruct((t, LANES), F32)],
                  grid=(N_SWA,),
                  in_specs=[_head_spec(t, 0), _head_spec(t, N_SWA, GQA), _head_spec(t, N_SWA + N_KV, GQA),
                            pl.BlockSpec((1, LANES), lambda h: (0, 0))],
                  out_specs=[_head_spec(t, 0), stat])(proj_b, proj_b, proj_b, sinks128)


def _rope_bwd(d, cos, sin):
    return d * cos + pltpu.roll(d * sin, HEAD_DIM // 2, 1)


def _swa_bwd(proj_b, d_attn, lse, sinks128, cos, sin):
    t = proj_b.shape[0]
    tq, blocks = _swa_blocks(t)

    def body(q_ref, k_ref, v_ref, do_ref, lse_ref, sink_ref, cos_ref, sin_ref,
             dq_ref, dk_ref, dv_ref, dsink_ref, dk_acc, dv_acc):
        h = pl.program_id(0)
        sink = _pick_lane(sink_ref[...], h)
        lse_all = _pick_lane(lse_ref[...], h)

        @pl.when(h == 0)
        def _():
            dsink_ref[...] = jnp.zeros_like(dsink_ref)

        @pl.when(h % GQA == 0)
        def _():
            dk_acc[...] = jnp.zeros_like(dk_acc)
            dv_acc[...] = jnp.zeros_like(dv_acc)

        dsink = jnp.zeros((1, 1), F32)
        for i, start, end in blocks:
            rows = slice(i * tq, end)
            q, k, v, do = q_ref[rows, :], k_ref[start:end, :], v_ref[start:end, :], do_ref[rows, :]
            s = _swa_scores(q, k, i, tq, start, end)
            p = jnp.exp(s - lse_all[rows, :])
            dp = lax.dot_general(do, v, NT_DIMS, preferred_element_type=F32)
            delta = jnp.sum(p * dp, axis=1, keepdims=True)
            ds = (p * (dp - delta)).astype(BF16)
            dq = jnp.dot(ds, k, preferred_element_type=F32) * ATT_SCALE
            dq_ref[rows, :] = _rope_bwd(dq, cos_ref[rows, :], sin_ref[rows, :]).astype(BF16)
            dk_acc[start:end, :] += lax.dot_general(ds, q, TN_DIMS, preferred_element_type=F32)
            dv_acc[start:end, :] += lax.dot_general(p.astype(BF16), do, TN_DIMS, preferred_element_type=F32)
            dsink = dsink - jnp.sum(jnp.exp(sink - lse_all[rows, :]) * delta, axis=0, keepdims=True)
        old = dsink_ref[...]
        lane = lax.broadcasted_iota(jnp.int32, old.shape, 1)
        dsink_ref[...] = jnp.where(lane == h, dsink, old)

        @pl.when(h % GQA == GQA - 1)
        def _():
            dk_ref[...] = _rope_bwd(dk_acc[...] * ATT_SCALE, cos_ref[...], sin_ref[...]).astype(BF16)
            dv_ref[...] = dv_acc[...].astype(BF16)

    stat = pl.BlockSpec((t, LANES), lambda h: (0, 0))
    vec = pl.BlockSpec((1, LANES), lambda h: (0, 0))
    head = _head_spec(t, 0)
    kv_out = _head_spec(t, 0, GQA)
    return _pcall(body, name="swa_bwd",
                  out_shape=[jax.ShapeDtypeStruct((t, SWA_W), BF16), jax.ShapeDtypeStruct((t, KV_W), BF16),
                             jax.ShapeDtypeStruct((t, KV_W), BF16), jax.ShapeDtypeStruct((1, LANES), F32)],
                  grid=(N_SWA,),
                  in_specs=[head, _head_spec(t, N_SWA, GQA), _head_spec(t, N_SWA + N_KV, GQA),
                            _head_spec(t, N_FOX), stat, vec, stat, stat],
                  out_specs=[head, kv_out, kv_out, vec],
                  scratch_shapes=[pltpu.VMEM((t, HEAD_DIM), F32), pltpu.VMEM((t, HEAD_DIM), F32)],
                  )(proj_b, proj_b, proj_b, d_attn, lse, sinks128, cos, sin)


def _adamw(w, g, m, v):
    m = ADAM_B1 * m + (1.0 - ADAM_B1) * g
    v = ADAM_B2 * v + (1.0 - ADAM_B2) * (g * g)
    m_hat = m / (1.0 - ADAM_B1 ** ADAM_STEP)
    v_hat = v / (1.0 - ADAM_B2 ** ADAM_STEP)
    delta = -ADAM_LR * (m_hat / (jnp.sqrt(v_hat) + ADAM_EPS) + ADAM_WD * w)
    return delta, m, v


def _adam_pieces(w, m, v, own, land, idx, name):
    rows, cols = w.shape
    tr, tc = (256, cols) if rows % 256 == 0 else (rows, _blk(cols, 512))

    def body(idx_ref, own_ref, l1_ref, l2_ref, l3_ref, w_ref, m_ref, v_ref, g_ref, d_ref, mo_ref, vo_ref):
        g = own_ref[...].astype(F32) + l1_ref[...].astype(F32) + l2_ref[...].astype(F32) + l3_ref[...].astype(F32)
        g_ref[...] = g
        d_ref[...], mo_ref[...], vo_ref[...] = _adamw(w_ref[...], g, m_ref[...], v_ref[...])

    def piece(p):
        return pl.BlockSpec((None, tr, tc), lambda i, j, idx_ref: (idx_ref[p], i, j))

    tile = pl.BlockSpec((tr, tc), lambda i, j, idx_ref: (i, j))
    out = jax.ShapeDtypeStruct((rows, cols), F32)
    grid_spec = pltpu.PrefetchScalarGridSpec(
        num_scalar_prefetch=1, grid=(rows // tr, cols // tc),
        in_specs=[piece(0), piece(1), piece(2), piece(3), tile, tile, tile], out_specs=[tile] * 4)
    return _pcall(body, name=name, out_shape=[out] * 4, grid_spec=grid_spec)(idx, own, land, land, land, w, m, v)


def _adam_mod(c_all, dmod_cols, w, m, v):
    rows, cols = w.shape
    tr = _blk(rows, 256)

    def body(c_ref, dm_ref, w_ref, m_ref, v_ref, g_ref, d_ref, mo_ref, vo_ref):
        cond = _silu(c_ref[...]).astype(BF16)
        g = lax.dot_general(cond, dm_ref[...].astype(BF16), TN_DIMS, preferred_element_type=F32)
        g_ref[...] = g
        d_ref[...], mo_ref[...], vo_ref[...] = _adamw(w_ref[...], g, m_ref[...], v_ref[...])

    tile = pl.BlockSpec((tr, cols), lambda i: (i, 0))
    out = jax.ShapeDtypeStruct((rows, cols), F32)
    return _pcall(body, name="adam_mod", out_shape=[out] * 4, grid=(rows // tr,),
                  in_specs=[pl.BlockSpec((N_DEV, tr), lambda i: (0, i)), pl.BlockSpec((N_DEV, cols), lambda i: (0, 0)),
                            tile, tile, tile],
                  out_specs=[tile] * 4)(c_all, dmod_cols, w, m, v)


def _sum_pieces(own, land, idx, name):
    _, rows, cols = own.shape
    tc = _blk(cols, 512)

    def body(idx_ref, own_ref, l1_ref, l2_ref, l3_ref, g_ref):
        g_ref[...] = (own_ref[...].astype(F32) + l1_ref[...].astype(F32) + l2_ref[...].astype(F32)
                      + l3_ref[...].astype(F32))

    def piece(p):
        return pl.BlockSpec((None, rows, tc), lambda j, idx_ref: (idx_ref[p], 0, j))

    grid_spec = pltpu.PrefetchScalarGridSpec(
        num_scalar_prefetch=1, grid=(cols // tc,), in_specs=[piece(0), piece(1), piece(2), piece(3)],
        out_specs=pl.BlockSpec((rows, tc), lambda j, idx_ref: (0, j)))
    return _pcall(body, name=name, out_shape=jax.ShapeDtypeStruct((rows, cols), F32),
                  grid_spec=grid_spec)(idx, own, land, land, land)


def _adam_rows(w, m, v, g, name):
    rows, cols = w.shape
    tr = rows // 2

    def body(w_ref, m_ref, v_ref, g_ref, d_ref, mo_ref, vo_ref):
        d_ref[...], mo_ref[...], vo_ref[...] = _adamw(w_ref[...], g_ref[...], m_ref[...], v_ref[...])

    tile = pl.BlockSpec((tr, cols), lambda i: (i, 0))
    out = jax.ShapeDtypeStruct((rows, cols), F32)
    return _pcall(body, name=name, out_shape=[out] * 3, grid=(rows // tr,), in_specs=[tile] * 4,
                  out_specs=[tile] * 3)(w, m, v, g)


def _adam_small(parts, w, m, v):
    nv = w.shape[1]

    def body(p_ref, w_ref, m_ref, v_ref, g_ref, d_ref, mo_ref, vo_ref):
        g = p_ref[0:1, :]
        for k in range(1, N_DEV):
            g = g + p_ref[k:k + 1, :]
        g_ref[...] = g
        d_ref[...], mo_ref[...], vo_ref[...] = _adamw(w_ref[...], g, m_ref[...], v_ref[...])

    vec = pl.BlockSpec((1, nv), lambda i: (0, 0))
    out = jax.ShapeDtypeStruct((1, nv), F32)
    return _pcall(body, name="adam_small", out_shape=[out] * 4, grid=(1,),
                  in_specs=[pl.BlockSpec((N_DEV, nv), lambda i: (0, 0)), vec, vec, vec],
                  out_specs=[vec] * 4)(parts, w, m, v)


def _pad_lanes(v, width=LANES):
    return jnp.pad(v, ((0, 0), (0, width - v.shape[1])))


def kernel(x, c, w_mod, b_mod, g_pre_mix, g_post_mix, w_in, b_forget, swa_sinks, w_out, g_pre_mlp, g_post_mlp, w_up, w_down, loss_target, m_w_mod, m_b_mod, m_g_pre_mix, m_g_post_mix, m_w_in, m_b_forget, m_swa_sinks, m_w_out, m_g_pre_mlp, m_g_post_mlp, m_w_up, m_w_down, v_w_mod, v_b_mod, v_g_pre_mix, v_g_post_mix, v_w_in, v_b_forget, v_swa_sinks, v_w_out, v_g_pre_mlp, v_g_post_mlp, v_w_up, v_w_down):
    ax, ay, ac = _position()
    me = 4 * ax + 2 * ay + ac
    x, target = x[0], loss_target[0]
    t, d = x.shape
    w_mod, w_in, w_out, w_up, w_down = w_mod[0], w_in[0], w_out[0], w_up[0], w_down[0]
    mod_w = w_mod.shape[1]
    in_w = w_in.shape[1]
    in_total = N_DEV * in_w
    shard_ff = w_up.shape[1]
    n_fox3 = 3 * FOX_W
    n_swa3 = SWA_W + 2 * KV_W
    assert in_total == n_fox3 + N_FOX + n_swa3 and d == FOX_W + SWA_W

    c_all = _all_gather([c], "gather_c")[0].reshape(N_DEV, d)
    b_part = lax.dynamic_slice(b_mod, (0, me * mod_w), (1, mod_w))
    mod_parts = _all_gather([_mod_part(c_all, w_mod, b_part)], "gather_mod")[0]
    mod = lax.dynamic_index_in_dim(mod_parts, me, axis=1, keepdims=False).reshape(1, N_DEV * mod_w)

    w_in_b, mod = lax.optimization_barrier((w_in.astype(BF16), mod))
    first = _ag_start([w_in_b], "ag_start_in")
    behind_first = first[4][0, 0]
    rest = _ag_start([(w + behind_first).astype(BF16) for w in (w_out, w_up, w_down)], "ag_start_rest")
    ag_send, ag_recv, ag_shard, ag_land = [a + b for a, b in zip(first[:4], rest[:4])]
    ag_token = rest[4]

    def gathered(i, after, name):
        shard, land = _ag_wait(ag_send[i], ag_recv[i], ag_shard[i], ag_land[i], after, "ag_wait_" + name)
        return lax.dynamic_update_slice(_ag_forward(land, "ag_fwd_" + name), shard[None], (me, 0, 0))

    def gathered_start(i, after, name):
        shard, land = _ag_wait(ag_send[i], ag_recv[i], ag_shard[i], ag_land[i], after, "ag_wait_" + name)
        return shard, _fwd_start(land, "fwd_start_" + name)

    def gathered_finish(started, after, name):
        shard, (send, recv, land, _) = started
        return lax.dynamic_update_slice(_fwd_wait(send, recv, land, after, "fwd_wait_" + name), shard[None], (me, 0, 0))

    sh_a, sc_a, gt_a, sh_m, sc_m, gt_m = [mod[:, i * d:(i + 1) * d] for i in range(6)]

    half = HEAD_DIM // 2
    inv_freq = 1.0 / (ROPE_THETA ** (jnp.arange(half, dtype=F32) * (2.0 / HEAD_DIM)))
    ang = jnp.arange(t).astype(F32)[:, None] * inv_freq[None, :]
    cos = jnp.concatenate([jnp.cos(ang), jnp.cos(ang)], axis=1)
    sin = jnp.concatenate([-jnp.sin(ang), jnp.sin(ang)], axis=1)

    b128 = _pad_lanes(b_forget)
    sinks128 = _pad_lanes(swa_sinks)

    h1 = _pre_attn(x, g_pre_mix + ag_token[0:1, 0:1], sc_a, sh_a)
    w_in_g = gathered(0, h1, "in")
    o_fg, o_sq = n_fox3, n_fox3 + N_FOX

    def cols(lo, hi):
        parts = []
        for j in range(lo // in_w, (hi - 1) // in_w + 1):
            parts.append(w_in_g[j, :, max(lo - j * in_w, 0):min(hi - j * in_w, in_w)])
        return parts

    w_in_r = jnp.concatenate(cols(0, o_fg) + cols(o_sq, in_total) + cols(o_fg, o_sq)
                             + [jnp.zeros((d, FG_PAD - N_FOX), BF16)], axis=1)
    proj_a = _matmul(h1, w_in_r, name="proj_a", n_cols=n_fox3, n_off=0)
    proj_b = _matmul(h1, w_in_r, name="proj_b", n_cols=n_swa3, n_off=n_fox3, tn=512, row_extras=(cos, sin),
                     epilogue=lambda acc, j, cs, sn: (_rope_cols(acc, j, cs, sn, SWA_W + KV_W),))
    out_started = gathered_start(1, proj_a, "out")
    fg = _matmul(h1, w_in_r, name="proj_fg", n_cols=FG_PAD, n_off=n_fox3 + n_swa3, tn=FG_PAD,
                 out_dtypes=(F32,), after=out_started[1][3])[:, 0:LANES]
    cum, cumt = _cum_fwd(fg, b128)
    fox_o, fox_lse = _fox_fwd(proj_a, cum, cumt)
    up_started = gathered_start(2, fox_o, "up")
    swa_o, swa_lse = _swa_fwd(proj_b, sinks128 + up_started[1][3][0:1, 0:1])
    w_out_full = gathered_finish(out_started, swa_o, "out").reshape(d, d)
    attn = jnp.concatenate([fox_o, swa_o], axis=1)
    mix = _matmul(attn, w_out_full, name="out_proj", out_dtypes=(F32,))
    x2, h2 = _post_mix(x, mix, gt_a, g_post_mix, g_pre_mlp, sc_m, sh_m)
    w_up_g = gathered_finish(up_started, h2, "up")
    u, act = _matmul(h2, w_up_g, name="mlp_up", b_sharded=True, out_dtypes=(BF16, BF16),
                     epilogue=lambda acc, j: (acc, jnp.square(jnp.maximum(acc, 0.0))))
    w_down_full = gathered(3, act, "down").reshape(N_DEV * shard_ff, d)
    y = _matmul(act, w_down_full, name="mlp_down", tk=4096, out_dtypes=(F32,))

    core = jnp.reshape(ac, (1,)).astype(jnp.int32)

    def reduce_start(started, after, name):
        send, recv, src, land, _ = started
        full, from_sibling = _sib_wait(send, recv, src, land, after, "sib_wait_" + name)
        return _rs_start(_chip_sum(full, from_sibling, core, "chip_sum_" + name), "rs_start_" + name)

    def tok(started):
        return started[4][0:1, 0:1]

    idx = jnp.stack([2 * ax + ay, 2 * (1 - ax) + ay, 2 * ax + (1 - ay), 2 * (1 - ax) + (1 - ay)]).astype(jnp.int32)

    def reduce_finish(started, after, w, m, v, name):
        send, recv, src, land, _ = started
        own, landed = _rs_wait(send, recv, src, land, after, "rs_wait_" + name)
        return _adam_pieces(w, m[0], v[0], own, landed, idx, "adam_" + name)

    dy, dout, dgt_m, dg3, loss_vec = _final(y, x2, target, gt_m, g_post_mlp)
    du = _matmul(dy, w_down_full, name="d_act", tb=True, tile_extras=(u,),
                 epilogue=lambda acc, j, uu: (acc * (2.0 * jnp.maximum(uu.astype(F32), 0.0)),))
    dw_down = _matmul(act, dy, name="dw_down", ta=True)
    sb_down = _sib_start(dw_down.reshape(N_DEV, shard_ff, d), "sib_start_down")
    dh2 = _matmul(du, w_up_g, name="d_h2", tb=True, b_sharded=True, b_pair=True, out_dtypes=(F32,),
                  after=sb_down[4])
    rs_down = reduce_start(sb_down, dh2, "down")
    per = shard_ff // _blk(shard_ff, 1024)
    dw_up = _matmul(h2, du, name="dw_up", ta=True, tn=_blk(shard_ff, 1024), out_shape=(N_DEV, d, shard_ff),
                    out_map=lambda tm, tn: pl.BlockSpec((None, tm, tn), lambda i, j, kk: (j // per, i, j % per)),
                    after=rs_down[4])
    sb_up = _sib_start(dw_up, "sib_start_up")
    dmix, dx2, dsh_m, dsc_m, dg2, dgt_a, dg1 = _mid_bwd(
        dh2, dout, x2, mix, g_pre_mlp + tok(sb_up), sc_m, gt_a, g_post_mix)
    d_attn = _matmul(dmix, w_out_full, name="d_attn", tb=True)
    rs_up = reduce_start(sb_up, d_attn, "up")
    dw_out = _matmul(attn, dmix, name="dw_out", ta=True, after=rs_up[4])
    sb_out = _sib_start(dw_out.reshape(N_DEV, d // N_DEV, d), "sib_start_out")
    dqf, dkf, dvf, dcs = _fox_bwd(proj_a, d_attn, cum, cumt, fox_lse)
    dsq, dsk, dsv, dsinks = _swa_bwd(proj_b, d_attn, swa_lse, sinks128 + tok(sb_out), cos, sin)
    rs_out = reduce_start(sb_out, dsq, "out")
    dfg, db_forget = _fg_bwd(dcs, fg, b128 + tok(rs_out))
    dproj = jnp.concatenate([dqf, dkf, dvf, dsq, dsk, dsv, _pad_lanes(dfg, FG_PAD)], axis=1)
    dw_in_r = _matmul(dproj, h1, name="dw_in", ta=True)

    def shard_rows(j):
        lo, hi = j * in_w, (j + 1) * in_w
        parts = []
        for seg_lo, seg_hi, shift in ((0, o_fg, 0), (o_fg, o_sq, n_swa3), (o_sq, in_total, -N_FOX)):
            a, b = max(lo, seg_lo), min(hi, seg_hi)
            if a < b:
                parts.append(dw_in_r[a + shift:b + shift, :])
        return parts[0] if len(parts) == 1 else jnp.concatenate(parts, axis=0)

    sb_in = _sib_start(jnp.stack([shard_rows(j) for j in range(N_DEV)]), "sib_start_in")
    dh1 = _matmul(dproj, w_in_r, name="d_h1", tb=True, tk=2560, out_dtypes=(F32,), after=sb_in[4])
    grad_x, dsh_a, dsc_a, dg0 = _x_bwd(dh1, dx2, x, g_pre_mix, sc_a)

    small = jnp.concatenate([dsh_a, dsc_a, dgt_a, dsh_m, dsc_m, dgt_m, dg0, dg1, dg2, dg3, db_forget, dsinks,
                             loss_vec[:, 0:LANES]], axis=1)
    small_all = _all_gather([small], "gather_small")[0].reshape(N_DEV, small.shape[1])
    rs_in = reduce_start(sb_in, small_all, "in")

    pack = lambda bm, g0_, g1_, g2_, g3_, bf_, sk_: jnp.concatenate(
        [bm, g0_, g1_, g2_, g3_, _pad_lanes(bf_), _pad_lanes(sk_), jnp.zeros((1, LANES), F32)], axis=1)
    p_small = pack(b_mod, g_pre_mix, g_post_mix, g_pre_mlp, g_post_mlp, b_forget, swa_sinks)
    m_small = pack(m_b_mod, m_g_pre_mix, m_g_post_mix, m_g_pre_mlp, m_g_post_mlp, m_b_forget, m_swa_sinks)
    v_small = pack(v_b_mod, v_g_pre_mix, v_g_post_mix, v_g_pre_mlp, v_g_post_mlp, v_b_forget, v_swa_sinks)
    small_out = _adam_small(small_all, p_small + tok(rs_in), m_small, v_small)

    n_mod = 6 * d

    def unpack(vec):
        o = n_mod
        return (vec[:, 0:n_mod], vec[:, o:o + d], vec[:, o + d:o + 2 * d], vec[:, o + 2 * d:o + 3 * d],
                vec[:, o + 3 * d:o + 4 * d], vec[:, o + 4 * d:o + 4 * d + N_FOX],
                vec[:, o + 4 * d + LANES:o + 4 * d + LANES + N_SWA])

    loss = small_out[0][0, n_mod + 4 * d + 2 * LANES]
    g_small, d_small, nm_small, nv_small = [unpack(vec) for vec in small_out]

    dmod_cols = lax.dynamic_slice(small_all, (0, me * mod_w), (N_DEV, mod_w))
    g_w_mod, d_w_mod, nm_w_mod, nv_w_mod = _adam_mod(c_all + tok(rs_in), dmod_cols, w_mod, m_w_mod[0], v_w_mod[0])

    g_w_down, d_w_down, nm_w_down, nv_w_down = reduce_finish(rs_down, d_w_mod, w_down, m_w_down, v_w_down, "w_down")
    g_w_up, d_w_up, nm_w_up, nv_w_up = reduce_finish(rs_up, d_w_down, w_up, m_w_up, v_w_up, "w_up")
    g_w_out, d_w_out, nm_w_out, nv_w_out = reduce_finish(rs_out, d_w_up, w_out, m_w_out, v_w_out, "w_out")

    def rows128(a):
        return jnp.transpose(a).reshape(in_w * d // LANES, LANES)

    def cols_in_w(a):
        return jnp.transpose(a.reshape(in_w, d))[None]

    send, recv, src, land, _ = rs_in
    own_in, landed_in = _rs_wait(send, recv, src, land, d_w_out, "rs_wait_w_in")
    g_in = _sum_pieces(own_in, landed_in, idx, "sum_w_in").reshape(in_w * d // LANES, LANES)
    g_w_in, d_w_in, nm_w_in, nv_w_in = [cols_in_w(a) for a in (g_in,) + tuple(_adam_rows(
        rows128(w_in), rows128(m_w_in[0]), rows128(v_w_in[0]), g_in, "adam_w_in"))]

    def assemble(w_mod_, small_, w_in_, w_out_, w_up_, w_down_):
        b_mod_, g0_, g1_, g2_, g3_, bf_, sk_ = small_
        return [w_mod_[None], b_mod_, g0_, g1_, w_in_, bf_, sk_, w_out_[None], g2_, g3_, w_up_[None], w_down_[None]]

    outs = [loss, grad_x[None]]
    outs += assemble(g_w_mod, g_small, g_w_in, g_w_out, g_w_up, g_w_down)
    outs += assemble(d_w_mod, d_small, d_w_in, d_w_out, d_w_up, d_w_down)
    outs += assemble(nm_w_mod, nm_small, nm_w_in, nm_w_out, nm_w_up, nm_w_down)
    outs += assemble(nv_w_mod, nv_small, nv_w_in, nv_w_out, nv_w_up, nv_w_down)
    return tuple(outs)
```

```python
import functools

import jax
import jax.numpy as jnp
from jax import lax
from jax.experimental import pallas as pl
from jax.experimental.pallas import tpu as pltpu

F32 = jnp.float32
BF16 = jnp.bfloat16
MESH = pl.DeviceIdType.MESH

N_DEV = 8
N_CHIP = 4
LANES = 128
HEAD_DIM = 128
N_FOX = 8
N_SWA = 8
N_KV = 2
GQA = N_SWA // N_KV
WINDOW = 128
FOX_W = N_FOX * HEAD_DIM
SWA_W = N_SWA * HEAD_DIM
KV_W = N_KV * HEAD_DIM
ROPE_THETA = 10000.0
NORM_EPS = 1e-6
ATT_SCALE = HEAD_DIM ** -0.5
FG_PAD = 512

ADAM_LR = 0.001
ADAM_B1 = 0.9
ADAM_B2 = 0.999
ADAM_EPS = 1e-08
ADAM_WD = 0.01
ADAM_STEP = 10

VMEM_LIMIT = 56 * 1024 * 1024

NT_DIMS = (((1,), (1,)), ((), ()))
TN_DIMS = (((0,), (0,)), ((), ()))
NN_DIMS = (((1,), (0,)), ((), ()))


def _pcall(body, *, name, out_shape, grid=(), in_specs=None, out_specs=None, scratch_shapes=(), grid_spec=None):
    params = pltpu.CompilerParams(vmem_limit_bytes=VMEM_LIMIT)
    if grid_spec is not None:
        return pl.pallas_call(body, name=name, out_shape=out_shape, grid_spec=grid_spec, compiler_params=params)
    return pl.pallas_call(body, name=name, out_shape=out_shape, grid=grid, in_specs=in_specs, out_specs=out_specs,
                          scratch_shapes=scratch_shapes, compiler_params=params)


def _blk(n, pref):
    if n <= pref:
        return n
    b = (pref // LANES) * LANES
    while n % b:
        b -= LANES
    return b


def _position():
    return lax.axis_index("x"), lax.axis_index("y"), lax.axis_index("c")


ANY = pl.BlockSpec(memory_space=pl.ANY)


def _all_gather(arrs, name):
    n = len(arrs)

    def body(*refs):
        ins, outs = refs[:n], refs[n:2 * n]
        send_sems, recv_sems, local_sems = refs[2 * n:]
        x, y, c = _position()
        me, sibling = (x, y, c), (x, y, 1 - c)
        chips = [(1 - x, y), (x, 1 - y), (1 - x, 1 - y)]

        def slot(p):
            return 4 * p[0] + 2 * p[1] + p[2]

        def copy(a, k, block, to, src=None):
            dst = outs[a].at[slot(block)]
            return pltpu.make_async_remote_copy(
                src_ref=dst if src is None else src, dst_ref=dst,
                send_sem=send_sems.at[7 * a + k], recv_sem=recv_sems.at[7 * a + k],
                device_id=to, device_id_type=MESH)

        mine = [pltpu.make_async_copy(ins[a], outs[a].at[slot(me)], local_sems.at[a]) for a in range(n)]
        for cp in mine:
            cp.start()
        first = []
        for a in range(n):
            first.append(copy(a, 0, me, sibling, src=ins[a]))
            first += [copy(a, 1 + j, me, (*chip, c), src=ins[a]) for j, chip in enumerate(chips)]
        for cp in first:
            cp.start()
        passed = []
        for a in range(n):
            for j, chip in enumerate(chips):
                copy(a, 1 + j, (*chip, c), me).wait_recv()
                cp = copy(a, 4 + j, (*chip, c), sibling)
                cp.start()
                passed.append(cp)
        for a in range(n):
            copy(a, 0, sibling, me).wait_recv()
            for j, chip in enumerate(chips):
                copy(a, 4 + j, (*chip, 1 - c), me).wait_recv()
        for cp in first + passed:
            cp.wait_send()
        for cp in mine:
            cp.wait()

    return _pcall(
        body, name=name,
        out_shape=[jax.ShapeDtypeStruct((N_DEV,) + a.shape, a.dtype) for a in arrs],
        in_specs=[ANY] * n, out_specs=[ANY] * n,
        scratch_shapes=[pltpu.SemaphoreType.DMA((7 * n,)), pltpu.SemaphoreType.DMA((7 * n,)),
                        pltpu.SemaphoreType.DMA((n,))],
    )(*arrs)


HBM = pl.BlockSpec(memory_space=pltpu.HBM)
SEM = pl.BlockSpec(memory_space=pltpu.SEMAPHORE)
EFFECT = pltpu.SideEffectType.DATAFLOW_SIDE_EFFECTING


def _hbm(a):
    return pltpu.with_memory_space_constraint(a, pltpu.HBM)


def _gather_peers():
    x, y, c = _position()
    return [(x, y, 1 - c), (1 - x, y, c), (x, 1 - y, c), (1 - x, 1 - y, c)]


def _ag_start(shards, name):
    n = len(shards)
    lands = [_hbm(lax.empty((N_DEV,) + s.shape, s.dtype)) for s in shards]

    def body(*refs):
        srcs, land, send, recv = refs[:n], refs[n:2 * n], refs[2 * n:3 * n], refs[3 * n:4 * n]
        token = refs[6 * n]
        x, y, c = _position()
        for a in range(n):
            for k, to in enumerate(_gather_peers()):
                pltpu.make_async_remote_copy(
                    src_ref=srcs[a], dst_ref=land[a].at[4 * x + 2 * y + c], send_sem=send[a].at[k],
                    recv_sem=recv[a].at[k], device_id=to, device_id_type=MESH).start()
        token[...] = jnp.zeros_like(token)

    sems = [pltpu.SemaphoreType.DMA((4,))] * (2 * n)
    out = pl.pallas_call(
        body, name=name,
        out_shape=sems + [pltpu.HBM(s.shape, s.dtype) for s in shards] + [pltpu.HBM(l.shape, l.dtype) for l in lands]
        + [jax.ShapeDtypeStruct((8, LANES), F32)],
        in_specs=[HBM] * (2 * n), out_specs=[SEM] * (2 * n) + [HBM] * (2 * n) + [pl.BlockSpec(memory_space=pltpu.VMEM)],
        input_output_aliases={**{a: 2 * n + a for a in range(n)}, **{n + a: 3 * n + a for a in range(n)}},
        compiler_params=pltpu.CompilerParams(has_side_effects=EFFECT),
    )(*[_hbm(s) for s in shards], *lands)
    return out[:n], out[n:2 * n], out[2 * n:3 * n], out[3 * n:4 * n], out[4 * n]


def _ag_wait(send, recv, shard_thru, land_thru, after, name):
    def body(v_ref, land_ref, send_sem, recv_sem, after_ref, v_dead, got_ref):
        for k, to in enumerate(_gather_peers()):
            cp = pltpu.make_async_remote_copy(
                src_ref=v_ref, dst_ref=land_ref.at[0], send_sem=send_sem.at[k], recv_sem=recv_sem.at[k],
                device_id=to, device_id_type=MESH)
            cp.wait_send()
            cp.wait_recv()

    return pl.pallas_call(
        body, name=name,
        out_shape=(pltpu.HBM(shard_thru.shape, shard_thru.dtype), pltpu.HBM(land_thru.shape, land_thru.dtype)),
        in_specs=(HBM, HBM, SEM, SEM, ANY), out_specs=(HBM, HBM), input_output_aliases={0: 0, 1: 1},
        compiler_params=pltpu.CompilerParams(has_side_effects=EFFECT),
    )(shard_thru, land_thru, send, recv, after)


def _ag_forward(land, name):
    def body(land_in, land_ref, send_sems, recv_sems):
        x, y, c = _position()
        copies = []
        for j, (px, py) in enumerate([(1 - x, y), (x, 1 - y), (1 - x, 1 - y)]):
            block = land_ref.at[4 * px + 2 * py + c]
            cp = pltpu.make_async_remote_copy(src_ref=block, dst_ref=block, send_sem=send_sems.at[j],
                                              recv_sem=recv_sems.at[j], device_id=(x, y, 1 - c), device_id_type=MESH)
            cp.start()
            copies.append(cp)
        for cp in copies:
            cp.wait()

    return pl.pallas_call(
        body, name=name, out_shape=jax.ShapeDtypeStruct(land.shape, land.dtype),
        in_specs=[ANY], out_specs=ANY, input_output_aliases={0: 0},
        scratch_shapes=[pltpu.SemaphoreType.DMA((3,)), pltpu.SemaphoreType.DMA((3,))],
    )(land)


def _fwd_start(land, name):
    def body(land_ref, send, recv, land_thru, token):
        x, y, c = _position()
        for j, (px, py) in enumerate([(1 - x, y), (x, 1 - y), (1 - x, 1 - y)]):
            block = land_ref.at[4 * px + 2 * py + c]
            pltpu.make_async_remote_copy(src_ref=block, dst_ref=block, send_sem=send.at[j], recv_sem=recv.at[j],
                                         device_id=(x, y, 1 - c), device_id_type=MESH).start()
        token[...] = jnp.zeros_like(token)

    return pl.pallas_call(
        body, name=name,
        out_shape=[pltpu.SemaphoreType.DMA((3,)), pltpu.SemaphoreType.DMA((3,)), pltpu.HBM(land.shape, land.dtype),
                   jax.ShapeDtypeStruct((8, LANES), F32)],
        in_specs=[HBM], out_specs=[SEM, SEM, HBM, pl.BlockSpec(memory_space=pltpu.VMEM)],
        input_output_aliases={0: 2},
        compiler_params=pltpu.CompilerParams(has_side_effects=EFFECT),
    )(land)


def _fwd_wait(send, recv, land_thru, after, name):
    def body(land_ref, send_sem, recv_sem, after_ref, land_out):
        x, y, c = _position()
        for j in range(3):
            cp = pltpu.make_async_remote_copy(
                src_ref=land_ref.at[0], dst_ref=land_ref.at[0], send_sem=send_sem.at[j], recv_sem=recv_sem.at[j],
                device_id=(x, y, 1 - c), device_id_type=MESH)
            cp.wait_send()
            cp.wait_recv()

    return pl.pallas_call(
        body, name=name, out_shape=pltpu.HBM(land_thru.shape, land_thru.dtype),
        in_specs=(HBM, SEM, SEM, ANY), out_specs=HBM, input_output_aliases={0: 0},
        compiler_params=pltpu.CompilerParams(has_side_effects=EFFECT),
    )(land_thru, send, recv, after)


def _rs_peers():
    x, y, c = _position()
    return [(1 - x, y, c), (x, 1 - y, c), (1 - x, 1 - y, c)]


def _rs_start(chip_sums, name):
    land = _hbm(lax.empty(chip_sums.shape, chip_sums.dtype))

    def body(src, land_ref, send, recv, src_thru, land_thru, token):
        x, y, c = _position()
        for j, (px, py, pc) in enumerate(_rs_peers()):
            pltpu.make_async_remote_copy(
                src_ref=src.at[2 * px + py], dst_ref=land_ref.at[2 * x + y], send_sem=send.at[j], recv_sem=recv.at[j],
                device_id=(px, py, pc), device_id_type=MESH).start()
        token[...] = jnp.zeros_like(token)

    return pl.pallas_call(
        body, name=name,
        out_shape=[pltpu.SemaphoreType.DMA((3,)), pltpu.SemaphoreType.DMA((3,)),
                   pltpu.HBM(chip_sums.shape, chip_sums.dtype), pltpu.HBM(land.shape, land.dtype),
                   jax.ShapeDtypeStruct((8, LANES), F32)],
        in_specs=[HBM, HBM], out_specs=[SEM, SEM, HBM, HBM, pl.BlockSpec(memory_space=pltpu.VMEM)],
        input_output_aliases={0: 2, 1: 3},
        compiler_params=pltpu.CompilerParams(has_side_effects=EFFECT),
    )(_hbm(chip_sums), land)


def _rs_wait(send, recv, src_thru, land_thru, after, name):
    def body(src, land_ref, send_sem, recv_sem, after_ref, src_out, land_out):
        for j, to in enumerate(_rs_peers()):
            cp = pltpu.make_async_remote_copy(
                src_ref=src.at[0], dst_ref=land_ref.at[0], send_sem=send_sem.at[j], recv_sem=recv_sem.at[j],
                device_id=to, device_id_type=MESH)
            cp.wait_send()
            cp.wait_recv()

    return pl.pallas_call(
        body, name=name,
        out_shape=(pltpu.HBM(src_thru.shape, src_thru.dtype), pltpu.HBM(land_thru.shape, land_thru.dtype)),
        in_specs=(HBM, HBM, SEM, SEM, ANY), out_specs=(HBM, HBM), input_output_aliases={0: 0, 1: 1},
        compiler_params=pltpu.CompilerParams(has_side_effects=EFFECT),
    )(src_thru, land_thru, send, recv, after)


def _sib_start(full, name):
    land = _hbm(lax.empty((N_CHIP,) + full.shape[1:], full.dtype))

    def body(src, land_ref, send, recv, src_thru, land_thru, token):
        x, y, c = _position()
        for k in range(N_CHIP):
            pltpu.make_async_remote_copy(
                src_ref=src.at[2 * k + (1 - c)], dst_ref=land_ref.at[k], send_sem=send.at[k], recv_sem=recv.at[k],
                device_id=(x, y, 1 - c), device_id_type=MESH).start()
        token[...] = jnp.zeros_like(token)

    return pl.pallas_call(
        body, name=name,
        out_shape=[pltpu.SemaphoreType.DMA((N_CHIP,)), pltpu.SemaphoreType.DMA((N_CHIP,)),
                   pltpu.HBM(full.shape, full.dtype), pltpu.HBM(land.shape, land.dtype),
                   jax.ShapeDtypeStruct((8, LANES), F32)],
        in_specs=[HBM, HBM], out_specs=[SEM, SEM, HBM, HBM, pl.BlockSpec(memory_space=pltpu.VMEM)],
        input_output_aliases={0: 2, 1: 3},
        compiler_params=pltpu.CompilerParams(has_side_effects=EFFECT),
    )(_hbm(full), land)


def _sib_wait(send, recv, src_thru, land_thru, after, name):
    def body(src, land_ref, send_sem, recv_sem, after_ref, src_out, land_out):
        x, y, c = _position()
        for k in range(N_CHIP):
            cp = pltpu.make_async_remote_copy(
                src_ref=src.at[0], dst_ref=land_ref.at[0], send_sem=send_sem.at[k], recv_sem=recv_sem.at[k],
                device_id=(x, y, 1 - c), device_id_type=MESH)
            cp.wait_send()
            cp.wait_recv()

    return pl.pallas_call(
        body, name=name,
        out_shape=(pltpu.HBM(src_thru.shape, src_thru.dtype), pltpu.HBM(land_thru.shape, land_thru.dtype)),
        in_specs=(HBM, HBM, SEM, SEM, ANY), out_specs=(HBM, HBM), input_output_aliases={0: 0, 1: 1},
        compiler_params=pltpu.CompilerParams(has_side_effects=EFFECT),
    )(src_thru, land_thru, send, recv, after)


def _rs_sibling(arrs, name):
    n = len(arrs)

    def body(*refs):
        ins, outs = refs[:n], refs[n:2 * n]
        send_sems, recv_sems = refs[2 * n:]
        x, y, c = _position()
        copies = []
        for a in range(n):
            for k in range(N_CHIP):
                cp = pltpu.make_async_remote_copy(
                    src_ref=ins[a].at[2 * k + (1 - c)], dst_ref=outs[a].at[k],
                    send_sem=send_sems.at[N_CHIP * a + k], recv_sem=recv_sems.at[N_CHIP * a + k],
                    device_id=(x, y, 1 - c), device_id_type=MESH)
                cp.start()
                copies.append(cp)
        for cp in copies:
            cp.wait()

    return _pcall(
        body, name=name,
        out_shape=[jax.ShapeDtypeStruct((N_CHIP,) + a.shape[1:], a.dtype) for a in arrs],
        in_specs=[ANY] * n, out_specs=[ANY] * n,
        scratch_shapes=[pltpu.SemaphoreType.DMA((N_CHIP * n,)), pltpu.SemaphoreType.DMA((N_CHIP * n,))],
    )(*arrs)


def _chip_sum(full, recv, core, name):
    _, rows, cols = full.shape
    tr = _blk(rows, 1024) if rows % LANES == 0 else rows

    def body(core_ref, a_ref, b_ref, o_ref):
        o_ref[...] = (a_ref[...].astype(F32) + b_ref[...].astype(F32)).astype(o_ref.dtype)

    grid_spec = pltpu.PrefetchScalarGridSpec(
        num_scalar_prefetch=1, grid=(N_CHIP, rows // tr),
        in_specs=[pl.BlockSpec((None, tr, cols), lambda k, i, core_ref: (2 * k + core_ref[0], i, 0)),
                  pl.BlockSpec((None, tr, cols), lambda k, i, core_ref: (k, i, 0))],
        out_specs=pl.BlockSpec((None, tr, cols), lambda k, i, core_ref: (k, i, 0)))
    return _pcall(body, name=name, out_shape=jax.ShapeDtypeStruct((N_CHIP, rows, cols), full.dtype),
                  grid_spec=grid_spec)(core, full, recv)


def _matmul(a, b, *, name, ta=False, tb=False, tm=1024, tn=1024, tk=2048, out_dtypes=(BF16,), epilogue=None,
            row_extras=(), tile_extras=(), out_shape=None, out_map=None, b_sharded=False, n_cols=None, n_off=0,
            after=None, b_pair=False):
    m, k = (a.shape[1], a.shape[0]) if ta else a.shape
    if b_sharded:
        shard_c = b.shape[2]
        n, kb = (b.shape[1], N_DEV * shard_c) if tb else (N_DEV * shard_c, b.shape[1])
        tn, tk = (tn, min(tk, shard_c)) if tb else (min(tn, shard_c), tk)
        if b_pair:
            assert tb and tk == shard_c
            tk = 2 * shard_c
    else:
        n, kb = b.shape if tb else (b.shape[1], b.shape[0])
    assert kb == k, (name, kb, k)
    if n_cols is not None:
        n = n_cols
    tm, tn, tk = _blk(m, tm), _blk(n, tn), _blk(k, tk)
    assert n_off % tn == 0
    nk = k // tk
    dims = (((0 if ta else 1,), (1 if tb else 0,)), ((), ()))
    behind = () if after is None else (after,)
    n_row, n_tile, n_out = len(row_extras), len(tile_extras), len(out_dtypes)
    n_b = 2 if b_pair else 1
    first_out = 1 + n_b + n_row + n_tile + len(behind)

    def body(*refs):
        a_ref, b_ref = refs[:2]
        extras = refs[1 + n_b:1 + n_b + n_row + n_tile]
        outs = refs[first_out:first_out + n_out]
        acc_ref = refs[-1]
        jj, kk = pl.program_id(1), pl.program_id(2)
        if b_pair:
            half = tk // 2
            part = (lax.dot_general(a_ref[:, 0:half].astype(BF16), b_ref[...].astype(BF16), dims,
                                    preferred_element_type=F32)
                    + lax.dot_general(a_ref[:, half:tk].astype(BF16), refs[2][...].astype(BF16), dims,
                                      preferred_element_type=F32))
        else:
            part = lax.dot_general(a_ref[...].astype(BF16), b_ref[...].astype(BF16), dims,
                                   preferred_element_type=F32)

        def finish(acc):
            res = (acc,) if epilogue is None else epilogue(acc, jj, *[e[...] for e in extras])
            for o_ref, r in zip(outs, res):
                o_ref[...] = r.astype(o_ref.dtype)

        if nk == 1:
            finish(part)
        else:
            @pl.when(kk == 0)
            def _():
                acc_ref[...] = part

            @pl.when(kk > 0)
            def _():
                acc_ref[...] += part

            @pl.when(kk == nk - 1)
            def _():
                finish(acc_ref[...])

    a_spec = pl.BlockSpec((tk, tm), lambda i, j, kk: (kk, i)) if ta else pl.BlockSpec((tm, tk), lambda i, j, kk: (i, kk))
    if b_pair:
        b_spec = pl.BlockSpec((None, tn, shard_c), lambda i, j, kk: (2 * kk, j, 0))
    elif b_sharded and tb:
        per = shard_c // tk
        b_spec = pl.BlockSpec((None, tn, tk), lambda i, j, kk: (kk // per, j, kk % per))
    elif b_sharded:
        per = shard_c // tn
        b_spec = pl.BlockSpec((None, tk, tn), lambda i, j, kk: (j // per, kk, j % per))
    elif tb:
        b_spec = pl.BlockSpec((tn, tk), lambda i, j, kk: (j + n_off // tn, kk))
    else:
        b_spec = pl.BlockSpec((tk, tn), lambda i, j, kk: (kk, j + n_off // tn))
    in_specs = [a_spec, b_spec]
    if b_pair:
        in_specs.append(pl.BlockSpec((None, tn, shard_c), lambda i, j, kk: (2 * kk + 1, j, 0)))
    in_specs += [pl.BlockSpec((tm, LANES), lambda i, j, kk: (i, 0)) for _ in row_extras]
    in_specs += [pl.BlockSpec((tm, tn), lambda i, j, kk: (i, j)) for _ in tile_extras]
    in_specs += [ANY for _ in behind]
    if out_map is None:
        out_specs = [pl.BlockSpec((tm, tn), lambda i, j, kk: (i, j)) for _ in out_dtypes]
        shapes = [jax.ShapeDtypeStruct((m, n), dt) for dt in out_dtypes]
    else:
        out_specs = [out_map(tm, tn)]
        shapes = [jax.ShapeDtypeStruct(out_shape, out_dtypes[0])]
    acc_shape = (tm, tn) if nk > 1 else (8, LANES)
    res = _pcall(body, name=name, out_shape=shapes, grid=(m // tm, n // tn, nk), in_specs=in_specs,
                 out_specs=out_specs, scratch_shapes=[pltpu.VMEM(acc_shape, F32)])(
                     a, *([b, b] if b_pair else [b]), *row_extras, *tile_extras, *behind)
    return res[0] if n_out == 1 else res


def _rope_cols(acc, j, cos, sin, n_rope):
    width = acc.shape[1]
    parts = []
    for g in range(width // HEAD_DIM):
        xg = acc[:, g * HEAD_DIM:(g + 1) * HEAD_DIM]
        roped = xg * cos + pltpu.roll(xg, HEAD_DIM // 2, 1) * sin
        parts.append(jnp.where(j * width + g * HEAD_DIM < n_rope, roped, xg))
    return jnp.concatenate(parts, axis=1) if len(parts) > 1 else parts[0]


def _silu(v):
    return v / (1.0 + jnp.exp(-v))


def _mod_part(c_all, w_mod, b_part):
    d, w = w_mod.shape
    tk = _blk(d, 512)

    def body(c_ref, w_ref, b_ref, o_ref):
        kk = pl.program_id(0)
        cond = _silu(c_ref[...]).astype(BF16)
        part = jnp.dot(cond, w_ref[...].astype(BF16), preferred_element_type=F32)

        @pl.when(kk == 0)
        def _():
            o_ref[...] = part + b_ref[...]

        @pl.when(kk > 0)
        def _():
            o_ref[...] += part

    return _pcall(body, name="mod_part", out_shape=jax.ShapeDtypeStruct((N_DEV, w), F32), grid=(d // tk,),
                  in_specs=[pl.BlockSpec((N_DEV, tk), lambda kk: (0, kk)), pl.BlockSpec((tk, w), lambda kk: (kk, 0)),
                            pl.BlockSpec((1, w), lambda kk: (0, 0))],
                  out_specs=pl.BlockSpec((N_DEV, w), lambda kk: (0, 0)))(c_all, w_mod, b_part)


def _row_call(body, name, t, d, tiled_in, vec_in, tiled_out_dtypes, n_vec_out, tr=256):
    tr = _blk(t, tr)
    tile = pl.BlockSpec((tr, d), lambda i: (i, 0))
    vec = pl.BlockSpec((1, d), lambda i: (0, 0))
    out_shape = [jax.ShapeDtypeStruct((t, d), dt) for dt in tiled_out_dtypes]
    out_shape += [jax.ShapeDtypeStruct((1, d), F32)] * n_vec_out
    return _pcall(body, name=name, out_shape=out_shape, grid=(t // tr,),
                  in_specs=[tile] * len(tiled_in) + [vec] * len(vec_in),
                  out_specs=[tile] * len(tiled_out_dtypes) + [vec] * n_vec_out)(*tiled_in, *vec_in)


def _accumulate(ref, val):
    @pl.when(pl.program_id(0) == 0)
    def _():
        ref[...] = val

    @pl.when(pl.program_id(0) > 0)
    def _():
        ref[...] += val


def _rsum(v):
    return jnp.sum(v, axis=0, keepdims=True)


def _rms(v):
    return lax.rsqrt(jnp.mean(v * v, axis=-1, keepdims=True) + NORM_EPS)


def _rms_bwd(vhat, r, dvhat):
    return r * (dvhat - vhat * jnp.mean(dvhat * vhat, axis=-1, keepdims=True))


def _pre_attn(x, g0, sc_a, sh_a):
    def body(x_ref, g_ref, sc_ref, sh_ref, h_ref):
        xv = x_ref[...]
        h_ref[...] = (xv * _rms(xv) * g_ref[...] * (1.0 + sc_ref[...]) + sh_ref[...]).astype(BF16)

    t, d = x.shape
    return _row_call(body, "pre_attn", t, d, [x], [g0, sc_a, sh_a], [BF16], 0)[0]


def _post_mix(x, mix, gt_a, g1, g2, sc_m, sh_m):
    def body(x_ref, mix_ref, gt_ref, g1_ref, g2_ref, sc_ref, sh_ref, x2_ref, h2_ref):
        mv = mix_ref[...]
        x2 = x_ref[...] + gt_ref[...] * (mv * _rms(mv) * g1_ref[...])
        x2_ref[...] = x2
        h2_ref[...] = (x2 * _rms(x2) * g2_ref[...] * (1.0 + sc_ref[...]) + sh_ref[...]).astype(BF16)

    t, d = x.shape
    return _row_call(body, "post_mix", t, d, [x, mix], [gt_a, g1, g2, sc_m, sh_m], [F32, BF16], 0)


def _final(y, x2, target, gt_m, g3):
    t, d = y.shape

    def body(y_ref, x2_ref, tg_ref, gt_ref, g3_ref, dy_ref, dout_ref, dgt_ref, dg3_ref, loss_ref):
        yv = y_ref[...]
        r = _rms(yv)
        yhat = yv * r
        n3 = yhat * g3_ref[...]
        err = x2_ref[...] + gt_ref[...] * n3 - tg_ref[...]
        _accumulate(loss_ref, jnp.zeros((1, d), F32) + 0.5 * jnp.sum(err * err) / d)
        dout = err * (1.0 / d)
        dout_ref[...] = dout
        _accumulate(dgt_ref, _rsum(dout * n3))
        dn3 = dout * gt_ref[...]
        _accumulate(dg3_ref, _rsum(dn3 * yhat))
        dy_ref[...] = _rms_bwd(yhat, r, dn3 * g3_ref[...]).astype(BF16)

    return _row_call(body, "final", t, d, [y, x2, target], [gt_m, g3], [BF16, F32], 3)


def _mid_bwd(dh2, dout, x2, mix, g2, sc_m, gt_a, g1):
    t, d = x2.shape

    def body(dh2_ref, dout_ref, x2_ref, mix_ref, g2_ref, sc_ref, gt_ref, g1_ref,
             dmix_ref, dx2_ref, dsh_ref, dsc_ref, dg2_ref, dgt_ref, dg1_ref):
        dh2v = dh2_ref[...]
        x2v = x2_ref[...]
        r2 = _rms(x2v)
        x2hat = x2v * r2
        _accumulate(dsh_ref, _rsum(dh2v))
        _accumulate(dsc_ref, _rsum(dh2v * (x2hat * g2_ref[...])))
        dn2 = dh2v * (1.0 + sc_ref[...])
        _accumulate(dg2_ref, _rsum(dn2 * x2hat))
        dx2 = dout_ref[...] + _rms_bwd(x2hat, r2, dn2 * g2_ref[...])
        dx2_ref[...] = dx2
        mv = mix_ref[...]
        r1 = _rms(mv)
        mhat = mv * r1
        _accumulate(dgt_ref, _rsum(dx2 * (mhat * g1_ref[...])))
        dn1 = dx2 * gt_ref[...]
        _accumulate(dg1_ref, _rsum(dn1 * mhat))
        dmix_ref[...] = _rms_bwd(mhat, r1, dn1 * g1_ref[...]).astype(BF16)

    return _row_call(body, "mid_bwd", t, d, [dh2, dout, x2, mix], [g2, sc_m, gt_a, g1], [BF16, F32], 5)


def _x_bwd(dh1, dx2, x, g0, sc_a):
    t, d = x.shape

    def body(dh1_ref, dx2_ref, x_ref, g0_ref, sc_ref, dx_ref, dsh_ref, dsc_ref, dg0_ref):
        dh1v = dh1_ref[...]
        xv = x_ref[...]
        r0 = _rms(xv)
        xhat = xv * r0
        _accumulate(dsh_ref, _rsum(dh1v))
        _accumulate(dsc_ref, _rsum(dh1v * (xhat * g0_ref[...])))
        dn0 = dh1v * (1.0 + sc_ref[...])
        _accumulate(dg0_ref, _rsum(dn0 * xhat))
        dx_ref[...] = dx2_ref[...] + _rms_bwd(xhat, r0, dn0 * g0_ref[...])

    return _row_call(body, "x_bwd", t, d, [dh1, dx2, x], [g0, sc_a], [F32], 3)


def _pick_lane(block, h):
    lane = lax.broadcasted_iota(jnp.int32, block.shape, 1)
    return jnp.sum(jnp.where(lane == h, block, 0.0), axis=1, keepdims=True)


def _put_lane(ref, rows, h, col):
    old = ref[rows, :]
    lane = lax.broadcasted_iota(jnp.int32, old.shape, 1)
    ref[rows, :] = jnp.where(lane == h, col, old)


def _tri(n, lower):
    r = lax.broadcasted_iota(jnp.int32, (n, n), 0)
    c = lax.broadcasted_iota(jnp.int32, (n, n), 1)
    return jnp.where((c <= r) if lower else (c >= r), 1.0, 0.0).astype(F32)


def _cum_fwd(fg, b128):
    t = fg.shape[0]
    nb = t // LANES

    def body(fg_ref, b_ref, cum_ref, cumt_ref):
        tri = _tri(LANES, True)
        carry = jnp.zeros((1, LANES), F32)
        for i in range(nb):
            z = fg_ref[i * LANES:(i + 1) * LANES, :] + b_ref[...]
            lf = jnp.minimum(z, 0.0) - jnp.log(1.0 + jnp.exp(-jnp.abs(z)))
            blk = jnp.dot(tri, lf, precision=lax.Precision.HIGHEST, preferred_element_type=F32) + carry
            cum_ref[i * LANES:(i + 1) * LANES, :] = blk
            carry = blk[LANES - 1:LANES, :]
        cumt_ref[...] = cum_ref[...].T[0:N_FOX, :]

    return _pcall(body, name="cum_fwd",
                  out_shape=[jax.ShapeDtypeStruct((t, LANES), F32), jax.ShapeDtypeStruct((N_FOX, t), F32)],
                  grid=(1,),
                  in_specs=[pl.BlockSpec((t, LANES), lambda i: (0, 0)), pl.BlockSpec((1, LANES), lambda i: (0, 0))],
                  out_specs=[pl.BlockSpec((t, LANES), lambda i: (0, 0)), pl.BlockSpec((N_FOX, t), lambda i: (0, 0))],
                  )(fg, b128)


def _fg_bwd(dcs_rows, fg, b128):
    t = fg.shape[0]
    nb = t // LANES

    def body(dcs_ref, fg_ref, b_ref, dfg_ref, db_ref, dcum_ref):
        dcum_ref[...] = -jnp.concatenate([dcs_ref[...], jnp.zeros((LANES - N_FOX, t), F32)], axis=0).T
        tri = _tri(LANES, False)
        carry = jnp.zeros((1, LANES), F32)
        db = jnp.zeros((1, LANES), F32)
        for i in reversed(range(nb)):
            rows = slice(i * LANES, (i + 1) * LANES)
            dlf = jnp.dot(tri, dcum_ref[rows, :], precision=lax.Precision.HIGHEST, preferred_element_type=F32) + carry
            carry = dlf[0:1, :]
            z = fg_ref[rows, :] + b_ref[...]
            dfg = dlf / (1.0 + jnp.exp(z))
            dfg_ref[rows, :] = dfg.astype(BF16)
            db = db + _rsum(dfg)
        db_ref[...] = db

    full = pl.BlockSpec((t, LANES), lambda i: (0, 0))
    vec = pl.BlockSpec((1, LANES), lambda i: (0, 0))
    return _pcall(body, name="fg_bwd",
                  out_shape=[jax.ShapeDtypeStruct((t, LANES), BF16), jax.ShapeDtypeStruct((1, LANES), F32)],
                  grid=(1,), in_specs=[pl.BlockSpec((N_FOX, t), lambda i: (0, 0)), full, vec], out_specs=[full, vec],
                  scratch_shapes=[pltpu.VMEM((t, LANES), F32)])(dcs_rows, fg, b128)


def _head_spec(t, col0, div=1):
    return pl.BlockSpec((t, HEAD_DIM), lambda h: (0, col0 + h // div))


def _fox_scores(q, k, cq, ck, i, tq, end):
    s = lax.dot_general(q, k, NT_DIMS, preferred_element_type=F32) * ATT_SCALE + cq - ck
    row = lax.broadcasted_iota(jnp.int32, (tq, end), 0) + i * tq
    col = lax.broadcasted_iota(jnp.int32, (tq, end), 1)
    return jnp.where(row >= col, s, -jnp.inf)


def _fox_fwd(proj_a, cum, cumt):
    t = proj_a.shape[0]
    tq = _blk(t, 512)
    nq = t // tq

    def body(q_ref, k_ref, v_ref, cum_ref, cumt_ref, o_ref, lse_ref):
        h = pl.program_id(0)
        cq_all = _pick_lane(cum_ref[...], h)
        ck_all = cumt_ref[pl.ds(h, 1), :]

        @pl.when(h == 0)
        def _():
            lse_ref[...] = jnp.zeros_like(lse_ref)

        for i in range(nq):
            rows, end = slice(i * tq, (i + 1) * tq), (i + 1) * tq
            s = _fox_scores(q_ref[rows, :], k_ref[0:end, :], cq_all[rows, :], ck_all[:, 0:end], i, tq, end)
            m = jnp.max(s, axis=1, keepdims=True)
            p = jnp.exp(s - m)
            l = jnp.sum(p, axis=1, keepdims=True)
            o = jnp.dot(p.astype(BF16), v_ref[0:end, :], preferred_element_type=F32) / l
            o_ref[rows, :] = o.astype(BF16)
            _put_lane(lse_ref, rows, h, m + jnp.log(l))

    nh = FOX_W // HEAD_DIM
    stat = pl.BlockSpec((t, LANES), lambda h: (0, 0))
    return _pcall(body, name="fox_fwd",
                  out_shape=[jax.ShapeDtypeStruct((t, FOX_W), BF16), jax.ShapeDtypeStruct((t, LANES), F32)],
                  grid=(N_FOX,),
                  in_specs=[_head_spec(t, 0), _head_spec(t, nh), _head_spec(t, 2 * nh), stat,
                            pl.BlockSpec((N_FOX, t), lambda h: (0, 0))],
                  out_specs=[_head_spec(t, 0), stat])(proj_a, proj_a, proj_a, cum, cumt)


def _fox_bwd(proj_a, d_attn, cum, cumt, lse):
    t = proj_a.shape[0]
    tq = _blk(t, 512)
    nq = t // tq

    def body(q_ref, k_ref, v_ref, do_ref, cum_ref, cumt_ref, lse_ref,
             dq_ref, dk_ref, dv_ref, dcs_ref, dk_acc, dv_acc, dcs_acc):
        h = pl.program_id(0)
        cq_all = _pick_lane(cum_ref[...], h)
        ck_all = cumt_ref[pl.ds(h, 1), :]
        lse_all = _pick_lane(lse_ref[...], h)
        dk_acc[...] = jnp.zeros_like(dk_acc)
        dv_acc[...] = jnp.zeros_like(dv_acc)
        dcs_acc[...] = jnp.zeros_like(dcs_acc)
        for i in range(nq):
            rows, end = slice(i * tq, (i + 1) * tq), (i + 1) * tq
            q, k, v, do = q_ref[rows, :], k_ref[0:end, :], v_ref[0:end, :], do_ref[rows, :]
            s = _fox_scores(q, k, cq_all[rows, :], ck_all[:, 0:end], i, tq, end)
            p = jnp.exp(s - lse_all[rows, :])
            dp = lax.dot_general(do, v, NT_DIMS, preferred_element_type=F32)
            ds = p * (dp - jnp.sum(p * dp, axis=1, keepdims=True))
            dcs_acc[:, 0:end] += jnp.sum(ds, axis=0, keepdims=True)
            ds = ds.astype(BF16)
            dq_ref[rows, :] = (jnp.dot(ds, k, preferred_element_type=F32) * ATT_SCALE).astype(BF16)
            dk_acc[0:end, :] += lax.dot_general(ds, q, TN_DIMS, preferred_element_type=F32)
            dv_acc[0:end, :] += lax.dot_general(p.astype(BF16), do, TN_DIMS, preferred_element_type=F32)
        dk_ref[...] = (dk_acc[...] * ATT_SCALE).astype(BF16)
        dv_ref[...] = dv_acc[...].astype(BF16)
        dcs_ref[pl.ds(h, 1), :] = dcs_acc[...]

    nh = FOX_W // HEAD_DIM
    stat = pl.BlockSpec((t, LANES), lambda h: (0, 0))
    rows8 = pl.BlockSpec((N_FOX, t), lambda h: (0, 0))
    head = _head_spec(t, 0)
    wide = jax.ShapeDtypeStruct((t, FOX_W), BF16)
    return _pcall(body, name="fox_bwd",
                  out_shape=[wide, wide, wide, jax.ShapeDtypeStruct((N_FOX, t), F32)],
                  grid=(N_FOX,),
                  in_specs=[_head_spec(t, 0), _head_spec(t, nh), _head_spec(t, 2 * nh), head, stat, rows8, stat],
                  out_specs=[head, head, head, rows8],
                  scratch_shapes=[pltpu.VMEM((t, HEAD_DIM), F32), pltpu.VMEM((t, HEAD_DIM), F32),
                                  pltpu.VMEM((1, t), F32)],
                  )(proj_a, proj_a, proj_a, d_attn, cum, cumt, lse)


def _swa_scores(q, k, i, tq, start, end):
    s = lax.dot_general(q, k, NT_DIMS, preferred_element_type=F32) * ATT_SCALE
    row = lax.broadcasted_iota(jnp.int32, (tq, end - start), 0) + i * tq
    col = lax.broadcasted_iota(jnp.int32, (tq, end - start), 1) + start
    diff = row - col
    return jnp.where((diff >= 0) & (diff < WINDOW), s, -jnp.inf)


def _swa_blocks(t):
    tq = _blk(t, 256)
    return tq, [(i, max(0, i * tq - WINDOW), (i + 1) * tq) for i in range(t // tq)]


def _swa_fwd(proj_b, sinks128):
    t = proj_b.shape[0]
    tq, blocks = _swa_blocks(t)

    def body(q_ref, k_ref, v_ref, sink_ref, o_ref, lse_ref):
        h = pl.program_id(0)
        sink = _pick_lane(sink_ref[...], h)

        @pl.when(h == 0)
        def _():
            lse_ref[...] = jnp.zeros_like(lse_ref)

        for i, start, end in blocks:
            rows = slice(i * tq, end)
            s = _swa_scores(q_ref[rows, :], k_ref[start:end, :], i, tq, start, end)
            m = jnp.maximum(jnp.max(s, axis=1, keepdims=True), sink)
            p = jnp.exp(s - m)
            l = jnp.sum(p, axis=1, keepdims=True) + jnp.exp(sink - m)
            o = jnp.dot(p.astype(BF16), v_ref[start:end, :], preferred_element_type=F32) / l
            o_ref[rows, :] = o.astype(BF16)
            _put_lane(lse_ref, rows, h, m + jnp.log(l))

    stat = pl.BlockSpec((t, LANES), lambda h: (0, 0))
    return _pcall(body, name="swa_fwd",
                  out_shape=[jax.ShapeDtypeStruct((t, SWA_W), BF16), jax.ShapeDtypeStruct((t, LANES), F32)],
                  grid=(N_SWA,),
                  in_specs=[_head_spec(t, 0), _head_spec(t, N_SWA, GQA), _head_spec(t, N_SWA + N_KV, GQA),
                            pl.BlockSpec((1, LANES), lambda h: (0, 0))],
                  out_specs=[_head_spec(t, 0), stat])(proj_b, proj_b, proj_b, sinks128)


def _rope_bwd(d, cos, sin):
    return d * cos + pltpu.roll(d * sin, HEAD_DIM // 2, 1)


def _swa_bwd(proj_b, d_attn, lse, sinks128, cos, sin):
    t = proj_b.shape[0]
    tq, blocks = _swa_blocks(t)

    def body(q_ref, k_ref, v_ref, do_ref, lse_ref, sink_ref, cos_ref, sin_ref,
             dq_ref, dk_ref, dv_ref, dsink_ref, dk_acc, dv_acc):
        h = pl.program_id(0)
        sink = _pick_lane(sink_ref[...], h)
        lse_all = _pick_lane(lse_ref[...], h)

        @pl.when(h == 0)
        def _():
            dsink_ref[...] = jnp.zeros_like(dsink_ref)

        @pl.when(h % GQA == 0)
        def _():
            dk_acc[...] = jnp.zeros_like(dk_acc)
            dv_acc[...] = jnp.zeros_like(dv_acc)

        dsink = jnp.zeros((1, 1), F32)
        for i, start, end in blocks:
            rows = slice(i * tq, end)
            q, k, v, do = q_ref[rows, :], k_ref[start:end, :], v_ref[start:end, :], do_ref[rows, :]
            s = _swa_scores(q, k, i, tq, start, end)
            p = jnp.exp(s - lse_all[rows, :])
            dp = lax.dot_general(do, v, NT_DIMS, preferred_element_type=F32)
            delta = jnp.sum(p * dp, axis=1, keepdims=True)
            ds = (p * (dp - delta)).astype(BF16)
            dq = jnp.dot(ds, k, preferred_element_type=F32) * ATT_SCALE
            dq_ref[rows, :] = _rope_bwd(dq, cos_ref[rows, :], sin_ref[rows, :]).astype(BF16)
            dk_acc[start:end, :] += lax.dot_general(ds, q, TN_DIMS, preferred_element_type=F32)
            dv_acc[start:end, :] += lax.dot_general(p.astype(BF16), do, TN_DIMS, preferred_element_type=F32)
            dsink = dsink - jnp.sum(jnp.exp(sink - lse_all[rows, :]) * delta, axis=0, keepdims=True)
        old = dsink_ref[...]
        lane = lax.broadcasted_iota(jnp.int32, old.shape, 1)
        dsink_ref[...] = jnp.where(lane == h, dsink, old)

        @pl.when(h % GQA == GQA - 1)
        def _():
            dk_ref[...] = _rope_bwd(dk_acc[...] * ATT_SCALE, cos_ref[...], sin_ref[...]).astype(BF16)
            dv_ref[...] = dv_acc[...].astype(BF16)

    stat = pl.BlockSpec((t, LANES), lambda h: (0, 0))
    vec = pl.BlockSpec((1, LANES), lambda h: (0, 0))
    head = _head_spec(t, 0)
    kv_out = _head_spec(t, 0, GQA)
    return _pcall(body, name="swa_bwd",
                  out_shape=[jax.ShapeDtypeStruct((t, SWA_W), BF16), jax.ShapeDtypeStruct((t, KV_W), BF16),
                             jax.ShapeDtypeStruct((t, KV_W), BF16), jax.ShapeDtypeStruct((1, LANES), F32)],
                  grid=(N_SWA,),
                  in_specs=[head, _head_spec(t, N_SWA, GQA), _head_spec(t, N_SWA + N_KV, GQA),
                            _head_spec(t, N_FOX), stat, vec, stat, stat],
                  out_specs=[head, kv_out, kv_out, vec],
                  scratch_shapes=[pltpu.VMEM((t, HEAD_DIM), F32), pltpu.VMEM((t, HEAD_DIM), F32)],
                  )(proj_b, proj_b, proj_b, d_attn, lse, sinks128, cos, sin)


def _adamw(w, g, m, v):
    m = ADAM_B1 * m + (1.0 - ADAM_B1) * g
    v = ADAM_B2 * v + (1.0 - ADAM_B2) * (g * g)
    m_hat = m / (1.0 - ADAM_B1 ** ADAM_STEP)
    v_hat = v / (1.0 - ADAM_B2 ** ADAM_STEP)
    delta = -ADAM_LR * (m_hat / (jnp.sqrt(v_hat) + ADAM_EPS) + ADAM_WD * w)
    return delta, m, v


def _adam_pieces(w, m, v, own, land, idx, name):
    rows, cols = w.shape
    tr, tc = (256, cols) if rows % 256 == 0 else (rows, _blk(cols, 512))

    def body(idx_ref, own_ref, l1_ref, l2_ref, l3_ref, w_ref, m_ref, v_ref, g_ref, d_ref, mo_ref, vo_ref):
        g = own_ref[...].astype(F32) + l1_ref[...].astype(F32) + l2_ref[...].astype(F32) + l3_ref[...].astype(F32)
        g_ref[...] = g
        d_ref[...], mo_ref[...], vo_ref[...] = _adamw(w_ref[...], g, m_ref[...], v_ref[...])

    def piece(p):
        return pl.BlockSpec((None, tr, tc), lambda i, j, idx_ref: (idx_ref[p], i, j))

    tile = pl.BlockSpec((tr, tc), lambda i, j, idx_ref: (i, j))
    out = jax.ShapeDtypeStruct((rows, cols), F32)
    grid_spec = pltpu.PrefetchScalarGridSpec(
        num_scalar_prefetch=1, grid=(rows // tr, cols // tc),
        in_specs=[piece(0), piece(1), piece(2), piece(3), tile, tile, tile], out_specs=[tile] * 4)
    return _pcall(body, name=name, out_shape=[out] * 4, grid_spec=grid_spec)(idx, own, land, land, land, w, m, v)


def _adam_mod(c_all, dmod_cols, w, m, v):
    rows, cols = w.shape
    tr = _blk(rows, 256)

    def body(c_ref, dm_ref, w_ref, m_ref, v_ref, g_ref, d_ref, mo_ref, vo_ref):
        cond = _silu(c_ref[...]).astype(BF16)
        g = lax.dot_general(cond, dm_ref[...].astype(BF16), TN_DIMS, preferred_element_type=F32)
        g_ref[...] = g
        d_ref[...], mo_ref[...], vo_ref[...] = _adamw(w_ref[...], g, m_ref[...], v_ref[...])

    tile = pl.BlockSpec((tr, cols), lambda i: (i, 0))
    out = jax.ShapeDtypeStruct((rows, cols), F32)
    return _pcall(body, name="adam_mod", out_shape=[out] * 4, grid=(rows // tr,),
                  in_specs=[pl.BlockSpec((N_DEV, tr), lambda i: (0, i)), pl.BlockSpec((N_DEV, cols), lambda i: (0, 0)),
                            tile, tile, tile],
                  out_specs=[tile] * 4)(c_all, dmod_cols, w, m, v)


def _adam_small(parts, w, m, v):
    nv = w.shape[1]

    def body(p_ref, w_ref, m_ref, v_ref, g_ref, d_ref, mo_ref, vo_ref):
        g = p_ref[0:1, :]
        for k in range(1, N_DEV):
            g = g + p_ref[k:k + 1, :]
        g_ref[...] = g
        d_ref[...], mo_ref[...], vo_ref[...] = _adamw(w_ref[...], g, m_ref[...], v_ref[...])

    vec = pl.BlockSpec((1, nv), lambda i: (0, 0))
    out = jax.ShapeDtypeStruct((1, nv), F32)
    return _pcall(body, name="adam_small", out_shape=[out] * 4, grid=(1,),
                  in_specs=[pl.BlockSpec((N_DEV, nv), lambda i: (0, 0)), vec, vec, vec],
                  out_specs=[vec] * 4)(parts, w, m, v)


def _pad_lanes(v, width=LANES):
    return jnp.pad(v, ((0, 0), (0, width - v.shape[1])))


def kernel(x, c, w_mod, b_mod, g_pre_mix, g_post_mix, w_in, b_forget, swa_sinks, w_out, g_pre_mlp, g_post_mlp, w_up, w_down, loss_target, m_w_mod, m_b_mod, m_g_pre_mix, m_g_post_mix, m_w_in, m_b_forget, m_swa_sinks, m_w_out, m_g_pre_mlp, m_g_post_mlp, m_w_up, m_w_down, v_w_mod, v_b_mod, v_g_pre_mix, v_g_post_mix, v_w_in, v_b_forget, v_swa_sinks, v_w_out, v_g_pre_mlp, v_g_post_mlp, v_w_up, v_w_down):
    ax, ay, ac = _position()
    me = 4 * ax + 2 * ay + ac
    x, target = x[0], loss_target[0]
    t, d = x.shape
    w_mod, w_in, w_out, w_up, w_down = w_mod[0], w_in[0], w_out[0], w_up[0], w_down[0]
    mod_w = w_mod.shape[1]
    in_w = w_in.shape[1]
    in_total = N_DEV * in_w
    shard_ff = w_up.shape[1]
    n_fox3 = 3 * FOX_W
    n_swa3 = SWA_W + 2 * KV_W
    assert in_total == n_fox3 + N_FOX + n_swa3 and d == FOX_W + SWA_W

    c_all = _all_gather([c], "gather_c")[0].reshape(N_DEV, d)
    b_part = lax.dynamic_slice(b_mod, (0, me * mod_w), (1, mod_w))
    mod_parts = _all_gather([_mod_part(c_all, w_mod, b_part)], "gather_mod")[0]
    mod = lax.dynamic_index_in_dim(mod_parts, me, axis=1, keepdims=False).reshape(1, N_DEV * mod_w)

    w_in_b, mod = lax.optimization_barrier((w_in.astype(BF16), mod))
    first = _ag_start([w_in_b], "ag_start_in")
    behind_first = first[4][0, 0]
    rest = _ag_start([(w + behind_first).astype(BF16) for w in (w_out, w_up, w_down)], "ag_start_rest")
    ag_send, ag_recv, ag_shard, ag_land = [a + b for a, b in zip(first[:4], rest[:4])]
    ag_token = rest[4]

    def gathered(i, after, name):
        shard, land = _ag_wait(ag_send[i], ag_recv[i], ag_shard[i], ag_land[i], after, "ag_wait_" + name)
        return lax.dynamic_update_slice(_ag_forward(land, "ag_fwd_" + name), shard[None], (me, 0, 0))

    def gathered_start(i, after, name):
        shard, land = _ag_wait(ag_send[i], ag_recv[i], ag_shard[i], ag_land[i], after, "ag_wait_" + name)
        return shard, _fwd_start(land, "fwd_start_" + name)

    def gathered_finish(started, after, name):
        shard, (send, recv, land, _) = started
        return lax.dynamic_update_slice(_fwd_wait(send, recv, land, after, "fwd_wait_" + name), shard[None], (me, 0, 0))

    sh_a, sc_a, gt_a, sh_m, sc_m, gt_m = [mod[:, i * d:(i + 1) * d] for i in range(6)]

    half = HEAD_DIM // 2
    inv_freq = 1.0 / (ROPE_THETA ** (jnp.arange(half, dtype=F32) * (2.0 / HEAD_DIM)))
    ang = jnp.arange(t).astype(F32)[:, None] * inv_freq[None, :]
    cos = jnp.concatenate([jnp.cos(ang), jnp.cos(ang)], axis=1)
    sin = jnp.concatenate([-jnp.sin(ang), jnp.sin(ang)], axis=1)

    b128 = _pad_lanes(b_forget)
    sinks128 = _pad_lanes(swa_sinks)

    h1 = _pre_attn(x, g_pre_mix + ag_token[0:1, 0:1], sc_a, sh_a)
    w_in_g = gathered(0, h1, "in")
    o_fg, o_sq = n_fox3, n_fox3 + N_FOX

    def cols(lo, hi):
        parts = []
        for j in range(lo // in_w, (hi - 1) // in_w + 1):
            parts.append(w_in_g[j, :, max(lo - j * in_w, 0):min(hi - j * in_w, in_w)])
        return parts

    w_in_r = jnp.concatenate(cols(0, o_fg) + cols(o_sq, in_total) + cols(o_fg, o_sq)
                             + [jnp.zeros((d, FG_PAD - N_FOX), BF16)], axis=1)
    proj_a = _matmul(h1, w_in_r, name="proj_a", n_cols=n_fox3, n_off=0)
    proj_b = _matmul(h1, w_in_r, name="proj_b", n_cols=n_swa3, n_off=n_fox3, tn=512, row_extras=(cos, sin),
                     epilogue=lambda acc, j, cs, sn: (_rope_cols(acc, j, cs, sn, SWA_W + KV_W),))
    out_started = gathered_start(1, proj_a, "out")
    fg = _matmul(h1, w_in_r, name="proj_fg", n_cols=FG_PAD, n_off=n_fox3 + n_swa3, tn=FG_PAD,
                 out_dtypes=(F32,), after=out_started[1][3])[:, 0:LANES]
    cum, cumt = _cum_fwd(fg, b128)
    fox_o, fox_lse = _fox_fwd(proj_a, cum, cumt)
    up_started = gathered_start(2, fox_o, "up")
    swa_o, swa_lse = _swa_fwd(proj_b, sinks128 + up_started[1][3][0:1, 0:1])
    w_out_full = gathered_finish(out_started, swa_o, "out").reshape(d, d)
    attn = jnp.concatenate([fox_o, swa_o], axis=1)
    mix = _matmul(attn, w_out_full, name="out_proj", out_dtypes=(F32,))
    x2, h2 = _post_mix(x, mix, gt_a, g_post_mix, g_pre_mlp, sc_m, sh_m)
    w_up_g = gathered_finish(up_started, h2, "up")
    u, act = _matmul(h2, w_up_g, name="mlp_up", b_sharded=True, out_dtypes=(BF16, BF16),
                     epilogue=lambda acc, j: (acc, jnp.square(jnp.maximum(acc, 0.0))))
    w_down_full = gathered(3, act, "down").reshape(N_DEV * shard_ff, d)
    y = _matmul(act, w_down_full, name="mlp_down", tk=4096, out_dtypes=(F32,))

    core = jnp.reshape(ac, (1,)).astype(jnp.int32)

    def reduce_start(started, after, name):
        send, recv, src, land, _ = started
        full, from_sibling = _sib_wait(send, recv, src, land, after, "sib_wait_" + name)
        return _rs_start(_chip_sum(full, from_sibling, core, "chip_sum_" + name), "rs_start_" + name)

    def tok(started):
        return started[4][0:1, 0:1]

    idx = jnp.stack([2 * ax + ay, 2 * (1 - ax) + ay, 2 * ax + (1 - ay), 2 * (1 - ax) + (1 - ay)]).astype(jnp.int32)

    def reduce_finish(started, after, w, m, v, name):
        send, recv, src, land, _ = started
        own, landed = _rs_wait(send, recv, src, land, after, "rs_wait_" + name)
        return _adam_pieces(w, m[0], v[0], own, landed, idx, "adam_" + name)

    dy, dout, dgt_m, dg3, loss_vec = _final(y, x2, target, gt_m, g_post_mlp)
    du = _matmul(dy, w_down_full, name="d_act", tb=True, tile_extras=(u,),
                 epilogue=lambda acc, j, uu: (acc * (2.0 * jnp.maximum(uu.astype(F32), 0.0)),))
    dw_down = _matmul(act, dy, name="dw_down", ta=True)
    sb_down = _sib_start(dw_down.reshape(N_DEV, shard_ff, d), "sib_start_down")
    dh2 = _matmul(du, w_up_g, name="d_h2", tb=True, b_sharded=True, b_pair=True, out_dtypes=(F32,),
                  after=sb_down[4])
    rs_down = reduce_start(sb_down, dh2, "down")
    per = shard_ff // _blk(shard_ff, 1024)
    dw_up = _matmul(h2, du, name="dw_up", ta=True, tn=_blk(shard_ff, 1024), out_shape=(N_DEV, d, shard_ff),
                    out_map=lambda tm, tn: pl.BlockSpec((None, tm, tn), lambda i, j, kk: (j // per, i, j % per)),
                    after=rs_down[4])
    sb_up = _sib_start(dw_up, "sib_start_up")
    dmix, dx2, dsh_m, dsc_m, dg2, dgt_a, dg1 = _mid_bwd(
        dh2, dout, x2, mix, g_pre_mlp + tok(sb_up), sc_m, gt_a, g_post_mix)
    d_attn = _matmul(dmix, w_out_full, name="d_attn", tb=True)
    rs_up = reduce_start(sb_up, d_attn, "up")
    dw_out = _matmul(attn, dmix, name="dw_out", ta=True, after=rs_up[4])
    sb_out = _sib_start(dw_out.reshape(N_DEV, d // N_DEV, d), "sib_start_out")
    dqf, dkf, dvf, dcs = _fox_bwd(proj_a, d_attn, cum, cumt, fox_lse)
    dsq, dsk, dsv, dsinks = _swa_bwd(proj_b, d_attn, swa_lse, sinks128 + tok(sb_out), cos, sin)
    rs_out = reduce_start(sb_out, dsq, "out")
    dfg, db_forget = _fg_bwd(dcs, fg, b128 + tok(rs_out))
    dproj = jnp.concatenate([dqf, dkf, dvf, dsq, dsk, dsv, _pad_lanes(dfg, FG_PAD)], axis=1)
    dw_in_r = _matmul(dproj, h1, name="dw_in", ta=True)

    def shard_rows(j):
        lo, hi = j * in_w, (j + 1) * in_w
        parts = []
        for seg_lo, seg_hi, shift in ((0, o_fg, 0), (o_fg, o_sq, n_swa3), (o_sq, in_total, -N_FOX)):
            a, b = max(lo, seg_lo), min(hi, seg_hi)
            if a < b:
                parts.append(dw_in_r[a + shift:b + shift, :])
        return parts[0] if len(parts) == 1 else jnp.concatenate(parts, axis=0)

    sb_in = _sib_start(jnp.stack([shard_rows(j) for j in range(N_DEV)]), "sib_start_in")
    dh1 = _matmul(dproj, w_in_r, name="d_h1", tb=True, tk=2560, out_dtypes=(F32,), after=sb_in[4])
    grad_x, dsh_a, dsc_a, dg0 = _x_bwd(dh1, dx2, x, g_pre_mix, sc_a)

    small = jnp.concatenate([dsh_a, dsc_a, dgt_a, dsh_m, dsc_m, dgt_m, dg0, dg1, dg2, dg3, db_forget, dsinks,
                             loss_vec[:, 0:LANES]], axis=1)
    small_all = _all_gather([small], "gather_small")[0].reshape(N_DEV, small.shape[1])
    rs_in = reduce_start(sb_in, small_all, "in")

    pack = lambda bm, g0_, g1_, g2_, g3_, bf_, sk_: jnp.concatenate(
        [bm, g0_, g1_, g2_, g3_, _pad_lanes(bf_), _pad_lanes(sk_), jnp.zeros((1, LANES), F32)], axis=1)
    p_small = pack(b_mod, g_pre_mix, g_post_mix, g_pre_mlp, g_post_mlp, b_forget, swa_sinks)
    m_small = pack(m_b_mod, m_g_pre_mix, m_g_post_mix, m_g_pre_mlp, m_g_post_mlp, m_b_forget, m_swa_sinks)
    v_small = pack(v_b_mod, v_g_pre_mix, v_g_post_mix, v_g_pre_mlp, v_g_post_mlp, v_b_forget, v_swa_sinks)
    small_out = _adam_small(small_all, p_small + tok(rs_in), m_small, v_small)

    n_mod = 6 * d

    def unpack(vec):
        o = n_mod
        return (vec[:, 0:n_mod], vec[:, o:o + d], vec[:, o + d:o + 2 * d], vec[:, o + 2 * d:o + 3 * d],
                vec[:, o + 3 * d:o + 4 * d], vec[:, o + 4 * d:o + 4 * d + N_FOX],
                vec[:, o + 4 * d + LANES:o + 4 * d + LANES + N_SWA])

    loss = small_out[0][0, n_mod + 4 * d + 2 * LANES]
    g_small, d_small, nm_small, nv_small = [unpack(vec) for vec in small_out]

    dmod_cols = lax.dynamic_slice(small_all, (0, me * mod_w), (N_DEV, mod_w))
    g_w_mod, d_w_mod, nm_w_mod, nv_w_mod = _adam_mod(c_all + tok(rs_in), dmod_cols, w_mod, m_w_mod[0], v_w_mod[0])

    g_w_down, d_w_down, nm_w_down, nv_w_down = reduce_finish(rs_down, d_w_mod, w_down, m_w_down, v_w_down, "w_down")
    g_w_up, d_w_up, nm_w_up, nv_w_up = reduce_finish(rs_up, d_w_down, w_up, m_w_up, v_w_up, "w_up")
    g_w_out, d_w_out, nm_w_out, nv_w_out = reduce_finish(rs_out, d_w_up, w_out, m_w_out, v_w_out, "w_out")
    g_w_in, d_w_in, nm_w_in, nv_w_in = reduce_finish(rs_in, d_w_out, jnp.transpose(w_in), (jnp.transpose(m_w_in[0]),),
                                                     (jnp.transpose(v_w_in[0]),), "w_in")

    def assemble(w_mod_, small_, w_in_, w_out_, w_up_, w_down_):
        b_mod_, g0_, g1_, g2_, g3_, bf_, sk_ = small_
        return [w_mod_[None], b_mod_, g0_, g1_, jnp.transpose(w_in_)[None], bf_, sk_, w_out_[None], g2_, g3_,
                w_up_[None], w_down_[None]]

    outs = [loss, grad_x[None]]
    outs += assemble(g_w_mod, g_small, g_w_in, g_w_out, g_w_up, g_w_down)
    outs += assemble(d_w_mod, d_small, d_w_in, d_w_out, d_w_up, d_w_down)
    outs += assemble(nm_w_mod, nm_small, nm_w_in, nm_w_out, nm_w_up, nm_w_down)
    outs += assemble(nv_w_mod, nv_small, nv_w_in, nv_w_out, nv_w_up, nv_w_down)
    return tuple(outs)
```

```python
import functools

import jax
import jax.numpy as jnp
from jax import lax
from jax.experimental import pallas as pl
from jax.experimental.pallas import tpu as pltpu

F32 = jnp.float32
BF16 = jnp.bfloat16
MESH = pl.DeviceIdType.MESH

N_DEV = 8
N_CHIP = 4
LANES = 128
HEAD_DIM = 128
N_FOX = 8
N_SWA = 8
N_KV = 2
GQA = N_SWA // N_KV
WINDOW = 128
FOX_W = N_FOX * HEAD_DIM
SWA_W = N_SWA * HEAD_DIM
KV_W = N_KV * HEAD_DIM
ROPE_THETA = 10000.0
NORM_EPS = 1e-6
ATT_SCALE = HEAD_DIM ** -0.5
FG_PAD = 512

ADAM_LR = 0.001
ADAM_B1 = 0.9
ADAM_B2 = 0.999
ADAM_EPS = 1e-08
ADAM_WD = 0.01
ADAM_STEP = 10

VMEM_LIMIT = 56 * 1024 * 1024

NT_DIMS = (((1,), (1,)), ((), ()))
TN_DIMS = (((0,), (0,)), ((), ()))
NN_DIMS = (((1,), (0,)), ((), ()))


def _pcall(body, *, name, out_shape, grid=(), in_specs=None, out_specs=None, scratch_shapes=(), grid_spec=None):
    params = pltpu.CompilerParams(vmem_limit_bytes=VMEM_LIMIT)
    if grid_spec is not None:
        return pl.pallas_call(body, name=name, out_shape=out_shape, grid_spec=grid_spec, compiler_params=params)
    return pl.pallas_call(body, name=name, out_shape=out_shape, grid=grid, in_specs=in_specs, out_specs=out_specs,
                          scratch_shapes=scratch_shapes, compiler_params=params)


def _blk(n, pref):
    if n <= pref:
        return n
    b = (pref // LANES) * LANES
    while n % b:
        b -= LANES
    return b


def _position():
    return lax.axis_index("x"), lax.axis_index("y"), lax.axis_index("c")


ANY = pl.BlockSpec(memory_space=pl.ANY)


def _all_gather(arrs, name):
    n = len(arrs)

    def body(*refs):
        ins, outs = refs[:n], refs[n:2 * n]
        send_sems, recv_sems, local_sems = refs[2 * n:]
        x, y, c = _position()
        me, sibling = (x, y, c), (x, y, 1 - c)
        chips = [(1 - x, y), (x, 1 - y), (1 - x, 1 - y)]

        def slot(p):
            return 4 * p[0] + 2 * p[1] + p[2]

        def copy(a, k, block, to, src=None):
            dst = outs[a].at[slot(block)]
            return pltpu.make_async_remote_copy(
                src_ref=dst if src is None else src, dst_ref=dst,
                send_sem=send_sems.at[7 * a + k], recv_sem=recv_sems.at[7 * a + k],
                device_id=to, device_id_type=MESH)

        mine = [pltpu.make_async_copy(ins[a], outs[a].at[slot(me)], local_sems.at[a]) for a in range(n)]
        for cp in mine:
            cp.start()
        first = []
        for a in range(n):
            first.append(copy(a, 0, me, sibling, src=ins[a]))
            first += [copy(a, 1 + j, me, (*chip, c), src=ins[a]) for j, chip in enumerate(chips)]
        for cp in first:
            cp.start()
        passed = []
        for a in range(n):
            for j, chip in enumerate(chips):
                copy(a, 1 + j, (*chip, c), me).wait_recv()
                cp = copy(a, 4 + j, (*chip, c), sibling)
                cp.start()
                passed.append(cp)
        for a in range(n):
            copy(a, 0, sibling, me).wait_recv()
            for j, chip in enumerate(chips):
                copy(a, 4 + j, (*chip, 1 - c), me).wait_recv()
        for cp in first + passed:
            cp.wait_send()
        for cp in mine:
            cp.wait()

    return _pcall(
        body, name=name,
        out_shape=[jax.ShapeDtypeStruct((N_DEV,) + a.shape, a.dtype) for a in arrs],
        in_specs=[ANY] * n, out_specs=[ANY] * n,
        scratch_shapes=[pltpu.SemaphoreType.DMA((7 * n,)), pltpu.SemaphoreType.DMA((7 * n,)),
                        pltpu.SemaphoreType.DMA((n,))],
    )(*arrs)


HBM = pl.BlockSpec(memory_space=pltpu.HBM)
SEM = pl.BlockSpec(memory_space=pltpu.SEMAPHORE)
EFFECT = pltpu.SideEffectType.DATAFLOW_SIDE_EFFECTING


def _hbm(a):
    return pltpu.with_memory_space_constraint(a, pltpu.HBM)


def _gather_peers():
    x, y, c = _position()
    return [(x, y, 1 - c), (1 - x, y, c), (x, 1 - y, c), (1 - x, 1 - y, c)]


def _ag_start(shards, name):
    n = len(shards)
    lands = [_hbm(lax.empty((N_DEV,) + s.shape, s.dtype)) for s in shards]

    def body(*refs):
        srcs, land, send, recv = refs[:n], refs[n:2 * n], refs[2 * n:3 * n], refs[3 * n:4 * n]
        token = refs[6 * n]
        x, y, c = _position()
        for a in range(n):
            for k, to in enumerate(_gather_peers()):
                pltpu.make_async_remote_copy(
                    src_ref=srcs[a], dst_ref=land[a].at[4 * x + 2 * y + c], send_sem=send[a].at[k],
                    recv_sem=recv[a].at[k], device_id=to, device_id_type=MESH).start()
        token[...] = jnp.zeros_like(token)

    sems = [pltpu.SemaphoreType.DMA((4,))] * (2 * n)
    out = pl.pallas_call(
        body, name=name,
        out_shape=sems + [pltpu.HBM(s.shape, s.dtype) for s in shards] + [pltpu.HBM(l.shape, l.dtype) for l in lands]
        + [jax.ShapeDtypeStruct((8, LANES), F32)],
        in_specs=[HBM] * (2 * n), out_specs=[SEM] * (2 * n) + [HBM] * (2 * n) + [pl.BlockSpec(memory_space=pltpu.VMEM)],
        input_output_aliases={**{a: 2 * n + a for a in range(n)}, **{n + a: 3 * n + a for a in range(n)}},
        compiler_params=pltpu.CompilerParams(has_side_effects=EFFECT),
    )(*[_hbm(s) for s in shards], *lands)
    return out[:n], out[n:2 * n], out[2 * n:3 * n], out[3 * n:4 * n], out[4 * n]


def _ag_wait(send, recv, shard_thru, land_thru, after, name):
    def body(v_ref, land_ref, send_sem, recv_sem, after_ref, v_dead, got_ref):
        for k, to in enumerate(_gather_peers()):
            cp = pltpu.make_async_remote_copy(
                src_ref=v_ref, dst_ref=land_ref.at[0], send_sem=send_sem.at[k], recv_sem=recv_sem.at[k],
                device_id=to, device_id_type=MESH)
            cp.wait_send()
            cp.wait_recv()

    return pl.pallas_call(
        body, name=name,
        out_shape=(pltpu.HBM(shard_thru.shape, shard_thru.dtype), pltpu.HBM(land_thru.shape, land_thru.dtype)),
        in_specs=(HBM, HBM, SEM, SEM, ANY), out_specs=(HBM, HBM), input_output_aliases={0: 0, 1: 1},
        compiler_params=pltpu.CompilerParams(has_side_effects=EFFECT),
    )(shard_thru, land_thru, send, recv, after)


def _ag_forward(land, name):
    def body(land_in, land_ref, send_sems, recv_sems):
        x, y, c = _position()
        copies = []
        for j, (px, py) in enumerate([(1 - x, y), (x, 1 - y), (1 - x, 1 - y)]):
            block = land_ref.at[4 * px + 2 * py + c]
            cp = pltpu.make_async_remote_copy(src_ref=block, dst_ref=block, send_sem=send_sems.at[j],
                                              recv_sem=recv_sems.at[j], device_id=(x, y, 1 - c), device_id_type=MESH)
            cp.start()
            copies.append(cp)
        for cp in copies:
            cp.wait()

    return pl.pallas_call(
        body, name=name, out_shape=jax.ShapeDtypeStruct(land.shape, land.dtype),
        in_specs=[ANY], out_specs=ANY, input_output_aliases={0: 0},
        scratch_shapes=[pltpu.SemaphoreType.DMA((3,)), pltpu.SemaphoreType.DMA((3,))],
    )(land)


def _fwd_start(land, name):
    def body(land_ref, send, recv, land_thru, token):
        x, y, c = _position()
        for j, (px, py) in enumerate([(1 - x, y), (x, 1 - y), (1 - x, 1 - y)]):
            block = land_ref.at[4 * px + 2 * py + c]
            pltpu.make_async_remote_copy(src_ref=block, dst_ref=block, send_sem=send.at[j], recv_sem=recv.at[j],
                                         device_id=(x, y, 1 - c), device_id_type=MESH).start()
        token[...] = jnp.zeros_like(token)

    return pl.pallas_call(
        body, name=name,
        out_shape=[pltpu.SemaphoreType.DMA((3,)), pltpu.SemaphoreType.DMA((3,)), pltpu.HBM(land.shape, land.dtype),
                   jax.ShapeDtypeStruct((8, LANES), F32)],
        in_specs=[HBM], out_specs=[SEM, SEM, HBM, pl.BlockSpec(memory_space=pltpu.VMEM)],
        input_output_aliases={0: 2},
        compiler_params=pltpu.CompilerParams(has_side_effects=EFFECT),
    )(land)


def _fwd_wait(send, recv, land_thru, after, name):
    def body(land_ref, send_sem, recv_sem, after_ref, land_out):
        x, y, c = _position()
        for j in range(3):
            cp = pltpu.make_async_remote_copy(
                src_ref=land_ref.at[0], dst_ref=land_ref.at[0], send_sem=send_sem.at[j], recv_sem=recv_sem.at[j],
                device_id=(x, y, 1 - c), device_id_type=MESH)
            cp.wait_send()
            cp.wait_recv()

    return pl.pallas_call(
        body, name=name, out_shape=pltpu.HBM(land_thru.shape, land_thru.dtype),
        in_specs=(HBM, SEM, SEM, ANY), out_specs=HBM, input_output_aliases={0: 0},
        compiler_params=pltpu.CompilerParams(has_side_effects=EFFECT),
    )(land_thru, send, recv, after)


def _rs_peers():
    x, y, c = _position()
    return [(1 - x, y, c), (x, 1 - y, c), (1 - x, 1 - y, c)]


def _rs_start(chip_sums, name):
    land = _hbm(lax.empty(chip_sums.shape, chip_sums.dtype))

    def body(src, land_ref, send, recv, src_thru, land_thru, token):
        x, y, c = _position()
        for j, (px, py, pc) in enumerate(_rs_peers()):
            pltpu.make_async_remote_copy(
                src_ref=src.at[2 * px + py], dst_ref=land_ref.at[2 * x + y], send_sem=send.at[j], recv_sem=recv.at[j],
                device_id=(px, py, pc), device_id_type=MESH).start()
        token[...] = jnp.zeros_like(token)

    return pl.pallas_call(
        body, name=name,
        out_shape=[pltpu.SemaphoreType.DMA((3,)), pltpu.SemaphoreType.DMA((3,)),
                   pltpu.HBM(chip_sums.shape, chip_sums.dtype), pltpu.HBM(land.shape, land.dtype),
                   jax.ShapeDtypeStruct((8, LANES), F32)],
        in_specs=[HBM, HBM], out_specs=[SEM, SEM, HBM, HBM, pl.BlockSpec(memory_space=pltpu.VMEM)],
        input_output_aliases={0: 2, 1: 3},
        compiler_params=pltpu.CompilerParams(has_side_effects=EFFECT),
    )(_hbm(chip_sums), land)


def _rs_wait(send, recv, src_thru, land_thru, after, name):
    def body(src, land_ref, send_sem, recv_sem, after_ref, src_out, land_out):
        for j, to in enumerate(_rs_peers()):
            cp = pltpu.make_async_remote_copy(
                src_ref=src.at[0], dst_ref=land_ref.at[0], send_sem=send_sem.at[j], recv_sem=recv_sem.at[j],
                device_id=to, device_id_type=MESH)
            cp.wait_send()
            cp.wait_recv()

    return pl.pallas_call(
        body, name=name,
        out_shape=(pltpu.HBM(src_thru.shape, src_thru.dtype), pltpu.HBM(land_thru.shape, land_thru.dtype)),
        in_specs=(HBM, HBM, SEM, SEM, ANY), out_specs=(HBM, HBM), input_output_aliases={0: 0, 1: 1},
        compiler_params=pltpu.CompilerParams(has_side_effects=EFFECT),
    )(src_thru, land_thru, send, recv, after)


def _sib_start(full, name):
    land = _hbm(lax.empty((N_CHIP,) + full.shape[1:], full.dtype))

    def body(src, land_ref, send, recv, src_thru, land_thru, token):
        x, y, c = _position()
        for k in range(N_CHIP):
            pltpu.make_async_remote_copy(
                src_ref=src.at[2 * k + (1 - c)], dst_ref=land_ref.at[k], send_sem=send.at[k], recv_sem=recv.at[k],
                device_id=(x, y, 1 - c), device_id_type=MESH).start()
        token[...] = jnp.zeros_like(token)

    return pl.pallas_call(
        body, name=name,
        out_shape=[pltpu.SemaphoreType.DMA((N_CHIP,)), pltpu.SemaphoreType.DMA((N_CHIP,)),
                   pltpu.HBM(full.shape, full.dtype), pltpu.HBM(land.shape, land.dtype),
                   jax.ShapeDtypeStruct((8, LANES), F32)],
        in_specs=[HBM, HBM], out_specs=[SEM, SEM, HBM, HBM, pl.BlockSpec(memory_space=pltpu.VMEM)],
        input_output_aliases={0: 2, 1: 3},
        compiler_params=pltpu.CompilerParams(has_side_effects=EFFECT),
    )(_hbm(full), land)


def _sib_wait(send, recv, src_thru, land_thru, after, name):
    def body(src, land_ref, send_sem, recv_sem, after_ref, src_out, land_out):
        x, y, c = _position()
        for k in range(N_CHIP):
            cp = pltpu.make_async_remote_copy(
                src_ref=src.at[0], dst_ref=land_ref.at[0], send_sem=send_sem.at[k], recv_sem=recv_sem.at[k],
                device_id=(x, y, 1 - c), device_id_type=MESH)
            cp.wait_send()
            cp.wait_recv()

    return pl.pallas_call(
        body, name=name,
        out_shape=(pltpu.HBM(src_thru.shape, src_thru.dtype), pltpu.HBM(land_thru.shape, land_thru.dtype)),
        in_specs=(HBM, HBM, SEM, SEM, ANY), out_specs=(HBM, HBM), input_output_aliases={0: 0, 1: 1},
        compiler_params=pltpu.CompilerParams(has_side_effects=EFFECT),
    )(src_thru, land_thru, send, recv, after)


def _rs_sibling(arrs, name):
    n = len(arrs)

    def body(*refs):
        ins, outs = refs[:n], refs[n:2 * n]
        send_sems, recv_sems = refs[2 * n:]
        x, y, c = _position()
        copies = []
        for a in range(n):
            for k in range(N_CHIP):
                cp = pltpu.make_async_remote_copy(
                    src_ref=ins[a].at[2 * k + (1 - c)], dst_ref=outs[a].at[k],
                    send_sem=send_sems.at[N_CHIP * a + k], recv_sem=recv_sems.at[N_CHIP * a + k],
                    device_id=(x, y, 1 - c), device_id_type=MESH)
                cp.start()
                copies.append(cp)
        for cp in copies:
            cp.wait()

    return _pcall(
        body, name=name,
        out_shape=[jax.ShapeDtypeStruct((N_CHIP,) + a.shape[1:], a.dtype) for a in arrs],
        in_specs=[ANY] * n, out_specs=[ANY] * n,
        scratch_shapes=[pltpu.SemaphoreType.DMA((N_CHIP * n,)), pltpu.SemaphoreType.DMA((N_CHIP * n,))],
    )(*arrs)


def _chip_sum(full, recv, core, name):
    _, rows, cols = full.shape
    tr = _blk(rows, 1024) if rows % LANES == 0 else rows

    def body(core_ref, a_ref, b_ref, o_ref):
        o_ref[...] = (a_ref[...].astype(F32) + b_ref[...].astype(F32)).astype(o_ref.dtype)

    grid_spec = pltpu.PrefetchScalarGridSpec(
        num_scalar_prefetch=1, grid=(N_CHIP, rows // tr),
        in_specs=[pl.BlockSpec((None, tr, cols), lambda k, i, core_ref: (2 * k + core_ref[0], i, 0)),
                  pl.BlockSpec((None, tr, cols), lambda k, i, core_ref: (k, i, 0))],
        out_specs=pl.BlockSpec((None, tr, cols), lambda k, i, core_ref: (k, i, 0)))
    return _pcall(body, name=name, out_shape=jax.ShapeDtypeStruct((N_CHIP, rows, cols), full.dtype),
                  grid_spec=grid_spec)(core, full, recv)


def _matmul(a, b, *, name, ta=False, tb=False, tm=1024, tn=1024, tk=2048, out_dtypes=(BF16,), epilogue=None,
            row_extras=(), tile_extras=(), out_shape=None, out_map=None, b_sharded=False, n_cols=None, n_off=0,
            after=None, b_pair=False):
    m, k = (a.shape[1], a.shape[0]) if ta else a.shape
    if b_sharded:
        shard_c = b.shape[2]
        n, kb = (b.shape[1], N_DEV * shard_c) if tb else (N_DEV * shard_c, b.shape[1])
        tn, tk = (tn, min(tk, shard_c)) if tb else (min(tn, shard_c), tk)
        if b_pair:
            assert tb and tk == shard_c
            tk = 2 * shard_c
    else:
        n, kb = b.shape if tb else (b.shape[1], b.shape[0])
    assert kb == k, (name, kb, k)
    if n_cols is not None:
        n = n_cols
    tm, tn, tk = _blk(m, tm), _blk(n, tn), _blk(k, tk)
    assert n_off % tn == 0
    nk = k // tk
    dims = (((0 if ta else 1,), (1 if tb else 0,)), ((), ()))
    behind = () if after is None else (after,)
    n_row, n_tile, n_out = len(row_extras), len(tile_extras), len(out_dtypes)
    n_b = 2 if b_pair else 1
    first_out = 1 + n_b + n_row + n_tile + len(behind)

    def body(*refs):
        a_ref, b_ref = refs[:2]
        extras = refs[1 + n_b:1 + n_b + n_row + n_tile]
        outs = refs[first_out:first_out + n_out]
        acc_ref = refs[-1]
        jj, kk = pl.program_id(1), pl.program_id(2)
        if b_pair:
            half = tk // 2
            part = (lax.dot_general(a_ref[:, 0:half].astype(BF16), b_ref[...].astype(BF16), dims,
                                    preferred_element_type=F32)
                    + lax.dot_general(a_ref[:, half:tk].astype(BF16), refs[2][...].astype(BF16), dims,
                                      preferred_element_type=F32))
        else:
            part = lax.dot_general(a_ref[...].astype(BF16), b_ref[...].astype(BF16), dims,
                                   preferred_element_type=F32)

        def finish(acc):
            res = (acc,) if epilogue is None else epilogue(acc, jj, *[e[...] for e in extras])
            for o_ref, r in zip(outs, res):
                o_ref[...] = r.astype(o_ref.dtype)

        if nk == 1:
            finish(part)
        else:
            @pl.when(kk == 0)
            def _():
                acc_ref[...] = part

            @pl.when(kk > 0)
            def _():
                acc_ref[...] += part

            @pl.when(kk == nk - 1)
            def _():
                finish(acc_ref[...])

    a_spec = pl.BlockSpec((tk, tm), lambda i, j, kk: (kk, i)) if ta else pl.BlockSpec((tm, tk), lambda i, j, kk: (i, kk))
    if b_pair:
        b_spec = pl.BlockSpec((None, tn, shard_c), lambda i, j, kk: (2 * kk, j, 0))
    elif b_sharded and tb:
        per = shard_c // tk
        b_spec = pl.BlockSpec((None, tn, tk), lambda i, j, kk: (kk // per, j, kk % per))
    elif b_sharded:
        per = shard_c // tn
        b_spec = pl.BlockSpec((None, tk, tn), lambda i, j, kk: (j // per, kk, j % per))
    elif tb:
        b_spec = pl.BlockSpec((tn, tk), lambda i, j, kk: (j + n_off // tn, kk))
    else:
        b_spec = pl.BlockSpec((tk, tn), lambda i, j, kk: (kk, j + n_off // tn))
    in_specs = [a_spec, b_spec]
    if b_pair:
        in_specs.append(pl.BlockSpec((None, tn, shard_c), lambda i, j, kk: (2 * kk + 1, j, 0)))
    in_specs += [pl.BlockSpec((tm, LANES), lambda i, j, kk: (i, 0)) for _ in row_extras]
    in_specs += [pl.BlockSpec((tm, tn), lambda i, j, kk: (i, j)) for _ in tile_extras]
    in_specs += [ANY for _ in behind]
    if out_map is None:
        out_specs = [pl.BlockSpec((tm, tn), lambda i, j, kk: (i, j)) for _ in out_dtypes]
        shapes = [jax.ShapeDtypeStruct((m, n), dt) for dt in out_dtypes]
    else:
        out_specs = [out_map(tm, tn)]
        shapes = [jax.ShapeDtypeStruct(out_shape, out_dtypes[0])]
    acc_shape = (tm, tn) if nk > 1 else (8, LANES)
    res = _pcall(body, name=name, out_shape=shapes, grid=(m // tm, n // tn, nk), in_specs=in_specs,
                 out_specs=out_specs, scratch_shapes=[pltpu.VMEM(acc_shape, F32)])(
                     a, *([b, b] if b_pair else [b]), *row_extras, *tile_extras, *behind)
    return res[0] if n_out == 1 else res


def _rope_cols(acc, j, cos, sin, n_rope):
    width = acc.shape[1]
    parts = []
    for g in range(width // HEAD_DIM):
        xg = acc[:, g * HEAD_DIM:(g + 1) * HEAD_DIM]
        roped = xg * cos + pltpu.roll(xg, HEAD_DIM // 2, 1) * sin
        parts.append(jnp.where(j * width + g * HEAD_DIM < n_rope, roped, xg))
    return jnp.concatenate(parts, axis=1) if len(parts) > 1 else parts[0]


def _silu(v):
    return v / (1.0 + jnp.exp(-v))


def _mod_part(c_all, w_mod, b_part):
    d, w = w_mod.shape
    tk = _blk(d, 512)

    def body(c_ref, w_ref, b_ref, o_ref):
        kk = pl.program_id(0)
        cond = _silu(c_ref[...]).astype(BF16)
        part = jnp.dot(cond, w_ref[...].astype(BF16), preferred_element_type=F32)

        @pl.when(kk == 0)
        def _():
            o_ref[...] = part + b_ref[...]

        @pl.when(kk > 0)
        def _():
            o_ref[...] += part

    return _pcall(body, name="mod_part", out_shape=jax.ShapeDtypeStruct((N_DEV, w), F32), grid=(d // tk,),
                  in_specs=[pl.BlockSpec((N_DEV, tk), lambda kk: (0, kk)), pl.BlockSpec((tk, w), lambda kk: (kk, 0)),
                            pl.BlockSpec((1, w), lambda kk: (0, 0))],
                  out_specs=pl.BlockSpec((N_DEV, w), lambda kk: (0, 0)))(c_all, w_mod, b_part)


def _row_call(body, name, t, d, tiled_in, vec_in, tiled_out_dtypes, n_vec_out, tr=256):
    tr = _blk(t, tr)
    tile = pl.BlockSpec((tr, d), lambda i: (i, 0))
    vec = pl.BlockSpec((1, d), lambda i: (0, 0))
    out_shape = [jax.ShapeDtypeStruct((t, d), dt) for dt in tiled_out_dtypes]
    out_shape += [jax.ShapeDtypeStruct((1, d), F32)] * n_vec_out
    return _pcall(body, name=name, out_shape=out_shape, grid=(t // tr,),
                  in_specs=[tile] * len(tiled_in) + [vec] * len(vec_in),
                  out_specs=[tile] * len(tiled_out_dtypes) + [vec] * n_vec_out)(*tiled_in, *vec_in)


def _accumulate(ref, val):
    @pl.when(pl.program_id(0) == 0)
    def _():
        ref[...] = val

    @pl.when(pl.program_id(0) > 0)
    def _():
        ref[...] += val


def _rsum(v):
    return jnp.sum(v, axis=0, keepdims=True)


def _rms(v):
    return lax.rsqrt(jnp.mean(v * v, axis=-1, keepdims=True) + NORM_EPS)


def _rms_bwd(vhat, r, dvhat):
    return r * (dvhat - vhat * jnp.mean(dvhat * vhat, axis=-1, keepdims=True))


def _pre_attn(x, g0, sc_a, sh_a):
    def body(x_ref, g_ref, sc_ref, sh_ref, h_ref):
        xv = x_ref[...]
        h_ref[...] = (xv * _rms(xv) * g_ref[...] * (1.0 + sc_ref[...]) + sh_ref[...]).astype(BF16)

    t, d = x.shape
    return _row_call(body, "pre_attn", t, d, [x], [g0, sc_a, sh_a], [BF16], 0)[0]


def _post_mix(x, mix, gt_a, g1, g2, sc_m, sh_m):
    def body(x_ref, mix_ref, gt_ref, g1_ref, g2_ref, sc_ref, sh_ref, x2_ref, h2_ref):
        mv = mix_ref[...]
        x2 = x_ref[...] + gt_ref[...] * (mv * _rms(mv) * g1_ref[...])
        x2_ref[...] = x2
        h2_ref[...] = (x2 * _rms(x2) * g2_ref[...] * (1.0 + sc_ref[...]) + sh_ref[...]).astype(BF16)

    t, d = x.shape
    return _row_call(body, "post_mix", t, d, [x, mix], [gt_a, g1, g2, sc_m, sh_m], [F32, BF16], 0)


def _final(y, x2, target, gt_m, g3):
    t, d = y.shape

    def body(y_ref, x2_ref, tg_ref, gt_ref, g3_ref, dy_ref, dout_ref, dgt_ref, dg3_ref, loss_ref):
        yv = y_ref[...]
        r = _rms(yv)
        yhat = yv * r
        n3 = yhat * g3_ref[...]
        err = x2_ref[...] + gt_ref[...] * n3 - tg_ref[...]
        _accumulate(loss_ref, jnp.zeros((1, d), F32) + 0.5 * jnp.sum(err * err) / d)
        dout = err * (1.0 / d)
        dout_ref[...] = dout
        _accumulate(dgt_ref, _rsum(dout * n3))
        dn3 = dout * gt_ref[...]
        _accumulate(dg3_ref, _rsum(dn3 * yhat))
        dy_ref[...] = _rms_bwd(yhat, r, dn3 * g3_ref[...]).astype(BF16)

    return _row_call(body, "final", t, d, [y, x2, target], [gt_m, g3], [BF16, F32], 3)


def _mid_bwd(dh2, dout, x2, mix, g2, sc_m, gt_a, g1):
    t, d = x2.shape

    def body(dh2_ref, dout_ref, x2_ref, mix_ref, g2_ref, sc_ref, gt_ref, g1_ref,
             dmix_ref, dx2_ref, dsh_ref, dsc_ref, dg2_ref, dgt_ref, dg1_ref):
        dh2v = dh2_ref[...]
        x2v = x2_ref[...]
        r2 = _rms(x2v)
        x2hat = x2v * r2
        _accumulate(dsh_ref, _rsum(dh2v))
        _accumulate(dsc_ref, _rsum(dh2v * (x2hat * g2_ref[...])))
        dn2 = dh2v * (1.0 + sc_ref[...])
        _accumulate(dg2_ref, _rsum(dn2 * x2hat))
        dx2 = dout_ref[...] + _rms_bwd(x2hat, r2, dn2 * g2_ref[...])
        dx2_ref[...] = dx2
        mv = mix_ref[...]
        r1 = _rms(mv)
        mhat = mv * r1
        _accumulate(dgt_ref, _rsum(dx2 * (mhat * g1_ref[...])))
        dn1 = dx2 * gt_ref[...]
        _accumulate(dg1_ref, _rsum(dn1 * mhat))
        dmix_ref[...] = _rms_bwd(mhat, r1, dn1 * g1_ref[...]).astype(BF16)

    return _row_call(body, "mid_bwd", t, d, [dh2, dout, x2, mix], [g2, sc_m, gt_a, g1], [BF16, F32], 5)


def _x_bwd(dh1, dx2, x, g0, sc_a):
    t, d = x.shape

    def body(dh1_ref, dx2_ref, x_ref, g0_ref, sc_ref, dx_ref, dsh_ref, dsc_ref, dg0_ref):
        dh1v = dh1_ref[...]
        xv = x_ref[...]
        r0 = _rms(xv)
        xhat = xv * r0
        _accumulate(dsh_ref, _rsum(dh1v))
        _accumulate(dsc_ref, _rsum(dh1v * (xhat * g0_ref[...])))
        dn0 = dh1v * (1.0 + sc_ref[...])
        _accumulate(dg0_ref, _rsum(dn0 * xhat))
        dx_ref[...] = dx2_ref[...] + _rms_bwd(xhat, r0, dn0 * g0_ref[...])

    return _row_call(body, "x_bwd", t, d, [dh1, dx2, x], [g0, sc_a], [F32], 3)


def _pick_lane(block, h):
    lane = lax.broadcasted_iota(jnp.int32, block.shape, 1)
    return jnp.sum(jnp.where(lane == h, block, 0.0), axis=1, keepdims=True)


def _put_lane(ref, rows, h, col):
    old = ref[rows, :]
    lane = lax.broadcasted_iota(jnp.int32, old.shape, 1)
    ref[rows, :] = jnp.where(lane == h, col, old)


def _tri(n, lower):
    r = lax.broadcasted_iota(jnp.int32, (n, n), 0)
    c = lax.broadcasted_iota(jnp.int32, (n, n), 1)
    return jnp.where((c <= r) if lower else (c >= r), 1.0, 0.0).astype(F32)


def _cum_fwd(fg, b128):
    t = fg.shape[0]
    nb = t // LANES

    def body(fg_ref, b_ref, cum_ref, cumt_ref):
        tri = _tri(LANES, True)
        carry = jnp.zeros((1, LANES), F32)
        for i in range(nb):
            z = fg_ref[i * LANES:(i + 1) * LANES, :] + b_ref[...]
            lf = jnp.minimum(z, 0.0) - jnp.log(1.0 + jnp.exp(-jnp.abs(z)))
            blk = jnp.dot(tri, lf, precision=lax.Precision.HIGHEST, preferred_element_type=F32) + carry
            cum_ref[i * LANES:(i + 1) * LANES, :] = blk
            carry = blk[LANES - 1:LANES, :]
        cumt_ref[...] = cum_ref[...].T[0:N_FOX, :]

    return _pcall(body, name="cum_fwd",
                  out_shape=[jax.ShapeDtypeStruct((t, LANES), F32), jax.ShapeDtypeStruct((N_FOX, t), F32)],
                  grid=(1,),
                  in_specs=[pl.BlockSpec((t, LANES), lambda i: (0, 0)), pl.BlockSpec((1, LANES), lambda i: (0, 0))],
                  out_specs=[pl.BlockSpec((t, LANES), lambda i: (0, 0)), pl.BlockSpec((N_FOX, t), lambda i: (0, 0))],
                  )(fg, b128)


def _fg_bwd(dcs_rows, fg, b128):
    t = fg.shape[0]
    nb = t // LANES

    def body(dcs_ref, fg_ref, b_ref, dfg_ref, db_ref, dcum_ref):
        dcum_ref[...] = -jnp.concatenate([dcs_ref[...], jnp.zeros((LANES - N_FOX, t), F32)], axis=0).T
        tri = _tri(LANES, False)
        carry = jnp.zeros((1, LANES), F32)
        db = jnp.zeros((1, LANES), F32)
        for i in reversed(range(nb)):
            rows = slice(i * LANES, (i + 1) * LANES)
            dlf = jnp.dot(tri, dcum_ref[rows, :], precision=lax.Precision.HIGHEST, preferred_element_type=F32) + carry
            carry = dlf[0:1, :]
            z = fg_ref[rows, :] + b_ref[...]
            dfg = dlf / (1.0 + jnp.exp(z))
            dfg_ref[rows, :] = dfg.astype(BF16)
            db = db + _rsum(dfg)
        db_ref[...] = db

    full = pl.BlockSpec((t, LANES), lambda i: (0, 0))
    vec = pl.BlockSpec((1, LANES), lambda i: (0, 0))
    return _pcall(body, name="fg_bwd",
                  out_shape=[jax.ShapeDtypeStruct((t, LANES), BF16), jax.ShapeDtypeStruct((1, LANES), F32)],
                  grid=(1,), in_specs=[pl.BlockSpec((N_FOX, t), lambda i: (0, 0)), full, vec], out_specs=[full, vec],
                  scratch_shapes=[pltpu.VMEM((t, LANES), F32)])(dcs_rows, fg, b128)


def _head_spec(t, col0, div=1):
    return pl.BlockSpec((t, HEAD_DIM), lambda h: (0, col0 + h // div))


def _fox_scores(q, k, cq, ck, i, tq, end):
    s = lax.dot_general(q, k, NT_DIMS, preferred_element_type=F32) * ATT_SCALE + cq - ck
    row = lax.broadcasted_iota(jnp.int32, (tq, end), 0) + i * tq
    col = lax.broadcasted_iota(jnp.int32, (tq, end), 1)
    return jnp.where(row >= col, s, -jnp.inf)


def _fox_fwd(proj_a, cum, cumt):
    t = proj_a.shape[0]
    tq = _blk(t, 256)
    nq = t // tq

    def body(q_ref, k_ref, v_ref, cum_ref, cumt_ref, o_ref, lse_ref):
        h = pl.program_id(0)
        cq_all = _pick_lane(cum_ref[...], h)
        ck_all = cumt_ref[pl.ds(h, 1), :]

        @pl.when(h == 0)
        def _():
            lse_ref[...] = jnp.zeros_like(lse_ref)

        for i in range(nq):
            rows, end = slice(i * tq, (i + 1) * tq), (i + 1) * tq
            s = _fox_scores(q_ref[rows, :], k_ref[0:end, :], cq_all[rows, :], ck_all[:, 0:end], i, tq, end)
            m = jnp.max(s, axis=1, keepdims=True)
            p = jnp.exp(s - m)
            l = jnp.sum(p, axis=1, keepdims=True)
            o = jnp.dot(p.astype(BF16), v_ref[0:end, :], preferred_element_type=F32) / l
            o_ref[rows, :] = o.astype(BF16)
            _put_lane(lse_ref, rows, h, m + jnp.log(l))

    nh = FOX_W // HEAD_DIM
    stat = pl.BlockSpec((t, LANES), lambda h: (0, 0))
    return _pcall(body, name="fox_fwd",
                  out_shape=[jax.ShapeDtypeStruct((t, FOX_W), BF16), jax.ShapeDtypeStruct((t, LANES), F32)],
                  grid=(N_FOX,),
                  in_specs=[_head_spec(t, 0), _head_spec(t, nh), _head_spec(t, 2 * nh), stat,
                            pl.BlockSpec((N_FOX, t), lambda h: (0, 0))],
                  out_specs=[_head_spec(t, 0), stat])(proj_a, proj_a, proj_a, cum, cumt)


def _fox_bwd(proj_a, d_attn, cum, cumt, lse):
    t = proj_a.shape[0]
    tq = _blk(t, 512)
    nq = t // tq

    def body(q_ref, k_ref, v_ref, do_ref, cum_ref, cumt_ref, lse_ref,
             dq_ref, dk_ref, dv_ref, dcs_ref, dk_acc, dv_acc, dcs_acc):
        h = pl.program_id(0)
        cq_all = _pick_lane(cum_ref[...], h)
        ck_all = cumt_ref[pl.ds(h, 1), :]
        lse_all = _pick_lane(lse_ref[...], h)
        dk_acc[...] = jnp.zeros_like(dk_acc)
        dv_acc[...] = jnp.zeros_like(dv_acc)
        dcs_acc[...] = jnp.zeros_like(dcs_acc)
        for i in range(nq):
            rows, end = slice(i * tq, (i + 1) * tq), (i + 1) * tq
            q, k, v, do = q_ref[rows, :], k_ref[0:end, :], v_ref[0:end, :], do_ref[rows, :]
            s = _fox_scores(q, k, cq_all[rows, :], ck_all[:, 0:end], i, tq, end)
            p = jnp.exp(s - lse_all[rows, :])
            dp = lax.dot_general(do, v, NT_DIMS, preferred_element_type=F32)
            ds = p * (dp - jnp.sum(p * dp, axis=1, keepdims=True))
            dcs_acc[:, 0:end] += jnp.sum(ds, axis=0, keepdims=True)
            ds = ds.astype(BF16)
            dq_ref[rows, :] = (jnp.dot(ds, k, preferred_element_type=F32) * ATT_SCALE).astype(BF16)
            dk_acc[0:end, :] += lax.dot_general(ds, q, TN_DIMS, preferred_element_type=F32)
            dv_acc[0:end, :] += lax.dot_general(p.astype(BF16), do, TN_DIMS, preferred_element_type=F32)
        dk_ref[...] = (dk_acc[...] * ATT_SCALE).astype(BF16)
        dv_ref[...] = dv_acc[...].astype(BF16)
        dcs_ref[pl.ds(h, 1), :] = dcs_acc[...]

    nh = FOX_W // HEAD_DIM
    stat = pl.BlockSpec((t, LANES), lambda h: (0, 0))
    rows8 = pl.BlockSpec((N_FOX, t), lambda h: (0, 0))
    head = _head_spec(t, 0)
    wide = jax.ShapeDtypeStruct((t, FOX_W), BF16)
    return _pcall(body, name="fox_bwd",
                  out_shape=[wide, wide, wide, jax.ShapeDtypeStruct((N_FOX, t), F32)],
                  grid=(N_FOX,),
                  in_specs=[_head_spec(t, 0), _head_spec(t, nh), _head_spec(t, 2 * nh), head, stat, rows8, stat],
                  out_specs=[head, head, head, rows8],
                  scratch_shapes=[pltpu.VMEM((t, HEAD_DIM), F32), pltpu.VMEM((t, HEAD_DIM), F32),
                                  pltpu.VMEM((1, t), F32)],
                  )(proj_a, proj_a, proj_a, d_attn, cum, cumt, lse)


def _swa_scores(q, k, i, tq, start, end):
    s = lax.dot_general(q, k, NT_DIMS, preferred_element_type=F32) * ATT_SCALE
    row = lax.broadcasted_iota(jnp.int32, (tq, end - start), 0) + i * tq
    col = lax.broadcasted_iota(jnp.int32, (tq, end - start), 1) + start
    diff = row - col
    return jnp.where((diff >= 0) & (diff < WINDOW), s, -jnp.inf)


def _swa_blocks(t):
    tq = _blk(t, 128)
    return tq, [(i, max(0, i * tq - WINDOW), (i + 1) * tq) for i in range(t // tq)]


def _swa_fwd(proj_b, sinks128):
    t = proj_b.shape[0]
    tq, blocks = _swa_blocks(t)

    def body(q_ref, k_ref, v_ref, sink_ref, o_ref, lse_ref):
        h = pl.program_id(0)
        sink = _pick_lane(sink_ref[...], h)

        @pl.when(h == 0)
        def _():
            lse_ref[...] = jnp.zeros_like(lse_ref)

        for i, start, end in blocks:
            rows = slice(i * tq, end)
            s = _swa_scores(q_ref[rows, :], k_ref[start:end, :], i, tq, start, end)
            m = jnp.maximum(jnp.max(s, axis=1, keepdims=True), sink)
            p = jnp.exp(s - m)
            l = jnp.sum(p, axis=1, keepdims=True) + jnp.exp(sink - m)
            o = jnp.dot(p.astype(BF16), v_ref[start:end, :], preferred_element_type=F32) / l
            o_ref[rows, :] = o.astype(BF16)
            _put_lane(lse_ref, rows, h, m + jnp.log(l))

    stat = pl.BlockSpec((t, LANES), lambda h: (0, 0))
    return _pcall(body, name="swa_fwd",
                  out_shape=[jax.ShapeDtypeStruct((t, SWA_W), BF16), jax.ShapeDtypeStruct((t, LANES), F32)],
                  grid=(N_SWA,),
                  in_specs=[_head_spec(t, 0), _head_spec(t, N_SWA, GQA), _head_spec(t, N_SWA + N_KV, GQA),
                            pl.BlockSpec((1, LANES), lambda h: (0, 0))],
                  out_specs=[_head_spec(t, 0), stat])(proj_b, proj_b, proj_b, sinks128)


def _rope_bwd(d, cos, sin):
    return d * cos + pltpu.roll(d * sin, HEAD_DIM // 2, 1)


def _swa_bwd(proj_b, d_attn, lse, sinks128, cos, sin):
    t = proj_b.shape[0]
    tq, blocks = _swa_blocks(t)

    def body(q_ref, k_ref, v_ref, do_ref, lse_ref, sink_ref, cos_ref, sin_ref,
             dq_ref, dk_ref, dv_ref, dsink_ref, dk_acc, dv_acc):
        h = pl.program_id(0)
        sink = _pick_lane(sink_ref[...], h)
        lse_all = _pick_lane(lse_ref[...], h)

        @pl.when(h == 0)
        def _():
            dsink_ref[...] = jnp.zeros_like(dsink_ref)

        @pl.when(h % GQA == 0)
        def _():
            dk_acc[...] = jnp.zeros_like(dk_acc)
            dv_acc[...] = jnp.zeros_like(dv_acc)

        dsink = jnp.zeros((1, 1), F32)
        for i, start, end in blocks:
            rows = slice(i * tq, end)
            q, k, v, do = q_ref[rows, :], k_ref[start:end, :], v_ref[start:end, :], do_ref[rows, :]
            s = _swa_scores(q, k, i, tq, start, end)
            p = jnp.exp(s - lse_all[rows, :])
            dp = lax.dot_general(do, v, NT_DIMS, preferred_element_type=F32)
            delta = jnp.sum(p * dp, axis=1, keepdims=True)
            ds = (p * (dp - delta)).astype(BF16)
            dq = jnp.dot(ds, k, preferred_element_type=F32) * ATT_SCALE
            dq_ref[rows, :] = _rope_bwd(dq, cos_ref[rows, :], sin_ref[rows, :]).astype(BF16)
            dk_acc[start:end, :] += lax.dot_general(ds, q, TN_DIMS, preferred_element_type=F32)
            dv_acc[start:end, :] += lax.dot_general(p.astype(BF16), do, TN_DIMS, preferred_element_type=F32)
            dsink = dsink - jnp.sum(jnp.exp(sink - lse_all[rows, :]) * delta, axis=0, keepdims=True)
        old = dsink_ref[...]
        lane = lax.broadcasted_iota(jnp.int32, old.shape, 1)
        dsink_ref[...] = jnp.where(lane == h, dsink, old)

        @pl.when(h % GQA == GQA - 1)
        def _():
            dk_ref[...] = _rope_bwd(dk_acc[...] * ATT_SCALE, cos_ref[...], sin_ref[...]).astype(BF16)
            dv_ref[...] = dv_acc[...].astype(BF16)

    stat = pl.BlockSpec((t, LANES), lambda h: (0, 0))
    vec = pl.BlockSpec((1, LANES), lambda h: (0, 0))
    head = _head_spec(t, 0)
    kv_out = _head_spec(t, 0, GQA)
    return _pcall(body, name="swa_bwd",
                  out_shape=[jax.ShapeDtypeStruct((t, SWA_W), BF16), jax.ShapeDtypeStruct((t, KV_W), BF16),
                             jax.ShapeDtypeStruct((t, KV_W), BF16), jax.ShapeDtypeStruct((1, LANES), F32)],
                  grid=(N_SWA,),
                  in_specs=[head, _head_spec(t, N_SWA, GQA), _head_spec(t, N_SWA + N_KV, GQA),
                            _head_spec(t, N_FOX), stat, vec, stat, stat],
                  out_specs=[head, kv_out, kv_out, vec],
                  scratch_shapes=[pltpu.VMEM((t, HEAD_DIM), F32), pltpu.VMEM((t, HEAD_DIM), F32)],
                  )(proj_b, proj_b, proj_b, d_attn, lse, sinks128, cos, sin)


def _adamw(w, g, m, v):
    m = ADAM_B1 * m + (1.0 - ADAM_B1) * g
    v = ADAM_B2 * v + (1.0 - ADAM_B2) * (g * g)
    m_hat = m / (1.0 - ADAM_B1 ** ADAM_STEP)
    v_hat = v / (1.0 - ADAM_B2 ** ADAM_STEP)
    delta = -ADAM_LR * (m_hat / (jnp.sqrt(v_hat) + ADAM_EPS) + ADAM_WD * w)
    return delta, m, v


def _adam_pieces(w, m, v, own, land, idx, name):
    rows, cols = w.shape
    tr, tc = (256, cols) if rows % 256 == 0 else (rows, _blk(cols, 512))

    def body(idx_ref, own_ref, l1_ref, l2_ref, l3_ref, w_ref, m_ref, v_ref, g_ref, d_ref, mo_ref, vo_ref):
        g = own_ref[...].astype(F32) + l1_ref[...].astype(F32) + l2_ref[...].astype(F32) + l3_ref[...].astype(F32)
        g_ref[...] = g
        d_ref[...], mo_ref[...], vo_ref[...] = _adamw(w_ref[...], g, m_ref[...], v_ref[...])

    def piece(p):
        return pl.BlockSpec((None, tr, tc), lambda i, j, idx_ref: (idx_ref[p], i, j))

    tile = pl.BlockSpec((tr, tc), lambda i, j, idx_ref: (i, j))
    out = jax.ShapeDtypeStruct((rows, cols), F32)
    grid_spec = pltpu.PrefetchScalarGridSpec(
        num_scalar_prefetch=1, grid=(rows // tr, cols // tc),
        in_specs=[piece(0), piece(1), piece(2), piece(3), tile, tile, tile], out_specs=[tile] * 4)
    return _pcall(body, name=name, out_shape=[out] * 4, grid_spec=grid_spec)(idx, own, land, land, land, w, m, v)


def _adam_mod(c_all, dmod_cols, w, m, v):
    rows, cols = w.shape
    tr = _blk(rows, 256)

    def body(c_ref, dm_ref, w_ref, m_ref, v_ref, g_ref, d_ref, mo_ref, vo_ref):
        cond = _silu(c_ref[...]).astype(BF16)
        g = lax.dot_general(cond, dm_ref[...].astype(BF16), TN_DIMS, preferred_element_type=F32)
        g_ref[...] = g
        d_ref[...], mo_ref[...], vo_ref[...] = _adamw(w_ref[...], g, m_ref[...], v_ref[...])

    tile = pl.BlockSpec((tr, cols), lambda i: (i, 0))
    out = jax.ShapeDtypeStruct((rows, cols), F32)
    return _pcall(body, name="adam_mod", out_shape=[out] * 4, grid=(rows // tr,),
                  in_specs=[pl.BlockSpec((N_DEV, tr), lambda i: (0, i)), pl.BlockSpec((N_DEV, cols), lambda i: (0, 0)),
                            tile, tile, tile],
                  out_specs=[tile] * 4)(c_all, dmod_cols, w, m, v)


def _adam_small(parts, w, m, v):
    nv = w.shape[1]

    def body(p_ref, w_ref, m_ref, v_ref, g_ref, d_ref, mo_ref, vo_ref):
        g = p_ref[0:1, :]
        for k in range(1, N_DEV):
            g = g + p_ref[k:k + 1, :]
        g_ref[...] = g
        d_ref[...], mo_ref[...], vo_ref[...] = _adamw(w_ref[...], g, m_ref[...], v_ref[...])

    vec = pl.BlockSpec((1, nv), lambda i: (0, 0))
    out = jax.ShapeDtypeStruct((1, nv), F32)
    return _pcall(body, name="adam_small", out_shape=[out] * 4, grid=(1,),
                  in_specs=[pl.BlockSpec((N_DEV, nv), lambda i: (0, 0)), vec, vec, vec],
                  out_specs=[vec] * 4)(parts, w, m, v)


def _pad_lanes(v, width=LANES):
    return jnp.pad(v, ((0, 0), (0, width - v.shape[1])))


def kernel(x, c, w_mod, b_mod, g_pre_mix, g_post_mix, w_in, b_forget, swa_sinks, w_out, g_pre_mlp, g_post_mlp, w_up, w_down, loss_target, m_w_mod, m_b_mod, m_g_pre_mix, m_g_post_mix, m_w_in, m_b_forget, m_swa_sinks, m_w_out, m_g_pre_mlp, m_g_post_mlp, m_w_up, m_w_down, v_w_mod, v_b_mod, v_g_pre_mix, v_g_post_mix, v_w_in, v_b_forget, v_swa_sinks, v_w_out, v_g_pre_mlp, v_g_post_mlp, v_w_up, v_w_down):
    ax, ay, ac = _position()
    me = 4 * ax + 2 * ay + ac
    x, target = x[0], loss_target[0]
    t, d = x.shape
    w_mod, w_in, w_out, w_up, w_down = w_mod[0], w_in[0], w_out[0], w_up[0], w_down[0]
    mod_w = w_mod.shape[1]
    in_w = w_in.shape[1]
    in_total = N_DEV * in_w
    shard_ff = w_up.shape[1]
    n_fox3 = 3 * FOX_W
    n_swa3 = SWA_W + 2 * KV_W
    assert in_total == n_fox3 + N_FOX + n_swa3 and d == FOX_W + SWA_W

    c_all = _all_gather([c], "gather_c")[0].reshape(N_DEV, d)
    b_part = lax.dynamic_slice(b_mod, (0, me * mod_w), (1, mod_w))
    mod_parts = _all_gather([_mod_part(c_all, w_mod, b_part)], "gather_mod")[0]
    mod = lax.dynamic_index_in_dim(mod_parts, me, axis=1, keepdims=False).reshape(1, N_DEV * mod_w)

    w_in_b, mod = lax.optimization_barrier((w_in.astype(BF16), mod))
    first = _ag_start([w_in_b], "ag_start_in")
    behind_first = first[4][0, 0]
    rest = _ag_start([(w + behind_first).astype(BF16) for w in (w_out, w_up, w_down)], "ag_start_rest")
    ag_send, ag_recv, ag_shard, ag_land = [a + b for a, b in zip(first[:4], rest[:4])]
    ag_token = rest[4]

    def gathered(i, after, name):
        shard, land = _ag_wait(ag_send[i], ag_recv[i], ag_shard[i], ag_land[i], after, "ag_wait_" + name)
        return lax.dynamic_update_slice(_ag_forward(land, "ag_fwd_" + name), shard[None], (me, 0, 0))

    def gathered_start(i, after, name):
        shard, land = _ag_wait(ag_send[i], ag_recv[i], ag_shard[i], ag_land[i], after, "ag_wait_" + name)
        return shard, _fwd_start(land, "fwd_start_" + name)

    def gathered_finish(started, after, name):
        shard, (send, recv, land, _) = started
        return lax.dynamic_update_slice(_fwd_wait(send, recv, land, after, "fwd_wait_" + name), shard[None], (me, 0, 0))

    sh_a, sc_a, gt_a, sh_m, sc_m, gt_m = [mod[:, i * d:(i + 1) * d] for i in range(6)]

    half = HEAD_DIM // 2
    inv_freq = 1.0 / (ROPE_THETA ** (jnp.arange(half, dtype=F32) * (2.0 / HEAD_DIM)))
    ang = jnp.arange(t).astype(F32)[:, None] * inv_freq[None, :]
    cos = jnp.concatenate([jnp.cos(ang), jnp.cos(ang)], axis=1)
    sin = jnp.concatenate([-jnp.sin(ang), jnp.sin(ang)], axis=1)

    b128 = _pad_lanes(b_forget)
    sinks128 = _pad_lanes(swa_sinks)

    h1 = _pre_attn(x, g_pre_mix + ag_token[0:1, 0:1], sc_a, sh_a)
    w_in_g = gathered(0, h1, "in")
    o_fg, o_sq = n_fox3, n_fox3 + N_FOX

    def cols(lo, hi):
        parts = []
        for j in range(lo // in_w, (hi - 1) // in_w + 1):
            parts.append(w_in_g[j, :, max(lo - j * in_w, 0):min(hi - j * in_w, in_w)])
        return parts

    w_in_r = jnp.concatenate(cols(0, o_fg) + cols(o_sq, in_total) + cols(o_fg, o_sq)
                             + [jnp.zeros((d, FG_PAD - N_FOX), BF16)], axis=1)
    proj_a = _matmul(h1, w_in_r, name="proj_a", n_cols=n_fox3, n_off=0)
    proj_b = _matmul(h1, w_in_r, name="proj_b", n_cols=n_swa3, n_off=n_fox3, tn=512, row_extras=(cos, sin),
                     epilogue=lambda acc, j, cs, sn: (_rope_cols(acc, j, cs, sn, SWA_W + KV_W),))
    out_started = gathered_start(1, proj_a, "out")
    fg = _matmul(h1, w_in_r, name="proj_fg", n_cols=FG_PAD, n_off=n_fox3 + n_swa3, tn=FG_PAD,
                 out_dtypes=(F32,), after=out_started[1][3])[:, 0:LANES]
    cum, cumt = _cum_fwd(fg, b128)
    fox_o, fox_lse = _fox_fwd(proj_a, cum, cumt)
    swa_o, swa_lse = _swa_fwd(proj_b, sinks128)
    up_started = gathered_start(2, swa_o, "up")
    w_out_full = gathered_finish(out_started, swa_o, "out").reshape(d, d)
    attn = jnp.concatenate([fox_o, swa_o], axis=1)
    mix = _matmul(attn, w_out_full, name="out_proj", out_dtypes=(F32,), after=up_started[1][3])
    x2, h2 = _post_mix(x, mix, gt_a, g_post_mix, g_pre_mlp, sc_m, sh_m)
    w_up_g = gathered_finish(up_started, h2, "up")
    u, act = _matmul(h2, w_up_g, name="mlp_up", b_sharded=True, out_dtypes=(BF16, BF16),
                     epilogue=lambda acc, j: (acc, jnp.square(jnp.maximum(acc, 0.0))))
    w_down_full = gathered(3, act, "down").reshape(N_DEV * shard_ff, d)
    y = _matmul(act, w_down_full, name="mlp_down", tk=4096, out_dtypes=(F32,))

    core = jnp.reshape(ac, (1,)).astype(jnp.int32)

    def reduce_start(started, after, name):
        send, recv, src, land, _ = started
        full, from_sibling = _sib_wait(send, recv, src, land, after, "sib_wait_" + name)
        return _rs_start(_chip_sum(full, from_sibling, core, "chip_sum_" + name), "rs_start_" + name)

    def tok(started):
        return started[4][0:1, 0:1]

    idx = jnp.stack([2 * ax + ay, 2 * (1 - ax) + ay, 2 * ax + (1 - ay), 2 * (1 - ax) + (1 - ay)]).astype(jnp.int32)

    def reduce_finish(started, after, w, m, v, name):
        send, recv, src, land, _ = started
        own, landed = _rs_wait(send, recv, src, land, after, "rs_wait_" + name)
        return _adam_pieces(w, m[0], v[0], own, landed, idx, "adam_" + name)

    dy, dout, dgt_m, dg3, loss_vec = _final(y, x2, target, gt_m, g_post_mlp)
    du = _matmul(dy, w_down_full, name="d_act", tb=True, tile_extras=(u,),
                 epilogue=lambda acc, j, uu: (acc * (2.0 * jnp.maximum(uu.astype(F32), 0.0)),))
    dw_down = _matmul(act, dy, name="dw_down", ta=True)
    sb_down = _sib_start(dw_down.reshape(N_DEV, shard_ff, d), "sib_start_down")
    dh2 = _matmul(du, w_up_g, name="d_h2", tb=True, b_sharded=True, b_pair=True, out_dtypes=(F32,),
                  after=sb_down[4])
    rs_down = reduce_start(sb_down, dh2, "down")
    per = shard_ff // _blk(shard_ff, 1024)
    dw_up = _matmul(h2, du, name="dw_up", ta=True, tn=_blk(shard_ff, 1024), out_shape=(N_DEV, d, shard_ff),
                    out_map=lambda tm, tn: pl.BlockSpec((None, tm, tn), lambda i, j, kk: (j // per, i, j % per)),
                    after=rs_down[4])
    sb_up = _sib_start(dw_up, "sib_start_up")
    dmix, dx2, dsh_m, dsc_m, dg2, dgt_a, dg1 = _mid_bwd(
        dh2, dout, x2, mix, g_pre_mlp + tok(sb_up), sc_m, gt_a, g_post_mix)
    d_attn = _matmul(dmix, w_out_full, name="d_attn", tb=True)
    rs_up = reduce_start(sb_up, d_attn, "up")
    dw_out = _matmul(attn, dmix, name="dw_out", ta=True, after=rs_up[4])
    sb_out = _sib_start(dw_out.reshape(N_DEV, d // N_DEV, d), "sib_start_out")
    dqf, dkf, dvf, dcs = _fox_bwd(proj_a, d_attn, cum, cumt, fox_lse)
    dsq, dsk, dsv, dsinks = _swa_bwd(proj_b, d_attn, swa_lse, sinks128 + tok(sb_out), cos, sin)
    rs_out = reduce_start(sb_out, dsq, "out")
    dfg, db_forget = _fg_bwd(dcs, fg, b128 + tok(rs_out))
    dproj = jnp.concatenate([dqf, dkf, dvf, dsq, dsk, dsv, _pad_lanes(dfg, FG_PAD)], axis=1)
    dw_in_r = _matmul(dproj, h1, name="dw_in", ta=True)

    def shard_rows(j):
        lo, hi = j * in_w, (j + 1) * in_w
        parts = []
        for seg_lo, seg_hi, shift in ((0, o_fg, 0), (o_fg, o_sq, n_swa3), (o_sq, in_total, -N_FOX)):
            a, b = max(lo, seg_lo), min(hi, seg_hi)
            if a < b:
                parts.append(dw_in_r[a + shift:b + shift, :])
        return parts[0] if len(parts) == 1 else jnp.concatenate(parts, axis=0)

    sb_in = _sib_start(jnp.stack([shard_rows(j) for j in range(N_DEV)]), "sib_start_in")
    dh1 = _matmul(dproj, w_in_r, name="d_h1", tb=True, tk=2560, out_dtypes=(F32,), after=sb_in[4])
    grad_x, dsh_a, dsc_a, dg0 = _x_bwd(dh1, dx2, x, g_pre_mix, sc_a)

    small = jnp.concatenate([dsh_a, dsc_a, dgt_a, dsh_m, dsc_m, dgt_m, dg0, dg1, dg2, dg3, db_forget, dsinks,
                             loss_vec[:, 0:LANES]], axis=1)
    small_all = _all_gather([small], "gather_small")[0].reshape(N_DEV, small.shape[1])
    rs_in = reduce_start(sb_in, small_all, "in")

    pack = lambda bm, g0_, g1_, g2_, g3_, bf_, sk_: jnp.concatenate(
        [bm, g0_, g1_, g2_, g3_, _pad_lanes(bf_), _pad_lanes(sk_), jnp.zeros((1, LANES), F32)], axis=1)
    p_small = pack(b_mod, g_pre_mix, g_post_mix, g_pre_mlp, g_post_mlp, b_forget, swa_sinks)
    m_small = pack(m_b_mod, m_g_pre_mix, m_g_post_mix, m_g_pre_mlp, m_g_post_mlp, m_b_forget, m_swa_sinks)
    v_small = pack(v_b_mod, v_g_pre_mix, v_g_post_mix, v_g_pre_mlp, v_g_post_mlp, v_b_forget, v_swa_sinks)
    small_out = _adam_small(small_all, p_small + tok(rs_in), m_small, v_small)

    n_mod = 6 * d

    def unpack(vec):
        o = n_mod
        return (vec[:, 0:n_mod], vec[:, o:o + d], vec[:, o + d:o + 2 * d], vec[:, o + 2 * d:o + 3 * d],
                vec[:, o + 3 * d:o + 4 * d], vec[:, o + 4 * d:o + 4 * d + N_FOX],
                vec[:, o + 4 * d + LANES:o + 4 * d + LANES + N_SWA])

    loss = small_out[0][0, n_mod + 4 * d + 2 * LANES]
    g_small, d_small, nm_small, nv_small = [unpack(vec) for vec in small_out]

    dmod_cols = lax.dynamic_slice(small_all, (0, me * mod_w), (N_DEV, mod_w))
    g_w_mod, d_w_mod, nm_w_mod, nv_w_mod = _adam_mod(c_all + tok(rs_in), dmod_cols, w_mod, m_w_mod[0], v_w_mod[0])

    g_w_down, d_w_down, nm_w_down, nv_w_down = reduce_finish(rs_down, d_w_mod, w_down, m_w_down, v_w_down, "w_down")
    g_w_up, d_w_up, nm_w_up, nv_w_up = reduce_finish(rs_up, d_w_down, w_up, m_w_up, v_w_up, "w_up")
    g_w_out, d_w_out, nm_w_out, nv_w_out = reduce_finish(rs_out, d_w_up, w_out, m_w_out, v_w_out, "w_out")
    g_w_in, d_w_in, nm_w_in, nv_w_in = reduce_finish(rs_in, d_w_out, jnp.transpose(w_in), (jnp.transpose(m_w_in[0]),),
                                                     (jnp.transpose(v_w_in[0]),), "w_in")

    def assemble(w_mod_, small_, w_in_, w_out_, w_up_, w_down_):
        b_mod_, g0_, g1_, g2_, g3_, bf_, sk_ = small_
        return [w_mod_[None], b_mod_, g0_, g1_, jnp.transpose(w_in_)[None], bf_, sk_, w_out_[None], g2_, g3_,
                w_up_[None], w_down_[None]]

    outs = [loss, grad_x[None]]
    outs += assemble(g_w_mod, g_small, g_w_in, g_w_out, g_w_up, g_w_down)
    outs += assemble(d_w_mod, d_small, d_w_in, d_w_out, d_w_up, d_w_down)
    outs += assemble(nm_w_mod, nm_small, nm_w_in, nm_w_out, nm_w_up, nm_w_down)
    outs += assemble(nv_w_mod, nv_small, nv_w_in, nv_w_out, nv_w_up, nv_w_down)
    return tuple(outs)
```

```python
import functools

import jax
import jax.numpy as jnp
from jax import lax
from jax.experimental import pallas as pl
from jax.experimental.pallas import tpu as pltpu

F32 = jnp.float32
BF16 = jnp.bfloat16
MESH = pl.DeviceIdType.MESH

N_DEV = 8
N_CHIP = 4
LANES = 128
HEAD_DIM = 128
N_FOX = 8
N_SWA = 8
N_KV = 2
GQA = N_SWA // N_KV
WINDOW = 128
FOX_W = N_FOX * HEAD_DIM
SWA_W = N_SWA * HEAD_DIM
KV_W = N_KV * HEAD_DIM
ROPE_THETA = 10000.0
NORM_EPS = 1e-6
ATT_SCALE = HEAD_DIM ** -0.5
FG_PAD = 512

ADAM_LR = 0.001
ADAM_B1 = 0.9
ADAM_B2 = 0.999
ADAM_EPS = 1e-08
ADAM_WD = 0.01
ADAM_STEP = 10

VMEM_LIMIT = 56 * 1024 * 1024

NT_DIMS = (((1,), (1,)), ((), ()))
TN_DIMS = (((0,), (0,)), ((), ()))
NN_DIMS = (((1,), (0,)), ((), ()))


def _pcall(body, *, name, out_shape, grid=(), in_specs=None, out_specs=None, scratch_shapes=(), grid_spec=None):
    params = pltpu.CompilerParams(vmem_limit_bytes=VMEM_LIMIT)
    if grid_spec is not None:
        return pl.pallas_call(body, name=name, out_shape=out_shape, grid_spec=grid_spec, compiler_params=params)
    return pl.pallas_call(body, name=name, out_shape=out_shape, grid=grid, in_specs=in_specs, out_specs=out_specs,
                          scratch_shapes=scratch_shapes, compiler_params=params)


def _blk(n, pref):
    if n <= pref:
        return n
    b = (pref // LANES) * LANES
    while n % b:
        b -= LANES
    return b


def _position():
    return lax.axis_index("x"), lax.axis_index("y"), lax.axis_index("c")


ANY = pl.BlockSpec(memory_space=pl.ANY)


def _all_gather(arrs, name):
    n = len(arrs)

    def body(*refs):
        ins, outs = refs[:n], refs[n:2 * n]
        send_sems, recv_sems, local_sems = refs[2 * n:]
        x, y, c = _position()
        me, sibling = (x, y, c), (x, y, 1 - c)
        chips = [(1 - x, y), (x, 1 - y), (1 - x, 1 - y)]

        def slot(p):
            return 4 * p[0] + 2 * p[1] + p[2]

        def copy(a, k, block, to, src=None):
            dst = outs[a].at[slot(block)]
            return pltpu.make_async_remote_copy(
                src_ref=dst if src is None else src, dst_ref=dst,
                send_sem=send_sems.at[7 * a + k], recv_sem=recv_sems.at[7 * a + k],
                device_id=to, device_id_type=MESH)

        mine = [pltpu.make_async_copy(ins[a], outs[a].at[slot(me)], local_sems.at[a]) for a in range(n)]
        for cp in mine:
            cp.start()
        first = []
        for a in range(n):
            first.append(copy(a, 0, me, sibling, src=ins[a]))
            first += [copy(a, 1 + j, me, (*chip, c), src=ins[a]) for j, chip in enumerate(chips)]
        for cp in first:
            cp.start()
        passed = []
        for a in range(n):
            for j, chip in enumerate(chips):
                copy(a, 1 + j, (*chip, c), me).wait_recv()
                cp = copy(a, 4 + j, (*chip, c), sibling)
                cp.start()
                passed.append(cp)
        for a in range(n):
            copy(a, 0, sibling, me).wait_recv()
            for j, chip in enumerate(chips):
                copy(a, 4 + j, (*chip, 1 - c), me).wait_recv()
        for cp in first + passed:
            cp.wait_send()
        for cp in mine:
            cp.wait()

    return _pcall(
        body, name=name,
        out_shape=[jax.ShapeDtypeStruct((N_DEV,) + a.shape, a.dtype) for a in arrs],
        in_specs=[ANY] * n, out_specs=[ANY] * n,
        scratch_shapes=[pltpu.SemaphoreType.DMA((7 * n,)), pltpu.SemaphoreType.DMA((7 * n,)),
                        pltpu.SemaphoreType.DMA((n,))],
    )(*arrs)


HBM = pl.BlockSpec(memory_space=pltpu.HBM)
SEM = pl.BlockSpec(memory_space=pltpu.SEMAPHORE)
EFFECT = pltpu.SideEffectType.DATAFLOW_SIDE_EFFECTING


def _hbm(a):
    return pltpu.with_memory_space_constraint(a, pltpu.HBM)


def _gather_peers():
    x, y, c = _position()
    return [(x, y, 1 - c), (1 - x, y, c), (x, 1 - y, c), (1 - x, 1 - y, c)]


def _ag_start(shards, name):
    n = len(shards)
    lands = [_hbm(lax.empty((N_DEV,) + s.shape, s.dtype)) for s in shards]

    def body(*refs):
        srcs, land, send, recv = refs[:n], refs[n:2 * n], refs[2 * n:3 * n], refs[3 * n:4 * n]
        token = refs[6 * n]
        x, y, c = _position()
        for a in range(n):
            for k, to in enumerate(_gather_peers()):
                pltpu.make_async_remote_copy(
                    src_ref=srcs[a], dst_ref=land[a].at[4 * x + 2 * y + c], send_sem=send[a].at[k],
                    recv_sem=recv[a].at[k], device_id=to, device_id_type=MESH).start()
        token[...] = jnp.zeros_like(token)

    sems = [pltpu.SemaphoreType.DMA((4,))] * (2 * n)
    out = pl.pallas_call(
        body, name=name,
        out_shape=sems + [pltpu.HBM(s.shape, s.dtype) for s in shards] + [pltpu.HBM(l.shape, l.dtype) for l in lands]
        + [jax.ShapeDtypeStruct((8, LANES), F32)],
        in_specs=[HBM] * (2 * n), out_specs=[SEM] * (2 * n) + [HBM] * (2 * n) + [pl.BlockSpec(memory_space=pltpu.VMEM)],
        input_output_aliases={**{a: 2 * n + a for a in range(n)}, **{n + a: 3 * n + a for a in range(n)}},
        compiler_params=pltpu.CompilerParams(has_side_effects=EFFECT),
    )(*[_hbm(s) for s in shards], *lands)
    return out[:n], out[n:2 * n], out[2 * n:3 * n], out[3 * n:4 * n], out[4 * n]


def _ag_wait(send, recv, shard_thru, land_thru, after, name):
    def body(v_ref, land_ref, send_sem, recv_sem, after_ref, v_dead, got_ref):
        for k, to in enumerate(_gather_peers()):
            cp = pltpu.make_async_remote_copy(
                src_ref=v_ref, dst_ref=land_ref.at[0], send_sem=send_sem.at[k], recv_sem=recv_sem.at[k],
                device_id=to, device_id_type=MESH)
            cp.wait_send()
            cp.wait_recv()

    return pl.pallas_call(
        body, name=name,
        out_shape=(pltpu.HBM(shard_thru.shape, shard_thru.dtype), pltpu.HBM(land_thru.shape, land_thru.dtype)),
        in_specs=(HBM, HBM, SEM, SEM, ANY), out_specs=(HBM, HBM), input_output_aliases={0: 0, 1: 1},
        compiler_params=pltpu.CompilerParams(has_side_effects=EFFECT),
    )(shard_thru, land_thru, send, recv, after)


def _ag_forward(land, name):
    def body(land_in, land_ref, send_sems, recv_sems):
        x, y, c = _position()
        copies = []
        for j, (px, py) in enumerate([(1 - x, y), (x, 1 - y), (1 - x, 1 - y)]):
            block = land_ref.at[4 * px + 2 * py + c]
            cp = pltpu.make_async_remote_copy(src_ref=block, dst_ref=block, send_sem=send_sems.at[j],
                                              recv_sem=recv_sems.at[j], device_id=(x, y, 1 - c), device_id_type=MESH)
            cp.start()
            copies.append(cp)
        for cp in copies:
            cp.wait()

    return pl.pallas_call(
        body, name=name, out_shape=jax.ShapeDtypeStruct(land.shape, land.dtype),
        in_specs=[ANY], out_specs=ANY, input_output_aliases={0: 0},
        scratch_shapes=[pltpu.SemaphoreType.DMA((3,)), pltpu.SemaphoreType.DMA((3,))],
    )(land)


def _fwd_start(land, name):
    def body(land_ref, send, recv, land_thru, token):
        x, y, c = _position()
        for j, (px, py) in enumerate([(1 - x, y), (x, 1 - y), (1 - x, 1 - y)]):
            block = land_ref.at[4 * px + 2 * py + c]
            pltpu.make_async_remote_copy(src_ref=block, dst_ref=block, send_sem=send.at[j], recv_sem=recv.at[j],
                                         device_id=(x, y, 1 - c), device_id_type=MESH).start()
        token[...] = jnp.zeros_like(token)

    return pl.pallas_call(
        body, name=name,
        out_shape=[pltpu.SemaphoreType.DMA((3,)), pltpu.SemaphoreType.DMA((3,)), pltpu.HBM(land.shape, land.dtype),
                   jax.ShapeDtypeStruct((8, LANES), F32)],
        in_specs=[HBM], out_specs=[SEM, SEM, HBM, pl.BlockSpec(memory_space=pltpu.VMEM)],
        input_output_aliases={0: 2},
        compiler_params=pltpu.CompilerParams(has_side_effects=EFFECT),
    )(land)


def _fwd_wait(send, recv, land_thru, after, name):
    def body(land_ref, send_sem, recv_sem, after_ref, land_out):
        x, y, c = _position()
        for j in range(3):
            cp = pltpu.make_async_remote_copy(
                src_ref=land_ref.at[0], dst_ref=land_ref.at[0], send_sem=send_sem.at[j], recv_sem=recv_sem.at[j],
                device_id=(x, y, 1 - c), device_id_type=MESH)
            cp.wait_send()
            cp.wait_recv()

    return pl.pallas_call(
        body, name=name, out_shape=pltpu.HBM(land_thru.shape, land_thru.dtype),
        in_specs=(HBM, SEM, SEM, ANY), out_specs=HBM, input_output_aliases={0: 0},
        compiler_params=pltpu.CompilerParams(has_side_effects=EFFECT),
    )(land_thru, send, recv, after)


def _rs_peers():
    x, y, c = _position()
    return [(1 - x, y, c), (x, 1 - y, c), (1 - x, 1 - y, c)]


def _rs_start(chip_sums, name):
    land = _hbm(lax.empty(chip_sums.shape, chip_sums.dtype))

    def body(src, land_ref, send, recv, src_thru, land_thru, token):
        x, y, c = _position()
        for j, (px, py, pc) in enumerate(_rs_peers()):
            pltpu.make_async_remote_copy(
                src_ref=src.at[2 * px + py], dst_ref=land_ref.at[2 * x + y], send_sem=send.at[j], recv_sem=recv.at[j],
                device_id=(px, py, pc), device_id_type=MESH).start()
        token[...] = jnp.zeros_like(token)

    return pl.pallas_call(
        body, name=name,
        out_shape=[pltpu.SemaphoreType.DMA((3,)), pltpu.SemaphoreType.DMA((3,)),
                   pltpu.HBM(chip_sums.shape, chip_sums.dtype), pltpu.HBM(land.shape, land.dtype),
                   jax.ShapeDtypeStruct((8, LANES), F32)],
        in_specs=[HBM, HBM], out_specs=[SEM, SEM, HBM, HBM, pl.BlockSpec(memory_space=pltpu.VMEM)],
        input_output_aliases={0: 2, 1: 3},
        compiler_params=pltpu.CompilerParams(has_side_effects=EFFECT),
    )(_hbm(chip_sums), land)


def _rs_wait(send, recv, src_thru, land_thru, after, name):
    def body(src, land_ref, send_sem, recv_sem, after_ref, src_out, land_out):
        for j, to in enumerate(_rs_peers()):
            cp = pltpu.make_async_remote_copy(
                src_ref=src.at[0], dst_ref=land_ref.at[0], send_sem=send_sem.at[j], recv_sem=recv_sem.at[j],
                device_id=to, device_id_type=MESH)
            cp.wait_send()
            cp.wait_recv()

    return pl.pallas_call(
        body, name=name,
        out_shape=(pltpu.HBM(src_thru.shape, src_thru.dtype), pltpu.HBM(land_thru.shape, land_thru.dtype)),
        in_specs=(HBM, HBM, SEM, SEM, ANY), out_specs=(HBM, HBM), input_output_aliases={0: 0, 1: 1},
        compiler_params=pltpu.CompilerParams(has_side_effects=EFFECT),
    )(src_thru, land_thru, send, recv, after)


def _sib_start(full, name):
    land = _hbm(lax.empty((N_CHIP,) + full.shape[1:], full.dtype))

    def body(src, land_ref, send, recv, src_thru, land_thru, token):
        x, y, c = _position()
        for k in range(N_CHIP):
            pltpu.make_async_remote_copy(
                src_ref=src.at[2 * k + (1 - c)], dst_ref=land_ref.at[k], send_sem=send.at[k], recv_sem=recv.at[k],
                device_id=(x, y, 1 - c), device_id_type=MESH).start()
        token[...] = jnp.zeros_like(token)

    return pl.pallas_call(
        body, name=name,
        out_shape=[pltpu.SemaphoreType.DMA((N_CHIP,)), pltpu.SemaphoreType.DMA((N_CHIP,)),
                   pltpu.HBM(full.shape, full.dtype), pltpu.HBM(land.shape, land.dtype),
                   jax.ShapeDtypeStruct((8, LANES), F32)],
        in_specs=[HBM, HBM], out_specs=[SEM, SEM, HBM, HBM, pl.BlockSpec(memory_space=pltpu.VMEM)],
        input_output_aliases={0: 2, 1: 3},
        compiler_params=pltpu.CompilerParams(has_side_effects=EFFECT),
    )(_hbm(full), land)


def _sib_wait(send, recv, src_thru, land_thru, after, name):
    def body(src, land_ref, send_sem, recv_sem, after_ref, src_out, land_out):
        x, y, c = _position()
        for k in range(N_CHIP):
            cp = pltpu.make_async_remote_copy(
                src_ref=src.at[0], dst_ref=land_ref.at[0], send_sem=send_sem.at[k], recv_sem=recv_sem.at[k],
                device_id=(x, y, 1 - c), device_id_type=MESH)
            cp.wait_send()
            cp.wait_recv()

    return pl.pallas_call(
        body, name=name,
        out_shape=(pltpu.HBM(src_thru.shape, src_thru.dtype), pltpu.HBM(land_thru.shape, land_thru.dtype)),
        in_specs=(HBM, HBM, SEM, SEM, ANY), out_specs=(HBM, HBM), input_output_aliases={0: 0, 1: 1},
        compiler_params=pltpu.CompilerParams(has_side_effects=EFFECT),
    )(src_thru, land_thru, send, recv, after)


def _rs_sibling(arrs, name):
    n = len(arrs)

    def body(*refs):
        ins, outs = refs[:n], refs[n:2 * n]
        send_sems, recv_sems = refs[2 * n:]
        x, y, c = _position()
        copies = []
        for a in range(n):
            for k in range(N_CHIP):
                cp = pltpu.make_async_remote_copy(
                    src_ref=ins[a].at[2 * k + (1 - c)], dst_ref=outs[a].at[k],
                    send_sem=send_sems.at[N_CHIP * a + k], recv_sem=recv_sems.at[N_CHIP * a + k],
                    device_id=(x, y, 1 - c), device_id_type=MESH)
                cp.start()
                copies.append(cp)
        for cp in copies:
            cp.wait()

    return _pcall(
        body, name=name,
        out_shape=[jax.ShapeDtypeStruct((N_CHIP,) + a.shape[1:], a.dtype) for a in arrs],
        in_specs=[ANY] * n, out_specs=[ANY] * n,
        scratch_shapes=[pltpu.SemaphoreType.DMA((N_CHIP * n,)), pltpu.SemaphoreType.DMA((N_CHIP * n,))],
    )(*arrs)


def _chip_sum(full, recv, core, name):
    _, rows, cols = full.shape
    tr = _blk(rows, 1024) if rows % LANES == 0 else rows

    def body(core_ref, a_ref, b_ref, o_ref):
        o_ref[...] = (a_ref[...].astype(F32) + b_ref[...].astype(F32)).astype(o_ref.dtype)

    grid_spec = pltpu.PrefetchScalarGridSpec(
        num_scalar_prefetch=1, grid=(N_CHIP, rows // tr),
        in_specs=[pl.BlockSpec((None, tr, cols), lambda k, i, core_ref: (2 * k + core_ref[0], i, 0)),
                  pl.BlockSpec((None, tr, cols), lambda k, i, core_ref: (k, i, 0))],
        out_specs=pl.BlockSpec((None, tr, cols), lambda k, i, core_ref: (k, i, 0)))
    return _pcall(body, name=name, out_shape=jax.ShapeDtypeStruct((N_CHIP, rows, cols), full.dtype),
                  grid_spec=grid_spec)(core, full, recv)


def _matmul(a, b, *, name, ta=False, tb=False, tm=1024, tn=1024, tk=2048, out_dtypes=(BF16,), epilogue=None,
            row_extras=(), tile_extras=(), out_shape=None, out_map=None, b_sharded=False, n_cols=None, n_off=0,
            after=None, b_pair=False):
    m, k = (a.shape[1], a.shape[0]) if ta else a.shape
    if b_sharded:
        shard_c = b.shape[2]
        n, kb = (b.shape[1], N_DEV * shard_c) if tb else (N_DEV * shard_c, b.shape[1])
        tn, tk = (tn, min(tk, shard_c)) if tb else (min(tn, shard_c), tk)
        if b_pair:
            assert tb and tk == shard_c
            tk = 2 * shard_c
    else:
        n, kb = b.shape if tb else (b.shape[1], b.shape[0])
    assert kb == k, (name, kb, k)
    if n_cols is not None:
        n = n_cols
    tm, tn, tk = _blk(m, tm), _blk(n, tn), _blk(k, tk)
    assert n_off % tn == 0
    nk = k // tk
    dims = (((0 if ta else 1,), (1 if tb else 0,)), ((), ()))
    behind = () if after is None else (after,)
    n_row, n_tile, n_out = len(row_extras), len(tile_extras), len(out_dtypes)
    n_b = 2 if b_pair else 1
    first_out = 1 + n_b + n_row + n_tile + len(behind)

    def body(*refs):
        a_ref, b_ref = refs[:2]
        extras = refs[1 + n_b:1 + n_b + n_row + n_tile]
        outs = refs[first_out:first_out + n_out]
        acc_ref = refs[-1]
        jj, kk = pl.program_id(1), pl.program_id(2)
        if b_pair:
            half = tk // 2
            part = (lax.dot_general(a_ref[:, 0:half].astype(BF16), b_ref[...].astype(BF16), dims,
                                    preferred_element_type=F32)
                    + lax.dot_general(a_ref[:, half:tk].astype(BF16), refs[2][...].astype(BF16), dims,
                                      preferred_element_type=F32))
        else:
            part = lax.dot_general(a_ref[...].astype(BF16), b_ref[...].astype(BF16), dims,
                                   preferred_element_type=F32)

        def finish(acc):
            res = (acc,) if epilogue is None else epilogue(acc, jj, *[e[...] for e in extras])
            for o_ref, r in zip(outs, res):
                o_ref[...] = r.astype(o_ref.dtype)

        if nk == 1:
            finish(part)
        else:
            @pl.when(kk == 0)
            def _():
                acc_ref[...] = part

            @pl.when(kk > 0)
            def _():
                acc_ref[...] += part

            @pl.when(kk == nk - 1)
            def _():
                finish(acc_ref[...])

    a_spec = pl.BlockSpec((tk, tm), lambda i, j, kk: (kk, i)) if ta else pl.BlockSpec((tm, tk), lambda i, j, kk: (i, kk))
    if b_pair:
        b_spec = pl.BlockSpec((None, tn, shard_c), lambda i, j, kk: (2 * kk, j, 0))
    elif b_sharded and tb:
        per = shard_c // tk
        b_spec = pl.BlockSpec((None, tn, tk), lambda i, j, kk: (kk // per, j, kk % per))
    elif b_sharded:
        per = shard_c // tn
        b_spec = pl.BlockSpec((None, tk, tn), lambda i, j, kk: (j // per, kk, j % per))
    elif tb:
        b_spec = pl.BlockSpec((tn, tk), lambda i, j, kk: (j + n_off // tn, kk))
    else:
        b_spec = pl.BlockSpec((tk, tn), lambda i, j, kk: (kk, j + n_off // tn))
    in_specs = [a_spec, b_spec]
    if b_pair:
        in_specs.append(pl.BlockSpec((None, tn, shard_c), lambda i, j, kk: (2 * kk + 1, j, 0)))
    in_specs += [pl.BlockSpec((tm, LANES), lambda i, j, kk: (i, 0)) for _ in row_extras]
    in_specs += [pl.BlockSpec((tm, tn), lambda i, j, kk: (i, j)) for _ in tile_extras]
    in_specs += [ANY for _ in behind]
    if out_map is None:
        out_specs = [pl.BlockSpec((tm, tn), lambda i, j, kk: (i, j)) for _ in out_dtypes]
        shapes = [jax.ShapeDtypeStruct((m, n), dt) for dt in out_dtypes]
    else:
        out_specs = [out_map(tm, tn)]
        shapes = [jax.ShapeDtypeStruct(out_shape, out_dtypes[0])]
    acc_shape = (tm, tn) if nk > 1 else (8, LANES)
    res = _pcall(body, name=name, out_shape=shapes, grid=(m // tm, n // tn, nk), in_specs=in_specs,
                 out_specs=out_specs, scratch_shapes=[pltpu.VMEM(acc_shape, F32)])(
                     a, *([b, b] if b_pair else [b]), *row_extras, *tile_extras, *behind)
    return res[0] if n_out == 1 else res


def _rope_cols(acc, j, cos, sin, n_rope):
    width = acc.shape[1]
    parts = []
    for g in range(width // HEAD_DIM):
        xg = acc[:, g * HEAD_DIM:(g + 1) * HEAD_DIM]
        roped = xg * cos + pltpu.roll(xg, HEAD_DIM // 2, 1) * sin
        parts.append(jnp.where(j * width + g * HEAD_DIM < n_rope, roped, xg))
    return jnp.concatenate(parts, axis=1) if len(parts) > 1 else parts[0]


def _silu(v):
    return v / (1.0 + jnp.exp(-v))


def _mod_part(c_all, w_mod, b_part):
    d, w = w_mod.shape
    tk = _blk(d, 512)

    def body(c_ref, w_ref, b_ref, o_ref):
        kk = pl.program_id(0)
        cond = _silu(c_ref[...]).astype(BF16)
        part = jnp.dot(cond, w_ref[...].astype(BF16), preferred_element_type=F32)

        @pl.when(kk == 0)
        def _():
            o_ref[...] = part + b_ref[...]

        @pl.when(kk > 0)
        def _():
            o_ref[...] += part

    return _pcall(body, name="mod_part", out_shape=jax.ShapeDtypeStruct((N_DEV, w), F32), grid=(d // tk,),
                  in_specs=[pl.BlockSpec((N_DEV, tk), lambda kk: (0, kk)), pl.BlockSpec((tk, w), lambda kk: (kk, 0)),
                            pl.BlockSpec((1, w), lambda kk: (0, 0))],
                  out_specs=pl.BlockSpec((N_DEV, w), lambda kk: (0, 0)))(c_all, w_mod, b_part)


def _row_call(body, name, t, d, tiled_in, vec_in, tiled_out_dtypes, n_vec_out, tr=256):
    tr = _blk(t, tr)
    tile = pl.BlockSpec((tr, d), lambda i: (i, 0))
    vec = pl.BlockSpec((1, d), lambda i: (0, 0))
    out_shape = [jax.ShapeDtypeStruct((t, d), dt) for dt in tiled_out_dtypes]
    out_shape += [jax.ShapeDtypeStruct((1, d), F32)] * n_vec_out
    return _pcall(body, name=name, out_shape=out_shape, grid=(t // tr,),
                  in_specs=[tile] * len(tiled_in) + [vec] * len(vec_in),
                  out_specs=[tile] * len(tiled_out_dtypes) + [vec] * n_vec_out)(*tiled_in, *vec_in)


def _accumulate(ref, val):
    @pl.when(pl.program_id(0) == 0)
    def _():
        ref[...] = val

    @pl.when(pl.program_id(0) > 0)
    def _():
        ref[...] += val


def _rsum(v):
    return jnp.sum(v, axis=0, keepdims=True)


def _rms(v):
    return lax.rsqrt(jnp.mean(v * v, axis=-1, keepdims=True) + NORM_EPS)


def _rms_bwd(vhat, r, dvhat):
    return r * (dvhat - vhat * jnp.mean(dvhat * vhat, axis=-1, keepdims=True))


def _pre_attn(x, g0, sc_a, sh_a):
    def body(x_ref, g_ref, sc_ref, sh_ref, h_ref):
        xv = x_ref[...]
        h_ref[...] = (xv * _rms(xv) * g_ref[...] * (1.0 + sc_ref[...]) + sh_ref[...]).astype(BF16)

    t, d = x.shape
    return _row_call(body, "pre_attn", t, d, [x], [g0, sc_a, sh_a], [BF16], 0)[0]


def _post_mix(x, mix, gt_a, g1, g2, sc_m, sh_m):
    def body(x_ref, mix_ref, gt_ref, g1_ref, g2_ref, sc_ref, sh_ref, x2_ref, h2_ref):
        mv = mix_ref[...]
        x2 = x_ref[...] + gt_ref[...] * (mv * _rms(mv) * g1_ref[...])
        x2_ref[...] = x2
        h2_ref[...] = (x2 * _rms(x2) * g2_ref[...] * (1.0 + sc_ref[...]) + sh_ref[...]).astype(BF16)

    t, d = x.shape
    return _row_call(body, "post_mix", t, d, [x, mix], [gt_a, g1, g2, sc_m, sh_m], [F32, BF16], 0)


def _final(y, x2, target, gt_m, g3):
    t, d = y.shape

    def body(y_ref, x2_ref, tg_ref, gt_ref, g3_ref, dy_ref, dout_ref, dgt_ref, dg3_ref, loss_ref):
        yv = y_ref[...]
        r = _rms(yv)
        yhat = yv * r
        n3 = yhat * g3_ref[...]
        err = x2_ref[...] + gt_ref[...] * n3 - tg_ref[...]
        _accumulate(loss_ref, jnp.zeros((1, d), F32) + 0.5 * jnp.sum(err * err) / d)
        dout = err * (1.0 / d)
        dout_ref[...] = dout
        _accumulate(dgt_ref, _rsum(dout * n3))
        dn3 = dout * gt_ref[...]
        _accumulate(dg3_ref, _rsum(dn3 * yhat))
        dy_ref[...] = _rms_bwd(yhat, r, dn3 * g3_ref[...]).astype(BF16)

    return _row_call(body, "final", t, d, [y, x2, target], [gt_m, g3], [BF16, F32], 3)


def _mid_bwd(dh2, dout, x2, mix, g2, sc_m, gt_a, g1):
    t, d = x2.shape

    def body(dh2_ref, dout_ref, x2_ref, mix_ref, g2_ref, sc_ref, gt_ref, g1_ref,
             dmix_ref, dx2_ref, dsh_ref, dsc_ref, dg2_ref, dgt_ref, dg1_ref):
        dh2v = dh2_ref[...]
        x2v = x2_ref[...]
        r2 = _rms(x2v)
        x2hat = x2v * r2
        _accumulate(dsh_ref, _rsum(dh2v))
        _accumulate(dsc_ref, _rsum(dh2v * (x2hat * g2_ref[...])))
        dn2 = dh2v * (1.0 + sc_ref[...])
        _accumulate(dg2_ref, _rsum(dn2 * x2hat))
        dx2 = dout_ref[...] + _rms_bwd(x2hat, r2, dn2 * g2_ref[...])
        dx2_ref[...] = dx2
        mv = mix_ref[...]
        r1 = _rms(mv)
        mhat = mv * r1
        _accumulate(dgt_ref, _rsum(dx2 * (mhat * g1_ref[...])))
        dn1 = dx2 * gt_ref[...]
        _accumulate(dg1_ref, _rsum(dn1 * mhat))
        dmix_ref[...] = _rms_bwd(mhat, r1, dn1 * g1_ref[...]).astype(BF16)

    return _row_call(body, "mid_bwd", t, d, [dh2, dout, x2, mix], [g2, sc_m, gt_a, g1], [BF16, F32], 5)


def _x_bwd(dh1, dx2, x, g0, sc_a):
    t, d = x.shape

    def body(dh1_ref, dx2_ref, x_ref, g0_ref, sc_ref, dx_ref, dsh_ref, dsc_ref, dg0_ref):
        dh1v = dh1_ref[...]
        xv = x_ref[...]
        r0 = _rms(xv)
        xhat = xv * r0
        _accumulate(dsh_ref, _rsum(dh1v))
        _accumulate(dsc_ref, _rsum(dh1v * (xhat * g0_ref[...])))
        dn0 = dh1v * (1.0 + sc_ref[...])
        _accumulate(dg0_ref, _rsum(dn0 * xhat))
        dx_ref[...] = dx2_ref[...] + _rms_bwd(xhat, r0, dn0 * g0_ref[...])

    return _row_call(body, "x_bwd", t, d, [dh1, dx2, x], [g0, sc_a], [F32], 3)


def _pick_lane(block, h):
    lane = lax.broadcasted_iota(jnp.int32, block.shape, 1)
    return jnp.sum(jnp.where(lane == h, block, 0.0), axis=1, keepdims=True)


def _put_lane(ref, rows, h, col):
    old = ref[rows, :]
    lane = lax.broadcasted_iota(jnp.int32, old.shape, 1)
    ref[rows, :] = jnp.where(lane == h, col, old)


def _tri(n, lower):
    r = lax.broadcasted_iota(jnp.int32, (n, n), 0)
    c = lax.broadcasted_iota(jnp.int32, (n, n), 1)
    return jnp.where((c <= r) if lower else (c >= r), 1.0, 0.0).astype(F32)


def _cum_fwd(fg, b128):
    t = fg.shape[0]
    nb = t // LANES

    def body(fg_ref, b_ref, cum_ref, cumt_ref):
        tri = _tri(LANES, True)
        carry = jnp.zeros((1, LANES), F32)
        for i in range(nb):
            z = fg_ref[i * LANES:(i + 1) * LANES, :] + b_ref[...]
            lf = jnp.minimum(z, 0.0) - jnp.log(1.0 + jnp.exp(-jnp.abs(z)))
            blk = jnp.dot(tri, lf, precision=lax.Precision.HIGHEST, preferred_element_type=F32) + carry
            cum_ref[i * LANES:(i + 1) * LANES, :] = blk
            carry = blk[LANES - 1:LANES, :]
        cumt_ref[...] = cum_ref[...].T[0:N_FOX, :]

    return _pcall(body, name="cum_fwd",
                  out_shape=[jax.ShapeDtypeStruct((t, LANES), F32), jax.ShapeDtypeStruct((N_FOX, t), F32)],
                  grid=(1,),
                  in_specs=[pl.BlockSpec((t, LANES), lambda i: (0, 0)), pl.BlockSpec((1, LANES), lambda i: (0, 0))],
                  out_specs=[pl.BlockSpec((t, LANES), lambda i: (0, 0)), pl.BlockSpec((N_FOX, t), lambda i: (0, 0))],
                  )(fg, b128)


def _fg_bwd(dcs_rows, fg, b128):
    t = fg.shape[0]
    nb = t // LANES

    def body(dcs_ref, fg_ref, b_ref, dfg_ref, db_ref, dcum_ref):
        dcum_ref[...] = -jnp.concatenate([dcs_ref[...], jnp.zeros((LANES - N_FOX, t), F32)], axis=0).T
        tri = _tri(LANES, False)
        carry = jnp.zeros((1, LANES), F32)
        db = jnp.zeros((1, LANES), F32)
        for i in reversed(range(nb)):
            rows = slice(i * LANES, (i + 1) * LANES)
            dlf = jnp.dot(tri, dcum_ref[rows, :], precision=lax.Precision.HIGHEST, preferred_element_type=F32) + carry
            carry = dlf[0:1, :]
            z = fg_ref[rows, :] + b_ref[...]
            dfg = dlf / (1.0 + jnp.exp(z))
            dfg_ref[rows, :] = dfg.astype(BF16)
            db = db + _rsum(dfg)
        db_ref[...] = db

    full = pl.BlockSpec((t, LANES), lambda i: (0, 0))
    vec = pl.BlockSpec((1, LANES), lambda i: (0, 0))
    return _pcall(body, name="fg_bwd",
                  out_shape=[jax.ShapeDtypeStruct((t, LANES), BF16), jax.ShapeDtypeStruct((1, LANES), F32)],
                  grid=(1,), in_specs=[pl.BlockSpec((N_FOX, t), lambda i: (0, 0)), full, vec], out_specs=[full, vec],
                  scratch_shapes=[pltpu.VMEM((t, LANES), F32)])(dcs_rows, fg, b128)


def _head_spec(t, col0, div=1):
    return pl.BlockSpec((t, HEAD_DIM), lambda h: (0, col0 + h // div))


def _fox_scores(q, k, cq, ck, i, tq, end):
    s = lax.dot_general(q, k, NT_DIMS, preferred_element_type=F32) * ATT_SCALE + cq - ck
    row = lax.broadcasted_iota(jnp.int32, (tq, end), 0) + i * tq
    col = lax.broadcasted_iota(jnp.int32, (tq, end), 1)
    return jnp.where(row >= col, s, -jnp.inf)


def _fox_fwd(proj_a, cum, cumt):
    t = proj_a.shape[0]
    tq = _blk(t, 512)
    nq = t // tq

    def body(q_ref, k_ref, v_ref, cum_ref, cumt_ref, o_ref, lse_ref):
        h = pl.program_id(0)
        cq_all = _pick_lane(cum_ref[...], h)
        ck_all = cumt_ref[pl.ds(h, 1), :]

        @pl.when(h == 0)
        def _():
            lse_ref[...] = jnp.zeros_like(lse_ref)

        for i in range(nq):
            rows, end = slice(i * tq, (i + 1) * tq), (i + 1) * tq
            s = _fox_scores(q_ref[rows, :], k_ref[0:end, :], cq_all[rows, :], ck_all[:, 0:end], i, tq, end)
            m = jnp.max(s, axis=1, keepdims=True)
            p = jnp.exp(s - m)
            l = jnp.sum(p, axis=1, keepdims=True)
            o = jnp.dot(p.astype(BF16), v_ref[0:end, :], preferred_element_type=F32) / l
            o_ref[rows, :] = o.astype(BF16)
            _put_lane(lse_ref, rows, h, m + jnp.log(l))

    nh = FOX_W // HEAD_DIM
    stat = pl.BlockSpec((t, LANES), lambda h: (0, 0))
    return _pcall(body, name="fox_fwd",
                  out_shape=[jax.ShapeDtypeStruct((t, FOX_W), BF16), jax.ShapeDtypeStruct((t, LANES), F32)],
                  grid=(N_FOX,),
                  in_specs=[_head_spec(t, 0), _head_spec(t, nh), _head_spec(t, 2 * nh), stat,
                            pl.BlockSpec((N_FOX, t), lambda h: (0, 0))],
                  out_specs=[_head_spec(t, 0), stat])(proj_a, proj_a, proj_a, cum, cumt)


def _fox_bwd(proj_a, d_attn, cum, cumt, lse):
    t = proj_a.shape[0]
    tq = _blk(t, 512)
    nq = t // tq

    def body(q_ref, k_ref, v_ref, do_ref, cum_ref, cumt_ref, lse_ref,
             dq_ref, dk_ref, dv_ref, dcs_ref, dk_acc, dv_acc, dcs_acc):
        h = pl.program_id(0)
        cq_all = _pick_lane(cum_ref[...], h)
        ck_all = cumt_ref[pl.ds(h, 1), :]
        lse_all = _pick_lane(lse_ref[...], h)
        dk_acc[...] = jnp.zeros_like(dk_acc)
        dv_acc[...] = jnp.zeros_like(dv_acc)
        dcs_acc[...] = jnp.zeros_like(dcs_acc)
        for i in range(nq):
            rows, end = slice(i * tq, (i + 1) * tq), (i + 1) * tq
            q, k, v, do = q_ref[rows, :], k_ref[0:end, :], v_ref[0:end, :], do_ref[rows, :]
            s = _fox_scores(q, k, cq_all[rows, :], ck_all[:, 0:end], i, tq, end)
            p = jnp.exp(s - lse_all[rows, :])
            dp = lax.dot_general(do, v, NT_DIMS, preferred_element_type=F32)
            ds = p * (dp - jnp.sum(p * dp, axis=1, keepdims=True))
            dcs_acc[:, 0:end] += jnp.sum(ds, axis=0, keepdims=True)
            ds = ds.astype(BF16)
            dq_ref[rows, :] = (jnp.dot(ds, k, preferred_element_type=F32) * ATT_SCALE).astype(BF16)
            dk_acc[0:end, :] += lax.dot_general(ds, q, TN_DIMS, preferred_element_type=F32)
            dv_acc[0:end, :] += lax.dot_general(p.astype(BF16), do, TN_DIMS, preferred_element_type=F32)
        dk_ref[...] = (dk_acc[...] * ATT_SCALE).astype(BF16)
        dv_ref[...] = dv_acc[...].astype(BF16)
        dcs_ref[pl.ds(h, 1), :] = dcs_acc[...]

    nh = FOX_W // HEAD_DIM
    stat = pl.BlockSpec((t, LANES), lambda h: (0, 0))
    rows8 = pl.BlockSpec((N_FOX, t), lambda h: (0, 0))
    head = _head_spec(t, 0)
    wide = jax.ShapeDtypeStruct((t, FOX_W), BF16)
    return _pcall(body, name="fox_bwd",
                  out_shape=[wide, wide, wide, jax.ShapeDtypeStruct((N_FOX, t), F32)],
                  grid=(N_FOX,),
                  in_specs=[_head_spec(t, 0), _head_spec(t, nh), _head_spec(t, 2 * nh), head, stat, rows8, stat],
                  out_specs=[head, head, head, rows8],
                  scratch_shapes=[pltpu.VMEM((t, HEAD_DIM), F32), pltpu.VMEM((t, HEAD_DIM), F32),
                                  pltpu.VMEM((1, t), F32)],
                  )(proj_a, proj_a, proj_a, d_attn, cum, cumt, lse)


def _swa_scores(q, k, i, tq, start, end):
    s = lax.dot_general(q, k, NT_DIMS, preferred_element_type=F32) * ATT_SCALE
    row = lax.broadcasted_iota(jnp.int32, (tq, end - start), 0) + i * tq
    col = lax.broadcasted_iota(jnp.int32, (tq, end - start), 1) + start
    diff = row - col
    return jnp.where((diff >= 0) & (diff < WINDOW), s, -jnp.inf)


def _swa_blocks(t):
    tq = _blk(t, 256)
    return tq, [(i, max(0, i * tq - WINDOW), (i + 1) * tq) for i in range(t // tq)]


def _swa_fwd(proj_b, sinks128):
    t = proj_b.shape[0]
    tq, blocks = _swa_blocks(t)

    def body(q_ref, k_ref, v_ref, sink_ref, o_ref, lse_ref):
        h = pl.program_id(0)
        sink = _pick_lane(sink_ref[...], h)

        @pl.when(h == 0)
        def _():
            lse_ref[...] = jnp.zeros_like(lse_ref)

        for i, start, end in blocks:
            rows = slice(i * tq, end)
            s = _swa_scores(q_ref[rows, :], k_ref[start:end, :], i, tq, start, end)
            m = jnp.maximum(jnp.max(s, axis=1, keepdims=True), sink)
            p = jnp.exp(s - m)
            l = jnp.sum(p, axis=1, keepdims=True) + jnp.exp(sink - m)
            o = jnp.dot(p.astype(BF16), v_ref[start:end, :], preferred_element_type=F32) / l
            o_ref[rows, :] = o.astype(BF16)
            _put_lane(lse_ref, rows, h, m + jnp.log(l))

    stat = pl.BlockSpec((t, LANES), lambda h: (0, 0))
    return _pcall(body, name="swa_fwd",
                  out_shape=[jax.ShapeDtypeStruct((t, SWA_W), BF16), jax.ShapeDtypeStruct((t, LANES), F32)],
                  grid=(N_SWA,),
                  in_specs=[_head_spec(t, 0), _head_spec(t, N_SWA, GQA), _head_spec(t, N_SWA + N_KV, GQA),
                            pl.BlockSpec((1, LANES), lambda h: (0, 0))],
                  out_specs=[_head_spec(t, 0), stat])(proj_b, proj_b, proj_b, sinks128)


def _rope_bwd(d, cos, sin):
    return d * cos + pltpu.roll(d * sin, HEAD_DIM // 2, 1)


def _swa_bwd(proj_b, d_attn, lse, sinks128, cos, sin):
    t = proj_b.shape[0]
    tq, blocks = _swa_blocks(t)

    def body(q_ref, k_ref, v_ref, do_ref, lse_ref, sink_ref, cos_ref, sin_ref,
             dq_ref, dk_ref, dv_ref, dsink_ref, dk_acc, dv_acc):
        h = pl.program_id(0)
        sink = _pick_lane(sink_ref[...], h)
        lse_all = _pick_lane(lse_ref[...], h)

        @pl.when(h == 0)
        def _():
            dsink_ref[...] = jnp.zeros_like(dsink_ref)

        @pl.when(h % GQA == 0)
        def _():
            dk_acc[...] = jnp.zeros_like(dk_acc)
            dv_acc[...] = jnp.zeros_like(dv_acc)

        dsink = jnp.zeros((1, 1), F32)
        for i, start, end in blocks:
            rows = slice(i * tq, end)
            q, k, v, do = q_ref[rows, :], k_ref[start:end, :], v_ref[start:end, :], do_ref[rows, :]
            s = _swa_scores(q, k, i, tq, start, end)
            p = jnp.exp(s - lse_all[rows, :])
            dp = lax.dot_general(do, v, NT_DIMS, preferred_element_type=F32)
            delta = jnp.sum(p * dp, axis=1, keepdims=True)
            ds = (p * (dp - delta)).astype(BF16)
            dq = jnp.dot(ds, k, preferred_element_type=F32) * ATT_SCALE
            dq_ref[rows, :] = _rope_bwd(dq, cos_ref[rows, :], sin_ref[rows, :]).astype(BF16)
            dk_acc[start:end, :] += lax.dot_general(ds, q, TN_DIMS, preferred_element_type=F32)
            dv_acc[start:end, :] += lax.dot_general(p.astype(BF16), do, TN_DIMS, preferred_element_type=F32)
            dsink = dsink - jnp.sum(jnp.exp(sink - lse_all[rows, :]) * delta, axis=0, keepdims=True)
        old = dsink_ref[...]
        lane = lax.broadcasted_iota(jnp.int32, old.shape, 1)
        dsink_ref[...] = jnp.where(lane == h, dsink, old)

        @pl.when(h % GQA == GQA - 1)
        def _():
            dk_ref[...] = _rope_bwd(dk_acc[...] * ATT_SCALE, cos_ref[...], sin_ref[...]).astype(BF16)
            dv_ref[...] = dv_acc[...].astype(BF16)

    stat = pl.BlockSpec((t, LANES), lambda h: (0, 0))
    vec = pl.BlockSpec((1, LANES), lambda h: (0, 0))
    head = _head_spec(t, 0)
    kv_out = _head_spec(t, 0, GQA)
    return _pcall(body, name="swa_bwd",
                  out_shape=[jax.ShapeDtypeStruct((t, SWA_W), BF16), jax.ShapeDtypeStruct((t, KV_W), BF16),
                             jax.ShapeDtypeStruct((t, KV_W), BF16), jax.ShapeDtypeStruct((1, LANES), F32)],
                  grid=(N_SWA,),
                  in_specs=[head, _head_spec(t, N_SWA, GQA), _head_spec(t, N_SWA + N_KV, GQA),
                            _head_spec(t, N_FOX), stat, vec, stat, stat],
                  out_specs=[head, kv_out, kv_out, vec],
                  scratch_shapes=[pltpu.VMEM((t, HEAD_DIM), F32), pltpu.VMEM((t, HEAD_DIM), F32)],
                  )(proj_b, proj_b, proj_b, d_attn, lse, sinks128, cos, sin)


def _adamw(w, g, m, v):
    m = ADAM_B1 * m + (1.0 - ADAM_B1) * g
    v = ADAM_B2 * v + (1.0 - ADAM_B2) * (g * g)
    m_hat = m / (1.0 - ADAM_B1 ** ADAM_STEP)
    v_hat = v / (1.0 - ADAM_B2 ** ADAM_STEP)
    delta = -ADAM_LR * (m_hat / (jnp.sqrt(v_hat) + ADAM_EPS) + ADAM_WD * w)
    return delta, m, v


def _adam_pieces(w, m, v, own, land, idx, name):
    rows, cols = w.shape
    tr, tc = (256, cols) if rows % 256 == 0 else (rows, _blk(cols, 512))

    def body(idx_ref, own_ref, l1_ref, l2_ref, l3_ref, w_ref, m_ref, v_ref, g_ref, d_ref, mo_ref, vo_ref):
        g = own_ref[...].astype(F32) + l1_ref[...].astype(F32) + l2_ref[...].astype(F32) + l3_ref[...].astype(F32)
        g_ref[...] = g
        d_ref[...], mo_ref[...], vo_ref[...] = _adamw(w_ref[...], g, m_ref[...], v_ref[...])

    def piece(p):
        return pl.BlockSpec((None, tr, tc), lambda i, j, idx_ref: (idx_ref[p], i, j))

    tile = pl.BlockSpec((tr, tc), lambda i, j, idx_ref: (i, j))
    out = jax.ShapeDtypeStruct((rows, cols), F32)
    grid_spec = pltpu.PrefetchScalarGridSpec(
        num_scalar_prefetch=1, grid=(rows // tr, cols // tc),
        in_specs=[piece(0), piece(1), piece(2), piece(3), tile, tile, tile], out_specs=[tile] * 4)
    return _pcall(body, name=name, out_shape=[out] * 4, grid_spec=grid_spec)(idx, own, land, land, land, w, m, v)


def _adam_mod(c_all, dmod_cols, w, m, v):
    rows, cols = w.shape
    tr = _blk(rows, 256)

    def body(c_ref, dm_ref, w_ref, m_ref, v_ref, g_ref, d_ref, mo_ref, vo_ref):
        cond = _silu(c_ref[...]).astype(BF16)
        g = lax.dot_general(cond, dm_ref[...].astype(BF16), TN_DIMS, preferred_element_type=F32)
        g_ref[...] = g
        d_ref[...], mo_ref[...], vo_ref[...] = _adamw(w_ref[...], g, m_ref[...], v_ref[...])

    tile = pl.BlockSpec((tr, cols), lambda i: (i, 0))
    out = jax.ShapeDtypeStruct((rows, cols), F32)
    return _pcall(body, name="adam_mod", out_shape=[out] * 4, grid=(rows // tr,),
                  in_specs=[pl.BlockSpec((N_DEV, tr), lambda i: (0, i)), pl.BlockSpec((N_DEV, cols), lambda i: (0, 0)),
                            tile, tile, tile],
                  out_specs=[tile] * 4)(c_all, dmod_cols, w, m, v)


def _adam_small(parts, w, m, v):
    nv = w.shape[1]

    def body(p_ref, w_ref, m_ref, v_ref, g_ref, d_ref, mo_ref, vo_ref):
        g = p_ref[0:1, :]
        for k in range(1, N_DEV):
            g = g + p_ref[k:k + 1, :]
        g_ref[...] = g
        d_ref[...], mo_ref[...], vo_ref[...] = _adamw(w_ref[...], g, m_ref[...], v_ref[...])

    vec = pl.BlockSpec((1, nv), lambda i: (0, 0))
    out = jax.ShapeDtypeStruct((1, nv), F32)
    return _pcall(body, name="adam_small", out_shape=[out] * 4, grid=(1,),
                  in_specs=[pl.BlockSpec((N_DEV, nv), lambda i: (0, 0)), vec, vec, vec],
                  out_specs=[vec] * 4)(parts, w, m, v)


def _pad_lanes(v, width=LANES):
    return jnp.pad(v, ((0, 0), (0, width - v.shape[1])))


def kernel(x, c, w_mod, b_mod, g_pre_mix, g_post_mix, w_in, b_forget, swa_sinks, w_out, g_pre_mlp, g_post_mlp, w_up, w_down, loss_target, m_w_mod, m_b_mod, m_g_pre_mix, m_g_post_mix, m_w_in, m_b_forget, m_swa_sinks, m_w_out, m_g_pre_mlp, m_g_post_mlp, m_w_up, m_w_down, v_w_mod, v_b_mod, v_g_pre_mix, v_g_post_mix, v_w_in, v_b_forget, v_swa_sinks, v_w_out, v_g_pre_mlp, v_g_post_mlp, v_w_up, v_w_down):
    ax, ay, ac = _position()
    me = 4 * ax + 2 * ay + ac
    x, target = x[0], loss_target[0]
    t, d = x.shape
    w_mod, w_in, w_out, w_up, w_down = w_mod[0], w_in[0], w_out[0], w_up[0], w_down[0]
    mod_w = w_mod.shape[1]
    in_w = w_in.shape[1]
    in_total = N_DEV * in_w
    shard_ff = w_up.shape[1]
    n_fox3 = 3 * FOX_W
    n_swa3 = SWA_W + 2 * KV_W
    assert in_total == n_fox3 + N_FOX + n_swa3 and d == FOX_W + SWA_W

    c_all = _all_gather([c], "gather_c")[0].reshape(N_DEV, d)
    b_part = lax.dynamic_slice(b_mod, (0, me * mod_w), (1, mod_w))
    mod_parts = _all_gather([_mod_part(c_all, w_mod, b_part)], "gather_mod")[0]
    mod = lax.dynamic_index_in_dim(mod_parts, me, axis=1, keepdims=False).reshape(1, N_DEV * mod_w)

    w_in_b, mod = lax.optimization_barrier((w_in.astype(BF16), mod))
    first = _ag_start([w_in_b], "ag_start_in")
    behind_first = first[4][0, 0]
    rest = _ag_start([(w + behind_first).astype(BF16) for w in (w_out, w_up, w_down)], "ag_start_rest")
    ag_send, ag_recv, ag_shard, ag_land = [a + b for a, b in zip(first[:4], rest[:4])]
    ag_token = rest[4]

    def gathered(i, after, name):
        shard, land = _ag_wait(ag_send[i], ag_recv[i], ag_shard[i], ag_land[i], after, "ag_wait_" + name)
        return lax.dynamic_update_slice(_ag_forward(land, "ag_fwd_" + name), shard[None], (me, 0, 0))

    def gathered_start(i, after, name):
        shard, land = _ag_wait(ag_send[i], ag_recv[i], ag_shard[i], ag_land[i], after, "ag_wait_" + name)
        return shard, _fwd_start(land, "fwd_start_" + name)

    def gathered_finish(started, after, name):
        shard, (send, recv, land, _) = started
        return lax.dynamic_update_slice(_fwd_wait(send, recv, land, after, "fwd_wait_" + name), shard[None], (me, 0, 0))

    sh_a, sc_a, gt_a, sh_m, sc_m, gt_m = [mod[:, i * d:(i + 1) * d] for i in range(6)]

    half = HEAD_DIM // 2
    inv_freq = 1.0 / (ROPE_THETA ** (jnp.arange(half, dtype=F32) * (2.0 / HEAD_DIM)))
    ang = jnp.arange(t).astype(F32)[:, None] * inv_freq[None, :]
    cos = jnp.concatenate([jnp.cos(ang), jnp.cos(ang)], axis=1)
    sin = jnp.concatenate([-jnp.sin(ang), jnp.sin(ang)], axis=1)

    b128 = _pad_lanes(b_forget)
    sinks128 = _pad_lanes(swa_sinks)

    h1 = _pre_attn(x, g_pre_mix + ag_token[0:1, 0:1], sc_a, sh_a)
    h1, adam_in = lax.optimization_barrier(
        (h1, (jnp.transpose(w_in), jnp.transpose(m_w_in[0]), jnp.transpose(v_w_in[0]))))
    w_in_g = gathered(0, h1, "in")
    o_fg, o_sq = n_fox3, n_fox3 + N_FOX

    def cols(lo, hi):
        parts = []
        for j in range(lo // in_w, (hi - 1) // in_w + 1):
            parts.append(w_in_g[j, :, max(lo - j * in_w, 0):min(hi - j * in_w, in_w)])
        return parts

    w_in_r = jnp.concatenate(cols(0, o_fg) + cols(o_sq, in_total) + cols(o_fg, o_sq)
                             + [jnp.zeros((d, FG_PAD - N_FOX), BF16)], axis=1)
    proj_a = _matmul(h1, w_in_r, name="proj_a", n_cols=n_fox3, n_off=0)
    proj_b = _matmul(h1, w_in_r, name="proj_b", n_cols=n_swa3, n_off=n_fox3, tn=512, row_extras=(cos, sin),
                     epilogue=lambda acc, j, cs, sn: (_rope_cols(acc, j, cs, sn, SWA_W + KV_W),))
    out_started = gathered_start(1, proj_a, "out")
    fg = _matmul(h1, w_in_r, name="proj_fg", n_cols=FG_PAD, n_off=n_fox3 + n_swa3, tn=FG_PAD,
                 out_dtypes=(F32,), after=out_started[1][3])[:, 0:LANES]
    cum, cumt = _cum_fwd(fg, b128)
    fox_o, fox_lse = _fox_fwd(proj_a, cum, cumt)
    up_started = gathered_start(2, fox_o, "up")
    swa_o, swa_lse = _swa_fwd(proj_b, sinks128 + up_started[1][3][0:1, 0:1])
    w_out_full = gathered_finish(out_started, swa_o, "out").reshape(d, d)
    attn = jnp.concatenate([fox_o, swa_o], axis=1)
    mix = _matmul(attn, w_out_full, name="out_proj", out_dtypes=(F32,))
    x2, h2 = _post_mix(x, mix, gt_a, g_post_mix, g_pre_mlp, sc_m, sh_m)
    w_up_g = gathered_finish(up_started, h2, "up")
    u, act = _matmul(h2, w_up_g, name="mlp_up", b_sharded=True, out_dtypes=(BF16, BF16),
                     epilogue=lambda acc, j: (acc, jnp.square(jnp.maximum(acc, 0.0))))
    w_down_full = gathered(3, act, "down").reshape(N_DEV * shard_ff, d)
    y = _matmul(act, w_down_full, name="mlp_down", tk=4096, out_dtypes=(F32,))

    core = jnp.reshape(ac, (1,)).astype(jnp.int32)

    def reduce_start(started, after, name):
        send, recv, src, land, _ = started
        full, from_sibling = _sib_wait(send, recv, src, land, after, "sib_wait_" + name)
        return _rs_start(_chip_sum(full, from_sibling, core, "chip_sum_" + name), "rs_start_" + name)

    def tok(started):
        return started[4][0:1, 0:1]

    idx = jnp.stack([2 * ax + ay, 2 * (1 - ax) + ay, 2 * ax + (1 - ay), 2 * (1 - ax) + (1 - ay)]).astype(jnp.int32)

    def reduce_finish(started, after, w, m, v, name):
        send, recv, src, land, _ = started
        own, landed = _rs_wait(send, recv, src, land, after, "rs_wait_" + name)
        return _adam_pieces(w, m[0], v[0], own, landed, idx, "adam_" + name)

    dy, dout, dgt_m, dg3, loss_vec = _final(y, x2, target, gt_m, g_post_mlp)
    du = _matmul(dy, w_down_full, name="d_act", tb=True, tile_extras=(u,),
                 epilogue=lambda acc, j, uu: (acc * (2.0 * jnp.maximum(uu.astype(F32), 0.0)),))
    dw_down = _matmul(act, dy, name="dw_down", ta=True)
    sb_down = _sib_start(dw_down.reshape(N_DEV, shard_ff, d), "sib_start_down")
    dh2 = _matmul(du, w_up_g, name="d_h2", tb=True, b_sharded=True, b_pair=True, out_dtypes=(F32,),
                  after=sb_down[4])
    rs_down = reduce_start(sb_down, dh2, "down")
    per = shard_ff // _blk(shard_ff, 1024)
    dw_up = _matmul(h2, du, name="dw_up", ta=True, tn=_blk(shard_ff, 1024), out_shape=(N_DEV, d, shard_ff),
                    out_map=lambda tm, tn: pl.BlockSpec((None, tm, tn), lambda i, j, kk: (j // per, i, j % per)),
                    after=rs_down[4])
    sb_up = _sib_start(dw_up, "sib_start_up")
    dmix, dx2, dsh_m, dsc_m, dg2, dgt_a, dg1 = _mid_bwd(
        dh2, dout, x2, mix, g_pre_mlp + tok(sb_up), sc_m, gt_a, g_post_mix)
    d_attn = _matmul(dmix, w_out_full, name="d_attn", tb=True)
    rs_up = reduce_start(sb_up, d_attn, "up")
    dw_out = _matmul(attn, dmix, name="dw_out", ta=True, after=rs_up[4])
    sb_out = _sib_start(dw_out.reshape(N_DEV, d // N_DEV, d), "sib_start_out")
    dqf, dkf, dvf, dcs = _fox_bwd(proj_a, d_attn, cum, cumt, fox_lse)
    dsq, dsk, dsv, dsinks = _swa_bwd(proj_b, d_attn, swa_lse, sinks128 + tok(sb_out), cos, sin)
    rs_out = reduce_start(sb_out, dsq, "out")
    dfg, db_forget = _fg_bwd(dcs, fg, b128 + tok(rs_out))
    dproj = jnp.concatenate([dqf, dkf, dvf, dsq, dsk, dsv, _pad_lanes(dfg, FG_PAD)], axis=1)
    dw_in_r = _matmul(dproj, h1, name="dw_in", ta=True)

    def shard_rows(j):
        lo, hi = j * in_w, (j + 1) * in_w
        parts = []
        for seg_lo, seg_hi, shift in ((0, o_fg, 0), (o_fg, o_sq, n_swa3), (o_sq, in_total, -N_FOX)):
            a, b = max(lo, seg_lo), min(hi, seg_hi)
            if a < b:
                parts.append(dw_in_r[a + shift:b + shift, :])
        return parts[0] if len(parts) == 1 else jnp.concatenate(parts, axis=0)

    sb_in = _sib_start(jnp.stack([shard_rows(j) for j in range(N_DEV)]), "sib_start_in")
    dh1 = _matmul(dproj, w_in_r, name="d_h1", tb=True, tk=2560, out_dtypes=(F32,), after=sb_in[4])
    grad_x, dsh_a, dsc_a, dg0 = _x_bwd(dh1, dx2, x, g_pre_mix, sc_a)

    small = jnp.concatenate([dsh_a, dsc_a, dgt_a, dsh_m, dsc_m, dgt_m, dg0, dg1, dg2, dg3, db_forget, dsinks,
                             loss_vec[:, 0:LANES]], axis=1)
    small_all = _all_gather([small], "gather_small")[0].reshape(N_DEV, small.shape[1])
    rs_in = reduce_start(sb_in, small_all, "in")

    pack = lambda bm, g0_, g1_, g2_, g3_, bf_, sk_: jnp.concatenate(
        [bm, g0_, g1_, g2_, g3_, _pad_lanes(bf_), _pad_lanes(sk_), jnp.zeros((1, LANES), F32)], axis=1)
    p_small = pack(b_mod, g_pre_mix, g_post_mix, g_pre_mlp, g_post_mlp, b_forget, swa_sinks)
    m_small = pack(m_b_mod, m_g_pre_mix, m_g_post_mix, m_g_pre_mlp, m_g_post_mlp, m_b_forget, m_swa_sinks)
    v_small = pack(v_b_mod, v_g_pre_mix, v_g_post_mix, v_g_pre_mlp, v_g_post_mlp, v_b_forget, v_swa_sinks)
    small_out = _adam_small(small_all, p_small + tok(rs_in), m_small, v_small)

    n_mod = 6 * d

    def unpack(vec):
        o = n_mod
        return (vec[:, 0:n_mod], vec[:, o:o + d], vec[:, o + d:o + 2 * d], vec[:, o + 2 * d:o + 3 * d],
                vec[:, o + 3 * d:o + 4 * d], vec[:, o + 4 * d:o + 4 * d + N_FOX],
                vec[:, o + 4 * d + LANES:o + 4 * d + LANES + N_SWA])

    loss = small_out[0][0, n_mod + 4 * d + 2 * LANES]
    g_small, d_small, nm_small, nv_small = [unpack(vec) for vec in small_out]

    dmod_cols = lax.dynamic_slice(small_all, (0, me * mod_w), (N_DEV, mod_w))
    g_w_mod, d_w_mod, nm_w_mod, nv_w_mod = _adam_mod(c_all + tok(rs_in), dmod_cols, w_mod, m_w_mod[0], v_w_mod[0])

    g_w_down, d_w_down, nm_w_down, nv_w_down = reduce_finish(rs_down, d_w_mod, w_down, m_w_down, v_w_down, "w_down")
    g_w_up, d_w_up, nm_w_up, nv_w_up = reduce_finish(rs_up, d_w_down, w_up, m_w_up, v_w_up, "w_up")
    g_w_out, d_w_out, nm_w_out, nv_w_out = reduce_finish(rs_out, d_w_up, w_out, m_w_out, v_w_out, "w_out")
    g_w_in, d_w_in, nm_w_in, nv_w_in = reduce_finish(rs_in, d_w_out, adam_in[0], (adam_in[1],), (adam_in[2],), "w_in")

    def assemble(w_mod_, small_, w_in_, w_out_, w_up_, w_down_):
        b_mod_, g0_, g1_, g2_, g3_, bf_, sk_ = small_
        return [w_mod_[None], b_mod_, g0_, g1_, jnp.transpose(w_in_)[None], bf_, sk_, w_out_[None], g2_, g3_,
                w_up_[None], w_down_[None]]

    outs = [loss, grad_x[None]]
    outs += assemble(g_w_mod, g_small, g_w_in, g_w_out, g_w_up, g_w_down)
    outs += assemble(d_w_mod, d_small, d_w_in, d_w_out, d_w_up, d_w_down)
    outs += assemble(nm_w_mod, nm_small, nm_w_in, nm_w_out, nm_w_up, nm_w_down)
    outs += assemble(nv_w_mod, nv_small, nv_w_in, nv_w_out, nv_w_up, nv_w_down)
    return tuple(outs)
```

```python
import functools

import jax
import jax.numpy as jnp
from jax import lax
from jax.experimental import pallas as pl
from jax.experimental.pallas import tpu as pltpu

F32 = jnp.float32
BF16 = jnp.bfloat16
MESH = pl.DeviceIdType.MESH

N_DEV = 8
N_CHIP = 4
LANES = 128
HEAD_DIM = 128
N_FOX = 8
N_SWA = 8
N_KV = 2
GQA = N_SWA // N_KV
WINDOW = 128
FOX_W = N_FOX * HEAD_DIM
SWA_W = N_SWA * HEAD_DIM
KV_W = N_KV * HEAD_DIM
ROPE_THETA = 10000.0
NORM_EPS = 1e-6
ATT_SCALE = HEAD_DIM ** -0.5
FG_PAD = 512

ADAM_LR = 0.001
ADAM_B1 = 0.9
ADAM_B2 = 0.999
ADAM_EPS = 1e-08
ADAM_WD = 0.01
ADAM_STEP = 10

VMEM_LIMIT = 56 * 1024 * 1024

NT_DIMS = (((1,), (1,)), ((), ()))
TN_DIMS = (((0,), (0,)), ((), ()))
NN_DIMS = (((1,), (0,)), ((), ()))


def _pcall(body, *, name, out_shape, grid=(), in_specs=None, out_specs=None, scratch_shapes=(), grid_spec=None):
    params = pltpu.CompilerParams(vmem_limit_bytes=VMEM_LIMIT)
    if grid_spec is not None:
        return pl.pallas_call(body, name=name, out_shape=out_shape, grid_spec=grid_spec, compiler_params=params)
    return pl.pallas_call(body, name=name, out_shape=out_shape, grid=grid, in_specs=in_specs, out_specs=out_specs,
                          scratch_shapes=scratch_shapes, compiler_params=params)


def _blk(n, pref):
    if n <= pref:
        return n
    b = (pref // LANES) * LANES
    while n % b:
        b -= LANES
    return b


def _position():
    return lax.axis_index("x"), lax.axis_index("y"), lax.axis_index("c")


ANY = pl.BlockSpec(memory_space=pl.ANY)


def _all_gather(arrs, name):
    n = len(arrs)

    def body(*refs):
        ins, outs = refs[:n], refs[n:2 * n]
        send_sems, recv_sems, local_sems = refs[2 * n:]
        x, y, c = _position()
        me, sibling = (x, y, c), (x, y, 1 - c)
        chips = [(1 - x, y), (x, 1 - y), (1 - x, 1 - y)]

        def slot(p):
            return 4 * p[0] + 2 * p[1] + p[2]

        def copy(a, k, block, to, src=None):
            dst = outs[a].at[slot(block)]
            return pltpu.make_async_remote_copy(
                src_ref=dst if src is None else src, dst_ref=dst,
                send_sem=send_sems.at[7 * a + k], recv_sem=recv_sems.at[7 * a + k],
                device_id=to, device_id_type=MESH)

        mine = [pltpu.make_async_copy(ins[a], outs[a].at[slot(me)], local_sems.at[a]) for a in range(n)]
        for cp in mine:
            cp.start()
        first = []
        for a in range(n):
            first.append(copy(a, 0, me, sibling, src=ins[a]))
            first += [copy(a, 1 + j, me, (*chip, c), src=ins[a]) for j, chip in enumerate(chips)]
        for cp in first:
            cp.start()
        passed = []
        for a in range(n):
            for j, chip in enumerate(chips):
                copy(a, 1 + j, (*chip, c), me).wait_recv()
                cp = copy(a, 4 + j, (*chip, c), sibling)
                cp.start()
                passed.append(cp)
        for a in range(n):
            copy(a, 0, sibling, me).wait_recv()
            for j, chip in enumerate(chips):
                copy(a, 4 + j, (*chip, 1 - c), me).wait_recv()
        for cp in first + passed:
            cp.wait_send()
        for cp in mine:
            cp.wait()

    return _pcall(
        body, name=name,
        out_shape=[jax.ShapeDtypeStruct((N_DEV,) + a.shape, a.dtype) for a in arrs],
        in_specs=[ANY] * n, out_specs=[ANY] * n,
        scratch_shapes=[pltpu.SemaphoreType.DMA((7 * n,)), pltpu.SemaphoreType.DMA((7 * n,)),
                        pltpu.SemaphoreType.DMA((n,))],
    )(*arrs)


HBM = pl.BlockSpec(memory_space=pltpu.HBM)
SEM = pl.BlockSpec(memory_space=pltpu.SEMAPHORE)
EFFECT = pltpu.SideEffectType.DATAFLOW_SIDE_EFFECTING


def _hbm(a):
    return pltpu.with_memory_space_constraint(a, pltpu.HBM)


def _gather_peers():
    x, y, c = _position()
    return [(x, y, 1 - c), (1 - x, y, c), (x, 1 - y, c), (1 - x, 1 - y, c)]


def _ag_start(shards, name):
    n = len(shards)
    lands = [_hbm(lax.empty((N_DEV,) + s.shape, s.dtype)) for s in shards]

    def body(*refs):
        srcs, land, send, recv = refs[:n], refs[n:2 * n], refs[2 * n:3 * n], refs[3 * n:4 * n]
        token = refs[6 * n]
        x, y, c = _position()
        for a in range(n):
            for k, to in enumerate(_gather_peers()):
                pltpu.make_async_remote_copy(
                    src_ref=srcs[a], dst_ref=land[a].at[4 * x + 2 * y + c], send_sem=send[a].at[k],
                    recv_sem=recv[a].at[k], device_id=to, device_id_type=MESH).start()
        token[...] = jnp.zeros_like(token)

    sems = [pltpu.SemaphoreType.DMA((4,))] * (2 * n)
    out = pl.pallas_call(
        body, name=name,
        out_shape=sems + [pltpu.HBM(s.shape, s.dtype) for s in shards] + [pltpu.HBM(l.shape, l.dtype) for l in lands]
        + [jax.ShapeDtypeStruct((8, LANES), F32)],
        in_specs=[HBM] * (2 * n), out_specs=[SEM] * (2 * n) + [HBM] * (2 * n) + [pl.BlockSpec(memory_space=pltpu.VMEM)],
        input_output_aliases={**{a: 2 * n + a for a in range(n)}, **{n + a: 3 * n + a for a in range(n)}},
        compiler_params=pltpu.CompilerParams(has_side_effects=EFFECT),
    )(*[_hbm(s) for s in shards], *lands)
    return out[:n], out[n:2 * n], out[2 * n:3 * n], out[3 * n:4 * n], out[4 * n]


def _ag_wait(send, recv, shard_thru, land_thru, after, name):
    def body(v_ref, land_ref, send_sem, recv_sem, after_ref, v_dead, got_ref):
        for k, to in enumerate(_gather_peers()):
            cp = pltpu.make_async_remote_copy(
                src_ref=v_ref, dst_ref=land_ref.at[0], send_sem=send_sem.at[k], recv_sem=recv_sem.at[k],
                device_id=to, device_id_type=MESH)
            cp.wait_send()
            cp.wait_recv()

    return pl.pallas_call(
        body, name=name,
        out_shape=(pltpu.HBM(shard_thru.shape, shard_thru.dtype), pltpu.HBM(land_thru.shape, land_thru.dtype)),
        in_specs=(HBM, HBM, SEM, SEM, ANY), out_specs=(HBM, HBM), input_output_aliases={0: 0, 1: 1},
        compiler_params=pltpu.CompilerParams(has_side_effects=EFFECT),
    )(shard_thru, land_thru, send, recv, after)


def _ag_forward(land, name):
    def body(land_in, land_ref, send_sems, recv_sems):
        x, y, c = _position()
        copies = []
        for j, (px, py) in enumerate([(1 - x, y), (x, 1 - y), (1 - x, 1 - y)]):
            block = land_ref.at[4 * px + 2 * py + c]
            cp = pltpu.make_async_remote_copy(src_ref=block, dst_ref=block, send_sem=send_sems.at[j],
                                              recv_sem=recv_sems.at[j], device_id=(x, y, 1 - c), device_id_type=MESH)
            cp.start()
            copies.append(cp)
        for cp in copies:
            cp.wait()

    return pl.pallas_call(
        body, name=name, out_shape=jax.ShapeDtypeStruct(land.shape, land.dtype),
        in_specs=[ANY], out_specs=ANY, input_output_aliases={0: 0},
        scratch_shapes=[pltpu.SemaphoreType.DMA((3,)), pltpu.SemaphoreType.DMA((3,))],
    )(land)


def _fwd_start(land, name):
    def body(land_ref, send, recv, land_thru, token):
        x, y, c = _position()
        for j, (px, py) in enumerate([(1 - x, y), (x, 1 - y), (1 - x, 1 - y)]):
            block = land_ref.at[4 * px + 2 * py + c]
            pltpu.make_async_remote_copy(src_ref=block, dst_ref=block, send_sem=send.at[j], recv_sem=recv.at[j],
                                         device_id=(x, y, 1 - c), device_id_type=MESH).start()
        token[...] = jnp.zeros_like(token)

    return pl.pallas_call(
        body, name=name,
        out_shape=[pltpu.SemaphoreType.DMA((3,)), pltpu.SemaphoreType.DMA((3,)), pltpu.HBM(land.shape, land.dtype),
                   jax.ShapeDtypeStruct((8, LANES), F32)],
        in_specs=[HBM], out_specs=[SEM, SEM, HBM, pl.BlockSpec(memory_space=pltpu.VMEM)],
        input_output_aliases={0: 2},
        compiler_params=pltpu.CompilerParams(has_side_effects=EFFECT),
    )(land)


def _fwd_wait(send, recv, land_thru, after, name):
    def body(land_ref, send_sem, recv_sem, after_ref, land_out):
        x, y, c = _position()
        for j in range(3):
            cp = pltpu.make_async_remote_copy(
                src_ref=land_ref.at[0], dst_ref=land_ref.at[0], send_sem=send_sem.at[j], recv_sem=recv_sem.at[j],
                device_id=(x, y, 1 - c), device_id_type=MESH)
            cp.wait_send()
            cp.wait_recv()

    return pl.pallas_call(
        body, name=name, out_shape=pltpu.HBM(land_thru.shape, land_thru.dtype),
        in_specs=(HBM, SEM, SEM, ANY), out_specs=HBM, input_output_aliases={0: 0},
        compiler_params=pltpu.CompilerParams(has_side_effects=EFFECT),
    )(land_thru, send, recv, after)


def _rs_peers():
    x, y, c = _position()
    return [(1 - x, y, c), (x, 1 - y, c), (1 - x, 1 - y, c)]


def _rs_start(chip_sums, name):
    land = _hbm(lax.empty(chip_sums.shape, chip_sums.dtype))

    def body(src, land_ref, send, recv, src_thru, land_thru, token):
        x, y, c = _position()
        for j, (px, py, pc) in enumerate(_rs_peers()):
            pltpu.make_async_remote_copy(
                src_ref=src.at[2 * px + py], dst_ref=land_ref.at[2 * x + y], send_sem=send.at[j], recv_sem=recv.at[j],
                device_id=(px, py, pc), device_id_type=MESH).start()
        token[...] = jnp.zeros_like(token)

    return pl.pallas_call(
        body, name=name,
        out_shape=[pltpu.SemaphoreType.DMA((3,)), pltpu.SemaphoreType.DMA((3,)),
                   pltpu.HBM(chip_sums.shape, chip_sums.dtype), pltpu.HBM(land.shape, land.dtype),
                   jax.ShapeDtypeStruct((8, LANES), F32)],
        in_specs=[HBM, HBM], out_specs=[SEM, SEM, HBM, HBM, pl.BlockSpec(memory_space=pltpu.VMEM)],
        input_output_aliases={0: 2, 1: 3},
        compiler_params=pltpu.CompilerParams(has_side_effects=EFFECT),
    )(_hbm(chip_sums), land)


def _rs_wait(send, recv, src_thru, land_thru, after, name):
    def body(src, land_ref, send_sem, recv_sem, after_ref, src_out, land_out):
        for j, to in enumerate(_rs_peers()):
            cp = pltpu.make_async_remote_copy(
                src_ref=src.at[0], dst_ref=land_ref.at[0], send_sem=send_sem.at[j], recv_sem=recv_sem.at[j],
                device_id=to, device_id_type=MESH)
            cp.wait_send()
            cp.wait_recv()

    return pl.pallas_call(
        body, name=name,
        out_shape=(pltpu.HBM(src_thru.shape, src_thru.dtype), pltpu.HBM(land_thru.shape, land_thru.dtype)),
        in_specs=(HBM, HBM, SEM, SEM, ANY), out_specs=(HBM, HBM), input_output_aliases={0: 0, 1: 1},
        compiler_params=pltpu.CompilerParams(has_side_effects=EFFECT),
    )(src_thru, land_thru, send, recv, after)


def _sib_start(full, name):
    land = _hbm(lax.empty((N_CHIP,) + full.shape[1:], full.dtype))

    def body(src, land_ref, send, recv, src_thru, land_thru, token):
        x, y, c = _position()
        for k in range(N_CHIP):
            pltpu.make_async_remote_copy(
                src_ref=src.at[2 * k + (1 - c)], dst_ref=land_ref.at[k], send_sem=send.at[k], recv_sem=recv.at[k],
                device_id=(x, y, 1 - c), device_id_type=MESH).start()
        token[...] = jnp.zeros_like(token)

    return pl.pallas_call(
        body, name=name,
        out_shape=[pltpu.SemaphoreType.DMA((N_CHIP,)), pltpu.SemaphoreType.DMA((N_CHIP,)),
                   pltpu.HBM(full.shape, full.dtype), pltpu.HBM(land.shape, land.dtype),
                   jax.ShapeDtypeStruct((8, LANES), F32)],
        in_specs=[HBM, HBM], out_specs=[SEM, SEM, HBM, HBM, pl.BlockSpec(memory_space=pltpu.VMEM)],
        input_output_aliases={0: 2, 1: 3},
        compiler_params=pltpu.CompilerParams(has_side_effects=EFFECT),
    )(_hbm(full), land)


def _sib_wait(send, recv, src_thru, land_thru, after, name):
    def body(src, land_ref, send_sem, recv_sem, after_ref, src_out, land_out):
        x, y, c = _position()
        for k in range(N_CHIP):
            cp = pltpu.make_async_remote_copy(
                src_ref=src.at[0], dst_ref=land_ref.at[0], send_sem=send_sem.at[k], recv_sem=recv_sem.at[k],
                device_id=(x, y, 1 - c), device_id_type=MESH)
            cp.wait_send()
            cp.wait_recv()

    return pl.pallas_call(
        body, name=name,
        out_shape=(pltpu.HBM(src_thru.shape, src_thru.dtype), pltpu.HBM(land_thru.shape, land_thru.dtype)),
        in_specs=(HBM, HBM, SEM, SEM, ANY), out_specs=(HBM, HBM), input_output_aliases={0: 0, 1: 1},
        compiler_params=pltpu.CompilerParams(has_side_effects=EFFECT),
    )(src_thru, land_thru, send, recv, after)


def _rs_sibling(arrs, name):
    n = len(arrs)

    def body(*refs):
        ins, outs = refs[:n], refs[n:2 * n]
        send_sems, recv_sems = refs[2 * n:]
        x, y, c = _position()
        copies = []
        for a in range(n):
            for k in range(N_CHIP):
                cp = pltpu.make_async_remote_copy(
                    src_ref=ins[a].at[2 * k + (1 - c)], dst_ref=outs[a].at[k],
                    send_sem=send_sems.at[N_CHIP * a + k], recv_sem=recv_sems.at[N_CHIP * a + k],
                    device_id=(x, y, 1 - c), device_id_type=MESH)
                cp.start()
                copies.append(cp)
        for cp in copies:
            cp.wait()

    return _pcall(
        body, name=name,
        out_shape=[jax.ShapeDtypeStruct((N_CHIP,) + a.shape[1:], a.dtype) for a in arrs],
        in_specs=[ANY] * n, out_specs=[ANY] * n,
        scratch_shapes=[pltpu.SemaphoreType.DMA((N_CHIP * n,)), pltpu.SemaphoreType.DMA((N_CHIP * n,))],
    )(*arrs)


def _chip_sum(full, recv, core, name):
    _, rows, cols = full.shape
    tr = _blk(rows, 1024) if rows % LANES == 0 else rows

    def body(core_ref, a_ref, b_ref, o_ref):
        o_ref[...] = (a_ref[...].astype(F32) + b_ref[...].astype(F32)).astype(o_ref.dtype)

    grid_spec = pltpu.PrefetchScalarGridSpec(
        num_scalar_prefetch=1, grid=(N_CHIP, rows // tr),
        in_specs=[pl.BlockSpec((None, tr, cols), lambda k, i, core_ref: (2 * k + core_ref[0], i, 0)),
                  pl.BlockSpec((None, tr, cols), lambda k, i, core_ref: (k, i, 0))],
        out_specs=pl.BlockSpec((None, tr, cols), lambda k, i, core_ref: (k, i, 0)))
    return _pcall(body, name=name, out_shape=jax.ShapeDtypeStruct((N_CHIP, rows, cols), full.dtype),
                  grid_spec=grid_spec)(core, full, recv)


def _matmul(a, b, *, name, ta=False, tb=False, tm=1024, tn=1024, tk=2048, out_dtypes=(BF16,), epilogue=None,
            row_extras=(), tile_extras=(), out_shape=None, out_map=None, b_sharded=False, n_cols=None, n_off=0,
            after=None, b_pair=False):
    m, k = (a.shape[1], a.shape[0]) if ta else a.shape
    if b_sharded:
        shard_c = b.shape[2]
        n, kb = (b.shape[1], N_DEV * shard_c) if tb else (N_DEV * shard_c, b.shape[1])
        tn, tk = (tn, min(tk, shard_c)) if tb else (min(tn, shard_c), tk)
        if b_pair:
            assert tb and tk == shard_c
            tk = 2 * shard_c
    else:
        n, kb = b.shape if tb else (b.shape[1], b.shape[0])
    assert kb == k, (name, kb, k)
    if n_cols is not None:
        n = n_cols
    tm, tn, tk = _blk(m, tm), _blk(n, tn), _blk(k, tk)
    assert n_off % tn == 0
    nk = k // tk
    dims = (((0 if ta else 1,), (1 if tb else 0,)), ((), ()))
    behind = () if after is None else (after,)
    n_row, n_tile, n_out = len(row_extras), len(tile_extras), len(out_dtypes)
    n_b = 2 if b_pair else 1
    first_out = 1 + n_b + n_row + n_tile + len(behind)

    def body(*refs):
        a_ref, b_ref = refs[:2]
        extras = refs[1 + n_b:1 + n_b + n_row + n_tile]
        outs = refs[first_out:first_out + n_out]
        acc_ref = refs[-1]
        jj, kk = pl.program_id(1), pl.program_id(2)
        if b_pair:
            half = tk // 2
            part = (lax.dot_general(a_ref[:, 0:half].astype(BF16), b_ref[...].astype(BF16), dims,
                                    preferred_element_type=F32)
                    + lax.dot_general(a_ref[:, half:tk].astype(BF16), refs[2][...].astype(BF16), dims,
                                      preferred_element_type=F32))
        else:
            part = lax.dot_general(a_ref[...].astype(BF16), b_ref[...].astype(BF16), dims,
                                   preferred_element_type=F32)

        def finish(acc):
            res = (acc,) if epilogue is None else epilogue(acc, jj, *[e[...] for e in extras])
            for o_ref, r in zip(outs, res):
                o_ref[...] = r.astype(o_ref.dtype)

        if nk == 1:
            finish(part)
        else:
            @pl.when(kk == 0)
            def _():
                acc_ref[...] = part

            @pl.when(kk > 0)
            def _():
                acc_ref[...] += part

            @pl.when(kk == nk - 1)
            def _():
                finish(acc_ref[...])

    a_spec = pl.BlockSpec((tk, tm), lambda i, j, kk: (kk, i)) if ta else pl.BlockSpec((tm, tk), lambda i, j, kk: (i, kk))
    if b_pair:
        b_spec = pl.BlockSpec((None, tn, shard_c), lambda i, j, kk: (2 * kk, j, 0))
    elif b_sharded and tb:
        per = shard_c // tk
        b_spec = pl.BlockSpec((None, tn, tk), lambda i, j, kk: (kk // per, j, kk % per))
    elif b_sharded:
        per = shard_c // tn
        b_spec = pl.BlockSpec((None, tk, tn), lambda i, j, kk: (j // per, kk, j % per))
    elif tb:
        b_spec = pl.BlockSpec((tn, tk), lambda i, j, kk: (j + n_off // tn, kk))
    else:
        b_spec = pl.BlockSpec((tk, tn), lambda i, j, kk: (kk, j + n_off // tn))
    in_specs = [a_spec, b_spec]
    if b_pair:
        in_specs.append(pl.BlockSpec((None, tn, shard_c), lambda i, j, kk: (2 * kk + 1, j, 0)))
    in_specs += [pl.BlockSpec((tm, LANES), lambda i, j, kk: (i, 0)) for _ in row_extras]
    in_specs += [pl.BlockSpec((tm, tn), lambda i, j, kk: (i, j)) for _ in tile_extras]
    in_specs += [ANY for _ in behind]
    if out_map is None:
        out_specs = [pl.BlockSpec((tm, tn), lambda i, j, kk: (i, j)) for _ in out_dtypes]
        shapes = [jax.ShapeDtypeStruct((m, n), dt) for dt in out_dtypes]
    else:
        out_specs = [out_map(tm, tn)]
        shapes = [jax.ShapeDtypeStruct(out_shape, out_dtypes[0])]
    acc_shape = (tm, tn) if nk > 1 else (8, LANES)
    res = _pcall(body, name=name, out_shape=shapes, grid=(m // tm, n // tn, nk), in_specs=in_specs,
                 out_specs=out_specs, scratch_shapes=[pltpu.VMEM(acc_shape, F32)])(
                     a, *([b, b] if b_pair else [b]), *row_extras, *tile_extras, *behind)
    return res[0] if n_out == 1 else res


def _rope_cols(acc, j, cos, sin, n_rope):
    width = acc.shape[1]
    parts = []
    for g in range(width // HEAD_DIM):
        xg = acc[:, g * HEAD_DIM:(g + 1) * HEAD_DIM]
        roped = xg * cos + pltpu.roll(xg, HEAD_DIM // 2, 1) * sin
        parts.append(jnp.where(j * width + g * HEAD_DIM < n_rope, roped, xg))
    return jnp.concatenate(parts, axis=1) if len(parts) > 1 else parts[0]


def _silu(v):
    return v / (1.0 + jnp.exp(-v))


def _mod_part(c_all, w_mod, b_part):
    d, w = w_mod.shape
    tk = _blk(d, 512)

    def body(c_ref, w_ref, b_ref, o_ref):
        kk = pl.program_id(0)
        cond = _silu(c_ref[...]).astype(BF16)
        part = jnp.dot(cond, w_ref[...].astype(BF16), preferred_element_type=F32)

        @pl.when(kk == 0)
        def _():
            o_ref[...] = part + b_ref[...]

        @pl.when(kk > 0)
        def _():
            o_ref[...] += part

    return _pcall(body, name="mod_part", out_shape=jax.ShapeDtypeStruct((N_DEV, w), F32), grid=(d // tk,),
                  in_specs=[pl.BlockSpec((N_DEV, tk), lambda kk: (0, kk)), pl.BlockSpec((tk, w), lambda kk: (kk, 0)),
                            pl.BlockSpec((1, w), lambda kk: (0, 0))],
                  out_specs=pl.BlockSpec((N_DEV, w), lambda kk: (0, 0)))(c_all, w_mod, b_part)


def _row_call(body, name, t, d, tiled_in, vec_in, tiled_out_dtypes, n_vec_out, tr=256):
    tr = _blk(t, tr)
    tile = pl.BlockSpec((tr, d), lambda i: (i, 0))
    vec = pl.BlockSpec((1, d), lambda i: (0, 0))
    out_shape = [jax.ShapeDtypeStruct((t, d), dt) for dt in tiled_out_dtypes]
    out_shape += [jax.ShapeDtypeStruct((1, d), F32)] * n_vec_out
    return _pcall(body, name=name, out_shape=out_shape, grid=(t // tr,),
                  in_specs=[tile] * len(tiled_in) + [vec] * len(vec_in),
                  out_specs=[tile] * len(tiled_out_dtypes) + [vec] * n_vec_out)(*tiled_in, *vec_in)


def _accumulate(ref, val):
    @pl.when(pl.program_id(0) == 0)
    def _():
        ref[...] = val

    @pl.when(pl.program_id(0) > 0)
    def _():
        ref[...] += val


def _rsum(v):
    return jnp.sum(v, axis=0, keepdims=True)


def _rms(v):
    return lax.rsqrt(jnp.mean(v * v, axis=-1, keepdims=True) + NORM_EPS)


def _rms_bwd(vhat, r, dvhat):
    return r * (dvhat - vhat * jnp.mean(dvhat * vhat, axis=-1, keepdims=True))


def _pre_attn(x, g0, sc_a, sh_a):
    def body(x_ref, g_ref, sc_ref, sh_ref, h_ref):
        xv = x_ref[...]
        h_ref[...] = (xv * _rms(xv) * g_ref[...] * (1.0 + sc_ref[...]) + sh_ref[...]).astype(BF16)

    t, d = x.shape
    return _row_call(body, "pre_attn", t, d, [x], [g0, sc_a, sh_a], [BF16], 0)[0]


def _post_mix(x, mix, gt_a, g1, g2, sc_m, sh_m):
    def body(x_ref, mix_ref, gt_ref, g1_ref, g2_ref, sc_ref, sh_ref, x2_ref, h2_ref):
        mv = mix_ref[...]
        x2 = x_ref[...] + gt_ref[...] * (mv * _rms(mv) * g1_ref[...])
        x2_ref[...] = x2
        h2_ref[...] = (x2 * _rms(x2) * g2_ref[...] * (1.0 + sc_ref[...]) + sh_ref[...]).astype(BF16)

    t, d = x.shape
    return _row_call(body, "post_mix", t, d, [x, mix], [gt_a, g1, g2, sc_m, sh_m], [F32, BF16], 0)


def _final(y, x2, target, gt_m, g3):
    t, d = y.shape

    def body(y_ref, x2_ref, tg_ref, gt_ref, g3_ref, dy_ref, dout_ref, dgt_ref, dg3_ref, loss_ref):
        yv = y_ref[...]
        r = _rms(yv)
        yhat = yv * r
        n3 = yhat * g3_ref[...]
        err = x2_ref[...] + gt_ref[...] * n3 - tg_ref[...]
        _accumulate(loss_ref, jnp.zeros((1, d), F32) + 0.5 * jnp.sum(err * err) / d)
        dout = err * (1.0 / d)
        dout_ref[...] = dout
        _accumulate(dgt_ref, _rsum(dout * n3))
        dn3 = dout * gt_ref[...]
        _accumulate(dg3_ref, _rsum(dn3 * yhat))
        dy_ref[...] = _rms_bwd(yhat, r, dn3 * g3_ref[...]).astype(BF16)

    return _row_call(body, "final", t, d, [y, x2, target], [gt_m, g3], [BF16, F32], 3)


def _mid_bwd(dh2, dout, x2, mix, g2, sc_m, gt_a, g1):
    t, d = x2.shape

    def body(dh2_ref, dout_ref, x2_ref, mix_ref, g2_ref, sc_ref, gt_ref, g1_ref,
             dmix_ref, dx2_ref, dsh_ref, dsc_ref, dg2_ref, dgt_ref, dg1_ref):
        dh2v = dh2_ref[...].astype(F32)
        x2v = x2_ref[...]
        r2 = _rms(x2v)
        x2hat = x2v * r2
        _accumulate(dsh_ref, _rsum(dh2v))
        _accumulate(dsc_ref, _rsum(dh2v * (x2hat * g2_ref[...])))
        dn2 = dh2v * (1.0 + sc_ref[...])
        _accumulate(dg2_ref, _rsum(dn2 * x2hat))
        dx2 = dout_ref[...] + _rms_bwd(x2hat, r2, dn2 * g2_ref[...])
        dx2_ref[...] = dx2
        mv = mix_ref[...]
        r1 = _rms(mv)
        mhat = mv * r1
        _accumulate(dgt_ref, _rsum(dx2 * (mhat * g1_ref[...])))
        dn1 = dx2 * gt_ref[...]
        _accumulate(dg1_ref, _rsum(dn1 * mhat))
        dmix_ref[...] = _rms_bwd(mhat, r1, dn1 * g1_ref[...]).astype(BF16)

    return _row_call(body, "mid_bwd", t, d, [dh2, dout, x2, mix], [g2, sc_m, gt_a, g1], [BF16, F32], 5)


def _x_bwd(dh1, dx2, x, g0, sc_a):
    t, d = x.shape

    def body(dh1_ref, dx2_ref, x_ref, g0_ref, sc_ref, dx_ref, dsh_ref, dsc_ref, dg0_ref):
        dh1v = dh1_ref[...].astype(F32)
        xv = x_ref[...]
        r0 = _rms(xv)
        xhat = xv * r0
        _accumulate(dsh_ref, _rsum(dh1v))
        _accumulate(dsc_ref, _rsum(dh1v * (xhat * g0_ref[...])))
        dn0 = dh1v * (1.0 + sc_ref[...])
        _accumulate(dg0_ref, _rsum(dn0 * xhat))
        dx_ref[...] = dx2_ref[...] + _rms_bwd(xhat, r0, dn0 * g0_ref[...])

    return _row_call(body, "x_bwd", t, d, [dh1, dx2, x], [g0, sc_a], [F32], 3)


def _pick_lane(block, h):
    lane = lax.broadcasted_iota(jnp.int32, block.shape, 1)
    return jnp.sum(jnp.where(lane == h, block, 0.0), axis=1, keepdims=True)


def _put_lane(ref, rows, h, col):
    old = ref[rows, :]
    lane = lax.broadcasted_iota(jnp.int32, old.shape, 1)
    ref[rows, :] = jnp.where(lane == h, col, old)


def _tri(n, lower):
    r = lax.broadcasted_iota(jnp.int32, (n, n), 0)
    c = lax.broadcasted_iota(jnp.int32, (n, n), 1)
    return jnp.where((c <= r) if lower else (c >= r), 1.0, 0.0).astype(F32)


def _cum_fwd(fg, b128):
    t = fg.shape[0]
    nb = t // LANES

    def body(fg_ref, b_ref, cum_ref, cumt_ref):
        tri = _tri(LANES, True)
        carry = jnp.zeros((1, LANES), F32)
        for i in range(nb):
            z = fg_ref[i * LANES:(i + 1) * LANES, :] + b_ref[...]
            lf = jnp.minimum(z, 0.0) - jnp.log(1.0 + jnp.exp(-jnp.abs(z)))
            blk = jnp.dot(tri, lf, precision=lax.Precision.HIGHEST, preferred_element_type=F32) + carry
            cum_ref[i * LANES:(i + 1) * LANES, :] = blk
            carry = blk[LANES - 1:LANES, :]
        cumt_ref[...] = cum_ref[...].T[0:N_FOX, :]

    return _pcall(body, name="cum_fwd",
                  out_shape=[jax.ShapeDtypeStruct((t, LANES), F32), jax.ShapeDtypeStruct((N_FOX, t), F32)],
                  grid=(1,),
                  in_specs=[pl.BlockSpec((t, LANES), lambda i: (0, 0)), pl.BlockSpec((1, LANES), lambda i: (0, 0))],
                  out_specs=[pl.BlockSpec((t, LANES), lambda i: (0, 0)), pl.BlockSpec((N_FOX, t), lambda i: (0, 0))],
                  )(fg, b128)


def _fg_bwd(dcs_rows, fg, b128):
    t = fg.shape[0]
    nb = t // LANES

    def body(dcs_ref, fg_ref, b_ref, dfg_ref, db_ref, dcum_ref):
        dcum_ref[...] = -jnp.concatenate([dcs_ref[...], jnp.zeros((LANES - N_FOX, t), F32)], axis=0).T
        tri = _tri(LANES, False)
        carry = jnp.zeros((1, LANES), F32)
        db = jnp.zeros((1, LANES), F32)
        for i in reversed(range(nb)):
            rows = slice(i * LANES, (i + 1) * LANES)
            dlf = jnp.dot(tri, dcum_ref[rows, :], precision=lax.Precision.HIGHEST, preferred_element_type=F32) + carry
            carry = dlf[0:1, :]
            z = fg_ref[rows, :] + b_ref[...]
            dfg = dlf / (1.0 + jnp.exp(z))
            dfg_ref[rows, :] = dfg.astype(BF16)
            db = db + _rsum(dfg)
        db_ref[...] = db

    full = pl.BlockSpec((t, LANES), lambda i: (0, 0))
    vec = pl.BlockSpec((1, LANES), lambda i: (0, 0))
    return _pcall(body, name="fg_bwd",
                  out_shape=[jax.ShapeDtypeStruct((t, LANES), BF16), jax.ShapeDtypeStruct((1, LANES), F32)],
                  grid=(1,), in_specs=[pl.BlockSpec((N_FOX, t), lambda i: (0, 0)), full, vec], out_specs=[full, vec],
                  scratch_shapes=[pltpu.VMEM((t, LANES), F32)])(dcs_rows, fg, b128)


def _head_spec(t, col0, div=1):
    return pl.BlockSpec((t, HEAD_DIM), lambda h: (0, col0 + h // div))


def _fox_scores(q, k, cq, ck, i, tq, end):
    s = lax.dot_general(q, k, NT_DIMS, preferred_element_type=F32) * ATT_SCALE + cq - ck
    row = lax.broadcasted_iota(jnp.int32, (tq, end), 0) + i * tq
    col = lax.broadcasted_iota(jnp.int32, (tq, end), 1)
    return jnp.where(row >= col, s, -jnp.inf)


def _fox_fwd(proj_a, cum, cumt):
    t = proj_a.shape[0]
    tq = _blk(t, 512)
    nq = t // tq

    def body(q_ref, k_ref, v_ref, cum_ref, cumt_ref, o_ref, lse_ref):
        h = pl.program_id(0)
        cq_all = _pick_lane(cum_ref[...], h)
        ck_all = cumt_ref[pl.ds(h, 1), :]

        @pl.when(h == 0)
        def _():
            lse_ref[...] = jnp.zeros_like(lse_ref)

        for i in range(nq):
            rows, end = slice(i * tq, (i + 1) * tq), (i + 1) * tq
            s = _fox_scores(q_ref[rows, :], k_ref[0:end, :], cq_all[rows, :], ck_all[:, 0:end], i, tq, end)
            m = jnp.max(s, axis=1, keepdims=True)
            p = jnp.exp(s - m)
            l = jnp.sum(p, axis=1, keepdims=True)
            o = jnp.dot(p.astype(BF16), v_ref[0:end, :], preferred_element_type=F32) / l
            o_ref[rows, :] = o.astype(BF16)
            _put_lane(lse_ref, rows, h, m + jnp.log(l))

    nh = FOX_W // HEAD_DIM
    stat = pl.BlockSpec((t, LANES), lambda h: (0, 0))
    return _pcall(body, name="fox_fwd",
                  out_shape=[jax.ShapeDtypeStruct((t, FOX_W), BF16), jax.ShapeDtypeStruct((t, LANES), F32)],
                  grid=(N_FOX,),
                  in_specs=[_head_spec(t, 0), _head_spec(t, nh), _head_spec(t, 2 * nh), stat,
                            pl.BlockSpec((N_FOX, t), lambda h: (0, 0))],
                  out_specs=[_head_spec(t, 0), stat])(proj_a, proj_a, proj_a, cum, cumt)


def _fox_bwd(proj_a, d_attn, cum, cumt, lse):
    t = proj_a.shape[0]
    tq = _blk(t, 512)
    nq = t // tq

    def body(q_ref, k_ref, v_ref, do_ref, cum_ref, cumt_ref, lse_ref,
             dq_ref, dk_ref, dv_ref, dcs_ref, dk_acc, dv_acc, dcs_acc):
        h = pl.program_id(0)
        cq_all = _pick_lane(cum_ref[...], h)
        ck_all = cumt_ref[pl.ds(h, 1), :]
        lse_all = _pick_lane(lse_ref[...], h)
        dk_acc[...] = jnp.zeros_like(dk_acc)
        dv_acc[...] = jnp.zeros_like(dv_acc)
        dcs_acc[...] = jnp.zeros_like(dcs_acc)
        for i in range(nq):
            rows, end = slice(i * tq, (i + 1) * tq), (i + 1) * tq
            q, k, v, do = q_ref[rows, :], k_ref[0:end, :], v_ref[0:end, :], do_ref[rows, :]
            s = _fox_scores(q, k, cq_all[rows, :], ck_all[:, 0:end], i, tq, end)
            p = jnp.exp(s - lse_all[rows, :])
            dp = lax.dot_general(do, v, NT_DIMS, preferred_element_type=F32)
            ds = p * (dp - jnp.sum(p * dp, axis=1, keepdims=True))
            dcs_acc[:, 0:end] += jnp.sum(ds, axis=0, keepdims=True)
            ds = ds.astype(BF16)
            dq_ref[rows, :] = (jnp.dot(ds, k, preferred_element_type=F32) * ATT_SCALE).astype(BF16)
            dk_acc[0:end, :] += lax.dot_general(ds, q, TN_DIMS, preferred_element_type=F32)
            dv_acc[0:end, :] += lax.dot_general(p.astype(BF16), do, TN_DIMS, preferred_element_type=F32)
        dk_ref[...] = (dk_acc[...] * ATT_SCALE).astype(BF16)
        dv_ref[...] = dv_acc[...].astype(BF16)
        dcs_ref[pl.ds(h, 1), :] = dcs_acc[...]

    nh = FOX_W // HEAD_DIM
    stat = pl.BlockSpec((t, LANES), lambda h: (0, 0))
    rows8 = pl.BlockSpec((N_FOX, t), lambda h: (0, 0))
    head = _head_spec(t, 0)
    wide = jax.ShapeDtypeStruct((t, FOX_W), BF16)
    return _pcall(body, name="fox_bwd",
                  out_shape=[wide, wide, wide, jax.ShapeDtypeStruct((N_FOX, t), F32)],
                  grid=(N_FOX,),
                  in_specs=[_head_spec(t, 0), _head_spec(t, nh), _head_spec(t, 2 * nh), head, stat, rows8, stat],
                  out_specs=[head, head, head, rows8],
                  scratch_shapes=[pltpu.VMEM((t, HEAD_DIM), F32), pltpu.VMEM((t, HEAD_DIM), F32),
                                  pltpu.VMEM((1, t), F32)],
                  )(proj_a, proj_a, proj_a, d_attn, cum, cumt, lse)


def _swa_scores(q, k, i, tq, start, end):
    s = lax.dot_general(q, k, NT_DIMS, preferred_element_type=F32) * ATT_SCALE
    row = lax.broadcasted_iota(jnp.int32, (tq, end - start), 0) + i * tq
    col = lax.broadcasted_iota(jnp.int32, (tq, end - start), 1) + start
    diff = row - col
    return jnp.where((diff >= 0) & (diff < WINDOW), s, -jnp.inf)


def _swa_blocks(t):
    tq = _blk(t, 256)
    return tq, [(i, max(0, i * tq - WINDOW), (i + 1) * tq) for i in range(t // tq)]


def _swa_fwd(proj_b, sinks128):
    t = proj_b.shape[0]
    tq, blocks = _swa_blocks(t)

    def body(q_ref, k_ref, v_ref, sink_ref, o_ref, lse_ref):
        h = pl.program_id(0)
        sink = _pick_lane(sink_ref[...], h)

        @pl.when(h == 0)
        def _():
            lse_ref[...] = jnp.zeros_like(lse_ref)

        for i, start, end in blocks:
            rows = slice(i * tq, end)
            s = _swa_scores(q_ref[rows, :], k_ref[start:end, :], i, tq, start, end)
            m = jnp.maximum(jnp.max(s, axis=1, keepdims=True), sink)
            p = jnp.exp(s - m)
            l = jnp.sum(p, axis=1, keepdims=True) + jnp.exp(sink - m)
            o = jnp.dot(p.astype(BF16), v_ref[start:end, :], preferred_element_type=F32) / l
            o_ref[rows, :] = o.astype(BF16)
            _put_lane(lse_ref, rows, h, m + jnp.log(l))

    stat = pl.BlockSpec((t, LANES), lambda h: (0, 0))
    return _pcall(body, name="swa_fwd",
                  out_shape=[jax.ShapeDtypeStruct((t, SWA_W), BF16), jax.ShapeDtypeStruct((t, LANES), F32)],
                  grid=(N_SWA,),
                  in_specs=[_head_spec(t, 0), _head_spec(t, N_SWA, GQA), _head_spec(t, N_SWA + N_KV, GQA),
                            pl.BlockSpec((1, LANES), lambda h: (0, 0))],
                  out_specs=[_head_spec(t, 0), stat])(proj_b, proj_b, proj_b, sinks128)


def _rope_bwd(d, cos, sin):
    return d * cos + pltpu.roll(d * sin, HEAD_DIM // 2, 1)


def _swa_bwd(proj_b, d_attn, lse, sinks128, cos, sin):
    t = proj_b.shape[0]
    tq, blocks = _swa_blocks(t)

    def body(q_ref, k_ref, v_ref, do_ref, lse_ref, sink_ref, cos_ref, sin_ref,
             dq_ref, dk_ref, dv_ref, dsink_ref, dk_acc, dv_acc):
        h = pl.program_id(0)
        sink = _pick_lane(sink_ref[...], h)
        lse_all = _pick_lane(lse_ref[...], h)

        @pl.when(h == 0)
        def _():
            dsink_ref[...] = jnp.zeros_like(dsink_ref)

        @pl.when(h % GQA == 0)
        def _():
            dk_acc[...] = jnp.zeros_like(dk_acc)
            dv_acc[...] = jnp.zeros_like(dv_acc)

        dsink = jnp.zeros((1, 1), F32)
        for i, start, end in blocks:
            rows = slice(i * tq, end)
            q, k, v, do = q_ref[rows, :], k_ref[start:end, :], v_ref[start:end, :], do_ref[rows, :]
            s = _swa_scores(q, k, i, tq, start, end)
            p = jnp.exp(s - lse_all[rows, :])
            dp = lax.dot_general(do, v, NT_DIMS, preferred_element_type=F32)
            delta = jnp.sum(p * dp, axis=1, keepdims=True)
            ds = (p * (dp - delta)).astype(BF16)
            dq = jnp.dot(ds, k, preferred_element_type=F32) * ATT_SCALE
            dq_ref[rows, :] = _rope_bwd(dq, cos_ref[rows, :], sin_ref[rows, :]).astype(BF16)
            dk_acc[start:end, :] += lax.dot_general(ds, q, TN_DIMS, preferred_element_type=F32)
            dv_acc[start:end, :] += lax.dot_general(p.astype(BF16), do, TN_DIMS, preferred_element_type=F32)
            dsink = dsink - jnp.sum(jnp.exp(sink - lse_all[rows, :]) * delta, axis=0, keepdims=True)
        old = dsink_ref[...]
        lane = lax.broadcasted_iota(jnp.int32, old.shape, 1)
        dsink_ref[...] = jnp.where(lane == h, dsink, old)

        @pl.when(h % GQA == GQA - 1)
        def _():
            dk_ref[...] = _rope_bwd(dk_acc[...] * ATT_SCALE, cos_ref[...], sin_ref[...]).astype(BF16)
            dv_ref[...] = dv_acc[...].astype(BF16)

    stat = pl.BlockSpec((t, LANES), lambda h: (0, 0))
    vec = pl.BlockSpec((1, LANES), lambda h: (0, 0))
    head = _head_spec(t, 0)
    kv_out = _head_spec(t, 0, GQA)
    return _pcall(body, name="swa_bwd",
                  out_shape=[jax.ShapeDtypeStruct((t, SWA_W), BF16), jax.ShapeDtypeStruct((t, KV_W), BF16),
                             jax.ShapeDtypeStruct((t, KV_W), BF16), jax.ShapeDtypeStruct((1, LANES), F32)],
                  grid=(N_SWA,),
                  in_specs=[head, _head_spec(t, N_SWA, GQA), _head_spec(t, N_SWA + N_KV, GQA),
                            _head_spec(t, N_FOX), stat, vec, stat, stat],
                  out_specs=[head, kv_out, kv_out, vec],
                  scratch_shapes=[pltpu.VMEM((t, HEAD_DIM), F32), pltpu.VMEM((t, HEAD_DIM), F32)],
                  )(proj_b, proj_b, proj_b, d_attn, lse, sinks128, cos, sin)


def _adamw(w, g, m, v):
    m = ADAM_B1 * m + (1.0 - ADAM_B1) * g
    v = ADAM_B2 * v + (1.0 - ADAM_B2) * (g * g)
    m_hat = m / (1.0 - ADAM_B1 ** ADAM_STEP)
    v_hat = v / (1.0 - ADAM_B2 ** ADAM_STEP)
    delta = -ADAM_LR * (m_hat / (jnp.sqrt(v_hat) + ADAM_EPS) + ADAM_WD * w)
    return delta, m, v


def _adam_pieces(w, m, v, own, land, idx, name):
    rows, cols = w.shape
    tr, tc = (256, cols) if rows % 256 == 0 else (rows, _blk(cols, 512))

    def body(idx_ref, own_ref, l1_ref, l2_ref, l3_ref, w_ref, m_ref, v_ref, g_ref, d_ref, mo_ref, vo_ref):
        g = own_ref[...].astype(F32) + l1_ref[...].astype(F32) + l2_ref[...].astype(F32) + l3_ref[...].astype(F32)
        g_ref[...] = g
        d_ref[...], mo_ref[...], vo_ref[...] = _adamw(w_ref[...], g, m_ref[...], v_ref[...])

    def piece(p):
        return pl.BlockSpec((None, tr, tc), lambda i, j, idx_ref: (idx_ref[p], i, j))

    tile = pl.BlockSpec((tr, tc), lambda i, j, idx_ref: (i, j))
    out = jax.ShapeDtypeStruct((rows, cols), F32)
    grid_spec = pltpu.PrefetchScalarGridSpec(
        num_scalar_prefetch=1, grid=(rows // tr, cols // tc),
        in_specs=[piece(0), piece(1), piece(2), piece(3), tile, tile, tile], out_specs=[tile] * 4)
    return _pcall(body, name=name, out_shape=[out] * 4, grid_spec=grid_spec)(idx, own, land, land, land, w, m, v)


def _adam_mod(c_all, dmod_cols, w, m, v):
    rows, cols = w.shape
    tr = _blk(rows, 256)

    def body(c_ref, dm_ref, w_ref, m_ref, v_ref, g_ref, d_ref, mo_ref, vo_ref):
        cond = _silu(c_ref[...]).astype(BF16)
        g = lax.dot_general(cond, dm_ref[...].astype(BF16), TN_DIMS, preferred_element_type=F32)
        g_ref[...] = g
        d_ref[...], mo_ref[...], vo_ref[...] = _adamw(w_ref[...], g, m_ref[...], v_ref[...])

    tile = pl.BlockSpec((tr, cols), lambda i: (i, 0))
    out = jax.ShapeDtypeStruct((rows, cols), F32)
    return _pcall(body, name="adam_mod", out_shape=[out] * 4, grid=(rows // tr,),
                  in_specs=[pl.BlockSpec((N_DEV, tr), lambda i: (0, i)), pl.BlockSpec((N_DEV, cols), lambda i: (0, 0)),
                            tile, tile, tile],
                  out_specs=[tile] * 4)(c_all, dmod_cols, w, m, v)


def _adam_small(parts, w, m, v):
    nv = w.shape[1]

    def body(p_ref, w_ref, m_ref, v_ref, g_ref, d_ref, mo_ref, vo_ref):
        g = p_ref[0:1, :]
        for k in range(1, N_DEV):
            g = g + p_ref[k:k + 1, :]
        g_ref[...] = g
        d_ref[...], mo_ref[...], vo_ref[...] = _adamw(w_ref[...], g, m_ref[...], v_ref[...])

    vec = pl.BlockSpec((1, nv), lambda i: (0, 0))
    out = jax.ShapeDtypeStruct((1, nv), F32)
    return _pcall(body, name="adam_small", out_shape=[out] * 4, grid=(1,),
                  in_specs=[pl.BlockSpec((N_DEV, nv), lambda i: (0, 0)), vec, vec, vec],
                  out_specs=[vec] * 4)(parts, w, m, v)


def _pad_lanes(v, width=LANES):
    return jnp.pad(v, ((0, 0), (0, width - v.shape[1])))


def kernel(x, c, w_mod, b_mod, g_pre_mix, g_post_mix, w_in, b_forget, swa_sinks, w_out, g_pre_mlp, g_post_mlp, w_up, w_down, loss_target, m_w_mod, m_b_mod, m_g_pre_mix, m_g_post_mix, m_w_in, m_b_forget, m_swa_sinks, m_w_out, m_g_pre_mlp, m_g_post_mlp, m_w_up, m_w_down, v_w_mod, v_b_mod, v_g_pre_mix, v_g_post_mix, v_w_in, v_b_forget, v_swa_sinks, v_w_out, v_g_pre_mlp, v_g_post_mlp, v_w_up, v_w_down):
    ax, ay, ac = _position()
    me = 4 * ax + 2 * ay + ac
    x, target = x[0], loss_target[0]
    t, d = x.shape
    w_mod, w_in, w_out, w_up, w_down = w_mod[0], w_in[0], w_out[0], w_up[0], w_down[0]
    mod_w = w_mod.shape[1]
    in_w = w_in.shape[1]
    in_total = N_DEV * in_w
    shard_ff = w_up.shape[1]
    n_fox3 = 3 * FOX_W
    n_swa3 = SWA_W + 2 * KV_W
    assert in_total == n_fox3 + N_FOX + n_swa3 and d == FOX_W + SWA_W

    c_all = _all_gather([c], "gather_c")[0].reshape(N_DEV, d)
    b_part = lax.dynamic_slice(b_mod, (0, me * mod_w), (1, mod_w))
    mod_parts = _all_gather([_mod_part(c_all, w_mod, b_part)], "gather_mod")[0]
    mod = lax.dynamic_index_in_dim(mod_parts, me, axis=1, keepdims=False).reshape(1, N_DEV * mod_w)

    w_in_b, mod = lax.optimization_barrier((w_in.astype(BF16), mod))
    first = _ag_start([w_in_b], "ag_start_in")
    behind_first = first[4][0, 0]
    rest = _ag_start([(w + behind_first).astype(BF16) for w in (w_out, w_up, w_down)], "ag_start_rest")
    ag_send, ag_recv, ag_shard, ag_land = [a + b for a, b in zip(first[:4], rest[:4])]
    ag_token = rest[4]

    def gathered(i, after, name):
        shard, land = _ag_wait(ag_send[i], ag_recv[i], ag_shard[i], ag_land[i], after, "ag_wait_" + name)
        return lax.dynamic_update_slice(_ag_forward(land, "ag_fwd_" + name), shard[None], (me, 0, 0))

    def gathered_start(i, after, name):
        shard, land = _ag_wait(ag_send[i], ag_recv[i], ag_shard[i], ag_land[i], after, "ag_wait_" + name)
        return shard, _fwd_start(land, "fwd_start_" + name)

    def gathered_finish(started, after, name):
        shard, (send, recv, land, _) = started
        return lax.dynamic_update_slice(_fwd_wait(send, recv, land, after, "fwd_wait_" + name), shard[None], (me, 0, 0))

    sh_a, sc_a, gt_a, sh_m, sc_m, gt_m = [mod[:, i * d:(i + 1) * d] for i in range(6)]

    half = HEAD_DIM // 2
    inv_freq = 1.0 / (ROPE_THETA ** (jnp.arange(half, dtype=F32) * (2.0 / HEAD_DIM)))
    ang = jnp.arange(t).astype(F32)[:, None] * inv_freq[None, :]
    cos = jnp.concatenate([jnp.cos(ang), jnp.cos(ang)], axis=1)
    sin = jnp.concatenate([-jnp.sin(ang), jnp.sin(ang)], axis=1)

    b128 = _pad_lanes(b_forget)
    sinks128 = _pad_lanes(swa_sinks)

    h1 = _pre_attn(x, g_pre_mix + ag_token[0:1, 0:1], sc_a, sh_a)
    h1, adam_in = lax.optimization_barrier(
        (h1, (jnp.transpose(w_in), jnp.transpose(m_w_in[0]), jnp.transpose(v_w_in[0]))))
    w_in_g = gathered(0, h1, "in")
    o_fg, o_sq = n_fox3, n_fox3 + N_FOX

    def cols(lo, hi):
        parts = []
        for j in range(lo // in_w, (hi - 1) // in_w + 1):
            parts.append(w_in_g[j, :, max(lo - j * in_w, 0):min(hi - j * in_w, in_w)])
        return parts

    w_in_r = jnp.concatenate(cols(0, o_fg) + cols(o_sq, in_total) + cols(o_fg, o_sq)
                             + [jnp.zeros((d, FG_PAD - N_FOX), BF16)], axis=1)
    proj_a = _matmul(h1, w_in_r, name="proj_a", n_cols=n_fox3, n_off=0)
    proj_b = _matmul(h1, w_in_r, name="proj_b", n_cols=n_swa3, n_off=n_fox3, tn=512, row_extras=(cos, sin),
                     epilogue=lambda acc, j, cs, sn: (_rope_cols(acc, j, cs, sn, SWA_W + KV_W),))
    out_started = gathered_start(1, proj_a, "out")
    fg = _matmul(h1, w_in_r, name="proj_fg", n_cols=FG_PAD, n_off=n_fox3 + n_swa3, tn=FG_PAD,
                 out_dtypes=(F32,), after=out_started[1][3])[:, 0:LANES]
    cum, cumt = _cum_fwd(fg, b128)
    fox_o, fox_lse = _fox_fwd(proj_a, cum, cumt)
    up_started = gathered_start(2, fox_o, "up")
    swa_o, swa_lse = _swa_fwd(proj_b, sinks128 + up_started[1][3][0:1, 0:1])
    w_out_full = gathered_finish(out_started, swa_o, "out").reshape(d, d)
    attn = jnp.concatenate([fox_o, swa_o], axis=1)
    mix = _matmul(attn, w_out_full, name="out_proj", out_dtypes=(F32,))
    x2, h2 = _post_mix(x, mix, gt_a, g_post_mix, g_pre_mlp, sc_m, sh_m)
    w_up_g = gathered_finish(up_started, h2, "up")
    u, act = _matmul(h2, w_up_g, name="mlp_up", b_sharded=True, out_dtypes=(BF16, BF16),
                     epilogue=lambda acc, j: (acc, jnp.square(jnp.maximum(acc, 0.0))))
    w_down_full = gathered(3, act, "down").reshape(N_DEV * shard_ff, d)
    y = _matmul(act, w_down_full, name="mlp_down", tk=4096, out_dtypes=(F32,))

    core = jnp.reshape(ac, (1,)).astype(jnp.int32)

    def reduce_start(started, after, name):
        send, recv, src, land, _ = started
        full, from_sibling = _sib_wait(send, recv, src, land, after, "sib_wait_" + name)
        return _rs_start(_chip_sum(full, from_sibling, core, "chip_sum_" + name), "rs_start_" + name)

    def tok(started):
        return started[4][0:1, 0:1]

    idx = jnp.stack([2 * ax + ay, 2 * (1 - ax) + ay, 2 * ax + (1 - ay), 2 * (1 - ax) + (1 - ay)]).astype(jnp.int32)

    def reduce_finish(started, after, w, m, v, name):
        send, recv, src, land, _ = started
        own, landed = _rs_wait(send, recv, src, land, after, "rs_wait_" + name)
        return _adam_pieces(w, m[0], v[0], own, landed, idx, "adam_" + name)

    dy, dout, dgt_m, dg3, loss_vec = _final(y, x2, target, gt_m, g_post_mlp)
    du = _matmul(dy, w_down_full, name="d_act", tb=True, tile_extras=(u,),
                 epilogue=lambda acc, j, uu: (acc * (2.0 * jnp.maximum(uu.astype(F32), 0.0)),))
    dw_down = _matmul(act, dy, name="dw_down", ta=True)
    sb_down = _sib_start(dw_down.reshape(N_DEV, shard_ff, d), "sib_start_down")
    dh2 = _matmul(du, w_up_g, name="d_h2", tb=True, b_sharded=True, b_pair=True, after=sb_down[4])
    rs_down = reduce_start(sb_down, dh2, "down")
    per = shard_ff // _blk(shard_ff, 1024)
    dw_up = _matmul(h2, du, name="dw_up", ta=True, tn=_blk(shard_ff, 1024), out_shape=(N_DEV, d, shard_ff),
                    out_map=lambda tm, tn: pl.BlockSpec((None, tm, tn), lambda i, j, kk: (j // per, i, j % per)),
                    after=rs_down[4])
    sb_up = _sib_start(dw_up, "sib_start_up")
    dmix, dx2, dsh_m, dsc_m, dg2, dgt_a, dg1 = _mid_bwd(
        dh2, dout, x2, mix, g_pre_mlp + tok(sb_up), sc_m, gt_a, g_post_mix)
    d_attn = _matmul(dmix, w_out_full, name="d_attn", tb=True)
    rs_up = reduce_start(sb_up, d_attn, "up")
    dw_out = _matmul(attn, dmix, name="dw_out", ta=True, after=rs_up[4])
    sb_out = _sib_start(dw_out.reshape(N_DEV, d // N_DEV, d), "sib_start_out")
    dqf, dkf, dvf, dcs = _fox_bwd(proj_a, d_attn, cum, cumt, fox_lse)
    dsq, dsk, dsv, dsinks = _swa_bwd(proj_b, d_attn, swa_lse, sinks128 + tok(sb_out), cos, sin)
    rs_out = reduce_start(sb_out, dsq, "out")
    dfg, db_forget = _fg_bwd(dcs, fg, b128 + tok(rs_out))
    dproj = jnp.concatenate([dqf, dkf, dvf, dsq, dsk, dsv, _pad_lanes(dfg, FG_PAD)], axis=1)
    dw_in_r = _matmul(dproj, h1, name="dw_in", ta=True)

    def shard_rows(j):
        lo, hi = j * in_w, (j + 1) * in_w
        parts = []
        for seg_lo, seg_hi, shift in ((0, o_fg, 0), (o_fg, o_sq, n_swa3), (o_sq, in_total, -N_FOX)):
            a, b = max(lo, seg_lo), min(hi, seg_hi)
            if a < b:
                parts.append(dw_in_r[a + shift:b + shift, :])
        return parts[0] if len(parts) == 1 else jnp.concatenate(parts, axis=0)

    sb_in = _sib_start(jnp.stack([shard_rows(j) for j in range(N_DEV)]), "sib_start_in")
    dh1 = _matmul(dproj, w_in_r, name="d_h1", tb=True, tk=2560, after=sb_in[4])
    grad_x, dsh_a, dsc_a, dg0 = _x_bwd(dh1, dx2, x, g_pre_mix, sc_a)

    small = jnp.concatenate([dsh_a, dsc_a, dgt_a, dsh_m, dsc_m, dgt_m, dg0, dg1, dg2, dg3, db_forget, dsinks,
                             loss_vec[:, 0:LANES]], axis=1)
    small_all = _all_gather([small], "gather_small")[0].reshape(N_DEV, small.shape[1])
    rs_in = reduce_start(sb_in, small_all, "in")

    pack = lambda bm, g0_, g1_, g2_, g3_, bf_, sk_: jnp.concatenate(
        [bm, g0_, g1_, g2_, g3_, _pad_lanes(bf_), _pad_lanes(sk_), jnp.zeros((1, LANES), F32)], axis=1)
    p_small = pack(b_mod, g_pre_mix, g_post_mix, g_pre_mlp, g_post_mlp, b_forget, swa_sinks)
    m_small = pack(m_b_mod, m_g_pre_mix, m_g_post_mix, m_g_pre_mlp, m_g_post_mlp, m_b_forget, m_swa_sinks)
    v_small = pack(v_b_mod, v_g_pre_mix, v_g_post_mix, v_g_pre_mlp, v_g_post_mlp, v_b_forget, v_swa_sinks)
    small_out = _adam_small(small_all, p_small + tok(rs_in), m_small, v_small)

    n_mod = 6 * d

    def unpack(vec):
        o = n_mod
        return (vec[:, 0:n_mod], vec[:, o:o + d], vec[:, o + d:o + 2 * d], vec[:, o + 2 * d:o + 3 * d],
                vec[:, o + 3 * d:o + 4 * d], vec[:, o + 4 * d:o + 4 * d + N_FOX],
                vec[:, o + 4 * d + LANES:o + 4 * d + LANES + N_SWA])

    loss = small_out[0][0, n_mod + 4 * d + 2 * LANES]
    g_small, d_small, nm_small, nv_small = [unpack(vec) for vec in small_out]

    dmod_cols = lax.dynamic_slice(small_all, (0, me * mod_w), (N_DEV, mod_w))
    g_w_mod, d_w_mod, nm_w_mod, nv_w_mod = _adam_mod(c_all + tok(rs_in), dmod_cols, w_mod, m_w_mod[0], v_w_mod[0])

    g_w_down, d_w_down, nm_w_down, nv_w_down = reduce_finish(rs_down, d_w_mod, w_down, m_w_down, v_w_down, "w_down")
    g_w_up, d_w_up, nm_w_up, nv_w_up = reduce_finish(rs_up, d_w_down, w_up, m_w_up, v_w_up, "w_up")
    g_w_out, d_w_out, nm_w_out, nv_w_out = reduce_finish(rs_out, d_w_up, w_out, m_w_out, v_w_out, "w_out")
    g_w_in, d_w_in, nm_w_in, nv_w_in = reduce_finish(rs_in, d_w_out, adam_in[0], (adam_in[1],), (adam_in[2],), "w_in")

    def assemble(w_mod_, small_, w_in_, w_out_, w_up_, w_down_):
        b_mod_, g0_, g1_, g2_, g3_, bf_, sk_ = small_
        return [w_mod_[None], b_mod_, g0_, g1_, jnp.transpose(w_in_)[None], bf_, sk_, w_out_[None], g2_, g3_,
                w_up_[None], w_down_[None]]

    outs = [loss, grad_x[None]]
    outs += assemble(g_w_mod, g_small, g_w_in, g_w_out, g_w_up, g_w_down)
    outs += assemble(d_w_mod, d_small, d_w_in, d_w_out, d_w_up, d_w_down)
    outs += assemble(nm_w_mod, nm_small, nm_w_in, nm_w_out, nm_w_up, nm_w_down)
    outs += assemble(nv_w_mod, nv_small, nv_w_in, nv_w_out, nv_w_up, nv_w_down)
    return tuple(outs)
```

```python
import functools

import jax
import jax.numpy as jnp
from jax import lax
from jax.experimental import pallas as pl
from jax.experimental.pallas import tpu as pltpu

F32 = jnp.float32
BF16 = jnp.bfloat16
MESH = pl.DeviceIdType.MESH

N_DEV = 8
N_CHIP = 4
LANES = 128
HEAD_DIM = 128
N_FOX = 8
N_SWA = 8
N_KV = 2
GQA = N_SWA // N_KV
WINDOW = 128
FOX_W = N_FOX * HEAD_DIM
SWA_W = N_SWA * HEAD_DIM
KV_W = N_KV * HEAD_DIM
ROPE_THETA = 10000.0
NORM_EPS = 1e-6
ATT_SCALE = HEAD_DIM ** -0.5
FG_PAD = 512

ADAM_LR = 0.001
ADAM_B1 = 0.9
ADAM_B2 = 0.999
ADAM_EPS = 1e-08
ADAM_WD = 0.01
ADAM_STEP = 10

VMEM_LIMIT = 56 * 1024 * 1024

NT_DIMS = (((1,), (1,)), ((), ()))
TN_DIMS = (((0,), (0,)), ((), ()))
NN_DIMS = (((1,), (0,)), ((), ()))


def _pcall(body, *, name, out_shape, grid=(), in_specs=None, out_specs=None, scratch_shapes=(), grid_spec=None):
    params = pltpu.CompilerParams(vmem_limit_bytes=VMEM_LIMIT)
    if grid_spec is not None:
        return pl.pallas_call(body, name=name, out_shape=out_shape, grid_spec=grid_spec, compiler_params=params)
    return pl.pallas_call(body, name=name, out_shape=out_shape, grid=grid, in_specs=in_specs, out_specs=out_specs,
                          scratch_shapes=scratch_shapes, compiler_params=params)


def _blk(n, pref):
    if n <= pref:
        return n
    b = (pref // LANES) * LANES
    while n % b:
        b -= LANES
    return b


def _position():
    return lax.axis_index("x"), lax.axis_index("y"), lax.axis_index("c")


ANY = pl.BlockSpec(memory_space=pl.ANY)


def _all_gather(arrs, name):
    n = len(arrs)

    def body(*refs):
        ins, outs = refs[:n], refs[n:2 * n]
        send_sems, recv_sems, local_sems = refs[2 * n:]
        x, y, c = _position()
        me, sibling = (x, y, c), (x, y, 1 - c)
        chips = [(1 - x, y), (x, 1 - y), (1 - x, 1 - y)]

        def slot(p):
            return 4 * p[0] + 2 * p[1] + p[2]

        def copy(a, k, block, to, src=None):
            dst = outs[a].at[slot(block)]
            return pltpu.make_async_remote_copy(
                src_ref=dst if src is None else src, dst_ref=dst,
                send_sem=send_sems.at[7 * a + k], recv_sem=recv_sems.at[7 * a + k],
                device_id=to, device_id_type=MESH)

        mine = [pltpu.make_async_copy(ins[a], outs[a].at[slot(me)], local_sems.at[a]) for a in range(n)]
        for cp in mine:
            cp.start()
        first = []
        for a in range(n):
            first.append(copy(a, 0, me, sibling, src=ins[a]))
            first += [copy(a, 1 + j, me, (*chip, c), src=ins[a]) for j, chip in enumerate(chips)]
        for cp in first:
            cp.start()
        passed = []
        for a in range(n):
            for j, chip in enumerate(chips):
                copy(a, 1 + j, (*chip, c), me).wait_recv()
                cp = copy(a, 4 + j, (*chip, c), sibling)
                cp.start()
                passed.append(cp)
        for a in range(n):
            copy(a, 0, sibling, me).wait_recv()
            for j, chip in enumerate(chips):
                copy(a, 4 + j, (*chip, 1 - c), me).wait_recv()
        for cp in first + passed:
            cp.wait_send()
        for cp in mine:
            cp.wait()

    return _pcall(
        body, name=name,
        out_shape=[jax.ShapeDtypeStruct((N_DEV,) + a.shape, a.dtype) for a in arrs],
        in_specs=[ANY] * n, out_specs=[ANY] * n,
        scratch_shapes=[pltpu.SemaphoreType.DMA((7 * n,)), pltpu.SemaphoreType.DMA((7 * n,)),
                        pltpu.SemaphoreType.DMA((n,))],
    )(*arrs)


HBM = pl.BlockSpec(memory_space=pltpu.HBM)
SEM = pl.BlockSpec(memory_space=pltpu.SEMAPHORE)
EFFECT = pltpu.SideEffectType.DATAFLOW_SIDE_EFFECTING


def _hbm(a):
    return pltpu.with_memory_space_constraint(a, pltpu.HBM)


def _gather_peers():
    x, y, c = _position()
    return [(x, y, 1 - c), (1 - x, y, c), (x, 1 - y, c), (1 - x, 1 - y, c)]


def _ag_start(shards, name):
    n = len(shards)
    lands = [_hbm(lax.empty((N_DEV,) + s.shape, s.dtype)) for s in shards]

    def body(*refs):
        srcs, land, send, recv = refs[:n], refs[n:2 * n], refs[2 * n:3 * n], refs[3 * n:4 * n]
        token = refs[6 * n]
        x, y, c = _position()
        for a in range(n):
            for k, to in enumerate(_gather_peers()):
                pltpu.make_async_remote_copy(
                    src_ref=srcs[a], dst_ref=land[a].at[4 * x + 2 * y + c], send_sem=send[a].at[k],
                    recv_sem=recv[a].at[k], device_id=to, device_id_type=MESH).start()
        token[...] = jnp.zeros_like(token)

    sems = [pltpu.SemaphoreType.DMA((4,))] * (2 * n)
    out = pl.pallas_call(
        body, name=name,
        out_shape=sems + [pltpu.HBM(s.shape, s.dtype) for s in shards] + [pltpu.HBM(l.shape, l.dtype) for l in lands]
        + [jax.ShapeDtypeStruct((8, LANES), F32)],
        in_specs=[HBM] * (2 * n), out_specs=[SEM] * (2 * n) + [HBM] * (2 * n) + [pl.BlockSpec(memory_space=pltpu.VMEM)],
        input_output_aliases={**{a: 2 * n + a for a in range(n)}, **{n + a: 3 * n + a for a in range(n)}},
        compiler_params=pltpu.CompilerParams(has_side_effects=EFFECT),
    )(*[_hbm(s) for s in shards], *lands)
    return out[:n], out[n:2 * n], out[2 * n:3 * n], out[3 * n:4 * n], out[4 * n]


def _ag_wait(send, recv, shard_thru, land_thru, after, name):
    def body(v_ref, land_ref, send_sem, recv_sem, after_ref, v_dead, got_ref):
        for k, to in enumerate(_gather_peers()):
            cp = pltpu.make_async_remote_copy(
                src_ref=v_ref, dst_ref=land_ref.at[0], send_sem=send_sem.at[k], recv_sem=recv_sem.at[k],
                device_id=to, device_id_type=MESH)
            cp.wait_send()
            cp.wait_recv()

    return pl.pallas_call(
        body, name=name,
        out_shape=(pltpu.HBM(shard_thru.shape, shard_thru.dtype), pltpu.HBM(land_thru.shape, land_thru.dtype)),
        in_specs=(HBM, HBM, SEM, SEM, ANY), out_specs=(HBM, HBM), input_output_aliases={0: 0, 1: 1},
        compiler_params=pltpu.CompilerParams(has_side_effects=EFFECT),
    )(shard_thru, land_thru, send, recv, after)


def _ag_forward(land, name):
    def body(land_in, land_ref, send_sems, recv_sems):
        x, y, c = _position()
        copies = []
        for j, (px, py) in enumerate([(1 - x, y), (x, 1 - y), (1 - x, 1 - y)]):
            block = land_ref.at[4 * px + 2 * py + c]
            cp = pltpu.make_async_remote_copy(src_ref=block, dst_ref=block, send_sem=send_sems.at[j],
                                              recv_sem=recv_sems.at[j], device_id=(x, y, 1 - c), device_id_type=MESH)
            cp.start()
            copies.append(cp)
        for cp in copies:
            cp.wait()

    return pl.pallas_call(
        body, name=name, out_shape=jax.ShapeDtypeStruct(land.shape, land.dtype),
        in_specs=[ANY], out_specs=ANY, input_output_aliases={0: 0},
        scratch_shapes=[pltpu.SemaphoreType.DMA((3,)), pltpu.SemaphoreType.DMA((3,))],
    )(land)


def _fwd_start(land, name):
    def body(land_ref, send, recv, land_thru, token):
        x, y, c = _position()
        for j, (px, py) in enumerate([(1 - x, y), (x, 1 - y), (1 - x, 1 - y)]):
            block = land_ref.at[4 * px + 2 * py + c]
            pltpu.make_async_remote_copy(src_ref=block, dst_ref=block, send_sem=send.at[j], recv_sem=recv.at[j],
                                         device_id=(x, y, 1 - c), device_id_type=MESH).start()
        token[...] = jnp.zeros_like(token)

    return pl.pallas_call(
        body, name=name,
        out_shape=[pltpu.SemaphoreType.DMA((3,)), pltpu.SemaphoreType.DMA((3,)), pltpu.HBM(land.shape, land.dtype),
                   jax.ShapeDtypeStruct((8, LANES), F32)],
        in_specs=[HBM], out_specs=[SEM, SEM, HBM, pl.BlockSpec(memory_space=pltpu.VMEM)],
        input_output_aliases={0: 2},
        compiler_params=pltpu.CompilerParams(has_side_effects=EFFECT),
    )(land)


def _fwd_wait(send, recv, land_thru, after, name):
    def body(land_ref, send_sem, recv_sem, after_ref, land_out):
        x, y, c = _position()
        for j in range(3):
            cp = pltpu.make_async_remote_copy(
                src_ref=land_ref.at[0], dst_ref=land_ref.at[0], send_sem=send_sem.at[j], recv_sem=recv_sem.at[j],
                device_id=(x, y, 1 - c), device_id_type=MESH)
            cp.wait_send()
            cp.wait_recv()

    return pl.pallas_call(
        body, name=name, out_shape=pltpu.HBM(land_thru.shape, land_thru.dtype),
        in_specs=(HBM, SEM, SEM, ANY), out_specs=HBM, input_output_aliases={0: 0},
        compiler_params=pltpu.CompilerParams(has_side_effects=EFFECT),
    )(land_thru, send, recv, after)


def _rs_peers():
    x, y, c = _position()
    return [(1 - x, y, c), (x, 1 - y, c), (1 - x, 1 - y, c)]


def _rs_start(chip_sums, name):
    land = _hbm(lax.empty(chip_sums.shape, chip_sums.dtype))

    def body(src, land_ref, send, recv, src_thru, land_thru, token):
        x, y, c = _position()
        for j, (px, py, pc) in enumerate(_rs_peers()):
            pltpu.make_async_remote_copy(
                src_ref=src.at[2 * px + py], dst_ref=land_ref.at[2 * x + y], send_sem=send.at[j], recv_sem=recv.at[j],
                device_id=(px, py, pc), device_id_type=MESH).start()
        token[...] = jnp.zeros_like(token)

    return pl.pallas_call(
        body, name=name,
        out_shape=[pltpu.SemaphoreType.DMA((3,)), pltpu.SemaphoreType.DMA((3,)),
                   pltpu.HBM(chip_sums.shape, chip_sums.dtype), pltpu.HBM(land.shape, land.dtype),
                   jax.ShapeDtypeStruct((8, LANES), F32)],
        in_specs=[HBM, HBM], out_specs=[SEM, SEM, HBM, HBM, pl.BlockSpec(memory_space=pltpu.VMEM)],
        input_output_aliases={0: 2, 1: 3},
        compiler_params=pltpu.CompilerParams(has_side_effects=EFFECT),
    )(_hbm(chip_sums), land)


def _rs_wait(send, recv, src_thru, land_thru, after, name):
    def body(src, land_ref, send_sem, recv_sem, after_ref, src_out, land_out):
        for j, to in enumerate(_rs_peers()):
            cp = pltpu.make_async_remote_copy(
                src_ref=src.at[0], dst_ref=land_ref.at[0], send_sem=send_sem.at[j], recv_sem=recv_sem.at[j],
                device_id=to, device_id_type=MESH)
            cp.wait_send()
            cp.wait_recv()

    return pl.pallas_call(
        body, name=name,
        out_shape=(pltpu.HBM(src_thru.shape, src_thru.dtype), pltpu.HBM(land_thru.shape, land_thru.dtype)),
        in_specs=(HBM, HBM, SEM, SEM, ANY), out_specs=(HBM, HBM), input_output_aliases={0: 0, 1: 1},
        compiler_params=pltpu.CompilerParams(has_side_effects=EFFECT),
    )(src_thru, land_thru, send, recv, after)


def _sib_start(full, name):
    land = _hbm(lax.empty((N_CHIP,) + full.shape[1:], full.dtype))

    def body(src, land_ref, send, recv, src_thru, land_thru, token):
        x, y, c = _position()
        for k in range(N_CHIP):
            pltpu.make_async_remote_copy(
                src_ref=src.at[2 * k + (1 - c)], dst_ref=land_ref.at[k], send_sem=send.at[k], recv_sem=recv.at[k],
                device_id=(x, y, 1 - c), device_id_type=MESH).start()
        token[...] = jnp.zeros_like(token)

    return pl.pallas_call(
        body, name=name,
        out_shape=[pltpu.SemaphoreType.DMA((N_CHIP,)), pltpu.SemaphoreType.DMA((N_CHIP,)),
                   pltpu.HBM(full.shape, full.dtype), pltpu.HBM(land.shape, land.dtype),
                   jax.ShapeDtypeStruct((8, LANES), F32)],
        in_specs=[HBM, HBM], out_specs=[SEM, SEM, HBM, HBM, pl.BlockSpec(memory_space=pltpu.VMEM)],
        input_output_aliases={0: 2, 1: 3},
        compiler_params=pltpu.CompilerParams(has_side_effects=EFFECT),
    )(_hbm(full), land)


def _sib_wait(send, recv, src_thru, land_thru, after, name):
    def body(src, land_ref, send_sem, recv_sem, after_ref, src_out, land_out):
        x, y, c = _position()
        for k in range(N_CHIP):
            cp = pltpu.make_async_remote_copy(
                src_ref=src.at[0], dst_ref=land_ref.at[0], send_sem=send_sem.at[k], recv_sem=recv_sem.at[k],
                device_id=(x, y, 1 - c), device_id_type=MESH)
            cp.wait_send()
            cp.wait_recv()

    return pl.pallas_call(
        body, name=name,
        out_shape=(pltpu.HBM(src_thru.shape, src_thru.dtype), pltpu.HBM(land_thru.shape, land_thru.dtype)),
        in_specs=(HBM, HBM, SEM, SEM, ANY), out_specs=(HBM, HBM), input_output_aliases={0: 0, 1: 1},
        compiler_params=pltpu.CompilerParams(has_side_effects=EFFECT),
    )(src_thru, land_thru, send, recv, after)


def _rs_sibling(arrs, name):
    n = len(arrs)

    def body(*refs):
        ins, outs = refs[:n], refs[n:2 * n]
        send_sems, recv_sems = refs[2 * n:]
        x, y, c = _position()
        copies = []
        for a in range(n):
            for k in range(N_CHIP):
                cp = pltpu.make_async_remote_copy(
                    src_ref=ins[a].at[2 * k + (1 - c)], dst_ref=outs[a].at[k],
                    send_sem=send_sems.at[N_CHIP * a + k], recv_sem=recv_sems.at[N_CHIP * a + k],
                    device_id=(x, y, 1 - c), device_id_type=MESH)
                cp.start()
                copies.append(cp)
        for cp in copies:
            cp.wait()

    return _pcall(
        body, name=name,
        out_shape=[jax.ShapeDtypeStruct((N_CHIP,) + a.shape[1:], a.dtype) for a in arrs],
        in_specs=[ANY] * n, out_specs=[ANY] * n,
        scratch_shapes=[pltpu.SemaphoreType.DMA((N_CHIP * n,)), pltpu.SemaphoreType.DMA((N_CHIP * n,))],
    )(*arrs)


def _chip_sum(full, recv, core, name):
    _, rows, cols = full.shape
    tr = _blk(rows, 1024) if rows % LANES == 0 else rows

    def body(core_ref, a_ref, b_ref, o_ref):
        o_ref[...] = (a_ref[...].astype(F32) + b_ref[...].astype(F32)).astype(o_ref.dtype)

    grid_spec = pltpu.PrefetchScalarGridSpec(
        num_scalar_prefetch=1, grid=(N_CHIP, rows // tr),
        in_specs=[pl.BlockSpec((None, tr, cols), lambda k, i, core_ref: (2 * k + core_ref[0], i, 0)),
                  pl.BlockSpec((None, tr, cols), lambda k, i, core_ref: (k, i, 0))],
        out_specs=pl.BlockSpec((None, tr, cols), lambda k, i, core_ref: (k, i, 0)))
    return _pcall(body, name=name, out_shape=jax.ShapeDtypeStruct((N_CHIP, rows, cols), full.dtype),
                  grid_spec=grid_spec)(core, full, recv)


def _matmul(a, b, *, name, ta=False, tb=False, tm=1024, tn=1024, tk=2048, out_dtypes=(BF16,), epilogue=None,
            row_extras=(), tile_extras=(), out_shape=None, out_map=None, b_sharded=False, n_cols=None, n_off=0,
            after=None, b_pair=False):
    m, k = (a.shape[1], a.shape[0]) if ta else a.shape
    if b_sharded:
        shard_c = b.shape[2]
        n, kb = (b.shape[1], N_DEV * shard_c) if tb else (N_DEV * shard_c, b.shape[1])
        tn, tk = (tn, min(tk, shard_c)) if tb else (min(tn, shard_c), tk)
        if b_pair:
            assert tb and tk == shard_c
            tk = 2 * shard_c
    else:
        n, kb = b.shape if tb else (b.shape[1], b.shape[0])
    assert kb == k, (name, kb, k)
    if n_cols is not None:
        n = n_cols
    tm, tn, tk = _blk(m, tm), _blk(n, tn), _blk(k, tk)
    assert n_off % tn == 0
    nk = k // tk
    dims = (((0 if ta else 1,), (1 if tb else 0,)), ((), ()))
    behind = () if after is None else (after,)
    n_row, n_tile, n_out = len(row_extras), len(tile_extras), len(out_dtypes)
    n_b = 2 if b_pair else 1
    first_out = 1 + n_b + n_row + n_tile + len(behind)

    def body(*refs):
        a_ref, b_ref = refs[:2]
        extras = refs[1 + n_b:1 + n_b + n_row + n_tile]
        outs = refs[first_out:first_out + n_out]
        acc_ref = refs[-1]
        jj, kk = pl.program_id(1), pl.program_id(2)
        if b_pair:
            half = tk // 2
            part = (lax.dot_general(a_ref[:, 0:half].astype(BF16), b_ref[...].astype(BF16), dims,
                                    preferred_element_type=F32)
                    + lax.dot_general(a_ref[:, half:tk].astype(BF16), refs[2][...].astype(BF16), dims,
                                      preferred_element_type=F32))
        else:
            part = lax.dot_general(a_ref[...].astype(BF16), b_ref[...].astype(BF16), dims,
                                   preferred_element_type=F32)

        def finish(acc):
            res = (acc,) if epilogue is None else epilogue(acc, jj, *[e[...] for e in extras])
            for o_ref, r in zip(outs, res):
                o_ref[...] = r.astype(o_ref.dtype)

        if nk == 1:
            finish(part)
        else:
            @pl.when(kk == 0)
            def _():
                acc_ref[...] = part

            @pl.when(kk > 0)
            def _():
                acc_ref[...] += part

            @pl.when(kk == nk - 1)
            def _():
                finish(acc_ref[...])

    a_spec = pl.BlockSpec((tk, tm), lambda i, j, kk: (kk, i)) if ta else pl.BlockSpec((tm, tk), lambda i, j, kk: (i, kk))
    if b_pair:
        b_spec = pl.BlockSpec((None, tn, shard_c), lambda i, j, kk: (2 * kk, j, 0))
    elif b_sharded and tb:
        per = shard_c // tk
        b_spec = pl.BlockSpec((None, tn, tk), lambda i, j, kk: (kk // per, j, kk % per))
    elif b_sharded:
        per = shard_c // tn
        b_spec = pl.BlockSpec((None, tk, tn), lambda i, j, kk: (j // per, kk, j % per))
    elif tb:
        b_spec = pl.BlockSpec((tn, tk), lambda i, j, kk: (j + n_off // tn, kk))
    else:
        b_spec = pl.BlockSpec((tk, tn), lambda i, j, kk: (kk, j + n_off // tn))
    in_specs = [a_spec, b_spec]
    if b_pair:
        in_specs.append(pl.BlockSpec((None, tn, shard_c), lambda i, j, kk: (2 * kk + 1, j, 0)))
    in_specs += [pl.BlockSpec((tm, LANES), lambda i, j, kk: (i, 0)) for _ in row_extras]
    in_specs += [pl.BlockSpec((tm, tn), lambda i, j, kk: (i, j)) for _ in tile_extras]
    in_specs += [ANY for _ in behind]
    if out_map is None:
        out_specs = [pl.BlockSpec((tm, tn), lambda i, j, kk: (i, j)) for _ in out_dtypes]
        shapes = [jax.ShapeDtypeStruct((m, n), dt) for dt in out_dtypes]
    else:
        out_specs = [out_map(tm, tn)]
        shapes = [jax.ShapeDtypeStruct(out_shape, out_dtypes[0])]
    acc_shape = (tm, tn) if nk > 1 else (8, LANES)
    res = _pcall(body, name=name, out_shape=shapes, grid=(m // tm, n // tn, nk), in_specs=in_specs,
                 out_specs=out_specs, scratch_shapes=[pltpu.VMEM(acc_shape, F32)])(
                     a, *([b, b] if b_pair else [b]), *row_extras, *tile_extras, *behind)
    return res[0] if n_out == 1 else res


def _rope_cols(acc, j, cos, sin, n_rope):
    width = acc.shape[1]
    parts = []
    for g in range(width // HEAD_DIM):
        xg = acc[:, g * HEAD_DIM:(g + 1) * HEAD_DIM]
        roped = xg * cos + pltpu.roll(xg, HEAD_DIM // 2, 1) * sin
        parts.append(jnp.where(j * width + g * HEAD_DIM < n_rope, roped, xg))
    return jnp.concatenate(parts, axis=1) if len(parts) > 1 else parts[0]


def _silu(v):
    return v / (1.0 + jnp.exp(-v))


def _mod_part(c_all, w_mod, b_part):
    d, w = w_mod.shape
    tk = _blk(d, 512)

    def body(c_ref, w_ref, b_ref, o_ref):
        kk = pl.program_id(0)
        cond = _silu(c_ref[...]).astype(BF16)
        part = jnp.dot(cond, w_ref[...].astype(BF16), preferred_element_type=F32)

        @pl.when(kk == 0)
        def _():
            o_ref[...] = part + b_ref[...]

        @pl.when(kk > 0)
        def _():
            o_ref[...] += part

    return _pcall(body, name="mod_part", out_shape=jax.ShapeDtypeStruct((N_DEV, w), F32), grid=(d // tk,),
                  in_specs=[pl.BlockSpec((N_DEV, tk), lambda kk: (0, kk)), pl.BlockSpec((tk, w), lambda kk: (kk, 0)),
                            pl.BlockSpec((1, w), lambda kk: (0, 0))],
                  out_specs=pl.BlockSpec((N_DEV, w), lambda kk: (0, 0)))(c_all, w_mod, b_part)


ROW_CHUNK = 16


def _row_chunks(n_rows, step):
    def one(i, carry):
        step(pl.ds(pl.multiple_of(i * ROW_CHUNK, ROW_CHUNK), ROW_CHUNK))
        return carry

    lax.fori_loop(0, n_rows // ROW_CHUNK, one, 0)


def _row_call(body, name, t, d, tiled_in, vec_in, tiled_out_dtypes, n_vec_out, tr=256, n_acc=0):
    tr = _blk(t, tr)
    tile = pl.BlockSpec((tr, d), lambda i: (i, 0))
    vec = pl.BlockSpec((1, d), lambda i: (0, 0))
    out_shape = [jax.ShapeDtypeStruct((t, d), dt) for dt in tiled_out_dtypes]
    out_shape += [jax.ShapeDtypeStruct((1, d), F32)] * n_vec_out
    return _pcall(body, name=name, out_shape=out_shape, grid=(t // tr,),
                  in_specs=[tile] * len(tiled_in) + [vec] * len(vec_in),
                  out_specs=[tile] * len(tiled_out_dtypes) + [vec] * n_vec_out,
                  scratch_shapes=[pltpu.VMEM((n_acc, ROW_CHUNK, d), F32)] if n_acc else ())(*tiled_in, *vec_in)


def _accumulate(ref, val):
    @pl.when(pl.program_id(0) == 0)
    def _():
        ref[...] = val

    @pl.when(pl.program_id(0) > 0)
    def _():
        ref[...] += val


def _rsum(v):
    return jnp.sum(v, axis=0, keepdims=True)


def _rms(v):
    return lax.rsqrt(jnp.mean(v * v, axis=-1, keepdims=True) + NORM_EPS)


def _rms_bwd(vhat, r, dvhat):
    return r * (dvhat - vhat * jnp.mean(dvhat * vhat, axis=-1, keepdims=True))


def _pre_attn(x, g0, sc_a, sh_a):
    def body(x_ref, g_ref, sc_ref, sh_ref, h_ref):
        xv = x_ref[...]
        h_ref[...] = (xv * _rms(xv) * g_ref[...] * (1.0 + sc_ref[...]) + sh_ref[...]).astype(BF16)

    t, d = x.shape
    return _row_call(body, "pre_attn", t, d, [x], [g0, sc_a, sh_a], [BF16], 0)[0]


def _post_mix(x, mix, gt_a, g1, g2, sc_m, sh_m):
    def body(x_ref, mix_ref, gt_ref, g1_ref, g2_ref, sc_ref, sh_ref, x2_ref, h2_ref):
        mv = mix_ref[...]
        x2 = x_ref[...] + gt_ref[...] * (mv * _rms(mv) * g1_ref[...])
        x2_ref[...] = x2
        h2_ref[...] = (x2 * _rms(x2) * g2_ref[...] * (1.0 + sc_ref[...]) + sh_ref[...]).astype(BF16)

    t, d = x.shape
    return _row_call(body, "post_mix", t, d, [x, mix], [gt_a, g1, g2, sc_m, sh_m], [F32, BF16], 0)


def _final(y, x2, target, gt_m, g3):
    t, d = y.shape

    def body(y_ref, x2_ref, tg_ref, gt_ref, g3_ref, dy_ref, dout_ref, dgt_ref, dg3_ref, loss_ref, acc_ref):
        gt, g3 = gt_ref[...], g3_ref[...]
        acc_ref[...] = jnp.zeros_like(acc_ref)

        def step(rows):
            yv = y_ref[rows, :]
            r = _rms(yv)
            yhat = yv * r
            n3 = yhat * g3
            err = x2_ref[rows, :] + gt * n3 - tg_ref[rows, :]
            acc_ref[0] += err * err
            dout = err * (1.0 / d)
            dout_ref[rows, :] = dout
            acc_ref[1] += dout * n3
            dn3 = dout * gt
            acc_ref[2] += dn3 * yhat
            dy_ref[rows, :] = _rms_bwd(yhat, r, dn3 * g3).astype(BF16)

        _row_chunks(y_ref.shape[0], step)
        _accumulate(loss_ref, jnp.zeros((1, d), F32) + 0.5 * jnp.sum(acc_ref[0]) / d)
        _accumulate(dgt_ref, _rsum(acc_ref[1]))
        _accumulate(dg3_ref, _rsum(acc_ref[2]))

    return _row_call(body, "final", t, d, [y, x2, target], [gt_m, g3], [BF16, F32], 3, n_acc=3)


def _mid_bwd(dh2, dout, x2, mix, g2, sc_m, gt_a, g1):
    t, d = x2.shape

    def body(dh2_ref, dout_ref, x2_ref, mix_ref, g2_ref, sc_ref, gt_ref, g1_ref,
             dmix_ref, dx2_ref, dsh_ref, dsc_ref, dg2_ref, dgt_ref, dg1_ref, acc_ref):
        g2, sc, gt, g1 = g2_ref[...], sc_ref[...], gt_ref[...], g1_ref[...]
        acc_ref[...] = jnp.zeros_like(acc_ref)

        def step(rows):
            dh2v = dh2_ref[rows, :].astype(F32)
            x2v = x2_ref[rows, :]
            r2 = _rms(x2v)
            x2hat = x2v * r2
            acc_ref[0] += dh2v
            acc_ref[1] += dh2v * (x2hat * g2)
            dn2 = dh2v * (1.0 + sc)
            acc_ref[2] += dn2 * x2hat
            dx2 = dout_ref[rows, :] + _rms_bwd(x2hat, r2, dn2 * g2)
            dx2_ref[rows, :] = dx2
            mv = mix_ref[rows, :]
            r1 = _rms(mv)
            mhat = mv * r1
            acc_ref[3] += dx2 * (mhat * g1)
            dn1 = dx2 * gt
            acc_ref[4] += dn1 * mhat
            dmix_ref[rows, :] = _rms_bwd(mhat, r1, dn1 * g1).astype(BF16)

        _row_chunks(x2_ref.shape[0], step)
        for k, ref in enumerate((dsh_ref, dsc_ref, dg2_ref, dgt_ref, dg1_ref)):
            _accumulate(ref, _rsum(acc_ref[k]))

    return _row_call(body, "mid_bwd", t, d, [dh2, dout, x2, mix], [g2, sc_m, gt_a, g1], [BF16, F32], 5, n_acc=5)


def _x_bwd(dh1, dx2, x, g0, sc_a):
    t, d = x.shape

    def body(dh1_ref, dx2_ref, x_ref, g0_ref, sc_ref, dx_ref, dsh_ref, dsc_ref, dg0_ref, acc_ref):
        g0, sc = g0_ref[...], sc_ref[...]
        acc_ref[...] = jnp.zeros_like(acc_ref)

        def step(rows):
            dh1v = dh1_ref[rows, :].astype(F32)
            xv = x_ref[rows, :]
            r0 = _rms(xv)
            xhat = xv * r0
            acc_ref[0] += dh1v
            acc_ref[1] += dh1v * (xhat * g0)
            dn0 = dh1v * (1.0 + sc)
            acc_ref[2] += dn0 * xhat
            dx_ref[rows, :] = dx2_ref[rows, :] + _rms_bwd(xhat, r0, dn0 * g0)

        _row_chunks(x_ref.shape[0], step)
        for k, ref in enumerate((dsh_ref, dsc_ref, dg0_ref)):
            _accumulate(ref, _rsum(acc_ref[k]))

    return _row_call(body, "x_bwd", t, d, [dh1, dx2, x], [g0, sc_a], [F32], 3, n_acc=3)


def _pick_lane(block, h):
    lane = lax.broadcasted_iota(jnp.int32, block.shape, 1)
    return jnp.sum(jnp.where(lane == h, block, 0.0), axis=1, keepdims=True)


def _put_lane(ref, rows, h, col):
    old = ref[rows, :]
    lane = lax.broadcasted_iota(jnp.int32, old.shape, 1)
    ref[rows, :] = jnp.where(lane == h, col, old)


def _tri(n, lower):
    r = lax.broadcasted_iota(jnp.int32, (n, n), 0)
    c = lax.broadcasted_iota(jnp.int32, (n, n), 1)
    return jnp.where((c <= r) if lower else (c >= r), 1.0, 0.0).astype(F32)


def _cum_fwd(fg, b128):
    t = fg.shape[0]
    nb = t // LANES

    def body(fg_ref, b_ref, cum_ref, cumt_ref):
        tri = _tri(LANES, True)
        carry = jnp.zeros((1, LANES), F32)
        for i in range(nb):
            z = fg_ref[i * LANES:(i + 1) * LANES, :] + b_ref[...]
            lf = jnp.minimum(z, 0.0) - jnp.log(1.0 + jnp.exp(-jnp.abs(z)))
            blk = jnp.dot(tri, lf, precision=lax.Precision.HIGHEST, preferred_element_type=F32) + carry
            cum_ref[i * LANES:(i + 1) * LANES, :] = blk
            carry = blk[LANES - 1:LANES, :]
        cumt_ref[...] = cum_ref[...].T[0:N_FOX, :]

    return _pcall(body, name="cum_fwd",
                  out_shape=[jax.ShapeDtypeStruct((t, LANES), F32), jax.ShapeDtypeStruct((N_FOX, t), F32)],
                  grid=(1,),
                  in_specs=[pl.BlockSpec((t, LANES), lambda i: (0, 0)), pl.BlockSpec((1, LANES), lambda i: (0, 0))],
                  out_specs=[pl.BlockSpec((t, LANES), lambda i: (0, 0)), pl.BlockSpec((N_FOX, t), lambda i: (0, 0))],
                  )(fg, b128)


def _fg_bwd(dcs_rows, fg, b128):
    t = fg.shape[0]
    nb = t // LANES

    def body(dcs_ref, fg_ref, b_ref, dfg_ref, db_ref, dcum_ref):
        dcum_ref[...] = -jnp.concatenate([dcs_ref[...], jnp.zeros((LANES - N_FOX, t), F32)], axis=0).T
        tri = _tri(LANES, False)
        carry = jnp.zeros((1, LANES), F32)
        db = jnp.zeros((1, LANES), F32)
        for i in reversed(range(nb)):
            rows = slice(i * LANES, (i + 1) * LANES)
            dlf = jnp.dot(tri, dcum_ref[rows, :], precision=lax.Precision.HIGHEST, preferred_element_type=F32) + carry
            carry = dlf[0:1, :]
            z = fg_ref[rows, :] + b_ref[...]
            dfg = dlf / (1.0 + jnp.exp(z))
            dfg_ref[rows, :] = dfg.astype(BF16)
            db = db + _rsum(dfg)
        db_ref[...] = db

    full = pl.BlockSpec((t, LANES), lambda i: (0, 0))
    vec = pl.BlockSpec((1, LANES), lambda i: (0, 0))
    return _pcall(body, name="fg_bwd",
                  out_shape=[jax.ShapeDtypeStruct((t, LANES), BF16), jax.ShapeDtypeStruct((1, LANES), F32)],
                  grid=(1,), in_specs=[pl.BlockSpec((N_FOX, t), lambda i: (0, 0)), full, vec], out_specs=[full, vec],
                  scratch_shapes=[pltpu.VMEM((t, LANES), F32)])(dcs_rows, fg, b128)


def _head_spec(t, col0, div=1):
    return pl.BlockSpec((t, HEAD_DIM), lambda h: (0, col0 + h // div))


def _fox_scores(q, k, cq, ck, i, tq, end):
    s = lax.dot_general(q, k, NT_DIMS, preferred_element_type=F32) * ATT_SCALE + cq - ck
    row = lax.broadcasted_iota(jnp.int32, (tq, end), 0) + i * tq
    col = lax.broadcasted_iota(jnp.int32, (tq, end), 1)
    return jnp.where(row >= col, s, -jnp.inf)


def _fox_fwd(proj_a, cum, cumt):
    t = proj_a.shape[0]
    tq = _blk(t, 512)
    nq = t // tq

    def body(q_ref, k_ref, v_ref, cum_ref, cumt_ref, o_ref, lse_ref):
        h = pl.program_id(0)
        cq_all = _pick_lane(cum_ref[...], h)
        ck_all = cumt_ref[pl.ds(h, 1), :]

        @pl.when(h == 0)
        def _():
            lse_ref[...] = jnp.zeros_like(lse_ref)

        for i in range(nq):
            rows, end = slice(i * tq, (i + 1) * tq), (i + 1) * tq
            s = _fox_scores(q_ref[rows, :], k_ref[0:end, :], cq_all[rows, :], ck_all[:, 0:end], i, tq, end)
            m = jnp.max(s, axis=1, keepdims=True)
            p = jnp.exp(s - m)
            l = jnp.sum(p, axis=1, keepdims=True)
            o = jnp.dot(p.astype(BF16), v_ref[0:end, :], preferred_element_type=F32) / l
            o_ref[rows, :] = o.astype(BF16)
            _put_lane(lse_ref, rows, h, m + jnp.log(l))

    nh = FOX_W // HEAD_DIM
    stat = pl.BlockSpec((t, LANES), lambda h: (0, 0))
    return _pcall(body, name="fox_fwd",
                  out_shape=[jax.ShapeDtypeStruct((t, FOX_W), BF16), jax.ShapeDtypeStruct((t, LANES), F32)],
                  grid=(N_FOX,),
                  in_specs=[_head_spec(t, 0), _head_spec(t, nh), _head_spec(t, 2 * nh), stat,
                            pl.BlockSpec((N_FOX, t), lambda h: (0, 0))],
                  out_specs=[_head_spec(t, 0), stat])(proj_a, proj_a, proj_a, cum, cumt)


def _fox_bwd(proj_a, d_attn, cum, cumt, lse):
    t = proj_a.shape[0]
    tq = _blk(t, 512)
    nq = t // tq

    def body(q_ref, k_ref, v_ref, do_ref, cum_ref, cumt_ref, lse_ref,
             dq_ref, dk_ref, dv_ref, dcs_ref, dk_acc, dv_acc, dcs_acc):
        h = pl.program_id(0)
        cq_all = _pick_lane(cum_ref[...], h)
        ck_all = cumt_ref[pl.ds(h, 1), :]
        lse_all = _pick_lane(lse_ref[...], h)
        dk_acc[...] = jnp.zeros_like(dk_acc)
        dv_acc[...] = jnp.zeros_like(dv_acc)
        dcs_acc[...] = jnp.zeros_like(dcs_acc)
        for i in range(nq):
            rows, end = slice(i * tq, (i + 1) * tq), (i + 1) * tq
            q, k, v, do = q_ref[rows, :], k_ref[0:end, :], v_ref[0:end, :], do_ref[rows, :]
            s = _fox_scores(q, k, cq_all[rows, :], ck_all[:, 0:end], i, tq, end)
            p = jnp.exp(s - lse_all[rows, :])
            dp = lax.dot_general(do, v, NT_DIMS, preferred_element_type=F32)
            ds = p * (dp - jnp.sum(p * dp, axis=1, keepdims=True))
            dcs_acc[:, 0:end] += jnp.sum(ds, axis=0, keepdims=True)
            ds = ds.astype(BF16)
            dq_ref[rows, :] = (jnp.dot(ds, k, preferred_element_type=F32) * ATT_SCALE).astype(BF16)
            dk_acc[0:end, :] += lax.dot_general(ds, q, TN_DIMS, preferred_element_type=F32)
            dv_acc[0:end, :] += lax.dot_general(p.astype(BF16), do, TN_DIMS, preferred_element_type=F32)
        dk_ref[...] = (dk_acc[...] * ATT_SCALE).astype(BF16)
        dv_ref[...] = dv_acc[...].astype(BF16)
        dcs_ref[pl.ds(h, 1), :] = dcs_acc[...]

    nh = FOX_W // HEAD_DIM
    stat = pl.BlockSpec((t, LANES), lambda h: (0, 0))
    rows8 = pl.BlockSpec((N_FOX, t), lambda h: (0, 0))
    head = _head_spec(t, 0)
    wide = jax.ShapeDtypeStruct((t, FOX_W), BF16)
    return _pcall(body, name="fox_bwd",
                  out_shape=[wide, wide, wide, jax.ShapeDtypeStruct((N_FOX, t), F32)],
                  grid=(N_FOX,),
                  in_specs=[_head_spec(t, 0), _head_spec(t, nh), _head_spec(t, 2 * nh), head, stat, rows8, stat],
                  out_specs=[head, head, head, rows8],
                  scratch_shapes=[pltpu.VMEM((t, HEAD_DIM), F32), pltpu.VMEM((t, HEAD_DIM), F32),
                                  pltpu.VMEM((1, t), F32)],
                  )(proj_a, proj_a, proj_a, d_attn, cum, cumt, lse)


def _swa_scores(q, k, i, tq, start, end):
    s = lax.dot_general(q, k, NT_DIMS, preferred_element_type=F32) * ATT_SCALE
    row = lax.broadcasted_iota(jnp.int32, (tq, end - start), 0) + i * tq
    col = lax.broadcasted_iota(jnp.int32, (tq, end - start), 1) + start
    diff = row - col
    return jnp.where((diff >= 0) & (diff < WINDOW), s, -jnp.inf)


def _swa_blocks(t):
    tq = _blk(t, 256)
    return tq, [(i, max(0, i * tq - WINDOW), (i + 1) * tq) for i in range(t // tq)]


def _swa_fwd(proj_b, sinks128):
    t = proj_b.shape[0]
    tq, blocks = _swa_blocks(t)

    def body(q_ref, k_ref, v_ref, sink_ref, o_ref, lse_ref):
        h = pl.program_id(0)
        sink = _pick_lane(sink_ref[...], h)

        @pl.when(h == 0)
        def _():
            lse_ref[...] = jnp.zeros_like(lse_ref)

        for i, start, end in blocks:
            rows = slice(i * tq, end)
            s = _swa_scores(q_ref[rows, :], k_ref[start:end, :], i, tq, start, end)
            m = jnp.maximum(jnp.max(s, axis=1, keepdims=True), sink)
            p = jnp.exp(s - m)
            l = jnp.sum(p, axis=1, keepdims=True) + jnp.exp(sink - m)
            o = jnp.dot(p.astype(BF16), v_ref[start:end, :], preferred_element_type=F32) / l
            o_ref[rows, :] = o.astype(BF16)
            _put_lane(lse_ref, rows, h, m + jnp.log(l))

    stat = pl.BlockSpec((t, LANES), lambda h: (0, 0))
    return _pcall(body, name="swa_fwd",
                  out_shape=[jax.ShapeDtypeStruct((t, SWA_W), BF16), jax.ShapeDtypeStruct((t, LANES), F32)],
                  grid=(N_SWA,),
                  in_specs=[_head_spec(t, 0), _head_spec(t, N_SWA, GQA), _head_spec(t, N_SWA + N_KV, GQA),
                            pl.BlockSpec((1, LANES), lambda h: (0, 0))],
                  out_specs=[_head_spec(t, 0), stat])(proj_b, proj_b, proj_b, sinks128)


def _rope_bwd(d, cos, sin):
    return d * cos + pltpu.roll(d * sin, HEAD_DIM // 2, 1)


def _swa_bwd(proj_b, d_attn, lse, sinks128, cos, sin):
    t = proj_b.shape[0]
    tq, blocks = _swa_blocks(t)

    def body(q_ref, k_ref, v_ref, do_ref, lse_ref, sink_ref, cos_ref, sin_ref,
             dq_ref, dk_ref, dv_ref, dsink_ref, dk_acc, dv_acc):
        h = pl.program_id(0)
        sink = _pick_lane(sink_ref[...], h)
        lse_all = _pick_lane(lse_ref[...], h)

        @pl.when(h == 0)
        def _():
            dsink_ref[...] = jnp.zeros_like(dsink_ref)

        @pl.when(h % GQA == 0)
        def _():
            dk_acc[...] = jnp.zeros_like(dk_acc)
            dv_acc[...] = jnp.zeros_like(dv_acc)

        dsink = jnp.zeros((1, 1), F32)
        for i, start, end in blocks:
            rows = slice(i * tq, end)
            q, k, v, do = q_ref[rows, :], k_ref[start:end, :], v_ref[start:end, :], do_ref[rows, :]
            s = _swa_scores(q, k, i, tq, start, end)
            p = jnp.exp(s - lse_all[rows, :])
            dp = lax.dot_general(do, v, NT_DIMS, preferred_element_type=F32)
            delta = jnp.sum(p * dp, axis=1, keepdims=True)
            ds = (p * (dp - delta)).astype(BF16)
            dq = jnp.dot(ds, k, preferred_element_type=F32) * ATT_SCALE
            dq_ref[rows, :] = _rope_bwd(dq, cos_ref[rows, :], sin_ref[rows, :]).astype(BF16)
            dk_acc[start:end, :] += lax.dot_general(ds, q, TN_DIMS, preferred_element_type=F32)
            dv_acc[start:end, :] += lax.dot_general(p.astype(BF16), do, TN_DIMS, preferred_element_type=F32)
            dsink = dsink - jnp.sum(jnp.exp(sink - lse_all[rows, :]) * delta, axis=0, keepdims=True)
        old = dsink_ref[...]
        lane = lax.broadcasted_iota(jnp.int32, old.shape, 1)
        dsink_ref[...] = jnp.where(lane == h, dsink, old)

        @pl.when(h % GQA == GQA - 1)
        def _():
            dk_ref[...] = _rope_bwd(dk_acc[...] * ATT_SCALE, cos_ref[...], sin_ref[...]).astype(BF16)
            dv_ref[...] = dv_acc[...].astype(BF16)

    stat = pl.BlockSpec((t, LANES), lambda h: (0, 0))
    vec = pl.BlockSpec((1, LANES), lambda h: (0, 0))
    head = _head_spec(t, 0)
    kv_out = _head_spec(t, 0, GQA)
    return _pcall(body, name="swa_bwd",
                  out_shape=[jax.ShapeDtypeStruct((t, SWA_W), BF16), jax.ShapeDtypeStruct((t, KV_W), BF16),
                             jax.ShapeDtypeStruct((t, KV_W), BF16), jax.ShapeDtypeStruct((1, LANES), F32)],
                  grid=(N_SWA,),
                  in_specs=[head, _head_spec(t, N_SWA, GQA), _head_spec(t, N_SWA + N_KV, GQA),
                            _head_spec(t, N_FOX), stat, vec, stat, stat],
                  out_specs=[head, kv_out, kv_out, vec],
                  scratch_shapes=[pltpu.VMEM((t, HEAD_DIM), F32), pltpu.VMEM((t, HEAD_DIM), F32)],
                  )(proj_b, proj_b, proj_b, d_attn, lse, sinks128, cos, sin)


def _adamw(w, g, m, v):
    m = ADAM_B1 * m + (1.0 - ADAM_B1) * g
    v = ADAM_B2 * v + (1.0 - ADAM_B2) * (g * g)
    m_hat = m / (1.0 - ADAM_B1 ** ADAM_STEP)
    v_hat = v / (1.0 - ADAM_B2 ** ADAM_STEP)
    delta = -ADAM_LR * (m_hat / (jnp.sqrt(v_hat) + ADAM_EPS) + ADAM_WD * w)
    return delta, m, v


def _adam_pieces(w, m, v, own, land, idx, name):
    rows, cols = w.shape
    tr, tc = (256, cols) if rows % 256 == 0 else (rows, _blk(cols, 512))

    def body(idx_ref, own_ref, l1_ref, l2_ref, l3_ref, w_ref, m_ref, v_ref, g_ref, d_ref, mo_ref, vo_ref):
        g = own_ref[...].astype(F32) + l1_ref[...].astype(F32) + l2_ref[...].astype(F32) + l3_ref[...].astype(F32)
        g_ref[...] = g
        d_ref[...], mo_ref[...], vo_ref[...] = _adamw(w_ref[...], g, m_ref[...], v_ref[...])

    def piece(p):
        return pl.BlockSpec((None, tr, tc), lambda i, j, idx_ref: (idx_ref[p], i, j))

    tile = pl.BlockSpec((tr, tc), lambda i, j, idx_ref: (i, j))
    out = jax.ShapeDtypeStruct((rows, cols), F32)
    grid_spec = pltpu.PrefetchScalarGridSpec(
        num_scalar_prefetch=1, grid=(rows // tr, cols // tc),
        in_specs=[piece(0), piece(1), piece(2), piece(3), tile, tile, tile], out_specs=[tile] * 4)
    return _pcall(body, name=name, out_shape=[out] * 4, grid_spec=grid_spec)(idx, own, land, land, land, w, m, v)


def _adam_mod(c_all, dmod_cols, w, m, v):
    rows, cols = w.shape
    tr = _blk(rows, 256)

    def body(c_ref, dm_ref, w_ref, m_ref, v_ref, g_ref, d_ref, mo_ref, vo_ref):
        cond = _silu(c_ref[...]).astype(BF16)
        g = lax.dot_general(cond, dm_ref[...].astype(BF16), TN_DIMS, preferred_element_type=F32)
        g_ref[...] = g
        d_ref[...], mo_ref[...], vo_ref[...] = _adamw(w_ref[...], g, m_ref[...], v_ref[...])

    tile = pl.BlockSpec((tr, cols), lambda i: (i, 0))
    out = jax.ShapeDtypeStruct((rows, cols), F32)
    return _pcall(body, name="adam_mod", out_shape=[out] * 4, grid=(rows // tr,),
                  in_specs=[pl.BlockSpec((N_DEV, tr), lambda i: (0, i)), pl.BlockSpec((N_DEV, cols), lambda i: (0, 0)),
                            tile, tile, tile],
                  out_specs=[tile] * 4)(c_all, dmod_cols, w, m, v)


def _adam_small(parts, w, m, v):
    nv = w.shape[1]

    def body(p_ref, w_ref, m_ref, v_ref, g_ref, d_ref, mo_ref, vo_ref):
        g = p_ref[0:1, :]
        for k in range(1, N_DEV):
            g = g + p_ref[k:k + 1, :]
        g_ref[...] = g
        d_ref[...], mo_ref[...], vo_ref[...] = _adamw(w_ref[...], g, m_ref[...], v_ref[...])

    vec = pl.BlockSpec((1, nv), lambda i: (0, 0))
    out = jax.ShapeDtypeStruct((1, nv), F32)
    return _pcall(body, name="adam_small", out_shape=[out] * 4, grid=(1,),
                  in_specs=[pl.BlockSpec((N_DEV, nv), lambda i: (0, 0)), vec, vec, vec],
                  out_specs=[vec] * 4)(parts, w, m, v)


def _pad_lanes(v, width=LANES):
    return jnp.pad(v, ((0, 0), (0, width - v.shape[1])))


def kernel(x, c, w_mod, b_mod, g_pre_mix, g_post_mix, w_in, b_forget, swa_sinks, w_out, g_pre_mlp, g_post_mlp, w_up, w_down, loss_target, m_w_mod, m_b_mod, m_g_pre_mix, m_g_post_mix, m_w_in, m_b_forget, m_swa_sinks, m_w_out, m_g_pre_mlp, m_g_post_mlp, m_w_up, m_w_down, v_w_mod, v_b_mod, v_g_pre_mix, v_g_post_mix, v_w_in, v_b_forget, v_swa_sinks, v_w_out, v_g_pre_mlp, v_g_post_mlp, v_w_up, v_w_down):
    ax, ay, ac = _position()
    me = 4 * ax + 2 * ay + ac
    x, target = x[0], loss_target[0]
    t, d = x.shape
    w_mod, w_in, w_out, w_up, w_down = w_mod[0], w_in[0], w_out[0], w_up[0], w_down[0]
    mod_w = w_mod.shape[1]
    in_w = w_in.shape[1]
    in_total = N_DEV * in_w
    shard_ff = w_up.shape[1]
    n_fox3 = 3 * FOX_W
    n_swa3 = SWA_W + 2 * KV_W
    assert in_total == n_fox3 + N_FOX + n_swa3 and d == FOX_W + SWA_W

    c_all = _all_gather([c], "gather_c")[0].reshape(N_DEV, d)
    b_part = lax.dynamic_slice(b_mod, (0, me * mod_w), (1, mod_w))
    mod_parts = _all_gather([_mod_part(c_all, w_mod, b_part)], "gather_mod")[0]
    mod = lax.dynamic_index_in_dim(mod_parts, me, axis=1, keepdims=False).reshape(1, N_DEV * mod_w)

    w_in_b, mod = lax.optimization_barrier((w_in.astype(BF16), mod))
    first = _ag_start([w_in_b], "ag_start_in")
    behind_first = first[4][0, 0]
    rest = _ag_start([(w + behind_first).astype(BF16) for w in (w_out, w_up, w_down)], "ag_start_rest")
    ag_send, ag_recv, ag_shard, ag_land = [a + b for a, b in zip(first[:4], rest[:4])]
    ag_token = rest[4]

    def gathered(i, after, name):
        shard, land = _ag_wait(ag_send[i], ag_recv[i], ag_shard[i], ag_land[i], after, "ag_wait_" + name)
        return lax.dynamic_update_slice(_ag_forward(land, "ag_fwd_" + name), shard[None], (me, 0, 0))

    def gathered_start(i, after, name):
        shard, land = _ag_wait(ag_send[i], ag_recv[i], ag_shard[i], ag_land[i], after, "ag_wait_" + name)
        return shard, _fwd_start(land, "fwd_start_" + name)

    def gathered_finish(started, after, name):
        shard, (send, recv, land, _) = started
        return lax.dynamic_update_slice(_fwd_wait(send, recv, land, after, "fwd_wait_" + name), shard[None], (me, 0, 0))

    sh_a, sc_a, gt_a, sh_m, sc_m, gt_m = [mod[:, i * d:(i + 1) * d] for i in range(6)]

    half = HEAD_DIM // 2
    inv_freq = 1.0 / (ROPE_THETA ** (jnp.arange(half, dtype=F32) * (2.0 / HEAD_DIM)))
    ang = jnp.arange(t).astype(F32)[:, None] * inv_freq[None, :]
    cos = jnp.concatenate([jnp.cos(ang), jnp.cos(ang)], axis=1)
    sin = jnp.concatenate([-jnp.sin(ang), jnp.sin(ang)], axis=1)

    b128 = _pad_lanes(b_forget)
    sinks128 = _pad_lanes(swa_sinks)

    h1 = _pre_attn(x, g_pre_mix + ag_token[0:1, 0:1], sc_a, sh_a)
    h1, adam_in = lax.optimization_barrier(
        (h1, (jnp.transpose(w_in), jnp.transpose(m_w_in[0]), jnp.transpose(v_w_in[0]))))
    w_in_g = gathered(0, h1, "in")
    o_fg, o_sq = n_fox3, n_fox3 + N_FOX

    def cols(lo, hi):
        parts = []
        for j in range(lo // in_w, (hi - 1) // in_w + 1):
            parts.append(w_in_g[j, :, max(lo - j * in_w, 0):min(hi - j * in_w, in_w)])
        return parts

    w_in_r = jnp.concatenate(cols(0, o_fg) + cols(o_sq, in_total) + cols(o_fg, o_sq)
                             + [jnp.zeros((d, FG_PAD - N_FOX), BF16)], axis=1)
    proj_a = _matmul(h1, w_in_r, name="proj_a", n_cols=n_fox3, n_off=0)
    proj_b = _matmul(h1, w_in_r, name="proj_b", n_cols=n_swa3, n_off=n_fox3, tn=512, row_extras=(cos, sin),
                     epilogue=lambda acc, j, cs, sn: (_rope_cols(acc, j, cs, sn, SWA_W + KV_W),))
    out_started = gathered_start(1, proj_a, "out")
    fg = _matmul(h1, w_in_r, name="proj_fg", n_cols=FG_PAD, n_off=n_fox3 + n_swa3, tn=FG_PAD,
                 out_dtypes=(F32,), after=out_started[1][3])[:, 0:LANES]
    cum, cumt = _cum_fwd(fg, b128)
    fox_o, fox_lse = _fox_fwd(proj_a, cum, cumt)
    up_started = gathered_start(2, fox_o, "up")
    swa_o, swa_lse = _swa_fwd(proj_b, sinks128 + up_started[1][3][0:1, 0:1])
    w_out_full = gathered_finish(out_started, swa_o, "out").reshape(d, d)
    attn = jnp.concatenate([fox_o, swa_o], axis=1)
    mix = _matmul(attn, w_out_full, name="out_proj", out_dtypes=(F32,))
    x2, h2 = _post_mix(x, mix, gt_a, g_post_mix, g_pre_mlp, sc_m, sh_m)
    w_up_g = gathered_finish(up_started, h2, "up")
    u, act = _matmul(h2, w_up_g, name="mlp_up", b_sharded=True, out_dtypes=(BF16, BF16),
                     epilogue=lambda acc, j: (acc, jnp.square(jnp.maximum(acc, 0.0))))
    w_down_full = gathered(3, act, "down").reshape(N_DEV * shard_ff, d)
    y = _matmul(act, w_down_full, name="mlp_down", tk=4096, out_dtypes=(F32,))

    core = jnp.reshape(ac, (1,)).astype(jnp.int32)

    def reduce_start(started, after, name):
        send, recv, src, land, _ = started
        full, from_sibling = _sib_wait(send, recv, src, land, after, "sib_wait_" + name)
        return _rs_start(_chip_sum(full, from_sibling, core, "chip_sum_" + name), "rs_start_" + name)

    def tok(started):
        return started[4][0:1, 0:1]

    idx = jnp.stack([2 * ax + ay, 2 * (1 - ax) + ay, 2 * ax + (1 - ay), 2 * (1 - ax) + (1 - ay)]).astype(jnp.int32)

    def reduce_finish(started, after, w, m, v, name):
        send, recv, src, land, _ = started
        own, landed = _rs_wait(send, recv, src, land, after, "rs_wait_" + name)
        return _adam_pieces(w, m[0], v[0], own, landed, idx, "adam_" + name)

    dy, dout, dgt_m, dg3, loss_vec = _final(y, x2, target, gt_m, g_post_mlp)
    du = _matmul(dy, w_down_full, name="d_act", tb=True, tile_extras=(u,),
                 epilogue=lambda acc, j, uu: (acc * (2.0 * jnp.maximum(uu.astype(F32), 0.0)),))
    dw_down = _matmul(act, dy, name="dw_down", ta=True)
    sb_down = _sib_start(dw_down.reshape(N_DEV, shard_ff, d), "sib_start_down")
    dh2 = _matmul(du, w_up_g, name="d_h2", tb=True, b_sharded=True, b_pair=True, after=sb_down[4])
    rs_down = reduce_start(sb_down, dh2, "down")
    per = shard_ff // _blk(shard_ff, 1024)
    dw_up = _matmul(h2, du, name="dw_up", ta=True, tn=_blk(shard_ff, 1024), out_shape=(N_DEV, d, shard_ff),
                    out_map=lambda tm, tn: pl.BlockSpec((None, tm, tn), lambda i, j, kk: (j // per, i, j % per)),
                    after=rs_down[4])
    sb_up = _sib_start(dw_up, "sib_start_up")
    dmix, dx2, dsh_m, dsc_m, dg2, dgt_a, dg1 = _mid_bwd(
        dh2, dout, x2, mix, g_pre_mlp + tok(sb_up), sc_m, gt_a, g_post_mix)
    d_attn = _matmul(dmix, w_out_full, name="d_attn", tb=True)
    rs_up = reduce_start(sb_up, d_attn, "up")
    dw_out = _matmul(attn, dmix, name="dw_out", ta=True, after=rs_up[4])
    sb_out = _sib_start(dw_out.reshape(N_DEV, d // N_DEV, d), "sib_start_out")
    dqf, dkf, dvf, dcs = _fox_bwd(proj_a, d_attn, cum, cumt, fox_lse)
    dsq, dsk, dsv, dsinks = _swa_bwd(proj_b, d_attn, swa_lse, sinks128 + tok(sb_out), cos, sin)
    rs_out = reduce_start(sb_out, dsq, "out")
    dfg, db_forget = _fg_bwd(dcs, fg, b128 + tok(rs_out))
    dproj = jnp.concatenate([dqf, dkf, dvf, dsq, dsk, dsv, _pad_lanes(dfg, FG_PAD)], axis=1)
    dw_in_r = _matmul(dproj, h1, name="dw_in", ta=True)

    def shard_rows(j):
        lo, hi = j * in_w, (j + 1) * in_w
        parts = []
        for seg_lo, seg_hi, shift in ((0, o_fg, 0), (o_fg, o_sq, n_swa3), (o_sq, in_total, -N_FOX)):
            a, b = max(lo, seg_lo), min(hi, seg_hi)
            if a < b:
                parts.append(dw_in_r[a + shift:b + shift, :])
        return parts[0] if len(parts) == 1 else jnp.concatenate(parts, axis=0)

    sb_in = _sib_start(jnp.stack([shard_rows(j) for j in range(N_DEV)]), "sib_start_in")
    dh1 = _matmul(dproj, w_in_r, name="d_h1", tb=True, tk=2560, after=sb_in[4])
    grad_x, dsh_a, dsc_a, dg0 = _x_bwd(dh1, dx2, x, g_pre_mix, sc_a)

    small = jnp.concatenate([dsh_a, dsc_a, dgt_a, dsh_m, dsc_m, dgt_m, dg0, dg1, dg2, dg3, db_forget, dsinks,
                             loss_vec[:, 0:LANES]], axis=1)
    small_all = _all_gather([small], "gather_small")[0].reshape(N_DEV, small.shape[1])
    rs_in = reduce_start(sb_in, small_all, "in")

    pack = lambda bm, g0_, g1_, g2_, g3_, bf_, sk_: jnp.concatenate(
        [bm, g0_, g1_, g2_, g3_, _pad_lanes(bf_), _pad_lanes(sk_), jnp.zeros((1, LANES), F32)], axis=1)
    p_small = pack(b_mod, g_pre_mix, g_post_mix, g_pre_mlp, g_post_mlp, b_forget, swa_sinks)
    m_small = pack(m_b_mod, m_g_pre_mix, m_g_post_mix, m_g_pre_mlp, m_g_post_mlp, m_b_forget, m_swa_sinks)
    v_small = pack(v_b_mod, v_g_pre_mix, v_g_post_mix, v_g_pre_mlp, v_g_post_mlp, v_b_forget, v_swa_sinks)
    small_out = _adam_small(small_all, p_small + tok(rs_in), m_small, v_small)

    n_mod = 6 * d

    def unpack(vec):
        o = n_mod
        return (vec[:, 0:n_mod], vec[:, o:o + d], vec[:, o + d:o + 2 * d], vec[:, o + 2 * d:o + 3 * d],
                vec[:, o + 3 * d:o + 4 * d], vec[:, o + 4 * d:o + 4 * d + N_FOX],
                vec[:, o + 4 * d + LANES:o + 4 * d + LANES + N_SWA])

    loss = small_out[0][0, n_mod + 4 * d + 2 * LANES]
    g_small, d_small, nm_small, nv_small = [unpack(vec) for vec in small_out]

    dmod_cols = lax.dynamic_slice(small_all, (0, me * mod_w), (N_DEV, mod_w))
    g_w_mod, d_w_mod, nm_w_mod, nv_w_mod = _adam_mod(c_all + tok(rs_in), dmod_cols, w_mod, m_w_mod[0], v_w_mod[0])

    g_w_down, d_w_down, nm_w_down, nv_w_down = reduce_finish(rs_down, d_w_mod, w_down, m_w_down, v_w_down, "w_down")
    g_w_up, d_w_up, nm_w_up, nv_w_up = reduce_finish(rs_up, d_w_down, w_up, m_w_up, v_w_up, "w_up")
    g_w_out, d_w_out, nm_w_out, nv_w_out = reduce_finish(rs_out, d_w_up, w_out, m_w_out, v_w_out, "w_out")
    g_w_in, d_w_in, nm_w_in, nv_w_in = reduce_finish(rs_in, d_w_out, adam_in[0], (adam_in[1],), (adam_in[2],), "w_in")

    def assemble(w_mod_, small_, w_in_, w_out_, w_up_, w_down_):
        b_mod_, g0_, g1_, g2_, g3_, bf_, sk_ = small_
        return [w_mod_[None], b_mod_, g0_, g1_, jnp.transpose(w_in_)[None], bf_, sk_, w_out_[None], g2_, g3_,
                w_up_[None], w_down_[None]]

    outs = [loss, grad_x[None]]
    outs += assemble(g_w_mod, g_small, g_w_in, g_w_out, g_w_up, g_w_down)
    outs += assemble(d_w_mod, d_small, d_w_in, d_w_out, d_w_up, d_w_down)
    outs += assemble(nm_w_mod, nm_small, nm_w_in, nm_w_out, nm_w_up, nm_w_down)
    outs += assemble(nv_w_mod, nv_small, nv_w_in, nv_w_out, nv_w_up, nv_w_down)
    return tuple(outs)
```

```python
import jax
import jax.numpy as jnp
from jax import lax
from jax.experimental import pallas as pl
from jax.experimental.pallas import tpu as pltpu

F32 = jnp.float32
BF16 = jnp.bfloat16
MESH = pl.DeviceIdType.MESH

N_DEV = 8
N_CHIP = 4
LANES = 128
HEAD_DIM = 128
N_FOX = 8
N_SWA = 8
N_KV = 2
GQA = N_SWA // N_KV
WINDOW = 128
FOX_W = N_FOX * HEAD_DIM
SWA_W = N_SWA * HEAD_DIM
KV_W = N_KV * HEAD_DIM
ROPE_THETA = 10000.0
NORM_EPS = 1e-6
ATT_SCALE = HEAD_DIM ** -0.5
FG_PAD = 512

ADAM_LR = 0.001
ADAM_B1 = 0.9
ADAM_B2 = 0.999
ADAM_EPS = 1e-08
ADAM_WD = 0.01
ADAM_STEP = 10

VMEM_LIMIT = 56 * 1024 * 1024

NT_DIMS = (((1,), (1,)), ((), ()))
TN_DIMS = (((0,), (0,)), ((), ()))


def _pcall(body, *, name, out_shape, grid=(), in_specs=None, out_specs=None, scratch_shapes=(), grid_spec=None):
    params = pltpu.CompilerParams(vmem_limit_bytes=VMEM_LIMIT)
    if grid_spec is not None:
        return pl.pallas_call(body, name=name, out_shape=out_shape, grid_spec=grid_spec, compiler_params=params)
    return pl.pallas_call(body, name=name, out_shape=out_shape, grid=grid, in_specs=in_specs, out_specs=out_specs,
                          scratch_shapes=scratch_shapes, compiler_params=params)


def _blk(n, pref):
    if n <= pref:
        return n
    b = (pref // LANES) * LANES
    while n % b:
        b -= LANES
    return b


def _position():
    return lax.axis_index("x"), lax.axis_index("y"), lax.axis_index("c")


ANY = pl.BlockSpec(memory_space=pl.ANY)


def _all_gather(arrs, name):
    n = len(arrs)

    def body(*refs):
        ins, outs = refs[:n], refs[n:2 * n]
        send_sems, recv_sems, local_sems = refs[2 * n:]
        x, y, c = _position()
        me, sibling = (x, y, c), (x, y, 1 - c)
        chips = [(1 - x, y), (x, 1 - y), (1 - x, 1 - y)]

        def slot(p):
            return 4 * p[0] + 2 * p[1] + p[2]

        def copy(a, k, block, to, src=None):
            dst = outs[a].at[slot(block)]
            return pltpu.make_async_remote_copy(
                src_ref=dst if src is None else src, dst_ref=dst,
                send_sem=send_sems.at[7 * a + k], recv_sem=recv_sems.at[7 * a + k],
                device_id=to, device_id_type=MESH)

        mine = [pltpu.make_async_copy(ins[a], outs[a].at[slot(me)], local_sems.at[a]) for a in range(n)]
        for cp in mine:
            cp.start()
        first = []
        for a in range(n):
            first.append(copy(a, 0, me, sibling, src=ins[a]))
            first += [copy(a, 1 + j, me, (*chip, c), src=ins[a]) for j, chip in enumerate(chips)]
        for cp in first:
            cp.start()
        passed = []
        for a in range(n):
            for j, chip in enumerate(chips):
                copy(a, 1 + j, (*chip, c), me).wait_recv()
                cp = copy(a, 4 + j, (*chip, c), sibling)
                cp.start()
                passed.append(cp)
        for a in range(n):
            copy(a, 0, sibling, me).wait_recv()
            for j, chip in enumerate(chips):
                copy(a, 4 + j, (*chip, 1 - c), me).wait_recv()
        for cp in first + passed:
            cp.wait_send()
        for cp in mine:
            cp.wait()

    return _pcall(
        body, name=name,
        out_shape=[jax.ShapeDtypeStruct((N_DEV,) + a.shape, a.dtype) for a in arrs],
        in_specs=[ANY] * n, out_specs=[ANY] * n,
        scratch_shapes=[pltpu.SemaphoreType.DMA((7 * n,)), pltpu.SemaphoreType.DMA((7 * n,)),
                        pltpu.SemaphoreType.DMA((n,))],
    )(*arrs)


HBM = pl.BlockSpec(memory_space=pltpu.HBM)
SEM = pl.BlockSpec(memory_space=pltpu.SEMAPHORE)
EFFECT = pltpu.SideEffectType.DATAFLOW_SIDE_EFFECTING


def _hbm(a):
    return pltpu.with_memory_space_constraint(a, pltpu.HBM)


def _gather_peers():
    x, y, c = _position()
    return [(x, y, 1 - c), (1 - x, y, c), (x, 1 - y, c), (1 - x, 1 - y, c)]


def _ag_start(shards, name):
    n = len(shards)
    lands = [_hbm(lax.empty((N_DEV,) + s.shape, s.dtype)) for s in shards]

    def body(*refs):
        srcs, land, send, recv = refs[:n], refs[n:2 * n], refs[2 * n:3 * n], refs[3 * n:4 * n]
        token = refs[6 * n]
        x, y, c = _position()
        for a in range(n):
            for k, to in enumerate(_gather_peers()):
                pltpu.make_async_remote_copy(
                    src_ref=srcs[a], dst_ref=land[a].at[4 * x + 2 * y + c], send_sem=send[a].at[k],
                    recv_sem=recv[a].at[k], device_id=to, device_id_type=MESH).start()
        token[...] = jnp.zeros_like(token)

    sems = [pltpu.SemaphoreType.DMA((4,))] * (2 * n)
    out = pl.pallas_call(
        body, name=name,
        out_shape=sems + [pltpu.HBM(s.shape, s.dtype) for s in shards] + [pltpu.HBM(l.shape, l.dtype) for l in lands]
        + [jax.ShapeDtypeStruct((8, LANES), F32)],
        in_specs=[HBM] * (2 * n), out_specs=[SEM] * (2 * n) + [HBM] * (2 * n) + [pl.BlockSpec(memory_space=pltpu.VMEM)],
        input_output_aliases={**{a: 2 * n + a for a in range(n)}, **{n + a: 3 * n + a for a in range(n)}},
        compiler_params=pltpu.CompilerParams(has_side_effects=EFFECT),
    )(*[_hbm(s) for s in shards], *lands)
    return out[:n], out[n:2 * n], out[2 * n:3 * n], out[3 * n:4 * n], out[4 * n]


def _ag_wait(send, recv, shard_thru, land_thru, after, name):
    def body(v_ref, land_ref, send_sem, recv_sem, after_ref, v_dead, got_ref):
        for k, to in enumerate(_gather_peers()):
            cp = pltpu.make_async_remote_copy(
                src_ref=v_ref, dst_ref=land_ref.at[0], send_sem=send_sem.at[k], recv_sem=recv_sem.at[k],
                device_id=to, device_id_type=MESH)
            cp.wait_send()
            cp.wait_recv()

    return pl.pallas_call(
        body, name=name,
        out_shape=(pltpu.HBM(shard_thru.shape, shard_thru.dtype), pltpu.HBM(land_thru.shape, land_thru.dtype)),
        in_specs=(HBM, HBM, SEM, SEM, ANY), out_specs=(HBM, HBM), input_output_aliases={0: 0, 1: 1},
        compiler_params=pltpu.CompilerParams(has_side_effects=EFFECT),
    )(shard_thru, land_thru, send, recv, after)


def _ag_forward(land, name):
    def body(land_in, land_ref, send_sems, recv_sems):
        x, y, c = _position()
        copies = []
        for j, (px, py) in enumerate([(1 - x, y), (x, 1 - y), (1 - x, 1 - y)]):
            block = land_ref.at[4 * px + 2 * py + c]
            cp = pltpu.make_async_remote_copy(src_ref=block, dst_ref=block, send_sem=send_sems.at[j],
                                              recv_sem=recv_sems.at[j], device_id=(x, y, 1 - c), device_id_type=MESH)
            cp.start()
            copies.append(cp)
        for cp in copies:
            cp.wait()

    return pl.pallas_call(
        body, name=name, out_shape=jax.ShapeDtypeStruct(land.shape, land.dtype),
        in_specs=[ANY], out_specs=ANY, input_output_aliases={0: 0},
        scratch_shapes=[pltpu.SemaphoreType.DMA((3,)), pltpu.SemaphoreType.DMA((3,))],
    )(land)


def _fwd_start(land, name):
    def body(land_ref, send, recv, land_thru, token):
        x, y, c = _position()
        for j, (px, py) in enumerate([(1 - x, y), (x, 1 - y), (1 - x, 1 - y)]):
            block = land_ref.at[4 * px + 2 * py + c]
            pltpu.make_async_remote_copy(src_ref=block, dst_ref=block, send_sem=send.at[j], recv_sem=recv.at[j],
                                         device_id=(x, y, 1 - c), device_id_type=MESH).start()
        token[...] = jnp.zeros_like(token)

    return pl.pallas_call(
        body, name=name,
        out_shape=[pltpu.SemaphoreType.DMA((3,)), pltpu.SemaphoreType.DMA((3,)), pltpu.HBM(land.shape, land.dtype),
                   jax.ShapeDtypeStruct((8, LANES), F32)],
        in_specs=[HBM], out_specs=[SEM, SEM, HBM, pl.BlockSpec(memory_space=pltpu.VMEM)],
        input_output_aliases={0: 2},
        compiler_params=pltpu.CompilerParams(has_side_effects=EFFECT),
    )(land)


def _fwd_wait(send, recv, land_thru, after, name):
    def body(land_ref, send_sem, recv_sem, after_ref, land_out):
        x, y, c = _position()
        for j in range(3):
            cp = pltpu.make_async_remote_copy(
                src_ref=land_ref.at[0], dst_ref=land_ref.at[0], send_sem=send_sem.at[j], recv_sem=recv_sem.at[j],
                device_id=(x, y, 1 - c), device_id_type=MESH)
            cp.wait_send()
            cp.wait_recv()

    return pl.pallas_call(
        body, name=name, out_shape=pltpu.HBM(land_thru.shape, land_thru.dtype),
        in_specs=(HBM, SEM, SEM, ANY), out_specs=HBM, input_output_aliases={0: 0},
        compiler_params=pltpu.CompilerParams(has_side_effects=EFFECT),
    )(land_thru, send, recv, after)


def _rs_peers():
    x, y, c = _position()
    return [(1 - x, y, c), (x, 1 - y, c), (1 - x, 1 - y, c)]


def _rs_start(chip_sums, name):
    land = _hbm(lax.empty(chip_sums.shape, chip_sums.dtype))

    def body(src, land_ref, send, recv, src_thru, land_thru, token):
        x, y, c = _position()
        for j, (px, py, pc) in enumerate(_rs_peers()):
            pltpu.make_async_remote_copy(
                src_ref=src.at[2 * px + py], dst_ref=land_ref.at[2 * x + y], send_sem=send.at[j], recv_sem=recv.at[j],
                device_id=(px, py, pc), device_id_type=MESH).start()
        token[...] = jnp.zeros_like(token)

    return pl.pallas_call(
        body, name=name,
        out_shape=[pltpu.SemaphoreType.DMA((3,)), pltpu.SemaphoreType.DMA((3,)),
                   pltpu.HBM(chip_sums.shape, chip_sums.dtype), pltpu.HBM(land.shape, land.dtype),
                   jax.ShapeDtypeStruct((8, LANES), F32)],
        in_specs=[HBM, HBM], out_specs=[SEM, SEM, HBM, HBM, pl.BlockSpec(memory_space=pltpu.VMEM)],
        input_output_aliases={0: 2, 1: 3},
        compiler_params=pltpu.CompilerParams(has_side_effects=EFFECT),
    )(_hbm(chip_sums), land)


def _rs_wait(send, recv, src_thru, land_thru, after, name):
    def body(src, land_ref, send_sem, recv_sem, after_ref, src_out, land_out):
        for j, to in enumerate(_rs_peers()):
            cp = pltpu.make_async_remote_copy(
                src_ref=src.at[0], dst_ref=land_ref.at[0], send_sem=send_sem.at[j], recv_sem=recv_sem.at[j],
                device_id=to, device_id_type=MESH)
            cp.wait_send()
            cp.wait_recv()

    return pl.pallas_call(
        body, name=name,
        out_shape=(pltpu.HBM(src_thru.shape, src_thru.dtype), pltpu.HBM(land_thru.shape, land_thru.dtype)),
        in_specs=(HBM, HBM, SEM, SEM, ANY), out_specs=(HBM, HBM), input_output_aliases={0: 0, 1: 1},
        compiler_params=pltpu.CompilerParams(has_side_effects=EFFECT),
    )(src_thru, land_thru, send, recv, after)


def _sib_start(full, name):
    land = _hbm(lax.empty((N_CHIP,) + full.shape[1:], full.dtype))

    def body(src, land_ref, send, recv, src_thru, land_thru, token):
        x, y, c = _position()
        for k in range(N_CHIP):
            pltpu.make_async_remote_copy(
                src_ref=src.at[2 * k + (1 - c)], dst_ref=land_ref.at[k], send_sem=send.at[k], recv_sem=recv.at[k],
                device_id=(x, y, 1 - c), device_id_type=MESH).start()
        token[...] = jnp.zeros_like(token)

    return pl.pallas_call(
        body, name=name,
        out_shape=[pltpu.SemaphoreType.DMA((N_CHIP,)), pltpu.SemaphoreType.DMA((N_CHIP,)),
                   pltpu.HBM(full.shape, full.dtype), pltpu.HBM(land.shape, land.dtype),
                   jax.ShapeDtypeStruct((8, LANES), F32)],
        in_specs=[HBM, HBM], out_specs=[SEM, SEM, HBM, HBM, pl.BlockSpec(memory_space=pltpu.VMEM)],
        input_output_aliases={0: 2, 1: 3},
        compiler_params=pltpu.CompilerParams(has_side_effects=EFFECT),
    )(_hbm(full), land)


def _sib_wait(send, recv, src_thru, land_thru, after, name):
    def body(src, land_ref, send_sem, recv_sem, after_ref, src_out, land_out):
        x, y, c = _position()
        for k in range(N_CHIP):
            cp = pltpu.make_async_remote_copy(
                src_ref=src.at[0], dst_ref=land_ref.at[0], send_sem=send_sem.at[k], recv_sem=recv_sem.at[k],
                device_id=(x, y, 1 - c), device_id_type=MESH)
            cp.wait_send()
            cp.wait_recv()

    return pl.pallas_call(
        body, name=name,
        out_shape=(pltpu.HBM(src_thru.shape, src_thru.dtype), pltpu.HBM(land_thru.shape, land_thru.dtype)),
        in_specs=(HBM, HBM, SEM, SEM, ANY), out_specs=(HBM, HBM), input_output_aliases={0: 0, 1: 1},
        compiler_params=pltpu.CompilerParams(has_side_effects=EFFECT),
    )(src_thru, land_thru, send, recv, after)


def _chip_sum(full, recv, core, name):
    _, rows, cols = full.shape
    tr = _blk(rows, 1024) if rows % LANES == 0 else rows

    def body(core_ref, a_ref, b_ref, o_ref):
        o_ref[...] = (a_ref[...].astype(F32) + b_ref[...].astype(F32)).astype(o_ref.dtype)

    grid_spec = pltpu.PrefetchScalarGridSpec(
        num_scalar_prefetch=1, grid=(N_CHIP, rows // tr),
        in_specs=[pl.BlockSpec((None, tr, cols), lambda k, i, core_ref: (2 * k + core_ref[0], i, 0)),
                  pl.BlockSpec((None, tr, cols), lambda k, i, core_ref: (k, i, 0))],
        out_specs=pl.BlockSpec((None, tr, cols), lambda k, i, core_ref: (k, i, 0)))
    return _pcall(body, name=name, out_shape=jax.ShapeDtypeStruct((N_CHIP, rows, cols), full.dtype),
                  grid_spec=grid_spec)(core, full, recv)


def _matmul(a, b, *, name, ta=False, tb=False, tm=1024, tn=1024, tk=2048, out_dtypes=(BF16,), epilogue=None,
            row_extras=(), tile_extras=(), out_shape=None, out_map=None, b_sharded=False, n_cols=None, n_off=0,
            after=None, b_pair=False):
    m, k = (a.shape[1], a.shape[0]) if ta else a.shape
    if b_sharded:
        shard_c = b.shape[2]
        n, kb = (b.shape[1], N_DEV * shard_c) if tb else (N_DEV * shard_c, b.shape[1])
        tn, tk = (tn, min(tk, shard_c)) if tb else (min(tn, shard_c), tk)
        if b_pair:
            assert tb and tk == shard_c
            tk = 2 * shard_c
    else:
        n, kb = b.shape if tb else (b.shape[1], b.shape[0])
    assert kb == k, (name, kb, k)
    if n_cols is not None:
        n = n_cols
    tm, tn, tk = _blk(m, tm), _blk(n, tn), _blk(k, tk)
    assert n_off % tn == 0
    nk = k // tk
    dims = (((0 if ta else 1,), (1 if tb else 0,)), ((), ()))
    behind = () if after is None else (after,)
    n_row, n_tile, n_out = len(row_extras), len(tile_extras), len(out_dtypes)
    n_b = 2 if b_pair else 1
    first_out = 1 + n_b + n_row + n_tile + len(behind)

    def body(*refs):
        a_ref, b_ref = refs[:2]
        extras = refs[1 + n_b:1 + n_b + n_row + n_tile]
        outs = refs[first_out:first_out + n_out]
        acc_ref = refs[-1]
        jj, kk = pl.program_id(1), pl.program_id(2)
        if b_pair:
            half = tk // 2
            part = (lax.dot_general(a_ref[:, 0:half].astype(BF16), b_ref[...].astype(BF16), dims,
                                    preferred_element_type=F32)
                    + lax.dot_general(a_ref[:, half:tk].astype(BF16), refs[2][...].astype(BF16), dims,
                                      preferred_element_type=F32))
        else:
            part = lax.dot_general(a_ref[...].astype(BF16), b_ref[...].astype(BF16), dims,
                                   preferred_element_type=F32)

        def finish(acc):
            res = (acc,) if epilogue is None else epilogue(acc, jj, *[e[...] for e in extras])
            for o_ref, r in zip(outs, res):
                o_ref[...] = r.astype(o_ref.dtype)

        if nk == 1:
            finish(part)
        else:
            @pl.when(kk == 0)
            def _():
                acc_ref[...] = part

            @pl.when(kk > 0)
            def _():
                acc_ref[...] += part

            @pl.when(kk == nk - 1)
            def _():
                finish(acc_ref[...])

    a_spec = pl.BlockSpec((tk, tm), lambda i, j, kk: (kk, i)) if ta else pl.BlockSpec((tm, tk), lambda i, j, kk: (i, kk))
    if b_pair:
        b_spec = pl.BlockSpec((None, tn, shard_c), lambda i, j, kk: (2 * kk, j, 0))
    elif b_sharded and tb:
        per = shard_c // tk
        b_spec = pl.BlockSpec((None, tn, tk), lambda i, j, kk: (kk // per, j, kk % per))
    elif b_sharded:
        per = shard_c // tn
        b_spec = pl.BlockSpec((None, tk, tn), lambda i, j, kk: (j // per, kk, j % per))
    elif tb:
        b_spec = pl.BlockSpec((tn, tk), lambda i, j, kk: (j + n_off // tn, kk))
    else:
        b_spec = pl.BlockSpec((tk, tn), lambda i, j, kk: (kk, j + n_off // tn))
    in_specs = [a_spec, b_spec]
    if b_pair:
        in_specs.append(pl.BlockSpec((None, tn, shard_c), lambda i, j, kk: (2 * kk + 1, j, 0)))
    in_specs += [pl.BlockSpec((tm, LANES), lambda i, j, kk: (i, 0)) for _ in row_extras]
    in_specs += [pl.BlockSpec((tm, tn), lambda i, j, kk: (i, j)) for _ in tile_extras]
    in_specs += [ANY for _ in behind]
    if out_map is None:
        out_specs = [pl.BlockSpec((tm, tn), lambda i, j, kk: (i, j)) for _ in out_dtypes]
        shapes = [jax.ShapeDtypeStruct((m, n), dt) for dt in out_dtypes]
    else:
        out_specs = [out_map(tm, tn)]
        shapes = [jax.ShapeDtypeStruct(out_shape, out_dtypes[0])]
    acc_shape = (tm, tn) if nk > 1 else (8, LANES)
    res = _pcall(body, name=name, out_shape=shapes, grid=(m // tm, n // tn, nk), in_specs=in_specs,
                 out_specs=out_specs, scratch_shapes=[pltpu.VMEM(acc_shape, F32)])(
                     a, *([b, b] if b_pair else [b]), *row_extras, *tile_extras, *behind)
    return res[0] if n_out == 1 else res


def _rope_cols(acc, j, cos, sin, n_rope):
    width = acc.shape[1]
    parts = []
    for g in range(width // HEAD_DIM):
        xg = acc[:, g * HEAD_DIM:(g + 1) * HEAD_DIM]
        roped = xg * cos + pltpu.roll(xg, HEAD_DIM // 2, 1) * sin
        parts.append(jnp.where(j * width + g * HEAD_DIM < n_rope, roped, xg))
    return jnp.concatenate(parts, axis=1) if len(parts) > 1 else parts[0]


def _silu(v):
    return v / (1.0 + jnp.exp(-v))


def _mod_part(c_all, w_mod, b_part):
    d, w = w_mod.shape
    tk = _blk(d, 512)

    def body(c_ref, w_ref, b_ref, o_ref):
        kk = pl.program_id(0)
        cond = _silu(c_ref[...]).astype(BF16)
        part = jnp.dot(cond, w_ref[...].astype(BF16), preferred_element_type=F32)

        @pl.when(kk == 0)
        def _():
            o_ref[...] = part + b_ref[...]

        @pl.when(kk > 0)
        def _():
            o_ref[...] += part

    return _pcall(body, name="mod_part", out_shape=jax.ShapeDtypeStruct((N_DEV, w), F32), grid=(d // tk,),
                  in_specs=[pl.BlockSpec((N_DEV, tk), lambda kk: (0, kk)), pl.BlockSpec((tk, w), lambda kk: (kk, 0)),
                            pl.BlockSpec((1, w), lambda kk: (0, 0))],
                  out_specs=pl.BlockSpec((N_DEV, w), lambda kk: (0, 0)))(c_all, w_mod, b_part)


def _row_call(body, name, t, d, tiled_in, vec_in, tiled_out_dtypes, n_vec_out, tr=256):
    tr = _blk(t, tr)
    tile = pl.BlockSpec((tr, d), lambda i: (i, 0))
    vec = pl.BlockSpec((1, d), lambda i: (0, 0))
    out_shape = [jax.ShapeDtypeStruct((t, d), dt) for dt in tiled_out_dtypes]
    out_shape += [jax.ShapeDtypeStruct((1, d), F32)] * n_vec_out
    return _pcall(body, name=name, out_shape=out_shape, grid=(t // tr,),
                  in_specs=[tile] * len(tiled_in) + [vec] * len(vec_in),
                  out_specs=[tile] * len(tiled_out_dtypes) + [vec] * n_vec_out)(*tiled_in, *vec_in)


def _accumulate(ref, val):
    @pl.when(pl.program_id(0) == 0)
    def _():
        ref[...] = val

    @pl.when(pl.program_id(0) > 0)
    def _():
        ref[...] += val


def _rsum(v):
    return jnp.sum(v, axis=0, keepdims=True)


def _rms(v):
    return lax.rsqrt(jnp.mean(v * v, axis=-1, keepdims=True) + NORM_EPS)


def _rms_bwd(vhat, r, dvhat):
    return r * (dvhat - vhat * jnp.mean(dvhat * vhat, axis=-1, keepdims=True))


def _pre_attn(x, g0, sc_a, sh_a):
    def body(x_ref, g_ref, sc_ref, sh_ref, h_ref):
        xv = x_ref[...]
        h_ref[...] = (xv * _rms(xv) * g_ref[...] * (1.0 + sc_ref[...]) + sh_ref[...]).astype(BF16)

    t, d = x.shape
    return _row_call(body, "pre_attn", t, d, [x], [g0, sc_a, sh_a], [BF16], 0)[0]


def _post_mix(x, mix, gt_a, g1, g2, sc_m, sh_m):
    def body(x_ref, mix_ref, gt_ref, g1_ref, g2_ref, sc_ref, sh_ref, x2_ref, h2_ref):
        mv = mix_ref[...]
        x2 = x_ref[...] + gt_ref[...] * (mv * _rms(mv) * g1_ref[...])
        x2_ref[...] = x2
        h2_ref[...] = (x2 * _rms(x2) * g2_ref[...] * (1.0 + sc_ref[...]) + sh_ref[...]).astype(BF16)

    t, d = x.shape
    return _row_call(body, "post_mix", t, d, [x, mix], [gt_a, g1, g2, sc_m, sh_m], [F32, BF16], 0)


def _final(y, x2, target, gt_m, g3):
    t, d = y.shape

    def body(y_ref, x2_ref, tg_ref, gt_ref, g3_ref, dy_ref, dout_ref, dgt_ref, dg3_ref, loss_ref):
        yv = y_ref[...]
        r = _rms(yv)
        yhat = yv * r
        n3 = yhat * g3_ref[...]
        err = x2_ref[...] + gt_ref[...] * n3 - tg_ref[...]
        _accumulate(loss_ref, jnp.zeros((1, d), F32) + 0.5 * jnp.sum(err * err) / d)
        dout = err * (1.0 / d)
        dout_ref[...] = dout
        dout_yhat = dout * yhat
        _accumulate(dgt_ref, _rsum(dout_yhat) * g3_ref[...])
        _accumulate(dg3_ref, _rsum(dout_yhat) * gt_ref[...])
        dy_ref[...] = _rms_bwd(yhat, r, dout * (gt_ref[...] * g3_ref[...])).astype(BF16)

    return _row_call(body, "final", t, d, [y, x2, target], [gt_m, g3], [BF16, F32], 3)


def _mid_bwd(dh2, dout, x2, mix, g2, sc_m, gt_a, g1):
    t, d = x2.shape

    def body(dh2_ref, dout_ref, x2_ref, mix_ref, g2_ref, sc_ref, gt_ref, g1_ref,
             dmix_ref, dx2_ref, dsh_ref, dsc_ref, dg2_ref, dgt_ref, dg1_ref):
        g2, scale, g1, gt = g2_ref[...], 1.0 + sc_ref[...], g1_ref[...], gt_ref[...]
        dh2v = dh2_ref[...].astype(F32)
        x2v = x2_ref[...]
        r2 = _rms(x2v)
        x2hat = x2v * r2
        dh_xhat = dh2v * x2hat
        _accumulate(dsh_ref, _rsum(dh2v))
        _accumulate(dsc_ref, _rsum(dh_xhat) * g2)
        _accumulate(dg2_ref, _rsum(dh_xhat) * scale)
        dx2 = dout_ref[...] + _rms_bwd(x2hat, r2, dh2v * (scale * g2))
        dx2_ref[...] = dx2
        mv = mix_ref[...]
        r1 = _rms(mv)
        mhat = mv * r1
        dx_mhat = dx2 * mhat
        _accumulate(dgt_ref, _rsum(dx_mhat) * g1)
        _accumulate(dg1_ref, _rsum(dx_mhat) * gt)
        dmix_ref[...] = _rms_bwd(mhat, r1, dx2 * (gt * g1)).astype(BF16)

    return _row_call(body, "mid_bwd", t, d, [dh2, dout, x2, mix], [g2, sc_m, gt_a, g1], [BF16, F32], 5)


def _x_bwd(dh1, dx2, x, g0, sc_a):
    t, d = x.shape

    def body(dh1_ref, dx2_ref, x_ref, g0_ref, sc_ref, dx_ref, dsh_ref, dsc_ref, dg0_ref):
        g0, scale = g0_ref[...], 1.0 + sc_ref[...]
        dh1v = dh1_ref[...].astype(F32)
        xv = x_ref[...]
        r0 = _rms(xv)
        xhat = xv * r0
        dh_xhat = dh1v * xhat
        _accumulate(dsh_ref, _rsum(dh1v))
        _accumulate(dsc_ref, _rsum(dh_xhat) * g0)
        _accumulate(dg0_ref, _rsum(dh_xhat) * scale)
        dx_ref[...] = dx2_ref[...] + _rms_bwd(xhat, r0, dh1v * (scale * g0))

    return _row_call(body, "x_bwd", t, d, [dh1, dx2, x], [g0, sc_a], [F32], 3)


def _pick_lane(block, h):
    lane = lax.broadcasted_iota(jnp.int32, block.shape, 1)
    return jnp.sum(jnp.where(lane == h, block, 0.0), axis=1, keepdims=True)


def _put_lane(ref, rows, h, col):
    old = ref[rows, :]
    lane = lax.broadcasted_iota(jnp.int32, old.shape, 1)
    ref[rows, :] = jnp.where(lane == h, col, old)


def _tri(n, lower):
    r = lax.broadcasted_iota(jnp.int32, (n, n), 0)
    c = lax.broadcasted_iota(jnp.int32, (n, n), 1)
    return jnp.where((c <= r) if lower else (c >= r), 1.0, 0.0).astype(F32)


def _cum_fwd(fg, b128):
    t = fg.shape[0]
    nb = t // LANES

    def body(fg_ref, b_ref, cum_ref, cumt_ref):
        tri = _tri(LANES, True)
        carry = jnp.zeros((1, LANES), F32)
        for i in range(nb):
            z = fg_ref[i * LANES:(i + 1) * LANES, :] + b_ref[...]
            lf = jnp.minimum(z, 0.0) - jnp.log(1.0 + jnp.exp(-jnp.abs(z)))
            blk = jnp.dot(tri, lf, precision=lax.Precision.HIGHEST, preferred_element_type=F32) + carry
            cum_ref[i * LANES:(i + 1) * LANES, :] = blk
            carry = blk[LANES - 1:LANES, :]
        cumt_ref[...] = cum_ref[...].T[0:N_FOX, :]

    return _pcall(body, name="cum_fwd",
                  out_shape=[jax.ShapeDtypeStruct((t, LANES), F32), jax.ShapeDtypeStruct((N_FOX, t), F32)],
                  grid=(1,),
                  in_specs=[pl.BlockSpec((t, LANES), lambda i: (0, 0)), pl.BlockSpec((1, LANES), lambda i: (0, 0))],
                  out_specs=[pl.BlockSpec((t, LANES), lambda i: (0, 0)), pl.BlockSpec((N_FOX, t), lambda i: (0, 0))],
                  )(fg, b128)


def _fg_bwd(dcs_rows, fg, b128):
    t = fg.shape[0]
    nb = t // LANES

    def body(dcs_ref, fg_ref, b_ref, dfg_ref, db_ref, dcum_ref):
        dcum_ref[...] = -jnp.concatenate([dcs_ref[...], jnp.zeros((LANES - N_FOX, t), F32)], axis=0).T
        tri = _tri(LANES, False)
        carry = jnp.zeros((1, LANES), F32)
        db = jnp.zeros((1, LANES), F32)
        for i in reversed(range(nb)):
            rows = slice(i * LANES, (i + 1) * LANES)
            dlf = jnp.dot(tri, dcum_ref[rows, :], precision=lax.Precision.HIGHEST, preferred_element_type=F32) + carry
            carry = dlf[0:1, :]
            z = fg_ref[rows, :] + b_ref[...]
            dfg = dlf / (1.0 + jnp.exp(z))
            dfg_ref[rows, :] = dfg.astype(BF16)
            db = db + _rsum(dfg)
        db_ref[...] = db

    full = pl.BlockSpec((t, LANES), lambda i: (0, 0))
    vec = pl.BlockSpec((1, LANES), lambda i: (0, 0))
    return _pcall(body, name="fg_bwd",
                  out_shape=[jax.ShapeDtypeStruct((t, LANES), BF16), jax.ShapeDtypeStruct((1, LANES), F32)],
                  grid=(1,), in_specs=[pl.BlockSpec((N_FOX, t), lambda i: (0, 0)), full, vec], out_specs=[full, vec],
                  scratch_shapes=[pltpu.VMEM((t, LANES), F32)])(dcs_rows, fg, b128)


def _head_spec(t, col0, div=1):
    return pl.BlockSpec((t, HEAD_DIM), lambda h: (0, col0 + h // div))


def _fox_scores(q, k, cq, ck, i, tq, end):
    s = lax.dot_general(q, k, NT_DIMS, preferred_element_type=F32) * ATT_SCALE + cq - ck
    row = lax.broadcasted_iota(jnp.int32, (tq, end), 0) + i * tq
    col = lax.broadcasted_iota(jnp.int32, (tq, end), 1)
    return jnp.where(row >= col, s, -jnp.inf)


def _fox_fwd(proj_a, cum, cumt):
    t = proj_a.shape[0]
    tq = _blk(t, 512)
    nq = t // tq

    def body(q_ref, k_ref, v_ref, cum_ref, cumt_ref, o_ref, lse_ref):
        h = pl.program_id(0)
        cq_all = _pick_lane(cum_ref[...], h)
        ck_all = cumt_ref[pl.ds(h, 1), :]

        @pl.when(h == 0)
        def _():
            lse_ref[...] = jnp.zeros_like(lse_ref)

        for i in range(nq):
            rows, end = slice(i * tq, (i + 1) * tq), (i + 1) * tq
            s = _fox_scores(q_ref[rows, :], k_ref[0:end, :], cq_all[rows, :], ck_all[:, 0:end], i, tq, end)
            m = jnp.max(s, axis=1, keepdims=True)
            p = jnp.exp(s - m)
            l = jnp.sum(p, axis=1, keepdims=True)
            o = jnp.dot(p.astype(BF16), v_ref[0:end, :], preferred_element_type=F32) / l
            o_ref[rows, :] = o.astype(BF16)
            _put_lane(lse_ref, rows, h, m + jnp.log(l))

    nh = FOX_W // HEAD_DIM
    stat = pl.BlockSpec((t, LANES), lambda h: (0, 0))
    return _pcall(body, name="fox_fwd",
                  out_shape=[jax.ShapeDtypeStruct((t, FOX_W), BF16), jax.ShapeDtypeStruct((t, LANES), F32)],
                  grid=(N_FOX,),
                  in_specs=[_head_spec(t, 0), _head_spec(t, nh), _head_spec(t, 2 * nh), stat,
                            pl.BlockSpec((N_FOX, t), lambda h: (0, 0))],
                  out_specs=[_head_spec(t, 0), stat])(proj_a, proj_a, proj_a, cum, cumt)


def _fox_bwd(proj_a, d_attn, cum, cumt, lse):
    t = proj_a.shape[0]
    tq = _blk(t, 512)
    nq = t // tq

    def body(q_ref, k_ref, v_ref, do_ref, cum_ref, cumt_ref, lse_ref,
             dq_ref, dk_ref, dv_ref, dcs_ref, dk_acc, dv_acc, dcs_acc):
        h = pl.program_id(0)
        cq_all = _pick_lane(cum_ref[...], h)
        ck_all = cumt_ref[pl.ds(h, 1), :]
        lse_all = _pick_lane(lse_ref[...], h)
        dk_acc[...] = jnp.zeros_like(dk_acc)
        dv_acc[...] = jnp.zeros_like(dv_acc)
        dcs_acc[...] = jnp.zeros_like(dcs_acc)
        for i in range(nq):
            rows, end = slice(i * tq, (i + 1) * tq), (i + 1) * tq
            q, k, v, do = q_ref[rows, :], k_ref[0:end, :], v_ref[0:end, :], do_ref[rows, :]
            s = _fox_scores(q, k, cq_all[rows, :], ck_all[:, 0:end], i, tq, end)
            p = jnp.exp(s - lse_all[rows, :])
            dp = lax.dot_general(do, v, NT_DIMS, preferred_element_type=F32)
            ds = p * (dp - jnp.sum(p * dp, axis=1, keepdims=True))
            dcs_acc[:, 0:end] += jnp.sum(ds, axis=0, keepdims=True)
            ds = ds.astype(BF16)
            dq_ref[rows, :] = (jnp.dot(ds, k, preferred_element_type=F32) * ATT_SCALE).astype(BF16)
            dk_acc[0:end, :] += lax.dot_general(ds, q, TN_DIMS, preferred_element_type=F32)
            dv_acc[0:end, :] += lax.dot_general(p.astype(BF16), do, TN_DIMS, preferred_element_type=F32)
        dk_ref[...] = (dk_acc[...] * ATT_SCALE).astype(BF16)
        dv_ref[...] = dv_acc[...].astype(BF16)
        dcs_ref[pl.ds(h, 1), :] = dcs_acc[...]

    nh = FOX_W // HEAD_DIM
    stat = pl.BlockSpec((t, LANES), lambda h: (0, 0))
    rows8 = pl.BlockSpec((N_FOX, t), lambda h: (0, 0))
    head = _head_spec(t, 0)
    wide = jax.ShapeDtypeStruct((t, FOX_W), BF16)
    return _pcall(body, name="fox_bwd",
                  out_shape=[wide, wide, wide, jax.ShapeDtypeStruct((N_FOX, t), F32)],
                  grid=(N_FOX,),
                  in_specs=[_head_spec(t, 0), _head_spec(t, nh), _head_spec(t, 2 * nh), head, stat, rows8, stat],
                  out_specs=[head, head, head, rows8],
                  scratch_shapes=[pltpu.VMEM((t, HEAD_DIM), F32), pltpu.VMEM((t, HEAD_DIM), F32),
                                  pltpu.VMEM((1, t), F32)],
                  )(proj_a, proj_a, proj_a, d_attn, cum, cumt, lse)


def _swa_scores(q, k, i, tq, start, end):
    s = lax.dot_general(q, k, NT_DIMS, preferred_element_type=F32) * ATT_SCALE
    row = lax.broadcasted_iota(jnp.int32, (tq, end - start), 0) + i * tq
    col = lax.broadcasted_iota(jnp.int32, (tq, end - start), 1) + start
    diff = row - col
    return jnp.where((diff >= 0) & (diff < WINDOW), s, -jnp.inf)


def _swa_blocks(t):
    tq = _blk(t, 256)
    return tq, [(i, max(0, i * tq - WINDOW), (i + 1) * tq) for i in range(t // tq)]


def _swa_fwd(proj_b, sinks128):
    t = proj_b.shape[0]
    tq, blocks = _swa_blocks(t)

    def body(q_ref, k_ref, v_ref, sink_ref, o_ref, lse_ref):
        h = pl.program_id(0)
        sink = _pick_lane(sink_ref[...], h)

        @pl.when(h == 0)
        def _():
            lse_ref[...] = jnp.zeros_like(lse_ref)

        for i, start, end in blocks:
            rows = slice(i * tq, end)
            s = _swa_scores(q_ref[rows, :], k_ref[start:end, :], i, tq, start, end)
            m = jnp.maximum(jnp.max(s, axis=1, keepdims=True), sink)
            p = jnp.exp(s - m)
            l = jnp.sum(p, axis=1, keepdims=True) + jnp.exp(sink - m)
            o = jnp.dot(p.astype(BF16), v_ref[start:end, :], preferred_element_type=F32) / l
            o_ref[rows, :] = o.astype(BF16)
            _put_lane(lse_ref, rows, h, m + jnp.log(l))

    stat = pl.BlockSpec((t, LANES), lambda h: (0, 0))
    return _pcall(body, name="swa_fwd",
                  out_shape=[jax.ShapeDtypeStruct((t, SWA_W), BF16), jax.ShapeDtypeStruct((t, LANES), F32)],
                  grid=(N_SWA,),
                  in_specs=[_head_spec(t, 0), _head_spec(t, N_SWA, GQA), _head_spec(t, N_SWA + N_KV, GQA),
                            pl.BlockSpec((1, LANES), lambda h: (0, 0))],
                  out_specs=[_head_spec(t, 0), stat])(proj_b, proj_b, proj_b, sinks128)


def _rope_bwd(d, cos, sin):
    return d * cos + pltpu.roll(d * sin, HEAD_DIM // 2, 1)


def _swa_bwd(proj_b, d_attn, lse, sinks128, cos, sin):
    t = proj_b.shape[0]
    tq, blocks = _swa_blocks(t)

    def body(q_ref, k_ref, v_ref, do_ref, lse_ref, sink_ref, cos_ref, sin_ref,
             dq_ref, dk_ref, dv_ref, dsink_ref, dk_acc, dv_acc):
        h = pl.program_id(0)
        sink = _pick_lane(sink_ref[...], h)
        lse_all = _pick_lane(lse_ref[...], h)

        @pl.when(h == 0)
        def _():
            dsink_ref[...] = jnp.zeros_like(dsink_ref)

        @pl.when(h % GQA == 0)
        def _():
            dk_acc[...] = jnp.zeros_like(dk_acc)
            dv_acc[...] = jnp.zeros_like(dv_acc)

        dsink = jnp.zeros((1, 1), F32)
        for i, start, end in blocks:
            rows = slice(i * tq, end)
            q, k, v, do = q_ref[rows, :], k_ref[start:end, :], v_ref[start:end, :], do_ref[rows, :]
            s = _swa_scores(q, k, i, tq, start, end)
            p = jnp.exp(s - lse_all[rows, :])
            dp = lax.dot_general(do, v, NT_DIMS, preferred_element_type=F32)
            delta = jnp.sum(p * dp, axis=1, keepdims=True)
            ds = (p * (dp - delta)).astype(BF16)
            dq = jnp.dot(ds, k, preferred_element_type=F32) * ATT_SCALE
            dq_ref[rows, :] = _rope_bwd(dq, cos_ref[rows, :], sin_ref[rows, :]).astype(BF16)
            dk_acc[start:end, :] += lax.dot_general(ds, q, TN_DIMS, preferred_element_type=F32)
            dv_acc[start:end, :] += lax.dot_general(p.astype(BF16), do, TN_DIMS, preferred_element_type=F32)
            dsink = dsink - jnp.sum(jnp.exp(sink - lse_all[rows, :]) * delta, axis=0, keepdims=True)
        old = dsink_ref[...]
        lane = lax.broadcasted_iota(jnp.int32, old.shape, 1)
        dsink_ref[...] = jnp.where(lane == h, dsink, old)

        @pl.when(h % GQA == GQA - 1)
        def _():
            dk_ref[...] = _rope_bwd(dk_acc[...] * ATT_SCALE, cos_ref[...], sin_ref[...]).astype(BF16)
            dv_ref[...] = dv_acc[...].astype(BF16)

    stat = pl.BlockSpec((t, LANES), lambda h: (0, 0))
    vec = pl.BlockSpec((1, LANES), lambda h: (0, 0))
    head = _head_spec(t, 0)
    kv_out = _head_spec(t, 0, GQA)
    return _pcall(body, name="swa_bwd",
                  out_shape=[jax.ShapeDtypeStruct((t, SWA_W), BF16), jax.ShapeDtypeStruct((t, KV_W), BF16),
                             jax.ShapeDtypeStruct((t, KV_W), BF16), jax.ShapeDtypeStruct((1, LANES), F32)],
                  grid=(N_SWA,),
                  in_specs=[head, _head_spec(t, N_SWA, GQA), _head_spec(t, N_SWA + N_KV, GQA),
                            _head_spec(t, N_FOX), stat, vec, stat, stat],
                  out_specs=[head, kv_out, kv_out, vec],
                  scratch_shapes=[pltpu.VMEM((t, HEAD_DIM), F32), pltpu.VMEM((t, HEAD_DIM), F32)],
                  )(proj_b, proj_b, proj_b, d_attn, lse, sinks128, cos, sin)


def _adamw(w, g, m, v):
    m = ADAM_B1 * m + (1.0 - ADAM_B1) * g
    v = ADAM_B2 * v + (1.0 - ADAM_B2) * (g * g)
    m_hat = m / (1.0 - ADAM_B1 ** ADAM_STEP)
    v_hat = v / (1.0 - ADAM_B2 ** ADAM_STEP)
    delta = -ADAM_LR * (m_hat / (jnp.sqrt(v_hat) + ADAM_EPS) + ADAM_WD * w)
    return delta, m, v


def _adam_pieces(w, m, v, own, land, idx, name):
    rows, cols = w.shape
    tr, tc = (256, cols) if rows % 256 == 0 else (rows, _blk(cols, 512))

    def body(idx_ref, own_ref, l1_ref, l2_ref, l3_ref, w_ref, m_ref, v_ref, g_ref, d_ref, mo_ref, vo_ref):
        g = own_ref[...].astype(F32) + l1_ref[...].astype(F32) + l2_ref[...].astype(F32) + l3_ref[...].astype(F32)
        g_ref[...] = g
        d_ref[...], mo_ref[...], vo_ref[...] = _adamw(w_ref[...], g, m_ref[...], v_ref[...])

    def piece(p):
        return pl.BlockSpec((None, tr, tc), lambda i, j, idx_ref: (idx_ref[p], i, j))

    tile = pl.BlockSpec((tr, tc), lambda i, j, idx_ref: (i, j))
    out = jax.ShapeDtypeStruct((rows, cols), F32)
    grid_spec = pltpu.PrefetchScalarGridSpec(
        num_scalar_prefetch=1, grid=(rows // tr, cols // tc),
        in_specs=[piece(0), piece(1), piece(2), piece(3), tile, tile, tile], out_specs=[tile] * 4)
    return _pcall(body, name=name, out_shape=[out] * 4, grid_spec=grid_spec)(idx, own, land, land, land, w, m, v)


def _adam_mod(c_all, dmod_cols, w, m, v):
    rows, cols = w.shape
    tr = _blk(rows, 256)

    def body(c_ref, dm_ref, w_ref, m_ref, v_ref, g_ref, d_ref, mo_ref, vo_ref):
        cond = _silu(c_ref[...]).astype(BF16)
        g = lax.dot_general(cond, dm_ref[...].astype(BF16), TN_DIMS, preferred_element_type=F32)
        g_ref[...] = g
        d_ref[...], mo_ref[...], vo_ref[...] = _adamw(w_ref[...], g, m_ref[...], v_ref[...])

    tile = pl.BlockSpec((tr, cols), lambda i: (i, 0))
    out = jax.ShapeDtypeStruct((rows, cols), F32)
    return _pcall(body, name="adam_mod", out_shape=[out] * 4, grid=(rows // tr,),
                  in_specs=[pl.BlockSpec((N_DEV, tr), lambda i: (0, i)), pl.BlockSpec((N_DEV, cols), lambda i: (0, 0)),
                            tile, tile, tile],
                  out_specs=[tile] * 4)(c_all, dmod_cols, w, m, v)


def _adam_small(parts, w, m, v):
    nv = w.shape[1]

    def body(p_ref, w_ref, m_ref, v_ref, g_ref, d_ref, mo_ref, vo_ref):
        g = p_ref[0:1, :]
        for k in range(1, N_DEV):
            g = g + p_ref[k:k + 1, :]
        g_ref[...] = g
        d_ref[...], mo_ref[...], vo_ref[...] = _adamw(w_ref[...], g, m_ref[...], v_ref[...])

    vec = pl.BlockSpec((1, nv), lambda i: (0, 0))
    out = jax.ShapeDtypeStruct((1, nv), F32)
    return _pcall(body, name="adam_small", out_shape=[out] * 4, grid=(1,),
                  in_specs=[pl.BlockSpec((N_DEV, nv), lambda i: (0, 0)), vec, vec, vec],
                  out_specs=[vec] * 4)(parts, w, m, v)


def _pad_lanes(v, width=LANES):
    return jnp.pad(v, ((0, 0), (0, width - v.shape[1])))


def kernel(x, c, w_mod, b_mod, g_pre_mix, g_post_mix, w_in, b_forget, swa_sinks, w_out, g_pre_mlp, g_post_mlp, w_up, w_down, loss_target, m_w_mod, m_b_mod, m_g_pre_mix, m_g_post_mix, m_w_in, m_b_forget, m_swa_sinks, m_w_out, m_g_pre_mlp, m_g_post_mlp, m_w_up, m_w_down, v_w_mod, v_b_mod, v_g_pre_mix, v_g_post_mix, v_w_in, v_b_forget, v_swa_sinks, v_w_out, v_g_pre_mlp, v_g_post_mlp, v_w_up, v_w_down):
    ax, ay, ac = _position()
    me = 4 * ax + 2 * ay + ac
    x, target = x[0], loss_target[0]
    t, d = x.shape
    w_mod, w_in, w_out, w_up, w_down = w_mod[0], w_in[0], w_out[0], w_up[0], w_down[0]
    mod_w = w_mod.shape[1]
    in_w = w_in.shape[1]
    in_total = N_DEV * in_w
    shard_ff = w_up.shape[1]
    n_fox3 = 3 * FOX_W
    n_swa3 = SWA_W + 2 * KV_W
    assert in_total == n_fox3 + N_FOX + n_swa3 and d == FOX_W + SWA_W

    c_all = _all_gather([c], "gather_c")[0].reshape(N_DEV, d)
    b_part = lax.dynamic_slice(b_mod, (0, me * mod_w), (1, mod_w))
    mod_parts = _all_gather([_mod_part(c_all, w_mod, b_part)], "gather_mod")[0]
    mod = lax.dynamic_index_in_dim(mod_parts, me, axis=1, keepdims=False).reshape(1, N_DEV * mod_w)

    w_in_b, mod = lax.optimization_barrier((w_in.astype(BF16), mod))
    first = _ag_start([w_in_b], "ag_start_in")
    behind_first = first[4][0, 0]
    rest = _ag_start([(w + behind_first).astype(BF16) for w in (w_out, w_up, w_down)], "ag_start_rest")
    ag_send, ag_recv, ag_shard, ag_land = [a + b for a, b in zip(first[:4], rest[:4])]
    ag_token = rest[4]

    def gathered(i, after, name):
        shard, land = _ag_wait(ag_send[i], ag_recv[i], ag_shard[i], ag_land[i], after, "ag_wait_" + name)
        return lax.dynamic_update_slice(_ag_forward(land, "ag_fwd_" + name), shard[None], (me, 0, 0))

    def gathered_start(i, after, name):
        shard, land = _ag_wait(ag_send[i], ag_recv[i], ag_shard[i], ag_land[i], after, "ag_wait_" + name)
        return shard, _fwd_start(land, "fwd_start_" + name)

    def gathered_finish(started, after, name):
        shard, (send, recv, land, _) = started
        return lax.dynamic_update_slice(_fwd_wait(send, recv, land, after, "fwd_wait_" + name), shard[None], (me, 0, 0))

    sh_a, sc_a, gt_a, sh_m, sc_m, gt_m = [mod[:, i * d:(i + 1) * d] for i in range(6)]

    half = HEAD_DIM // 2
    inv_freq = 1.0 / (ROPE_THETA ** (jnp.arange(half, dtype=F32) * (2.0 / HEAD_DIM)))
    ang = jnp.arange(t).astype(F32)[:, None] * inv_freq[None, :]
    cos = jnp.concatenate([jnp.cos(ang), jnp.cos(ang)], axis=1)
    sin = jnp.concatenate([-jnp.sin(ang), jnp.sin(ang)], axis=1)

    b128 = _pad_lanes(b_forget)
    sinks128 = _pad_lanes(swa_sinks)

    h1 = _pre_attn(x, g_pre_mix + ag_token[0:1, 0:1], sc_a, sh_a)
    h1, adam_in = lax.optimization_barrier(
        (h1, (jnp.transpose(w_in), jnp.transpose(m_w_in[0]), jnp.transpose(v_w_in[0]))))
    w_in_g = gathered(0, h1, "in")
    o_fg, o_sq = n_fox3, n_fox3 + N_FOX

    def cols(lo, hi):
        parts = []
        for j in range(lo // in_w, (hi - 1) // in_w + 1):
            parts.append(w_in_g[j, :, max(lo - j * in_w, 0):min(hi - j * in_w, in_w)])
        return parts

    w_in_r = jnp.concatenate(cols(0, o_fg) + cols(o_sq, in_total) + cols(o_fg, o_sq)
                             + [jnp.zeros((d, FG_PAD - N_FOX), BF16)], axis=1)
    proj_a = _matmul(h1, w_in_r, name="proj_a", n_cols=n_fox3, n_off=0)
    proj_b = _matmul(h1, w_in_r, name="proj_b", n_cols=n_swa3, n_off=n_fox3, tn=512, row_extras=(cos, sin),
                     epilogue=lambda acc, j, cs, sn: (_rope_cols(acc, j, cs, sn, SWA_W + KV_W),))
    out_started = gathered_start(1, proj_a, "out")
    fg = _matmul(h1, w_in_r, name="proj_fg", n_cols=FG_PAD, n_off=n_fox3 + n_swa3, tn=FG_PAD,
                 out_dtypes=(F32,), after=out_started[1][3])[:, 0:LANES]
    cum, cumt = _cum_fwd(fg, b128)
    fox_o, fox_lse = _fox_fwd(proj_a, cum, cumt)
    up_started = gathered_start(2, fox_o, "up")
    swa_o, swa_lse = _swa_fwd(proj_b, sinks128 + up_started[1][3][0:1, 0:1])
    w_out_full = gathered_finish(out_started, swa_o, "out").reshape(d, d)
    attn = jnp.concatenate([fox_o, swa_o], axis=1)
    mix = _matmul(attn, w_out_full, name="out_proj", out_dtypes=(F32,))
    x2, h2 = _post_mix(x, mix, gt_a, g_post_mix, g_pre_mlp, sc_m, sh_m)
    w_up_g = gathered_finish(up_started, h2, "up")
    u, act = _matmul(h2, w_up_g, name="mlp_up", b_sharded=True, out_dtypes=(BF16, BF16),
                     epilogue=lambda acc, j: (acc, jnp.square(jnp.maximum(acc, 0.0))))
    w_down_full = gathered(3, act, "down").reshape(N_DEV * shard_ff, d)
    y = _matmul(act, w_down_full, name="mlp_down", tk=4096, out_dtypes=(F32,))

    core = jnp.reshape(ac, (1,)).astype(jnp.int32)

    def reduce_start(started, after, name):
        send, recv, src, land, _ = started
        full, from_sibling = _sib_wait(send, recv, src, land, after, "sib_wait_" + name)
        return _rs_start(_chip_sum(full, from_sibling, core, "chip_sum_" + name), "rs_start_" + name)

    def tok(started):
        return started[4][0:1, 0:1]

    idx = jnp.stack([2 * ax + ay, 2 * (1 - ax) + ay, 2 * ax + (1 - ay), 2 * (1 - ax) + (1 - ay)]).astype(jnp.int32)

    def reduce_finish(started, after, w, m, v, name):
        send, recv, src, land, _ = started
        own, landed = _rs_wait(send, recv, src, land, after, "rs_wait_" + name)
        return _adam_pieces(w, m[0], v[0], own, landed, idx, "adam_" + name)

    dy, dout, dgt_m, dg3, loss_vec = _final(y, x2, target, gt_m, g_post_mlp)
    du = _matmul(dy, w_down_full, name="d_act", tb=True, tile_extras=(u,),
                 epilogue=lambda acc, j, uu: (acc * (2.0 * jnp.maximum(uu.astype(F32), 0.0)),))
    dw_down = _matmul(act, dy, name="dw_down", ta=True)
    sb_down = _sib_start(dw_down.reshape(N_DEV, shard_ff, d), "sib_start_down")
    dh2 = _matmul(du, w_up_g, name="d_h2", tb=True, b_sharded=True, b_pair=True, after=sb_down[4])
    rs_down = reduce_start(sb_down, dh2, "down")
    per = shard_ff // _blk(shard_ff, 1024)
    dw_up = _matmul(h2, du, name="dw_up", ta=True, tn=_blk(shard_ff, 1024), out_shape=(N_DEV, d, shard_ff),
                    out_map=lambda tm, tn: pl.BlockSpec((None, tm, tn), lambda i, j, kk: (j // per, i, j % per)),
                    after=rs_down[4])
    sb_up = _sib_start(dw_up, "sib_start_up")
    dmix, dx2, dsh_m, dsc_m, dg2, dgt_a, dg1 = _mid_bwd(
        dh2, dout, x2, mix, g_pre_mlp + tok(sb_up), sc_m, gt_a, g_post_mix)
    d_attn = _matmul(dmix, w_out_full, name="d_attn", tb=True)
    rs_up = reduce_start(sb_up, d_attn, "up")
    dw_out = _matmul(attn, dmix, name="dw_out", ta=True, after=rs_up[4])
    sb_out = _sib_start(dw_out.reshape(N_DEV, d // N_DEV, d), "sib_start_out")
    dqf, dkf, dvf, dcs = _fox_bwd(proj_a, d_attn, cum, cumt, fox_lse)
    dsq, dsk, dsv, dsinks = _swa_bwd(proj_b, d_attn, swa_lse, sinks128 + tok(sb_out), cos, sin)
    rs_out = reduce_start(sb_out, dsq, "out")
    dfg, db_forget = _fg_bwd(dcs, fg, b128 + tok(rs_out))
    dproj = jnp.concatenate([dqf, dkf, dvf, dsq, dsk, dsv, _pad_lanes(dfg, FG_PAD)], axis=1)
    dw_in_r = _matmul(dproj, h1, name="dw_in", ta=True)

    def shard_rows(j):
        lo, hi = j * in_w, (j + 1) * in_w
        parts = []
        for seg_lo, seg_hi, shift in ((0, o_fg, 0), (o_fg, o_sq, n_swa3), (o_sq, in_total, -N_FOX)):
            a, b = max(lo, seg_lo), min(hi, seg_hi)
            if a < b:
                parts.append(dw_in_r[a + shift:b + shift, :])
        return parts[0] if len(parts) == 1 else jnp.concatenate(parts, axis=0)

    sb_in = _sib_start(jnp.stack([shard_rows(j) for j in range(N_DEV)]), "sib_start_in")
    dh1 = _matmul(dproj, w_in_r, name="d_h1", tb=True, tk=2560, after=sb_in[4])
    grad_x, dsh_a, dsc_a, dg0 = _x_bwd(dh1, dx2, x, g_pre_mix, sc_a)

    small = jnp.concatenate([dsh_a, dsc_a, dgt_a, dsh_m, dsc_m, dgt_m, dg0, dg1, dg2, dg3, db_forget, dsinks,
                             loss_vec[:, 0:LANES]], axis=1)
    small_all = _all_gather([small], "gather_small")[0].reshape(N_DEV, small.shape[1])
    rs_in = reduce_start(sb_in, small_all, "in")

    pack = lambda bm, g0_, g1_, g2_, g3_, bf_, sk_: jnp.concatenate(
        [bm, g0_, g1_, g2_, g3_, _pad_lanes(bf_), _pad_lanes(sk_), jnp.zeros((1, LANES), F32)], axis=1)
    p_small = pack(b_mod, g_pre_mix, g_post_mix, g_pre_mlp, g_post_mlp, b_forget, swa_sinks)
    m_small = pack(m_b_mod, m_g_pre_mix, m_g_post_mix, m_g_pre_mlp, m_g_post_mlp, m_b_forget, m_swa_sinks)
    v_small = pack(v_b_mod, v_g_pre_mix, v_g_post_mix, v_g_pre_mlp, v_g_post_mlp, v_b_forget, v_swa_sinks)
    small_out = _adam_small(small_all, p_small + tok(rs_in), m_small, v_small)

    n_mod = 6 * d

    def unpack(vec):
        o = n_mod
        return (vec[:, 0:n_mod], vec[:, o:o + d], vec[:, o + d:o + 2 * d], vec[:, o + 2 * d:o + 3 * d],
                vec[:, o + 3 * d:o + 4 * d], vec[:, o + 4 * d:o + 4 * d + N_FOX],
                vec[:, o + 4 * d + LANES:o + 4 * d + LANES + N_SWA])

    loss = small_out[0][0, n_mod + 4 * d + 2 * LANES]
    g_small, d_small, nm_small, nv_small = [unpack(vec) for vec in small_out]

    dmod_cols = lax.dynamic_slice(small_all, (0, me * mod_w), (N_DEV, mod_w))
    g_w_mod, d_w_mod, nm_w_mod, nv_w_mod = _adam_mod(c_all + tok(rs_in), dmod_cols, w_mod, m_w_mod[0], v_w_mod[0])

    g_w_down, d_w_down, nm_w_down, nv_w_down = reduce_finish(rs_down, d_w_mod, w_down, m_w_down, v_w_down, "w_down")
    g_w_up, d_w_up, nm_w_up, nv_w_up = reduce_finish(rs_up, d_w_down, w_up, m_w_up, v_w_up, "w_up")
    g_w_out, d_w_out, nm_w_out, nv_w_out = reduce_finish(rs_out, d_w_up, w_out, m_w_out, v_w_out, "w_out")
    g_w_in, d_w_in, nm_w_in, nv_w_in = reduce_finish(rs_in, d_w_out, adam_in[0], (adam_in[1],), (adam_in[2],), "w_in")

    def assemble(w_mod_, small_, w_in_, w_out_, w_up_, w_down_):
        b_mod_, g0_, g1_, g2_, g3_, bf_, sk_ = small_
        return [w_mod_[None], b_mod_, g0_, g1_, jnp.transpose(w_in_)[None], bf_, sk_, w_out_[None], g2_, g3_,
                w_up_[None], w_down_[None]]

    outs = [loss, grad_x[None]]
    outs += assemble(g_w_mod, g_small, g_w_in, g_w_out, g_w_up, g_w_down)
    outs += assemble(d_w_mod, d_small, d_w_in, d_w_out, d_w_up, d_w_down)
    outs += assemble(nm_w_mod, nm_small, nm_w_in, nm_w_out, nm_w_up, nm_w_down)
    outs += assemble(nv_w_mod, nv_small, nv_w_in, nv_w_out, nv_w_up, nv_w_down)
    return tuple(outs)
```

```python
import jax
import jax.numpy as jnp
from jax import lax
from jax.experimental import pallas as pl
from jax.experimental.pallas import tpu as pltpu

F32 = jnp.float32
BF16 = jnp.bfloat16
MESH = pl.DeviceIdType.MESH

N_DEV = 8
N_CHIP = 4
LANES = 128
HEAD_DIM = 128
N_FOX = 8
N_SWA = 8
N_KV = 2
GQA = N_SWA // N_KV
WINDOW = 128
FOX_W = N_FOX * HEAD_DIM
SWA_W = N_SWA * HEAD_DIM
KV_W = N_KV * HEAD_DIM
ROPE_THETA = 10000.0
NORM_EPS = 1e-6
ATT_SCALE = HEAD_DIM ** -0.5
FG_PAD = 512

ADAM_LR = 0.001
ADAM_B1 = 0.9
ADAM_B2 = 0.999
ADAM_EPS = 1e-08
ADAM_WD = 0.01
ADAM_STEP = 10

VMEM_LIMIT = 56 * 1024 * 1024

NT_DIMS = (((1,), (1,)), ((), ()))
TN_DIMS = (((0,), (0,)), ((), ()))


def _pcall(body, *, name, out_shape, grid=(), in_specs=None, out_specs=None, scratch_shapes=(), grid_spec=None):
    params = pltpu.CompilerParams(vmem_limit_bytes=VMEM_LIMIT)
    if grid_spec is not None:
        return pl.pallas_call(body, name=name, out_shape=out_shape, grid_spec=grid_spec, compiler_params=params)
    return pl.pallas_call(body, name=name, out_shape=out_shape, grid=grid, in_specs=in_specs, out_specs=out_specs,
                          scratch_shapes=scratch_shapes, compiler_params=params)


def _blk(n, pref):
    if n <= pref:
        return n
    b = (pref // LANES) * LANES
    while n % b:
        b -= LANES
    return b


def _position():
    return lax.axis_index("x"), lax.axis_index("y"), lax.axis_index("c")


ANY = pl.BlockSpec(memory_space=pl.ANY)


def _all_gather(arrs, name):
    n = len(arrs)

    def body(*refs):
        ins, outs = refs[:n], refs[n:2 * n]
        send_sems, recv_sems, local_sems = refs[2 * n:]
        x, y, c = _position()
        me, sibling = (x, y, c), (x, y, 1 - c)
        chips = [(1 - x, y), (x, 1 - y), (1 - x, 1 - y)]

        def slot(p):
            return 4 * p[0] + 2 * p[1] + p[2]

        def copy(a, k, block, to, src=None):
            dst = outs[a].at[slot(block)]
            return pltpu.make_async_remote_copy(
                src_ref=dst if src is None else src, dst_ref=dst,
                send_sem=send_sems.at[7 * a + k], recv_sem=recv_sems.at[7 * a + k],
                device_id=to, device_id_type=MESH)

        mine = [pltpu.make_async_copy(ins[a], outs[a].at[slot(me)], local_sems.at[a]) for a in range(n)]
        for cp in mine:
            cp.start()
        first = []
        for a in range(n):
            first.append(copy(a, 0, me, sibling, src=ins[a]))
            first += [copy(a, 1 + j, me, (*chip, c), src=ins[a]) for j, chip in enumerate(chips)]
        for cp in first:
            cp.start()
        passed = []
        for a in range(n):
            for j, chip in enumerate(chips):
                copy(a, 1 + j, (*chip, c), me).wait_recv()
                cp = copy(a, 4 + j, (*chip, c), sibling)
                cp.start()
                passed.append(cp)
        for a in range(n):
            copy(a, 0, sibling, me).wait_recv()
            for j, chip in enumerate(chips):
                copy(a, 4 + j, (*chip, 1 - c), me).wait_recv()
        for cp in first + passed:
            cp.wait_send()
        for cp in mine:
            cp.wait()

    return _pcall(
        body, name=name,
        out_shape=[jax.ShapeDtypeStruct((N_DEV,) + a.shape, a.dtype) for a in arrs],
        in_specs=[ANY] * n, out_specs=[ANY] * n,
        scratch_shapes=[pltpu.SemaphoreType.DMA((7 * n,)), pltpu.SemaphoreType.DMA((7 * n,)),
                        pltpu.SemaphoreType.DMA((n,))],
    )(*arrs)


HBM = pl.BlockSpec(memory_space=pltpu.HBM)
SEM = pl.BlockSpec(memory_space=pltpu.SEMAPHORE)
EFFECT = pltpu.SideEffectType.DATAFLOW_SIDE_EFFECTING


def _hbm(a):
    return pltpu.with_memory_space_constraint(a, pltpu.HBM)


def _gather_peers():
    x, y, c = _position()
    return [(x, y, 1 - c), (1 - x, y, c), (x, 1 - y, c), (1 - x, 1 - y, c)]


def _ag_start(shards, name):
    n = len(shards)
    lands = [_hbm(lax.empty((N_DEV,) + s.shape, s.dtype)) for s in shards]

    def body(*refs):
        srcs, land, send, recv = refs[:n], refs[n:2 * n], refs[2 * n:3 * n], refs[3 * n:4 * n]
        token = refs[6 * n]
        x, y, c = _position()
        for a in range(n):
            for k, to in enumerate(_gather_peers()):
                pltpu.make_async_remote_copy(
                    src_ref=srcs[a], dst_ref=land[a].at[4 * x + 2 * y + c], send_sem=send[a].at[k],
                    recv_sem=recv[a].at[k], device_id=to, device_id_type=MESH).start()
        token[...] = jnp.zeros_like(token)

    sems = [pltpu.SemaphoreType.DMA((4,))] * (2 * n)
    out = pl.pallas_call(
        body, name=name,
        out_shape=sems + [pltpu.HBM(s.shape, s.dtype) for s in shards] + [pltpu.HBM(l.shape, l.dtype) for l in lands]
        + [jax.ShapeDtypeStruct((8, LANES), F32)],
        in_specs=[HBM] * (2 * n), out_specs=[SEM] * (2 * n) + [HBM] * (2 * n) + [pl.BlockSpec(memory_space=pltpu.VMEM)],
        input_output_aliases={**{a: 2 * n + a for a in range(n)}, **{n + a: 3 * n + a for a in range(n)}},
        compiler_params=pltpu.CompilerParams(has_side_effects=EFFECT),
    )(*[_hbm(s) for s in shards], *lands)
    return out[:n], out[n:2 * n], out[2 * n:3 * n], out[3 * n:4 * n], out[4 * n]


def _ag_wait(send, recv, shard_thru, land_thru, after, name):
    def body(v_ref, land_ref, send_sem, recv_sem, after_ref, v_dead, got_ref):
        for k, to in enumerate(_gather_peers()):
            cp = pltpu.make_async_remote_copy(
                src_ref=v_ref, dst_ref=land_ref.at[0], send_sem=send_sem.at[k], recv_sem=recv_sem.at[k],
                device_id=to, device_id_type=MESH)
            cp.wait_send()
            cp.wait_recv()

    return pl.pallas_call(
        body, name=name,
        out_shape=(pltpu.HBM(shard_thru.shape, shard_thru.dtype), pltpu.HBM(land_thru.shape, land_thru.dtype)),
        in_specs=(HBM, HBM, SEM, SEM, ANY), out_specs=(HBM, HBM), input_output_aliases={0: 0, 1: 1},
        compiler_params=pltpu.CompilerParams(has_side_effects=EFFECT),
    )(shard_thru, land_thru, send, recv, after)


def _ag_forward(land, name):
    def body(land_in, land_ref, send_sems, recv_sems):
        x, y, c = _position()
        copies = []
        for j, (px, py) in enumerate([(1 - x, y), (x, 1 - y), (1 - x, 1 - y)]):
            block = land_ref.at[4 * px + 2 * py + c]
            cp = pltpu.make_async_remote_copy(src_ref=block, dst_ref=block, send_sem=send_sems.at[j],
                                              recv_sem=recv_sems.at[j], device_id=(x, y, 1 - c), device_id_type=MESH)
            cp.start()
            copies.append(cp)
        for cp in copies:
            cp.wait()

    return pl.pallas_call(
        body, name=name, out_shape=jax.ShapeDtypeStruct(land.shape, land.dtype),
        in_specs=[ANY], out_specs=ANY, input_output_aliases={0: 0},
        scratch_shapes=[pltpu.SemaphoreType.DMA((3,)), pltpu.SemaphoreType.DMA((3,))],
    )(land)


def _fwd_start(land, name):
    def body(land_ref, send, recv, land_thru, token):
        x, y, c = _position()
        for j, (px, py) in enumerate([(1 - x, y), (x, 1 - y), (1 - x, 1 - y)]):
            block = land_ref.at[4 * px + 2 * py + c]
            pltpu.make_async_remote_copy(src_ref=block, dst_ref=block, send_sem=send.at[j], recv_sem=recv.at[j],
                                         device_id=(x, y, 1 - c), device_id_type=MESH).start()
        token[...] = jnp.zeros_like(token)

    return pl.pallas_call(
        body, name=name,
        out_shape=[pltpu.SemaphoreType.DMA((3,)), pltpu.SemaphoreType.DMA((3,)), pltpu.HBM(land.shape, land.dtype),
                   jax.ShapeDtypeStruct((8, LANES), F32)],
        in_specs=[HBM], out_specs=[SEM, SEM, HBM, pl.BlockSpec(memory_space=pltpu.VMEM)],
        input_output_aliases={0: 2},
        compiler_params=pltpu.CompilerParams(has_side_effects=EFFECT),
    )(land)


def _fwd_wait(send, recv, land_thru, after, name):
    def body(land_ref, send_sem, recv_sem, after_ref, land_out):
        x, y, c = _position()
        for j in range(3):
            cp = pltpu.make_async_remote_copy(
                src_ref=land_ref.at[0], dst_ref=land_ref.at[0], send_sem=send_sem.at[j], recv_sem=recv_sem.at[j],
                device_id=(x, y, 1 - c), device_id_type=MESH)
            cp.wait_send()
            cp.wait_recv()

    return pl.pallas_call(
        body, name=name, out_shape=pltpu.HBM(land_thru.shape, land_thru.dtype),
        in_specs=(HBM, SEM, SEM, ANY), out_specs=HBM, input_output_aliases={0: 0},
        compiler_params=pltpu.CompilerParams(has_side_effects=EFFECT),
    )(land_thru, send, recv, after)


def _rs_peers():
    x, y, c = _position()
    return [(1 - x, y, c), (x, 1 - y, c), (1 - x, 1 - y, c)]


def _rs_start(chip_sums, name):
    land = _hbm(lax.empty(chip_sums.shape, chip_sums.dtype))

    def body(src, land_ref, send, recv, src_thru, land_thru, token):
        x, y, c = _position()
        for j, (px, py, pc) in enumerate(_rs_peers()):
            pltpu.make_async_remote_copy(
                src_ref=src.at[2 * px + py], dst_ref=land_ref.at[2 * x + y], send_sem=send.at[j], recv_sem=recv.at[j],
                device_id=(px, py, pc), device_id_type=MESH).start()
        token[...] = jnp.zeros_like(token)

    return pl.pallas_call(
        body, name=name,
        out_shape=[pltpu.SemaphoreType.DMA((3,)), pltpu.SemaphoreType.DMA((3,)),
                   pltpu.HBM(chip_sums.shape, chip_sums.dtype), pltpu.HBM(land.shape, land.dtype),
                   jax.ShapeDtypeStruct((8, LANES), F32)],
        in_specs=[HBM, HBM], out_specs=[SEM, SEM, HBM, HBM, pl.BlockSpec(memory_space=pltpu.VMEM)],
        input_output_aliases={0: 2, 1: 3},
        compiler_params=pltpu.CompilerParams(has_side_effects=EFFECT),
    )(_hbm(chip_sums), land)


def _rs_wait(send, recv, src_thru, land_thru, after, name):
    def body(src, land_ref, send_sem, recv_sem, after_ref, src_out, land_out):
        for j, to in enumerate(_rs_peers()):
            cp = pltpu.make_async_remote_copy(
                src_ref=src.at[0], dst_ref=land_ref.at[0], send_sem=send_sem.at[j], recv_sem=recv_sem.at[j],
                device_id=to, device_id_type=MESH)
            cp.wait_send()
            cp.wait_recv()

    return pl.pallas_call(
        body, name=name,
        out_shape=(pltpu.HBM(src_thru.shape, src_thru.dtype), pltpu.HBM(land_thru.shape, land_thru.dtype)),
        in_specs=(HBM, HBM, SEM, SEM, ANY), out_specs=(HBM, HBM), input_output_aliases={0: 0, 1: 1},
        compiler_params=pltpu.CompilerParams(has_side_effects=EFFECT),
    )(src_thru, land_thru, send, recv, after)


def _sib_start(full, name):
    land = _hbm(lax.empty((N_CHIP,) + full.shape[1:], full.dtype))

    def body(src, land_ref, send, recv, src_thru, land_thru, token):
        x, y, c = _position()
        for k in range(N_CHIP):
            pltpu.make_async_remote_copy(
                src_ref=src.at[2 * k + (1 - c)], dst_ref=land_ref.at[k], send_sem=send.at[k], recv_sem=recv.at[k],
                device_id=(x, y, 1 - c), device_id_type=MESH).start()
        token[...] = jnp.zeros_like(token)

    return pl.pallas_call(
        body, name=name,
        out_shape=[pltpu.SemaphoreType.DMA((N_CHIP,)), pltpu.SemaphoreType.DMA((N_CHIP,)),
                   pltpu.HBM(full.shape, full.dtype), pltpu.HBM(land.shape, land.dtype),
                   jax.ShapeDtypeStruct((8, LANES), F32)],
        in_specs=[HBM, HBM], out_specs=[SEM, SEM, HBM, HBM, pl.BlockSpec(memory_space=pltpu.VMEM)],
        input_output_aliases={0: 2, 1: 3},
        compiler_params=pltpu.CompilerParams(has_side_effects=EFFECT),
    )(_hbm(full), land)


def _sib_wait(send, recv, src_thru, land_thru, after, name):
    def body(src, land_ref, send_sem, recv_sem, after_ref, src_out, land_out):
        x, y, c = _position()
        for k in range(N_CHIP):
            cp = pltpu.make_async_remote_copy(
                src_ref=src.at[0], dst_ref=land_ref.at[0], send_sem=send_sem.at[k], recv_sem=recv_sem.at[k],
                device_id=(x, y, 1 - c), device_id_type=MESH)
            cp.wait_send()
            cp.wait_recv()

    return pl.pallas_call(
        body, name=name,
        out_shape=(pltpu.HBM(src_thru.shape, src_thru.dtype), pltpu.HBM(land_thru.shape, land_thru.dtype)),
        in_specs=(HBM, HBM, SEM, SEM, ANY), out_specs=(HBM, HBM), input_output_aliases={0: 0, 1: 1},
        compiler_params=pltpu.CompilerParams(has_side_effects=EFFECT),
    )(src_thru, land_thru, send, recv, after)


def _chip_sum(full, recv, core, name):
    _, rows, cols = full.shape
    tr = _blk(rows, 1024) if rows % LANES == 0 else rows

    def body(core_ref, a_ref, b_ref, o_ref):
        o_ref[...] = (a_ref[...].astype(F32) + b_ref[...].astype(F32)).astype(o_ref.dtype)

    grid_spec = pltpu.PrefetchScalarGridSpec(
        num_scalar_prefetch=1, grid=(N_CHIP, rows // tr),
        in_specs=[pl.BlockSpec((None, tr, cols), lambda k, i, core_ref: (2 * k + core_ref[0], i, 0)),
                  pl.BlockSpec((None, tr, cols), lambda k, i, core_ref: (k, i, 0))],
        out_specs=pl.BlockSpec((None, tr, cols), lambda k, i, core_ref: (k, i, 0)))
    return _pcall(body, name=name, out_shape=jax.ShapeDtypeStruct((N_CHIP, rows, cols), full.dtype),
                  grid_spec=grid_spec)(core, full, recv)


def _matmul(a, b, *, name, ta=False, tb=False, tm=1024, tn=1024, tk=2048, out_dtypes=(BF16,), epilogue=None,
            row_extras=(), tile_extras=(), out_shape=None, out_map=None, b_sharded=False, n_cols=None, n_off=0,
            after=None, b_pair=False):
    m, k = (a.shape[1], a.shape[0]) if ta else a.shape
    if b_sharded:
        shard_c = b.shape[2]
        n, kb = (b.shape[1], N_DEV * shard_c) if tb else (N_DEV * shard_c, b.shape[1])
        tn, tk = (tn, min(tk, shard_c)) if tb else (min(tn, shard_c), tk)
        if b_pair:
            assert tb and tk == shard_c
            tk = 2 * shard_c
    else:
        n, kb = b.shape if tb else (b.shape[1], b.shape[0])
    assert kb == k, (name, kb, k)
    if n_cols is not None:
        n = n_cols
    tm, tn, tk = _blk(m, tm), _blk(n, tn), _blk(k, tk)
    assert n_off % tn == 0
    nk = k // tk
    dims = (((0 if ta else 1,), (1 if tb else 0,)), ((), ()))
    behind = () if after is None else (after,)
    n_row, n_tile, n_out = len(row_extras), len(tile_extras), len(out_dtypes)
    n_b = 2 if b_pair else 1
    first_out = 1 + n_b + n_row + n_tile + len(behind)

    def body(*refs):
        a_ref, b_ref = refs[:2]
        extras = refs[1 + n_b:1 + n_b + n_row + n_tile]
        outs = refs[first_out:first_out + n_out]
        acc_ref = refs[-1]
        jj, kk = pl.program_id(1), pl.program_id(2)
        if b_pair:
            half = tk // 2
            part = (lax.dot_general(a_ref[:, 0:half].astype(BF16), b_ref[...].astype(BF16), dims,
                                    preferred_element_type=F32)
                    + lax.dot_general(a_ref[:, half:tk].astype(BF16), refs[2][...].astype(BF16), dims,
                                      preferred_element_type=F32))
        else:
            part = lax.dot_general(a_ref[...].astype(BF16), b_ref[...].astype(BF16), dims,
                                   preferred_element_type=F32)

        def finish(acc):
            res = (acc,) if epilogue is None else epilogue(acc, jj, *[e[...] for e in extras])
            for o_ref, r in zip(outs, res):
                o_ref[...] = r.astype(o_ref.dtype)

        if nk == 1:
            finish(part)
        else:
            @pl.when(kk == 0)
            def _():
                acc_ref[...] = part

            @pl.when(kk > 0)
            def _():
                acc_ref[...] += part

            @pl.when(kk == nk - 1)
            def _():
                finish(acc_ref[...])

    a_spec = pl.BlockSpec((tk, tm), lambda i, j, kk: (kk, i)) if ta else pl.BlockSpec((tm, tk), lambda i, j, kk: (i, kk))
    if b_pair:
        b_spec = pl.BlockSpec((None, tn, shard_c), lambda i, j, kk: (2 * kk, j, 0))
    elif b_sharded and tb:
        per = shard_c // tk
        b_spec = pl.BlockSpec((None, tn, tk), lambda i, j, kk: (kk // per, j, kk % per))
    elif b_sharded:
        per = shard_c // tn
        b_spec = pl.BlockSpec((None, tk, tn), lambda i, j, kk: (j // per, kk, j % per))
    elif tb:
        b_spec = pl.BlockSpec((tn, tk), lambda i, j, kk: (j + n_off // tn, kk))
    else:
        b_spec = pl.BlockSpec((tk, tn), lambda i, j, kk: (kk, j + n_off // tn))
    in_specs = [a_spec, b_spec]
    if b_pair:
        in_specs.append(pl.BlockSpec((None, tn, shard_c), lambda i, j, kk: (2 * kk + 1, j, 0)))
    in_specs += [pl.BlockSpec((tm, LANES), lambda i, j, kk: (i, 0)) for _ in row_extras]
    in_specs += [pl.BlockSpec((tm, tn), lambda i, j, kk: (i, j)) for _ in tile_extras]
    in_specs += [ANY for _ in behind]
    if out_map is None:
        out_specs = [pl.BlockSpec((tm, tn), lambda i, j, kk: (i, j)) for _ in out_dtypes]
        shapes = [jax.ShapeDtypeStruct((m, n), dt) for dt in out_dtypes]
    else:
        out_specs = [out_map(tm, tn)]
        shapes = [jax.ShapeDtypeStruct(out_shape, out_dtypes[0])]
    acc_shape = (tm, tn) if nk > 1 else (8, LANES)
    res = _pcall(body, name=name, out_shape=shapes, grid=(m // tm, n // tn, nk), in_specs=in_specs,
                 out_specs=out_specs, scratch_shapes=[pltpu.VMEM(acc_shape, F32)])(
                     a, *([b, b] if b_pair else [b]), *row_extras, *tile_extras, *behind)
    return res[0] if n_out == 1 else res


def _rope_cols(acc, j, cos, sin, n_rope):
    width = acc.shape[1]
    parts = []
    for g in range(width // HEAD_DIM):
        xg = acc[:, g * HEAD_DIM:(g + 1) * HEAD_DIM]
        roped = xg * cos + pltpu.roll(xg, HEAD_DIM // 2, 1) * sin
        parts.append(jnp.where(j * width + g * HEAD_DIM < n_rope, roped, xg))
    return jnp.concatenate(parts, axis=1) if len(parts) > 1 else parts[0]


def _silu(v):
    return v / (1.0 + jnp.exp(-v))


def _mod_part(c_all, w_mod, b_part):
    d, w = w_mod.shape
    tk = _blk(d, 512)

    def body(c_ref, w_ref, b_ref, o_ref):
        kk = pl.program_id(0)
        cond = _silu(c_ref[...]).astype(BF16)
        part = jnp.dot(cond, w_ref[...].astype(BF16), preferred_element_type=F32)

        @pl.when(kk == 0)
        def _():
            o_ref[...] = part + b_ref[...]

        @pl.when(kk > 0)
        def _():
            o_ref[...] += part

    return _pcall(body, name="mod_part", out_shape=jax.ShapeDtypeStruct((N_DEV, w), F32), grid=(d // tk,),
                  in_specs=[pl.BlockSpec((N_DEV, tk), lambda kk: (0, kk)), pl.BlockSpec((tk, w), lambda kk: (kk, 0)),
                            pl.BlockSpec((1, w), lambda kk: (0, 0))],
                  out_specs=pl.BlockSpec((N_DEV, w), lambda kk: (0, 0)))(c_all, w_mod, b_part)


def _row_call(body, name, t, d, tiled_in, vec_in, tiled_out_dtypes, n_vec_out, tr=256):
    tr = _blk(t, tr)
    tile = pl.BlockSpec((tr, d), lambda i: (i, 0))
    vec = pl.BlockSpec((1, d), lambda i: (0, 0))
    out_shape = [jax.ShapeDtypeStruct((t, d), dt) for dt in tiled_out_dtypes]
    out_shape += [jax.ShapeDtypeStruct((1, d), F32)] * n_vec_out
    return _pcall(body, name=name, out_shape=out_shape, grid=(t // tr,),
                  in_specs=[tile] * len(tiled_in) + [vec] * len(vec_in),
                  out_specs=[tile] * len(tiled_out_dtypes) + [vec] * n_vec_out)(*tiled_in, *vec_in)


def _accumulate(ref, val):
    @pl.when(pl.program_id(0) == 0)
    def _():
        ref[...] = val

    @pl.when(pl.program_id(0) > 0)
    def _():
        ref[...] += val


def _rsum(v):
    return jnp.sum(v, axis=0, keepdims=True)


def _rms(v):
    return lax.rsqrt(jnp.mean(v * v, axis=-1, keepdims=True) + NORM_EPS)


def _rms_bwd(vhat, r, dvhat):
    return r * (dvhat - vhat * jnp.mean(dvhat * vhat, axis=-1, keepdims=True))


def _pre_attn(x, g0, sc_a, sh_a):
    def body(x_ref, g_ref, sc_ref, sh_ref, h_ref):
        xv = x_ref[...]
        h_ref[...] = (xv * _rms(xv) * g_ref[...] * (1.0 + sc_ref[...]) + sh_ref[...]).astype(BF16)

    t, d = x.shape
    return _row_call(body, "pre_attn", t, d, [x], [g0, sc_a, sh_a], [BF16], 0)[0]


def _post_mix(x, mix, gt_a, g1, g2, sc_m, sh_m):
    def body(x_ref, mix_ref, gt_ref, g1_ref, g2_ref, sc_ref, sh_ref, x2_ref, h2_ref):
        mv = mix_ref[...]
        x2 = x_ref[...] + gt_ref[...] * (mv * _rms(mv) * g1_ref[...])
        x2_ref[...] = x2
        h2_ref[...] = (x2 * _rms(x2) * g2_ref[...] * (1.0 + sc_ref[...]) + sh_ref[...]).astype(BF16)

    t, d = x.shape
    return _row_call(body, "post_mix", t, d, [x, mix], [gt_a, g1, g2, sc_m, sh_m], [F32, BF16], 0)


def _final(y, x2, target, gt_m, g3):
    t, d = y.shape

    def body(y_ref, x2_ref, tg_ref, gt_ref, g3_ref, dy_ref, dout_ref, dgt_ref, dg3_ref, loss_ref):
        yv = y_ref[...]
        r = _rms(yv)
        yhat = yv * r
        n3 = yhat * g3_ref[...]
        err = x2_ref[...] + gt_ref[...] * n3 - tg_ref[...]
        _accumulate(loss_ref, jnp.zeros((1, d), F32) + 0.5 * jnp.sum(err * err) / d)
        dout = err * (1.0 / d)
        dout_ref[...] = dout
        dout_yhat = dout * yhat
        _accumulate(dgt_ref, _rsum(dout_yhat) * g3_ref[...])
        _accumulate(dg3_ref, _rsum(dout_yhat) * gt_ref[...])
        dy_ref[...] = _rms_bwd(yhat, r, dout * (gt_ref[...] * g3_ref[...])).astype(BF16)

    return _row_call(body, "final", t, d, [y, x2, target], [gt_m, g3], [BF16, F32], 3, tr=128)


def _mid_bwd(dh2, dout, x2, mix, g2, sc_m, gt_a, g1):
    t, d = x2.shape

    def body(dh2_ref, dout_ref, x2_ref, mix_ref, g2_ref, sc_ref, gt_ref, g1_ref,
             dmix_ref, dx2_ref, dsh_ref, dsc_ref, dg2_ref, dgt_ref, dg1_ref):
        g2, scale, g1, gt = g2_ref[...], 1.0 + sc_ref[...], g1_ref[...], gt_ref[...]
        dh2v = dh2_ref[...].astype(F32)
        x2v = x2_ref[...]
        r2 = _rms(x2v)
        x2hat = x2v * r2
        dh_xhat = dh2v * x2hat
        _accumulate(dsh_ref, _rsum(dh2v))
        _accumulate(dsc_ref, _rsum(dh_xhat) * g2)
        _accumulate(dg2_ref, _rsum(dh_xhat) * scale)
        dx2 = dout_ref[...] + _rms_bwd(x2hat, r2, dh2v * (scale * g2))
        dx2_ref[...] = dx2
        mv = mix_ref[...]
        r1 = _rms(mv)
        mhat = mv * r1
        dx_mhat = dx2 * mhat
        _accumulate(dgt_ref, _rsum(dx_mhat) * g1)
        _accumulate(dg1_ref, _rsum(dx_mhat) * gt)
        dmix_ref[...] = _rms_bwd(mhat, r1, dx2 * (gt * g1)).astype(BF16)

    return _row_call(body, "mid_bwd", t, d, [dh2, dout, x2, mix], [g2, sc_m, gt_a, g1], [BF16, F32], 5, tr=128)


def _x_bwd(dh1, dx2, x, g0, sc_a):
    t, d = x.shape

    def body(dh1_ref, dx2_ref, x_ref, g0_ref, sc_ref, dx_ref, dsh_ref, dsc_ref, dg0_ref):
        g0, scale = g0_ref[...], 1.0 + sc_ref[...]
        dh1v = dh1_ref[...].astype(F32)
        xv = x_ref[...]
        r0 = _rms(xv)
        xhat = xv * r0
        dh_xhat = dh1v * xhat
        _accumulate(dsh_ref, _rsum(dh1v))
        _accumulate(dsc_ref, _rsum(dh_xhat) * g0)
        _accumulate(dg0_ref, _rsum(dh_xhat) * scale)
        dx_ref[...] = dx2_ref[...] + _rms_bwd(xhat, r0, dh1v * (scale * g0))

    return _row_call(body, "x_bwd", t, d, [dh1, dx2, x], [g0, sc_a], [F32], 3, tr=128)


def _pick_lane(block, h):
    lane = lax.broadcasted_iota(jnp.int32, block.shape, 1)
    return jnp.sum(jnp.where(lane == h, block, 0.0), axis=1, keepdims=True)


def _put_lane(ref, rows, h, col):
    old = ref[rows, :]
    lane = lax.broadcasted_iota(jnp.int32, old.shape, 1)
    ref[rows, :] = jnp.where(lane == h, col, old)


def _tri(n, lower):
    r = lax.broadcasted_iota(jnp.int32, (n, n), 0)
    c = lax.broadcasted_iota(jnp.int32, (n, n), 1)
    return jnp.where((c <= r) if lower else (c >= r), 1.0, 0.0).astype(F32)


def _cum_fwd(fg, b128):
    t = fg.shape[0]
    nb = t // LANES

    def body(fg_ref, b_ref, cum_ref, cumt_ref):
        tri = _tri(LANES, True)
        carry = jnp.zeros((1, LANES), F32)
        for i in range(nb):
            z = fg_ref[i * LANES:(i + 1) * LANES, :] + b_ref[...]
            lf = jnp.minimum(z, 0.0) - jnp.log(1.0 + jnp.exp(-jnp.abs(z)))
            blk = jnp.dot(tri, lf, precision=lax.Precision.HIGHEST, preferred_element_type=F32) + carry
            cum_ref[i * LANES:(i + 1) * LANES, :] = blk
            carry = blk[LANES - 1:LANES, :]
        cumt_ref[...] = cum_ref[...].T[0:N_FOX, :]

    return _pcall(body, name="cum_fwd",
                  out_shape=[jax.ShapeDtypeStruct((t, LANES), F32), jax.ShapeDtypeStruct((N_FOX, t), F32)],
                  grid=(1,),
                  in_specs=[pl.BlockSpec((t, LANES), lambda i: (0, 0)), pl.BlockSpec((1, LANES), lambda i: (0, 0))],
                  out_specs=[pl.BlockSpec((t, LANES), lambda i: (0, 0)), pl.BlockSpec((N_FOX, t), lambda i: (0, 0))],
                  )(fg, b128)


def _fg_bwd(dcs_rows, fg, b128):
    t = fg.shape[0]
    nb = t // LANES

    def body(dcs_ref, fg_ref, b_ref, dfg_ref, db_ref, dcum_ref):
        dcum_ref[...] = -jnp.concatenate([dcs_ref[...], jnp.zeros((LANES - N_FOX, t), F32)], axis=0).T
        tri = _tri(LANES, False)
        carry = jnp.zeros((1, LANES), F32)
        db = jnp.zeros((1, LANES), F32)
        for i in reversed(range(nb)):
            rows = slice(i * LANES, (i + 1) * LANES)
            dlf = jnp.dot(tri, dcum_ref[rows, :], precision=lax.Precision.HIGHEST, preferred_element_type=F32) + carry
            carry = dlf[0:1, :]
            z = fg_ref[rows, :] + b_ref[...]
            dfg = dlf / (1.0 + jnp.exp(z))
            dfg_ref[rows, :] = dfg.astype(BF16)
            db = db + _rsum(dfg)
        db_ref[...] = db

    full = pl.BlockSpec((t, LANES), lambda i: (0, 0))
    vec = pl.BlockSpec((1, LANES), lambda i: (0, 0))
    return _pcall(body, name="fg_bwd",
                  out_shape=[jax.ShapeDtypeStruct((t, LANES), BF16), jax.ShapeDtypeStruct((1, LANES), F32)],
                  grid=(1,), in_specs=[pl.BlockSpec((N_FOX, t), lambda i: (0, 0)), full, vec], out_specs=[full, vec],
                  scratch_shapes=[pltpu.VMEM((t, LANES), F32)])(dcs_rows, fg, b128)


def _head_spec(t, col0, div=1):
    return pl.BlockSpec((t, HEAD_DIM), lambda h: (0, col0 + h // div))


def _fox_scores(q, k, cq, ck, i, tq, end):
    s = lax.dot_general(q, k, NT_DIMS, preferred_element_type=F32) * ATT_SCALE + cq - ck
    row = lax.broadcasted_iota(jnp.int32, (tq, end), 0) + i * tq
    col = lax.broadcasted_iota(jnp.int32, (tq, end), 1)
    return jnp.where(row >= col, s, -jnp.inf)


def _fox_fwd(proj_a, cum, cumt):
    t = proj_a.shape[0]
    tq = _blk(t, 512)
    nq = t // tq

    def body(q_ref, k_ref, v_ref, cum_ref, cumt_ref, o_ref, lse_ref):
        h = pl.program_id(0)
        cq_all = _pick_lane(cum_ref[...], h)
        ck_all = cumt_ref[pl.ds(h, 1), :]

        @pl.when(h == 0)
        def _():
            lse_ref[...] = jnp.zeros_like(lse_ref)

        for i in range(nq):
            rows, end = slice(i * tq, (i + 1) * tq), (i + 1) * tq
            s = _fox_scores(q_ref[rows, :], k_ref[0:end, :], cq_all[rows, :], ck_all[:, 0:end], i, tq, end)
            m = jnp.max(s, axis=1, keepdims=True)
            p = jnp.exp(s - m)
            l = jnp.sum(p, axis=1, keepdims=True)
            o = jnp.dot(p.astype(BF16), v_ref[0:end, :], preferred_element_type=F32) / l
            o_ref[rows, :] = o.astype(BF16)
            _put_lane(lse_ref, rows, h, m + jnp.log(l))

    nh = FOX_W // HEAD_DIM
    stat = pl.BlockSpec((t, LANES), lambda h: (0, 0))
    return _pcall(body, name="fox_fwd",
                  out_shape=[jax.ShapeDtypeStruct((t, FOX_W), BF16), jax.ShapeDtypeStruct((t, LANES), F32)],
                  grid=(N_FOX,),
                  in_specs=[_head_spec(t, 0), _head_spec(t, nh), _head_spec(t, 2 * nh), stat,
                            pl.BlockSpec((N_FOX, t), lambda h: (0, 0))],
                  out_specs=[_head_spec(t, 0), stat])(proj_a, proj_a, proj_a, cum, cumt)


def _fox_bwd(proj_a, d_attn, cum, cumt, lse):
    t = proj_a.shape[0]
    tq = _blk(t, 512)
    nq = t // tq

    def body(q_ref, k_ref, v_ref, do_ref, cum_ref, cumt_ref, lse_ref,
             dq_ref, dk_ref, dv_ref, dcs_ref, dk_acc, dv_acc, dcs_acc):
        h = pl.program_id(0)
        cq_all = _pick_lane(cum_ref[...], h)
        ck_all = cumt_ref[pl.ds(h, 1), :]
        lse_all = _pick_lane(lse_ref[...], h)
        dk_acc[...] = jnp.zeros_like(dk_acc)
        dv_acc[...] = jnp.zeros_like(dv_acc)
        dcs_acc[...] = jnp.zeros_like(dcs_acc)
        for i in range(nq):
            rows, end = slice(i * tq, (i + 1) * tq), (i + 1) * tq
            q, k, v, do = q_ref[rows, :], k_ref[0:end, :], v_ref[0:end, :], do_ref[rows, :]
            s = _fox_scores(q, k, cq_all[rows, :], ck_all[:, 0:end], i, tq, end)
            p = jnp.exp(s - lse_all[rows, :])
            dp = lax.dot_general(do, v, NT_DIMS, preferred_element_type=F32)
            ds = p * (dp - jnp.sum(p * dp, axis=1, keepdims=True))
            dcs_acc[:, 0:end] += jnp.sum(ds, axis=0, keepdims=True)
            ds = ds.astype(BF16)
            dq_ref[rows, :] = (jnp.dot(ds, k, preferred_element_type=F32) * ATT_SCALE).astype(BF16)
            dk_acc[0:end, :] += lax.dot_general(ds, q, TN_DIMS, preferred_element_type=F32)
            dv_acc[0:end, :] += lax.dot_general(p.astype(BF16), do, TN_DIMS, preferred_element_type=F32)
        dk_ref[...] = (dk_acc[...] * ATT_SCALE).astype(BF16)
        dv_ref[...] = dv_acc[...].astype(BF16)
        dcs_ref[pl.ds(h, 1), :] = dcs_acc[...]

    nh = FOX_W // HEAD_DIM
    stat = pl.BlockSpec((t, LANES), lambda h: (0, 0))
    rows8 = pl.BlockSpec((N_FOX, t), lambda h: (0, 0))
    head = _head_spec(t, 0)
    wide = jax.ShapeDtypeStruct((t, FOX_W), BF16)
    return _pcall(body, name="fox_bwd",
                  out_shape=[wide, wide, wide, jax.ShapeDtypeStruct((N_FOX, t), F32)],
                  grid=(N_FOX,),
                  in_specs=[_head_spec(t, 0), _head_spec(t, nh), _head_spec(t, 2 * nh), head, stat, rows8, stat],
                  out_specs=[head, head, head, rows8],
                  scratch_shapes=[pltpu.VMEM((t, HEAD_DIM), F32), pltpu.VMEM((t, HEAD_DIM), F32),
                                  pltpu.VMEM((1, t), F32)],
                  )(proj_a, proj_a, proj_a, d_attn, cum, cumt, lse)


def _swa_scores(q, k, i, tq, start, end):
    s = lax.dot_general(q, k, NT_DIMS, preferred_element_type=F32) * ATT_SCALE
    row = lax.broadcasted_iota(jnp.int32, (tq, end - start), 0) + i * tq
    col = lax.broadcasted_iota(jnp.int32, (tq, end - start), 1) + start
    diff = row - col
    return jnp.where((diff >= 0) & (diff < WINDOW), s, -jnp.inf)


def _swa_blocks(t):
    tq = _blk(t, 256)
    return tq, [(i, max(0, i * tq - WINDOW), (i + 1) * tq) for i in range(t // tq)]


def _swa_fwd(proj_b, sinks128):
    t = proj_b.shape[0]
    tq, blocks = _swa_blocks(t)

    def body(q_ref, k_ref, v_ref, sink_ref, o_ref, lse_ref):
        h = pl.program_id(0)
        sink = _pick_lane(sink_ref[...], h)

        @pl.when(h == 0)
        def _():
            lse_ref[...] = jnp.zeros_like(lse_ref)

        for i, start, end in blocks:
            rows = slice(i * tq, end)
            s = _swa_scores(q_ref[rows, :], k_ref[start:end, :], i, tq, start, end)
            m = jnp.maximum(jnp.max(s, axis=1, keepdims=True), sink)
            p = jnp.exp(s - m)
            l = jnp.sum(p, axis=1, keepdims=True) + jnp.exp(sink - m)
            o = jnp.dot(p.astype(BF16), v_ref[start:end, :], preferred_element_type=F32) / l
            o_ref[rows, :] = o.astype(BF16)
            _put_lane(lse_ref, rows, h, m + jnp.log(l))

    stat = pl.BlockSpec((t, LANES), lambda h: (0, 0))
    return _pcall(body, name="swa_fwd",
                  out_shape=[jax.ShapeDtypeStruct((t, SWA_W), BF16), jax.ShapeDtypeStruct((t, LANES), F32)],
                  grid=(N_SWA,),
                  in_specs=[_head_spec(t, 0), _head_spec(t, N_SWA, GQA), _head_spec(t, N_SWA + N_KV, GQA),
                            pl.BlockSpec((1, LANES), lambda h: (0, 0))],
                  out_specs=[_head_spec(t, 0), stat])(proj_b, proj_b, proj_b, sinks128)


def _rope_bwd(d, cos, sin):
    return d * cos + pltpu.roll(d * sin, HEAD_DIM // 2, 1)


def _swa_bwd(proj_b, d_attn, lse, sinks128, cos, sin):
    t = proj_b.shape[0]
    tq, blocks = _swa_blocks(t)

    def body(q_ref, k_ref, v_ref, do_ref, lse_ref, sink_ref, cos_ref, sin_ref,
             dq_ref, dk_ref, dv_ref, dsink_ref, dk_acc, dv_acc):
        h = pl.program_id(0)
        sink = _pick_lane(sink_ref[...], h)
        lse_all = _pick_lane(lse_ref[...], h)

        @pl.when(h == 0)
        def _():
            dsink_ref[...] = jnp.zeros_like(dsink_ref)

        @pl.when(h % GQA == 0)
        def _():
            dk_acc[...] = jnp.zeros_like(dk_acc)
            dv_acc[...] = jnp.zeros_like(dv_acc)

        dsink = jnp.zeros((1, 1), F32)
        for i, start, end in blocks:
            rows = slice(i * tq, end)
            q, k, v, do = q_ref[rows, :], k_ref[start:end, :], v_ref[start:end, :], do_ref[rows, :]
            s = _swa_scores(q, k, i, tq, start, end)
            p = jnp.exp(s - lse_all[rows, :])
            dp = lax.dot_general(do, v, NT_DIMS, preferred_element_type=F32)
            delta = jnp.sum(p * dp, axis=1, keepdims=True)
            ds = (p * (dp - delta)).astype(BF16)
            dq = jnp.dot(ds, k, preferred_element_type=F32) * ATT_SCALE
            dq_ref[rows, :] = _rope_bwd(dq, cos_ref[rows, :], sin_ref[rows, :]).astype(BF16)
            dk_acc[start:end, :] += lax.dot_general(ds, q, TN_DIMS, preferred_element_type=F32)
            dv_acc[start:end, :] += lax.dot_general(p.astype(BF16), do, TN_DIMS, preferred_element_type=F32)
            dsink = dsink - jnp.sum(jnp.exp(sink - lse_all[rows, :]) * delta, axis=0, keepdims=True)
        old = dsink_ref[...]
        lane = lax.broadcasted_iota(jnp.int32, old.shape, 1)
        dsink_ref[...] = jnp.where(lane == h, dsink, old)

        @pl.when(h % GQA == GQA - 1)
        def _():
            dk_ref[...] = _rope_bwd(dk_acc[...] * ATT_SCALE, cos_ref[...], sin_ref[...]).astype(BF16)
            dv_ref[...] = dv_acc[...].astype(BF16)

    stat = pl.BlockSpec((t, LANES), lambda h: (0, 0))
    vec = pl.BlockSpec((1, LANES), lambda h: (0, 0))
    head = _head_spec(t, 0)
    kv_out = _head_spec(t, 0, GQA)
    return _pcall(body, name="swa_bwd",
                  out_shape=[jax.ShapeDtypeStruct((t, SWA_W), BF16), jax.ShapeDtypeStruct((t, KV_W), BF16),
                             jax.ShapeDtypeStruct((t, KV_W), BF16), jax.ShapeDtypeStruct((1, LANES), F32)],
                  grid=(N_SWA,),
                  in_specs=[head, _head_spec(t, N_SWA, GQA), _head_spec(t, N_SWA + N_KV, GQA),
                            _head_spec(t, N_FOX), stat, vec, stat, stat],
                  out_specs=[head, kv_out, kv_out, vec],
                  scratch_shapes=[pltpu.VMEM((t, HEAD_DIM), F32), pltpu.VMEM((t, HEAD_DIM), F32)],
                  )(proj_b, proj_b, proj_b, d_attn, lse, sinks128, cos, sin)


def _adamw(w, g, m, v):
    m = ADAM_B1 * m + (1.0 - ADAM_B1) * g
    v = ADAM_B2 * v + (1.0 - ADAM_B2) * (g * g)
    m_hat = m / (1.0 - ADAM_B1 ** ADAM_STEP)
    v_hat = v / (1.0 - ADAM_B2 ** ADAM_STEP)
    delta = -ADAM_LR * (m_hat / (jnp.sqrt(v_hat) + ADAM_EPS) + ADAM_WD * w)
    return delta, m, v


def _adam_pieces(w, m, v, own, land, idx, name):
    rows, cols = w.shape
    tr, tc = (256, cols) if rows % 256 == 0 else (rows, _blk(cols, 512))

    def body(idx_ref, own_ref, l1_ref, l2_ref, l3_ref, w_ref, m_ref, v_ref, g_ref, d_ref, mo_ref, vo_ref):
        g = own_ref[...].astype(F32) + l1_ref[...].astype(F32) + l2_ref[...].astype(F32) + l3_ref[...].astype(F32)
        g_ref[...] = g
        d_ref[...], mo_ref[...], vo_ref[...] = _adamw(w_ref[...], g, m_ref[...], v_ref[...])

    def piece(p):
        return pl.BlockSpec((None, tr, tc), lambda i, j, idx_ref: (idx_ref[p], i, j))

    tile = pl.BlockSpec((tr, tc), lambda i, j, idx_ref: (i, j))
    out = jax.ShapeDtypeStruct((rows, cols), F32)
    grid_spec = pltpu.PrefetchScalarGridSpec(
        num_scalar_prefetch=1, grid=(rows // tr, cols // tc),
        in_specs=[piece(0), piece(1), piece(2), piece(3), tile, tile, tile], out_specs=[tile] * 4)
    return _pcall(body, name=name, out_shape=[out] * 4, grid_spec=grid_spec)(idx, own, land, land, land, w, m, v)


def _adam_mod(c_all, dmod_cols, w, m, v):
    rows, cols = w.shape
    tr = _blk(rows, 256)

    def body(c_ref, dm_ref, w_ref, m_ref, v_ref, g_ref, d_ref, mo_ref, vo_ref):
        cond = _silu(c_ref[...]).astype(BF16)
        g = lax.dot_general(cond, dm_ref[...].astype(BF16), TN_DIMS, preferred_element_type=F32)
        g_ref[...] = g
        d_ref[...], mo_ref[...], vo_ref[...] = _adamw(w_ref[...], g, m_ref[...], v_ref[...])

    tile = pl.BlockSpec((tr, cols), lambda i: (i, 0))
    out = jax.ShapeDtypeStruct((rows, cols), F32)
    return _pcall(body, name="adam_mod", out_shape=[out] * 4, grid=(rows // tr,),
                  in_specs=[pl.BlockSpec((N_DEV, tr), lambda i: (0, i)), pl.BlockSpec((N_DEV, cols), lambda i: (0, 0)),
                            tile, tile, tile],
                  out_specs=[tile] * 4)(c_all, dmod_cols, w, m, v)


def _adam_small(parts, w, m, v):
    nv = w.shape[1]

    def body(p_ref, w_ref, m_ref, v_ref, g_ref, d_ref, mo_ref, vo_ref):
        g = p_ref[0:1, :]
        for k in range(1, N_DEV):
            g = g + p_ref[k:k + 1, :]
        g_ref[...] = g
        d_ref[...], mo_ref[...], vo_ref[...] = _adamw(w_ref[...], g, m_ref[...], v_ref[...])

    vec = pl.BlockSpec((1, nv), lambda i: (0, 0))
    out = jax.ShapeDtypeStruct((1, nv), F32)
    return _pcall(body, name="adam_small", out_shape=[out] * 4, grid=(1,),
                  in_specs=[pl.BlockSpec((N_DEV, nv), lambda i: (0, 0)), vec, vec, vec],
                  out_specs=[vec] * 4)(parts, w, m, v)


def _pad_lanes(v, width=LANES):
    return jnp.pad(v, ((0, 0), (0, width - v.shape[1])))


def kernel(x, c, w_mod, b_mod, g_pre_mix, g_post_mix, w_in, b_forget, swa_sinks, w_out, g_pre_mlp, g_post_mlp, w_up, w_down, loss_target, m_w_mod, m_b_mod, m_g_pre_mix, m_g_post_mix, m_w_in, m_b_forget, m_swa_sinks, m_w_out, m_g_pre_mlp, m_g_post_mlp, m_w_up, m_w_down, v_w_mod, v_b_mod, v_g_pre_mix, v_g_post_mix, v_w_in, v_b_forget, v_swa_sinks, v_w_out, v_g_pre_mlp, v_g_post_mlp, v_w_up, v_w_down):
    ax, ay, ac = _position()
    me = 4 * ax + 2 * ay + ac
    x, target = x[0], loss_target[0]
    t, d = x.shape
    w_mod, w_in, w_out, w_up, w_down = w_mod[0], w_in[0], w_out[0], w_up[0], w_down[0]
    mod_w = w_mod.shape[1]
    in_w = w_in.shape[1]
    in_total = N_DEV * in_w
    shard_ff = w_up.shape[1]
    n_fox3 = 3 * FOX_W
    n_swa3 = SWA_W + 2 * KV_W
    assert in_total == n_fox3 + N_FOX + n_swa3 and d == FOX_W + SWA_W

    c_all = _all_gather([c], "gather_c")[0].reshape(N_DEV, d)
    b_part = lax.dynamic_slice(b_mod, (0, me * mod_w), (1, mod_w))
    mod_parts = _all_gather([_mod_part(c_all, w_mod, b_part)], "gather_mod")[0]
    mod = lax.dynamic_index_in_dim(mod_parts, me, axis=1, keepdims=False).reshape(1, N_DEV * mod_w)

    w_in_b, mod = lax.optimization_barrier((w_in.astype(BF16), mod))
    first = _ag_start([w_in_b], "ag_start_in")
    behind_first = first[4][0, 0]
    rest = _ag_start([(w + behind_first).astype(BF16) for w in (w_out, w_up, w_down)], "ag_start_rest")
    ag_send, ag_recv, ag_shard, ag_land = [a + b for a, b in zip(first[:4], rest[:4])]
    ag_token = rest[4]

    def gathered(i, after, name):
        shard, land = _ag_wait(ag_send[i], ag_recv[i], ag_shard[i], ag_land[i], after, "ag_wait_" + name)
        return lax.dynamic_update_slice(_ag_forward(land, "ag_fwd_" + name), shard[None], (me, 0, 0))

    def gathered_start(i, after, name):
        shard, land = _ag_wait(ag_send[i], ag_recv[i], ag_shard[i], ag_land[i], after, "ag_wait_" + name)
        return shard, _fwd_start(land, "fwd_start_" + name)

    def gathered_finish(started, after, name):
        shard, (send, recv, land, _) = started
        return lax.dynamic_update_slice(_fwd_wait(send, recv, land, after, "fwd_wait_" + name), shard[None], (me, 0, 0))

    sh_a, sc_a, gt_a, sh_m, sc_m, gt_m = [mod[:, i * d:(i + 1) * d] for i in range(6)]

    half = HEAD_DIM // 2
    inv_freq = 1.0 / (ROPE_THETA ** (jnp.arange(half, dtype=F32) * (2.0 / HEAD_DIM)))
    ang = jnp.arange(t).astype(F32)[:, None] * inv_freq[None, :]
    cos = jnp.concatenate([jnp.cos(ang), jnp.cos(ang)], axis=1)
    sin = jnp.concatenate([-jnp.sin(ang), jnp.sin(ang)], axis=1)

    b128 = _pad_lanes(b_forget)
    sinks128 = _pad_lanes(swa_sinks)

    h1 = _pre_attn(x, g_pre_mix + ag_token[0:1, 0:1], sc_a, sh_a)
    h1, adam_in = lax.optimization_barrier(
        (h1, (jnp.transpose(w_in), jnp.transpose(m_w_in[0]), jnp.transpose(v_w_in[0]))))
    w_in_g = gathered(0, h1, "in")
    o_fg, o_sq = n_fox3, n_fox3 + N_FOX

    def cols(lo, hi):
        parts = []
        for j in range(lo // in_w, (hi - 1) // in_w + 1):
            parts.append(w_in_g[j, :, max(lo - j * in_w, 0):min(hi - j * in_w, in_w)])
        return parts

    w_in_r = jnp.concatenate(cols(0, o_fg) + cols(o_sq, in_total) + cols(o_fg, o_sq)
                             + [jnp.zeros((d, FG_PAD - N_FOX), BF16)], axis=1)
    proj_a = _matmul(h1, w_in_r, name="proj_a", n_cols=n_fox3, n_off=0)
    proj_b = _matmul(h1, w_in_r, name="proj_b", n_cols=n_swa3, n_off=n_fox3, tn=512, row_extras=(cos, sin),
                     epilogue=lambda acc, j, cs, sn: (_rope_cols(acc, j, cs, sn, SWA_W + KV_W),))
    out_started = gathered_start(1, proj_a, "out")
    fg = _matmul(h1, w_in_r, name="proj_fg", n_cols=FG_PAD, n_off=n_fox3 + n_swa3, tn=FG_PAD,
                 out_dtypes=(F32,), after=out_started[1][3])[:, 0:LANES]
    cum, cumt = _cum_fwd(fg, b128)
    fox_o, fox_lse = _fox_fwd(proj_a, cum, cumt)
    up_started = gathered_start(2, fox_o, "up")
    swa_o, swa_lse = _swa_fwd(proj_b, sinks128 + up_started[1][3][0:1, 0:1])
    w_out_full = gathered_finish(out_started, swa_o, "out").reshape(d, d)
    attn = jnp.concatenate([fox_o, swa_o], axis=1)
    mix = _matmul(attn, w_out_full, name="out_proj", out_dtypes=(F32,))
    x2, h2 = _post_mix(x, mix, gt_a, g_post_mix, g_pre_mlp, sc_m, sh_m)
    w_up_g = gathered_finish(up_started, h2, "up")
    u, act = _matmul(h2, w_up_g, name="mlp_up", b_sharded=True, out_dtypes=(BF16, BF16),
                     epilogue=lambda acc, j: (acc, jnp.square(jnp.maximum(acc, 0.0))))
    w_down_full = gathered(3, act, "down").reshape(N_DEV * shard_ff, d)
    y = _matmul(act, w_down_full, name="mlp_down", tk=4096, out_dtypes=(F32,))

    core = jnp.reshape(ac, (1,)).astype(jnp.int32)

    def reduce_start(started, after, name):
        send, recv, src, land, _ = started
        full, from_sibling = _sib_wait(send, recv, src, land, after, "sib_wait_" + name)
        return _rs_start(_chip_sum(full, from_sibling, core, "chip_sum_" + name), "rs_start_" + name)

    def tok(started):
        return started[4][0:1, 0:1]

    idx = jnp.stack([2 * ax + ay, 2 * (1 - ax) + ay, 2 * ax + (1 - ay), 2 * (1 - ax) + (1 - ay)]).astype(jnp.int32)

    def reduce_finish(started, after, w, m, v, name):
        send, recv, src, land, _ = started
        own, landed = _rs_wait(send, recv, src, land, after, "rs_wait_" + name)
        return _adam_pieces(w, m[0], v[0], own, landed, idx, "adam_" + name)

    dy, dout, dgt_m, dg3, loss_vec = _final(y, x2, target, gt_m, g_post_mlp)
    du = _matmul(dy, w_down_full, name="d_act", tb=True, tile_extras=(u,),
                 epilogue=lambda acc, j, uu: (acc * (2.0 * jnp.maximum(uu.astype(F32), 0.0)),))
    dw_down = _matmul(act, dy, name="dw_down", ta=True)
    sb_down = _sib_start(dw_down.reshape(N_DEV, shard_ff, d), "sib_start_down")
    dh2 = _matmul(du, w_up_g, name="d_h2", tb=True, b_sharded=True, b_pair=True, after=sb_down[4])
    rs_down = reduce_start(sb_down, dh2, "down")
    per = shard_ff // _blk(shard_ff, 1024)
    dw_up = _matmul(h2, du, name="dw_up", ta=True, tn=_blk(shard_ff, 1024), out_shape=(N_DEV, d, shard_ff),
                    out_map=lambda tm, tn: pl.BlockSpec((None, tm, tn), lambda i, j, kk: (j // per, i, j % per)),
                    after=rs_down[4])
    sb_up = _sib_start(dw_up, "sib_start_up")
    dmix, dx2, dsh_m, dsc_m, dg2, dgt_a, dg1 = _mid_bwd(
        dh2, dout, x2, mix, g_pre_mlp + tok(sb_up), sc_m, gt_a, g_post_mix)
    d_attn = _matmul(dmix, w_out_full, name="d_attn", tb=True)
    rs_up = reduce_start(sb_up, d_attn, "up")
    dw_out = _matmul(attn, dmix, name="dw_out", ta=True, after=rs_up[4])
    sb_out = _sib_start(dw_out.reshape(N_DEV, d // N_DEV, d), "sib_start_out")
    dqf, dkf, dvf, dcs = _fox_bwd(proj_a, d_attn, cum, cumt, fox_lse)
    dsq, dsk, dsv, dsinks = _swa_bwd(proj_b, d_attn, swa_lse, sinks128 + tok(sb_out), cos, sin)
    rs_out = reduce_start(sb_out, dsq, "out")
    dfg, db_forget = _fg_bwd(dcs, fg, b128 + tok(rs_out))
    dproj = jnp.concatenate([dqf, dkf, dvf, dsq, dsk, dsv, _pad_lanes(dfg, FG_PAD)], axis=1)
    dw_in_r = _matmul(dproj, h1, name="dw_in", ta=True)

    def shard_rows(j):
        lo, hi = j * in_w, (j + 1) * in_w
        parts = []
        for seg_lo, seg_hi, shift in ((0, o_fg, 0), (o_fg, o_sq, n_swa3), (o_sq, in_total, -N_FOX)):
            a, b = max(lo, seg_lo), min(hi, seg_hi)
            if a < b:
                parts.append(dw_in_r[a + shift:b + shift, :])
        return parts[0] if len(parts) == 1 else jnp.concatenate(parts, axis=0)

    sb_in = _sib_start(jnp.stack([shard_rows(j) for j in range(N_DEV)]), "sib_start_in")
    dh1 = _matmul(dproj, w_in_r, name="d_h1", tb=True, tk=2560, after=sb_in[4])
    grad_x, dsh_a, dsc_a, dg0 = _x_bwd(dh1, dx2, x, g_pre_mix, sc_a)

    small = jnp.concatenate([dsh_a, dsc_a, dgt_a, dsh_m, dsc_m, dgt_m, dg0, dg1, dg2, dg3, db_forget, dsinks,
                             loss_vec[:, 0:LANES]], axis=1)
    small_all = _all_gather([small], "gather_small")[0].reshape(N_DEV, small.shape[1])
    rs_in = reduce_start(sb_in, small_all, "in")

    pack = lambda bm, g0_, g1_, g2_, g3_, bf_, sk_: jnp.concatenate(
        [bm, g0_, g1_, g2_, g3_, _pad_lanes(bf_), _pad_lanes(sk_), jnp.zeros((1, LANES), F32)], axis=1)
    p_small = pack(b_mod, g_pre_mix, g_post_mix, g_pre_mlp, g_post_mlp, b_forget, swa_sinks)
    m_small = pack(m_b_mod, m_g_pre_mix, m_g_post_mix, m_g_pre_mlp, m_g_post_mlp, m_b_forget, m_swa_sinks)
    v_small = pack(v_b_mod, v_g_pre_mix, v_g_post_mix, v_g_pre_mlp, v_g_post_mlp, v_b_forget, v_swa_sinks)
    small_out = _adam_small(small_all, p_small + tok(rs_in), m_small, v_small)

    n_mod = 6 * d

    def unpack(vec):
        o = n_mod
        return (vec[:, 0:n_mod], vec[:, o:o + d], vec[:, o + d:o + 2 * d], vec[:, o + 2 * d:o + 3 * d],
                vec[:, o + 3 * d:o + 4 * d], vec[:, o + 4 * d:o + 4 * d + N_FOX],
                vec[:, o + 4 * d + LANES:o + 4 * d + LANES + N_SWA])

    loss = small_out[0][0, n_mod + 4 * d + 2 * LANES]
    g_small, d_small, nm_small, nv_small = [unpack(vec) for vec in small_out]

    dmod_cols = lax.dynamic_slice(small_all, (0, me * mod_w), (N_DEV, mod_w))
    g_w_mod, d_w_mod, nm_w_mod, nv_w_mod = _adam_mod(c_all + tok(rs_in), dmod_cols, w_mod, m_w_mod[0], v_w_mod[0])

    g_w_down, d_w_down, nm_w_down, nv_w_down = reduce_finish(rs_down, d_w_mod, w_down, m_w_down, v_w_down, "w_down")
    g_w_up, d_w_up, nm_w_up, nv_w_up = reduce_finish(rs_up, d_w_down, w_up, m_w_up, v_w_up, "w_up")
    g_w_out, d_w_out, nm_w_out, nv_w_out = reduce_finish(rs_out, d_w_up, w_out, m_w_out, v_w_out, "w_out")
    g_w_in, d_w_in, nm_w_in, nv_w_in = reduce_finish(rs_in, d_w_out, adam_in[0], (adam_in[1],), (adam_in[2],), "w_in")

    def assemble(w_mod_, small_, w_in_, w_out_, w_up_, w_down_):
        b_mod_, g0_, g1_, g2_, g3_, bf_, sk_ = small_
        return [w_mod_[None], b_mod_, g0_, g1_, jnp.transpose(w_in_)[None], bf_, sk_, w_out_[None], g2_, g3_,
                w_up_[None], w_down_[None]]

    outs = [loss, grad_x[None]]
    outs += assemble(g_w_mod, g_small, g_w_in, g_w_out, g_w_up, g_w_down)
    outs += assemble(d_w_mod, d_small, d_w_in, d_w_out, d_w_up, d_w_down)
    outs += assemble(nm_w_mod, nm_small, nm_w_in, nm_w_out, nm_w_up, nm_w_down)
    outs += assemble(nv_w_mod, nv_small, nv_w_in, nv_w_out, nv_w_up, nv_w_down)
    return tuple(outs)
```

```python
import jax
import jax.numpy as jnp
from jax import lax
from jax.experimental import pallas as pl
from jax.experimental.pallas import tpu as pltpu

F32 = jnp.float32
BF16 = jnp.bfloat16
MESH = pl.DeviceIdType.MESH

N_DEV = 8
N_CHIP = 4
LANES = 128
HEAD_DIM = 128
N_FOX = 8
N_SWA = 8
N_KV = 2
GQA = N_SWA // N_KV
WINDOW = 128
FOX_W = N_FOX * HEAD_DIM
SWA_W = N_SWA * HEAD_DIM
KV_W = N_KV * HEAD_DIM
ROPE_THETA = 10000.0
NORM_EPS = 1e-6
ATT_SCALE = HEAD_DIM ** -0.5
FG_PAD = 512

ADAM_LR = 0.001
ADAM_B1 = 0.9
ADAM_B2 = 0.999
ADAM_EPS = 1e-08
ADAM_WD = 0.01
ADAM_STEP = 10

VMEM_LIMIT = 56 * 1024 * 1024

NT_DIMS = (((1,), (1,)), ((), ()))
TN_DIMS = (((0,), (0,)), ((), ()))


def _pcall(body, *, name, out_shape, grid=(), in_specs=None, out_specs=None, scratch_shapes=(), grid_spec=None):
    params = pltpu.CompilerParams(vmem_limit_bytes=VMEM_LIMIT)
    if grid_spec is not None:
        return pl.pallas_call(body, name=name, out_shape=out_shape, grid_spec=grid_spec, compiler_params=params)
    return pl.pallas_call(body, name=name, out_shape=out_shape, grid=grid, in_specs=in_specs, out_specs=out_specs,
                          scratch_shapes=scratch_shapes, compiler_params=params)


def _blk(n, pref):
    if n <= pref:
        return n
    b = (pref // LANES) * LANES
    while n % b:
        b -= LANES
    return b


def _position():
    return lax.axis_index("x"), lax.axis_index("y"), lax.axis_index("c")


ANY = pl.BlockSpec(memory_space=pl.ANY)


def _all_gather(arrs, name):
    n = len(arrs)

    def body(*refs):
        ins, outs = refs[:n], refs[n:2 * n]
        send_sems, recv_sems, local_sems = refs[2 * n:]
        x, y, c = _position()
        me, sibling = (x, y, c), (x, y, 1 - c)
        chips = [(1 - x, y), (x, 1 - y), (1 - x, 1 - y)]

        def slot(p):
            return 4 * p[0] + 2 * p[1] + p[2]

        def copy(a, k, block, to, src=None):
            dst = outs[a].at[slot(block)]
            return pltpu.make_async_remote_copy(
                src_ref=dst if src is None else src, dst_ref=dst,
                send_sem=send_sems.at[7 * a + k], recv_sem=recv_sems.at[7 * a + k],
                device_id=to, device_id_type=MESH)

        mine = [pltpu.make_async_copy(ins[a], outs[a].at[slot(me)], local_sems.at[a]) for a in range(n)]
        for cp in mine:
            cp.start()
        first = []
        for a in range(n):
            first.append(copy(a, 0, me, sibling, src=ins[a]))
            first += [copy(a, 1 + j, me, (*chip, c), src=ins[a]) for j, chip in enumerate(chips)]
        for cp in first:
            cp.start()
        passed = []
        for a in range(n):
            for j, chip in enumerate(chips):
                copy(a, 1 + j, (*chip, c), me).wait_recv()
                cp = copy(a, 4 + j, (*chip, c), sibling)
                cp.start()
                passed.append(cp)
        for a in range(n):
            copy(a, 0, sibling, me).wait_recv()
            for j, chip in enumerate(chips):
                copy(a, 4 + j, (*chip, 1 - c), me).wait_recv()
        for cp in first + passed:
            cp.wait_send()
        for cp in mine:
            cp.wait()

    return _pcall(
        body, name=name,
        out_shape=[jax.ShapeDtypeStruct((N_DEV,) + a.shape, a.dtype) for a in arrs],
        in_specs=[ANY] * n, out_specs=[ANY] * n,
        scratch_shapes=[pltpu.SemaphoreType.DMA((7 * n,)), pltpu.SemaphoreType.DMA((7 * n,)),
                        pltpu.SemaphoreType.DMA((n,))],
    )(*arrs)


HBM = pl.BlockSpec(memory_space=pltpu.HBM)
SEM = pl.BlockSpec(memory_space=pltpu.SEMAPHORE)
EFFECT = pltpu.SideEffectType.DATAFLOW_SIDE_EFFECTING


def _hbm(a):
    return pltpu.with_memory_space_constraint(a, pltpu.HBM)


def _gather_peers():
    x, y, c = _position()
    return [(x, y, 1 - c), (1 - x, y, c), (x, 1 - y, c), (1 - x, 1 - y, c)]


def _ag_start(shards, name):
    n = len(shards)
    lands = [_hbm(lax.empty((N_DEV,) + s.shape, s.dtype)) for s in shards]

    def body(*refs):
        srcs, land, send, recv = refs[:n], refs[n:2 * n], refs[2 * n:3 * n], refs[3 * n:4 * n]
        token = refs[6 * n]
        x, y, c = _position()
        for a in range(n):
            for k, to in enumerate(_gather_peers()):
                pltpu.make_async_remote_copy(
                    src_ref=srcs[a], dst_ref=land[a].at[4 * x + 2 * y + c], send_sem=send[a].at[k],
                    recv_sem=recv[a].at[k], device_id=to, device_id_type=MESH).start()
        token[...] = jnp.zeros_like(token)

    sems = [pltpu.SemaphoreType.DMA((4,))] * (2 * n)
    out = pl.pallas_call(
        body, name=name,
        out_shape=sems + [pltpu.HBM(s.shape, s.dtype) for s in shards] + [pltpu.HBM(l.shape, l.dtype) for l in lands]
        + [jax.ShapeDtypeStruct((8, LANES), F32)],
        in_specs=[HBM] * (2 * n), out_specs=[SEM] * (2 * n) + [HBM] * (2 * n) + [pl.BlockSpec(memory_space=pltpu.VMEM)],
        input_output_aliases={**{a: 2 * n + a for a in range(n)}, **{n + a: 3 * n + a for a in range(n)}},
        compiler_params=pltpu.CompilerParams(has_side_effects=EFFECT),
    )(*[_hbm(s) for s in shards], *lands)
    return out[:n], out[n:2 * n], out[2 * n:3 * n], out[3 * n:4 * n], out[4 * n]


def _ag_wait(send, recv, shard_thru, land_thru, after, name):
    def body(v_ref, land_ref, send_sem, recv_sem, after_ref, v_dead, got_ref):
        for k, to in enumerate(_gather_peers()):
            cp = pltpu.make_async_remote_copy(
                src_ref=v_ref, dst_ref=land_ref.at[0], send_sem=send_sem.at[k], recv_sem=recv_sem.at[k],
                device_id=to, device_id_type=MESH)
            cp.wait_send()
            cp.wait_recv()

    return pl.pallas_call(
        body, name=name,
        out_shape=(pltpu.HBM(shard_thru.shape, shard_thru.dtype), pltpu.HBM(land_thru.shape, land_thru.dtype)),
        in_specs=(HBM, HBM, SEM, SEM, ANY), out_specs=(HBM, HBM), input_output_aliases={0: 0, 1: 1},
        compiler_params=pltpu.CompilerParams(has_side_effects=EFFECT),
    )(shard_thru, land_thru, send, recv, after)


def _ag_forward(land, name):
    def body(land_in, land_ref, send_sems, recv_sems):
        x, y, c = _position()
        copies = []
        for j, (px, py) in enumerate([(1 - x, y), (x, 1 - y), (1 - x, 1 - y)]):
            block = land_ref.at[4 * px + 2 * py + c]
            cp = pltpu.make_async_remote_copy(src_ref=block, dst_ref=block, send_sem=send_sems.at[j],
                                              recv_sem=recv_sems.at[j], device_id=(x, y, 1 - c), device_id_type=MESH)
            cp.start()
            copies.append(cp)
        for cp in copies:
            cp.wait()

    return pl.pallas_call(
        body, name=name, out_shape=jax.ShapeDtypeStruct(land.shape, land.dtype),
        in_specs=[ANY], out_specs=ANY, input_output_aliases={0: 0},
        scratch_shapes=[pltpu.SemaphoreType.DMA((3,)), pltpu.SemaphoreType.DMA((3,))],
    )(land)


def _fwd_start(land, name):
    def body(land_ref, send, recv, land_thru, token):
        x, y, c = _position()
        for j, (px, py) in enumerate([(1 - x, y), (x, 1 - y), (1 - x, 1 - y)]):
            block = land_ref.at[4 * px + 2 * py + c]
            pltpu.make_async_remote_copy(src_ref=block, dst_ref=block, send_sem=send.at[j], recv_sem=recv.at[j],
                                         device_id=(x, y, 1 - c), device_id_type=MESH).start()
        token[...] = jnp.zeros_like(token)

    return pl.pallas_call(
        body, name=name,
        out_shape=[pltpu.SemaphoreType.DMA((3,)), pltpu.SemaphoreType.DMA((3,)), pltpu.HBM(land.shape, land.dtype),
                   jax.ShapeDtypeStruct((8, LANES), F32)],
        in_specs=[HBM], out_specs=[SEM, SEM, HBM, pl.BlockSpec(memory_space=pltpu.VMEM)],
        input_output_aliases={0: 2},
        compiler_params=pltpu.CompilerParams(has_side_effects=EFFECT),
    )(land)


def _fwd_wait(send, recv, land_thru, after, name):
    def body(land_ref, send_sem, recv_sem, after_ref, land_out):
        x, y, c = _position()
        for j in range(3):
            cp = pltpu.make_async_remote_copy(
                src_ref=land_ref.at[0], dst_ref=land_ref.at[0], send_sem=send_sem.at[j], recv_sem=recv_sem.at[j],
                device_id=(x, y, 1 - c), device_id_type=MESH)
            cp.wait_send()
            cp.wait_recv()

    return pl.pallas_call(
        body, name=name, out_shape=pltpu.HBM(land_thru.shape, land_thru.dtype),
        in_specs=(HBM, SEM, SEM, ANY), out_specs=HBM, input_output_aliases={0: 0},
        compiler_params=pltpu.CompilerParams(has_side_effects=EFFECT),
    )(land_thru, send, recv, after)


def _rs_peers():
    x, y, c = _position()
    return [(1 - x, y, c), (x, 1 - y, c), (1 - x, 1 - y, c)]


def _rs_start(chip_sums, name):
    land = _hbm(lax.empty(chip_sums.shape, chip_sums.dtype))

    def body(src, land_ref, send, recv, src_thru, land_thru, token):
        x, y, c = _position()
        for j, (px, py, pc) in enumerate(_rs_peers()):
            pltpu.make_async_remote_copy(
                src_ref=src.at[2 * px + py], dst_ref=land_ref.at[2 * x + y], send_sem=send.at[j], recv_sem=recv.at[j],
                device_id=(px, py, pc), device_id_type=MESH).start()
        token[...] = jnp.zeros_like(token)

    return pl.pallas_call(
        body, name=name,
        out_shape=[pltpu.SemaphoreType.DMA((3,)), pltpu.SemaphoreType.DMA((3,)),
                   pltpu.HBM(chip_sums.shape, chip_sums.dtype), pltpu.HBM(land.shape, land.dtype),
                   jax.ShapeDtypeStruct((8, LANES), F32)],
        in_specs=[HBM, HBM], out_specs=[SEM, SEM, HBM, HBM, pl.BlockSpec(memory_space=pltpu.VMEM)],
        input_output_aliases={0: 2, 1: 3},
        compiler_params=pltpu.CompilerParams(has_side_effects=EFFECT),
    )(_hbm(chip_sums), land)


def _rs_wait(send, recv, src_thru, land_thru, after, name):
    def body(src, land_ref, send_sem, recv_sem, after_ref, src_out, land_out):
        for j, to in enumerate(_rs_peers()):
            cp = pltpu.make_async_remote_copy(
                src_ref=src.at[0], dst_ref=land_ref.at[0], send_sem=send_sem.at[j], recv_sem=recv_sem.at[j],
                device_id=to, device_id_type=MESH)
            cp.wait_send()
            cp.wait_recv()

    return pl.pallas_call(
        body, name=name,
        out_shape=(pltpu.HBM(src_thru.shape, src_thru.dtype), pltpu.HBM(land_thru.shape, land_thru.dtype)),
        in_specs=(HBM, HBM, SEM, SEM, ANY), out_specs=(HBM, HBM), input_output_aliases={0: 0, 1: 1},
        compiler_params=pltpu.CompilerParams(has_side_effects=EFFECT),
    )(src_thru, land_thru, send, recv, after)


def _sib_start(full, name):
    land = _hbm(lax.empty((N_CHIP,) + full.shape[1:], full.dtype))

    def body(src, land_ref, send, recv, src_thru, land_thru, token):
        x, y, c = _position()
        for k in range(N_CHIP):
            pltpu.make_async_remote_copy(
                src_ref=src.at[2 * k + (1 - c)], dst_ref=land_ref.at[k], send_sem=send.at[k], recv_sem=recv.at[k],
                device_id=(x, y, 1 - c), device_id_type=MESH).start()
        token[...] = jnp.zeros_like(token)

    return pl.pallas_call(
        body, name=name,
        out_shape=[pltpu.SemaphoreType.DMA((N_CHIP,)), pltpu.SemaphoreType.DMA((N_CHIP,)),
                   pltpu.HBM(full.shape, full.dtype), pltpu.HBM(land.shape, land.dtype),
                   jax.ShapeDtypeStruct((8, LANES), F32)],
        in_specs=[HBM, HBM], out_specs=[SEM, SEM, HBM, HBM, pl.BlockSpec(memory_space=pltpu.VMEM)],
        input_output_aliases={0: 2, 1: 3},
        compiler_params=pltpu.CompilerParams(has_side_effects=EFFECT),
    )(_hbm(full), land)


def _sib_wait(send, recv, src_thru, land_thru, after, name):
    def body(src, land_ref, send_sem, recv_sem, after_ref, src_out, land_out):
        x, y, c = _position()
        for k in range(N_CHIP):
            cp = pltpu.make_async_remote_copy(
                src_ref=src.at[0], dst_ref=land_ref.at[0], send_sem=send_sem.at[k], recv_sem=recv_sem.at[k],
                device_id=(x, y, 1 - c), device_id_type=MESH)
            cp.wait_send()
            cp.wait_recv()

    return pl.pallas_call(
        body, name=name,
        out_shape=(pltpu.HBM(src_thru.shape, src_thru.dtype), pltpu.HBM(land_thru.shape, land_thru.dtype)),
        in_specs=(HBM, HBM, SEM, SEM, ANY), out_specs=(HBM, HBM), input_output_aliases={0: 0, 1: 1},
        compiler_params=pltpu.CompilerParams(has_side_effects=EFFECT),
    )(src_thru, land_thru, send, recv, after)


def _chip_sum(full, recv, core, name):
    _, rows, cols = full.shape
    tr = _blk(rows, 1024) if rows % LANES == 0 else rows

    def body(core_ref, a_ref, b_ref, o_ref):
        o_ref[...] = (a_ref[...].astype(F32) + b_ref[...].astype(F32)).astype(o_ref.dtype)

    grid_spec = pltpu.PrefetchScalarGridSpec(
        num_scalar_prefetch=1, grid=(N_CHIP, rows // tr),
        in_specs=[pl.BlockSpec((None, tr, cols), lambda k, i, core_ref: (2 * k + core_ref[0], i, 0)),
                  pl.BlockSpec((None, tr, cols), lambda k, i, core_ref: (k, i, 0))],
        out_specs=pl.BlockSpec((None, tr, cols), lambda k, i, core_ref: (k, i, 0)))
    return _pcall(body, name=name, out_shape=jax.ShapeDtypeStruct((N_CHIP, rows, cols), full.dtype),
                  grid_spec=grid_spec)(core, full, recv)


def _matmul(a, b, *, name, ta=False, tb=False, tm=1024, tn=1024, tk=2048, out_dtypes=(BF16,), epilogue=None,
            row_extras=(), tile_extras=(), out_shape=None, out_map=None, b_sharded=False, n_cols=None, n_off=0,
            after=None, b_pair=False):
    m, k = (a.shape[1], a.shape[0]) if ta else a.shape
    if b_sharded:
        shard_c = b.shape[2]
        n, kb = (b.shape[1], N_DEV * shard_c) if tb else (N_DEV * shard_c, b.shape[1])
        tn, tk = (tn, min(tk, shard_c)) if tb else (min(tn, shard_c), tk)
        if b_pair:
            assert tb and tk == shard_c
            tk = 2 * shard_c
    else:
        n, kb = b.shape if tb else (b.shape[1], b.shape[0])
    assert kb == k, (name, kb, k)
    if n_cols is not None:
        n = n_cols
    tm, tn, tk = _blk(m, tm), _blk(n, tn), _blk(k, tk)
    assert n_off % tn == 0
    nk = k // tk
    dims = (((0 if ta else 1,), (1 if tb else 0,)), ((), ()))
    behind = () if after is None else (after,)
    n_row, n_tile, n_out = len(row_extras), len(tile_extras), len(out_dtypes)
    n_b = 2 if b_pair else 1
    first_out = 1 + n_b + n_row + n_tile + len(behind)

    def body(*refs):
        a_ref, b_ref = refs[:2]
        extras = refs[1 + n_b:1 + n_b + n_row + n_tile]
        outs = refs[first_out:first_out + n_out]
        acc_ref = refs[-1]
        jj, kk = pl.program_id(1), pl.program_id(2)
        if b_pair:
            half = tk // 2
            part = (lax.dot_general(a_ref[:, 0:half].astype(BF16), b_ref[...].astype(BF16), dims,
                                    preferred_element_type=F32)
                    + lax.dot_general(a_ref[:, half:tk].astype(BF16), refs[2][...].astype(BF16), dims,
                                      preferred_element_type=F32))
        else:
            part = lax.dot_general(a_ref[...].astype(BF16), b_ref[...].astype(BF16), dims,
                                   preferred_element_type=F32)

        def finish(acc):
            res = (acc,) if epilogue is None else epilogue(acc, jj, *[e[...] for e in extras])
            for o_ref, r in zip(outs, res):
                o_ref[...] = r.astype(o_ref.dtype)

        if nk == 1:
            finish(part)
        else:
            @pl.when(kk == 0)
            def _():
                acc_ref[...] = part

            @pl.when(kk > 0)
            def _():
                acc_ref[...] += part

            @pl.when(kk == nk - 1)
            def _():
                finish(acc_ref[...])

    a_spec = pl.BlockSpec((tk, tm), lambda i, j, kk: (kk, i)) if ta else pl.BlockSpec((tm, tk), lambda i, j, kk: (i, kk))
    if b_pair:
        b_spec = pl.BlockSpec((None, tn, shard_c), lambda i, j, kk: (2 * kk, j, 0))
    elif b_sharded and tb:
        per = shard_c // tk
        b_spec = pl.BlockSpec((None, tn, tk), lambda i, j, kk: (kk // per, j, kk % per))
    elif b_sharded:
        per = shard_c // tn
        b_spec = pl.BlockSpec((None, tk, tn), lambda i, j, kk: (j // per, kk, j % per))
    elif tb:
        b_spec = pl.BlockSpec((tn, tk), lambda i, j, kk: (j + n_off // tn, kk))
    else:
        b_spec = pl.BlockSpec((tk, tn), lambda i, j, kk: (kk, j + n_off // tn))
    in_specs = [a_spec, b_spec]
    if b_pair:
        in_specs.append(pl.BlockSpec((None, tn, shard_c), lambda i, j, kk: (2 * kk + 1, j, 0)))
    in_specs += [pl.BlockSpec((tm, LANES), lambda i, j, kk: (i, 0)) for _ in row_extras]
    in_specs += [pl.BlockSpec((tm, tn), lambda i, j, kk: (i, j)) for _ in tile_extras]
    in_specs += [ANY for _ in behind]
    if out_map is None:
        out_specs = [pl.BlockSpec((tm, tn), lambda i, j, kk: (i, j)) for _ in out_dtypes]
        shapes = [jax.ShapeDtypeStruct((m, n), dt) for dt in out_dtypes]
    else:
        out_specs = [out_map(tm, tn)]
        shapes = [jax.ShapeDtypeStruct(out_shape, out_dtypes[0])]
    acc_shape = (tm, tn) if nk > 1 else (8, LANES)
    res = _pcall(body, name=name, out_shape=shapes, grid=(m // tm, n // tn, nk), in_specs=in_specs,
                 out_specs=out_specs, scratch_shapes=[pltpu.VMEM(acc_shape, F32)])(
                     a, *([b, b] if b_pair else [b]), *row_extras, *tile_extras, *behind)
    return res[0] if n_out == 1 else res


def _rope_cols(acc, j, cos, sin, n_rope):
    width = acc.shape[1]
    parts = []
    for g in range(width // HEAD_DIM):
        xg = acc[:, g * HEAD_DIM:(g + 1) * HEAD_DIM]
        roped = xg * cos + pltpu.roll(xg, HEAD_DIM // 2, 1) * sin
        parts.append(jnp.where(j * width + g * HEAD_DIM < n_rope, roped, xg))
    return jnp.concatenate(parts, axis=1) if len(parts) > 1 else parts[0]


def _silu(v):
    return v / (1.0 + jnp.exp(-v))


def _mod_part(c_all, w_mod, b_part):
    d, w = w_mod.shape
    tk = _blk(d, 512)

    def body(c_ref, w_ref, b_ref, o_ref):
        kk = pl.program_id(0)
        cond = _silu(c_ref[...]).astype(BF16)
        part = jnp.dot(cond, w_ref[...].astype(BF16), preferred_element_type=F32)

        @pl.when(kk == 0)
        def _():
            o_ref[...] = part + b_ref[...]

        @pl.when(kk > 0)
        def _():
            o_ref[...] += part

    return _pcall(body, name="mod_part", out_shape=jax.ShapeDtypeStruct((N_DEV, w), F32), grid=(d // tk,),
                  in_specs=[pl.BlockSpec((N_DEV, tk), lambda kk: (0, kk)), pl.BlockSpec((tk, w), lambda kk: (kk, 0)),
                            pl.BlockSpec((1, w), lambda kk: (0, 0))],
                  out_specs=pl.BlockSpec((N_DEV, w), lambda kk: (0, 0)))(c_all, w_mod, b_part)


def _row_call(body, name, t, d, tiled_in, vec_in, tiled_out_dtypes, n_vec_out, tr=256):
    tr = _blk(t, tr)
    tile = pl.BlockSpec((tr, d), lambda i: (i, 0))
    vec = pl.BlockSpec((1, d), lambda i: (0, 0))
    out_shape = [jax.ShapeDtypeStruct((t, d), dt) for dt in tiled_out_dtypes]
    out_shape += [jax.ShapeDtypeStruct((1, d), F32)] * n_vec_out
    return _pcall(body, name=name, out_shape=out_shape, grid=(t // tr,),
                  in_specs=[tile] * len(tiled_in) + [vec] * len(vec_in),
                  out_specs=[tile] * len(tiled_out_dtypes) + [vec] * n_vec_out)(*tiled_in, *vec_in)


def _accumulate(ref, val):
    @pl.when(pl.program_id(0) == 0)
    def _():
        ref[...] = val

    @pl.when(pl.program_id(0) > 0)
    def _():
        ref[...] += val


def _rsum(v):
    return jnp.sum(v, axis=0, keepdims=True)


def _rms(v):
    return lax.rsqrt(jnp.mean(v * v, axis=-1, keepdims=True) + NORM_EPS)


def _rms_bwd(vhat, r, dvhat):
    return r * (dvhat - vhat * jnp.mean(dvhat * vhat, axis=-1, keepdims=True))


def _pre_attn(x, g0, sc_a, sh_a):
    def body(x_ref, g_ref, sc_ref, sh_ref, h_ref):
        xv = x_ref[...]
        h_ref[...] = (xv * _rms(xv) * g_ref[...] * (1.0 + sc_ref[...]) + sh_ref[...]).astype(BF16)

    t, d = x.shape
    return _row_call(body, "pre_attn", t, d, [x], [g0, sc_a, sh_a], [BF16], 0)[0]


def _post_mix(x, mix, gt_a, g1, g2, sc_m, sh_m):
    def body(x_ref, mix_ref, gt_ref, g1_ref, g2_ref, sc_ref, sh_ref, x2_ref, h2_ref):
        mv = mix_ref[...]
        x2 = x_ref[...] + gt_ref[...] * (mv * _rms(mv) * g1_ref[...])
        x2_ref[...] = x2
        h2_ref[...] = (x2 * _rms(x2) * g2_ref[...] * (1.0 + sc_ref[...]) + sh_ref[...]).astype(BF16)

    t, d = x.shape
    return _row_call(body, "post_mix", t, d, [x, mix], [gt_a, g1, g2, sc_m, sh_m], [F32, BF16], 0)


def _final(y, x2, target, gt_m, g3):
    t, d = y.shape

    def body(y_ref, x2_ref, tg_ref, gt_ref, g3_ref, dy_ref, dout_ref, dgt_ref, dg3_ref, loss_ref):
        yv = y_ref[...]
        r = _rms(yv)
        yhat = yv * r
        n3 = yhat * g3_ref[...]
        err = x2_ref[...] + gt_ref[...] * n3 - tg_ref[...]
        _accumulate(loss_ref, jnp.zeros((1, d), F32) + 0.5 * jnp.sum(err * err) / d)
        dout = err * (1.0 / d)
        dout_ref[...] = dout
        dout_yhat = dout * yhat
        _accumulate(dgt_ref, _rsum(dout_yhat) * g3_ref[...])
        _accumulate(dg3_ref, _rsum(dout_yhat) * gt_ref[...])
        dy_ref[...] = _rms_bwd(yhat, r, dout * (gt_ref[...] * g3_ref[...])).astype(BF16)

    return _row_call(body, "final", t, d, [y, x2, target], [gt_m, g3], [BF16, F32], 3, tr=512)


def _mid_bwd(dh2, dout, x2, mix, g2, sc_m, gt_a, g1):
    t, d = x2.shape

    def body(dh2_ref, dout_ref, x2_ref, mix_ref, g2_ref, sc_ref, gt_ref, g1_ref,
             dmix_ref, dx2_ref, dsh_ref, dsc_ref, dg2_ref, dgt_ref, dg1_ref):
        g2, scale, g1, gt = g2_ref[...], 1.0 + sc_ref[...], g1_ref[...], gt_ref[...]
        dh2v = dh2_ref[...].astype(F32)
        x2v = x2_ref[...]
        r2 = _rms(x2v)
        x2hat = x2v * r2
        dh_xhat = dh2v * x2hat
        _accumulate(dsh_ref, _rsum(dh2v))
        _accumulate(dsc_ref, _rsum(dh_xhat) * g2)
        _accumulate(dg2_ref, _rsum(dh_xhat) * scale)
        dx2 = dout_ref[...] + _rms_bwd(x2hat, r2, dh2v * (scale * g2))
        dx2_ref[...] = dx2
        mv = mix_ref[...]
        r1 = _rms(mv)
        mhat = mv * r1
        dx_mhat = dx2 * mhat
        _accumulate(dgt_ref, _rsum(dx_mhat) * g1)
        _accumulate(dg1_ref, _rsum(dx_mhat) * gt)
        dmix_ref[...] = _rms_bwd(mhat, r1, dx2 * (gt * g1)).astype(BF16)

    return _row_call(body, "mid_bwd", t, d, [dh2, dout, x2, mix], [g2, sc_m, gt_a, g1], [BF16, F32], 5)


def _x_bwd(dh1, dx2, x, g0, sc_a):
    t, d = x.shape

    def body(dh1_ref, dx2_ref, x_ref, g0_ref, sc_ref, dx_ref, dsh_ref, dsc_ref, dg0_ref):
        g0, scale = g0_ref[...], 1.0 + sc_ref[...]
        dh1v = dh1_ref[...].astype(F32)
        xv = x_ref[...]
        r0 = _rms(xv)
        xhat = xv * r0
        dh_xhat = dh1v * xhat
        _accumulate(dsh_ref, _rsum(dh1v))
        _accumulate(dsc_ref, _rsum(dh_xhat) * g0)
        _accumulate(dg0_ref, _rsum(dh_xhat) * scale)
        dx_ref[...] = dx2_ref[...] + _rms_bwd(xhat, r0, dh1v * (scale * g0))

    return _row_call(body, "x_bwd", t, d, [dh1, dx2, x], [g0, sc_a], [F32], 3, tr=512)


def _pick_lane(block, h):
    lane = lax.broadcasted_iota(jnp.int32, block.shape, 1)
    return jnp.sum(jnp.where(lane == h, block, 0.0), axis=1, keepdims=True)


def _put_lane(ref, rows, h, col):
    old = ref[rows, :]
    lane = lax.broadcasted_iota(jnp.int32, old.shape, 1)
    ref[rows, :] = jnp.where(lane == h, col, old)


def _tri(n, lower):
    r = lax.broadcasted_iota(jnp.int32, (n, n), 0)
    c = lax.broadcasted_iota(jnp.int32, (n, n), 1)
    return jnp.where((c <= r) if lower else (c >= r), 1.0, 0.0).astype(F32)


def _cum_fwd(fg, b128):
    t = fg.shape[0]
    nb = t // LANES

    def body(fg_ref, b_ref, cum_ref, cumt_ref):
        tri = _tri(LANES, True)
        carry = jnp.zeros((1, LANES), F32)
        for i in range(nb):
            z = fg_ref[i * LANES:(i + 1) * LANES, :] + b_ref[...]
            lf = jnp.minimum(z, 0.0) - jnp.log(1.0 + jnp.exp(-jnp.abs(z)))
            blk = jnp.dot(tri, lf, precision=lax.Precision.HIGHEST, preferred_element_type=F32) + carry
            cum_ref[i * LANES:(i + 1) * LANES, :] = blk
            carry = blk[LANES - 1:LANES, :]
        cumt_ref[...] = cum_ref[...].T[0:N_FOX, :]

    return _pcall(body, name="cum_fwd",
                  out_shape=[jax.ShapeDtypeStruct((t, LANES), F32), jax.ShapeDtypeStruct((N_FOX, t), F32)],
                  grid=(1,),
                  in_specs=[pl.BlockSpec((t, LANES), lambda i: (0, 0)), pl.BlockSpec((1, LANES), lambda i: (0, 0))],
                  out_specs=[pl.BlockSpec((t, LANES), lambda i: (0, 0)), pl.BlockSpec((N_FOX, t), lambda i: (0, 0))],
                  )(fg, b128)


def _fg_bwd(dcs_rows, fg, b128):
    t = fg.shape[0]
    nb = t // LANES

    def body(dcs_ref, fg_ref, b_ref, dfg_ref, db_ref, dcum_ref):
        dcum_ref[...] = -jnp.concatenate([dcs_ref[...], jnp.zeros((LANES - N_FOX, t), F32)], axis=0).T
        tri = _tri(LANES, False)
        carry = jnp.zeros((1, LANES), F32)
        db = jnp.zeros((1, LANES), F32)
        for i in reversed(range(nb)):
            rows = slice(i * LANES, (i + 1) * LANES)
            dlf = jnp.dot(tri, dcum_ref[rows, :], precision=lax.Precision.HIGHEST, preferred_element_type=F32) + carry
            carry = dlf[0:1, :]
            z = fg_ref[rows, :] + b_ref[...]
            dfg = dlf / (1.0 + jnp.exp(z))
            dfg_ref[rows, :] = dfg.astype(BF16)
            db = db + _rsum(dfg)
        db_ref[...] = db

    full = pl.BlockSpec((t, LANES), lambda i: (0, 0))
    vec = pl.BlockSpec((1, LANES), lambda i: (0, 0))
    return _pcall(body, name="fg_bwd",
                  out_shape=[jax.ShapeDtypeStruct((t, LANES), BF16), jax.ShapeDtypeStruct((1, LANES), F32)],
                  grid=(1,), in_specs=[pl.BlockSpec((N_FOX, t), lambda i: (0, 0)), full, vec], out_specs=[full, vec],
                  scratch_shapes=[pltpu.VMEM((t, LANES), F32)])(dcs_rows, fg, b128)


def _head_spec(t, col0, div=1):
    return pl.BlockSpec((t, HEAD_DIM), lambda h: (0, col0 + h // div))


def _fox_scores(q, k, cq, ck, i, tq, end):
    s = lax.dot_general(q, k, NT_DIMS, preferred_element_type=F32) * ATT_SCALE + cq - ck
    row = lax.broadcasted_iota(jnp.int32, (tq, end), 0) + i * tq
    col = lax.broadcasted_iota(jnp.int32, (tq, end), 1)
    return jnp.where(row >= col, s, -jnp.inf)


def _fox_fwd(proj_a, cum, cumt):
    t = proj_a.shape[0]
    tq = _blk(t, 512)
    nq = t // tq

    def body(q_ref, k_ref, v_ref, cum_ref, cumt_ref, o_ref, lse_ref):
        h = pl.program_id(0)
        cq_all = _pick_lane(cum_ref[...], h)
        ck_all = cumt_ref[pl.ds(h, 1), :]

        @pl.when(h == 0)
        def _():
            lse_ref[...] = jnp.zeros_like(lse_ref)

        for i in range(nq):
            rows, end = slice(i * tq, (i + 1) * tq), (i + 1) * tq
            s = _fox_scores(q_ref[rows, :], k_ref[0:end, :], cq_all[rows, :], ck_all[:, 0:end], i, tq, end)
            m = jnp.max(s, axis=1, keepdims=True)
            p = jnp.exp(s - m)
            l = jnp.sum(p, axis=1, keepdims=True)
            o = jnp.dot(p.astype(BF16), v_ref[0:end, :], preferred_element_type=F32) / l
            o_ref[rows, :] = o.astype(BF16)
            _put_lane(lse_ref, rows, h, m + jnp.log(l))

    nh = FOX_W // HEAD_DIM
    stat = pl.BlockSpec((t, LANES), lambda h: (0, 0))
    return _pcall(body, name="fox_fwd",
                  out_shape=[jax.ShapeDtypeStruct((t, FOX_W), BF16), jax.ShapeDtypeStruct((t, LANES), F32)],
                  grid=(N_FOX,),
                  in_specs=[_head_spec(t, 0), _head_spec(t, nh), _head_spec(t, 2 * nh), stat,
                            pl.BlockSpec((N_FOX, t), lambda h: (0, 0))],
                  out_specs=[_head_spec(t, 0), stat])(proj_a, proj_a, proj_a, cum, cumt)


def _fox_bwd(proj_a, d_attn, cum, cumt, lse):
    t = proj_a.shape[0]
    tq = _blk(t, 512)
    nq = t // tq

    def body(q_ref, k_ref, v_ref, do_ref, cum_ref, cumt_ref, lse_ref,
             dq_ref, dk_ref, dv_ref, dcs_ref, dk_acc, dv_acc, dcs_acc):
        h = pl.program_id(0)
        cq_all = _pick_lane(cum_ref[...], h)
        ck_all = cumt_ref[pl.ds(h, 1), :]
        lse_all = _pick_lane(lse_ref[...], h)
        dk_acc[...] = jnp.zeros_like(dk_acc)
        dv_acc[...] = jnp.zeros_like(dv_acc)
        dcs_acc[...] = jnp.zeros_like(dcs_acc)
        for i in range(nq):
            rows, end = slice(i * tq, (i + 1) * tq), (i + 1) * tq
            q, k, v, do = q_ref[rows, :], k_ref[0:end, :], v_ref[0:end, :], do_ref[rows, :]
            s = _fox_scores(q, k, cq_all[rows, :], ck_all[:, 0:end], i, tq, end)
            p = jnp.exp(s - lse_all[rows, :])
            dp = lax.dot_general(do, v, NT_DIMS, preferred_element_type=F32)
            ds = p * (dp - jnp.sum(p * dp, axis=1, keepdims=True))
            dcs_acc[:, 0:end] += jnp.sum(ds, axis=0, keepdims=True)
            ds = ds.astype(BF16)
            dq_ref[rows, :] = (jnp.dot(ds, k, preferred_element_type=F32) * ATT_SCALE).astype(BF16)
            dk_acc[0:end, :] += lax.dot_general(ds, q, TN_DIMS, preferred_element_type=F32)
            dv_acc[0:end, :] += lax.dot_general(p.astype(BF16), do, TN_DIMS, preferred_element_type=F32)
        dk_ref[...] = (dk_acc[...] * ATT_SCALE).astype(BF16)
        dv_ref[...] = dv_acc[...].astype(BF16)
        dcs_ref[pl.ds(h, 1), :] = dcs_acc[...]

    nh = FOX_W // HEAD_DIM
    stat = pl.BlockSpec((t, LANES), lambda h: (0, 0))
    rows8 = pl.BlockSpec((N_FOX, t), lambda h: (0, 0))
    head = _head_spec(t, 0)
    wide = jax.ShapeDtypeStruct((t, FOX_W), BF16)
    return _pcall(body, name="fox_bwd",
                  out_shape=[wide, wide, wide, jax.ShapeDtypeStruct((N_FOX, t), F32)],
                  grid=(N_FOX,),
                  in_specs=[_head_spec(t, 0), _head_spec(t, nh), _head_spec(t, 2 * nh), head, stat, rows8, stat],
                  out_specs=[head, head, head, rows8],
                  scratch_shapes=[pltpu.VMEM((t, HEAD_DIM), F32), pltpu.VMEM((t, HEAD_DIM), F32),
                                  pltpu.VMEM((1, t), F32)],
                  )(proj_a, proj_a, proj_a, d_attn, cum, cumt, lse)


def _swa_scores(q, k, i, tq, start, end):
    s = lax.dot_general(q, k, NT_DIMS, preferred_element_type=F32) * ATT_SCALE
    row = lax.broadcasted_iota(jnp.int32, (tq, end - start), 0) + i * tq
    col = lax.broadcasted_iota(jnp.int32, (tq, end - start), 1) + start
    diff = row - col
    return jnp.where((diff >= 0) & (diff < WINDOW), s, -jnp.inf)


def _swa_blocks(t):
    tq = _blk(t, 256)
    return tq, [(i, max(0, i * tq - WINDOW), (i + 1) * tq) for i in range(t // tq)]


def _swa_fwd(proj_b, sinks128):
    t = proj_b.shape[0]
    tq, blocks = _swa_blocks(t)

    def body(q_ref, k_ref, v_ref, sink_ref, o_ref, lse_ref):
        h = pl.program_id(0)
        sink = _pick_lane(sink_ref[...], h)

        @pl.when(h == 0)
        def _():
            lse_ref[...] = jnp.zeros_like(lse_ref)

        for i, start, end in blocks:
            rows = slice(i * tq, end)
            s = _swa_scores(q_ref[rows, :], k_ref[start:end, :], i, tq, start, end)
            m = jnp.maximum(jnp.max(s, axis=1, keepdims=True), sink)
            p = jnp.exp(s - m)
            l = jnp.sum(p, axis=1, keepdims=True) + jnp.exp(sink - m)
            o = jnp.dot(p.astype(BF16), v_ref[start:end, :], preferred_element_type=F32) / l
            o_ref[rows, :] = o.astype(BF16)
            _put_lane(lse_ref, rows, h, m + jnp.log(l))

    stat = pl.BlockSpec((t, LANES), lambda h: (0, 0))
    return _pcall(body, name="swa_fwd",
                  out_shape=[jax.ShapeDtypeStruct((t, SWA_W), BF16), jax.ShapeDtypeStruct((t, LANES), F32)],
                  grid=(N_SWA,),
                  in_specs=[_head_spec(t, 0), _head_spec(t, N_SWA, GQA), _head_spec(t, N_SWA + N_KV, GQA),
                            pl.BlockSpec((1, LANES), lambda h: (0, 0))],
                  out_specs=[_head_spec(t, 0), stat])(proj_b, proj_b, proj_b, sinks128)


def _rope_bwd(d, cos, sin):
    return d * cos + pltpu.roll(d * sin, HEAD_DIM // 2, 1)


def _swa_bwd(proj_b, d_attn, lse, sinks128, cos, sin):
    t = proj_b.shape[0]
    tq, blocks = _swa_blocks(t)

    def body(q_ref, k_ref, v_ref, do_ref, lse_ref, sink_ref, cos_ref, sin_ref,
             dq_ref, dk_ref, dv_ref, dsink_ref, dk_acc, dv_acc):
        h = pl.program_id(0)
        sink = _pick_lane(sink_ref[...], h)
        lse_all = _pick_lane(lse_ref[...], h)

        @pl.when(h == 0)
        def _():
            dsink_ref[...] = jnp.zeros_like(dsink_ref)

        @pl.when(h % GQA == 0)
        def _():
            dk_acc[...] = jnp.zeros_like(dk_acc)
            dv_acc[...] = jnp.zeros_like(dv_acc)

        dsink = jnp.zeros((1, 1), F32)
        for i, start, end in blocks:
            rows = slice(i * tq, end)
            q, k, v, do = q_ref[rows, :], k_ref[start:end, :], v_ref[start:end, :], do_ref[rows, :]
            s = _swa_scores(q, k, i, tq, start, end)
            p = jnp.exp(s - lse_all[rows, :])
            dp = lax.dot_general(do, v, NT_DIMS, preferred_element_type=F32)
            delta = jnp.sum(p * dp, axis=1, keepdims=True)
            ds = (p * (dp - delta)).astype(BF16)
            dq = jnp.dot(ds, k, preferred_element_type=F32) * ATT_SCALE
            dq_ref[rows, :] = _rope_bwd(dq, cos_ref[rows, :], sin_ref[rows, :]).astype(BF16)
            dk_acc[start:end, :] += lax.dot_general(ds, q, TN_DIMS, preferred_element_type=F32)
            dv_acc[start:end, :] += lax.dot_general(p.astype(BF16), do, TN_DIMS, preferred_element_type=F32)
            dsink = dsink - jnp.sum(jnp.exp(sink - lse_all[rows, :]) * delta, axis=0, keepdims=True)
        old = dsink_ref[...]
        lane = lax.broadcasted_iota(jnp.int32, old.shape, 1)
        dsink_ref[...] = jnp.where(lane == h, dsink, old)

        @pl.when(h % GQA == GQA - 1)
        def _():
            dk_ref[...] = _rope_bwd(dk_acc[...] * ATT_SCALE, cos_ref[...], sin_ref[...]).astype(BF16)
            dv_ref[...] = dv_acc[...].astype(BF16)

    stat = pl.BlockSpec((t, LANES), lambda h: (0, 0))
    vec = pl.BlockSpec((1, LANES), lambda h: (0, 0))
    head = _head_spec(t, 0)
    kv_out = _head_spec(t, 0, GQA)
    return _pcall(body, name="swa_bwd",
                  out_shape=[jax.ShapeDtypeStruct((t, SWA_W), BF16), jax.ShapeDtypeStruct((t, KV_W), BF16),
                             jax.ShapeDtypeStruct((t, KV_W), BF16), jax.ShapeDtypeStruct((1, LANES), F32)],
                  grid=(N_SWA,),
                  in_specs=[head, _head_spec(t, N_SWA, GQA), _head_spec(t, N_SWA + N_KV, GQA),
                            _head_spec(t, N_FOX), stat, vec, stat, stat],
                  out_specs=[head, kv_out, kv_out, vec],
                  scratch_shapes=[pltpu.VMEM((t, HEAD_DIM), F32), pltpu.VMEM((t, HEAD_DIM), F32)],
                  )(proj_b, proj_b, proj_b, d_attn, lse, sinks128, cos, sin)


def _adamw(w, g, m, v):
    m = ADAM_B1 * m + (1.0 - ADAM_B1) * g
    v = ADAM_B2 * v + (1.0 - ADAM_B2) * (g * g)
    m_hat = m / (1.0 - ADAM_B1 ** ADAM_STEP)
    v_hat = v / (1.0 - ADAM_B2 ** ADAM_STEP)
    delta = -ADAM_LR * (m_hat / (jnp.sqrt(v_hat) + ADAM_EPS) + ADAM_WD * w)
    return delta, m, v


def _adam_pieces(w, m, v, own, land, idx, name):
    rows, cols = w.shape
    tr, tc = (256, cols) if rows % 256 == 0 else (rows, _blk(cols, 512))

    def body(idx_ref, own_ref, l1_ref, l2_ref, l3_ref, w_ref, m_ref, v_ref, g_ref, d_ref, mo_ref, vo_ref):
        g = own_ref[...].astype(F32) + l1_ref[...].astype(F32) + l2_ref[...].astype(F32) + l3_ref[...].astype(F32)
        g_ref[...] = g
        d_ref[...], mo_ref[...], vo_ref[...] = _adamw(w_ref[...], g, m_ref[...], v_ref[...])

    def piece(p):
        return pl.BlockSpec((None, tr, tc), lambda i, j, idx_ref: (idx_ref[p], i, j))

    tile = pl.BlockSpec((tr, tc), lambda i, j, idx_ref: (i, j))
    out = jax.ShapeDtypeStruct((rows, cols), F32)
    grid_spec = pltpu.PrefetchScalarGridSpec(
        num_scalar_prefetch=1, grid=(rows // tr, cols // tc),
        in_specs=[piece(0), piece(1), piece(2), piece(3), tile, tile, tile], out_specs=[tile] * 4)
    return _pcall(body, name=name, out_shape=[out] * 4, grid_spec=grid_spec)(idx, own, land, land, land, w, m, v)


def _adam_mod(c_all, dmod_cols, w, m, v):
    rows, cols = w.shape
    tr = _blk(rows, 256)

    def body(c_ref, dm_ref, w_ref, m_ref, v_ref, g_ref, d_ref, mo_ref, vo_ref):
        cond = _silu(c_ref[...]).astype(BF16)
        g = lax.dot_general(cond, dm_ref[...].astype(BF16), TN_DIMS, preferred_element_type=F32)
        g_ref[...] = g
        d_ref[...], mo_ref[...], vo_ref[...] = _adamw(w_ref[...], g, m_ref[...], v_ref[...])

    tile = pl.BlockSpec((tr, cols), lambda i: (i, 0))
    out = jax.ShapeDtypeStruct((rows, cols), F32)
    return _pcall(body, name="adam_mod", out_shape=[out] * 4, grid=(rows // tr,),
                  in_specs=[pl.BlockSpec((N_DEV, tr), lambda i: (0, i)), pl.BlockSpec((N_DEV, cols), lambda i: (0, 0)),
                            tile, tile, tile],
                  out_specs=[tile] * 4)(c_all, dmod_cols, w, m, v)


def _adam_small(parts, w, m, v):
    nv = w.shape[1]

    def body(p_ref, w_ref, m_ref, v_ref, g_ref, d_ref, mo_ref, vo_ref):
        g = p_ref[0:1, :]
        for k in range(1, N_DEV):
            g = g + p_ref[k:k + 1, :]
        g_ref[...] = g
        d_ref[...], mo_ref[...], vo_ref[...] = _adamw(w_ref[...], g, m_ref[...], v_ref[...])

    vec = pl.BlockSpec((1, nv), lambda i: (0, 0))
    out = jax.ShapeDtypeStruct((1, nv), F32)
    return _pcall(body, name="adam_small", out_shape=[out] * 4, grid=(1,),
                  in_specs=[pl.BlockSpec((N_DEV, nv), lambda i: (0, 0)), vec, vec, vec],
                  out_specs=[vec] * 4)(parts, w, m, v)


def _pad_lanes(v, width=LANES):
    return jnp.pad(v, ((0, 0), (0, width - v.shape[1])))


def kernel(x, c, w_mod, b_mod, g_pre_mix, g_post_mix, w_in, b_forget, swa_sinks, w_out, g_pre_mlp, g_post_mlp, w_up, w_down, loss_target, m_w_mod, m_b_mod, m_g_pre_mix, m_g_post_mix, m_w_in, m_b_forget, m_swa_sinks, m_w_out, m_g_pre_mlp, m_g_post_mlp, m_w_up, m_w_down, v_w_mod, v_b_mod, v_g_pre_mix, v_g_post_mix, v_w_in, v_b_forget, v_swa_sinks, v_w_out, v_g_pre_mlp, v_g_post_mlp, v_w_up, v_w_down):
    ax, ay, ac = _position()
    me = 4 * ax + 2 * ay + ac
    x, target = x[0], loss_target[0]
    t, d = x.shape
    w_mod, w_in, w_out, w_up, w_down = w_mod[0], w_in[0], w_out[0], w_up[0], w_down[0]
    mod_w = w_mod.shape[1]
    in_w = w_in.shape[1]
    in_total = N_DEV * in_w
    shard_ff = w_up.shape[1]
    n_fox3 = 3 * FOX_W
    n_swa3 = SWA_W + 2 * KV_W
    assert in_total == n_fox3 + N_FOX + n_swa3 and d == FOX_W + SWA_W

    c_all = _all_gather([c], "gather_c")[0].reshape(N_DEV, d)
    b_part = lax.dynamic_slice(b_mod, (0, me * mod_w), (1, mod_w))
    mod_parts = _all_gather([_mod_part(c_all, w_mod, b_part)], "gather_mod")[0]
    mod = lax.dynamic_index_in_dim(mod_parts, me, axis=1, keepdims=False).reshape(1, N_DEV * mod_w)

    w_in_b, mod = lax.optimization_barrier((w_in.astype(BF16), mod))
    first = _ag_start([w_in_b], "ag_start_in")
    behind_first = first[4][0, 0]
    rest = _ag_start([(w + behind_first).astype(BF16) for w in (w_out, w_up, w_down)], "ag_start_rest")
    ag_send, ag_recv, ag_shard, ag_land = [a + b for a, b in zip(first[:4], rest[:4])]
    ag_token = rest[4]

    def gathered(i, after, name):
        shard, land = _ag_wait(ag_send[i], ag_recv[i], ag_shard[i], ag_land[i], after, "ag_wait_" + name)
        return lax.dynamic_update_slice(_ag_forward(land, "ag_fwd_" + name), shard[None], (me, 0, 0))

    def gathered_start(i, after, name):
        shard, land = _ag_wait(ag_send[i], ag_recv[i], ag_shard[i], ag_land[i], after, "ag_wait_" + name)
        return shard, _fwd_start(land, "fwd_start_" + name)

    def gathered_finish(started, after, name):
        shard, (send, recv, land, _) = started
        return lax.dynamic_update_slice(_fwd_wait(send, recv, land, after, "fwd_wait_" + name), shard[None], (me, 0, 0))

    sh_a, sc_a, gt_a, sh_m, sc_m, gt_m = [mod[:, i * d:(i + 1) * d] for i in range(6)]

    half = HEAD_DIM // 2
    inv_freq = 1.0 / (ROPE_THETA ** (jnp.arange(half, dtype=F32) * (2.0 / HEAD_DIM)))
    ang = jnp.arange(t).astype(F32)[:, None] * inv_freq[None, :]
    cos = jnp.concatenate([jnp.cos(ang), jnp.cos(ang)], axis=1)
    sin = jnp.concatenate([-jnp.sin(ang), jnp.sin(ang)], axis=1)

    b128 = _pad_lanes(b_forget)
    sinks128 = _pad_lanes(swa_sinks)

    h1 = _pre_attn(x, g_pre_mix + ag_token[0:1, 0:1], sc_a, sh_a)
    h1, adam_in = lax.optimization_barrier(
        (h1, (jnp.transpose(w_in), jnp.transpose(m_w_in[0]), jnp.transpose(v_w_in[0]))))
    w_in_g = gathered(0, h1, "in")
    o_fg, o_sq = n_fox3, n_fox3 + N_FOX

    def cols(lo, hi):
        parts = []
        for j in range(lo // in_w, (hi - 1) // in_w + 1):
            parts.append(w_in_g[j, :, max(lo - j * in_w, 0):min(hi - j * in_w, in_w)])
        return parts

    w_in_r = jnp.concatenate(cols(0, o_fg) + cols(o_sq, in_total) + cols(o_fg, o_sq)
                             + [jnp.zeros((d, FG_PAD - N_FOX), BF16)], axis=1)
    proj_a = _matmul(h1, w_in_r, name="proj_a", n_cols=n_fox3, n_off=0)
    proj_b = _matmul(h1, w_in_r, name="proj_b", n_cols=n_swa3, n_off=n_fox3, tn=512, row_extras=(cos, sin),
                     epilogue=lambda acc, j, cs, sn: (_rope_cols(acc, j, cs, sn, SWA_W + KV_W),))
    out_started = gathered_start(1, proj_a, "out")
    fg = _matmul(h1, w_in_r, name="proj_fg", n_cols=FG_PAD, n_off=n_fox3 + n_swa3, tn=FG_PAD,
                 out_dtypes=(F32,), after=out_started[1][3])[:, 0:LANES]
    cum, cumt = _cum_fwd(fg, b128)
    fox_o, fox_lse = _fox_fwd(proj_a, cum, cumt)
    up_started = gathered_start(2, fox_o, "up")
    swa_o, swa_lse = _swa_fwd(proj_b, sinks128 + up_started[1][3][0:1, 0:1])
    w_out_full = gathered_finish(out_started, swa_o, "out").reshape(d, d)
    attn = jnp.concatenate([fox_o, swa_o], axis=1)
    mix = _matmul(attn, w_out_full, name="out_proj", out_dtypes=(F32,))
    x2, h2 = _post_mix(x, mix, gt_a, g_post_mix, g_pre_mlp, sc_m, sh_m)
    w_up_g = gathered_finish(up_started, h2, "up")
    u, act = _matmul(h2, w_up_g, name="mlp_up", b_sharded=True, out_dtypes=(BF16, BF16),
                     epilogue=lambda acc, j: (acc, jnp.square(jnp.maximum(acc, 0.0))))
    w_down_full = gathered(3, act, "down").reshape(N_DEV * shard_ff, d)
    y = _matmul(act, w_down_full, name="mlp_down", tk=4096, out_dtypes=(F32,))

    core = jnp.reshape(ac, (1,)).astype(jnp.int32)

    def reduce_start(started, after, name):
        send, recv, src, land, _ = started
        full, from_sibling = _sib_wait(send, recv, src, land, after, "sib_wait_" + name)
        return _rs_start(_chip_sum(full, from_sibling, core, "chip_sum_" + name), "rs_start_" + name)

    def tok(started):
        return started[4][0:1, 0:1]

    idx = jnp.stack([2 * ax + ay, 2 * (1 - ax) + ay, 2 * ax + (1 - ay), 2 * (1 - ax) + (1 - ay)]).astype(jnp.int32)

    def reduce_finish(started, after, w, m, v, name):
        send, recv, src, land, _ = started
        own, landed = _rs_wait(send, recv, src, land, after, "rs_wait_" + name)
        return _adam_pieces(w, m[0], v[0], own, landed, idx, "adam_" + name)

    dy, dout, dgt_m, dg3, loss_vec = _final(y, x2, target, gt_m, g_post_mlp)
    du = _matmul(dy, w_down_full, name="d_act", tb=True, tile_extras=(u,),
                 epilogue=lambda acc, j, uu: (acc * (2.0 * jnp.maximum(uu.astype(F32), 0.0)),))
    dw_down = _matmul(act, dy, name="dw_down", ta=True)
    sb_down = _sib_start(dw_down.reshape(N_DEV, shard_ff, d), "sib_start_down")
    dh2 = _matmul(du, w_up_g, name="d_h2", tb=True, b_sharded=True, b_pair=True, after=sb_down[4])
    rs_down = reduce_start(sb_down, dh2, "down")
    per = shard_ff // _blk(shard_ff, 1024)
    dw_up = _matmul(h2, du, name="dw_up", ta=True, tn=_blk(shard_ff, 1024), out_shape=(N_DEV, d, shard_ff),
                    out_map=lambda tm, tn: pl.BlockSpec((None, tm, tn), lambda i, j, kk: (j // per, i, j % per)),
                    after=rs_down[4])
    sb_up = _sib_start(dw_up, "sib_start_up")
    dmix, dx2, dsh_m, dsc_m, dg2, dgt_a, dg1 = _mid_bwd(
        dh2, dout, x2, mix, g_pre_mlp + tok(sb_up), sc_m, gt_a, g_post_mix)
    d_attn = _matmul(dmix, w_out_full, name="d_attn", tb=True)
    rs_up = reduce_start(sb_up, d_attn, "up")
    dw_out = _matmul(attn, dmix, name="dw_out", ta=True, after=rs_up[4])
    sb_out = _sib_start(dw_out.reshape(N_DEV, d // N_DEV, d), "sib_start_out")
    dqf, dkf, dvf, dcs = _fox_bwd(proj_a, d_attn, cum, cumt, fox_lse)
    dsq, dsk, dsv, dsinks = _swa_bwd(proj_b, d_attn, swa_lse, sinks128 + tok(sb_out), cos, sin)
    rs_out = reduce_start(sb_out, dsq, "out")
    dfg, db_forget = _fg_bwd(dcs, fg, b128 + tok(rs_out))
    dproj = jnp.concatenate([dqf, dkf, dvf, dsq, dsk, dsv, _pad_lanes(dfg, FG_PAD)], axis=1)
    dw_in_r = _matmul(dproj, h1, name="dw_in", ta=True)

    def shard_rows(j):
        lo, hi = j * in_w, (j + 1) * in_w
        parts = []
        for seg_lo, seg_hi, shift in ((0, o_fg, 0), (o_fg, o_sq, n_swa3), (o_sq, in_total, -N_FOX)):
            a, b = max(lo, seg_lo), min(hi, seg_hi)
            if a < b:
                parts.append(dw_in_r[a + shift:b + shift, :])
        return parts[0] if len(parts) == 1 else jnp.concatenate(parts, axis=0)

    sb_in = _sib_start(jnp.stack([shard_rows(j) for j in range(N_DEV)]), "sib_start_in")
    dh1 = _matmul(dproj, w_in_r, name="d_h1", tb=True, tk=2560, after=sb_in[4])
    grad_x, dsh_a, dsc_a, dg0 = _x_bwd(dh1, dx2, x, g_pre_mix, sc_a)

    small = jnp.concatenate([dsh_a, dsc_a, dgt_a, dsh_m, dsc_m, dgt_m, dg0, dg1, dg2, dg3, db_forget, dsinks,
                             loss_vec[:, 0:LANES]], axis=1)
    small_all = _all_gather([small], "gather_small")[0].reshape(N_DEV, small.shape[1])
    rs_in = reduce_start(sb_in, small_all, "in")

    pack = lambda bm, g0_, g1_, g2_, g3_, bf_, sk_: jnp.concatenate(
        [bm, g0_, g1_, g2_, g3_, _pad_lanes(bf_), _pad_lanes(sk_), jnp.zeros((1, LANES), F32)], axis=1)
    p_small = pack(b_mod, g_pre_mix, g_post_mix, g_pre_mlp, g_post_mlp, b_forget, swa_sinks)
    m_small = pack(m_b_mod, m_g_pre_mix, m_g_post_mix, m_g_pre_mlp, m_g_post_mlp, m_b_forget, m_swa_sinks)
    v_small = pack(v_b_mod, v_g_pre_mix, v_g_post_mix, v_g_pre_mlp, v_g_post_mlp, v_b_forget, v_swa_sinks)
    small_out = _adam_small(small_all, p_small + tok(rs_in), m_small, v_small)

    n_mod = 6 * d

    def unpack(vec):
        o = n_mod
        return (vec[:, 0:n_mod], vec[:, o:o + d], vec[:, o + d:o + 2 * d], vec[:, o + 2 * d:o + 3 * d],
                vec[:, o + 3 * d:o + 4 * d], vec[:, o + 4 * d:o + 4 * d + N_FOX],
                vec[:, o + 4 * d + LANES:o + 4 * d + LANES + N_SWA])

    loss = small_out[0][0, n_mod + 4 * d + 2 * LANES]
    g_small, d_small, nm_small, nv_small = [unpack(vec) for vec in small_out]

    dmod_cols = lax.dynamic_slice(small_all, (0, me * mod_w), (N_DEV, mod_w))
    g_w_mod, d_w_mod, nm_w_mod, nv_w_mod = _adam_mod(c_all + tok(rs_in), dmod_cols, w_mod, m_w_mod[0], v_w_mod[0])

    g_w_down, d_w_down, nm_w_down, nv_w_down = reduce_finish(rs_down, d_w_mod, w_down, m_w_down, v_w_down, "w_down")
    g_w_up, d_w_up, nm_w_up, nv_w_up = reduce_finish(rs_up, d_w_down, w_up, m_w_up, v_w_up, "w_up")
    g_w_out, d_w_out, nm_w_out, nv_w_out = reduce_finish(rs_out, d_w_up, w_out, m_w_out, v_w_out, "w_out")
    g_w_in, d_w_in, nm_w_in, nv_w_in = reduce_finish(rs_in, d_w_out, adam_in[0], (adam_in[1],), (adam_in[2],), "w_in")

    def assemble(w_mod_, small_, w_in_, w_out_, w_up_, w_down_):
        b_mod_, g0_, g1_, g2_, g3_, bf_, sk_ = small_
        return [w_mod_[None], b_mod_, g0_, g1_, jnp.transpose(w_in_)[None], bf_, sk_, w_out_[None], g2_, g3_,
                w_up_[None], w_down_[None]]

    outs = [loss, grad_x[None]]
    outs += assemble(g_w_mod, g_small, g_w_in, g_w_out, g_w_up, g_w_down)
    outs += assemble(d_w_mod, d_small, d_w_in, d_w_out, d_w_up, d_w_down)
    outs += assemble(nm_w_mod, nm_small, nm_w_in, nm_w_out, nm_w_up, nm_w_down)
    outs += assemble(nv_w_mod, nv_small, nv_w_in, nv_w_out, nv_w_up, nv_w_down)
    return tuple(outs)
```
